```python
import jax, jax.numpy as jnp
from jax import lax
import numpy as np

D_MODEL = 1024
BATCH = 8
SEQ = 4096
DEPTH = 1

EXPAND = 2
D_MIX = EXPAND * D_MODEL
HEAD_DIM = 64
D_SSD = D_MIX // 2
D_SB = D_MIX - D_SSD
N_SSD_HEADS = D_SSD // HEAD_DIM
N_SB_HEADS = D_SB // HEAD_DIM
SSD_GROUPS = 2
SSD_STATE = 128
CONV_WIDTH = 4
D_CONV = D_SSD + 2 * SSD_GROUPS * SSD_STATE
SSD_CHUNK = 128
SB_BLOCK = 128
D_FF = -(-8 * D_MODEL // (3 * 256)) * 256
N_MOD = 6
EPS = 1e-6
IN_SPLITS = [int(s) for s in np.cumsum([D_SSD, D_CONV, N_SSD_HEADS, D_SB, D_SB])]
D_IN_PROJ = IN_SPLITS[-1] + D_SB

kernel_name = "hymba_ssd_stickbreaking_adaln_block"


def rms_norm(x, w):
    xf = x.astype(jnp.float32)
    y = xf * lax.rsqrt(jnp.mean(xf * xf, axis=-1, keepdims=True) + EPS)
    return (y * w.astype(jnp.float32)).astype(x.dtype)


def causal_depthwise_conv(u, w, b):
    out = lax.conv_general_dilated(
        u, w[:, None, :], window_strides=(1,), padding=[(CONV_WIDTH - 1, 0)],
        dimension_numbers=("NWC", "WIO", "NWC"), feature_group_count=u.shape[-1])
    return out + b


def ssd_chunked(x, dt, a, b_mat, c_mat):
    bsz, seq, n_heads, p = x.shape
    g, n = b_mat.shape[2], b_mat.shape[3]
    e = n_heads // g
    nc, t = seq // SSD_CHUNK, SSD_CHUNK
    xg = (x * dt[..., None]).reshape(bsz, nc, t, g, e, p)
    a_dt = jnp.moveaxis((a * dt).reshape(bsz, nc, t, g, e), 2, -1)
    a_cs = jnp.cumsum(a_dt, axis=-1)
    bm = b_mat.reshape(bsz, nc, t, g, n)
    cm = c_mat.reshape(bsz, nc, t, g, n)
    idx = jnp.arange(t)
    causal = idx[:, None] >= idx[None, :]
    seg = a_cs[..., :, None] - a_cs[..., None, :]
    decay = jnp.exp(jnp.where(causal, seg, -jnp.inf))
    scores = jnp.einsum("bclgn,bcsgn->bcgls", cm, bm)
    y_diag = jnp.einsum("bcgls,bcgels,bcsgep->bclgep", scores, decay, xg)
    decay_states = jnp.exp(a_cs[..., -1:] - a_cs)
    states = jnp.einsum("bclgn,bcgel,bclgep->bcgepn", bm, decay_states, xg)
    chunk_decay = jnp.exp(a_cs[..., -1])

    def step(carry, inp):
        st, dec = inp
        return carry * dec[..., None, None] + st, carry

    init = jnp.zeros_like(states[:, 0])
    _, prev_states = lax.scan(step, init, (jnp.moveaxis(states, 1, 0), jnp.moveaxis(chunk_decay, 1, 0)))
    prev_states = jnp.moveaxis(prev_states, 0, 1)
    y_off = jnp.einsum("bclgn,bcgepn,bcgel->bclgep", cm, prev_states, jnp.exp(a_cs))
    return (y_diag + y_off).reshape(bsz, seq, n_heads, p)


def ssd_mixer(z, xbc, dt_raw, conv_w, conv_b, dt_bias, a_log, d_skip, norm_w):
    bsz, seq, _ = z.shape
    f32 = jnp.float32
    xbc = jax.nn.silu(causal_depthwise_conv(xbc, conv_w, conv_b)).astype(f32)
    xs = xbc[..., :D_SSD]
    bm = xbc[..., D_SSD:D_SSD + SSD_GROUPS * SSD_STATE].reshape(bsz, seq, SSD_GROUPS, SSD_STATE)
    cm = xbc[..., D_SSD + SSD_GROUPS * SSD_STATE:].reshape(bsz, seq, SSD_GROUPS, SSD_STATE)
    xh = xs.reshape(bsz, seq, N_SSD_HEADS, HEAD_DIM)
    dt = jax.nn.softplus(dt_raw.astype(f32) + dt_bias.astype(f32))
    a = -jnp.exp(a_log.astype(f32))
    y = ssd_chunked(xh, dt, a, bm, cm) + d_skip.astype(f32)[:, None] * xh
    y = y.reshape(bsz, seq, D_SSD) * jax.nn.silu(z.astype(f32))
    yg = y.reshape(bsz, seq, SSD_GROUPS, D_SSD // SSD_GROUPS)
    yg = yg * lax.rsqrt(jnp.mean(yg * yg, axis=-1, keepdims=True) + EPS)
    return (yg.reshape(bsz, seq, D_SSD) * norm_w.astype(f32)).astype(z.dtype)


def stick_breaking_mixer(q, k, v, q_norm_w, k_norm_w):
    bsz, seq, _ = q.shape
    f32 = jnp.float32
    q = rms_norm(q.reshape(bsz, seq, N_SB_HEADS, HEAD_DIM), q_norm_w).astype(f32).transpose(0, 2, 1, 3)
    k = rms_norm(k.reshape(bsz, seq, N_SB_HEADS, HEAD_DIM), k_norm_w).astype(f32).transpose(0, 2, 1, 3)
    vh = v.reshape(bsz, seq, N_SB_HEADS, HEAD_DIM).astype(f32).transpose(0, 2, 1, 3)
    scale = HEAD_DIM ** -0.5
    outs = []
    for blk in range(seq // SB_BLOCK):
        start, end = blk * SB_BLOCK, (blk + 1) * SB_BLOCK
        qb, kp, vp = q[:, :, start:end], k[:, :, :end], vh[:, :, :end]
        logits = jnp.einsum("bhqd,bhkd->bhqk", qb, kp) * scale
        t_pos = start + jnp.arange(SB_BLOCK)
        s_pos = jnp.arange(end)
        strict = s_pos[None, :] < t_pos[:, None]
        log_rem = jnp.where(strict, jax.nn.log_sigmoid(-logits), 0.0)
        log_between = lax.cumsum(log_rem, axis=3, reverse=True) - log_rem
        weights = jnp.where(strict, jnp.exp(jax.nn.log_sigmoid(logits) + log_between), 0.0)
        outs.append(jnp.einsum("bhqk,bhkd->bhqd", weights, vp))
    o = jnp.concatenate(outs, axis=2)
    return o.transpose(0, 2, 1, 3).reshape(bsz, seq, D_SB).astype(v.dtype)


def _fwd_setup_inputs(seed: int = 0) -> dict:
    key = jax.random.key(seed)
    ks = jax.random.split(key, 24)
    f32 = jnp.float32
    nrm = lambda k, shape, s: jax.random.normal(k, shape, f32) * s
    dt0 = jnp.exp(jax.random.uniform(ks[8], (DEPTH, N_SSD_HEADS), f32,
                                     float(np.log(1e-3)), float(np.log(1e-1))))
    dt_bias = dt0 + jnp.log(-jnp.expm1(-dt0))
    return {
        "x": nrm(ks[0], (BATCH, SEQ, D_MODEL), 1.0),
        "c": nrm(ks[1], (BATCH, D_MODEL), 1.0),
        "w_ada": nrm(ks[2], (DEPTH, D_MODEL, N_MOD * D_MODEL), 0.5 * D_MODEL ** -0.5),
        "b_ada": nrm(ks[3], (DEPTH, N_MOD * D_MODEL), 0.01),
        "norm1_w": 1.0 + nrm(ks[4], (DEPTH, D_MODEL), 0.02),
        "w_in": nrm(ks[5], (DEPTH, D_MODEL, D_IN_PROJ), D_MODEL ** -0.5),
        "conv_w": nrm(ks[6], (DEPTH, CONV_WIDTH, D_CONV), CONV_WIDTH ** -0.5),
        "conv_b": nrm(ks[7], (DEPTH, D_CONV), 0.01),
        "dt_bias": dt_bias,
        "a_log": jnp.log(jax.random.uniform(ks[9], (DEPTH, N_SSD_HEADS), f32, 1.0, 16.0)),
        "d_skip": 1.0 + nrm(ks[10], (DEPTH, N_SSD_HEADS), 0.02),
        "ssd_norm_w": 1.0 + nrm(ks[11], (DEPTH, D_SSD), 0.02),
        "q_norm_w": 1.0 + nrm(ks[12], (DEPTH, HEAD_DIM), 0.02),
        "k_norm_w": 1.0 + nrm(ks[13], (DEPTH, HEAD_DIM), 0.02),
        "w_out": nrm(ks[14], (DEPTH, D_MIX, D_MODEL), D_MIX ** -0.5),
        "norm2_w": 1.0 + nrm(ks[15], (DEPTH, D_MODEL), 0.02),
        "w_gate": nrm(ks[16], (DEPTH, D_MODEL, D_FF), D_MODEL ** -0.5),
        "w_up": nrm(ks[17], (DEPTH, D_MODEL, D_FF), D_MODEL ** -0.5),
        "w_down": nrm(ks[18], (DEPTH, D_FF, D_MODEL), D_FF ** -0.5),
    }


def _fwd_reference(x, c, w_ada, b_ada, norm1_w, w_in, conv_w, conv_b, dt_bias, a_log, d_skip,
              ssd_norm_w, q_norm_w, k_norm_w, w_out, norm2_w, w_gate, w_up, w_down):
    cond = jax.nn.silu(c)
    for layer in range(DEPTH):
        mod = (cond @ w_ada[layer] + b_ada[layer])[:, None, :]
        sh1, sc1, g1, sh2, sc2, g2 = jnp.split(mod, N_MOD, axis=-1)
        h = rms_norm(x, norm1_w[layer]) * (1.0 + sc1) + sh1
        proj = h @ w_in[layer]
        z, xbc, dt_raw, q, k, v = jnp.split(proj, IN_SPLITS, axis=-1)
        y_ssd = ssd_mixer(z, xbc, dt_raw, conv_w[layer], conv_b[layer], dt_bias[layer],
                          a_log[layer], d_skip[layer], ssd_norm_w[layer])
        y_sb = stick_breaking_mixer(q, k, v, q_norm_w[layer], k_norm_w[layer])
        mix = jnp.concatenate([y_ssd, y_sb], axis=-1) @ w_out[layer]
        x = x + g1 * mix
        h = rms_norm(x, norm2_w[layer]) * (1.0 + sc2) + sh2
        ffn = (jax.nn.silu(h @ w_gate[layer]) * (h @ w_up[layer])) @ w_down[layer]
        x = x + g2 * ffn
    return x


import jax as _jax
import jax.numpy as _jnp

TWIN_FORMAT = 'train_step'
FWD_PARAMS = ['x', 'c', 'w_ada', 'b_ada', 'norm1_w', 'w_in', 'conv_w', 'conv_b', 'dt_bias', 'a_log', 'd_skip', 'ssd_norm_w', 'q_norm_w', 'k_norm_w', 'w_out', 'norm2_w', 'w_gate', 'w_up', 'w_down']
TWIN_WEIGHTS = ['w_ada', 'b_ada', 'norm1_w', 'w_in', 'conv_w', 'conv_b', 'dt_bias', 'a_log', 'd_skip', 'ssd_norm_w', 'q_norm_w', 'k_norm_w', 'w_out', 'norm2_w', 'w_gate', 'w_up', 'w_down']
TWIN_DIFF_INPUT = 'x'
TWIN_INPUTS = ['x', 'c', 'w_ada', 'b_ada', 'norm1_w', 'w_in', 'conv_w', 'conv_b', 'dt_bias', 'a_log', 'd_skip', 'ssd_norm_w', 'q_norm_w', 'k_norm_w', 'w_out', 'norm2_w', 'w_gate', 'w_up', 'w_down', 'loss_target', 'm_w_ada', 'm_b_ada', 'm_norm1_w', 'm_w_in', 'm_conv_w', 'm_conv_b', 'm_dt_bias', 'm_a_log', 'm_d_skip', 'm_ssd_norm_w', 'm_q_norm_w', 'm_k_norm_w', 'm_w_out', 'm_norm2_w', 'm_w_gate', 'm_w_up', 'm_w_down', 'v_w_ada', 'v_b_ada', 'v_norm1_w', 'v_w_in', 'v_conv_w', 'v_conv_b', 'v_dt_bias', 'v_a_log', 'v_d_skip', 'v_ssd_norm_w', 'v_q_norm_w', 'v_k_norm_w', 'v_w_out', 'v_norm2_w', 'v_w_gate', 'v_w_up', 'v_w_down']
TWIN_OUTPUTS = ['loss', 'grad_x', 'grad_w_ada', 'grad_b_ada', 'grad_norm1_w', 'grad_w_in', 'grad_conv_w', 'grad_conv_b', 'grad_dt_bias', 'grad_a_log', 'grad_d_skip', 'grad_ssd_norm_w', 'grad_q_norm_w', 'grad_k_norm_w', 'grad_w_out', 'grad_norm2_w', 'grad_w_gate', 'grad_w_up', 'grad_w_down', 'delta_w_ada', 'delta_b_ada', 'delta_norm1_w', 'delta_w_in', 'delta_conv_w', 'delta_conv_b', 'delta_dt_bias', 'delta_a_log', 'delta_d_skip', 'delta_ssd_norm_w', 'delta_q_norm_w', 'delta_k_norm_w', 'delta_w_out', 'delta_norm2_w', 'delta_w_gate', 'delta_w_up', 'delta_w_down', 'new_m_w_ada', 'new_m_b_ada', 'new_m_norm1_w', 'new_m_w_in', 'new_m_conv_w', 'new_m_conv_b', 'new_m_dt_bias', 'new_m_a_log', 'new_m_d_skip', 'new_m_ssd_norm_w', 'new_m_q_norm_w', 'new_m_k_norm_w', 'new_m_w_out', 'new_m_norm2_w', 'new_m_w_gate', 'new_m_w_up', 'new_m_w_down', 'new_v_w_ada', 'new_v_b_ada', 'new_v_norm1_w', 'new_v_w_in', 'new_v_conv_w', 'new_v_conv_b', 'new_v_dt_bias', 'new_v_a_log', 'new_v_d_skip', 'new_v_ssd_norm_w', 'new_v_q_norm_w', 'new_v_k_norm_w', 'new_v_w_out', 'new_v_norm2_w', 'new_v_w_gate', 'new_v_w_up', 'new_v_w_down']
TWIN_LEAF_KINDS = {'loss': 'loss', 'grad_x': 'grad_x', 'grad_w_ada': 'grad_w', 'grad_b_ada': 'grad_w', 'grad_norm1_w': 'grad_w', 'grad_w_in': 'grad_w', 'grad_conv_w': 'grad_w', 'grad_conv_b': 'grad_w', 'grad_dt_bias': 'grad_w', 'grad_a_log': 'grad_w', 'grad_d_skip': 'grad_w', 'grad_ssd_norm_w': 'grad_w', 'grad_q_norm_w': 'grad_w', 'grad_k_norm_w': 'grad_w', 'grad_w_out': 'grad_w', 'grad_norm2_w': 'grad_w', 'grad_w_gate': 'grad_w', 'grad_w_up': 'grad_w', 'grad_w_down': 'grad_w', 'delta_w_ada': 'delta_w', 'delta_b_ada': 'delta_w', 'delta_norm1_w': 'delta_w', 'delta_w_in': 'delta_w', 'delta_conv_w': 'delta_w', 'delta_conv_b': 'delta_w', 'delta_dt_bias': 'delta_w', 'delta_a_log': 'delta_w', 'delta_d_skip': 'delta_w', 'delta_ssd_norm_w': 'delta_w', 'delta_q_norm_w': 'delta_w', 'delta_k_norm_w': 'delta_w', 'delta_w_out': 'delta_w', 'delta_norm2_w': 'delta_w', 'delta_w_gate': 'delta_w', 'delta_w_up': 'delta_w', 'delta_w_down': 'delta_w', 'new_m_w_ada': 'new_m', 'new_m_b_ada': 'new_m', 'new_m_norm1_w': 'new_m', 'new_m_w_in': 'new_m', 'new_m_conv_w': 'new_m', 'new_m_conv_b': 'new_m', 'new_m_dt_bias': 'new_m', 'new_m_a_log': 'new_m', 'new_m_d_skip': 'new_m', 'new_m_ssd_norm_w': 'new_m', 'new_m_q_norm_w': 'new_m', 'new_m_k_norm_w': 'new_m', 'new_m_w_out': 'new_m', 'new_m_norm2_w': 'new_m', 'new_m_w_gate': 'new_m', 'new_m_w_up': 'new_m', 'new_m_w_down': 'new_m', 'new_v_w_ada': 'new_v', 'new_v_b_ada': 'new_v', 'new_v_norm1_w': 'new_v', 'new_v_w_in': 'new_v', 'new_v_conv_w': 'new_v', 'new_v_conv_b': 'new_v', 'new_v_dt_bias': 'new_v', 'new_v_a_log': 'new_v', 'new_v_d_skip': 'new_v', 'new_v_ssd_norm_w': 'new_v', 'new_v_q_norm_w': 'new_v', 'new_v_k_norm_w': 'new_v', 'new_v_w_out': 'new_v', 'new_v_norm2_w': 'new_v', 'new_v_w_gate': 'new_v', 'new_v_w_up': 'new_v', 'new_v_w_down': 'new_v'}


def _forward(args):
    return _fwd_reference(*[args[k] for k in FWD_PARAMS])


def _output_shape():
    def fwd():
        inp = _fwd_setup_inputs(0)
        return _fwd_reference(*[inp[k] for k in FWD_PARAMS])
    out = _jax.eval_shape(fwd)
    return out.shape, out.dtype

N_MICROBATCH = 1
ADAM_LR = 0.001
ADAM_B1 = 0.9
ADAM_B2 = 0.999
ADAM_EPS = 1e-08
ADAM_WD = 0.01
ADAM_STEP = 10
PER_EXAMPLE_BATCH_AXIS = {'x': 0, 'c': 0, 'loss_target': 0}
SHARED_INPUTS = []
_WEIGHT_DTYPES = {'w_ada': _jnp.float32, 'b_ada': _jnp.float32, 'norm1_w': _jnp.float32, 'w_in': _jnp.float32, 'conv_w': _jnp.float32, 'conv_b': _jnp.float32, 'dt_bias': _jnp.float32, 'a_log': _jnp.float32, 'd_skip': _jnp.float32, 'ssd_norm_w': _jnp.float32, 'q_norm_w': _jnp.float32, 'k_norm_w': _jnp.float32, 'w_out': _jnp.float32, 'norm2_w': _jnp.float32, 'w_gate': _jnp.float32, 'w_up': _jnp.float32, 'w_down': _jnp.float32}
MOMENT_SCALE = {'w_ada': 9.171420e-01, 'b_ada': 2.229055e+00, 'norm1_w': 6.767054e-01, 'w_in': 7.131393e-02, 'conv_w': 9.825284e-02, 'conv_b': 2.149283e-01, 'dt_bias': 3.798163e-01, 'a_log': 7.544827e-01, 'd_skip': 1.130159e+00, 'ssd_norm_w': 2.232961e+00, 'q_norm_w': 1.569147e+00, 'k_norm_w': 1.561451e+00, 'w_out': 1.839714e-01, 'norm2_w': 3.198713e+00, 'w_gate': 6.615283e-02, 'w_up': 4.791315e-02, 'w_down': 7.485734e-02}


def _to_microbatches(a, axis):
    t = _jnp.moveaxis(a, axis, 0)
    t = t.reshape((N_MICROBATCH, t.shape[0] // N_MICROBATCH) + t.shape[1:])
    return _jnp.moveaxis(t, 1, axis + 1)


def setup_inputs(seed: int = 0) -> dict:
    inp = _fwd_setup_inputs(seed)
    key = _jax.random.fold_in(_jax.random.key(seed), 7919)
    shape, _ = _output_shape()
    out = dict(inp)
    out["loss_target"] = _jax.random.normal(_jax.random.fold_in(key, 0), shape, _jnp.float32)
    for i, name in enumerate(TWIN_WEIGHTS):
        w = inp[name].astype(_jnp.float32)
        if MOMENT_SCALE is None:
            s = _jnp.sqrt(_jnp.mean(_jnp.square(w)) + 1e-30)
        else:
            s = MOMENT_SCALE[name]
        km, kv = _jax.random.split(_jax.random.fold_in(key, i + 1))
        out[name] = w
        out["m_" + name] = s * _jax.random.normal(km, w.shape, _jnp.float32)
        out["v_" + name] = (s * s) * _jax.random.uniform(kv, w.shape, _jnp.float32, 0.5, 1.5)
    if N_MICROBATCH > 1:
        for name, axis in PER_EXAMPLE_BATCH_AXIS.items():
            out[name] = _to_microbatches(out[name], axis)
    return {'x': out['x'], 'c': out['c'], 'w_ada': out['w_ada'], 'b_ada': out['b_ada'], 'norm1_w': out['norm1_w'], 'w_in': out['w_in'], 'conv_w': out['conv_w'], 'conv_b': out['conv_b'], 'dt_bias': out['dt_bias'], 'a_log': out['a_log'], 'd_skip': out['d_skip'], 'ssd_norm_w': out['ssd_norm_w'], 'q_norm_w': out['q_norm_w'], 'k_norm_w': out['k_norm_w'], 'w_out': out['w_out'], 'norm2_w': out['norm2_w'], 'w_gate': out['w_gate'], 'w_up': out['w_up'], 'w_down': out['w_down'], 'loss_target': out['loss_target'], 'm_w_ada': out['m_w_ada'], 'm_b_ada': out['m_b_ada'], 'm_norm1_w': out['m_norm1_w'], 'm_w_in': out['m_w_in'], 'm_conv_w': out['m_conv_w'], 'm_conv_b': out['m_conv_b'], 'm_dt_bias': out['m_dt_bias'], 'm_a_log': out['m_a_log'], 'm_d_skip': out['m_d_skip'], 'm_ssd_norm_w': out['m_ssd_norm_w'], 'm_q_norm_w': out['m_q_norm_w'], 'm_k_norm_w': out['m_k_norm_w'], 'm_w_out': out['m_w_out'], 'm_norm2_w': out['m_norm2_w'], 'm_w_gate': out['m_w_gate'], 'm_w_up': out['m_w_up'], 'm_w_down': out['m_w_down'], 'v_w_ada': out['v_w_ada'], 'v_b_ada': out['v_b_ada'], 'v_norm1_w': out['v_norm1_w'], 'v_w_in': out['v_w_in'], 'v_conv_w': out['v_conv_w'], 'v_conv_b': out['v_conv_b'], 'v_dt_bias': out['v_dt_bias'], 'v_a_log': out['v_a_log'], 'v_d_skip': out['v_d_skip'], 'v_ssd_norm_w': out['v_ssd_norm_w'], 'v_q_norm_w': out['v_q_norm_w'], 'v_k_norm_w': out['v_k_norm_w'], 'v_w_out': out['v_w_out'], 'v_norm2_w': out['v_norm2_w'], 'v_w_gate': out['v_w_gate'], 'v_w_up': out['v_w_up'], 'v_w_down': out['v_w_down']}


def _loss(weights, diff, rest, loss_target):
    with _jax.named_scope("forward"):
        args = {**rest, TWIN_DIFF_INPUT: diff, **{k: w.astype(_WEIGHT_DTYPES[k]) for k, w in weights.items()}}
        y = _forward(args)
    with _jax.named_scope("loss_head"):
        err = _jnp.square(y.astype(_jnp.float32) - loss_target)
        return 0.5 * _jnp.sum(_jnp.mean(err, axis=-1)) if err.ndim else 0.5 * err


def _adamw(w, g, m, v):
    m = ADAM_B1 * m + (1.0 - ADAM_B1) * g
    v = ADAM_B2 * v + (1.0 - ADAM_B2) * _jnp.square(g)
    m_hat = m / (1.0 - ADAM_B1 ** ADAM_STEP)
    v_hat = v / (1.0 - ADAM_B2 ** ADAM_STEP)
    delta = -ADAM_LR * (m_hat / (_jnp.sqrt(v_hat) + ADAM_EPS) + ADAM_WD * w)
    return delta, m, v


def reference(x, c, w_ada, b_ada, norm1_w, w_in, conv_w, conv_b, dt_bias, a_log, d_skip, ssd_norm_w, q_norm_w, k_norm_w, w_out, norm2_w, w_gate, w_up, w_down, loss_target, m_w_ada, m_b_ada, m_norm1_w, m_w_in, m_conv_w, m_conv_b, m_dt_bias, m_a_log, m_d_skip, m_ssd_norm_w, m_q_norm_w, m_k_norm_w, m_w_out, m_norm2_w, m_w_gate, m_w_up, m_w_down, v_w_ada, v_b_ada, v_norm1_w, v_w_in, v_conv_w, v_conv_b, v_dt_bias, v_a_log, v_d_skip, v_ssd_norm_w, v_q_norm_w, v_k_norm_w, v_w_out, v_norm2_w, v_w_gate, v_w_up, v_w_down):
    given = dict(x=x, c=c, w_ada=w_ada, b_ada=b_ada, norm1_w=norm1_w, w_in=w_in, conv_w=conv_w, conv_b=conv_b, dt_bias=dt_bias, a_log=a_log, d_skip=d_skip, ssd_norm_w=ssd_norm_w, q_norm_w=q_norm_w, k_norm_w=k_norm_w, w_out=w_out, norm2_w=norm2_w, w_gate=w_gate, w_up=w_up, w_down=w_down, loss_target=loss_target, m_w_ada=m_w_ada, m_b_ada=m_b_ada, m_norm1_w=m_norm1_w, m_w_in=m_w_in, m_conv_w=m_conv_w, m_conv_b=m_conv_b, m_dt_bias=m_dt_bias, m_a_log=m_a_log, m_d_skip=m_d_skip, m_ssd_norm_w=m_ssd_norm_w, m_q_norm_w=m_q_norm_w, m_k_norm_w=m_k_norm_w, m_w_out=m_w_out, m_norm2_w=m_norm2_w, m_w_gate=m_w_gate, m_w_up=m_w_up, m_w_down=m_w_down, v_w_ada=v_w_ada, v_b_ada=v_b_ada, v_norm1_w=v_norm1_w, v_w_in=v_w_in, v_conv_w=v_conv_w, v_conv_b=v_conv_b, v_dt_bias=v_dt_bias, v_a_log=v_a_log, v_d_skip=v_d_skip, v_ssd_norm_w=v_ssd_norm_w, v_q_norm_w=v_q_norm_w, v_k_norm_w=v_k_norm_w, v_w_out=v_w_out, v_norm2_w=v_norm2_w, v_w_gate=v_w_gate, v_w_up=v_w_up, v_w_down=v_w_down)
    weights = {n: given[n] for n in TWIN_WEIGHTS}
    shared = {n: given[n] for n in SHARED_INPUTS}
    per_example = {n: given[n] for n in ['x', 'c']}
    grad_fn = _jax.value_and_grad(_loss, argnums=(0, 1))

    def one_microbatch(ex, loss_target):
        ex = dict(ex)
        diff = ex.pop(TWIN_DIFF_INPUT)
        return grad_fn(weights, diff, {**shared, **ex}, loss_target)

    if N_MICROBATCH == 1:
        loss, (grad_w, grad_x) = one_microbatch(per_example, given["loss_target"])
    else:
        def body(carry, xs):
            loss_sum, grad_sum = carry
            l_k, (gw_k, gx_k) = one_microbatch(xs[0], xs[1])
            with _jax.named_scope("update"):
                return (loss_sum + l_k, _jax.tree.map(_jnp.add, grad_sum, gw_k)), gx_k

        init = (_jnp.zeros((), _jnp.float32), _jax.tree.map(_jnp.zeros_like, weights))
        (loss, grad_w), grad_x = _jax.lax.scan(body, init, (per_example, given["loss_target"]))
    with _jax.named_scope("update"):
        delta_w, new_m, new_v = {}, {}, {}
        for n in TWIN_WEIGHTS:
            delta_w[n], new_m[n], new_v[n] = _adamw(weights[n], grad_w[n], given["m_" + n], given["v_" + n])
    return (loss, grad_x, *[grad_w[n] for n in TWIN_WEIGHTS], *[delta_w[n] for n in TWIN_WEIGHTS],
            *[new_m[n] for n in TWIN_WEIGHTS], *[new_v[n] for n in TWIN_WEIGHTS])
```

```python
import functools

import jax
import jax.numpy as jnp
from jax import lax
from jax.experimental import pallas as pl
from jax.experimental.pallas import tpu as pltpu

F32 = jnp.float32
BF16 = jnp.bfloat16
MXU_DTYPE = jnp.bfloat16
HIGHEST = lax.Precision.HIGHEST
MESH_IDS = pl.DeviceIdType.MESH

N_DEV = 8
D_MODEL = 1024
HEAD_DIM = 64
N_HEADS = 16
D_SSD = 1024
D_SB = 1024
SSD_GROUPS = 2
SSD_STATE = 128
GROUP_WIDTH = D_SSD // SSD_GROUPS
D_CONV = D_SSD + 2 * SSD_GROUPS * SSD_STATE
CONV_WIDTH = 4
CHUNK = 128
D_FF = 2816
N_MOD = 6
EPS = 1e-6
D_IN_PROJ = 5648
DT_PAD = 128
D_IN_PAD = D_SSD + D_CONV + DT_PAD + 3 * D_SB
HALO = 8

ADAM_LR = 0.001
ADAM_B1 = 0.9
ADAM_B2 = 0.999
ADAM_EPS = 1e-08
ADAM_WD = 0.01
ADAM_STEP = 10

ROWS_W_IN = 706
ROWS_W_OUT = 256
ROWS_FF = 352
PACK_ROWS = 2048
SM_B_ADA = 0
SM_NORM1 = 6144
SM_CONV_B = 7168
SM_CONV_W = 8704
SM_DT_BIAS = 14848
SM_A_LOG = 14864
SM_D_SKIP = 14880
SM_SSD_NORM = 14976
SM_Q_NORM = 16000
SM_K_NORM = 16064
SM_NORM2 = 16128
SM_TOTAL = 17152


def _pick(n, cap, mult):
    if n <= cap:
        return n
    best = None
    for t in range(mult, cap + 1, mult):
        if n % t == 0:
            best = t
    assert best is not None, (n, cap, mult)
    return best


def _dot(a, b, precision=None):
    return jnp.dot(a, b, preferred_element_type=F32, precision=precision)


def _dot_nt(a, b, precision=None):
    return lax.dot_general(a, b, (((1,), (1,)), ((), ())), preferred_element_type=F32, precision=precision)


def _dot_tn(a, b, precision=None):
    return lax.dot_general(a, b, (((0,), (0,)), ((), ())), preferred_element_type=F32, precision=precision)


def _softplus(v):
    return jnp.maximum(v, 0.0) + jnp.log1p(jnp.exp(-jnp.abs(v)))


def _sigmoid(v):
    return jax.nn.sigmoid(v)


def _colsum(v):
    return jnp.sum(v, axis=0, keepdims=True)


def _split_dot(v, tri2):
    hi = v.astype(BF16)
    lo = (v - hi.astype(F32)).astype(BF16)
    return _dot(jnp.concatenate([hi, lo], axis=1), tri2)


def _position():
    x, y, c = lax.axis_index("x"), lax.axis_index("y"), lax.axis_index("c")
    return x, y, c


def _peer(x, y, c, k):
    px = 1 - x if (k >> 2) & 1 else x
    py = 1 - y if (k >> 1) & 1 else y
    pc = 1 - c if k & 1 else c
    return px, py, pc


def _all_gather_small(v, name):
    n = v.shape[1]

    def body(v_ref, out_ref, send_sems, recv_sems, local_sem):
        x, y, c = _position()
        me = 4 * x + 2 * y + c
        mine = pltpu.make_async_copy(v_ref, out_ref.at[me], local_sem)
        mine.start()
        sends = []
        for k in range(1, N_DEV):
            cp = pltpu.make_async_remote_copy(
                src_ref=v_ref, dst_ref=out_ref.at[me], send_sem=send_sems.at[k - 1], recv_sem=recv_sems.at[k - 1],
                device_id=_peer(x, y, c, k), device_id_type=MESH_IDS)
            cp.start()
            sends.append(cp)
        for k in range(1, N_DEV):
            px, py, pc = _peer(x, y, c, k)
            pltpu.make_async_remote_copy(
                src_ref=v_ref, dst_ref=out_ref.at[4 * px + 2 * py + pc], send_sem=send_sems.at[k - 1],
                recv_sem=recv_sems.at[k - 1], device_id=(px, py, pc), device_id_type=MESH_IDS).wait_recv()
        for cp in sends:
            cp.wait_send()
        mine.wait()

    return pl.pallas_call(
        body, name=name,
        out_shape=jax.ShapeDtypeStruct((N_DEV, 1, n), v.dtype),
        in_specs=[pl.BlockSpec(memory_space=pltpu.VMEM)],
        out_specs=pl.BlockSpec(memory_space=pltpu.VMEM),
        scratch_shapes=[pltpu.SemaphoreType.DMA((N_DEV - 1,)), pltpu.SemaphoreType.DMA((N_DEV - 1,)),
                        pltpu.SemaphoreType.DMA],
    )(v)


def _all_gather_big(block, name):
    r, n = block.shape

    def body(b_ref, out_ref, send_sems, recv_sems, local_sem):
        x, y, c = _position()
        me, sibling = (x, y, c), (x, y, 1 - c)
        chips = [(1 - x, y), (x, 1 - y), (1 - x, 1 - y)]

        def slot(px, py, pc):
            return out_ref.at[4 * px + 2 * py + pc]

        def copy(k, blk, to, src=None):
            return pltpu.make_async_remote_copy(
                src_ref=slot(*blk) if src is None else src, dst_ref=slot(*blk),
                send_sem=send_sems.at[k], recv_sem=recv_sems.at[k], device_id=to, device_id_type=MESH_IDS)

        mine = pltpu.make_async_copy(b_ref, slot(*me), local_sem)
        mine.start()
        first = [copy(0, me, sibling, src=b_ref)]
        first += [copy(1 + j, me, (*chip, c), src=b_ref) for j, chip in enumerate(chips)]
        for cp in first:
            cp.start()
        passed = [copy(4 + j, (*chip, c), sibling) for j, chip in enumerate(chips)]
        for j, chip in enumerate(chips):
            copy(1 + j, (*chip, c), me).wait_recv()
            passed[j].start()
        copy(0, sibling, me).wait_recv()
        for j, chip in enumerate(chips):
            copy(4 + j, (*chip, 1 - c), me).wait_recv()
        for cp in first + passed:
            cp.wait_send()
        mine.wait()

    return pl.pallas_call(
        body, name=name,
        out_shape=jax.ShapeDtypeStruct((N_DEV, r, n), block.dtype),
        in_specs=[pl.BlockSpec(memory_space=pl.ANY)],
        out_specs=pl.BlockSpec(memory_space=pl.ANY),
        scratch_shapes=[pltpu.SemaphoreType.DMA((N_DEV - 1,)), pltpu.SemaphoreType.DMA((N_DEV - 1,)),
                        pltpu.SemaphoreType.DMA],
    )(block)


def _all_to_all_big(blocks, name):
    _, r, n = blocks.shape

    def body(b_ref, out_ref, send_sems, recv_sems, local_sem):
        x, y, c = _position()
        me = 4 * x + 2 * y + c
        mine = pltpu.make_async_copy(b_ref.at[me], out_ref.at[me], local_sem)
        mine.start()
        sends = []
        for k in range(1, N_DEV):
            px, py, pc = _peer(x, y, c, k)
            cp = pltpu.make_async_remote_copy(
                src_ref=b_ref.at[4 * px + 2 * py + pc], dst_ref=out_ref.at[me], send_sem=send_sems.at[k - 1],
                recv_sem=recv_sems.at[k - 1], device_id=(px, py, pc), device_id_type=MESH_IDS)
            cp.start()
            sends.append(cp)
        for k in range(1, N_DEV):
            px, py, pc = _peer(x, y, c, k)
            pltpu.make_async_remote_copy(
                src_ref=b_ref.at[me], dst_ref=out_ref.at[4 * px + 2 * py + pc], send_sem=send_sems.at[k - 1],
                recv_sem=recv_sems.at[k - 1], device_id=(px, py, pc), device_id_type=MESH_IDS).wait_recv()
        for cp in sends:
            cp.wait_send()
        mine.wait()

    return pl.pallas_call(
        body, name=name,
        out_shape=jax.ShapeDtypeStruct(blocks.shape, blocks.dtype),
        in_specs=[pl.BlockSpec(memory_space=pl.ANY)],
        out_specs=pl.BlockSpec(memory_space=pl.ANY),
        scratch_shapes=[pltpu.SemaphoreType.DMA((N_DEV - 1,)), pltpu.SemaphoreType.DMA((N_DEV - 1,)),
                        pltpu.SemaphoreType.DMA],
    )(blocks)


def _sum_slots(slots, name):
    _, r, n = slots.shape
    tr = _pick(r, 256, 16)

    def body(s_ref, o_ref):
        acc = s_ref[0].astype(F32)
        for d in range(1, N_DEV):
            acc = acc + s_ref[d].astype(F32)
        o_ref[...] = acc

    return pl.pallas_call(
        body, name=name, grid=(r // tr,),
        in_specs=[pl.BlockSpec((N_DEV, tr, n), lambda i: (0, i, 0))],
        out_specs=pl.BlockSpec((tr, n), lambda i: (i, 0)),
        out_shape=jax.ShapeDtypeStruct((r, n), F32),
    )(slots)


def _matmul(a, b, name, out_dtype=F32):
    m, k = a.shape
    _, n = b.shape
    tm, tn, tk = _pick(m, 512, 16), _pick(n, 1408, 128), _pick(k, 1408, 128)
    nk = k // tk

    def body(a_ref, b_ref, o_ref, acc_ref):
        kk = pl.program_id(2)

        @pl.when(kk == 0)
        def _():
            acc_ref[...] = jnp.zeros_like(acc_ref)

        acc_ref[...] += _dot(a_ref[...].astype(MXU_DTYPE), b_ref[...].astype(MXU_DTYPE))

        @pl.when(kk == nk - 1)
        def _():
            o_ref[...] = acc_ref[...].astype(o_ref.dtype)

    return pl.pallas_call(
        body, name=name, grid=(m // tm, n // tn, nk),
        in_specs=[pl.BlockSpec((tm, tk), lambda i, j, kk: (i, kk)), pl.BlockSpec((tk, tn), lambda i, j, kk: (kk, j))],
        out_specs=pl.BlockSpec((tm, tn), lambda i, j, kk: (i, j)),
        out_shape=jax.ShapeDtypeStruct((m, n), out_dtype),
        scratch_shapes=[pltpu.VMEM((tm, tn), F32)],
    )(a, b)


def _matmul_tn(a, b, name):
    l, m = a.shape
    _, n = b.shape
    tm, tn, tl = _pick(m, 1024, 128), _pick(n, 1408, 128), _pick(l, 512, 16)
    nl = l // tl

    def body(a_ref, b_ref, o_ref):
        ll = pl.program_id(2)

        @pl.when(ll == 0)
        def _():
            o_ref[...] = jnp.zeros_like(o_ref)

        o_ref[...] += _dot_tn(a_ref[...].astype(MXU_DTYPE), b_ref[...].astype(MXU_DTYPE))

    return pl.pallas_call(
        body, name=name, grid=(m // tm, n // tn, nl),
        in_specs=[pl.BlockSpec((tl, tm), lambda i, j, ll: (ll, i)), pl.BlockSpec((tl, tn), lambda i, j, ll: (ll, j))],
        out_specs=pl.BlockSpec((tm, tn), lambda i, j, ll: (i, j)),
        out_shape=jax.ShapeDtypeStruct((m, n), F32),
    )(a, b)


def _row_specs(l, d, n_rows, n_vecs):
    tl = _pick(l, 512, 16)
    row = pl.BlockSpec((tl, d), lambda i: (i, 0))
    vec = pl.BlockSpec((1, d), lambda i: (0, 0))
    return tl, row, vec, [row] * n_rows + [vec] * n_vecs


def _rms_mod(x, nw, sc, sh, name):
    l, d = x.shape
    tl, row, _, in_specs = _row_specs(l, d, 1, 3)

    def body(x_ref, nw_ref, sc_ref, sh_ref, h_ref):
        xv = x_ref[...]
        r = lax.rsqrt(jnp.mean(xv * xv, axis=-1, keepdims=True) + EPS)
        h_ref[...] = (((xv * r) * nw_ref[...]) * (1.0 + sc_ref[...]) + sh_ref[...]).astype(h_ref.dtype)

    return pl.pallas_call(body, name=name, grid=(l // tl,), in_specs=in_specs, out_specs=row,
                          out_shape=jax.ShapeDtypeStruct((l, d), MXU_DTYPE))(x, nw, sc, sh)


def _residual_rms_mod(x, mix, g, nw, sc, sh, name):
    l, d = x.shape
    tl, row, _, in_specs = _row_specs(l, d, 2, 4)

    def body(x_ref, mix_ref, g_ref, nw_ref, sc_ref, sh_ref, x1_ref, h_ref):
        xv = x_ref[...] + g_ref[...] * mix_ref[...]
        x1_ref[...] = xv
        r = lax.rsqrt(jnp.mean(xv * xv, axis=-1, keepdims=True) + EPS)
        h_ref[...] = (((xv * r) * nw_ref[...]) * (1.0 + sc_ref[...]) + sh_ref[...]).astype(h_ref.dtype)

    return pl.pallas_call(body, name=name, grid=(l // tl,), in_specs=in_specs, out_specs=[row, row],
                          out_shape=[jax.ShapeDtypeStruct((l, d), F32), jax.ShapeDtypeStruct((l, d), MXU_DTYPE)],
                          )(x, mix, g, nw, sc, sh)


def _norm_bwd(x, dh, dres, nw, sc, name):
    l, d = x.shape
    tl, row, _, in_specs = _row_specs(l, d, 3, 2)

    def body(x_ref, dh_ref, dres_ref, nw_ref, sc_ref, dx_ref, st_ref):
        xv, dh_v = x_ref[...], dh_ref[...]
        r = lax.rsqrt(jnp.mean(xv * xv, axis=-1, keepdims=True) + EPS)
        xn = xv * r
        dxn = dh_v * (nw_ref[...] * (1.0 + sc_ref[...]))
        dx_ref[...] = dres_ref[...] + r * (dxn - xn * jnp.mean(dxn * xn, axis=-1, keepdims=True))

        @pl.when(pl.program_id(0) == 0)
        def _():
            st_ref[...] = jnp.zeros_like(st_ref)

        dhx = dh_v * xn
        st_ref[0:1, :] += _colsum(dhx * nw_ref[...])
        st_ref[1:2, :] += _colsum(dh_v)
        st_ref[2:3, :] += _colsum(dhx * (1.0 + sc_ref[...]))

    return pl.pallas_call(body, name=name, grid=(l // tl,), in_specs=in_specs,
                          out_specs=[row, pl.BlockSpec((8, d), lambda i: (0, 0))],
                          out_shape=[jax.ShapeDtypeStruct((l, d), F32), jax.ShapeDtypeStruct((8, d), F32)],
                          )(x, dh, dres, nw, sc)


def _gate_bwd(dres, val, g, name):
    l, d = dres.shape
    tl, row, _, in_specs = _row_specs(l, d, 2, 1)

    def body(dres_ref, val_ref, g_ref, dval_ref, st_ref):
        dr = dres_ref[...]
        dval_ref[...] = (g_ref[...] * dr).astype(dval_ref.dtype)

        @pl.when(pl.program_id(0) == 0)
        def _():
            st_ref[...] = jnp.zeros_like(st_ref)

        st_ref[0:1, :] += _colsum(dr * val_ref[...])

    return pl.pallas_call(body, name=name, grid=(l // tl,), in_specs=in_specs,
                          out_specs=[row, pl.BlockSpec((8, d), lambda i: (0, 0))],
                          out_shape=[jax.ShapeDtypeStruct((l, d), MXU_DTYPE), jax.ShapeDtypeStruct((8, d), F32)],
                          )(dres, val, g)


def _loss_head(x1, f, g, target, name):
    l, d = x1.shape
    tl, row, _, _ = _row_specs(l, d, 0, 0)
    vec = pl.BlockSpec((1, d), lambda i: (0, 0))

    def body(x1_ref, f_ref, g_ref, t_ref, dy_ref, loss_ref):
        e = x1_ref[...] + g_ref[...] * f_ref[...] - t_ref[...]
        dy_ref[...] = e * (1.0 / d)

        @pl.when(pl.program_id(0) == 0)
        def _():
            loss_ref[...] = jnp.zeros_like(loss_ref)

        s = jnp.sum(jnp.sum(e * e, axis=1, keepdims=True), axis=0, keepdims=True)
        loss_ref[...] += (0.5 / d) * s

    return pl.pallas_call(body, name=name, grid=(l // tl,), in_specs=[row, row, vec, row],
                          out_specs=[row, pl.BlockSpec((8, 128), lambda i: (0, 0))],
                          out_shape=[jax.ShapeDtypeStruct((l, d), F32), jax.ShapeDtypeStruct((8, 128), F32)],
                          )(x1, f, g, target)


def _swiglu(gate, up, name):
    l, f = gate.shape
    tl, tf = _pick(l, 512, 16), _pick(f, 1408, 128)
    spec = pl.BlockSpec((tl, tf), lambda i, j: (i, j))

    def body(g_ref, u_ref, a_ref):
        gv = g_ref[...]
        a_ref[...] = (gv * _sigmoid(gv) * u_ref[...]).astype(a_ref.dtype)

    return pl.pallas_call(body, name=name, grid=(l // tl, f // tf), in_specs=[spec, spec], out_specs=spec,
                          out_shape=jax.ShapeDtypeStruct((l, f), MXU_DTYPE))(gate, up)


def _swiglu_bwd(gate, up, da, name):
    l, f = gate.shape
    tl, tf = _pick(l, 512, 16), _pick(f, 1408, 128)
    spec = pl.BlockSpec((tl, tf), lambda i, j: (i, j))

    def body(g_ref, u_ref, da_ref, dg_ref, du_ref):
        gv, dav = g_ref[...], da_ref[...]
        s = _sigmoid(gv)
        dg_ref[...] = (dav * u_ref[...] * (s * (1.0 + gv * (1.0 - s)))).astype(dg_ref.dtype)
        du_ref[...] = (dav * (gv * s)).astype(du_ref.dtype)

    return pl.pallas_call(body, name=name, grid=(l // tl, f // tf), in_specs=[spec, spec, spec], out_specs=[spec, spec],
                          out_shape=[jax.ShapeDtypeStruct((l, f), MXU_DTYPE)] * 2)(gate, up, da)


def _conv_tile(l):
    return _pick(l, 512, 16)


def _conv_pre(buf, w_ref, b_ref, tl):
    acc = b_ref[...] + w_ref[3:4, :] * buf[HALO:HALO + tl, :]
    for k in range(CONV_WIDTH - 1):
        s = HALO - (CONV_WIDTH - 1) + k
        acc = acc + w_ref[k:k + 1, :] * buf[s:s + tl, :]
    return acc


def _fill_past(buf, u_ref, halo_ref, tl):
    i = pl.program_id(0)

    @pl.when(i == 0)
    def _():
        buf[0:HALO, :] = jnp.zeros((HALO, buf.shape[1]), F32)

    @pl.when(i > 0)
    def _():
        buf[0:HALO, :] = halo_ref[...]

    buf[HALO:HALO + tl, :] = u_ref[...]


def _conv_silu(u, w, b, name):
    l, ch = u.shape
    tl = _conv_tile(l)
    per = tl // HALO
    cur = pl.BlockSpec((tl, ch), lambda i: (i, 0))
    past = pl.BlockSpec((HALO, ch), lambda i: (jnp.maximum(i * per - 1, 0), 0))

    def body(u_ref, halo_ref, w_ref, b_ref, o_ref, buf):
        _fill_past(buf, u_ref, halo_ref, tl)
        pre = _conv_pre(buf, w_ref, b_ref, tl)
        o_ref[...] = pre * _sigmoid(pre)

    return pl.pallas_call(body, name=name, grid=(l // tl,),
                          in_specs=[cur, past, pl.BlockSpec((CONV_WIDTH, ch), lambda i: (0, 0)), pl.BlockSpec((1, ch), lambda i: (0, 0))],
                          out_specs=cur, out_shape=jax.ShapeDtypeStruct((l, ch), F32),
                          scratch_shapes=[pltpu.VMEM((tl + HALO, ch), F32)])(u, u, w, b)


def _conv_silu_bwd_pre(u, dxc, w, b, name):
    l, ch = u.shape
    tl = _conv_tile(l)
    per = tl // HALO
    cur = pl.BlockSpec((tl, ch), lambda i: (i, 0))
    past = pl.BlockSpec((HALO, ch), lambda i: (jnp.maximum(i * per - 1, 0), 0))

    def body(u_ref, halo_ref, d_ref, w_ref, b_ref, dpre_ref, st_ref, buf):
        _fill_past(buf, u_ref, halo_ref, tl)
        pre = _conv_pre(buf, w_ref, b_ref, tl)
        s = _sigmoid(pre)
        dpre = d_ref[...] * (s * (1.0 + pre * (1.0 - s)))
        dpre_ref[...] = dpre

        @pl.when(pl.program_id(0) == 0)
        def _():
            st_ref[...] = jnp.zeros_like(st_ref)

        for k in range(CONV_WIDTH):
            s0 = HALO - (CONV_WIDTH - 1) + k
            st_ref[k:k + 1, :] += _colsum(dpre * buf[s0:s0 + tl, :])
        st_ref[CONV_WIDTH:CONV_WIDTH + 1, :] += _colsum(dpre)

    return pl.pallas_call(body, name=name, grid=(l // tl,),
                          in_specs=[cur, past, cur, pl.BlockSpec((CONV_WIDTH, ch), lambda i: (0, 0)), pl.BlockSpec((1, ch), lambda i: (0, 0))],
                          out_specs=[cur, pl.BlockSpec((8, ch), lambda i: (0, 0))],
                          out_shape=[jax.ShapeDtypeStruct((l, ch), F32), jax.ShapeDtypeStruct((8, ch), F32)],
                          scratch_shapes=[pltpu.VMEM((tl + HALO, ch), F32)])(u, u, dxc, w, b)


def _conv_bwd_input(dpre, w, name):
    l, ch = dpre.shape
    tl = _conv_tile(l)
    per = tl // HALO
    nt = l // tl
    cur = pl.BlockSpec((tl, ch), lambda i: (i, 0))
    nxt = pl.BlockSpec((HALO, ch), lambda i: (jnp.minimum((i + 1) * per, l // HALO - 1), 0))

    def body(d_ref, halo_ref, w_ref, du_ref, buf):
        i = pl.program_id(0)
        buf[0:tl, :] = d_ref[...]

        @pl.when(i == nt - 1)
        def _():
            buf[tl:tl + HALO, :] = jnp.zeros((HALO, ch), F32)

        @pl.when(i < nt - 1)
        def _():
            buf[tl:tl + HALO, :] = halo_ref[...]

        acc = w_ref[3:4, :] * buf[0:tl, :]
        for k in range(CONV_WIDTH - 1):
            s = CONV_WIDTH - 1 - k
            acc = acc + w_ref[k:k + 1, :] * buf[s:s + tl, :]
        du_ref[...] = acc

    return pl.pallas_call(body, name=name, grid=(nt,),
                          in_specs=[cur, nxt, pl.BlockSpec((CONV_WIDTH, ch), lambda i: (0, 0))],
                          out_specs=cur, out_shape=jax.ShapeDtypeStruct((l, ch), F32),
                          scratch_shapes=[pltpu.VMEM((tl + HALO, ch), F32)])(dpre, dpre, w)


HEAD_TILE = 128
D_HEAD_TILES = N_HEADS * HEAD_TILE


def _chunk_iota():
    r = lax.broadcasted_iota(jnp.int32, (CHUNK, CHUNK), 0)
    c = lax.broadcasted_iota(jnp.int32, (CHUNK, CHUNK), 1)
    return r, c


def _split3_dot(v, ones_b):
    hi = v.astype(BF16)
    r1 = v - hi.astype(F32)
    mid = r1.astype(BF16)
    lo = (r1 - mid.astype(F32)).astype(BF16)
    return _dot(hi, ones_b) + _dot(mid, ones_b) + _dot(lo, ones_b)


def _ssd_decays(dtx_ref, dty_ref, dtbx_ref, dtby_ref, alx_ref, aly_ref, r, c):
    tri = (r >= c).astype(F32)
    dt_x = _softplus(dtx_ref[...] + dtbx_ref[...])
    a_x = -jnp.exp(alx_ref[...])
    adt_x = a_x * dt_x
    acs_x = _dot(tri, adt_x, HIGHEST)
    atot_x = _colsum(adt_x)
    acs_y = _dot(tri, -jnp.exp(aly_ref[...]) * _softplus(dty_ref[...] + dtby_ref[...]), HIGHEST)
    return dt_x, a_x, acs_y, jnp.exp(acs_x), jnp.exp(atot_x - acs_x), jnp.exp(atot_x)


def _head_decay(acs_y, e, r, c):
    col = acs_y[:, e * HEAD_TILE:(e + 1) * HEAD_TILE]
    return jnp.where(r >= c, jnp.exp(col - col.T), 0.0)


def _half_masks():
    lane = lax.broadcasted_iota(jnp.int32, (CHUNK, 2 * HEAD_DIM), 1)
    return lane < HEAD_DIM, lane >= HEAD_DIM


def _ssd_specs(nc, reverse):
    def at(i):
        return nc - 1 - i if reverse else i
    xc = pl.BlockSpec((CHUNK, D_CONV), lambda i: (at(i), 0))
    wide = pl.BlockSpec((CHUNK, D_SSD), lambda i: (at(i), 0))
    wide_y = pl.BlockSpec((CHUNK, D_HEAD_TILES), lambda i: (at(i), 0))
    vec = pl.BlockSpec((1, D_SSD), lambda i: (0, 0))
    vec_y = pl.BlockSpec((1, D_HEAD_TILES), lambda i: (0, 0))
    state = pl.BlockSpec((1, SSD_STATE, D_SSD), lambda i: (at(i), 0, 0))
    return xc, wide, wide_y, vec, vec_y, state


def _ssd_fwd(xc, dtr_x, dtr_y, dtb_x, dtb_y, al_x, al_y, name):
    l = xc.shape[0]
    nc = l // CHUNK
    xc_s, wide_s, wide_y_s, vec_s, vec_y_s, state_s = _ssd_specs(nc, False)
    pairs_per_group = GROUP_WIDTH // (2 * HEAD_DIM)

    def body(xc_ref, dtx_ref, dty_ref, dtbx_ref, dtby_ref, alx_ref, aly_ref, y_ref, sp_ref, state):
        @pl.when(pl.program_id(0) == 0)
        def _():
            state[...] = jnp.zeros_like(state)

        r, c = _chunk_iota()
        halves = _half_masks()
        dt_x, _, acs_y, ea_x, ds_x, eatot_x = _ssd_decays(dtx_ref, dty_ref, dtbx_ref, dtby_ref, alx_ref, aly_ref, r, c)
        xg = xc_ref[:, 0:D_SSD] * dt_x
        sp_ref[0] = state[...]
        for g in range(SSD_GROUPS):
            lanes = slice(g * GROUP_WIDTH, (g + 1) * GROUP_WIDTH)
            bb = xc_ref[:, D_SSD + g * SSD_STATE:D_SSD + (g + 1) * SSD_STATE].astype(MXU_DTYPE)
            cb = xc_ref[:, D_SSD + (SSD_GROUPS + g) * SSD_STATE:D_SSD + (SSD_GROUPS + g + 1) * SSD_STATE].astype(MXU_DTYPE)
            scores = _dot_nt(cb, bb)
            sg = state[:, lanes]
            ys = []
            for j in range(g * pairs_per_group, (g + 1) * pairs_per_group):
                xg_pair = xg[:, j * 2 * HEAD_DIM:(j + 1) * 2 * HEAD_DIM]
                acc = jnp.zeros((CHUNK, 2 * HEAD_DIM), F32)
                for half in range(2):
                    m = (scores * _head_decay(acs_y, 2 * j + half, r, c)).astype(MXU_DTYPE)
                    acc = acc + _dot(m, jnp.where(halves[half], xg_pair, 0.0).astype(MXU_DTYPE))
                ys.append(acc)
            y_ref[:, lanes] = jnp.concatenate(ys, axis=1) + _dot(cb, sg.astype(MXU_DTYPE)) * ea_x[:, lanes]
            state[:, lanes] = sg * eatot_x[:, lanes] + _dot_tn(bb, (xg[:, lanes] * ds_x[:, lanes]).astype(MXU_DTYPE))

    return pl.pallas_call(
        body, name=name, grid=(nc,),
        in_specs=[xc_s, wide_s, wide_y_s, vec_s, vec_y_s, vec_s, vec_y_s],
        out_specs=[wide_s, state_s],
        out_shape=[jax.ShapeDtypeStruct((l, D_SSD), F32), jax.ShapeDtypeStruct((nc, SSD_STATE, D_SSD), F32)],
        scratch_shapes=[pltpu.VMEM((SSD_STATE, D_SSD), F32)],
    )(xc, dtr_x, dtr_y, dtb_x, dtb_y, al_x, al_y)


def _ssd_bwd(xc, dtr_x, dtr_y, dtb_x, dtb_y, al_x, al_y, dy, states, head_ones, name):
    l = xc.shape[0]
    nc = l // CHUNK
    xc_s, wide_s, wide_y_s, vec_s, vec_y_s, state_s = _ssd_specs(nc, True)
    pairs_per_group = GROUP_WIDTH // (2 * HEAD_DIM)

    def body(xc_ref, dtx_ref, dty_ref, dtbx_ref, dtby_ref, alx_ref, aly_ref, dy_ref, sp_ref, ones_ref,
             dxc_ref, ddtr_ref, st_ref, hs_ref, dstate):
        @pl.when(pl.program_id(0) == 0)
        def _():
            dstate[...] = jnp.zeros_like(dstate)
            st_ref[...] = jnp.zeros_like(st_ref)

        r, c = _chunk_iota()
        halves = _half_masks()
        ones_b = ones_ref[...]
        dt_x, a_x, acs_y, ea_x, ds_x, eatot_x = _ssd_decays(dtx_ref, dty_ref, dtbx_ref, dtby_ref, alx_ref, aly_ref, r, c)
        xs = xc_ref[:, 0:D_SSD]
        xg = xs * dt_x
        gy = dy_ref[...]
        s_prev = sp_ref[0]
        gea = gy * ea_x
        xds = xg * ds_x
        ds_old = dstate[...]
        later2 = jnp.concatenate([(r <= c).astype(BF16)] * 2, axis=1)
        dxg_parts, state_term, yoff_parts, dadt_parts = [], [], [], []
        for g in range(SSD_GROUPS):
            lanes = slice(g * GROUP_WIDTH, (g + 1) * GROUP_WIDTH)
            b_lo = D_SSD + g * SSD_STATE
            c_lo = D_SSD + (SSD_GROUPS + g) * SSD_STATE
            bb = xc_ref[:, b_lo:b_lo + SSD_STATE].astype(MXU_DTYPE)
            cb = xc_ref[:, c_lo:c_lo + SSD_STATE].astype(MXU_DTYPE)
            scores = _dot_nt(cb, bb)
            dsg = ds_old[:, lanes].astype(MXU_DTYPE)
            gea_b = gea[:, lanes].astype(MXU_DTYPE)
            xds_b = xds[:, lanes].astype(MXU_DTYPE)
            dxg_state = _dot(bb, dsg) * ds_x[:, lanes]
            dc = _dot_nt(gea_b, s_prev[:, lanes].astype(MXU_DTYPE))
            db = _dot_nt(xds_b, dsg)
            dscores = jnp.zeros((CHUNK, CHUNK), F32)
            diag = []
            for j in range(g * pairs_per_group, (g + 1) * pairs_per_group):
                pair = slice(j * 2 * HEAD_DIM, (j + 1) * 2 * HEAD_DIM)
                xg_pair = xg[:, pair].astype(MXU_DTYPE)
                acc = jnp.zeros((CHUNK, 2 * HEAD_DIM), F32)
                cols = []
                for half in range(2):
                    decay = _head_decay(acs_y, 2 * j + half, r, c)
                    g_e = jnp.where(halves[half], gy[:, pair], 0.0).astype(MXU_DTYPE)
                    acc = acc + _dot_tn((scores * decay).astype(MXU_DTYPE), g_e)
                    dm = _dot_nt(g_e, xg_pair) * decay
                    dscores = dscores + dm
                    wq = dm * scores
                    hi = wq.astype(BF16)
                    lo = (wq - hi.astype(F32)).astype(BF16)
                    later = _dot(later2, jnp.concatenate([hi, lo], axis=0))
                    cols.append(jnp.sum(jnp.where(c < r, later, 0.0), axis=1, keepdims=True))
                diag.append(acc)
                dadt_parts.append(jnp.where(halves[0], cols[0], cols[1]))
            dsc_b = dscores.astype(MXU_DTYPE)
            dc = dc + _dot(dsc_b, bb)
            db = db + _dot_tn(dsc_b, cb)
            dxc_ref[:, b_lo:b_lo + SSD_STATE] = db
            dxc_ref[:, c_lo:c_lo + SSD_STATE] = dc
            dxg_parts.append(jnp.concatenate(diag, axis=1) + dxg_state)
            state_term.append(dxg_state)
            yoff_parts.append(_dot(cb, s_prev[:, lanes].astype(MXU_DTYPE)) * ea_x[:, lanes])
            dstate[:, lanes] = ds_old[:, lanes] * eatot_x[:, lanes] + _dot_tn(cb, gea_b)
        dxg = jnp.concatenate(dxg_parts, axis=1)
        dxc_ref[:, 0:D_SSD] = dxg * dt_x
        through_out = _dot((r <= c).astype(F32), gy * jnp.concatenate(yoff_parts, axis=1), HIGHEST)
        through_in = _dot((c < r).astype(F32), xg * jnp.concatenate(state_term, axis=1), HIGHEST)
        carried = jnp.broadcast_to(_colsum(ds_old * s_prev) * eatot_x, (8, D_SSD))
        dadt = (jnp.concatenate(dadt_parts, axis=1) + _split3_dot(through_out + through_in, ones_b)
                + jnp.max(_split3_dot(carried, ones_b), axis=0, keepdims=True))
        ddt = a_x * dadt + _split3_dot(dxg * xs, ones_b)
        draw = ddt * _sigmoid(dtx_ref[...] + dtbx_ref[...])
        ddtr_ref[...] = draw
        st_ref[0:1, :] += _colsum(dt_x * dadt) * a_x
        st_ref[1:2, :] += _colsum(draw)
        st_ref[2:3, :] += _colsum(gy * xs)

        @pl.when(pl.program_id(0) == nc - 1)
        def _():
            hs_ref[...] = _split3_dot(st_ref[...], ones_b)

    stats = pl.BlockSpec((8, D_SSD), lambda i: (0, 0))
    return pl.pallas_call(
        body, name=name, grid=(nc,),
        in_specs=[xc_s, wide_s, wide_y_s, vec_s, vec_y_s, vec_s, vec_y_s, wide_s, state_s,
                  pl.BlockSpec((D_SSD, D_SSD), lambda i: (0, 0))],
        out_specs=[xc_s, wide_s, stats, stats],
        out_shape=[jax.ShapeDtypeStruct((l, D_CONV), F32), jax.ShapeDtypeStruct((l, D_SSD), F32),
                   jax.ShapeDtypeStruct((8, D_SSD), F32), jax.ShapeDtypeStruct((8, D_SSD), F32)],
        scratch_shapes=[pltpu.VMEM((SSD_STATE, D_SSD), F32)],
    )(xc, dtr_x, dtr_y, dtb_x, dtb_y, al_x, al_y, dy, states, head_ones)


def _ssd_gate_norm(ycore, xc, z, dskip_x, norm_w, name):
    l = ycore.shape[0]
    tl = _pick(l, 512, 16)
    row = pl.BlockSpec((tl, D_SSD), lambda i: (i, 0))
    vec = pl.BlockSpec((1, D_SSD), lambda i: (0, 0))

    def body(y_ref, xs_ref, z_ref, dk_ref, nw_ref, o_ref):
        zv = z_ref[...]
        yv = (y_ref[...] + dk_ref[...] * xs_ref[...]) * (zv * _sigmoid(zv))
        for g in range(SSD_GROUPS):
            lanes = slice(g * GROUP_WIDTH, (g + 1) * GROUP_WIDTH)
            yg = yv[:, lanes]
            rg = lax.rsqrt(jnp.mean(yg * yg, axis=-1, keepdims=True) + EPS)
            o_ref[:, lanes] = (yg * rg * nw_ref[:, lanes]).astype(o_ref.dtype)

    return pl.pallas_call(body, name=name, grid=(l // tl,), in_specs=[row, row, row, vec, vec], out_specs=row,
                          out_shape=jax.ShapeDtypeStruct((l, D_SSD), MXU_DTYPE))(ycore, xc, z, dskip_x, norm_w)


def _ssd_gate_norm_bwd(dout, ycore, xc, z, dskip_x, norm_w, name):
    l = ycore.shape[0]
    tl = _pick(l, 512, 16)
    nt = l // tl
    row = pl.BlockSpec((tl, D_SSD), lambda i: (i, 0))
    vec = pl.BlockSpec((1, D_SSD), lambda i: (0, 0))

    def body(do_ref, y_ref, xs_ref, z_ref, dk_ref, nw_ref, dyc_ref, dxs_ref, dz_ref, st_ref):
        @pl.when(pl.program_id(0) == 0)
        def _():
            st_ref[...] = jnp.zeros_like(st_ref)

        zv, xs = z_ref[...], xs_ref[...]
        s = _sigmoid(zv)
        gz = zv * s
        yc = y_ref[...] + dk_ref[...] * xs
        yv = yc * gz
        dov = do_ref[...]
        dnw, dyv = [], []
        for g in range(SSD_GROUPS):
            lanes = slice(g * GROUP_WIDTH, (g + 1) * GROUP_WIDTH)
            yg = yv[:, lanes]
            rg = lax.rsqrt(jnp.mean(yg * yg, axis=-1, keepdims=True) + EPS)
            yn = yg * rg
            dnw.append(_colsum(dov[:, lanes] * yn))
            dyn = dov[:, lanes] * nw_ref[:, lanes]
            dyv.append(rg * (dyn - yn * jnp.mean(dyn * yn, axis=-1, keepdims=True)))
        dy = jnp.concatenate(dyv, axis=1)
        dyc = dy * gz
        dyc_ref[...] = dyc
        dxs_ref[...] = dyc * dk_ref[...]
        dz_ref[...] = dy * yc * (s * (1.0 + zv * (1.0 - s)))
        st_ref[0:1, :] += jnp.concatenate(dnw, axis=1)

    return pl.pallas_call(
        body, name=name, grid=(nt,), in_specs=[row, row, row, row, vec, vec],
        out_specs=[row, row, row, pl.BlockSpec((8, D_SSD), lambda i: (0, 0))],
        out_shape=[jax.ShapeDtypeStruct((l, D_SSD), F32)] * 3 + [jax.ShapeDtypeStruct((8, D_SSD), F32)],
    )(dout, ycore, xc, z, dskip_x, norm_w)


def _head_norm(t, w, scale, name):
    h, l, dh = t.shape
    tl = _pick(l, 1024, 16)
    blk = pl.BlockSpec((1, tl, dh), lambda i, j: (i, j, 0))

    def body(t_ref, w_ref, o_ref):
        tv = t_ref[0]
        r = lax.rsqrt(jnp.mean(tv * tv, axis=-1, keepdims=True) + EPS)
        o_ref[0] = ((tv * r) * w_ref[...] * scale).astype(o_ref.dtype)

    return pl.pallas_call(body, name=name, grid=(h, l // tl), in_specs=[blk, pl.BlockSpec((1, dh), lambda i, j: (0, 0))],
                          out_specs=blk, out_shape=jax.ShapeDtypeStruct((h, l, dh), MXU_DTYPE))(t, w)


def _head_norm_bwd(t, dn, w, scale, name):
    h, l, dh = t.shape
    tl = _pick(l, 1024, 16)
    blk = pl.BlockSpec((1, tl, dh), lambda i, j: (i, j, 0))

    def body(t_ref, dn_ref, w_ref, dt_ref, st_ref):
        @pl.when((pl.program_id(0) == 0) & (pl.program_id(1) == 0))
        def _():
            st_ref[...] = jnp.zeros_like(st_ref)

        tv = t_ref[0]
        r = lax.rsqrt(jnp.mean(tv * tv, axis=-1, keepdims=True) + EPS)
        tn = tv * r
        dnv = dn_ref[0] * scale
        dtn = dnv * w_ref[...]
        dt_ref[0] = r * (dtn - tn * jnp.mean(dtn * tn, axis=-1, keepdims=True))
        st_ref[0:1, :] += _colsum(dnv * tn)

    return pl.pallas_call(body, name=name, grid=(h, l // tl),
                          in_specs=[blk, blk, pl.BlockSpec((1, dh), lambda i, j: (0, 0))],
                          out_specs=[blk, pl.BlockSpec((8, dh), lambda i, j: (0, 0))],
                          out_shape=[jax.ShapeDtypeStruct((h, l, dh), F32), jax.ShapeDtypeStruct((8, dh), F32)])(t, dn, w)


def _sb_masks():
    r = lax.broadcasted_iota(jnp.int32, (CHUNK, CHUNK), 0)
    c = lax.broadcasted_iota(jnp.int32, (CHUNK, CHUNK), 1)
    return r, c


def _stack2(mask):
    t = mask.astype(BF16)
    return jnp.concatenate([t, t], axis=0)


def _sb_fwd(q, k, v, name):
    h, l, dh = q.shape
    nq = l // CHUNK
    qblk = pl.BlockSpec((1, CHUNK, dh), lambda i, j: (i, j, 0))
    full = pl.BlockSpec((1, l, dh), lambda i, j: (i, 0, 0))

    def body(q_ref, k_ref, v_ref, o_ref, tot_ref):
        qb = pl.program_id(1)
        r, c = _sb_masks()
        strict = c < r
        after2 = _stack2(r > c)
        qv = q_ref[0]

        def block(kb, carry, acc, diagonal):
            start = pl.multiple_of(kb * CHUNK, CHUNK)
            kv = k_ref[0, pl.ds(start, CHUNK), :]
            vv = v_ref[0, pl.ds(start, CHUNK), :]
            lg = _dot_nt(qv, kv)
            sp = _softplus(lg)
            lr = jnp.where(strict, -sp, 0.0) if diagonal else -sp
            cs = _split_dot(lr, after2) + carry
            w = jnp.exp(lg - sp + cs)
            if diagonal:
                w = jnp.where(strict, w, 0.0)
            acc = acc + _dot(w.astype(MXU_DTYPE), vv)
            return carry + jnp.sum(lr, axis=1, keepdims=True), acc

        carry, acc = block(qb, jnp.zeros((CHUNK, 1), F32), jnp.zeros((CHUNK, dh), F32), True)

        def step(i, st):
            return block(qb - i, st[0], st[1], False)

        carry, acc = lax.fori_loop(1, qb + 1, step, (carry, acc))
        o_ref[0] = acc
        tot_ref[0] = jnp.broadcast_to(carry, (CHUNK, CHUNK))

    return pl.pallas_call(
        body, name=name, grid=(h, nq), in_specs=[qblk, full, full],
        out_specs=[qblk, pl.BlockSpec((1, CHUNK, CHUNK), lambda i, j: (i, j, 0))],
        out_shape=[jax.ShapeDtypeStruct((h, l, dh), F32), jax.ShapeDtypeStruct((h, l, CHUNK), F32)],
    )(q, k, v)


def _sb_bwd(q, k, v, do, tot, name):
    h, l, dh = q.shape
    nq = l // CHUNK
    qblk = pl.BlockSpec((1, CHUNK, dh), lambda i, j: (i, j, 0))
    full = pl.BlockSpec((1, l, dh), lambda i, j: (i, 0, 0))

    def body(q_ref, k_ref, v_ref, do_ref, tot_ref, dq_ref, dk_ref, dv_ref):
        qb = pl.program_id(1)

        @pl.when(qb == 0)
        def _():
            dk_ref[...] = jnp.zeros_like(dk_ref)
            dv_ref[...] = jnp.zeros_like(dv_ref)

        r, c = _sb_masks()
        strict = c < r
        upto2 = _stack2(r <= c)
        before2 = _stack2(r < c)
        qv = q_ref[0]
        dov = do_ref[0].astype(MXU_DTYPE)
        total = tot_ref[0]

        def block(kb, carry_p, carry_d, dq, diagonal):
            start = pl.multiple_of(kb * CHUNK, CHUNK)
            rows = pl.ds(start, CHUNK)
            kv = k_ref[0, rows, :]
            vv = v_ref[0, rows, :]
            lg = _dot_nt(qv, kv)
            sp = _softplus(lg)
            lr = jnp.where(strict, -sp, 0.0) if diagonal else -sp
            p_incl = _split_dot(lr, upto2) + carry_p
            ls = lg - sp
            w = jnp.exp(ls + (total - p_incl))
            if diagonal:
                w = jnp.where(strict, w, 0.0)
            da = _dot_nt(dov, vv) * w
            d_excl = _split_dot(da, before2) + carry_d
            sig = jnp.exp(ls)
            dl = da * (1.0 - sig) - d_excl * sig
            if diagonal:
                dl = jnp.where(strict, dl, 0.0)
            dl_b = dl.astype(MXU_DTYPE)
            dv_ref[0, rows, :] += _dot_tn(w.astype(MXU_DTYPE), dov)
            dk_ref[0, rows, :] += _dot_tn(dl_b, qv)
            dq = dq + _dot(dl_b, kv)
            return carry_p + jnp.sum(lr, axis=1, keepdims=True), carry_d + jnp.sum(da, axis=1, keepdims=True), dq

        def step(kb, st):
            return block(kb, st[0], st[1], st[2], False)

        zero = jnp.zeros((CHUNK, 1), F32)
        carry_p, carry_d, dq = lax.fori_loop(0, qb, step, (zero, zero, jnp.zeros((CHUNK, dh), F32)))
        _, _, dq = block(qb, carry_p, carry_d, dq, True)
        dq_ref[0] = dq

    return pl.pallas_call(
        body, name=name, grid=(h, nq),
        in_specs=[qblk, full, full, qblk, pl.BlockSpec((1, CHUNK, CHUNK), lambda i, j: (i, j, 0))],
        out_specs=[qblk, full, full],
        out_shape=[jax.ShapeDtypeStruct((h, l, dh), F32)] * 3,
    )(q, k, v, do, tot)


def _ada_fwd(c_all, w_shard, name):
    def body(c_ref, w_ref, o_ref):
        cv = c_ref[...]
        o_ref[...] = _dot(cv * _sigmoid(cv), w_ref[...], HIGHEST)

    return pl.pallas_call(body, name=name, out_shape=jax.ShapeDtypeStruct((c_all.shape[0], w_shard.shape[1]), F32))(c_all, w_shard)


def _ada_bwd(c_all, dmod_shard, name):
    def body(c_ref, d_ref, o_ref):
        cv = c_ref[...]
        o_ref[...] = _dot_tn(cv * _sigmoid(cv), d_ref[...], HIGHEST)

    return pl.pallas_call(body, name=name, out_shape=jax.ShapeDtypeStruct((c_all.shape[1], dmod_shard.shape[1]), F32))(c_all, dmod_shard)


def _sum_small(parts, name):
    def body(p_ref, o_ref):
        acc = p_ref[0]
        for d in range(1, N_DEV):
            acc = acc + p_ref[d]
        o_ref[...] = acc

    return pl.pallas_call(body, name=name, out_shape=jax.ShapeDtypeStruct(parts.shape[1:], F32))(parts)


def _adamw(w, g, m, v, name):
    rows, cols = w.shape
    tr = _pick(rows, 256, 8)
    spec = pl.BlockSpec((tr, cols), lambda i: (i, 0))
    bc1 = 1.0 - ADAM_B1 ** ADAM_STEP
    bc2 = 1.0 - ADAM_B2 ** ADAM_STEP

    def body(w_ref, g_ref, m_ref, v_ref, d_ref, nm_ref, nv_ref):
        gv = g_ref[...]
        nm = ADAM_B1 * m_ref[...] + (1.0 - ADAM_B1) * gv
        nv = ADAM_B2 * v_ref[...] + (1.0 - ADAM_B2) * (gv * gv)
        nm_ref[...] = nm
        nv_ref[...] = nv
        d_ref[...] = -ADAM_LR * ((nm / bc1) / (jnp.sqrt(nv / bc2) + ADAM_EPS) + ADAM_WD * w_ref[...])

    return pl.pallas_call(body, name=name, grid=(rows // tr,), in_specs=[spec] * 4, out_specs=[spec] * 3,
                          out_shape=[jax.ShapeDtypeStruct((rows, cols), F32)] * 3)(w, g, m, v)


def _pack_shards(w_in, w_out, w_gate, w_up, w_down):
    used = ROWS_W_IN + ROWS_W_OUT + 3 * ROWS_FF
    parts = [w_in.reshape(ROWS_W_IN, D_MODEL), w_out, w_gate.reshape(ROWS_FF, D_MODEL), w_up.reshape(ROWS_FF, D_MODEL),
             w_down, jnp.zeros((PACK_ROWS - used, D_MODEL), w_in.dtype)]
    return jnp.concatenate(parts, axis=0)


def _unpack_shards(p):
    o = 0
    w_in = p[o:o + ROWS_W_IN].reshape(D_MODEL, ROWS_W_IN)
    o += ROWS_W_IN
    w_out = p[o:o + ROWS_W_OUT]
    o += ROWS_W_OUT
    w_gate = p[o:o + ROWS_FF].reshape(D_MODEL, ROWS_FF)
    o += ROWS_FF
    w_up = p[o:o + ROWS_FF].reshape(D_MODEL, ROWS_FF)
    o += ROWS_FF
    w_down = p[o:o + ROWS_FF]
    return w_in, w_out, w_gate, w_up, w_down


def _unpack_gathered(gw):
    def cols(lo, rows):
        blk = gw[:, lo:lo + rows].reshape(N_DEV, D_MODEL, rows)
        return blk.transpose(1, 0, 2).reshape(D_MODEL, N_DEV * rows)

    def rows_(lo, rows):
        return gw[:, lo:lo + rows].reshape(N_DEV * rows, D_MODEL)

    o = 0
    w_in = cols(o, ROWS_W_IN)
    o += ROWS_W_IN
    w_out = rows_(o, ROWS_W_OUT)
    o += ROWS_W_OUT
    w_gate = cols(o, ROWS_FF)
    o += ROWS_FF
    w_up = cols(o, ROWS_FF)
    o += ROWS_FF
    w_down = rows_(o, ROWS_FF)
    return w_in, w_out, w_gate, w_up, w_down


def _pack_grads(g_in, g_out, g_gate, g_up, g_down):
    def cols(gw, rows):
        return gw.reshape(D_MODEL, N_DEV, rows).transpose(1, 0, 2).reshape(N_DEV, rows, D_MODEL)

    def rows_(gw, rows):
        return gw.reshape(N_DEV, rows, D_MODEL)

    used = ROWS_W_IN + ROWS_W_OUT + 3 * ROWS_FF
    parts = [cols(g_in, ROWS_W_IN), rows_(g_out, ROWS_W_OUT), cols(g_gate, ROWS_FF), cols(g_up, ROWS_FF), rows_(g_down, ROWS_FF),
             jnp.zeros((N_DEV, PACK_ROWS - used, D_MODEL), g_in.dtype)]
    return jnp.concatenate(parts, axis=1)


def _pad_w_in(w_in):
    lo = D_SSD + D_CONV
    return jnp.concatenate([w_in[:, :lo + N_HEADS], jnp.zeros((w_in.shape[0], DT_PAD - N_HEADS), w_in.dtype), w_in[:, lo + N_HEADS:]], axis=1)


def _unpad_w_in(g):
    lo = D_SSD + D_CONV
    return jnp.concatenate([g[:, :lo + N_HEADS], g[:, lo + DT_PAD:]], axis=1)


def _to_heads(t):
    l = t.shape[0]
    return t.reshape(l, N_HEADS, HEAD_DIM).transpose(1, 0, 2)


def _from_heads(t):
    l = t.shape[1]
    return t.transpose(1, 0, 2).reshape(l, N_HEADS * HEAD_DIM)


def _local_step(x, target, mod, norm1_w, w_in_p, conv_w, conv_b, dt_bias, a_log, d_skip, ssd_norm_w, q_norm_w, k_norm_w,
                w_out, norm2_w, w_gate, w_up, w_down):
    sh1, sc1, g1, sh2, sc2, g2 = [mod[:, i * D_MODEL:(i + 1) * D_MODEL] for i in range(N_MOD)]
    o_xbc = D_SSD
    o_dt = D_SSD + D_CONV
    o_q = o_dt + DT_PAD
    dskip_x = jnp.repeat(d_skip, HEAD_DIM, axis=1)
    dtb_x, dtb_y = jnp.repeat(dt_bias, HEAD_DIM, axis=1), jnp.repeat(dt_bias, HEAD_TILE, axis=1)
    al_x, al_y = jnp.repeat(a_log, HEAD_DIM, axis=1), jnp.repeat(a_log, HEAD_TILE, axis=1)
    head_of_lane = jnp.arange(D_SSD, dtype=jnp.int32) // HEAD_DIM
    head_ones = (head_of_lane[:, None] == head_of_lane[None, :]).astype(BF16)
    sb_scale = HEAD_DIM ** -0.5

    h1 = _rms_mod(x, norm1_w, sc1, sh1, "rms_mod1")
    proj = _matmul(h1, w_in_p, "in_proj")
    z = proj[:, :D_SSD]
    xbc = proj[:, o_xbc:o_xbc + D_CONV]
    dtr = proj[:, o_dt:o_dt + N_HEADS]
    dtr_x, dtr_y = jnp.repeat(dtr, HEAD_DIM, axis=1), jnp.repeat(dtr, HEAD_TILE, axis=1)
    q, k, v = [_to_heads(proj[:, o_q + i * D_SB:o_q + (i + 1) * D_SB]) for i in range(3)]

    xc = _conv_silu(xbc, conv_w, conv_b, "conv_silu")
    ycore, states = _ssd_fwd(xc, dtr_x, dtr_y, dtb_x, dtb_y, al_x, al_y, "ssd_fwd")
    y_ssd = _ssd_gate_norm(ycore, xc, z, dskip_x, ssd_norm_w, "ssd_gate_norm")

    qn = _head_norm(q, q_norm_w, sb_scale, "q_norm")
    kn = _head_norm(k, k_norm_w, 1.0, "k_norm")
    vb = v.astype(MXU_DTYPE)
    o_sb, tot = _sb_fwd(qn, kn, vb, "sb_fwd")
    ycat = jnp.concatenate([y_ssd, _from_heads(o_sb).astype(MXU_DTYPE)], axis=1)

    mix = _matmul(ycat, w_out, "out_proj")
    x1, h2 = _residual_rms_mod(x, mix, g1, norm2_w, sc2, sh2, "residual_rms_mod2")
    gate = _matmul(h2, w_gate, "ffn_gate")
    up = _matmul(h2, w_up, "ffn_up")
    act = _swiglu(gate, up, "swiglu")
    ffn = _matmul(act, w_down, "ffn_down")
    dy, loss_blk = _loss_head(x1, ffn, g2, target, "loss_head")

    dffn, st_g2 = _gate_bwd(dy, ffn, g2, "gate2_bwd")
    g_down = _matmul_tn(act, dffn, "ffn_down_dw")
    dact = _matmul(dffn, w_down.T, "ffn_down_dx")
    dgate, dup = _swiglu_bwd(gate, up, dact, "swiglu_bwd")
    g_gate = _matmul_tn(h2, dgate, "ffn_gate_dw")
    g_up = _matmul_tn(h2, dup, "ffn_up_dw")
    dh2 = _matmul(jnp.concatenate([dgate, dup], axis=1), jnp.concatenate([w_gate, w_up], axis=1).T, "ffn_dh")
    dx1, st_n2 = _norm_bwd(x1, dh2, dy, norm2_w, sc2, "norm2_bwd")

    dmix, st_g1 = _gate_bwd(dx1, mix, g1, "gate1_bwd")
    g_out = _matmul_tn(ycat, dmix, "out_proj_dw")
    dycat = _matmul(dmix, w_out.T, "out_proj_dx")
    dy_ssd, do_sb = dycat[:, :D_SSD], _to_heads(dycat[:, D_SSD:])

    dqn, dkn, dv = _sb_bwd(qn, kn, vb, do_sb, tot, "sb_bwd")
    dq, st_q = _head_norm_bwd(q, dqn, q_norm_w, sb_scale, "q_norm_bwd")
    dk, st_k = _head_norm_bwd(k, dkn, k_norm_w, 1.0, "k_norm_bwd")

    dycore, dxs_skip, dz, st_gn = _ssd_gate_norm_bwd(dy_ssd, ycore, xc, z, dskip_x, ssd_norm_w, "ssd_gate_norm_bwd")
    dxc, ddtr_x, st_ssd, st_heads = _ssd_bwd(xc, dtr_x, dtr_y, dtb_x, dtb_y, al_x, al_y, dycore, states, head_ones, "ssd_bwd")
    ddtr = ddtr_x[:, ::HEAD_DIM]
    dxc = jnp.concatenate([dxc[:, :D_SSD] + dxs_skip, dxc[:, D_SSD:]], axis=1)
    dpre, st_conv = _conv_silu_bwd_pre(xbc, dxc, conv_w, conv_b, "conv_silu_bwd")
    dxbc = _conv_bwd_input(dpre, conv_w, "conv_bwd_input")

    dproj = jnp.concatenate(
        [dz, dxbc, ddtr, jnp.zeros((x.shape[0], DT_PAD - N_HEADS), F32), _from_heads(dq), _from_heads(dk), _from_heads(dv)],
        axis=1).astype(MXU_DTYPE)
    g_in = _unpad_w_in(_matmul_tn(h1, dproj, "in_proj_dw"))
    dh1 = _matmul(dproj, w_in_p.T, "in_proj_dx")
    grad_x, st_n1 = _norm_bwd(x, dh1, dx1, norm1_w, sc1, "norm1_bwd")

    pad = jnp.zeros((1, SM_SSD_NORM - SM_D_SKIP - N_HEADS), F32)
    small = jnp.concatenate(
        [st_n1[1:2], st_n1[0:1], st_g1[0:1], st_n2[1:2], st_n2[0:1], st_g2[0:1],
         st_n1[2:3], st_conv[4:5], st_conv[0:4].reshape(1, CONV_WIDTH * D_CONV),
         st_ssd[1:2, ::HEAD_DIM], st_ssd[0:1, ::HEAD_DIM], st_heads[2:3, ::HEAD_DIM], pad,
         st_gn[0:1], st_q[0:1], st_k[0:1], st_n2[2:3]], axis=1)
    return loss_blk, grad_x, (g_in, g_out, g_gate, g_up, g_down), small


def kernel(x, c, w_ada, b_ada, norm1_w, w_in, conv_w, conv_b, dt_bias, a_log, d_skip, ssd_norm_w, q_norm_w, k_norm_w, w_out, norm2_w, w_gate, w_up, w_down, loss_target, m_w_ada, m_b_ada, m_norm1_w, m_w_in, m_conv_w, m_conv_b, m_dt_bias, m_a_log, m_d_skip, m_ssd_norm_w, m_q_norm_w, m_k_norm_w, m_w_out, m_norm2_w, m_w_gate, m_w_up, m_w_down, v_w_ada, v_b_ada, v_norm1_w, v_w_in, v_conv_w, v_conv_b, v_dt_bias, v_a_log, v_d_skip, v_ssd_norm_w, v_q_norm_w, v_k_norm_w, v_w_out, v_norm2_w, v_w_gate, v_w_up, v_w_down):
    me = 4 * lax.axis_index("x") + 2 * lax.axis_index("y") + lax.axis_index("c")
    conv_cols = D_CONV // N_DEV
    ada_cols = N_MOD * D_MODEL // N_DEV

    packed = _pack_shards(w_in[0], w_out[0], w_gate[0], w_up[0], w_down[0]).astype(BF16)
    gathered = _all_gather_big(packed, "gather_weights")
    w_in_f, w_out_f, w_gate_f, w_up_f, w_down_f = _unpack_gathered(gathered)
    w_in_p = _pad_w_in(w_in_f)

    first = _all_gather_small(jnp.concatenate([c, conv_w[0].reshape(1, CONV_WIDTH * conv_cols)], axis=1), "gather_cond")
    c_all = first[:, 0, :D_MODEL]
    conv_w_f = first[:, 0, D_MODEL:].reshape(N_DEV, CONV_WIDTH, conv_cols).transpose(1, 0, 2).reshape(CONV_WIDTH, D_CONV)
    mod_cols = _ada_fwd(c_all, w_ada[0], "ada_fwd")
    mods = _all_gather_small(mod_cols.reshape(1, N_DEV * ada_cols), "gather_mod")
    mod = lax.dynamic_index_in_dim(mods.reshape(N_DEV, N_DEV, ada_cols), me, axis=1, keepdims=False)
    mod = mod.reshape(1, N_MOD * D_MODEL) + b_ada

    loss_blk, grad_x, big, small = _local_step(
        x[0], loss_target[0], mod, norm1_w, w_in_p, conv_w_f, conv_b, dt_bias, a_log, d_skip, ssd_norm_w, q_norm_w, k_norm_w,
        w_out_f, norm2_w, w_gate_f, w_up_f, w_down_f)
    loss = lax.psum(loss_blk[0, 0], ("x", "y", "c"))

    slots = _all_to_all_big(_pack_grads(*big).astype(BF16), "scatter_grads")
    g_in, g_out, g_gate, g_up, g_down = _unpack_shards(_sum_slots(slots, "sum_grads"))

    parts = _all_gather_small(small, "gather_small")
    gsum = _sum_small(parts, "sum_small")
    dmod_shard = lax.dynamic_slice_in_dim(parts[:, 0, :N_MOD * D_MODEL], me * ada_cols, ada_cols, axis=1)
    g_ada = _ada_bwd(c_all, dmod_shard, "ada_bwd")
    g_conv_w = lax.dynamic_slice_in_dim(gsum[:, SM_CONV_W:SM_DT_BIAS].reshape(CONV_WIDTH, D_CONV), me * conv_cols, conv_cols, axis=1)

    def pack_small(b_ada_, norm1_, conv_b_, dt_bias_, a_log_, d_skip_, ssd_norm_, q_norm_, k_norm_, norm2_):
        return jnp.concatenate(
            [b_ada_, norm1_, conv_b_, jnp.zeros((1, CONV_WIDTH * D_CONV), F32), dt_bias_, a_log_, d_skip_,
             jnp.zeros((1, SM_SSD_NORM - SM_D_SKIP - N_HEADS), F32), ssd_norm_, q_norm_, k_norm_, norm2_], axis=1)

    def unpack_small(p):
        return {
            "b_ada": p[:, SM_B_ADA:SM_NORM1], "norm1_w": p[:, SM_NORM1:SM_CONV_B], "conv_b": p[:, SM_CONV_B:SM_CONV_W],
            "dt_bias": p[:, SM_DT_BIAS:SM_A_LOG], "a_log": p[:, SM_A_LOG:SM_D_SKIP], "d_skip": p[:, SM_D_SKIP:SM_D_SKIP + N_HEADS],
            "ssd_norm_w": p[:, SM_SSD_NORM:SM_Q_NORM], "q_norm_w": p[:, SM_Q_NORM:SM_K_NORM], "k_norm_w": p[:, SM_K_NORM:SM_NORM2],
            "norm2_w": p[:, SM_NORM2:SM_TOTAL]}

    w_small = pack_small(b_ada, norm1_w, conv_b, dt_bias, a_log, d_skip, ssd_norm_w, q_norm_w, k_norm_w, norm2_w)
    m_small = pack_small(m_b_ada, m_norm1_w, m_conv_b, m_dt_bias, m_a_log, m_d_skip, m_ssd_norm_w, m_q_norm_w, m_k_norm_w, m_norm2_w)
    v_small = pack_small(v_b_ada, v_norm1_w, v_conv_b, v_dt_bias, v_a_log, v_d_skip, v_ssd_norm_w, v_q_norm_w, v_k_norm_w, v_norm2_w)
    small_out = [unpack_small(t) for t in (gsum,) + tuple(_adamw(w_small, gsum, m_small, v_small, "adamw_small"))]

    sharded = {
        "w_ada": (w_ada[0], g_ada, m_w_ada[0], v_w_ada[0]),
        "w_in": (w_in[0], g_in, m_w_in[0], v_w_in[0]),
        "conv_w": (conv_w[0], g_conv_w, m_conv_w[0], v_conv_w[0]),
        "w_out": (w_out[0], g_out, m_w_out[0], v_w_out[0]),
        "w_gate": (w_gate[0], g_gate, m_w_gate[0], v_w_gate[0]),
        "w_up": (w_up[0], g_up, m_w_up[0], v_w_up[0]),
        "w_down": (w_down[0], g_down, m_w_down[0], v_w_down[0]),
    }
    sharded_out = {n: (t[1],) + tuple(_adamw(*t, "adamw_" + n)) for n, t in sharded.items()}

    names = ["w_ada", "b_ada", "norm1_w", "w_in", "conv_w", "conv_b", "dt_bias", "a_log", "d_skip", "ssd_norm_w", "q_norm_w",
             "k_norm_w", "w_out", "norm2_w", "w_gate", "w_up", "w_down"]
    outs = [loss, grad_x[None]]
    for kind in range(4):
        for n in names:
            outs.append(sharded_out[n][kind][None] if n in sharded_out else small_out[kind][n])
    return tuple(outs)
```

```python
import functools

import jax
import jax.numpy as jnp
from jax import lax
from jax.experimental import pallas as pl
from jax.experimental.pallas import tpu as pltpu

F32 = jnp.float32
BF16 = jnp.bfloat16
MXU_DTYPE = jnp.bfloat16
HIGHEST = lax.Precision.HIGHEST
MESH_IDS = pl.DeviceIdType.MESH

N_DEV = 8
D_MODEL = 1024
HEAD_DIM = 64
N_HEADS = 16
D_SSD = 1024
D_SB = 1024
SSD_GROUPS = 2
SSD_STATE = 128
GROUP_WIDTH = D_SSD // SSD_GROUPS
D_CONV = D_SSD + 2 * SSD_GROUPS * SSD_STATE
CONV_WIDTH = 4
CHUNK = 128
D_FF = 2816
N_MOD = 6
EPS = 1e-6
D_IN_PROJ = 5648
O_XBC = 0
O_DT = D_CONV
DT_PAD = 512
O_Z = O_DT + DT_PAD
O_Q = O_Z + D_SSD
D_IN_PAD = O_Q + 3 * D_SB
HALO = 8

ADAM_LR = 0.001
ADAM_B1 = 0.9
ADAM_B2 = 0.999
ADAM_EPS = 1e-08
ADAM_WD = 0.01
ADAM_STEP = 10

ROWS_W_IN = 706
ROWS_W_OUT = 256
ROWS_FF = 352
PACK_ROWS = 2048
SM_B_ADA = 0
SM_NORM1 = 6144
SM_CONV_B = 7168
SM_CONV_W = 8704
SM_DT_BIAS = 14848
SM_A_LOG = 14864
SM_D_SKIP = 14880
SM_SSD_NORM = 14976
SM_Q_NORM = 16000
SM_K_NORM = 16064
SM_NORM2 = 16128
SM_TOTAL = 17152


def _pick(n, cap, mult):
    if n <= cap:
        return n
    best = None
    for t in range(mult, cap + 1, mult):
        if n % t == 0:
            best = t
    assert best is not None, (n, cap, mult)
    return best


def _dot(a, b, precision=None):
    return jnp.dot(a, b, preferred_element_type=F32, precision=precision)


def _dot_nt(a, b, precision=None):
    return lax.dot_general(a, b, (((1,), (1,)), ((), ())), preferred_element_type=F32, precision=precision)


def _dot_tn(a, b, precision=None):
    return lax.dot_general(a, b, (((0,), (0,)), ((), ())), preferred_element_type=F32, precision=precision)


def _softplus(v):
    return jnp.maximum(v, 0.0) + jnp.log1p(jnp.exp(-jnp.abs(v)))


def _sigmoid(v):
    return jax.nn.sigmoid(v)


def _colsum(v):
    return jnp.sum(v, axis=0, keepdims=True)


def _split_dot(v, tri2):
    hi = v.astype(BF16)
    lo = (v - hi.astype(F32)).astype(BF16)
    return _dot(jnp.concatenate([hi, lo], axis=1), tri2)


def _position():
    x, y, c = lax.axis_index("x"), lax.axis_index("y"), lax.axis_index("c")
    return x, y, c


def _peer(x, y, c, k):
    px = 1 - x if (k >> 2) & 1 else x
    py = 1 - y if (k >> 1) & 1 else y
    pc = 1 - c if k & 1 else c
    return px, py, pc


def _all_gather_small(v, name):
    n = v.shape[1]

    def body(v_ref, out_ref, send_sems, recv_sems, local_sem):
        x, y, c = _position()
        me = 4 * x + 2 * y + c
        mine = pltpu.make_async_copy(v_ref, out_ref.at[me], local_sem)
        mine.start()
        sends = []
        for k in range(1, N_DEV):
            cp = pltpu.make_async_remote_copy(
                src_ref=v_ref, dst_ref=out_ref.at[me], send_sem=send_sems.at[k - 1], recv_sem=recv_sems.at[k - 1],
                device_id=_peer(x, y, c, k), device_id_type=MESH_IDS)
            cp.start()
            sends.append(cp)
        for k in range(1, N_DEV):
            px, py, pc = _peer(x, y, c, k)
            pltpu.make_async_remote_copy(
                src_ref=v_ref, dst_ref=out_ref.at[4 * px + 2 * py + pc], send_sem=send_sems.at[k - 1],
                recv_sem=recv_sems.at[k - 1], device_id=(px, py, pc), device_id_type=MESH_IDS).wait_recv()
        for cp in sends:
            cp.wait_send()
        mine.wait()

    return pl.pallas_call(
        body, name=name,
        out_shape=jax.ShapeDtypeStruct((N_DEV, 1, n), v.dtype),
        in_specs=[pl.BlockSpec(memory_space=pltpu.VMEM)],
        out_specs=pl.BlockSpec(memory_space=pltpu.VMEM),
        scratch_shapes=[pltpu.SemaphoreType.DMA((N_DEV - 1,)), pltpu.SemaphoreType.DMA((N_DEV - 1,)),
                        pltpu.SemaphoreType.DMA],
    )(v)


def _all_gather_big(block, name):
    r, n = block.shape

    def body(b_ref, out_ref, send_sems, recv_sems, local_sem):
        x, y, c = _position()
        me, sibling = (x, y, c), (x, y, 1 - c)
        chips = [(1 - x, y), (x, 1 - y), (1 - x, 1 - y)]

        def slot(px, py, pc):
            return out_ref.at[4 * px + 2 * py + pc]

        def copy(k, blk, to, src=None):
            return pltpu.make_async_remote_copy(
                src_ref=slot(*blk) if src is None else src, dst_ref=slot(*blk),
                send_sem=send_sems.at[k], recv_sem=recv_sems.at[k], device_id=to, device_id_type=MESH_IDS)

        mine = pltpu.make_async_copy(b_ref, slot(*me), local_sem)
        mine.start()
        first = [copy(0, me, sibling, src=b_ref)]
        first += [copy(1 + j, me, (*chip, c), src=b_ref) for j, chip in enumerate(chips)]
        for cp in first:
            cp.start()
        passed = [copy(4 + j, (*chip, c), sibling) for j, chip in enumerate(chips)]
        for j, chip in enumerate(chips):
            copy(1 + j, (*chip, c), me).wait_recv()
            passed[j].start()
        copy(0, sibling, me).wait_recv()
        for j, chip in enumerate(chips):
            copy(4 + j, (*chip, 1 - c), me).wait_recv()
        for cp in first + passed:
            cp.wait_send()
        mine.wait()

    return pl.pallas_call(
        body, name=name,
        out_shape=jax.ShapeDtypeStruct((N_DEV, r, n), block.dtype),
        in_specs=[pl.BlockSpec(memory_space=pl.ANY)],
        out_specs=pl.BlockSpec(memory_space=pl.ANY),
        scratch_shapes=[pltpu.SemaphoreType.DMA((N_DEV - 1,)), pltpu.SemaphoreType.DMA((N_DEV - 1,)),
                        pltpu.SemaphoreType.DMA],
    )(block)


def _all_to_all_big(blocks, name):
    _, r, n = blocks.shape

    def body(b_ref, out_ref, send_sems, recv_sems, local_sem):
        x, y, c = _position()
        me = 4 * x + 2 * y + c
        mine = pltpu.make_async_copy(b_ref.at[me], out_ref.at[me], local_sem)
        mine.start()
        sends = []
        for k in range(1, N_DEV):
            px, py, pc = _peer(x, y, c, k)
            cp = pltpu.make_async_remote_copy(
                src_ref=b_ref.at[4 * px + 2 * py + pc], dst_ref=out_ref.at[me], send_sem=send_sems.at[k - 1],
                recv_sem=recv_sems.at[k - 1], device_id=(px, py, pc), device_id_type=MESH_IDS)
            cp.start()
            sends.append(cp)
        for k in range(1, N_DEV):
            px, py, pc = _peer(x, y, c, k)
            pltpu.make_async_remote_copy(
                src_ref=b_ref.at[me], dst_ref=out_ref.at[4 * px + 2 * py + pc], send_sem=send_sems.at[k - 1],
                recv_sem=recv_sems.at[k - 1], device_id=(px, py, pc), device_id_type=MESH_IDS).wait_recv()
        for cp in sends:
            cp.wait_send()
        mine.wait()

    return pl.pallas_call(
        body, name=name,
        out_shape=jax.ShapeDtypeStruct(blocks.shape, blocks.dtype),
        in_specs=[pl.BlockSpec(memory_space=pl.ANY)],
        out_specs=pl.BlockSpec(memory_space=pl.ANY),
        scratch_shapes=[pltpu.SemaphoreType.DMA((N_DEV - 1,)), pltpu.SemaphoreType.DMA((N_DEV - 1,)),
                        pltpu.SemaphoreType.DMA],
    )(blocks)


def _sum_slots(slots, name):
    _, r, n = slots.shape
    tr = _pick(r, 256, 16)

    def body(s_ref, o_ref):
        acc = s_ref[0].astype(F32)
        for d in range(1, N_DEV):
            acc = acc + s_ref[d].astype(F32)
        o_ref[...] = acc

    return pl.pallas_call(
        body, name=name, grid=(r // tr,),
        in_specs=[pl.BlockSpec((N_DEV, tr, n), lambda i: (0, i, 0))],
        out_specs=pl.BlockSpec((tr, n), lambda i: (i, 0)),
        out_shape=jax.ShapeDtypeStruct((r, n), F32),
    )(slots)


def _matmul(a, b, name, out_dtype=F32):
    m, k = a.shape
    _, n = b.shape
    tm, tn, tk = _pick(m, 512, 16), _pick(n, 1408, 128), _pick(k, 1408, 128)
    nk = k // tk

    def body(a_ref, b_ref, o_ref, acc_ref):
        kk = pl.program_id(2)

        @pl.when(kk == 0)
        def _():
            acc_ref[...] = jnp.zeros_like(acc_ref)

        acc_ref[...] += _dot(a_ref[...].astype(MXU_DTYPE), b_ref[...].astype(MXU_DTYPE))

        @pl.when(kk == nk - 1)
        def _():
            o_ref[...] = acc_ref[...].astype(o_ref.dtype)

    return pl.pallas_call(
        body, name=name, grid=(m // tm, n // tn, nk),
        in_specs=[pl.BlockSpec((tm, tk), lambda i, j, kk: (i, kk)), pl.BlockSpec((tk, tn), lambda i, j, kk: (kk, j))],
        out_specs=pl.BlockSpec((tm, tn), lambda i, j, kk: (i, j)),
        out_shape=jax.ShapeDtypeStruct((m, n), out_dtype),
        scratch_shapes=[pltpu.VMEM((tm, tn), F32)],
    )(a, b)


def _matmul_tn(a, b, name):
    l, m = a.shape
    _, n = b.shape
    tm, tn, tl = _pick(m, 1024, 128), _pick(n, 1408, 128), _pick(l, 512, 16)
    nl = l // tl

    def body(a_ref, b_ref, o_ref):
        ll = pl.program_id(2)

        @pl.when(ll == 0)
        def _():
            o_ref[...] = jnp.zeros_like(o_ref)

        o_ref[...] += _dot_tn(a_ref[...].astype(MXU_DTYPE), b_ref[...].astype(MXU_DTYPE))

    return pl.pallas_call(
        body, name=name, grid=(m // tm, n // tn, nl),
        in_specs=[pl.BlockSpec((tl, tm), lambda i, j, ll: (ll, i)), pl.BlockSpec((tl, tn), lambda i, j, ll: (ll, j))],
        out_specs=pl.BlockSpec((tm, tn), lambda i, j, ll: (i, j)),
        out_shape=jax.ShapeDtypeStruct((m, n), F32),
    )(a, b)


def _row_specs(l, d, n_rows, n_vecs):
    tl = _pick(l, 512, 16)
    row = pl.BlockSpec((tl, d), lambda i: (i, 0))
    vec = pl.BlockSpec((1, d), lambda i: (0, 0))
    return tl, row, vec, [row] * n_rows + [vec] * n_vecs


def _rms_mod(x, nw, sc, sh, name):
    l, d = x.shape
    tl, row, _, in_specs = _row_specs(l, d, 1, 3)

    def body(x_ref, nw_ref, sc_ref, sh_ref, h_ref):
        xv = x_ref[...]
        r = lax.rsqrt(jnp.mean(xv * xv, axis=-1, keepdims=True) + EPS)
        h_ref[...] = (((xv * r) * nw_ref[...]) * (1.0 + sc_ref[...]) + sh_ref[...]).astype(h_ref.dtype)

    return pl.pallas_call(body, name=name, grid=(l // tl,), in_specs=in_specs, out_specs=row,
                          out_shape=jax.ShapeDtypeStruct((l, d), MXU_DTYPE))(x, nw, sc, sh)


def _residual_rms_mod(x, mix, g, nw, sc, sh, name):
    l, d = x.shape
    tl, row, _, in_specs = _row_specs(l, d, 2, 4)

    def body(x_ref, mix_ref, g_ref, nw_ref, sc_ref, sh_ref, x1_ref, h_ref):
        xv = x_ref[...] + g_ref[...] * mix_ref[...]
        x1_ref[...] = xv
        r = lax.rsqrt(jnp.mean(xv * xv, axis=-1, keepdims=True) + EPS)
        h_ref[...] = (((xv * r) * nw_ref[...]) * (1.0 + sc_ref[...]) + sh_ref[...]).astype(h_ref.dtype)

    return pl.pallas_call(body, name=name, grid=(l // tl,), in_specs=in_specs, out_specs=[row, row],
                          out_shape=[jax.ShapeDtypeStruct((l, d), F32), jax.ShapeDtypeStruct((l, d), MXU_DTYPE)],
                          )(x, mix, g, nw, sc, sh)


def _norm_bwd(x, dh, dres, nw, sc, name):
    l, d = x.shape
    tl, row, _, in_specs = _row_specs(l, d, 3, 2)

    def body(x_ref, dh_ref, dres_ref, nw_ref, sc_ref, dx_ref, st_ref):
        xv, dh_v = x_ref[...], dh_ref[...]
        r = lax.rsqrt(jnp.mean(xv * xv, axis=-1, keepdims=True) + EPS)
        xn = xv * r
        dxn = dh_v * (nw_ref[...] * (1.0 + sc_ref[...]))
        dx_ref[...] = dres_ref[...] + r * (dxn - xn * jnp.mean(dxn * xn, axis=-1, keepdims=True))

        @pl.when(pl.program_id(0) == 0)
        def _():
            st_ref[...] = jnp.zeros_like(st_ref)

        dhx = dh_v * xn
        st_ref[0:1, :] += _colsum(dhx * nw_ref[...])
        st_ref[1:2, :] += _colsum(dh_v)
        st_ref[2:3, :] += _colsum(dhx * (1.0 + sc_ref[...]))

    return pl.pallas_call(body, name=name, grid=(l // tl,), in_specs=in_specs,
                          out_specs=[row, pl.BlockSpec((8, d), lambda i: (0, 0))],
                          out_shape=[jax.ShapeDtypeStruct((l, d), F32), jax.ShapeDtypeStruct((8, d), F32)],
                          )(x, dh, dres, nw, sc)


def _gate_bwd(dres, val, g, name):
    l, d = dres.shape
    tl, row, _, in_specs = _row_specs(l, d, 2, 1)

    def body(dres_ref, val_ref, g_ref, dval_ref, st_ref):
        dr = dres_ref[...]
        dval_ref[...] = (g_ref[...] * dr).astype(dval_ref.dtype)

        @pl.when(pl.program_id(0) == 0)
        def _():
            st_ref[...] = jnp.zeros_like(st_ref)

        st_ref[0:1, :] += _colsum(dr * val_ref[...])

    return pl.pallas_call(body, name=name, grid=(l // tl,), in_specs=in_specs,
                          out_specs=[row, pl.BlockSpec((8, d), lambda i: (0, 0))],
                          out_shape=[jax.ShapeDtypeStruct((l, d), MXU_DTYPE), jax.ShapeDtypeStruct((8, d), F32)],
                          )(dres, val, g)


def _loss_head(x1, f, g, target, name):
    l, d = x1.shape
    tl, row, _, _ = _row_specs(l, d, 0, 0)
    vec = pl.BlockSpec((1, d), lambda i: (0, 0))

    def body(x1_ref, f_ref, g_ref, t_ref, dy_ref, loss_ref):
        e = x1_ref[...] + g_ref[...] * f_ref[...] - t_ref[...]
        dy_ref[...] = e * (1.0 / d)

        @pl.when(pl.program_id(0) == 0)
        def _():
            loss_ref[...] = jnp.zeros_like(loss_ref)

        s = jnp.sum(jnp.sum(e * e, axis=1, keepdims=True), axis=0, keepdims=True)
        loss_ref[...] += (0.5 / d) * s

    return pl.pallas_call(body, name=name, grid=(l // tl,), in_specs=[row, row, vec, row],
                          out_specs=[row, pl.BlockSpec((8, 128), lambda i: (0, 0))],
                          out_shape=[jax.ShapeDtypeStruct((l, d), F32), jax.ShapeDtypeStruct((8, 128), F32)],
                          )(x1, f, g, target)


def _swiglu(gate, up, name):
    l, f = gate.shape
    tl, tf = _pick(l, 512, 16), _pick(f, 1408, 128)
    spec = pl.BlockSpec((tl, tf), lambda i, j: (i, j))

    def body(g_ref, u_ref, a_ref):
        gv = g_ref[...]
        a_ref[...] = (gv * _sigmoid(gv) * u_ref[...]).astype(a_ref.dtype)

    return pl.pallas_call(body, name=name, grid=(l // tl, f // tf), in_specs=[spec, spec], out_specs=spec,
                          out_shape=jax.ShapeDtypeStruct((l, f), MXU_DTYPE))(gate, up)


def _swiglu_bwd(gate, up, da, name):
    l, f = gate.shape
    tl, tf = _pick(l, 512, 16), _pick(f, 1408, 128)
    spec = pl.BlockSpec((tl, tf), lambda i, j: (i, j))

    def body(g_ref, u_ref, da_ref, dg_ref, du_ref):
        gv, dav = g_ref[...], da_ref[...]
        s = _sigmoid(gv)
        dg_ref[...] = (dav * u_ref[...] * (s * (1.0 + gv * (1.0 - s)))).astype(dg_ref.dtype)
        du_ref[...] = (dav * (gv * s)).astype(du_ref.dtype)

    return pl.pallas_call(body, name=name, grid=(l // tl, f // tf), in_specs=[spec, spec, spec], out_specs=[spec, spec],
                          out_shape=[jax.ShapeDtypeStruct((l, f), MXU_DTYPE)] * 2)(gate, up, da)


def _conv_tile(l):
    return _pick(l, 512, 16)


def _conv_pre(buf, w_ref, b_ref, tl):
    acc = b_ref[...] + w_ref[3:4, :] * buf[HALO:HALO + tl, :]
    for k in range(CONV_WIDTH - 1):
        s = HALO - (CONV_WIDTH - 1) + k
        acc = acc + w_ref[k:k + 1, :] * buf[s:s + tl, :]
    return acc


def _fill_past(buf, u_ref, halo_ref, tl):
    i = pl.program_id(0)

    @pl.when(i == 0)
    def _():
        buf[0:HALO, :] = jnp.zeros((HALO, buf.shape[1]), F32)

    @pl.when(i > 0)
    def _():
        buf[0:HALO, :] = halo_ref[...]

    buf[HALO:HALO + tl, :] = u_ref[...]


def _conv_silu(u, w, b, name):
    l, ch = u.shape[0], w.shape[1]
    tl = _conv_tile(l)
    per = tl // HALO
    cur = pl.BlockSpec((tl, ch), lambda i: (i, 0))
    past = pl.BlockSpec((HALO, ch), lambda i: (jnp.maximum(i * per - 1, 0), 0))

    def body(u_ref, halo_ref, w_ref, b_ref, o_ref, buf):
        _fill_past(buf, u_ref, halo_ref, tl)
        pre = _conv_pre(buf, w_ref, b_ref, tl)
        o_ref[...] = pre * _sigmoid(pre)

    return pl.pallas_call(body, name=name, grid=(l // tl,),
                          in_specs=[cur, past, pl.BlockSpec((CONV_WIDTH, ch), lambda i: (0, 0)), pl.BlockSpec((1, ch), lambda i: (0, 0))],
                          out_specs=cur, out_shape=jax.ShapeDtypeStruct((l, ch), F32),
                          scratch_shapes=[pltpu.VMEM((tl + HALO, ch), F32)])(u, u, w, b)


def _conv_silu_bwd_pre(u, dxc, w, b, name):
    l, ch = u.shape[0], w.shape[1]
    tl = _conv_tile(l)
    per = tl // HALO
    cur = pl.BlockSpec((tl, ch), lambda i: (i, 0))
    past = pl.BlockSpec((HALO, ch), lambda i: (jnp.maximum(i * per - 1, 0), 0))

    def body(u_ref, halo_ref, d_ref, w_ref, b_ref, dpre_ref, st_ref, buf):
        _fill_past(buf, u_ref, halo_ref, tl)
        pre = _conv_pre(buf, w_ref, b_ref, tl)
        s = _sigmoid(pre)
        dpre = d_ref[...] * (s * (1.0 + pre * (1.0 - s)))
        dpre_ref[...] = dpre

        @pl.when(pl.program_id(0) == 0)
        def _():
            st_ref[...] = jnp.zeros_like(st_ref)

        for k in range(CONV_WIDTH):
            s0 = HALO - (CONV_WIDTH - 1) + k
            st_ref[k:k + 1, :] += _colsum(dpre * buf[s0:s0 + tl, :])
        st_ref[CONV_WIDTH:CONV_WIDTH + 1, :] += _colsum(dpre)

    return pl.pallas_call(body, name=name, grid=(l // tl,),
                          in_specs=[cur, past, cur, pl.BlockSpec((CONV_WIDTH, ch), lambda i: (0, 0)), pl.BlockSpec((1, ch), lambda i: (0, 0))],
                          out_specs=[cur, pl.BlockSpec((8, ch), lambda i: (0, 0))],
                          out_shape=[jax.ShapeDtypeStruct((l, ch), F32), jax.ShapeDtypeStruct((8, ch), F32)],
                          scratch_shapes=[pltpu.VMEM((tl + HALO, ch), F32)])(u, u, dxc, w, b)


def _conv_bwd_input(dpre, w, name):
    l, ch = dpre.shape
    tl = _conv_tile(l)
    per = tl // HALO
    nt = l // tl
    cur = pl.BlockSpec((tl, ch), lambda i: (i, 0))
    nxt = pl.BlockSpec((HALO, ch), lambda i: (jnp.minimum((i + 1) * per, l // HALO - 1), 0))

    def body(d_ref, halo_ref, w_ref, du_ref, buf):
        i = pl.program_id(0)
        buf[0:tl, :] = d_ref[...]

        @pl.when(i == nt - 1)
        def _():
            buf[tl:tl + HALO, :] = jnp.zeros((HALO, ch), F32)

        @pl.when(i < nt - 1)
        def _():
            buf[tl:tl + HALO, :] = halo_ref[...]

        acc = w_ref[3:4, :] * buf[0:tl, :]
        for k in range(CONV_WIDTH - 1):
            s = CONV_WIDTH - 1 - k
            acc = acc + w_ref[k:k + 1, :] * buf[s:s + tl, :]
        du_ref[...] = acc

    return pl.pallas_call(body, name=name, grid=(nt,),
                          in_specs=[cur, nxt, pl.BlockSpec((CONV_WIDTH, ch), lambda i: (0, 0))],
                          out_specs=cur, out_shape=jax.ShapeDtypeStruct((l, ch), F32),
                          scratch_shapes=[pltpu.VMEM((tl + HALO, ch), F32)])(dpre, dpre, w)


HEAD_TILE = 128
D_HEAD_TILES = N_HEADS * HEAD_TILE


def _chunk_iota():
    r = lax.broadcasted_iota(jnp.int32, (CHUNK, CHUNK), 0)
    c = lax.broadcasted_iota(jnp.int32, (CHUNK, CHUNK), 1)
    return r, c


def _split3_dot(v, ones_b):
    hi = v.astype(BF16)
    r1 = v - hi.astype(F32)
    mid = r1.astype(BF16)
    lo = (r1 - mid.astype(F32)).astype(BF16)
    return _dot(hi, ones_b) + _dot(mid, ones_b) + _dot(lo, ones_b)


def _ssd_decays(dtx_ref, dty_ref, dtbx_ref, dtby_ref, alx_ref, aly_ref, r, c):
    tri = (r >= c).astype(F32)
    dt_x = _softplus(dtx_ref[...] + dtbx_ref[...])
    a_x = -jnp.exp(alx_ref[...])
    adt_x = a_x * dt_x
    acs_x = _dot(tri, adt_x, HIGHEST)
    atot_x = _colsum(adt_x)
    acs_y = _dot(tri, -jnp.exp(aly_ref[...]) * _softplus(dty_ref[...] + dtby_ref[...]), HIGHEST)
    return dt_x, a_x, acs_y, jnp.exp(acs_x), jnp.exp(atot_x - acs_x), jnp.exp(atot_x)


def _head_decay(acs_y, e, r, c):
    col = acs_y[:, e * HEAD_TILE:(e + 1) * HEAD_TILE]
    return jnp.where(r >= c, jnp.exp(col - col.T), 0.0)


def _half_masks():
    lane = lax.broadcasted_iota(jnp.int32, (CHUNK, 2 * HEAD_DIM), 1)
    return lane < HEAD_DIM, lane >= HEAD_DIM


def _ssd_specs(nc, reverse):
    def at(i):
        return nc - 1 - i if reverse else i
    xc = pl.BlockSpec((CHUNK, D_CONV), lambda i: (at(i), 0))
    wide = pl.BlockSpec((CHUNK, D_SSD), lambda i: (at(i), 0))
    wide_y = pl.BlockSpec((CHUNK, D_HEAD_TILES), lambda i: (at(i), 0))
    vec = pl.BlockSpec((1, D_SSD), lambda i: (0, 0))
    vec_y = pl.BlockSpec((1, D_HEAD_TILES), lambda i: (0, 0))
    state = pl.BlockSpec((1, SSD_STATE, D_SSD), lambda i: (at(i), 0, 0))
    return xc, wide, wide_y, vec, vec_y, state


def _ssd_fwd(xc, dtr_x, dtr_y, dtb_x, dtb_y, al_x, al_y, name):
    l = xc.shape[0]
    nc = l // CHUNK
    xc_s, wide_s, wide_y_s, vec_s, vec_y_s, state_s = _ssd_specs(nc, False)
    pairs_per_group = GROUP_WIDTH // (2 * HEAD_DIM)

    def body(xc_ref, dtx_ref, dty_ref, dtbx_ref, dtby_ref, alx_ref, aly_ref, y_ref, sp_ref, state):
        @pl.when(pl.program_id(0) == 0)
        def _():
            state[...] = jnp.zeros_like(state)

        r, c = _chunk_iota()
        halves = _half_masks()
        dt_x, _, acs_y, ea_x, ds_x, eatot_x = _ssd_decays(dtx_ref, dty_ref, dtbx_ref, dtby_ref, alx_ref, aly_ref, r, c)
        xg = xc_ref[:, 0:D_SSD] * dt_x
        sp_ref[0] = state[...]
        for g in range(SSD_GROUPS):
            lanes = slice(g * GROUP_WIDTH, (g + 1) * GROUP_WIDTH)
            bb = xc_ref[:, D_SSD + g * SSD_STATE:D_SSD + (g + 1) * SSD_STATE].astype(MXU_DTYPE)
            cb = xc_ref[:, D_SSD + (SSD_GROUPS + g) * SSD_STATE:D_SSD + (SSD_GROUPS + g + 1) * SSD_STATE].astype(MXU_DTYPE)
            scores = _dot_nt(cb, bb)
            sg = state[:, lanes]
            ys = []
            for j in range(g * pairs_per_group, (g + 1) * pairs_per_group):
                xg_pair = xg[:, j * 2 * HEAD_DIM:(j + 1) * 2 * HEAD_DIM]
                acc = jnp.zeros((CHUNK, 2 * HEAD_DIM), F32)
                for half in range(2):
                    m = (scores * _head_decay(acs_y, 2 * j + half, r, c)).astype(MXU_DTYPE)
                    acc = acc + _dot(m, jnp.where(halves[half], xg_pair, 0.0).astype(MXU_DTYPE))
                ys.append(acc)
            y_ref[:, lanes] = jnp.concatenate(ys, axis=1) + _dot(cb, sg.astype(MXU_DTYPE)) * ea_x[:, lanes]
            state[:, lanes] = sg * eatot_x[:, lanes] + _dot_tn(bb, (xg[:, lanes] * ds_x[:, lanes]).astype(MXU_DTYPE))

    return pl.pallas_call(
        body, name=name, grid=(nc,),
        in_specs=[xc_s, wide_s, wide_y_s, vec_s, vec_y_s, vec_s, vec_y_s],
        out_specs=[wide_s, state_s],
        out_shape=[jax.ShapeDtypeStruct((l, D_SSD), F32), jax.ShapeDtypeStruct((nc, SSD_STATE, D_SSD), F32)],
        scratch_shapes=[pltpu.VMEM((SSD_STATE, D_SSD), F32)],
    )(xc, dtr_x, dtr_y, dtb_x, dtb_y, al_x, al_y)


def _ssd_bwd(xc, dtr_x, dtr_y, dtb_x, dtb_y, al_x, al_y, dy, states, head_ones, name):
    l = xc.shape[0]
    nc = l // CHUNK
    xc_s, wide_s, wide_y_s, vec_s, vec_y_s, state_s = _ssd_specs(nc, True)
    pairs_per_group = GROUP_WIDTH // (2 * HEAD_DIM)

    def body(xc_ref, dtx_ref, dty_ref, dtbx_ref, dtby_ref, alx_ref, aly_ref, dy_ref, sp_ref, ones_ref,
             dxc_ref, ddtr_ref, st_ref, hs_ref, dstate):
        @pl.when(pl.program_id(0) == 0)
        def _():
            dstate[...] = jnp.zeros_like(dstate)
            st_ref[...] = jnp.zeros_like(st_ref)

        r, c = _chunk_iota()
        halves = _half_masks()
        ones_b = ones_ref[...]
        dt_x, a_x, acs_y, ea_x, ds_x, eatot_x = _ssd_decays(dtx_ref, dty_ref, dtbx_ref, dtby_ref, alx_ref, aly_ref, r, c)
        xs = xc_ref[:, 0:D_SSD]
        xg = xs * dt_x
        gy = dy_ref[...]
        s_prev = sp_ref[0]
        gea = gy * ea_x
        xds = xg * ds_x
        ds_old = dstate[...]
        later2 = jnp.concatenate([(r <= c).astype(BF16)] * 2, axis=1)
        dxg_parts, state_term, yoff_parts, dadt_parts = [], [], [], []
        for g in range(SSD_GROUPS):
            lanes = slice(g * GROUP_WIDTH, (g + 1) * GROUP_WIDTH)
            b_lo = D_SSD + g * SSD_STATE
            c_lo = D_SSD + (SSD_GROUPS + g) * SSD_STATE
            bb = xc_ref[:, b_lo:b_lo + SSD_STATE].astype(MXU_DTYPE)
            cb = xc_ref[:, c_lo:c_lo + SSD_STATE].astype(MXU_DTYPE)
            scores = _dot_nt(cb, bb)
            dsg = ds_old[:, lanes].astype(MXU_DTYPE)
            gea_b = gea[:, lanes].astype(MXU_DTYPE)
            xds_b = xds[:, lanes].astype(MXU_DTYPE)
            dxg_state = _dot(bb, dsg) * ds_x[:, lanes]
            dc = _dot_nt(gea_b, s_prev[:, lanes].astype(MXU_DTYPE))
            db = _dot_nt(xds_b, dsg)
            dscores = jnp.zeros((CHUNK, CHUNK), F32)
            diag = []
            for j in range(g * pairs_per_group, (g + 1) * pairs_per_group):
                pair = slice(j * 2 * HEAD_DIM, (j + 1) * 2 * HEAD_DIM)
                xg_pair = xg[:, pair].astype(MXU_DTYPE)
                acc = jnp.zeros((CHUNK, 2 * HEAD_DIM), F32)
                cols = []
                for half in range(2):
                    decay = _head_decay(acs_y, 2 * j + half, r, c)
                    g_e = jnp.where(halves[half], gy[:, pair], 0.0).astype(MXU_DTYPE)
                    acc = acc + _dot_tn((scores * decay).astype(MXU_DTYPE), g_e)
                    dm = _dot_nt(g_e, xg_pair) * decay
                    dscores = dscores + dm
                    wq = dm * scores
                    hi = wq.astype(BF16)
                    lo = (wq - hi.astype(F32)).astype(BF16)
                    later = _dot(later2, jnp.concatenate([hi, lo], axis=0))
                    cols.append(jnp.sum(jnp.where(c < r, later, 0.0), axis=1, keepdims=True))
                diag.append(acc)
                dadt_parts.append(jnp.where(halves[0], cols[0], cols[1]))
            dsc_b = dscores.astype(MXU_DTYPE)
            dc = dc + _dot(dsc_b, bb)
            db = db + _dot_tn(dsc_b, cb)
            dxc_ref[:, b_lo:b_lo + SSD_STATE] = db
            dxc_ref[:, c_lo:c_lo + SSD_STATE] = dc
            dxg_parts.append(jnp.concatenate(diag, axis=1) + dxg_state)
            state_term.append(dxg_state)
            yoff_parts.append(_dot(cb, s_prev[:, lanes].astype(MXU_DTYPE)) * ea_x[:, lanes])
            dstate[:, lanes] = ds_old[:, lanes] * eatot_x[:, lanes] + _dot_tn(cb, gea_b)
        dxg = jnp.concatenate(dxg_parts, axis=1)
        dxc_ref[:, 0:D_SSD] = dxg * dt_x
        through_out = _dot((r <= c).astype(F32), gy * jnp.concatenate(yoff_parts, axis=1), HIGHEST)
        through_in = _dot((c < r).astype(F32), xg * jnp.concatenate(state_term, axis=1), HIGHEST)
        carried = jnp.broadcast_to(_colsum(ds_old * s_prev) * eatot_x, (8, D_SSD))
        dadt = (jnp.concatenate(dadt_parts, axis=1) + _split3_dot(through_out + through_in, ones_b)
                + jnp.max(_split3_dot(carried, ones_b), axis=0, keepdims=True))
        ddt = a_x * dadt + _split3_dot(dxg * xs, ones_b)
        draw = ddt * _sigmoid(dtx_ref[...] + dtbx_ref[...])
        ddtr_ref[...] = draw
        st_ref[0:1, :] += _colsum(dt_x * dadt) * a_x
        st_ref[1:2, :] += _colsum(draw)
        st_ref[2:3, :] += _colsum(gy * xs)

        @pl.when(pl.program_id(0) == nc - 1)
        def _():
            hs_ref[...] = _split3_dot(st_ref[...], ones_b)

    stats = pl.BlockSpec((8, D_SSD), lambda i: (0, 0))
    return pl.pallas_call(
        body, name=name, grid=(nc,),
        in_specs=[xc_s, wide_s, wide_y_s, vec_s, vec_y_s, vec_s, vec_y_s, wide_s, state_s,
                  pl.BlockSpec((D_SSD, D_SSD), lambda i: (0, 0))],
        out_specs=[xc_s, wide_s, stats, stats],
        out_shape=[jax.ShapeDtypeStruct((l, D_CONV), F32), jax.ShapeDtypeStruct((l, D_SSD), F32),
                   jax.ShapeDtypeStruct((8, D_SSD), F32), jax.ShapeDtypeStruct((8, D_SSD), F32)],
        scratch_shapes=[pltpu.VMEM((SSD_STATE, D_SSD), F32)],
    )(xc, dtr_x, dtr_y, dtb_x, dtb_y, al_x, al_y, dy, states, head_ones)


def _ssd_gate_norm(ycore, xc, z, dskip_x, norm_w, name):
    l = ycore.shape[0]
    tl = _pick(l, 512, 16)
    row = pl.BlockSpec((tl, D_SSD), lambda i: (i, 0))
    vec = pl.BlockSpec((1, D_SSD), lambda i: (0, 0))

    def body(y_ref, xs_ref, z_ref, dk_ref, nw_ref, o_ref):
        zv = z_ref[...]
        yv = (y_ref[...] + dk_ref[...] * xs_ref[...]) * (zv * _sigmoid(zv))
        for g in range(SSD_GROUPS):
            lanes = slice(g * GROUP_WIDTH, (g + 1) * GROUP_WIDTH)
            yg = yv[:, lanes]
            rg = lax.rsqrt(jnp.mean(yg * yg, axis=-1, keepdims=True) + EPS)
            o_ref[:, lanes] = (yg * rg * nw_ref[:, lanes]).astype(o_ref.dtype)

    z_spec = pl.BlockSpec((tl, D_SSD), lambda i: (i, O_Z // D_SSD))
    return pl.pallas_call(body, name=name, grid=(l // tl,), in_specs=[row, row, z_spec, vec, vec], out_specs=row,
                          out_shape=jax.ShapeDtypeStruct((l, D_SSD), MXU_DTYPE))(ycore, xc, z, dskip_x, norm_w)


def _ssd_gate_norm_bwd(dout, ycore, xc, z, dskip_x, norm_w, name):
    l = ycore.shape[0]
    tl = _pick(l, 512, 16)
    nt = l // tl
    row = pl.BlockSpec((tl, D_SSD), lambda i: (i, 0))
    vec = pl.BlockSpec((1, D_SSD), lambda i: (0, 0))

    def body(do_ref, y_ref, xs_ref, z_ref, dk_ref, nw_ref, dyc_ref, dxs_ref, dz_ref, st_ref):
        @pl.when(pl.program_id(0) == 0)
        def _():
            st_ref[...] = jnp.zeros_like(st_ref)

        zv, xs = z_ref[...], xs_ref[...]
        s = _sigmoid(zv)
        gz = zv * s
        yc = y_ref[...] + dk_ref[...] * xs
        yv = yc * gz
        dov = do_ref[...]
        dnw, dyv = [], []
        for g in range(SSD_GROUPS):
            lanes = slice(g * GROUP_WIDTH, (g + 1) * GROUP_WIDTH)
            yg = yv[:, lanes]
            rg = lax.rsqrt(jnp.mean(yg * yg, axis=-1, keepdims=True) + EPS)
            yn = yg * rg
            dnw.append(_colsum(dov[:, lanes] * yn))
            dyn = dov[:, lanes] * nw_ref[:, lanes]
            dyv.append(rg * (dyn - yn * jnp.mean(dyn * yn, axis=-1, keepdims=True)))
        dy = jnp.concatenate(dyv, axis=1)
        dyc = dy * gz
        dyc_ref[...] = dyc
        dxs_ref[...] = dyc * dk_ref[...]
        dz_ref[...] = dy * yc * (s * (1.0 + zv * (1.0 - s)))
        st_ref[0:1, :] += jnp.concatenate(dnw, axis=1)

    return pl.pallas_call(
        body, name=name, grid=(nt,), in_specs=[row, row, row, pl.BlockSpec((tl, D_SSD), lambda i: (i, O_Z // D_SSD)), vec, vec],
        out_specs=[row, row, row, pl.BlockSpec((8, D_SSD), lambda i: (0, 0))],
        out_shape=[jax.ShapeDtypeStruct((l, D_SSD), F32)] * 3 + [jax.ShapeDtypeStruct((8, D_SSD), F32)],
    )(dout, ycore, xc, z, dskip_x, norm_w)


PAIR = 2 * HEAD_DIM
N_PAIRS = N_HEADS // 2
Q_TILE0 = O_Q // PAIR
SB_KEYS = 1024
SB_SCALE = HEAD_DIM ** -0.5


def _pair_sum(v, lo):
    s_lo = jnp.sum(jnp.where(lo, v, 0.0), axis=-1, keepdims=True)
    s_hi = jnp.sum(jnp.where(lo, 0.0, v), axis=-1, keepdims=True)
    return jnp.where(lo, s_lo, s_hi)


def _qkv_prep(proj, qw2, kw2, name):
    l = proj.shape[0]
    tl = _pick(l, 1024, 16)
    out = pl.BlockSpec((tl, PAIR), lambda i, j: (i, j))
    vec = pl.BlockSpec((1, PAIR), lambda i, j: (0, 0))

    def at(which):
        return pl.BlockSpec((tl, PAIR), lambda i, j: (i, Q_TILE0 + which * N_PAIRS + j))

    def body(q_ref, k_ref, v_ref, qw_ref, kw_ref, qn_ref, kn_ref, vb_ref):
        lo = lax.broadcasted_iota(jnp.int32, (tl, PAIR), 1) < HEAD_DIM
        for t_ref, w_ref, o_ref, scale in ((q_ref, qw_ref, qn_ref, SB_SCALE), (k_ref, kw_ref, kn_ref, 1.0)):
            tv = t_ref[...]
            r = lax.rsqrt(_pair_sum(tv * tv, lo) * (1.0 / HEAD_DIM) + EPS)
            o_ref[...] = ((tv * r) * w_ref[...] * scale).astype(o_ref.dtype)
        vb_ref[...] = v_ref[...].astype(vb_ref.dtype)

    return pl.pallas_call(body, name=name, grid=(l // tl, N_PAIRS), in_specs=[at(0), at(1), at(2), vec, vec],
                          out_specs=[out, out, out], out_shape=[jax.ShapeDtypeStruct((l, D_SB), MXU_DTYPE)] * 3,
                          )(proj, proj, proj, qw2, kw2)


def _qk_norm_bwd(proj, dqn, dkn, qw2, kw2, name):
    l = proj.shape[0]
    tl = _pick(l, 1024, 16)
    out = pl.BlockSpec((tl, PAIR), lambda i, j: (i, j))
    vec = pl.BlockSpec((1, PAIR), lambda i, j: (0, 0))
    stats = pl.BlockSpec((8, PAIR), lambda i, j: (0, 0))

    def at(which):
        return pl.BlockSpec((tl, PAIR), lambda i, j: (i, Q_TILE0 + which * N_PAIRS + j))

    def body(q_ref, k_ref, dqn_ref, dkn_ref, qw_ref, kw_ref, dq_ref, dk_ref, stq_ref, stk_ref):
        @pl.when((pl.program_id(0) == 0) & (pl.program_id(1) == 0))
        def _():
            stq_ref[...] = jnp.zeros_like(stq_ref)
            stk_ref[...] = jnp.zeros_like(stk_ref)

        lo = lax.broadcasted_iota(jnp.int32, (tl, PAIR), 1) < HEAD_DIM
        for t_ref, dn_ref, w_ref, d_ref, st_ref, scale in ((q_ref, dqn_ref, qw_ref, dq_ref, stq_ref, SB_SCALE),
                                                           (k_ref, dkn_ref, kw_ref, dk_ref, stk_ref, 1.0)):
            tv = t_ref[...]
            r = lax.rsqrt(_pair_sum(tv * tv, lo) * (1.0 / HEAD_DIM) + EPS)
            tn = tv * r
            dnv = dn_ref[...] * scale
            dtn = dnv * w_ref[...]
            d_ref[...] = r * (dtn - tn * (_pair_sum(dtn * tn, lo) * (1.0 / HEAD_DIM)))
            st_ref[0:1, :] += _colsum(dnv * tn)

    return pl.pallas_call(body, name=name, grid=(l // tl, N_PAIRS), in_specs=[at(0), at(1), out, out, vec, vec],
                          out_specs=[out, out, stats, stats],
                          out_shape=[jax.ShapeDtypeStruct((l, D_SB), F32)] * 2 + [jax.ShapeDtypeStruct((8, PAIR), F32)] * 2,
                          )(proj, proj, dqn, dkn, qw2, kw2)


def _sb_masks():
    r = lax.broadcasted_iota(jnp.int32, (CHUNK, CHUNK), 0)
    c = lax.broadcasted_iota(jnp.int32, (CHUNK, CHUNK), 1)
    return r, c


def _stack2(mask):
    t = mask.astype(BF16)
    return jnp.concatenate([t, t], axis=0)


def _sb_fwd(q, k, v, name):
    l = q.shape[0]
    nq = l // CHUNK
    kt = _pick(l, SB_KEYS, CHUNK)
    sub = kt // CHUNK
    qblk = pl.BlockSpec((CHUNK, PAIR), lambda i, j: (j, i))
    full = pl.BlockSpec((l, PAIR), lambda i, j: (0, i))

    def body(q_ref, k_ref, v_ref, o_ref, tot_ref):
        qb = pl.program_id(1)
        r, c = _sb_masks()
        after2 = _stack2(r > c)
        halves = _half_masks()
        zero = jnp.zeros((CHUNK, PAIR), q_ref.dtype)
        qm = [jnp.where(halves[a], q_ref[...], zero) for a in range(2)]
        key_minus_query = lax.broadcasted_iota(jnp.int32, (CHUNK, kt), 1) - lax.broadcasted_iota(jnp.int32, (CHUNK, kt), 0)
        last = (qb * CHUNK) // kt

        def tile(t, carries, accs, masked):
            rows = pl.ds(pl.multiple_of(t * kt, kt), kt)
            kv, vv = k_ref[rows, :], v_ref[rows, :]
            keep = key_minus_query < qb * CHUNK - t * kt
            new_carries, new_accs = [], []
            for a in range(2):
                lg = _dot_nt(qm[a], kv)
                sp = _softplus(lg)
                lr = jnp.where(keep, -sp, 0.0) if masked else -sp
                offset, parts = carries[a], [None] * sub
                for j in reversed(range(sub)):
                    piece = lr[:, j * CHUNK:(j + 1) * CHUNK]
                    parts[j] = _split_dot(piece, after2) + offset
                    offset = offset + jnp.sum(piece, axis=1, keepdims=True)
                w = jnp.exp(lg - sp + jnp.concatenate(parts, axis=1))
                if masked:
                    w = jnp.where(keep, w, 0.0)
                new_accs.append(accs[a] + _dot(w.astype(MXU_DTYPE), vv))
                new_carries.append(offset)
            return tuple(new_carries), tuple(new_accs)

        carries = tuple(jnp.zeros((CHUNK, 1), F32) for _ in range(2))
        accs = tuple(jnp.zeros((CHUNK, PAIR), F32) for _ in range(2))
        carries, accs = tile(last, carries, accs, True)
        carries, accs = lax.fori_loop(1, last + 1, lambda i, st: tile(last - i, st[0], st[1], False), (carries, accs))
        o_ref[...] = jnp.where(halves[0], accs[0], accs[1])
        tot_ref[...] = jnp.where(halves[0], carries[0], carries[1])

    return pl.pallas_call(
        body, name=name, grid=(N_PAIRS, nq), in_specs=[qblk, full, full],
        out_specs=[qblk, qblk],
        out_shape=[jax.ShapeDtypeStruct((l, D_SB), F32), jax.ShapeDtypeStruct((l, D_SB), F32)],
    )(q, k, v)


def _sb_bwd(q, k, v, dycat, tot, name):
    l = q.shape[0]
    nq = l // CHUNK
    kt = _pick(l, SB_KEYS, CHUNK)
    sub = kt // CHUNK
    qblk = pl.BlockSpec((CHUNK, PAIR), lambda i, j: (j, i))
    doblk = pl.BlockSpec((CHUNK, PAIR), lambda i, j: (j, D_SSD // PAIR + i))
    full = pl.BlockSpec((l, PAIR), lambda i, j: (0, i))

    def body(q_ref, k_ref, v_ref, do_ref, tot_ref, dq_ref, dk_ref, dv_ref):
        qb = pl.program_id(1)

        @pl.when(qb == 0)
        def _():
            dk_ref[...] = jnp.zeros_like(dk_ref)
            dv_ref[...] = jnp.zeros_like(dv_ref)

        r, c = _sb_masks()
        upto2 = _stack2(r <= c)
        before2 = _stack2(r < c)
        halves = _half_masks()
        qm = [jnp.where(halves[a], q_ref[...], jnp.zeros((CHUNK, PAIR), q_ref.dtype)) for a in range(2)]
        dom = [jnp.where(halves[a], do_ref[...], 0.0).astype(MXU_DTYPE) for a in range(2)]
        total = [jnp.max(jnp.where(halves[a], tot_ref[...], -jnp.inf), axis=1, keepdims=True) for a in range(2)]
        key_minus_query = lax.broadcasted_iota(jnp.int32, (CHUNK, kt), 1) - lax.broadcasted_iota(jnp.int32, (CHUNK, kt), 0)
        last = (qb * CHUNK) // kt

        def prefix(values, tri2, offset):
            parts = []
            for j in range(sub):
                piece = values[:, j * CHUNK:(j + 1) * CHUNK]
                parts.append(_split_dot(piece, tri2) + offset)
                offset = offset + jnp.sum(piece, axis=1, keepdims=True)
            return jnp.concatenate(parts, axis=1), offset

        def tile(t, carry_p, carry_d, dq, masked):
            rows = pl.ds(pl.multiple_of(t * kt, kt), kt)
            kv, vv = k_ref[rows, :], v_ref[rows, :]
            keep = key_minus_query < qb * CHUNK - t * kt
            new_p, new_d, new_dq = [], [], []
            dk_upd, dv_upd = None, None
            for a in range(2):
                lg = _dot_nt(qm[a], kv)
                sp = _softplus(lg)
                lr = jnp.where(keep, -sp, 0.0) if masked else -sp
                p_incl, p_next = prefix(lr, upto2, carry_p[a])
                ls = lg - sp
                w = jnp.exp(ls + (total[a] - p_incl))
                if masked:
                    w = jnp.where(keep, w, 0.0)
                da = _dot_nt(dom[a], vv) * w
                d_excl, d_next = prefix(da, before2, carry_d[a])
                sig = jnp.exp(ls)
                dl = da * (1.0 - sig) - d_excl * sig
                if masked:
                    dl = jnp.where(keep, dl, 0.0)
                dl_b = dl.astype(MXU_DTYPE)
                dv_a = _dot_tn(w.astype(MXU_DTYPE), dom[a])
                dk_a = _dot_tn(dl_b, qm[a])
                dv_upd = dv_a if dv_upd is None else dv_upd + dv_a
                dk_upd = dk_a if dk_upd is None else dk_upd + dk_a
                new_dq.append(dq[a] + _dot(dl_b, kv))
                new_p.append(p_next)
                new_d.append(d_next)
            dv_ref[rows, :] += dv_upd
            dk_ref[rows, :] += dk_upd
            return tuple(new_p), tuple(new_d), tuple(new_dq)

        zeros = tuple(jnp.zeros((CHUNK, 1), F32) for _ in range(2))
        dq0 = tuple(jnp.zeros((CHUNK, PAIR), F32) for _ in range(2))
        carry_p, carry_d, dq = lax.fori_loop(0, last, lambda t, st: tile(t, st[0], st[1], st[2], False), (zeros, zeros, dq0))
        _, _, dq = tile(last, carry_p, carry_d, dq, True)
        dq_ref[...] = jnp.where(halves[0], dq[0], dq[1])

    return pl.pallas_call(
        body, name=name, grid=(N_PAIRS, nq),
        in_specs=[qblk, full, full, doblk, qblk],
        out_specs=[qblk, full, full],
        out_shape=[jax.ShapeDtypeStruct((l, D_SB), F32)] * 3,
    )(q, k, v, dycat, tot)


def _ada_fwd(c_all, w_shard, name):
    def body(c_ref, w_ref, o_ref):
        cv = c_ref[...]
        o_ref[...] = _dot(cv * _sigmoid(cv), w_ref[...], HIGHEST)

    return pl.pallas_call(body, name=name, out_shape=jax.ShapeDtypeStruct((c_all.shape[0], w_shard.shape[1]), F32))(c_all, w_shard)


def _ada_bwd(c_all, dmod_shard, name):
    def body(c_ref, d_ref, o_ref):
        cv = c_ref[...]
        o_ref[...] = _dot_tn(cv * _sigmoid(cv), d_ref[...], HIGHEST)

    return pl.pallas_call(body, name=name, out_shape=jax.ShapeDtypeStruct((c_all.shape[1], dmod_shard.shape[1]), F32))(c_all, dmod_shard)


def _sum_small(parts, name):
    def body(p_ref, o_ref):
        acc = p_ref[0]
        for d in range(1, N_DEV):
            acc = acc + p_ref[d]
        o_ref[...] = acc

    return pl.pallas_call(body, name=name, out_shape=jax.ShapeDtypeStruct(parts.shape[1:], F32))(parts)


def _adamw(w, g, m, v, name):
    rows, cols = w.shape
    tr = _pick(rows, 256, 8)
    spec = pl.BlockSpec((tr, cols), lambda i: (i, 0))
    bc1 = 1.0 - ADAM_B1 ** ADAM_STEP
    bc2 = 1.0 - ADAM_B2 ** ADAM_STEP

    def body(w_ref, g_ref, m_ref, v_ref, d_ref, nm_ref, nv_ref):
        gv = g_ref[...]
        nm = ADAM_B1 * m_ref[...] + (1.0 - ADAM_B1) * gv
        nv = ADAM_B2 * v_ref[...] + (1.0 - ADAM_B2) * (gv * gv)
        nm_ref[...] = nm
        nv_ref[...] = nv
        d_ref[...] = -ADAM_LR * ((nm / bc1) / (jnp.sqrt(nv / bc2) + ADAM_EPS) + ADAM_WD * w_ref[...])

    return pl.pallas_call(body, name=name, grid=(rows // tr,), in_specs=[spec] * 4, out_specs=[spec] * 3,
                          out_shape=[jax.ShapeDtypeStruct((rows, cols), F32)] * 3)(w, g, m, v)


def _pack_shards(w_in, w_out, w_gate, w_up, w_down):
    used = ROWS_W_IN + ROWS_W_OUT + 3 * ROWS_FF
    parts = [w_in.reshape(ROWS_W_IN, D_MODEL), w_out, w_gate.reshape(ROWS_FF, D_MODEL), w_up.reshape(ROWS_FF, D_MODEL),
             w_down, jnp.zeros((PACK_ROWS - used, D_MODEL), w_in.dtype)]
    return jnp.concatenate(parts, axis=0)


def _unpack_shards(p):
    o = 0
    w_in = p[o:o + ROWS_W_IN].reshape(D_MODEL, ROWS_W_IN)
    o += ROWS_W_IN
    w_out = p[o:o + ROWS_W_OUT]
    o += ROWS_W_OUT
    w_gate = p[o:o + ROWS_FF].reshape(D_MODEL, ROWS_FF)
    o += ROWS_FF
    w_up = p[o:o + ROWS_FF].reshape(D_MODEL, ROWS_FF)
    o += ROWS_FF
    w_down = p[o:o + ROWS_FF]
    return w_in, w_out, w_gate, w_up, w_down


def _unpack_gathered(gw):
    def cols(lo, rows):
        blk = gw[:, lo:lo + rows].reshape(N_DEV, D_MODEL, rows)
        return blk.transpose(1, 0, 2).reshape(D_MODEL, N_DEV * rows)

    def rows_(lo, rows):
        return gw[:, lo:lo + rows].reshape(N_DEV * rows, D_MODEL)

    o = 0
    w_in = cols(o, ROWS_W_IN)
    o += ROWS_W_IN
    w_out = rows_(o, ROWS_W_OUT)
    o += ROWS_W_OUT
    w_gate = cols(o, ROWS_FF)
    o += ROWS_FF
    w_up = cols(o, ROWS_FF)
    o += ROWS_FF
    w_down = rows_(o, ROWS_FF)
    return w_in, w_out, w_gate, w_up, w_down


def _pack_grads(g_in, g_out, g_gate, g_up, g_down):
    def cols(gw, rows):
        return gw.reshape(D_MODEL, N_DEV, rows).transpose(1, 0, 2).reshape(N_DEV, rows, D_MODEL)

    def rows_(gw, rows):
        return gw.reshape(N_DEV, rows, D_MODEL)

    used = ROWS_W_IN + ROWS_W_OUT + 3 * ROWS_FF
    parts = [cols(g_in, ROWS_W_IN), rows_(g_out, ROWS_W_OUT), cols(g_gate, ROWS_FF), cols(g_up, ROWS_FF), rows_(g_down, ROWS_FF),
             jnp.zeros((N_DEV, PACK_ROWS - used, D_MODEL), g_in.dtype)]
    return jnp.concatenate(parts, axis=1)


def _pad_w_in(w_in):
    lo = D_SSD + D_CONV
    return jnp.concatenate([w_in[:, D_SSD:lo + N_HEADS], jnp.zeros((w_in.shape[0], DT_PAD - N_HEADS), w_in.dtype),
                            w_in[:, :D_SSD], w_in[:, lo + N_HEADS:]], axis=1)


def _unpad_w_in(g):
    return jnp.concatenate([g[:, O_Z:O_Q], g[:, :O_DT + N_HEADS], g[:, O_Q:]], axis=1)


def _local_step(x, target, mod, norm1_w, w_in_p, conv_w, conv_b, dt_bias, a_log, d_skip, ssd_norm_w, q_norm_w, k_norm_w,
                w_out, norm2_w, w_gate, w_up, w_down):
    sh1, sc1, g1, sh2, sc2, g2 = [mod[:, i * D_MODEL:(i + 1) * D_MODEL] for i in range(N_MOD)]
    qw2, kw2 = jnp.tile(q_norm_w, (1, 2)), jnp.tile(k_norm_w, (1, 2))
    dskip_x = jnp.repeat(d_skip, HEAD_DIM, axis=1)
    dtb_x, dtb_y = jnp.repeat(dt_bias, HEAD_DIM, axis=1), jnp.repeat(dt_bias, HEAD_TILE, axis=1)
    al_x, al_y = jnp.repeat(a_log, HEAD_DIM, axis=1), jnp.repeat(a_log, HEAD_TILE, axis=1)
    head_of_lane = jnp.arange(D_SSD, dtype=jnp.int32) // HEAD_DIM
    head_ones = (head_of_lane[:, None] == head_of_lane[None, :]).astype(BF16)

    h1 = _rms_mod(x, norm1_w, sc1, sh1, "rms_mod1")
    proj = _matmul(h1, w_in_p, "in_proj")
    dtr = proj[:, O_DT:O_DT + N_HEADS]
    dtr_x, dtr_y = jnp.repeat(dtr, HEAD_DIM, axis=1), jnp.repeat(dtr, HEAD_TILE, axis=1)

    xc = _conv_silu(proj, conv_w, conv_b, "conv_silu")
    ycore, states = _ssd_fwd(xc, dtr_x, dtr_y, dtb_x, dtb_y, al_x, al_y, "ssd_fwd")
    y_ssd = _ssd_gate_norm(ycore, xc, proj, dskip_x, ssd_norm_w, "ssd_gate_norm")

    qn, kn, vb = _qkv_prep(proj, qw2, kw2, "qkv_prep")
    o_sb, tot = _sb_fwd(qn, kn, vb, "sb_fwd")
    ycat = jnp.concatenate([y_ssd, o_sb.astype(MXU_DTYPE)], axis=1)

    mix = _matmul(ycat, w_out, "out_proj")
    x1, h2 = _residual_rms_mod(x, mix, g1, norm2_w, sc2, sh2, "residual_rms_mod2")
    gate = _matmul(h2, w_gate, "ffn_gate")
    up = _matmul(h2, w_up, "ffn_up")
    act = _swiglu(gate, up, "swiglu")
    ffn = _matmul(act, w_down, "ffn_down")
    dy, loss_blk = _loss_head(x1, ffn, g2, target, "loss_head")

    dffn, st_g2 = _gate_bwd(dy, ffn, g2, "gate2_bwd")
    g_down = _matmul_tn(act, dffn, "ffn_down_dw")
    dact = _matmul(dffn, w_down.T, "ffn_down_dx")
    dgate, dup = _swiglu_bwd(gate, up, dact, "swiglu_bwd")
    g_gate = _matmul_tn(h2, dgate, "ffn_gate_dw")
    g_up = _matmul_tn(h2, dup, "ffn_up_dw")
    dh2 = _matmul(jnp.concatenate([dgate, dup], axis=1), jnp.concatenate([w_gate, w_up], axis=1).T, "ffn_dh")
    dx1, st_n2 = _norm_bwd(x1, dh2, dy, norm2_w, sc2, "norm2_bwd")

    dmix, st_g1 = _gate_bwd(dx1, mix, g1, "gate1_bwd")
    g_out = _matmul_tn(ycat, dmix, "out_proj_dw")
    dycat = _matmul(dmix, w_out.T, "out_proj_dx")

    dqn, dkn, dv = _sb_bwd(qn, kn, vb, dycat, tot, "sb_bwd")
    dq, dk, st_q, st_k = _qk_norm_bwd(proj, dqn, dkn, qw2, kw2, "qk_norm_bwd")

    dycore, dxs_skip, dz, st_gn = _ssd_gate_norm_bwd(dycat, ycore, xc, proj, dskip_x, ssd_norm_w, "ssd_gate_norm_bwd")
    dxc, ddtr_x, st_ssd, st_heads = _ssd_bwd(xc, dtr_x, dtr_y, dtb_x, dtb_y, al_x, al_y, dycore, states, head_ones, "ssd_bwd")
    ddtr = ddtr_x[:, ::HEAD_DIM]
    dxc = jnp.concatenate([dxc[:, :D_SSD] + dxs_skip, dxc[:, D_SSD:]], axis=1)
    dpre, st_conv = _conv_silu_bwd_pre(proj, dxc, conv_w, conv_b, "conv_silu_bwd")
    dxbc = _conv_bwd_input(dpre, conv_w, "conv_bwd_input")

    dproj = jnp.concatenate([dxbc, ddtr, jnp.zeros((x.shape[0], DT_PAD - N_HEADS), F32), dz, dq, dk, dv], axis=1).astype(MXU_DTYPE)
    g_in = _unpad_w_in(_matmul_tn(h1, dproj, "in_proj_dw"))
    dh1 = _matmul(dproj, w_in_p.T, "in_proj_dx")
    grad_x, st_n1 = _norm_bwd(x, dh1, dx1, norm1_w, sc1, "norm1_bwd")

    pad = jnp.zeros((1, SM_SSD_NORM - SM_D_SKIP - N_HEADS), F32)
    small = jnp.concatenate(
        [st_n1[1:2], st_n1[0:1], st_g1[0:1], st_n2[1:2], st_n2[0:1], st_g2[0:1],
         st_n1[2:3], st_conv[4:5], st_conv[0:4].reshape(1, CONV_WIDTH * D_CONV),
         st_ssd[1:2, ::HEAD_DIM], st_ssd[0:1, ::HEAD_DIM], st_heads[2:3, ::HEAD_DIM], pad,
         st_gn[0:1], st_q[0:1, :HEAD_DIM] + st_q[0:1, HEAD_DIM:], st_k[0:1, :HEAD_DIM] + st_k[0:1, HEAD_DIM:], st_n2[2:3]], axis=1)
    return loss_blk, grad_x, (g_in, g_out, g_gate, g_up, g_down), small


def kernel(x, c, w_ada, b_ada, norm1_w, w_in, conv_w, conv_b, dt_bias, a_log, d_skip, ssd_norm_w, q_norm_w, k_norm_w, w_out, norm2_w, w_gate, w_up, w_down, loss_target, m_w_ada, m_b_ada, m_norm1_w, m_w_in, m_conv_w, m_conv_b, m_dt_bias, m_a_log, m_d_skip, m_ssd_norm_w, m_q_norm_w, m_k_norm_w, m_w_out, m_norm2_w, m_w_gate, m_w_up, m_w_down, v_w_ada, v_b_ada, v_norm1_w, v_w_in, v_conv_w, v_conv_b, v_dt_bias, v_a_log, v_d_skip, v_ssd_norm_w, v_q_norm_w, v_k_norm_w, v_w_out, v_norm2_w, v_w_gate, v_w_up, v_w_down):
    me = 4 * lax.axis_index("x") + 2 * lax.axis_index("y") + lax.axis_index("c")
    conv_cols = D_CONV // N_DEV
    ada_cols = N_MOD * D_MODEL // N_DEV

    packed = _pack_shards(w_in[0], w_out[0], w_gate[0], w_up[0], w_down[0]).astype(BF16)
    gathered = _all_gather_big(packed, "gather_weights")
    w_in_f, w_out_f, w_gate_f, w_up_f, w_down_f = _unpack_gathered(gathered)
    w_in_p = _pad_w_in(w_in_f)

    first = _all_gather_small(jnp.concatenate([c, conv_w[0].reshape(1, CONV_WIDTH * conv_cols)], axis=1), "gather_cond")
    c_all = first[:, 0, :D_MODEL]
    conv_w_f = first[:, 0, D_MODEL:].reshape(N_DEV, CONV_WIDTH, conv_cols).transpose(1, 0, 2).reshape(CONV_WIDTH, D_CONV)
    mod_cols = _ada_fwd(c_all, w_ada[0], "ada_fwd")
    mods = _all_gather_small(mod_cols.reshape(1, N_DEV * ada_cols), "gather_mod")
    mod = lax.dynamic_index_in_dim(mods.reshape(N_DEV, N_DEV, ada_cols), me, axis=1, keepdims=False)
    mod = mod.reshape(1, N_MOD * D_MODEL) + b_ada

    loss_blk, grad_x, big, small = _local_step(
        x[0], loss_target[0], mod, norm1_w, w_in_p, conv_w_f, conv_b, dt_bias, a_log, d_skip, ssd_norm_w, q_norm_w, k_norm_w,
        w_out_f, norm2_w, w_gate_f, w_up_f, w_down_f)
    loss = lax.psum(loss_blk[0, 0], ("x", "y", "c"))

    slots = _all_to_all_big(_pack_grads(*big).astype(BF16), "scatter_grads")
    g_in, g_out, g_gate, g_up, g_down = _unpack_shards(_sum_slots(slots, "sum_grads"))

    parts = _all_gather_small(small, "gather_small")
    gsum = _sum_small(parts, "sum_small")
    dmod_shard = lax.dynamic_slice_in_dim(parts[:, 0, :N_MOD * D_MODEL], me * ada_cols, ada_cols, axis=1)
    g_ada = _ada_bwd(c_all, dmod_shard, "ada_bwd")
    g_conv_w = lax.dynamic_slice_in_dim(gsum[:, SM_CONV_W:SM_DT_BIAS].reshape(CONV_WIDTH, D_CONV), me * conv_cols, conv_cols, axis=1)

    def pack_small(b_ada_, norm1_, conv_b_, dt_bias_, a_log_, d_skip_, ssd_norm_, q_norm_, k_norm_, norm2_):
        return jnp.concatenate(
            [b_ada_, norm1_, conv_b_, jnp.zeros((1, CONV_WIDTH * D_CONV), F32), dt_bias_, a_log_, d_skip_,
             jnp.zeros((1, SM_SSD_NORM - SM_D_SKIP - N_HEADS), F32), ssd_norm_, q_norm_, k_norm_, norm2_], axis=1)

    def unpack_small(p):
        return {
            "b_ada": p[:, SM_B_ADA:SM_NORM1], "norm1_w": p[:, SM_NORM1:SM_CONV_B], "conv_b": p[:, SM_CONV_B:SM_CONV_W],
            "dt_bias": p[:, SM_DT_BIAS:SM_A_LOG], "a_log": p[:, SM_A_LOG:SM_D_SKIP], "d_skip": p[:, SM_D_SKIP:SM_D_SKIP + N_HEADS],
            "ssd_norm_w": p[:, SM_SSD_NORM:SM_Q_NORM], "q_norm_w": p[:, SM_Q_NORM:SM_K_NORM], "k_norm_w": p[:, SM_K_NORM:SM_NORM2],
            "norm2_w": p[:, SM_NORM2:SM_TOTAL]}

    w_small = pack_small(b_ada, norm1_w, conv_b, dt_bias, a_log, d_skip, ssd_norm_w, q_norm_w, k_norm_w, norm2_w)
    m_small = pack_small(m_b_ada, m_norm1_w, m_conv_b, m_dt_bias, m_a_log, m_d_skip, m_ssd_norm_w, m_q_norm_w, m_k_norm_w, m_norm2_w)
    v_small = pack_small(v_b_ada, v_norm1_w, v_conv_b, v_dt_bias, v_a_log, v_d_skip, v_ssd_norm_w, v_q_norm_w, v_k_norm_w, v_norm2_w)
    small_out = [unpack_small(t) for t in (gsum,) + tuple(_adamw(w_small, gsum, m_small, v_small, "adamw_small"))]

    sharded = {
        "w_ada": (w_ada[0], g_ada, m_w_ada[0], v_w_ada[0]),
        "w_in": (w_in[0], g_in, m_w_in[0], v_w_in[0]),
        "conv_w": (conv_w[0], g_conv_w, m_conv_w[0], v_conv_w[0]),
        "w_out": (w_out[0], g_out, m_w_out[0], v_w_out[0]),
        "w_gate": (w_gate[0], g_gate, m_w_gate[0], v_w_gate[0]),
        "w_up": (w_up[0], g_up, m_w_up[0], v_w_up[0]),
        "w_down": (w_down[0], g_down, m_w_down[0], v_w_down[0]),
    }
    sharded_out = {n: (t[1],) + tuple(_adamw(*t, "adamw_" + n)) for n, t in sharded.items()}

    names = ["w_ada", "b_ada", "norm1_w", "w_in", "conv_w", "conv_b", "dt_bias", "a_log", "d_skip", "ssd_norm_w", "q_norm_w",
             "k_norm_w", "w_out", "norm2_w", "w_gate", "w_up", "w_down"]
    outs = [loss, grad_x[None]]
    for kind in range(4):
        for n in names:
            outs.append(sharded_out[n][kind][None] if n in sharded_out else small_out[kind][n])
    return tuple(outs)
```

```python
import functools

import jax
import jax.numpy as jnp
from jax import lax
from jax.experimental import pallas as pl
from jax.experimental.pallas import tpu as pltpu

F32 = jnp.float32
BF16 = jnp.bfloat16
MXU_DTYPE = jnp.bfloat16
HIGHEST = lax.Precision.HIGHEST
MESH_IDS = pl.DeviceIdType.MESH

N_DEV = 8
D_MODEL = 1024
HEAD_DIM = 64
N_HEADS = 16
D_SSD = 1024
D_SB = 1024
SSD_GROUPS = 2
SSD_STATE = 128
GROUP_WIDTH = D_SSD // SSD_GROUPS
D_CONV = D_SSD + 2 * SSD_GROUPS * SSD_STATE
CONV_WIDTH = 4
CHUNK = 128
D_FF = 2816
N_MOD = 6
EPS = 1e-6
D_IN_PROJ = 5648
O_XBC = 0
O_DT = D_CONV
DT_PAD = 512
O_Z = O_DT + DT_PAD
O_Q = O_Z + D_SSD
D_IN_PAD = O_Q + 3 * D_SB
HALO = 8

ADAM_LR = 0.001
ADAM_B1 = 0.9
ADAM_B2 = 0.999
ADAM_EPS = 1e-08
ADAM_WD = 0.01
ADAM_STEP = 10

ROWS_W_IN = 706
ROWS_W_OUT = 256
ROWS_FF = 352
PACK_ROWS = 2048
SM_B_ADA = 0
SM_NORM1 = 6144
SM_CONV_B = 7168
SM_CONV_W = 8704
SM_DT_BIAS = 14848
SM_A_LOG = 14864
SM_D_SKIP = 14880
SM_SSD_NORM = 14976
SM_Q_NORM = 16000
SM_K_NORM = 16064
SM_NORM2 = 16128
SM_TOTAL = 17152


def _pick(n, cap, mult):
    if n <= cap:
        return n
    best = None
    for t in range(mult, cap + 1, mult):
        if n % t == 0:
            best = t
    assert best is not None, (n, cap, mult)
    return best


def _dot(a, b, precision=None):
    return jnp.dot(a, b, preferred_element_type=F32, precision=precision)


def _dot_nt(a, b, precision=None):
    return lax.dot_general(a, b, (((1,), (1,)), ((), ())), preferred_element_type=F32, precision=precision)


def _dot_tn(a, b, precision=None):
    return lax.dot_general(a, b, (((0,), (0,)), ((), ())), preferred_element_type=F32, precision=precision)


def _softplus(v):
    return jnp.maximum(v, 0.0) + jnp.log1p(jnp.exp(-jnp.abs(v)))


def _sigmoid(v):
    return jax.nn.sigmoid(v)


def _colsum(v):
    return jnp.sum(v, axis=0, keepdims=True)


def _split_dot(v, tri2):
    hi = v.astype(BF16)
    lo = (v - hi.astype(F32)).astype(BF16)
    return _dot(jnp.concatenate([hi, lo], axis=1), tri2)


def _position():
    x, y, c = lax.axis_index("x"), lax.axis_index("y"), lax.axis_index("c")
    return x, y, c


def _peer(x, y, c, k):
    px = 1 - x if (k >> 2) & 1 else x
    py = 1 - y if (k >> 1) & 1 else y
    pc = 1 - c if k & 1 else c
    return px, py, pc


def _all_gather_small(v, name):
    n = v.shape[1]

    def body(v_ref, out_ref, send_sems, recv_sems, local_sem):
        x, y, c = _position()
        me = 4 * x + 2 * y + c
        mine = pltpu.make_async_copy(v_ref, out_ref.at[me], local_sem)
        mine.start()
        sends = []
        for k in range(1, N_DEV):
            cp = pltpu.make_async_remote_copy(
                src_ref=v_ref, dst_ref=out_ref.at[me], send_sem=send_sems.at[k - 1], recv_sem=recv_sems.at[k - 1],
                device_id=_peer(x, y, c, k), device_id_type=MESH_IDS)
            cp.start()
            sends.append(cp)
        for k in range(1, N_DEV):
            px, py, pc = _peer(x, y, c, k)
            pltpu.make_async_remote_copy(
                src_ref=v_ref, dst_ref=out_ref.at[4 * px + 2 * py + pc], send_sem=send_sems.at[k - 1],
                recv_sem=recv_sems.at[k - 1], device_id=(px, py, pc), device_id_type=MESH_IDS).wait_recv()
        for cp in sends:
            cp.wait_send()
        mine.wait()

    return pl.pallas_call(
        body, name=name,
        out_shape=jax.ShapeDtypeStruct((N_DEV, 1, n), v.dtype),
        in_specs=[pl.BlockSpec(memory_space=pltpu.VMEM)],
        out_specs=pl.BlockSpec(memory_space=pltpu.VMEM),
        scratch_shapes=[pltpu.SemaphoreType.DMA((N_DEV - 1,)), pltpu.SemaphoreType.DMA((N_DEV - 1,)),
                        pltpu.SemaphoreType.DMA],
    )(v)


def _all_gather_big(blocks, name):
    na = len(blocks)
    copies = N_DEV - 1

    def body(*refs):
        b_refs, out_refs = refs[:na], refs[na:2 * na]
        send_sems, recv_sems, local_sems = refs[2 * na:]
        x, y, c = _position()
        me, sibling = (x, y, c), (x, y, 1 - c)
        chips = [(1 - x, y), (x, 1 - y), (1 - x, 1 - y)]

        def copy(a, k, blk, to, own=False):
            slot = out_refs[a].at[4 * blk[0] + 2 * blk[1] + blk[2]]
            return pltpu.make_async_remote_copy(
                src_ref=b_refs[a] if own else slot, dst_ref=slot,
                send_sem=send_sems.at[copies * a + k], recv_sem=recv_sems.at[copies * a + k], device_id=to, device_id_type=MESH_IDS)

        mine = [pltpu.make_async_copy(b_refs[a], out_refs[a].at[4 * x + 2 * y + c], local_sems.at[a]) for a in range(na)]
        for cp in mine:
            cp.start()
        first = [copy(a, 0, me, sibling, own=True) for a in range(na)]
        first += [copy(a, 1 + j, me, (*chip, c), own=True) for j, chip in enumerate(chips) for a in range(na)]
        for cp in first:
            cp.start()
        passed = []
        for j, chip in enumerate(chips):
            for a in range(na):
                copy(a, 1 + j, (*chip, c), me).wait_recv()
                passed.append(copy(a, 4 + j, (*chip, c), sibling))
                passed[-1].start()
        for a in range(na):
            copy(a, 0, sibling, me).wait_recv()
        for j, chip in enumerate(chips):
            for a in range(na):
                copy(a, 4 + j, (*chip, 1 - c), me).wait_recv()
        for cp in first + passed:
            cp.wait_send()
        for cp in mine:
            cp.wait()

    hbm = pl.BlockSpec(memory_space=pl.ANY)
    return pl.pallas_call(
        body, name=name,
        out_shape=[jax.ShapeDtypeStruct((N_DEV,) + b.shape, b.dtype) for b in blocks],
        in_specs=[hbm] * na, out_specs=[hbm] * na,
        scratch_shapes=[pltpu.SemaphoreType.DMA((copies * na,)), pltpu.SemaphoreType.DMA((copies * na,)),
                        pltpu.SemaphoreType.DMA((na,))],
    )(*blocks)


def _all_to_all_big(blocks, name):
    na = len(blocks)
    copies = N_DEV - 1

    def body(*refs):
        b_refs, out_refs = refs[:na], refs[na:2 * na]
        send_sems, recv_sems, local_sems = refs[2 * na:]
        x, y, c = _position()
        me = 4 * x + 2 * y + c
        mine = [pltpu.make_async_copy(b_refs[a].at[me], out_refs[a].at[me], local_sems.at[a]) for a in range(na)]
        for cp in mine:
            cp.start()
        sends = []
        for k in range(1, N_DEV):
            px, py, pc = _peer(x, y, c, k)
            for a in range(na):
                cp = pltpu.make_async_remote_copy(
                    src_ref=b_refs[a].at[4 * px + 2 * py + pc], dst_ref=out_refs[a].at[me], send_sem=send_sems.at[copies * a + k - 1],
                    recv_sem=recv_sems.at[copies * a + k - 1], device_id=(px, py, pc), device_id_type=MESH_IDS)
                cp.start()
                sends.append(cp)
        for k in range(1, N_DEV):
            px, py, pc = _peer(x, y, c, k)
            for a in range(na):
                pltpu.make_async_remote_copy(
                    src_ref=b_refs[a].at[me], dst_ref=out_refs[a].at[4 * px + 2 * py + pc], send_sem=send_sems.at[copies * a + k - 1],
                    recv_sem=recv_sems.at[copies * a + k - 1], device_id=(px, py, pc), device_id_type=MESH_IDS).wait_recv()
        for cp in sends:
            cp.wait_send()
        for cp in mine:
            cp.wait()

    hbm = pl.BlockSpec(memory_space=pl.ANY)
    return pl.pallas_call(
        body, name=name,
        out_shape=[jax.ShapeDtypeStruct(b.shape, b.dtype) for b in blocks],
        in_specs=[hbm] * na, out_specs=[hbm] * na,
        scratch_shapes=[pltpu.SemaphoreType.DMA((copies * na,)), pltpu.SemaphoreType.DMA((copies * na,)),
                        pltpu.SemaphoreType.DMA((na,))],
    )(*blocks)


def _sum_slots(slots, name):
    _, r, n = slots.shape
    tr = _pick(r, 256, 16) if r % 16 == 0 else r

    def body(s_ref, o_ref):
        acc = s_ref[0].astype(F32)
        for d in range(1, N_DEV):
            acc = acc + s_ref[d].astype(F32)
        o_ref[...] = acc

    return pl.pallas_call(
        body, name=name, grid=(r // tr,),
        in_specs=[pl.BlockSpec((N_DEV, tr, n), lambda i: (0, i, 0))],
        out_specs=pl.BlockSpec((tr, n), lambda i: (i, 0)),
        out_shape=jax.ShapeDtypeStruct((r, n), F32),
    )(slots)


def _matmul(a, b, name, out_dtype=F32):
    m, k = a.shape
    _, n = b.shape
    tm, tn, tk = _pick(m, 512, 16), _pick(n, 1408, 128), _pick(k, 1408, 128)
    nk = k // tk

    def body(a_ref, b_ref, o_ref, acc_ref):
        kk = pl.program_id(2)

        @pl.when(kk == 0)
        def _():
            acc_ref[...] = jnp.zeros_like(acc_ref)

        acc_ref[...] += _dot(a_ref[...].astype(MXU_DTYPE), b_ref[...].astype(MXU_DTYPE))

        @pl.when(kk == nk - 1)
        def _():
            o_ref[...] = acc_ref[...].astype(o_ref.dtype)

    return pl.pallas_call(
        body, name=name, grid=(m // tm, n // tn, nk),
        in_specs=[pl.BlockSpec((tm, tk), lambda i, j, kk: (i, kk)), pl.BlockSpec((tk, tn), lambda i, j, kk: (kk, j))],
        out_specs=pl.BlockSpec((tm, tn), lambda i, j, kk: (i, j)),
        out_shape=jax.ShapeDtypeStruct((m, n), out_dtype),
        scratch_shapes=[pltpu.VMEM((tm, tn), F32)],
    )(a, b)


def _matmul_nt(a, bt, name, out_dtype=F32):
    m, k = a.shape
    n, _ = bt.shape
    tm, tn, tk = _pick(m, 512, 16), _pick(n, 1408, 128), _pick(k, 1408, 128)
    nk = k // tk

    def body(a_ref, b_ref, o_ref, acc_ref):
        kk = pl.program_id(2)

        @pl.when(kk == 0)
        def _():
            acc_ref[...] = jnp.zeros_like(acc_ref)

        acc_ref[...] += _dot_nt(a_ref[...].astype(MXU_DTYPE), b_ref[...].astype(MXU_DTYPE))

        @pl.when(kk == nk - 1)
        def _():
            o_ref[...] = acc_ref[...].astype(o_ref.dtype)

    return pl.pallas_call(
        body, name=name, grid=(m // tm, n // tn, nk),
        in_specs=[pl.BlockSpec((tm, tk), lambda i, j, kk: (i, kk)), pl.BlockSpec((tn, tk), lambda i, j, kk: (j, kk))],
        out_specs=pl.BlockSpec((tm, tn), lambda i, j, kk: (i, j)),
        out_shape=jax.ShapeDtypeStruct((m, n), out_dtype),
        scratch_shapes=[pltpu.VMEM((tm, tn), F32)],
    )(a, bt)


def _matmul_sum2(a1, b1, a2, b2, name):
    m, k = a1.shape
    _, n = b1.shape
    tm, tn, tk = _pick(m, 512, 16), _pick(n, 1408, 128), _pick(k, 1408, 128)
    nk = k // tk

    def body(a1_ref, b1_ref, a2_ref, b2_ref, o_ref):
        @pl.when(pl.program_id(2) == 0)
        def _():
            o_ref[...] = jnp.zeros_like(o_ref)

        o_ref[...] += (_dot(a1_ref[...].astype(MXU_DTYPE), b1_ref[...].astype(MXU_DTYPE))
                       + _dot(a2_ref[...].astype(MXU_DTYPE), b2_ref[...].astype(MXU_DTYPE)))

    a_spec = pl.BlockSpec((tm, tk), lambda i, j, kk: (i, kk))
    b_spec = pl.BlockSpec((tk, tn), lambda i, j, kk: (kk, j))
    return pl.pallas_call(
        body, name=name, grid=(m // tm, n // tn, nk), in_specs=[a_spec, b_spec, a_spec, b_spec],
        out_specs=pl.BlockSpec((tm, tn), lambda i, j, kk: (i, j)),
        out_shape=jax.ShapeDtypeStruct((m, n), F32),
    )(a1, b1, a2, b2)


def _matmul_tn(a, b, name, out_dtype=F32):
    l, m = a.shape
    _, n = b.shape
    tm, tn, tl = _pick(m, 1024, 128), _pick(n, 1408, 128), _pick(l, 512, 16)
    nl = l // tl

    def body(a_ref, b_ref, o_ref, acc_ref):
        ll = pl.program_id(2)

        @pl.when(ll == 0)
        def _():
            acc_ref[...] = jnp.zeros_like(acc_ref)

        acc_ref[...] += _dot_tn(a_ref[...].astype(MXU_DTYPE), b_ref[...].astype(MXU_DTYPE))

        @pl.when(ll == nl - 1)
        def _():
            o_ref[...] = acc_ref[...].astype(o_ref.dtype)

    return pl.pallas_call(
        body, name=name, grid=(m // tm, n // tn, nl),
        in_specs=[pl.BlockSpec((tl, tm), lambda i, j, ll: (ll, i)), pl.BlockSpec((tl, tn), lambda i, j, ll: (ll, j))],
        out_specs=pl.BlockSpec((tm, tn), lambda i, j, ll: (i, j)),
        out_shape=jax.ShapeDtypeStruct((m, n), out_dtype),
        scratch_shapes=[pltpu.VMEM((tm, tn), F32)],
    )(a, b)


def _row_specs(l, d, n_rows, n_vecs):
    tl = _pick(l, 512, 16)
    row = pl.BlockSpec((tl, d), lambda i: (i, 0))
    vec = pl.BlockSpec((1, d), lambda i: (0, 0))
    return tl, row, vec, [row] * n_rows + [vec] * n_vecs


def _rms_mod(x, nw, sc, sh, name):
    l, d = x.shape
    tl, row, _, in_specs = _row_specs(l, d, 1, 3)

    def body(x_ref, nw_ref, sc_ref, sh_ref, h_ref):
        xv = x_ref[...]
        r = lax.rsqrt(jnp.mean(xv * xv, axis=-1, keepdims=True) + EPS)
        h_ref[...] = (((xv * r) * nw_ref[...]) * (1.0 + sc_ref[...]) + sh_ref[...]).astype(h_ref.dtype)

    return pl.pallas_call(body, name=name, grid=(l // tl,), in_specs=in_specs, out_specs=row,
                          out_shape=jax.ShapeDtypeStruct((l, d), MXU_DTYPE))(x, nw, sc, sh)


def _residual_rms_mod(x, mix, g, nw, sc, sh, name):
    l, d = x.shape
    tl, row, _, in_specs = _row_specs(l, d, 2, 4)

    def body(x_ref, mix_ref, g_ref, nw_ref, sc_ref, sh_ref, x1_ref, h_ref):
        xv = x_ref[...] + g_ref[...] * mix_ref[...]
        x1_ref[...] = xv
        r = lax.rsqrt(jnp.mean(xv * xv, axis=-1, keepdims=True) + EPS)
        h_ref[...] = (((xv * r) * nw_ref[...]) * (1.0 + sc_ref[...]) + sh_ref[...]).astype(h_ref.dtype)

    return pl.pallas_call(body, name=name, grid=(l // tl,), in_specs=in_specs, out_specs=[row, row],
                          out_shape=[jax.ShapeDtypeStruct((l, d), F32), jax.ShapeDtypeStruct((l, d), MXU_DTYPE)],
                          )(x, mix, g, nw, sc, sh)


def _norm_bwd(x, dh, dres, nw, sc, name):
    l, d = x.shape
    tl, row, _, in_specs = _row_specs(l, d, 3, 2)

    def body(x_ref, dh_ref, dres_ref, nw_ref, sc_ref, dx_ref, st_ref):
        xv, dh_v = x_ref[...], dh_ref[...]
        r = lax.rsqrt(jnp.mean(xv * xv, axis=-1, keepdims=True) + EPS)
        xn = xv * r
        dxn = dh_v * (nw_ref[...] * (1.0 + sc_ref[...]))
        dx_ref[...] = dres_ref[...] + r * (dxn - xn * jnp.mean(dxn * xn, axis=-1, keepdims=True))

        @pl.when(pl.program_id(0) == 0)
        def _():
            st_ref[...] = jnp.zeros_like(st_ref)

        dhx = dh_v * xn
        st_ref[0:1, :] += _colsum(dhx * nw_ref[...])
        st_ref[1:2, :] += _colsum(dh_v)
        st_ref[2:3, :] += _colsum(dhx * (1.0 + sc_ref[...]))

    return pl.pallas_call(body, name=name, grid=(l // tl,), in_specs=in_specs,
                          out_specs=[row, pl.BlockSpec((8, d), lambda i: (0, 0))],
                          out_shape=[jax.ShapeDtypeStruct((l, d), F32), jax.ShapeDtypeStruct((8, d), F32)],
                          )(x, dh, dres, nw, sc)


def _gate_bwd(dres, val, g, name):
    l, d = dres.shape
    tl, row, _, in_specs = _row_specs(l, d, 2, 1)

    def body(dres_ref, val_ref, g_ref, dval_ref, st_ref):
        dr = dres_ref[...]
        dval_ref[...] = (g_ref[...] * dr).astype(dval_ref.dtype)

        @pl.when(pl.program_id(0) == 0)
        def _():
            st_ref[...] = jnp.zeros_like(st_ref)

        st_ref[0:1, :] += _colsum(dr * val_ref[...])

    return pl.pallas_call(body, name=name, grid=(l // tl,), in_specs=in_specs,
                          out_specs=[row, pl.BlockSpec((8, d), lambda i: (0, 0))],
                          out_shape=[jax.ShapeDtypeStruct((l, d), MXU_DTYPE), jax.ShapeDtypeStruct((8, d), F32)],
                          )(dres, val, g)


def _loss_head(x1, f, g, target, name):
    l, d = x1.shape
    tl, row, _, _ = _row_specs(l, d, 0, 0)
    vec = pl.BlockSpec((1, d), lambda i: (0, 0))

    def body(x1_ref, f_ref, g_ref, t_ref, dy_ref, loss_ref):
        e = x1_ref[...] + g_ref[...] * f_ref[...] - t_ref[...]
        dy_ref[...] = e * (1.0 / d)

        @pl.when(pl.program_id(0) == 0)
        def _():
            loss_ref[...] = jnp.zeros_like(loss_ref)

        s = jnp.sum(jnp.sum(e * e, axis=1, keepdims=True), axis=0, keepdims=True)
        loss_ref[...] += (0.5 / d) * s

    return pl.pallas_call(body, name=name, grid=(l // tl,), in_specs=[row, row, vec, row],
                          out_specs=[row, pl.BlockSpec((8, 128), lambda i: (0, 0))],
                          out_shape=[jax.ShapeDtypeStruct((l, d), F32), jax.ShapeDtypeStruct((8, 128), F32)],
                          )(x1, f, g, target)


def _swiglu(gate, up, name):
    l, f = gate.shape
    tl, tf = _pick(l, 512, 16), _pick(f, 1408, 128)
    spec = pl.BlockSpec((tl, tf), lambda i, j: (i, j))

    def body(g_ref, u_ref, a_ref):
        gv = g_ref[...]
        a_ref[...] = (gv * _sigmoid(gv) * u_ref[...]).astype(a_ref.dtype)

    return pl.pallas_call(body, name=name, grid=(l // tl, f // tf), in_specs=[spec, spec], out_specs=spec,
                          out_shape=jax.ShapeDtypeStruct((l, f), MXU_DTYPE))(gate, up)


def _swiglu_bwd(gate, up, da, name):
    l, f = gate.shape
    tl, tf = _pick(l, 512, 16), _pick(f, 1408, 128)
    spec = pl.BlockSpec((tl, tf), lambda i, j: (i, j))

    def body(g_ref, u_ref, da_ref, dg_ref, du_ref):
        gv, dav = g_ref[...], da_ref[...]
        s = _sigmoid(gv)
        dg_ref[...] = (dav * u_ref[...] * (s * (1.0 + gv * (1.0 - s)))).astype(dg_ref.dtype)
        du_ref[...] = (dav * (gv * s)).astype(du_ref.dtype)

    return pl.pallas_call(body, name=name, grid=(l // tl, f // tf), in_specs=[spec, spec, spec], out_specs=[spec, spec],
                          out_shape=[jax.ShapeDtypeStruct((l, f), MXU_DTYPE)] * 2)(gate, up, da)


def _conv_tile(l):
    return _pick(l, 512, 16)


def _conv_pre(buf, w_ref, b_ref, tl):
    acc = b_ref[...] + w_ref[3:4, :] * buf[HALO:HALO + tl, :]
    for k in range(CONV_WIDTH - 1):
        s = HALO - (CONV_WIDTH - 1) + k
        acc = acc + w_ref[k:k + 1, :] * buf[s:s + tl, :]
    return acc


def _fill_past(buf, u_ref, halo_ref, tl):
    i = pl.program_id(0)

    @pl.when(i == 0)
    def _():
        buf[0:HALO, :] = jnp.zeros((HALO, buf.shape[1]), F32)

    @pl.when(i > 0)
    def _():
        buf[0:HALO, :] = halo_ref[...]

    buf[HALO:HALO + tl, :] = u_ref[...]


def _conv_silu(u, w, b, name):
    l, ch = u.shape[0], w.shape[1]
    tl = _conv_tile(l)
    per = tl // HALO
    cur = pl.BlockSpec((tl, ch), lambda i: (i, 0))
    past = pl.BlockSpec((HALO, ch), lambda i: (jnp.maximum(i * per - 1, 0), 0))

    def body(u_ref, halo_ref, w_ref, b_ref, o_ref, buf):
        _fill_past(buf, u_ref, halo_ref, tl)
        pre = _conv_pre(buf, w_ref, b_ref, tl)
        o_ref[...] = pre * _sigmoid(pre)

    return pl.pallas_call(body, name=name, grid=(l // tl,),
                          in_specs=[cur, past, pl.BlockSpec((CONV_WIDTH, ch), lambda i: (0, 0)), pl.BlockSpec((1, ch), lambda i: (0, 0))],
                          out_specs=cur, out_shape=jax.ShapeDtypeStruct((l, ch), F32),
                          scratch_shapes=[pltpu.VMEM((tl + HALO, ch), F32)])(u, u, w, b)


def _conv_silu_bwd_pre(u, dxc, w, b, name):
    l, ch = u.shape[0], w.shape[1]
    tl = _conv_tile(l)
    per = tl // HALO
    cur = pl.BlockSpec((tl, ch), lambda i: (i, 0))
    past = pl.BlockSpec((HALO, ch), lambda i: (jnp.maximum(i * per - 1, 0), 0))

    def body(u_ref, halo_ref, d_ref, w_ref, b_ref, dpre_ref, st_ref, buf):
        _fill_past(buf, u_ref, halo_ref, tl)
        pre = _conv_pre(buf, w_ref, b_ref, tl)
        s = _sigmoid(pre)
        dpre = d_ref[...] * (s * (1.0 + pre * (1.0 - s)))
        dpre_ref[...] = dpre

        @pl.when(pl.program_id(0) == 0)
        def _():
            st_ref[...] = jnp.zeros_like(st_ref)

        for k in range(CONV_WIDTH):
            s0 = HALO - (CONV_WIDTH - 1) + k
            st_ref[k:k + 1, :] += _colsum(dpre * buf[s0:s0 + tl, :])
        st_ref[CONV_WIDTH:CONV_WIDTH + 1, :] += _colsum(dpre)

    return pl.pallas_call(body, name=name, grid=(l // tl,),
                          in_specs=[cur, past, cur, pl.BlockSpec((CONV_WIDTH, ch), lambda i: (0, 0)), pl.BlockSpec((1, ch), lambda i: (0, 0))],
                          out_specs=[cur, pl.BlockSpec((8, ch), lambda i: (0, 0))],
                          out_shape=[jax.ShapeDtypeStruct((l, ch), F32), jax.ShapeDtypeStruct((8, ch), F32)],
                          scratch_shapes=[pltpu.VMEM((tl + HALO, ch), F32)])(u, u, dxc, w, b)


def _conv_bwd_input(dpre, w, name):
    l, ch = dpre.shape
    tl = _conv_tile(l)
    per = tl // HALO
    nt = l // tl
    cur = pl.BlockSpec((tl, ch), lambda i: (i, 0))
    nxt = pl.BlockSpec((HALO, ch), lambda i: (jnp.minimum((i + 1) * per, l // HALO - 1), 0))

    def body(d_ref, halo_ref, w_ref, du_ref, buf):
        i = pl.program_id(0)
        buf[0:tl, :] = d_ref[...]

        @pl.when(i == nt - 1)
        def _():
            buf[tl:tl + HALO, :] = jnp.zeros((HALO, ch), F32)

        @pl.when(i < nt - 1)
        def _():
            buf[tl:tl + HALO, :] = halo_ref[...]

        acc = w_ref[3:4, :] * buf[0:tl, :]
        for k in range(CONV_WIDTH - 1):
            s = CONV_WIDTH - 1 - k
            acc = acc + w_ref[k:k + 1, :] * buf[s:s + tl, :]
        du_ref[...] = acc

    return pl.pallas_call(body, name=name, grid=(nt,),
                          in_specs=[cur, nxt, pl.BlockSpec((CONV_WIDTH, ch), lambda i: (0, 0))],
                          out_specs=cur, out_shape=jax.ShapeDtypeStruct((l, ch), F32),
                          scratch_shapes=[pltpu.VMEM((tl + HALO, ch), F32)])(dpre, dpre, w)


HEAD_TILE = 128
D_HEAD_TILES = N_HEADS * HEAD_TILE


def _chunk_iota():
    r = lax.broadcasted_iota(jnp.int32, (CHUNK, CHUNK), 0)
    c = lax.broadcasted_iota(jnp.int32, (CHUNK, CHUNK), 1)
    return r, c


def _split3_dot(v, ones_b):
    hi = v.astype(BF16)
    r1 = v - hi.astype(F32)
    mid = r1.astype(BF16)
    lo = (r1 - mid.astype(F32)).astype(BF16)
    return _dot(hi, ones_b) + _dot(mid, ones_b) + _dot(lo, ones_b)


def _ssd_decays(dtx_ref, dty_ref, dtbx_ref, dtby_ref, alx_ref, aly_ref, r, c):
    tri = (r >= c).astype(F32)
    dt_x = _softplus(dtx_ref[...] + dtbx_ref[...])
    a_x = -jnp.exp(alx_ref[...])
    adt_x = a_x * dt_x
    acs_x = _dot(tri, adt_x, HIGHEST)
    atot_x = _colsum(adt_x)
    acs_y = _dot(tri, -jnp.exp(aly_ref[...]) * _softplus(dty_ref[...] + dtby_ref[...]), HIGHEST)
    return dt_x, a_x, acs_y, jnp.exp(acs_x), jnp.exp(atot_x - acs_x), jnp.exp(atot_x)


def _head_decay(acs_y, e, r, c):
    col = acs_y[:, e * HEAD_TILE:(e + 1) * HEAD_TILE]
    return jnp.where(r >= c, jnp.exp(col - col.T), 0.0)


def _half_masks():
    lane = lax.broadcasted_iota(jnp.int32, (CHUNK, 2 * HEAD_DIM), 1)
    return lane < HEAD_DIM, lane >= HEAD_DIM


def _ssd_specs(nc, reverse):
    def at(i):
        return nc - 1 - i if reverse else i
    xc = pl.BlockSpec((CHUNK, D_CONV), lambda i: (at(i), 0))
    wide = pl.BlockSpec((CHUNK, D_SSD), lambda i: (at(i), 0))
    wide_y = pl.BlockSpec((CHUNK, D_HEAD_TILES), lambda i: (at(i), 0))
    vec = pl.BlockSpec((1, D_SSD), lambda i: (0, 0))
    vec_y = pl.BlockSpec((1, D_HEAD_TILES), lambda i: (0, 0))
    state = pl.BlockSpec((1, SSD_STATE, D_SSD), lambda i: (at(i), 0, 0))
    return xc, wide, wide_y, vec, vec_y, state


def _ssd_fwd(xc, dtr_x, dtr_y, dtb_x, dtb_y, al_x, al_y, name):
    l = xc.shape[0]
    nc = l // CHUNK
    xc_s, wide_s, wide_y_s, vec_s, vec_y_s, state_s = _ssd_specs(nc, False)
    pairs_per_group = GROUP_WIDTH // (2 * HEAD_DIM)

    def body(xc_ref, dtx_ref, dty_ref, dtbx_ref, dtby_ref, alx_ref, aly_ref, y_ref, sp_ref, state):
        @pl.when(pl.program_id(0) == 0)
        def _():
            state[...] = jnp.zeros_like(state)

        r, c = _chunk_iota()
        halves = _half_masks()
        dt_x, _, acs_y, ea_x, ds_x, eatot_x = _ssd_decays(dtx_ref, dty_ref, dtbx_ref, dtby_ref, alx_ref, aly_ref, r, c)
        xg = xc_ref[:, 0:D_SSD] * dt_x
        sp_ref[0] = state[...]
        for g in range(SSD_GROUPS):
            lanes = slice(g * GROUP_WIDTH, (g + 1) * GROUP_WIDTH)
            bb = xc_ref[:, D_SSD + g * SSD_STATE:D_SSD + (g + 1) * SSD_STATE].astype(MXU_DTYPE)
            cb = xc_ref[:, D_SSD + (SSD_GROUPS + g) * SSD_STATE:D_SSD + (SSD_GROUPS + g + 1) * SSD_STATE].astype(MXU_DTYPE)
            scores = _dot_nt(cb, bb)
            sg = state[:, lanes]
            ys = []
            for j in range(g * pairs_per_group, (g + 1) * pairs_per_group):
                xg_pair = xg[:, j * 2 * HEAD_DIM:(j + 1) * 2 * HEAD_DIM]
                acc = jnp.zeros((CHUNK, 2 * HEAD_DIM), F32)
                for half in range(2):
                    m = (scores * _head_decay(acs_y, 2 * j + half, r, c)).astype(MXU_DTYPE)
                    acc = acc + _dot(m, jnp.where(halves[half], xg_pair, 0.0).astype(MXU_DTYPE))
                ys.append(acc)
            y_ref[:, lanes] = jnp.concatenate(ys, axis=1) + _dot(cb, sg.astype(MXU_DTYPE)) * ea_x[:, lanes]
            state[:, lanes] = sg * eatot_x[:, lanes] + _dot_tn(bb, (xg[:, lanes] * ds_x[:, lanes]).astype(MXU_DTYPE))

    return pl.pallas_call(
        body, name=name, grid=(nc,),
        in_specs=[xc_s, wide_s, wide_y_s, vec_s, vec_y_s, vec_s, vec_y_s],
        out_specs=[wide_s, state_s],
        out_shape=[jax.ShapeDtypeStruct((l, D_SSD), F32), jax.ShapeDtypeStruct((nc, SSD_STATE, D_SSD), F32)],
        scratch_shapes=[pltpu.VMEM((SSD_STATE, D_SSD), F32)],
    )(xc, dtr_x, dtr_y, dtb_x, dtb_y, al_x, al_y)


def _ssd_bwd(xc, dtr_x, dtr_y, dtb_x, dtb_y, al_x, al_y, dy, states, head_ones, name):
    l = xc.shape[0]
    nc = l // CHUNK
    xc_s, wide_s, wide_y_s, vec_s, vec_y_s, state_s = _ssd_specs(nc, True)
    pairs_per_group = GROUP_WIDTH // (2 * HEAD_DIM)

    def body(xc_ref, dtx_ref, dty_ref, dtbx_ref, dtby_ref, alx_ref, aly_ref, dy_ref, sp_ref, ones_ref,
             dxc_ref, ddtr_ref, st_ref, hs_ref, dstate):
        @pl.when(pl.program_id(0) == 0)
        def _():
            dstate[...] = jnp.zeros_like(dstate)
            st_ref[...] = jnp.zeros_like(st_ref)

        r, c = _chunk_iota()
        halves = _half_masks()
        ones_b = ones_ref[...]
        dt_x, a_x, acs_y, ea_x, ds_x, eatot_x = _ssd_decays(dtx_ref, dty_ref, dtbx_ref, dtby_ref, alx_ref, aly_ref, r, c)
        xs = xc_ref[:, 0:D_SSD]
        xg = xs * dt_x
        gy = dy_ref[...]
        s_prev = sp_ref[0]
        gea = gy * ea_x
        xds = xg * ds_x
        ds_old = dstate[...]
        later2 = jnp.concatenate([(r <= c).astype(BF16)] * 2, axis=1)
        dxg_parts, state_term, yoff_parts, dadt_parts = [], [], [], []
        for g in range(SSD_GROUPS):
            lanes = slice(g * GROUP_WIDTH, (g + 1) * GROUP_WIDTH)
            b_lo = D_SSD + g * SSD_STATE
            c_lo = D_SSD + (SSD_GROUPS + g) * SSD_STATE
            bb = xc_ref[:, b_lo:b_lo + SSD_STATE].astype(MXU_DTYPE)
            cb = xc_ref[:, c_lo:c_lo + SSD_STATE].astype(MXU_DTYPE)
            scores = _dot_nt(cb, bb)
            dsg = ds_old[:, lanes].astype(MXU_DTYPE)
            gea_b = gea[:, lanes].astype(MXU_DTYPE)
            xds_b = xds[:, lanes].astype(MXU_DTYPE)
            dxg_state = _dot(bb, dsg) * ds_x[:, lanes]
            dc = _dot_nt(gea_b, s_prev[:, lanes].astype(MXU_DTYPE))
            db = _dot_nt(xds_b, dsg)
            dscores = jnp.zeros((CHUNK, CHUNK), F32)
            diag = []
            for j in range(g * pairs_per_group, (g + 1) * pairs_per_group):
                pair = slice(j * 2 * HEAD_DIM, (j + 1) * 2 * HEAD_DIM)
                xg_pair = xg[:, pair].astype(MXU_DTYPE)
                acc = jnp.zeros((CHUNK, 2 * HEAD_DIM), F32)
                cols = []
                for half in range(2):
                    decay = _head_decay(acs_y, 2 * j + half, r, c)
                    g_e = jnp.where(halves[half], gy[:, pair], 0.0).astype(MXU_DTYPE)
                    acc = acc + _dot_tn((scores * decay).astype(MXU_DTYPE), g_e)
                    dm = _dot_nt(g_e, xg_pair) * decay
                    dscores = dscores + dm
                    wq = dm * scores
                    hi = wq.astype(BF16)
                    lo = (wq - hi.astype(F32)).astype(BF16)
                    later = _dot(later2, jnp.concatenate([hi, lo], axis=0))
                    cols.append(jnp.sum(jnp.where(c < r, later, 0.0), axis=1, keepdims=True))
                diag.append(acc)
                dadt_parts.append(jnp.where(halves[0], cols[0], cols[1]))
            dsc_b = dscores.astype(MXU_DTYPE)
            dc = dc + _dot(dsc_b, bb)
            db = db + _dot_tn(dsc_b, cb)
            dxc_ref[:, b_lo:b_lo + SSD_STATE] = db
            dxc_ref[:, c_lo:c_lo + SSD_STATE] = dc
            dxg_parts.append(jnp.concatenate(diag, axis=1) + dxg_state)
            state_term.append(dxg_state)
            yoff_parts.append(_dot(cb, s_prev[:, lanes].astype(MXU_DTYPE)) * ea_x[:, lanes])
            dstate[:, lanes] = ds_old[:, lanes] * eatot_x[:, lanes] + _dot_tn(cb, gea_b)
        dxg = jnp.concatenate(dxg_parts, axis=1)
        dxc_ref[:, 0:D_SSD] = dxg * dt_x
        through_out = _dot((r <= c).astype(F32), gy * jnp.concatenate(yoff_parts, axis=1), HIGHEST)
        through_in = _dot((c < r).astype(F32), xg * jnp.concatenate(state_term, axis=1), HIGHEST)
        carried = jnp.broadcast_to(_colsum(ds_old * s_prev) * eatot_x, (8, D_SSD))
        dadt = (jnp.concatenate(dadt_parts, axis=1) + _split3_dot(through_out + through_in, ones_b)
                + jnp.max(_split3_dot(carried, ones_b), axis=0, keepdims=True))
        ddt = a_x * dadt + _split3_dot(dxg * xs, ones_b)
        draw = ddt * _sigmoid(dtx_ref[...] + dtbx_ref[...])
        ddtr_ref[...] = draw
        st_ref[0:1, :] += _colsum(dt_x * dadt) * a_x
        st_ref[1:2, :] += _colsum(draw)
        st_ref[2:3, :] += _colsum(gy * xs)

        @pl.when(pl.program_id(0) == nc - 1)
        def _():
            hs_ref[...] = _split3_dot(st_ref[...], ones_b)

    stats = pl.BlockSpec((8, D_SSD), lambda i: (0, 0))
    return pl.pallas_call(
        body, name=name, grid=(nc,),
        in_specs=[xc_s, wide_s, wide_y_s, vec_s, vec_y_s, vec_s, vec_y_s, wide_s, state_s,
                  pl.BlockSpec((D_SSD, D_SSD), lambda i: (0, 0))],
        out_specs=[xc_s, wide_s, stats, stats],
        out_shape=[jax.ShapeDtypeStruct((l, D_CONV), F32), jax.ShapeDtypeStruct((l, D_SSD), F32),
                   jax.ShapeDtypeStruct((8, D_SSD), F32), jax.ShapeDtypeStruct((8, D_SSD), F32)],
        scratch_shapes=[pltpu.VMEM((SSD_STATE, D_SSD), F32)],
    )(xc, dtr_x, dtr_y, dtb_x, dtb_y, al_x, al_y, dy, states, head_ones)


def _ssd_gate_norm(ycore, xc, z, dskip_x, norm_w, name):
    l = ycore.shape[0]
    tl = _pick(l, 512, 16)
    row = pl.BlockSpec((tl, D_SSD), lambda i: (i, 0))
    vec = pl.BlockSpec((1, D_SSD), lambda i: (0, 0))

    def body(y_ref, xs_ref, z_ref, dk_ref, nw_ref, o_ref):
        zv = z_ref[...]
        yv = (y_ref[...] + dk_ref[...] * xs_ref[...]) * (zv * _sigmoid(zv))
        for g in range(SSD_GROUPS):
            lanes = slice(g * GROUP_WIDTH, (g + 1) * GROUP_WIDTH)
            yg = yv[:, lanes]
            rg = lax.rsqrt(jnp.mean(yg * yg, axis=-1, keepdims=True) + EPS)
            o_ref[:, lanes] = (yg * rg * nw_ref[:, lanes]).astype(o_ref.dtype)

    z_spec = pl.BlockSpec((tl, D_SSD), lambda i: (i, O_Z // D_SSD))
    return pl.pallas_call(body, name=name, grid=(l // tl,), in_specs=[row, row, z_spec, vec, vec], out_specs=row,
                          out_shape=jax.ShapeDtypeStruct((l, D_SSD), MXU_DTYPE))(ycore, xc, z, dskip_x, norm_w)


def _ssd_gate_norm_bwd(dout, ycore, xc, z, dskip_x, norm_w, name):
    l = ycore.shape[0]
    tl = _pick(l, 512, 16)
    nt = l // tl
    row = pl.BlockSpec((tl, D_SSD), lambda i: (i, 0))
    vec = pl.BlockSpec((1, D_SSD), lambda i: (0, 0))

    def body(do_ref, y_ref, xs_ref, z_ref, dk_ref, nw_ref, dyc_ref, dxs_ref, dz_ref, st_ref):
        @pl.when(pl.program_id(0) == 0)
        def _():
            st_ref[...] = jnp.zeros_like(st_ref)

        zv, xs = z_ref[...], xs_ref[...]
        s = _sigmoid(zv)
        gz = zv * s
        yc = y_ref[...] + dk_ref[...] * xs
        yv = yc * gz
        dov = do_ref[...]
        dnw, dyv = [], []
        for g in range(SSD_GROUPS):
            lanes = slice(g * GROUP_WIDTH, (g + 1) * GROUP_WIDTH)
            yg = yv[:, lanes]
            rg = lax.rsqrt(jnp.mean(yg * yg, axis=-1, keepdims=True) + EPS)
            yn = yg * rg
            dnw.append(_colsum(dov[:, lanes] * yn))
            dyn = dov[:, lanes] * nw_ref[:, lanes]
            dyv.append(rg * (dyn - yn * jnp.mean(dyn * yn, axis=-1, keepdims=True)))
        dy = jnp.concatenate(dyv, axis=1)
        dyc = dy * gz
        dyc_ref[...] = dyc
        dxs_ref[...] = dyc * dk_ref[...]
        dz_ref[...] = dy * yc * (s * (1.0 + zv * (1.0 - s)))
        st_ref[0:1, :] += jnp.concatenate(dnw, axis=1)

    return pl.pallas_call(
        body, name=name, grid=(nt,), in_specs=[row, row, row, pl.BlockSpec((tl, D_SSD), lambda i: (i, O_Z // D_SSD)), vec, vec],
        out_specs=[row, row, row, pl.BlockSpec((8, D_SSD), lambda i: (0, 0))],
        out_shape=[jax.ShapeDtypeStruct((l, D_SSD), F32)] * 3 + [jax.ShapeDtypeStruct((8, D_SSD), F32)],
    )(dout, ycore, xc, z, dskip_x, norm_w)


PAIR = 2 * HEAD_DIM
N_PAIRS = N_HEADS // 2
Q_TILE0 = O_Q // PAIR
SB_KEYS = 1024
SB_SCALE = HEAD_DIM ** -0.5


def _pair_sum(v, lo):
    s_lo = jnp.sum(jnp.where(lo, v, 0.0), axis=-1, keepdims=True)
    s_hi = jnp.sum(jnp.where(lo, 0.0, v), axis=-1, keepdims=True)
    return jnp.where(lo, s_lo, s_hi)


def _qkv_prep(proj, qw2, kw2, name):
    l = proj.shape[0]
    tl = _pick(l, 1024, 16)
    out = pl.BlockSpec((tl, PAIR), lambda i, j: (i, j))
    vec = pl.BlockSpec((1, PAIR), lambda i, j: (0, 0))

    def at(which):
        return pl.BlockSpec((tl, PAIR), lambda i, j: (i, Q_TILE0 + which * N_PAIRS + j))

    def body(q_ref, k_ref, v_ref, qw_ref, kw_ref, qn_ref, kn_ref, vb_ref):
        lo = lax.broadcasted_iota(jnp.int32, (tl, PAIR), 1) < HEAD_DIM
        for t_ref, w_ref, o_ref, scale in ((q_ref, qw_ref, qn_ref, SB_SCALE), (k_ref, kw_ref, kn_ref, 1.0)):
            tv = t_ref[...]
            r = lax.rsqrt(_pair_sum(tv * tv, lo) * (1.0 / HEAD_DIM) + EPS)
            o_ref[...] = ((tv * r) * w_ref[...] * scale).astype(o_ref.dtype)
        vb_ref[...] = v_ref[...].astype(vb_ref.dtype)

    return pl.pallas_call(body, name=name, grid=(l // tl, N_PAIRS), in_specs=[at(0), at(1), at(2), vec, vec],
                          out_specs=[out, out, out], out_shape=[jax.ShapeDtypeStruct((l, D_SB), MXU_DTYPE)] * 3,
                          )(proj, proj, proj, qw2, kw2)


def _qk_norm_bwd(proj, dqn, dkn, qw2, kw2, name):
    l = proj.shape[0]
    tl = _pick(l, 1024, 16)
    out = pl.BlockSpec((tl, PAIR), lambda i, j: (i, j))
    vec = pl.BlockSpec((1, PAIR), lambda i, j: (0, 0))
    stats = pl.BlockSpec((8, PAIR), lambda i, j: (0, 0))

    def at(which):
        return pl.BlockSpec((tl, PAIR), lambda i, j: (i, Q_TILE0 + which * N_PAIRS + j))

    def body(q_ref, k_ref, dqn_ref, dkn_ref, qw_ref, kw_ref, dq_ref, dk_ref, stq_ref, stk_ref):
        @pl.when((pl.program_id(0) == 0) & (pl.program_id(1) == 0))
        def _():
            stq_ref[...] = jnp.zeros_like(stq_ref)
            stk_ref[...] = jnp.zeros_like(stk_ref)

        lo = lax.broadcasted_iota(jnp.int32, (tl, PAIR), 1) < HEAD_DIM
        for t_ref, dn_ref, w_ref, d_ref, st_ref, scale in ((q_ref, dqn_ref, qw_ref, dq_ref, stq_ref, SB_SCALE),
                                                           (k_ref, dkn_ref, kw_ref, dk_ref, stk_ref, 1.0)):
            tv = t_ref[...]
            r = lax.rsqrt(_pair_sum(tv * tv, lo) * (1.0 / HEAD_DIM) + EPS)
            tn = tv * r
            dnv = dn_ref[...] * scale
            dtn = dnv * w_ref[...]
            d_ref[...] = r * (dtn - tn * (_pair_sum(dtn * tn, lo) * (1.0 / HEAD_DIM)))
            st_ref[0:1, :] += _colsum(dnv * tn)

    return pl.pallas_call(body, name=name, grid=(l // tl, N_PAIRS), in_specs=[at(0), at(1), out, out, vec, vec],
                          out_specs=[out, out, stats, stats],
                          out_shape=[jax.ShapeDtypeStruct((l, D_SB), F32)] * 2 + [jax.ShapeDtypeStruct((8, PAIR), F32)] * 2,
                          )(proj, proj, dqn, dkn, qw2, kw2)


def _sb_masks():
    r = lax.broadcasted_iota(jnp.int32, (CHUNK, CHUNK), 0)
    c = lax.broadcasted_iota(jnp.int32, (CHUNK, CHUNK), 1)
    return r, c


def _stack2(mask):
    t = mask.astype(BF16)
    return jnp.concatenate([t, t], axis=0)


def _sb_fwd(q, k, v, name):
    l = q.shape[0]
    nq = l // CHUNK
    kt = _pick(l, SB_KEYS, CHUNK)
    sub = kt // CHUNK
    qblk = pl.BlockSpec((CHUNK, PAIR), lambda i, j: (j, i))
    full = pl.BlockSpec((l, PAIR), lambda i, j: (0, i))

    def body(q_ref, k_ref, v_ref, o_ref, tot_ref):
        qb = pl.program_id(1)
        r, c = _sb_masks()
        after2 = _stack2(r > c)
        halves = _half_masks()
        zero = jnp.zeros((CHUNK, PAIR), q_ref.dtype)
        qm = [jnp.where(halves[a], q_ref[...], zero) for a in range(2)]
        last = (qb * CHUNK) // kt

        def tile(t, width, carries, accs, masked):
            rows = pl.ds(pl.multiple_of(t * width, width), width)
            kv, vv = k_ref[rows, :], v_ref[rows, :]
            if masked:
                key_minus_query = (lax.broadcasted_iota(jnp.int32, (CHUNK, width), 1)
                                   - lax.broadcasted_iota(jnp.int32, (CHUNK, width), 0))
                keep = key_minus_query < qb * CHUNK - t * width
            new_carries, new_accs = [], []
            for a in range(2):
                lg = _dot_nt(qm[a], kv)
                sp = _softplus(lg)
                lr = jnp.where(keep, -sp, 0.0) if masked else -sp
                offset, parts = carries[a], [None] * (width // CHUNK)
                for j in reversed(range(width // CHUNK)):
                    piece = lr[:, j * CHUNK:(j + 1) * CHUNK]
                    parts[j] = _split_dot(piece, after2) + offset
                    offset = offset + jnp.sum(piece, axis=1, keepdims=True)
                w = jnp.exp(lg - sp + jnp.concatenate(parts, axis=1))
                if masked:
                    w = jnp.where(keep, w, 0.0)
                new_accs.append(accs[a] + _dot(w.astype(MXU_DTYPE), vv))
                new_carries.append(offset)
            return tuple(new_carries), tuple(new_accs)

        carries = tuple(jnp.zeros((CHUNK, 1), F32) for _ in range(2))
        accs = tuple(jnp.zeros((CHUNK, PAIR), F32) for _ in range(2))
        if sub % 2 == 0:
            half = kt // 2
            carries, accs = lax.cond((qb * CHUNK) % kt >= half,
                                     lambda cr, ac: tile(2 * last + 1, half, cr, ac, True), lambda cr, ac: (cr, ac), carries, accs)
            carries, accs = tile(2 * last, half, carries, accs, True)
        else:
            carries, accs = tile(last, kt, carries, accs, True)
        carries, accs = lax.fori_loop(1, last + 1, lambda i, st: tile(last - i, kt, st[0], st[1], False), (carries, accs))
        o_ref[...] = jnp.where(halves[0], accs[0], accs[1])
        tot_ref[...] = jnp.where(halves[0], carries[0], carries[1])

    return pl.pallas_call(
        body, name=name, grid=(N_PAIRS, nq), in_specs=[qblk, full, full],
        out_specs=[qblk, qblk],
        out_shape=[jax.ShapeDtypeStruct((l, D_SB), F32), jax.ShapeDtypeStruct((l, D_SB), F32)],
    )(q, k, v)


def _sb_bwd(q, k, v, dycat, tot, name):
    l = q.shape[0]
    nq = l // CHUNK
    kt = _pick(l, SB_KEYS, CHUNK)
    sub = kt // CHUNK
    qblk = pl.BlockSpec((CHUNK, PAIR), lambda i, j: (j, i))
    doblk = pl.BlockSpec((CHUNK, PAIR), lambda i, j: (j, D_SSD // PAIR + i))
    full = pl.BlockSpec((l, PAIR), lambda i, j: (0, i))

    def body(q_ref, k_ref, v_ref, do_ref, tot_ref, dq_ref, dk_ref, dv_ref):
        qb = pl.program_id(1)

        @pl.when(qb == 0)
        def _():
            dk_ref[...] = jnp.zeros_like(dk_ref)
            dv_ref[...] = jnp.zeros_like(dv_ref)

        r, c = _sb_masks()
        upto2 = _stack2(r <= c)
        before2 = _stack2(r < c)
        halves = _half_masks()
        qm = [jnp.where(halves[a], q_ref[...], jnp.zeros((CHUNK, PAIR), q_ref.dtype)) for a in range(2)]
        dom = [jnp.where(halves[a], do_ref[...], 0.0).astype(MXU_DTYPE) for a in range(2)]
        total = [jnp.max(jnp.where(halves[a], tot_ref[...], -jnp.inf), axis=1, keepdims=True) for a in range(2)]
        last = (qb * CHUNK) // kt

        def prefix(values, tri2, offset):
            parts = []
            for j in range(values.shape[1] // CHUNK):
                piece = values[:, j * CHUNK:(j + 1) * CHUNK]
                parts.append(_split_dot(piece, tri2) + offset)
                offset = offset + jnp.sum(piece, axis=1, keepdims=True)
            return jnp.concatenate(parts, axis=1), offset

        def tile(t, width, carry_p, carry_d, dq, masked):
            rows = pl.ds(pl.multiple_of(t * width, width), width)
            kv, vv = k_ref[rows, :], v_ref[rows, :]
            if masked:
                key_minus_query = (lax.broadcasted_iota(jnp.int32, (CHUNK, width), 1)
                                   - lax.broadcasted_iota(jnp.int32, (CHUNK, width), 0))
                keep = key_minus_query < qb * CHUNK - t * width
            new_p, new_d, new_dq = [], [], []
            dk_upd, dv_upd = None, None
            for a in range(2):
                lg = _dot_nt(qm[a], kv)
                sp = _softplus(lg)
                lr = jnp.where(keep, -sp, 0.0) if masked else -sp
                p_incl, p_next = prefix(lr, upto2, carry_p[a])
                ls = lg - sp
                w = jnp.exp(ls + (total[a] - p_incl))
                if masked:
                    w = jnp.where(keep, w, 0.0)
                da = _dot_nt(dom[a], vv) * w
                d_excl, d_next = prefix(da, before2, carry_d[a])
                sig = jnp.exp(ls)
                dl = da * (1.0 - sig) - d_excl * sig
                if masked:
                    dl = jnp.where(keep, dl, 0.0)
                dl_b = dl.astype(MXU_DTYPE)
                dv_a = _dot_tn(w.astype(MXU_DTYPE), dom[a])
                dk_a = _dot_tn(dl_b, qm[a])
                dv_upd = dv_a if dv_upd is None else dv_upd + dv_a
                dk_upd = dk_a if dk_upd is None else dk_upd + dk_a
                new_dq.append(dq[a] + _dot(dl_b, kv))
                new_p.append(p_next)
                new_d.append(d_next)
            dv_ref[rows, :] += dv_upd
            dk_ref[rows, :] += dk_upd
            return tuple(new_p), tuple(new_d), tuple(new_dq)

        zeros = tuple(jnp.zeros((CHUNK, 1), F32) for _ in range(2))
        dq0 = tuple(jnp.zeros((CHUNK, PAIR), F32) for _ in range(2))
        state = lax.fori_loop(0, last, lambda t, st: tile(t, kt, st[0], st[1], st[2], False), (zeros, zeros, dq0))
        if sub % 2 == 0:
            half = kt // 2
            state = tile(2 * last, half, *state, True)
            state = lax.cond((qb * CHUNK) % kt >= half,
                             lambda cp, cd, dq: tile(2 * last + 1, half, cp, cd, dq, True), lambda cp, cd, dq: (cp, cd, dq), *state)
        else:
            state = tile(last, kt, *state, True)
        dq_ref[...] = jnp.where(halves[0], state[2][0], state[2][1])

    return pl.pallas_call(
        body, name=name, grid=(N_PAIRS, nq),
        in_specs=[qblk, full, full, doblk, qblk],
        out_specs=[qblk, full, full],
        out_shape=[jax.ShapeDtypeStruct((l, D_SB), F32)] * 3,
    )(q, k, v, dycat, tot)


def _ada_fwd(c_all, w_shard, name):
    def body(c_ref, w_ref, o_ref):
        cv = c_ref[...]
        o_ref[...] = _dot(cv * _sigmoid(cv), w_ref[...], HIGHEST)

    return pl.pallas_call(body, name=name, out_shape=jax.ShapeDtypeStruct((c_all.shape[0], w_shard.shape[1]), F32))(c_all, w_shard)


def _ada_bwd(c_all, dmod_shard, name):
    def body(c_ref, d_ref, o_ref):
        cv = c_ref[...]
        o_ref[...] = _dot_tn(cv * _sigmoid(cv), d_ref[...], HIGHEST)

    return pl.pallas_call(body, name=name, out_shape=jax.ShapeDtypeStruct((c_all.shape[1], dmod_shard.shape[1]), F32))(c_all, dmod_shard)


def _sum_small(parts, name):
    def body(p_ref, o_ref):
        acc = p_ref[0]
        for d in range(1, N_DEV):
            acc = acc + p_ref[d]
        o_ref[...] = acc

    return pl.pallas_call(body, name=name, out_shape=jax.ShapeDtypeStruct(parts.shape[1:], F32))(parts)


def _adamw(w, g, m, v, name):
    rows, cols = w.shape
    tr = _pick(rows, 256, 8)
    spec = pl.BlockSpec((tr, cols), lambda i: (i, 0))
    bc1 = 1.0 - ADAM_B1 ** ADAM_STEP
    bc2 = 1.0 - ADAM_B2 ** ADAM_STEP

    def body(w_ref, g_ref, m_ref, v_ref, d_ref, nm_ref, nv_ref):
        gv = g_ref[...]
        nm = ADAM_B1 * m_ref[...] + (1.0 - ADAM_B1) * gv
        nv = ADAM_B2 * v_ref[...] + (1.0 - ADAM_B2) * (gv * gv)
        nm_ref[...] = nm
        nv_ref[...] = nv
        d_ref[...] = -ADAM_LR * ((nm / bc1) / (jnp.sqrt(nv / bc2) + ADAM_EPS) + ADAM_WD * w_ref[...])

    return pl.pallas_call(body, name=name, grid=(rows // tr,), in_specs=[spec] * 4, out_specs=[spec] * 3,
                          out_shape=[jax.ShapeDtypeStruct((rows, cols), F32)] * 3)(w, g, m, v)


def _pad_w_in_t(w_in_t):
    lo = D_SSD + D_CONV
    return jnp.concatenate([w_in_t[D_SSD:lo + N_HEADS], jnp.zeros((DT_PAD - N_HEADS, w_in_t.shape[1]), w_in_t.dtype),
                            w_in_t[:D_SSD], w_in_t[lo + N_HEADS:]], axis=0)


def _unpad_w_in_t(g):
    return jnp.concatenate([g[O_Z:O_Q], g[:O_DT + N_HEADS], g[O_Q:]], axis=0)


def _local_step(x, target, mod, norm1_w, w_in_tp, conv_w, conv_b, dt_bias, a_log, d_skip, ssd_norm_w, q_norm_w, k_norm_w,
                w_out, norm2_w, w_gate_t, w_up_t, w_down):
    sh1, sc1, g1, sh2, sc2, g2 = [mod[:, i * D_MODEL:(i + 1) * D_MODEL] for i in range(N_MOD)]
    qw2, kw2 = jnp.tile(q_norm_w, (1, 2)), jnp.tile(k_norm_w, (1, 2))
    dskip_x = jnp.repeat(d_skip, HEAD_DIM, axis=1)
    dtb_x, dtb_y = jnp.repeat(dt_bias, HEAD_DIM, axis=1), jnp.repeat(dt_bias, HEAD_TILE, axis=1)
    al_x, al_y = jnp.repeat(a_log, HEAD_DIM, axis=1), jnp.repeat(a_log, HEAD_TILE, axis=1)
    head_of_lane = jnp.arange(D_SSD, dtype=jnp.int32) // HEAD_DIM
    head_ones = (head_of_lane[:, None] == head_of_lane[None, :]).astype(BF16)

    h1 = _rms_mod(x, norm1_w, sc1, sh1, "rms_mod1")
    proj = _matmul_nt(h1, w_in_tp, "in_proj")
    dtr = proj[:, O_DT:O_DT + N_HEADS]
    dtr_x, dtr_y = jnp.repeat(dtr, HEAD_DIM, axis=1), jnp.repeat(dtr, HEAD_TILE, axis=1)

    xc = _conv_silu(proj, conv_w, conv_b, "conv_silu")
    ycore, states = _ssd_fwd(xc, dtr_x, dtr_y, dtb_x, dtb_y, al_x, al_y, "ssd_fwd")
    y_ssd = _ssd_gate_norm(ycore, xc, proj, dskip_x, ssd_norm_w, "ssd_gate_norm")

    qn, kn, vb = _qkv_prep(proj, qw2, kw2, "qkv_prep")
    o_sb, tot = _sb_fwd(qn, kn, vb, "sb_fwd")
    ycat = jnp.concatenate([y_ssd, o_sb.astype(MXU_DTYPE)], axis=1)

    mix = _matmul(ycat, w_out, "out_proj")
    x1, h2 = _residual_rms_mod(x, mix, g1, norm2_w, sc2, sh2, "residual_rms_mod2")
    gate = _matmul_nt(h2, w_gate_t, "ffn_gate")
    up = _matmul_nt(h2, w_up_t, "ffn_up")
    act = _swiglu(gate, up, "swiglu")
    ffn = _matmul(act, w_down, "ffn_down")
    dy, loss_blk = _loss_head(x1, ffn, g2, target, "loss_head")

    dffn, st_g2 = _gate_bwd(dy, ffn, g2, "gate2_bwd")
    g_down = _matmul_tn(act, dffn, "ffn_down_dw", MXU_DTYPE)
    dact = _matmul_nt(dffn, w_down, "ffn_down_dx")
    dgate, dup = _swiglu_bwd(gate, up, dact, "swiglu_bwd")
    g_gate_t = _matmul_tn(dgate, h2, "ffn_gate_dw", MXU_DTYPE)
    g_up_t = _matmul_tn(dup, h2, "ffn_up_dw", MXU_DTYPE)
    dh2 = _matmul_sum2(dgate, w_gate_t, dup, w_up_t, "ffn_dh")
    dx1, st_n2 = _norm_bwd(x1, dh2, dy, norm2_w, sc2, "norm2_bwd")

    dmix, st_g1 = _gate_bwd(dx1, mix, g1, "gate1_bwd")
    g_out = _matmul_tn(ycat, dmix, "out_proj_dw", MXU_DTYPE)
    dycat = _matmul_nt(dmix, w_out, "out_proj_dx")

    dqn, dkn, dv = _sb_bwd(qn, kn, vb, dycat, tot, "sb_bwd")
    dq, dk, st_q, st_k = _qk_norm_bwd(proj, dqn, dkn, qw2, kw2, "qk_norm_bwd")

    dycore, dxs_skip, dz, st_gn = _ssd_gate_norm_bwd(dycat, ycore, xc, proj, dskip_x, ssd_norm_w, "ssd_gate_norm_bwd")
    dxc, ddtr_x, st_ssd, st_heads = _ssd_bwd(xc, dtr_x, dtr_y, dtb_x, dtb_y, al_x, al_y, dycore, states, head_ones, "ssd_bwd")
    ddtr = ddtr_x[:, ::HEAD_DIM]
    dxc = jnp.concatenate([dxc[:, :D_SSD] + dxs_skip, dxc[:, D_SSD:]], axis=1)
    dpre, st_conv = _conv_silu_bwd_pre(proj, dxc, conv_w, conv_b, "conv_silu_bwd")
    dxbc = _conv_bwd_input(dpre, conv_w, "conv_bwd_input")

    dproj = jnp.concatenate([dxbc, ddtr, jnp.zeros((x.shape[0], DT_PAD - N_HEADS), F32), dz, dq, dk, dv], axis=1).astype(MXU_DTYPE)
    g_in_t = _unpad_w_in_t(_matmul_tn(dproj, h1, "in_proj_dw", MXU_DTYPE))
    dh1 = _matmul(dproj, w_in_tp, "in_proj_dx")
    grad_x, st_n1 = _norm_bwd(x, dh1, dx1, norm1_w, sc1, "norm1_bwd")

    pad = jnp.zeros((1, SM_SSD_NORM - SM_D_SKIP - N_HEADS), F32)
    small = jnp.concatenate(
        [st_n1[1:2], st_n1[0:1], st_g1[0:1], st_n2[1:2], st_n2[0:1], st_g2[0:1],
         st_n1[2:3], st_conv[4:5], st_conv[0:4].reshape(1, CONV_WIDTH * D_CONV),
         st_ssd[1:2, ::HEAD_DIM], st_ssd[0:1, ::HEAD_DIM], st_heads[2:3, ::HEAD_DIM], pad,
         st_gn[0:1], st_q[0:1, :HEAD_DIM] + st_q[0:1, HEAD_DIM:], st_k[0:1, :HEAD_DIM] + st_k[0:1, HEAD_DIM:], st_n2[2:3]], axis=1)
    return loss_blk, grad_x, (g_in_t, g_out, g_gate_t, g_up_t, g_down), small


def kernel(x, c, w_ada, b_ada, norm1_w, w_in, conv_w, conv_b, dt_bias, a_log, d_skip, ssd_norm_w, q_norm_w, k_norm_w, w_out, norm2_w, w_gate, w_up, w_down, loss_target, m_w_ada, m_b_ada, m_norm1_w, m_w_in, m_conv_w, m_conv_b, m_dt_bias, m_a_log, m_d_skip, m_ssd_norm_w, m_q_norm_w, m_k_norm_w, m_w_out, m_norm2_w, m_w_gate, m_w_up, m_w_down, v_w_ada, v_b_ada, v_norm1_w, v_w_in, v_conv_w, v_conv_b, v_dt_bias, v_a_log, v_d_skip, v_ssd_norm_w, v_q_norm_w, v_k_norm_w, v_w_out, v_norm2_w, v_w_gate, v_w_up, v_w_down):
    me = 4 * lax.axis_index("x") + 2 * lax.axis_index("y") + lax.axis_index("c")
    conv_cols = D_CONV // N_DEV
    ada_cols = N_MOD * D_MODEL // N_DEV

    shards = [w_in[0].T, w_out[0], w_gate[0].T, w_up[0].T, w_down[0]]
    gathered = _all_gather_big([s.astype(BF16) for s in shards], "gather_weights")
    w_in_t, w_out_f, w_gate_t, w_up_t, w_down_f = [g.reshape(N_DEV * g.shape[1], D_MODEL) for g in gathered]
    w_in_tp = _pad_w_in_t(w_in_t)

    first = _all_gather_small(jnp.concatenate([c, conv_w[0].reshape(1, CONV_WIDTH * conv_cols)], axis=1), "gather_cond")
    c_all = first[:, 0, :D_MODEL]
    conv_w_f = first[:, 0, D_MODEL:].reshape(N_DEV, CONV_WIDTH, conv_cols).transpose(1, 0, 2).reshape(CONV_WIDTH, D_CONV)
    mod_cols = _ada_fwd(c_all, w_ada[0], "ada_fwd")
    mods = _all_gather_small(mod_cols.reshape(1, N_DEV * ada_cols), "gather_mod")
    mod = lax.dynamic_index_in_dim(mods.reshape(N_DEV, N_DEV, ada_cols), me, axis=1, keepdims=False)
    mod = mod.reshape(1, N_MOD * D_MODEL) + b_ada

    loss_blk, grad_x, big, small = _local_step(
        x[0], loss_target[0], mod, norm1_w, w_in_tp, conv_w_f, conv_b, dt_bias, a_log, d_skip, ssd_norm_w, q_norm_w, k_norm_w,
        w_out_f, norm2_w, w_gate_t, w_up_t, w_down_f)
    loss = lax.psum(loss_blk[0, 0], ("x", "y", "c"))

    slots = _all_to_all_big([g.reshape(N_DEV, g.shape[0] // N_DEV, D_MODEL).astype(BF16) for g in big], "scatter_grads")
    g_in_t, g_out, g_gate_t, g_up_t, g_down = [_sum_slots(s, "sum_grads_" + n) for s, n in zip(slots, ("in", "out", "gate", "up", "down"))]
    g_in, g_gate, g_up = g_in_t.T, g_gate_t.T, g_up_t.T

    parts = _all_gather_small(small, "gather_small")
    gsum = _sum_small(parts, "sum_small")
    dmod_shard = lax.dynamic_slice_in_dim(parts[:, 0, :N_MOD * D_MODEL], me * ada_cols, ada_cols, axis=1)
    g_ada = _ada_bwd(c_all, dmod_shard, "ada_bwd")
    g_conv_w = lax.dynamic_slice_in_dim(gsum[:, SM_CONV_W:SM_DT_BIAS].reshape(CONV_WIDTH, D_CONV), me * conv_cols, conv_cols, axis=1)

    def pack_small(b_ada_, norm1_, conv_b_, dt_bias_, a_log_, d_skip_, ssd_norm_, q_norm_, k_norm_, norm2_):
        return jnp.concatenate(
            [b_ada_, norm1_, conv_b_, jnp.zeros((1, CONV_WIDTH * D_CONV), F32), dt_bias_, a_log_, d_skip_,
             jnp.zeros((1, SM_SSD_NORM - SM_D_SKIP - N_HEADS), F32), ssd_norm_, q_norm_, k_norm_, norm2_], axis=1)

    def unpack_small(p):
        return {
            "b_ada": p[:, SM_B_ADA:SM_NORM1], "norm1_w": p[:, SM_NORM1:SM_CONV_B], "conv_b": p[:, SM_CONV_B:SM_CONV_W],
            "dt_bias": p[:, SM_DT_BIAS:SM_A_LOG], "a_log": p[:, SM_A_LOG:SM_D_SKIP], "d_skip": p[:, SM_D_SKIP:SM_D_SKIP + N_HEADS],
            "ssd_norm_w": p[:, SM_SSD_NORM:SM_Q_NORM], "q_norm_w": p[:, SM_Q_NORM:SM_K_NORM], "k_norm_w": p[:, SM_K_NORM:SM_NORM2],
            "norm2_w": p[:, SM_NORM2:SM_TOTAL]}

    w_small = pack_small(b_ada, norm1_w, conv_b, dt_bias, a_log, d_skip, ssd_norm_w, q_norm_w, k_norm_w, norm2_w)
    m_small = pack_small(m_b_ada, m_norm1_w, m_conv_b, m_dt_bias, m_a_log, m_d_skip, m_ssd_norm_w, m_q_norm_w, m_k_norm_w, m_norm2_w)
    v_small = pack_small(v_b_ada, v_norm1_w, v_conv_b, v_dt_bias, v_a_log, v_d_skip, v_ssd_norm_w, v_q_norm_w, v_k_norm_w, v_norm2_w)
    small_out = [unpack_small(t) for t in (gsum,) + tuple(_adamw(w_small, gsum, m_small, v_small, "adamw_small"))]

    sharded = {
        "w_ada": (w_ada[0], g_ada, m_w_ada[0], v_w_ada[0]),
        "w_in": (w_in[0], g_in, m_w_in[0], v_w_in[0]),
        "conv_w": (conv_w[0], g_conv_w, m_conv_w[0], v_conv_w[0]),
        "w_out": (w_out[0], g_out, m_w_out[0], v_w_out[0]),
        "w_gate": (w_gate[0], g_gate, m_w_gate[0], v_w_gate[0]),
        "w_up": (w_up[0], g_up, m_w_up[0], v_w_up[0]),
        "w_down": (w_down[0], g_down, m_w_down[0], v_w_down[0]),
    }
    sharded_out = {n: (t[1],) + tuple(_adamw(*t, "adamw_" + n)) for n, t in sharded.items()}

    names = ["w_ada", "b_ada", "norm1_w", "w_in", "conv_w", "conv_b", "dt_bias", "a_log", "d_skip", "ssd_norm_w", "q_norm_w",
             "k_norm_w", "w_out", "norm2_w", "w_gate", "w_up", "w_down"]
    outs = [loss, grad_x[None]]
    for kind in range(4):
        for n in names:
            outs.append(sharded_out[n][kind][None] if n in sharded_out else small_out[kind][n])
    return tuple(outs)
```

```python
import functools

import jax
import jax.numpy as jnp
from jax import lax
from jax.experimental import pallas as pl
from jax.experimental.pallas import tpu as pltpu

F32 = jnp.float32
BF16 = jnp.bfloat16
MXU_DTYPE = jnp.bfloat16
HIGHEST = lax.Precision.HIGHEST
MESH_IDS = pl.DeviceIdType.MESH

N_DEV = 8
D_MODEL = 1024
HEAD_DIM = 64
N_HEADS = 16
D_SSD = 1024
D_SB = 1024
SSD_GROUPS = 2
SSD_STATE = 128
GROUP_WIDTH = D_SSD // SSD_GROUPS
D_CONV = D_SSD + 2 * SSD_GROUPS * SSD_STATE
CONV_WIDTH = 4
CHUNK = 128
D_FF = 2816
N_MOD = 6
EPS = 1e-6
D_IN_PROJ = 5648
O_XBC = 0
O_DT = D_CONV
DT_PAD = 512
O_Z = O_DT + DT_PAD
O_Q = O_Z + D_SSD
D_IN_PAD = O_Q + 3 * D_SB
HALO = 8

ADAM_LR = 0.001
ADAM_B1 = 0.9
ADAM_B2 = 0.999
ADAM_EPS = 1e-08
ADAM_WD = 0.01
ADAM_STEP = 10

ROWS_W_IN = 706
ROWS_W_OUT = 256
ROWS_FF = 352
PACK_ROWS = 2048
SM_B_ADA = 0
SM_NORM1 = 6144
SM_CONV_B = 7168
SM_CONV_W = 8704
SM_DT_BIAS = 14848
SM_A_LOG = 14864
SM_D_SKIP = 14880
SM_SSD_NORM = 14976
SM_Q_NORM = 16000
SM_K_NORM = 16064
SM_NORM2 = 16128
SM_TOTAL = 17152


def _pick(n, cap, mult):
    if n <= cap:
        return n
    best = None
    for t in range(mult, cap + 1, mult):
        if n % t == 0:
            best = t
    assert best is not None, (n, cap, mult)
    return best


def _dot(a, b, precision=None):
    return jnp.dot(a, b, preferred_element_type=F32, precision=precision)


def _dot_nt(a, b, precision=None):
    return lax.dot_general(a, b, (((1,), (1,)), ((), ())), preferred_element_type=F32, precision=precision)


def _dot_tn(a, b, precision=None):
    return lax.dot_general(a, b, (((0,), (0,)), ((), ())), preferred_element_type=F32, precision=precision)


def _softplus(v):
    return jnp.maximum(v, 0.0) + jnp.log1p(jnp.exp(-jnp.abs(v)))


def _sigmoid(v):
    return jax.nn.sigmoid(v)


def _colsum(v):
    return jnp.sum(v, axis=0, keepdims=True)


def _split_dot(v, tri2):
    hi = v.astype(BF16)
    lo = (v - hi.astype(F32)).astype(BF16)
    return _dot(jnp.concatenate([hi, lo], axis=1), tri2)


def _position():
    x, y, c = lax.axis_index("x"), lax.axis_index("y"), lax.axis_index("c")
    return x, y, c


def _peer(x, y, c, k):
    px = 1 - x if (k >> 2) & 1 else x
    py = 1 - y if (k >> 1) & 1 else y
    pc = 1 - c if k & 1 else c
    return px, py, pc


def _all_gather_small(v, name):
    n = v.shape[1]

    def body(v_ref, out_ref, send_sems, recv_sems, local_sem):
        x, y, c = _position()
        me = 4 * x + 2 * y + c
        mine = pltpu.make_async_copy(v_ref, out_ref.at[me], local_sem)
        mine.start()
        sends = []
        for k in range(1, N_DEV):
            cp = pltpu.make_async_remote_copy(
                src_ref=v_ref, dst_ref=out_ref.at[me], send_sem=send_sems.at[k - 1], recv_sem=recv_sems.at[k - 1],
                device_id=_peer(x, y, c, k), device_id_type=MESH_IDS)
            cp.start()
            sends.append(cp)
        for k in range(1, N_DEV):
            px, py, pc = _peer(x, y, c, k)
            pltpu.make_async_remote_copy(
                src_ref=v_ref, dst_ref=out_ref.at[4 * px + 2 * py + pc], send_sem=send_sems.at[k - 1],
                recv_sem=recv_sems.at[k - 1], device_id=(px, py, pc), device_id_type=MESH_IDS).wait_recv()
        for cp in sends:
            cp.wait_send()
        mine.wait()

    return pl.pallas_call(
        body, name=name,
        out_shape=jax.ShapeDtypeStruct((N_DEV, 1, n), v.dtype),
        in_specs=[pl.BlockSpec(memory_space=pltpu.VMEM)],
        out_specs=pl.BlockSpec(memory_space=pltpu.VMEM),
        scratch_shapes=[pltpu.SemaphoreType.DMA((N_DEV - 1,)), pltpu.SemaphoreType.DMA((N_DEV - 1,)),
                        pltpu.SemaphoreType.DMA],
    )(v)


def _all_gather_big(blocks, name):
    na = len(blocks)
    copies = N_DEV - 1

    def body(*refs):
        b_refs, out_refs = refs[:na], refs[na:2 * na]
        send_sems, recv_sems, local_sems = refs[2 * na:]
        x, y, c = _position()
        me, sibling = (x, y, c), (x, y, 1 - c)
        chips = [(1 - x, y), (x, 1 - y), (1 - x, 1 - y)]

        def copy(a, k, blk, to, own=False):
            slot = out_refs[a].at[4 * blk[0] + 2 * blk[1] + blk[2]]
            return pltpu.make_async_remote_copy(
                src_ref=b_refs[a] if own else slot, dst_ref=slot,
                send_sem=send_sems.at[copies * a + k], recv_sem=recv_sems.at[copies * a + k], device_id=to, device_id_type=MESH_IDS)

        mine = [pltpu.make_async_copy(b_refs[a], out_refs[a].at[4 * x + 2 * y + c], local_sems.at[a]) for a in range(na)]
        for cp in mine:
            cp.start()
        first = [copy(a, 0, me, sibling, own=True) for a in range(na)]
        first += [copy(a, 1 + j, me, (*chip, c), own=True) for j, chip in enumerate(chips) for a in range(na)]
        for cp in first:
            cp.start()
        passed = []
        for j, chip in enumerate(chips):
            for a in range(na):
                copy(a, 1 + j, (*chip, c), me).wait_recv()
                passed.append(copy(a, 4 + j, (*chip, c), sibling))
                passed[-1].start()
        for a in range(na):
            copy(a, 0, sibling, me).wait_recv()
        for j, chip in enumerate(chips):
            for a in range(na):
                copy(a, 4 + j, (*chip, 1 - c), me).wait_recv()
        for cp in first + passed:
            cp.wait_send()
        for cp in mine:
            cp.wait()

    hbm = pl.BlockSpec(memory_space=pl.ANY)
    return pl.pallas_call(
        body, name=name,
        out_shape=[jax.ShapeDtypeStruct((N_DEV,) + b.shape, b.dtype) for b in blocks],
        in_specs=[hbm] * na, out_specs=[hbm] * na,
        scratch_shapes=[pltpu.SemaphoreType.DMA((copies * na,)), pltpu.SemaphoreType.DMA((copies * na,)),
                        pltpu.SemaphoreType.DMA((na,))],
    )(*blocks)


def _all_to_all_big(blocks, name):
    na = len(blocks)
    copies = N_DEV - 1

    def body(*refs):
        b_refs, out_refs = refs[:na], refs[na:2 * na]
        send_sems, recv_sems, local_sems = refs[2 * na:]
        x, y, c = _position()
        me = 4 * x + 2 * y + c
        mine = [pltpu.make_async_copy(b_refs[a].at[me], out_refs[a].at[me], local_sems.at[a]) for a in range(na)]
        for cp in mine:
            cp.start()
        sends = []
        for k in range(1, N_DEV):
            px, py, pc = _peer(x, y, c, k)
            for a in range(na):
                cp = pltpu.make_async_remote_copy(
                    src_ref=b_refs[a].at[4 * px + 2 * py + pc], dst_ref=out_refs[a].at[me], send_sem=send_sems.at[copies * a + k - 1],
                    recv_sem=recv_sems.at[copies * a + k - 1], device_id=(px, py, pc), device_id_type=MESH_IDS)
                cp.start()
                sends.append(cp)
        for k in range(1, N_DEV):
            px, py, pc = _peer(x, y, c, k)
            for a in range(na):
                pltpu.make_async_remote_copy(
                    src_ref=b_refs[a].at[me], dst_ref=out_refs[a].at[4 * px + 2 * py + pc], send_sem=send_sems.at[copies * a + k - 1],
                    recv_sem=recv_sems.at[copies * a + k - 1], device_id=(px, py, pc), device_id_type=MESH_IDS).wait_recv()
        for cp in sends:
            cp.wait_send()
        for cp in mine:
            cp.wait()

    hbm = pl.BlockSpec(memory_space=pl.ANY)
    return pl.pallas_call(
        body, name=name,
        out_shape=[jax.ShapeDtypeStruct(b.shape, b.dtype) for b in blocks],
        in_specs=[hbm] * na, out_specs=[hbm] * na,
        scratch_shapes=[pltpu.SemaphoreType.DMA((copies * na,)), pltpu.SemaphoreType.DMA((copies * na,)),
                        pltpu.SemaphoreType.DMA((na,))],
    )(*blocks)


def _sum_slots(slots, name):
    _, r, n = slots.shape
    tr = _pick(r, 256, 16) if r % 16 == 0 else r

    def body(s_ref, o_ref):
        acc = s_ref[0].astype(F32)
        for d in range(1, N_DEV):
            acc = acc + s_ref[d].astype(F32)
        o_ref[...] = acc

    return pl.pallas_call(
        body, name=name, grid=(r // tr,),
        in_specs=[pl.BlockSpec((N_DEV, tr, n), lambda i: (0, i, 0))],
        out_specs=pl.BlockSpec((tr, n), lambda i: (i, 0)),
        out_shape=jax.ShapeDtypeStruct((r, n), F32),
    )(slots)


def _matmul(a, b, name, out_dtype=F32):
    m, k = a.shape
    _, n = b.shape
    tm, tn, tk = _pick(m, 1024, 16), _pick(n, 1408, 128), _pick(k, 1408, 128)
    nk = k // tk

    def body(a_ref, b_ref, o_ref, acc_ref):
        kk = pl.program_id(2)

        @pl.when(kk == 0)
        def _():
            acc_ref[...] = jnp.zeros_like(acc_ref)

        acc_ref[...] += _dot(a_ref[...].astype(MXU_DTYPE), b_ref[...].astype(MXU_DTYPE))

        @pl.when(kk == nk - 1)
        def _():
            o_ref[...] = acc_ref[...].astype(o_ref.dtype)

    return pl.pallas_call(
        body, name=name, grid=(m // tm, n // tn, nk),
        in_specs=[pl.BlockSpec((tm, tk), lambda i, j, kk: (i, kk)), pl.BlockSpec((tk, tn), lambda i, j, kk: (kk, j))],
        out_specs=pl.BlockSpec((tm, tn), lambda i, j, kk: (i, j)),
        out_shape=jax.ShapeDtypeStruct((m, n), out_dtype),
        scratch_shapes=[pltpu.VMEM((tm, tn), F32)],
    )(a, b)


def _matmul_nt(a, bt, name, out_dtype=F32):
    m, k = a.shape
    n, _ = bt.shape
    tm, tn, tk = _pick(m, 1024, 16), _pick(n, 1408, 128), _pick(k, 1408, 128)
    nk = k // tk

    def body(a_ref, b_ref, o_ref, acc_ref):
        kk = pl.program_id(2)

        @pl.when(kk == 0)
        def _():
            acc_ref[...] = jnp.zeros_like(acc_ref)

        acc_ref[...] += _dot_nt(a_ref[...].astype(MXU_DTYPE), b_ref[...].astype(MXU_DTYPE))

        @pl.when(kk == nk - 1)
        def _():
            o_ref[...] = acc_ref[...].astype(o_ref.dtype)

    return pl.pallas_call(
        body, name=name, grid=(m // tm, n // tn, nk),
        in_specs=[pl.BlockSpec((tm, tk), lambda i, j, kk: (i, kk)), pl.BlockSpec((tn, tk), lambda i, j, kk: (j, kk))],
        out_specs=pl.BlockSpec((tm, tn), lambda i, j, kk: (i, j)),
        out_shape=jax.ShapeDtypeStruct((m, n), out_dtype),
        scratch_shapes=[pltpu.VMEM((tm, tn), F32)],
    )(a, bt)


def _matmul_sum2(a1, b1, a2, b2, name):
    m, k = a1.shape
    _, n = b1.shape
    tm, tn, tk = _pick(m, 1024, 16), _pick(n, 1408, 128), _pick(k, 1408, 128)
    nk = k // tk

    def body(a1_ref, b1_ref, a2_ref, b2_ref, o_ref):
        @pl.when(pl.program_id(2) == 0)
        def _():
            o_ref[...] = jnp.zeros_like(o_ref)

        o_ref[...] += (_dot(a1_ref[...].astype(MXU_DTYPE), b1_ref[...].astype(MXU_DTYPE))
                       + _dot(a2_ref[...].astype(MXU_DTYPE), b2_ref[...].astype(MXU_DTYPE)))

    a_spec = pl.BlockSpec((tm, tk), lambda i, j, kk: (i, kk))
    b_spec = pl.BlockSpec((tk, tn), lambda i, j, kk: (kk, j))
    return pl.pallas_call(
        body, name=name, grid=(m // tm, n // tn, nk), in_specs=[a_spec, b_spec, a_spec, b_spec],
        out_specs=pl.BlockSpec((tm, tn), lambda i, j, kk: (i, j)),
        out_shape=jax.ShapeDtypeStruct((m, n), F32),
    )(a1, b1, a2, b2)


def _matmul_tn(a, b, name, out_dtype=F32):
    l, m = a.shape
    _, n = b.shape
    tm, tn, tl = _pick(m, 1408, 128), _pick(n, 1408, 128), _pick(l, 512, 16)
    nl = l // tl

    def body(a_ref, b_ref, o_ref, acc_ref):
        ll = pl.program_id(2)

        @pl.when(ll == 0)
        def _():
            acc_ref[...] = jnp.zeros_like(acc_ref)

        acc_ref[...] += _dot_tn(a_ref[...].astype(MXU_DTYPE), b_ref[...].astype(MXU_DTYPE))

        @pl.when(ll == nl - 1)
        def _():
            o_ref[...] = acc_ref[...].astype(o_ref.dtype)

    return pl.pallas_call(
        body, name=name, grid=(m // tm, n // tn, nl),
        in_specs=[pl.BlockSpec((tl, tm), lambda i, j, ll: (ll, i)), pl.BlockSpec((tl, tn), lambda i, j, ll: (ll, j))],
        out_specs=pl.BlockSpec((tm, tn), lambda i, j, ll: (i, j)),
        out_shape=jax.ShapeDtypeStruct((m, n), out_dtype),
        scratch_shapes=[pltpu.VMEM((tm, tn), F32)],
    )(a, b)


def _row_specs(l, d, n_rows, n_vecs):
    tl = _pick(l, 512, 16)
    row = pl.BlockSpec((tl, d), lambda i: (i, 0))
    vec = pl.BlockSpec((1, d), lambda i: (0, 0))
    return tl, row, vec, [row] * n_rows + [vec] * n_vecs


def _rms_mod(x, nw, sc, sh, name):
    l, d = x.shape
    tl, row, _, in_specs = _row_specs(l, d, 1, 3)

    def body(x_ref, nw_ref, sc_ref, sh_ref, h_ref):
        xv = x_ref[...]
        r = lax.rsqrt(jnp.mean(xv * xv, axis=-1, keepdims=True) + EPS)
        h_ref[...] = (((xv * r) * nw_ref[...]) * (1.0 + sc_ref[...]) + sh_ref[...]).astype(h_ref.dtype)

    return pl.pallas_call(body, name=name, grid=(l // tl,), in_specs=in_specs, out_specs=row,
                          out_shape=jax.ShapeDtypeStruct((l, d), MXU_DTYPE))(x, nw, sc, sh)


def _residual_rms_mod(x, mix, g, nw, sc, sh, name):
    l, d = x.shape
    tl, row, _, in_specs = _row_specs(l, d, 2, 4)

    def body(x_ref, mix_ref, g_ref, nw_ref, sc_ref, sh_ref, x1_ref, h_ref):
        xv = x_ref[...] + g_ref[...] * mix_ref[...]
        x1_ref[...] = xv
        r = lax.rsqrt(jnp.mean(xv * xv, axis=-1, keepdims=True) + EPS)
        h_ref[...] = (((xv * r) * nw_ref[...]) * (1.0 + sc_ref[...]) + sh_ref[...]).astype(h_ref.dtype)

    return pl.pallas_call(body, name=name, grid=(l // tl,), in_specs=in_specs, out_specs=[row, row],
                          out_shape=[jax.ShapeDtypeStruct((l, d), F32), jax.ShapeDtypeStruct((l, d), MXU_DTYPE)],
                          )(x, mix, g, nw, sc, sh)


def _norm_bwd(x, dh, dres, nw, sc, name):
    l, d = x.shape
    tl, row, _, in_specs = _row_specs(l, d, 3, 2)

    def body(x_ref, dh_ref, dres_ref, nw_ref, sc_ref, dx_ref, st_ref):
        xv, dh_v = x_ref[...], dh_ref[...]
        r = lax.rsqrt(jnp.mean(xv * xv, axis=-1, keepdims=True) + EPS)
        xn = xv * r
        dxn = dh_v * (nw_ref[...] * (1.0 + sc_ref[...]))
        dx_ref[...] = dres_ref[...] + r * (dxn - xn * jnp.mean(dxn * xn, axis=-1, keepdims=True))

        @pl.when(pl.program_id(0) == 0)
        def _():
            st_ref[...] = jnp.zeros_like(st_ref)

        dhx = dh_v * xn
        st_ref[0:1, :] += _colsum(dhx * nw_ref[...])
        st_ref[1:2, :] += _colsum(dh_v)
        st_ref[2:3, :] += _colsum(dhx * (1.0 + sc_ref[...]))

    return pl.pallas_call(body, name=name, grid=(l // tl,), in_specs=in_specs,
                          out_specs=[row, pl.BlockSpec((8, d), lambda i: (0, 0))],
                          out_shape=[jax.ShapeDtypeStruct((l, d), F32), jax.ShapeDtypeStruct((8, d), F32)],
                          )(x, dh, dres, nw, sc)


def _gate_bwd(dres, val, g, name):
    l, d = dres.shape
    tl, row, _, in_specs = _row_specs(l, d, 2, 1)

    def body(dres_ref, val_ref, g_ref, dval_ref, st_ref):
        dr = dres_ref[...]
        dval_ref[...] = (g_ref[...] * dr).astype(dval_ref.dtype)

        @pl.when(pl.program_id(0) == 0)
        def _():
            st_ref[...] = jnp.zeros_like(st_ref)

        st_ref[0:1, :] += _colsum(dr * val_ref[...])

    return pl.pallas_call(body, name=name, grid=(l // tl,), in_specs=in_specs,
                          out_specs=[row, pl.BlockSpec((8, d), lambda i: (0, 0))],
                          out_shape=[jax.ShapeDtypeStruct((l, d), MXU_DTYPE), jax.ShapeDtypeStruct((8, d), F32)],
                          )(dres, val, g)


def _loss_head(x1, f, g, target, name):
    l, d = x1.shape
    tl, row, _, _ = _row_specs(l, d, 0, 0)
    vec = pl.BlockSpec((1, d), lambda i: (0, 0))

    def body(x1_ref, f_ref, g_ref, t_ref, dy_ref, loss_ref):
        e = x1_ref[...] + g_ref[...] * f_ref[...] - t_ref[...]
        dy_ref[...] = e * (1.0 / d)

        @pl.when(pl.program_id(0) == 0)
        def _():
            loss_ref[...] = jnp.zeros_like(loss_ref)

        s = jnp.sum(jnp.sum(e * e, axis=1, keepdims=True), axis=0, keepdims=True)
        loss_ref[...] += (0.5 / d) * s

    return pl.pallas_call(body, name=name, grid=(l // tl,), in_specs=[row, row, vec, row],
                          out_specs=[row, pl.BlockSpec((8, 128), lambda i: (0, 0))],
                          out_shape=[jax.ShapeDtypeStruct((l, d), F32), jax.ShapeDtypeStruct((8, 128), F32)],
                          )(x1, f, g, target)


def _swiglu(gate, up, name):
    l, f = gate.shape
    tl, tf = _pick(l, 512, 16), _pick(f, 1408, 128)
    spec = pl.BlockSpec((tl, tf), lambda i, j: (i, j))

    def body(g_ref, u_ref, a_ref):
        gv = g_ref[...]
        a_ref[...] = (gv * _sigmoid(gv) * u_ref[...]).astype(a_ref.dtype)

    return pl.pallas_call(body, name=name, grid=(l // tl, f // tf), in_specs=[spec, spec], out_specs=spec,
                          out_shape=jax.ShapeDtypeStruct((l, f), MXU_DTYPE))(gate, up)


def _swiglu_bwd(gate, up, da, name):
    l, f = gate.shape
    tl, tf = _pick(l, 512, 16), _pick(f, 1408, 128)
    spec = pl.BlockSpec((tl, tf), lambda i, j: (i, j))

    def body(g_ref, u_ref, da_ref, dg_ref, du_ref):
        gv, dav = g_ref[...], da_ref[...]
        s = _sigmoid(gv)
        dg_ref[...] = (dav * u_ref[...] * (s * (1.0 + gv * (1.0 - s)))).astype(dg_ref.dtype)
        du_ref[...] = (dav * (gv * s)).astype(du_ref.dtype)

    return pl.pallas_call(body, name=name, grid=(l // tl, f // tf), in_specs=[spec, spec, spec], out_specs=[spec, spec],
                          out_shape=[jax.ShapeDtypeStruct((l, f), MXU_DTYPE)] * 2)(gate, up, da)


def _conv_tile(l):
    return _pick(l, 512, 16)


def _conv_pre(buf, w_ref, b_ref, tl):
    acc = b_ref[...] + w_ref[3:4, :] * buf[HALO:HALO + tl, :]
    for k in range(CONV_WIDTH - 1):
        s = HALO - (CONV_WIDTH - 1) + k
        acc = acc + w_ref[k:k + 1, :] * buf[s:s + tl, :]
    return acc


def _fill_past(buf, u_ref, halo_ref, tl):
    i = pl.program_id(0)

    @pl.when(i == 0)
    def _():
        buf[0:HALO, :] = jnp.zeros((HALO, buf.shape[1]), F32)

    @pl.when(i > 0)
    def _():
        buf[0:HALO, :] = halo_ref[...]

    buf[HALO:HALO + tl, :] = u_ref[...]


def _conv_silu(u, w, b, name):
    l, ch = u.shape[0], w.shape[1]
    tl = _conv_tile(l)
    per = tl // HALO
    cur = pl.BlockSpec((tl, ch), lambda i: (i, 0))
    past = pl.BlockSpec((HALO, ch), lambda i: (jnp.maximum(i * per - 1, 0), 0))

    def body(u_ref, halo_ref, w_ref, b_ref, o_ref, buf):
        _fill_past(buf, u_ref, halo_ref, tl)
        pre = _conv_pre(buf, w_ref, b_ref, tl)
        o_ref[...] = pre * _sigmoid(pre)

    return pl.pallas_call(body, name=name, grid=(l // tl,),
                          in_specs=[cur, past, pl.BlockSpec((CONV_WIDTH, ch), lambda i: (0, 0)), pl.BlockSpec((1, ch), lambda i: (0, 0))],
                          out_specs=cur, out_shape=jax.ShapeDtypeStruct((l, ch), F32),
                          scratch_shapes=[pltpu.VMEM((tl + HALO, ch), F32)])(u, u, w, b)


def _conv_silu_bwd_pre(u, dxc, w, b, name):
    l, ch = u.shape[0], w.shape[1]
    tl = _conv_tile(l)
    per = tl // HALO
    cur = pl.BlockSpec((tl, ch), lambda i: (i, 0))
    past = pl.BlockSpec((HALO, ch), lambda i: (jnp.maximum(i * per - 1, 0), 0))

    def body(u_ref, halo_ref, d_ref, w_ref, b_ref, dpre_ref, st_ref, buf):
        _fill_past(buf, u_ref, halo_ref, tl)
        pre = _conv_pre(buf, w_ref, b_ref, tl)
        s = _sigmoid(pre)
        dpre = d_ref[...] * (s * (1.0 + pre * (1.0 - s)))
        dpre_ref[...] = dpre

        @pl.when(pl.program_id(0) == 0)
        def _():
            st_ref[...] = jnp.zeros_like(st_ref)

        for k in range(CONV_WIDTH):
            s0 = HALO - (CONV_WIDTH - 1) + k
            st_ref[k:k + 1, :] += _colsum(dpre * buf[s0:s0 + tl, :])
        st_ref[CONV_WIDTH:CONV_WIDTH + 1, :] += _colsum(dpre)

    return pl.pallas_call(body, name=name, grid=(l // tl,),
                          in_specs=[cur, past, cur, pl.BlockSpec((CONV_WIDTH, ch), lambda i: (0, 0)), pl.BlockSpec((1, ch), lambda i: (0, 0))],
                          out_specs=[cur, pl.BlockSpec((8, ch), lambda i: (0, 0))],
                          out_shape=[jax.ShapeDtypeStruct((l, ch), F32), jax.ShapeDtypeStruct((8, ch), F32)],
                          scratch_shapes=[pltpu.VMEM((tl + HALO, ch), F32)])(u, u, dxc, w, b)


def _conv_bwd_input(dpre, w, name):
    l, ch = dpre.shape
    tl = _conv_tile(l)
    per = tl // HALO
    nt = l // tl
    cur = pl.BlockSpec((tl, ch), lambda i: (i, 0))
    nxt = pl.BlockSpec((HALO, ch), lambda i: (jnp.minimum((i + 1) * per, l // HALO - 1), 0))

    def body(d_ref, halo_ref, w_ref, du_ref, buf):
        i = pl.program_id(0)
        buf[0:tl, :] = d_ref[...]

        @pl.when(i == nt - 1)
        def _():
            buf[tl:tl + HALO, :] = jnp.zeros((HALO, ch), F32)

        @pl.when(i < nt - 1)
        def _():
            buf[tl:tl + HALO, :] = halo_ref[...]

        acc = w_ref[3:4, :] * buf[0:tl, :]
        for k in range(CONV_WIDTH - 1):
            s = CONV_WIDTH - 1 - k
            acc = acc + w_ref[k:k + 1, :] * buf[s:s + tl, :]
        du_ref[...] = acc

    return pl.pallas_call(body, name=name, grid=(nt,),
                          in_specs=[cur, nxt, pl.BlockSpec((CONV_WIDTH, ch), lambda i: (0, 0))],
                          out_specs=cur, out_shape=jax.ShapeDtypeStruct((l, ch), F32),
                          scratch_shapes=[pltpu.VMEM((tl + HALO, ch), F32)])(dpre, dpre, w)


HEAD_TILE = 128
D_HEAD_TILES = N_HEADS * HEAD_TILE


def _chunk_iota():
    r = lax.broadcasted_iota(jnp.int32, (CHUNK, CHUNK), 0)
    c = lax.broadcasted_iota(jnp.int32, (CHUNK, CHUNK), 1)
    return r, c


def _split3_dot(v, ones_b):
    hi = v.astype(BF16)
    r1 = v - hi.astype(F32)
    mid = r1.astype(BF16)
    lo = (r1 - mid.astype(F32)).astype(BF16)
    return _dot(hi, ones_b) + _dot(mid, ones_b) + _dot(lo, ones_b)


def _ssd_decays(dtx_ref, dty_ref, dtbx_ref, dtby_ref, alx_ref, aly_ref, r, c):
    tri = (r >= c).astype(F32)
    dt_x = _softplus(dtx_ref[...] + dtbx_ref[...])
    a_x = -jnp.exp(alx_ref[...])
    adt_x = a_x * dt_x
    acs_x = _dot(tri, adt_x, HIGHEST)
    atot_x = _colsum(adt_x)
    acs_y = _dot(tri, -jnp.exp(aly_ref[...]) * _softplus(dty_ref[...] + dtby_ref[...]), HIGHEST)
    return dt_x, a_x, acs_y, jnp.exp(acs_x), jnp.exp(atot_x - acs_x), jnp.exp(atot_x)


def _head_decay(acs_y, e, r, c):
    col = acs_y[:, e * HEAD_TILE:(e + 1) * HEAD_TILE]
    return jnp.where(r >= c, jnp.exp(col - col.T), 0.0)


def _half_masks():
    lane = lax.broadcasted_iota(jnp.int32, (CHUNK, 2 * HEAD_DIM), 1)
    return lane < HEAD_DIM, lane >= HEAD_DIM


def _ssd_specs(nc, reverse):
    def at(i):
        return nc - 1 - i if reverse else i
    xc = pl.BlockSpec((CHUNK, D_CONV), lambda i: (at(i), 0))
    wide = pl.BlockSpec((CHUNK, D_SSD), lambda i: (at(i), 0))
    wide_y = pl.BlockSpec((CHUNK, D_HEAD_TILES), lambda i: (at(i), 0))
    vec = pl.BlockSpec((1, D_SSD), lambda i: (0, 0))
    vec_y = pl.BlockSpec((1, D_HEAD_TILES), lambda i: (0, 0))
    state = pl.BlockSpec((1, SSD_STATE, D_SSD), lambda i: (at(i), 0, 0))
    return xc, wide, wide_y, vec, vec_y, state


def _ssd_fwd(xc, dtr_x, dtr_y, dtb_x, dtb_y, al_x, al_y, name):
    l = xc.shape[0]
    nc = l // CHUNK
    xc_s, wide_s, wide_y_s, vec_s, vec_y_s, state_s = _ssd_specs(nc, False)
    pairs_per_group = GROUP_WIDTH // (2 * HEAD_DIM)

    def body(xc_ref, dtx_ref, dty_ref, dtbx_ref, dtby_ref, alx_ref, aly_ref, y_ref, sp_ref, state):
        @pl.when(pl.program_id(0) == 0)
        def _():
            state[...] = jnp.zeros_like(state)

        r, c = _chunk_iota()
        halves = _half_masks()
        dt_x, _, acs_y, ea_x, ds_x, eatot_x = _ssd_decays(dtx_ref, dty_ref, dtbx_ref, dtby_ref, alx_ref, aly_ref, r, c)
        xg = xc_ref[:, 0:D_SSD] * dt_x
        sp_ref[0] = state[...]
        for g in range(SSD_GROUPS):
            lanes = slice(g * GROUP_WIDTH, (g + 1) * GROUP_WIDTH)
            bb = xc_ref[:, D_SSD + g * SSD_STATE:D_SSD + (g + 1) * SSD_STATE].astype(MXU_DTYPE)
            cb = xc_ref[:, D_SSD + (SSD_GROUPS + g) * SSD_STATE:D_SSD + (SSD_GROUPS + g + 1) * SSD_STATE].astype(MXU_DTYPE)
            scores = _dot_nt(cb, bb)
            sg = state[:, lanes]
            ys = []
            for j in range(g * pairs_per_group, (g + 1) * pairs_per_group):
                xg_pair = xg[:, j * 2 * HEAD_DIM:(j + 1) * 2 * HEAD_DIM]
                acc = jnp.zeros((CHUNK, 2 * HEAD_DIM), F32)
                for half in range(2):
                    m = (scores * _head_decay(acs_y, 2 * j + half, r, c)).astype(MXU_DTYPE)
                    acc = acc + _dot(m, jnp.where(halves[half], xg_pair, 0.0).astype(MXU_DTYPE))
                ys.append(acc)
            y_ref[:, lanes] = jnp.concatenate(ys, axis=1) + _dot(cb, sg.astype(MXU_DTYPE)) * ea_x[:, lanes]
            state[:, lanes] = sg * eatot_x[:, lanes] + _dot_tn(bb, (xg[:, lanes] * ds_x[:, lanes]).astype(MXU_DTYPE))

    return pl.pallas_call(
        body, name=name, grid=(nc,),
        in_specs=[xc_s, wide_s, wide_y_s, vec_s, vec_y_s, vec_s, vec_y_s],
        out_specs=[wide_s, state_s],
        out_shape=[jax.ShapeDtypeStruct((l, D_SSD), F32), jax.ShapeDtypeStruct((nc, SSD_STATE, D_SSD), F32)],
        scratch_shapes=[pltpu.VMEM((SSD_STATE, D_SSD), F32)],
    )(xc, dtr_x, dtr_y, dtb_x, dtb_y, al_x, al_y)


def _ssd_bwd(xc, dtr_x, dtr_y, dtb_x, dtb_y, al_x, al_y, dy, states, head_ones, name):
    l = xc.shape[0]
    nc = l // CHUNK
    xc_s, wide_s, wide_y_s, vec_s, vec_y_s, state_s = _ssd_specs(nc, True)
    pairs_per_group = GROUP_WIDTH // (2 * HEAD_DIM)

    def body(xc_ref, dtx_ref, dty_ref, dtbx_ref, dtby_ref, alx_ref, aly_ref, dy_ref, sp_ref, ones_ref,
             dxc_ref, ddtr_ref, st_ref, hs_ref, dstate):
        @pl.when(pl.program_id(0) == 0)
        def _():
            dstate[...] = jnp.zeros_like(dstate)
            st_ref[...] = jnp.zeros_like(st_ref)

        r, c = _chunk_iota()
        halves = _half_masks()
        ones_b = ones_ref[...]
        dt_x, a_x, acs_y, ea_x, ds_x, eatot_x = _ssd_decays(dtx_ref, dty_ref, dtbx_ref, dtby_ref, alx_ref, aly_ref, r, c)
        xs = xc_ref[:, 0:D_SSD]
        xg = xs * dt_x
        gy = dy_ref[...]
        s_prev = sp_ref[0]
        gea = gy * ea_x
        xds = xg * ds_x
        ds_old = dstate[...]
        later2 = jnp.concatenate([(r <= c).astype(BF16)] * 2, axis=1)
        dxg_parts, state_term, yoff_parts, dadt_parts = [], [], [], []
        for g in range(SSD_GROUPS):
            lanes = slice(g * GROUP_WIDTH, (g + 1) * GROUP_WIDTH)
            b_lo = D_SSD + g * SSD_STATE
            c_lo = D_SSD + (SSD_GROUPS + g) * SSD_STATE
            bb = xc_ref[:, b_lo:b_lo + SSD_STATE].astype(MXU_DTYPE)
            cb = xc_ref[:, c_lo:c_lo + SSD_STATE].astype(MXU_DTYPE)
            scores = _dot_nt(cb, bb)
            dsg = ds_old[:, lanes].astype(MXU_DTYPE)
            gea_b = gea[:, lanes].astype(MXU_DTYPE)
            xds_b = xds[:, lanes].astype(MXU_DTYPE)
            dxg_state = _dot(bb, dsg) * ds_x[:, lanes]
            dc = _dot_nt(gea_b, s_prev[:, lanes].astype(MXU_DTYPE))
            db = _dot_nt(xds_b, dsg)
            dscores = jnp.zeros((CHUNK, CHUNK), F32)
            diag = []
            for j in range(g * pairs_per_group, (g + 1) * pairs_per_group):
                pair = slice(j * 2 * HEAD_DIM, (j + 1) * 2 * HEAD_DIM)
                xg_pair = xg[:, pair].astype(MXU_DTYPE)
                acc = jnp.zeros((CHUNK, 2 * HEAD_DIM), F32)
                cols = []
                for half in range(2):
                    decay = _head_decay(acs_y, 2 * j + half, r, c)
                    g_e = jnp.where(halves[half], gy[:, pair], 0.0).astype(MXU_DTYPE)
                    acc = acc + _dot_tn((scores * decay).astype(MXU_DTYPE), g_e)
                    dm = _dot_nt(g_e, xg_pair) * decay
                    dscores = dscores + dm
                    wq = dm * scores
                    hi = wq.astype(BF16)
                    lo = (wq - hi.astype(F32)).astype(BF16)
                    later = _dot(later2, jnp.concatenate([hi, lo], axis=0))
                    cols.append(jnp.sum(jnp.where(c < r, later, 0.0), axis=1, keepdims=True))
                diag.append(acc)
                dadt_parts.append(jnp.where(halves[0], cols[0], cols[1]))
            dsc_b = dscores.astype(MXU_DTYPE)
            dc = dc + _dot(dsc_b, bb)
            db = db + _dot_tn(dsc_b, cb)
            dxc_ref[:, b_lo:b_lo + SSD_STATE] = db
            dxc_ref[:, c_lo:c_lo + SSD_STATE] = dc
            dxg_parts.append(jnp.concatenate(diag, axis=1) + dxg_state)
            state_term.append(dxg_state)
            yoff_parts.append(_dot(cb, s_prev[:, lanes].astype(MXU_DTYPE)) * ea_x[:, lanes])
            dstate[:, lanes] = ds_old[:, lanes] * eatot_x[:, lanes] + _dot_tn(cb, gea_b)
        dxg = jnp.concatenate(dxg_parts, axis=1)
        dxc_ref[:, 0:D_SSD] = dxg * dt_x
        through_out = _dot((r <= c).astype(F32), gy * jnp.concatenate(yoff_parts, axis=1), HIGHEST)
        through_in = _dot((c < r).astype(F32), xg * jnp.concatenate(state_term, axis=1), HIGHEST)
        carried = jnp.broadcast_to(_colsum(ds_old * s_prev) * eatot_x, (8, D_SSD))
        dadt = (jnp.concatenate(dadt_parts, axis=1) + _split3_dot(through_out + through_in, ones_b)
                + jnp.max(_split3_dot(carried, ones_b), axis=0, keepdims=True))
        ddt = a_x * dadt + _split3_dot(dxg * xs, ones_b)
        draw = ddt * _sigmoid(dtx_ref[...] + dtbx_ref[...])
        ddtr_ref[...] = draw
        st_ref[0:1, :] += _colsum(dt_x * dadt) * a_x
        st_ref[1:2, :] += _colsum(draw)
        st_ref[2:3, :] += _colsum(gy * xs)

        @pl.when(pl.program_id(0) == nc - 1)
        def _():
            hs_ref[...] = _split3_dot(st_ref[...], ones_b)

    stats = pl.BlockSpec((8, D_SSD), lambda i: (0, 0))
    return pl.pallas_call(
        body, name=name, grid=(nc,),
        in_specs=[xc_s, wide_s, wide_y_s, vec_s, vec_y_s, vec_s, vec_y_s, wide_s, state_s,
                  pl.BlockSpec((D_SSD, D_SSD), lambda i: (0, 0))],
        out_specs=[xc_s, wide_s, stats, stats],
        out_shape=[jax.ShapeDtypeStruct((l, D_CONV), F32), jax.ShapeDtypeStruct((l, D_SSD), F32),
                   jax.ShapeDtypeStruct((8, D_SSD), F32), jax.ShapeDtypeStruct((8, D_SSD), F32)],
        scratch_shapes=[pltpu.VMEM((SSD_STATE, D_SSD), F32)],
    )(xc, dtr_x, dtr_y, dtb_x, dtb_y, al_x, al_y, dy, states, head_ones)


def _ssd_gate_norm(ycore, xc, z, dskip_x, norm_w, name):
    l = ycore.shape[0]
    tl = _pick(l, 512, 16)
    row = pl.BlockSpec((tl, D_SSD), lambda i: (i, 0))
    vec = pl.BlockSpec((1, D_SSD), lambda i: (0, 0))

    def body(y_ref, xs_ref, z_ref, dk_ref, nw_ref, o_ref):
        zv = z_ref[...]
        yv = (y_ref[...] + dk_ref[...] * xs_ref[...]) * (zv * _sigmoid(zv))
        for g in range(SSD_GROUPS):
            lanes = slice(g * GROUP_WIDTH, (g + 1) * GROUP_WIDTH)
            yg = yv[:, lanes]
            rg = lax.rsqrt(jnp.mean(yg * yg, axis=-1, keepdims=True) + EPS)
            o_ref[:, lanes] = (yg * rg * nw_ref[:, lanes]).astype(o_ref.dtype)

    z_spec = pl.BlockSpec((tl, D_SSD), lambda i: (i, O_Z // D_SSD))
    return pl.pallas_call(body, name=name, grid=(l // tl,), in_specs=[row, row, z_spec, vec, vec], out_specs=row,
                          out_shape=jax.ShapeDtypeStruct((l, D_SSD), MXU_DTYPE))(ycore, xc, z, dskip_x, norm_w)


def _ssd_gate_norm_bwd(dout, ycore, xc, z, dskip_x, norm_w, name):
    l = ycore.shape[0]
    tl = _pick(l, 512, 16)
    nt = l // tl
    row = pl.BlockSpec((tl, D_SSD), lambda i: (i, 0))
    vec = pl.BlockSpec((1, D_SSD), lambda i: (0, 0))

    def body(do_ref, y_ref, xs_ref, z_ref, dk_ref, nw_ref, dyc_ref, dxs_ref, dz_ref, st_ref):
        @pl.when(pl.program_id(0) == 0)
        def _():
            st_ref[...] = jnp.zeros_like(st_ref)

        zv, xs = z_ref[...], xs_ref[...]
        s = _sigmoid(zv)
        gz = zv * s
        yc = y_ref[...] + dk_ref[...] * xs
        yv = yc * gz
        dov = do_ref[...]
        dnw, dyv = [], []
        for g in range(SSD_GROUPS):
            lanes = slice(g * GROUP_WIDTH, (g + 1) * GROUP_WIDTH)
            yg = yv[:, lanes]
            rg = lax.rsqrt(jnp.mean(yg * yg, axis=-1, keepdims=True) + EPS)
            yn = yg * rg
            dnw.append(_colsum(dov[:, lanes] * yn))
            dyn = dov[:, lanes] * nw_ref[:, lanes]
            dyv.append(rg * (dyn - yn * jnp.mean(dyn * yn, axis=-1, keepdims=True)))
        dy = jnp.concatenate(dyv, axis=1)
        dyc = dy * gz
        dyc_ref[...] = dyc
        dxs_ref[...] = dyc * dk_ref[...]
        dz_ref[...] = dy * yc * (s * (1.0 + zv * (1.0 - s)))
        st_ref[0:1, :] += jnp.concatenate(dnw, axis=1)

    return pl.pallas_call(
        body, name=name, grid=(nt,), in_specs=[row, row, row, pl.BlockSpec((tl, D_SSD), lambda i: (i, O_Z // D_SSD)), vec, vec],
        out_specs=[row, row, row, pl.BlockSpec((8, D_SSD), lambda i: (0, 0))],
        out_shape=[jax.ShapeDtypeStruct((l, D_SSD), F32)] * 3 + [jax.ShapeDtypeStruct((8, D_SSD), F32)],
    )(dout, ycore, xc, z, dskip_x, norm_w)


PAIR = 2 * HEAD_DIM
N_PAIRS = N_HEADS // 2
Q_TILE0 = O_Q // PAIR
SB_KEYS = 1024
SB_PAIRS = 2
SB_SCALE = HEAD_DIM ** -0.5


def _pair_sum(v, lo):
    s_lo = jnp.sum(jnp.where(lo, v, 0.0), axis=-1, keepdims=True)
    s_hi = jnp.sum(jnp.where(lo, 0.0, v), axis=-1, keepdims=True)
    return jnp.where(lo, s_lo, s_hi)


def _qkv_prep(proj, qw2, kw2, name):
    l = proj.shape[0]
    tl = _pick(l, 1024, 16)
    out = pl.BlockSpec((tl, PAIR), lambda i, j: (i, j))
    vec = pl.BlockSpec((1, PAIR), lambda i, j: (0, 0))

    def at(which):
        return pl.BlockSpec((tl, PAIR), lambda i, j: (i, Q_TILE0 + which * N_PAIRS + j))

    def body(q_ref, k_ref, v_ref, qw_ref, kw_ref, qn_ref, kn_ref, vb_ref):
        lo = lax.broadcasted_iota(jnp.int32, (tl, PAIR), 1) < HEAD_DIM
        for t_ref, w_ref, o_ref, scale in ((q_ref, qw_ref, qn_ref, SB_SCALE), (k_ref, kw_ref, kn_ref, 1.0)):
            tv = t_ref[...]
            r = lax.rsqrt(_pair_sum(tv * tv, lo) * (1.0 / HEAD_DIM) + EPS)
            o_ref[...] = ((tv * r) * w_ref[...] * scale).astype(o_ref.dtype)
        vb_ref[...] = v_ref[...].astype(vb_ref.dtype)

    return pl.pallas_call(body, name=name, grid=(l // tl, N_PAIRS), in_specs=[at(0), at(1), at(2), vec, vec],
                          out_specs=[out, out, out], out_shape=[jax.ShapeDtypeStruct((l, D_SB), MXU_DTYPE)] * 3,
                          )(proj, proj, proj, qw2, kw2)


def _qk_norm_bwd(proj, dqn, dkn, qw2, kw2, name):
    l = proj.shape[0]
    tl = _pick(l, 1024, 16)
    out = pl.BlockSpec((tl, PAIR), lambda i, j: (i, j))
    vec = pl.BlockSpec((1, PAIR), lambda i, j: (0, 0))
    stats = pl.BlockSpec((8, PAIR), lambda i, j: (0, 0))

    def at(which):
        return pl.BlockSpec((tl, PAIR), lambda i, j: (i, Q_TILE0 + which * N_PAIRS + j))

    def body(q_ref, k_ref, dqn_ref, dkn_ref, qw_ref, kw_ref, dq_ref, dk_ref, stq_ref, stk_ref):
        @pl.when((pl.program_id(0) == 0) & (pl.program_id(1) == 0))
        def _():
            stq_ref[...] = jnp.zeros_like(stq_ref)
            stk_ref[...] = jnp.zeros_like(stk_ref)

        lo = lax.broadcasted_iota(jnp.int32, (tl, PAIR), 1) < HEAD_DIM
        for t_ref, dn_ref, w_ref, d_ref, st_ref, scale in ((q_ref, dqn_ref, qw_ref, dq_ref, stq_ref, SB_SCALE),
                                                           (k_ref, dkn_ref, kw_ref, dk_ref, stk_ref, 1.0)):
            tv = t_ref[...]
            r = lax.rsqrt(_pair_sum(tv * tv, lo) * (1.0 / HEAD_DIM) + EPS)
            tn = tv * r
            dnv = dn_ref[...] * scale
            dtn = dnv * w_ref[...]
            d_ref[...] = r * (dtn - tn * (_pair_sum(dtn * tn, lo) * (1.0 / HEAD_DIM)))
            st_ref[0:1, :] += _colsum(dnv * tn)

    return pl.pallas_call(body, name=name, grid=(l // tl, N_PAIRS), in_specs=[at(0), at(1), out, out, vec, vec],
                          out_specs=[out, out, stats, stats],
                          out_shape=[jax.ShapeDtypeStruct((l, D_SB), F32)] * 2 + [jax.ShapeDtypeStruct((8, PAIR), F32)] * 2,
                          )(proj, proj, dqn, dkn, qw2, kw2)


def _sb_masks():
    r = lax.broadcasted_iota(jnp.int32, (CHUNK, CHUNK), 0)
    c = lax.broadcasted_iota(jnp.int32, (CHUNK, CHUNK), 1)
    return r, c


def _stack2(mask):
    t = mask.astype(BF16)
    return jnp.concatenate([t, t], axis=0)


def _sb_fwd(q, k, v, name):
    l = q.shape[0]
    nq = l // CHUNK
    kt = _pick(l, SB_KEYS, CHUNK)
    sub = kt // CHUNK
    heads = 2 * SB_PAIRS
    qblk = pl.BlockSpec((CHUNK, SB_PAIRS * PAIR), lambda i, j: (j, i))
    full = pl.BlockSpec((l, SB_PAIRS * PAIR), lambda i, j: (0, i))

    def body(q_ref, k_ref, v_ref, o_ref, tot_ref):
        qb = pl.program_id(1)
        r, c = _sb_masks()
        after2 = _stack2(r > c)
        halves = _half_masks()
        zero = jnp.zeros((CHUNK, PAIR), q_ref.dtype)
        lanes = [slice((a // 2) * PAIR, (a // 2 + 1) * PAIR) for a in range(heads)]
        qm = [jnp.where(halves[a % 2], q_ref[:, lanes[a]], zero) for a in range(heads)]
        last = (qb * CHUNK) // kt

        def tile(t, width, carries, accs, masked):
            rows = pl.ds(pl.multiple_of(t * width, width), width)
            if masked:
                key_minus_query = (lax.broadcasted_iota(jnp.int32, (CHUNK, width), 1)
                                   - lax.broadcasted_iota(jnp.int32, (CHUNK, width), 0))
                keep = key_minus_query < qb * CHUNK - t * width

            def logits(a):
                lg = _dot_nt(qm[a], k_ref[rows, lanes[a]])
                sp = _softplus(lg)
                return lg - sp, (jnp.where(keep, -sp, 0.0) if masked else -sp)

            def sums(a, lr):
                offset, parts = carries[a], [None] * (width // CHUNK)
                for j in reversed(range(width // CHUNK)):
                    piece = lr[:, j * CHUNK:(j + 1) * CHUNK]
                    parts[j] = _split_dot(piece, after2) + offset
                    offset = offset + jnp.sum(piece, axis=1, keepdims=True)
                return jnp.concatenate(parts, axis=1), offset

            def output(a, ls, cs):
                w = jnp.exp(ls + cs)
                if masked:
                    w = jnp.where(keep, w, 0.0)
                return accs[a] + _dot(w.astype(MXU_DTYPE), v_ref[rows, lanes[a]])

            new_carries, new_accs = [None] * heads, [None] * heads
            ls, lr = logits(0)
            for a in range(heads):
                cs, new_carries[a] = sums(a, lr)
                if a + 1 < heads:
                    ls_next, lr = logits(a + 1)
                new_accs[a] = output(a, ls, cs)
                ls = ls_next
            return tuple(new_carries), tuple(new_accs)

        carries = tuple(jnp.zeros((CHUNK, 1), F32) for _ in range(heads))
        accs = tuple(jnp.zeros((CHUNK, PAIR), F32) for _ in range(heads))
        if sub % 2 == 0:
            half = kt // 2
            carries, accs = lax.cond((qb * CHUNK) % kt >= half,
                                     lambda cr, ac: tile(2 * last + 1, half, cr, ac, True), lambda cr, ac: (cr, ac), carries, accs)
            carries, accs = tile(2 * last, half, carries, accs, True)
        else:
            carries, accs = tile(last, kt, carries, accs, True)
        carries, accs = lax.fori_loop(1, last + 1, lambda i, st: tile(last - i, kt, st[0], st[1], False), (carries, accs))
        for p in range(SB_PAIRS):
            o_ref[:, lanes[2 * p]] = jnp.where(halves[0], accs[2 * p], accs[2 * p + 1])
            tot_ref[:, lanes[2 * p]] = jnp.where(halves[0], carries[2 * p], carries[2 * p + 1])

    return pl.pallas_call(
        body, name=name, grid=(N_PAIRS // SB_PAIRS, nq), in_specs=[qblk, full, full],
        out_specs=[qblk, qblk],
        out_shape=[jax.ShapeDtypeStruct((l, D_SB), F32), jax.ShapeDtypeStruct((l, D_SB), F32)],
    )(q, k, v)


def _sb_bwd(q, k, v, dycat, tot, name):
    l = q.shape[0]
    nq = l // CHUNK
    kt = _pick(l, SB_KEYS, CHUNK)
    sub = kt // CHUNK
    heads = 2 * SB_PAIRS
    width_all = SB_PAIRS * PAIR
    qblk = pl.BlockSpec((CHUNK, width_all), lambda i, j: (j, i))
    doblk = pl.BlockSpec((CHUNK, width_all), lambda i, j: (j, D_SSD // width_all + i))
    full = pl.BlockSpec((l, width_all), lambda i, j: (0, i))

    def body(q_ref, k_ref, v_ref, do_ref, tot_ref, dq_ref, dk_ref, dv_ref):
        qb = pl.program_id(1)

        @pl.when(qb == 0)
        def _():
            dk_ref[...] = jnp.zeros_like(dk_ref)
            dv_ref[...] = jnp.zeros_like(dv_ref)

        r, c = _sb_masks()
        upto2 = _stack2(r <= c)
        before2 = _stack2(r < c)
        halves = _half_masks()
        lanes = [slice((a // 2) * PAIR, (a // 2 + 1) * PAIR) for a in range(heads)]
        qm = [jnp.where(halves[a % 2], q_ref[:, lanes[a]], jnp.zeros((CHUNK, PAIR), q_ref.dtype)) for a in range(heads)]
        dom = [jnp.where(halves[a % 2], do_ref[:, lanes[a]], 0.0).astype(MXU_DTYPE) for a in range(heads)]
        total = [jnp.max(jnp.where(halves[a % 2], tot_ref[:, lanes[a]], -jnp.inf), axis=1, keepdims=True) for a in range(heads)]
        last = (qb * CHUNK) // kt

        def prefix(values, tri2, offset):
            parts = []
            for j in range(values.shape[1] // CHUNK):
                piece = values[:, j * CHUNK:(j + 1) * CHUNK]
                parts.append(_split_dot(piece, tri2) + offset)
                offset = offset + jnp.sum(piece, axis=1, keepdims=True)
            return jnp.concatenate(parts, axis=1), offset

        def tile(t, width, carry_p, carry_d, dq, masked):
            rows = pl.ds(pl.multiple_of(t * width, width), width)
            if masked:
                key_minus_query = (lax.broadcasted_iota(jnp.int32, (CHUNK, width), 1)
                                   - lax.broadcasted_iota(jnp.int32, (CHUNK, width), 0))
                keep = key_minus_query < qb * CHUNK - t * width

            def weights(a):
                lg = _dot_nt(qm[a], k_ref[rows, lanes[a]])
                sp = _softplus(lg)
                lr = jnp.where(keep, -sp, 0.0) if masked else -sp
                p_incl, p_next = prefix(lr, upto2, carry_p[a])
                ls = lg - sp
                w = jnp.exp(ls + (total[a] - p_incl))
                if masked:
                    w = jnp.where(keep, w, 0.0)
                return ls, w, p_next

            def gradients(a, ls, w):
                da = _dot_nt(dom[a], v_ref[rows, lanes[a]]) * w
                d_excl, d_next = prefix(da, before2, carry_d[a])
                sig = jnp.exp(ls)
                dl = da * (1.0 - sig) - d_excl * sig
                if masked:
                    dl = jnp.where(keep, dl, 0.0)
                dl_b = dl.astype(MXU_DTYPE)
                return (d_next, dq[a] + _dot(dl_b, k_ref[rows, lanes[a]]),
                        _dot_tn(w.astype(MXU_DTYPE), dom[a]), _dot_tn(dl_b, qm[a]))

            new_p, new_d, new_dq = [None] * heads, [None] * heads, [None] * heads
            dv_upd, dk_upd = [None] * heads, [None] * heads
            ls, w, new_p[0] = weights(0)
            for a in range(heads):
                if a + 1 < heads:
                    ls_next, w_next, new_p[a + 1] = weights(a + 1)
                new_d[a], new_dq[a], dv_upd[a], dk_upd[a] = gradients(a, ls, w)
                if a % 2 == 1:
                    dv_ref[rows, lanes[a]] += dv_upd[a - 1] + dv_upd[a]
                    dk_ref[rows, lanes[a]] += dk_upd[a - 1] + dk_upd[a]
                if a + 1 < heads:
                    ls, w = ls_next, w_next
            return tuple(new_p), tuple(new_d), tuple(new_dq)

        zeros = tuple(jnp.zeros((CHUNK, 1), F32) for _ in range(heads))
        dq0 = tuple(jnp.zeros((CHUNK, PAIR), F32) for _ in range(heads))
        state = lax.fori_loop(0, last, lambda t, st: tile(t, kt, st[0], st[1], st[2], False), (zeros, zeros, dq0))
        if sub % 2 == 0:
            half = kt // 2
            state = tile(2 * last, half, *state, True)
            state = lax.cond((qb * CHUNK) % kt >= half,
                             lambda cp, cd, dq: tile(2 * last + 1, half, cp, cd, dq, True), lambda cp, cd, dq: (cp, cd, dq), *state)
        else:
            state = tile(last, kt, *state, True)
        for p in range(SB_PAIRS):
            dq_ref[:, lanes[2 * p]] = jnp.where(halves[0], state[2][2 * p], state[2][2 * p + 1])

    return pl.pallas_call(
        body, name=name, grid=(N_PAIRS // SB_PAIRS, nq),
        in_specs=[qblk, full, full, doblk, qblk],
        out_specs=[qblk, full, full],
        out_shape=[jax.ShapeDtypeStruct((l, D_SB), F32)] * 3,
    )(q, k, v, dycat, tot)


def _ada_fwd(c_all, w_shard, name):
    def body(c_ref, w_ref, o_ref):
        cv = c_ref[...]
        o_ref[...] = _dot(cv * _sigmoid(cv), w_ref[...], HIGHEST)

    return pl.pallas_call(body, name=name, out_shape=jax.ShapeDtypeStruct((c_all.shape[0], w_shard.shape[1]), F32))(c_all, w_shard)


def _ada_bwd(c_all, dmod_shard, name):
    def body(c_ref, d_ref, o_ref):
        cv = c_ref[...]
        o_ref[...] = _dot_tn(cv * _sigmoid(cv), d_ref[...], HIGHEST)

    return pl.pallas_call(body, name=name, out_shape=jax.ShapeDtypeStruct((c_all.shape[1], dmod_shard.shape[1]), F32))(c_all, dmod_shard)


def _sum_small(parts, name):
    def body(p_ref, o_ref):
        acc = p_ref[0]
        for d in range(1, N_DEV):
            acc = acc + p_ref[d]
        o_ref[...] = acc

    return pl.pallas_call(body, name=name, out_shape=jax.ShapeDtypeStruct(parts.shape[1:], F32))(parts)


def _adamw(w, g, m, v, name):
    rows, cols = w.shape
    tr = _pick(rows, 256, 8)
    spec = pl.BlockSpec((tr, cols), lambda i: (i, 0))
    bc1 = 1.0 - ADAM_B1 ** ADAM_STEP
    bc2 = 1.0 - ADAM_B2 ** ADAM_STEP

    def body(w_ref, g_ref, m_ref, v_ref, d_ref, nm_ref, nv_ref):
        gv = g_ref[...]
        nm = ADAM_B1 * m_ref[...] + (1.0 - ADAM_B1) * gv
        nv = ADAM_B2 * v_ref[...] + (1.0 - ADAM_B2) * (gv * gv)
        nm_ref[...] = nm
        nv_ref[...] = nv
        d_ref[...] = -ADAM_LR * ((nm / bc1) / (jnp.sqrt(nv / bc2) + ADAM_EPS) + ADAM_WD * w_ref[...])

    return pl.pallas_call(body, name=name, grid=(rows // tr,), in_specs=[spec] * 4, out_specs=[spec] * 3,
                          out_shape=[jax.ShapeDtypeStruct((rows, cols), F32)] * 3)(w, g, m, v)


def _pad_w_in_t(w_in_t):
    lo = D_SSD + D_CONV
    return jnp.concatenate([w_in_t[D_SSD:lo + N_HEADS], jnp.zeros((DT_PAD - N_HEADS, w_in_t.shape[1]), w_in_t.dtype),
                            w_in_t[:D_SSD], w_in_t[lo + N_HEADS:]], axis=0)


def _unpad_w_in_t(g):
    return jnp.concatenate([g[O_Z:O_Q], g[:O_DT + N_HEADS], g[O_Q:]], axis=0)


def _local_step(x, target, mod, norm1_w, w_in_tp, conv_w, conv_b, dt_bias, a_log, d_skip, ssd_norm_w, q_norm_w, k_norm_w,
                w_out, norm2_w, w_gate_t, w_up_t, w_down):
    sh1, sc1, g1, sh2, sc2, g2 = [mod[:, i * D_MODEL:(i + 1) * D_MODEL] for i in range(N_MOD)]
    qw2, kw2 = jnp.tile(q_norm_w, (1, 2)), jnp.tile(k_norm_w, (1, 2))
    dskip_x = jnp.repeat(d_skip, HEAD_DIM, axis=1)
    dtb_x, dtb_y = jnp.repeat(dt_bias, HEAD_DIM, axis=1), jnp.repeat(dt_bias, HEAD_TILE, axis=1)
    al_x, al_y = jnp.repeat(a_log, HEAD_DIM, axis=1), jnp.repeat(a_log, HEAD_TILE, axis=1)
    head_of_lane = jnp.arange(D_SSD, dtype=jnp.int32) // HEAD_DIM
    head_ones = (head_of_lane[:, None] == head_of_lane[None, :]).astype(BF16)

    h1 = _rms_mod(x, norm1_w, sc1, sh1, "rms_mod1")
    proj = _matmul_nt(h1, w_in_tp, "in_proj")
    dtr = proj[:, O_DT:O_DT + N_HEADS]
    dtr_x, dtr_y = jnp.repeat(dtr, HEAD_DIM, axis=1), jnp.repeat(dtr, HEAD_TILE, axis=1)

    xc = _conv_silu(proj, conv_w, conv_b, "conv_silu")
    ycore, states = _ssd_fwd(xc, dtr_x, dtr_y, dtb_x, dtb_y, al_x, al_y, "ssd_fwd")
    y_ssd = _ssd_gate_norm(ycore, xc, proj, dskip_x, ssd_norm_w, "ssd_gate_norm")

    qn, kn, vb = _qkv_prep(proj, qw2, kw2, "qkv_prep")
    o_sb, tot = _sb_fwd(qn, kn, vb, "sb_fwd")
    ycat = jnp.concatenate([y_ssd, o_sb.astype(MXU_DTYPE)], axis=1)

    mix = _matmul(ycat, w_out, "out_proj")
    x1, h2 = _residual_rms_mod(x, mix, g1, norm2_w, sc2, sh2, "residual_rms_mod2")
    gate = _matmul_nt(h2, w_gate_t, "ffn_gate")
    up = _matmul_nt(h2, w_up_t, "ffn_up")
    act = _swiglu(gate, up, "swiglu")
    ffn = _matmul(act, w_down, "ffn_down")
    dy, loss_blk = _loss_head(x1, ffn, g2, target, "loss_head")

    dffn, st_g2 = _gate_bwd(dy, ffn, g2, "gate2_bwd")
    g_down = _matmul_tn(act, dffn, "ffn_down_dw", MXU_DTYPE)
    dact = _matmul_nt(dffn, w_down, "ffn_down_dx")
    dgate, dup = _swiglu_bwd(gate, up, dact, "swiglu_bwd")
    g_gate_t = _matmul_tn(dgate, h2, "ffn_gate_dw", MXU_DTYPE)
    g_up_t = _matmul_tn(dup, h2, "ffn_up_dw", MXU_DTYPE)
    dh2 = _matmul_sum2(dgate, w_gate_t, dup, w_up_t, "ffn_dh")
    dx1, st_n2 = _norm_bwd(x1, dh2, dy, norm2_w, sc2, "norm2_bwd")

    dmix, st_g1 = _gate_bwd(dx1, mix, g1, "gate1_bwd")
    g_out = _matmul_tn(ycat, dmix, "out_proj_dw", MXU_DTYPE)
    dycat = _matmul_nt(dmix, w_out, "out_proj_dx")

    dqn, dkn, dv = _sb_bwd(qn, kn, vb, dycat, tot, "sb_bwd")
    dq, dk, st_q, st_k = _qk_norm_bwd(proj, dqn, dkn, qw2, kw2, "qk_norm_bwd")

    dycore, dxs_skip, dz, st_gn = _ssd_gate_norm_bwd(dycat, ycore, xc, proj, dskip_x, ssd_norm_w, "ssd_gate_norm_bwd")
    dxc, ddtr_x, st_ssd, st_heads = _ssd_bwd(xc, dtr_x, dtr_y, dtb_x, dtb_y, al_x, al_y, dycore, states, head_ones, "ssd_bwd")
    ddtr = ddtr_x[:, ::HEAD_DIM]
    dxc = jnp.concatenate([dxc[:, :D_SSD] + dxs_skip, dxc[:, D_SSD:]], axis=1)
    dpre, st_conv = _conv_silu_bwd_pre(proj, dxc, conv_w, conv_b, "conv_silu_bwd")
    dxbc = _conv_bwd_input(dpre, conv_w, "conv_bwd_input")

    dproj = jnp.concatenate([dxbc, ddtr, jnp.zeros((x.shape[0], DT_PAD - N_HEADS), F32), dz, dq, dk, dv], axis=1).astype(MXU_DTYPE)
    g_in_t = _unpad_w_in_t(_matmul_tn(dproj, h1, "in_proj_dw", MXU_DTYPE))
    dh1 = _matmul(dproj, w_in_tp, "in_proj_dx")
    grad_x, st_n1 = _norm_bwd(x, dh1, dx1, norm1_w, sc1, "norm1_bwd")

    pad = jnp.zeros((1, SM_SSD_NORM - SM_D_SKIP - N_HEADS), F32)
    small = jnp.concatenate(
        [st_n1[1:2], st_n1[0:1], st_g1[0:1], st_n2[1:2], st_n2[0:1], st_g2[0:1],
         st_n1[2:3], st_conv[4:5], st_conv[0:4].reshape(1, CONV_WIDTH * D_CONV),
         st_ssd[1:2, ::HEAD_DIM], st_ssd[0:1, ::HEAD_DIM], st_heads[2:3, ::HEAD_DIM], pad,
         st_gn[0:1], st_q[0:1, :HEAD_DIM] + st_q[0:1, HEAD_DIM:], st_k[0:1, :HEAD_DIM] + st_k[0:1, HEAD_DIM:], st_n2[2:3]], axis=1)
    return loss_blk, grad_x, (g_in_t, g_out, g_gate_t, g_up_t, g_down), small


def kernel(x, c, w_ada, b_ada, norm1_w, w_in, conv_w, conv_b, dt_bias, a_log, d_skip, ssd_norm_w, q_norm_w, k_norm_w, w_out, norm2_w, w_gate, w_up, w_down, loss_target, m_w_ada, m_b_ada, m_norm1_w, m_w_in, m_conv_w, m_conv_b, m_dt_bias, m_a_log, m_d_skip, m_ssd_norm_w, m_q_norm_w, m_k_norm_w, m_w_out, m_norm2_w, m_w_gate, m_w_up, m_w_down, v_w_ada, v_b_ada, v_norm1_w, v_w_in, v_conv_w, v_conv_b, v_dt_bias, v_a_log, v_d_skip, v_ssd_norm_w, v_q_norm_w, v_k_norm_w, v_w_out, v_norm2_w, v_w_gate, v_w_up, v_w_down):
    me = 4 * lax.axis_index("x") + 2 * lax.axis_index("y") + lax.axis_index("c")
    conv_cols = D_CONV // N_DEV
    ada_cols = N_MOD * D_MODEL // N_DEV

    shards = [w_in[0].T, w_out[0], w_gate[0].T, w_up[0].T, w_down[0]]
    gathered = _all_gather_big([s.astype(BF16) for s in shards], "gather_weights")
    w_in_t, w_out_f, w_gate_t, w_up_t, w_down_f = [g.reshape(N_DEV * g.shape[1], D_MODEL) for g in gathered]
    w_in_tp = _pad_w_in_t(w_in_t)

    first = _all_gather_small(jnp.concatenate([c, conv_w[0].reshape(1, CONV_WIDTH * conv_cols)], axis=1), "gather_cond")
    c_all = first[:, 0, :D_MODEL]
    conv_w_f = first[:, 0, D_MODEL:].reshape(N_DEV, CONV_WIDTH, conv_cols).transpose(1, 0, 2).reshape(CONV_WIDTH, D_CONV)
    mod_cols = _ada_fwd(c_all, w_ada[0], "ada_fwd")
    mods = _all_gather_small(mod_cols.reshape(1, N_DEV * ada_cols), "gather_mod")
    mod = lax.dynamic_index_in_dim(mods.reshape(N_DEV, N_DEV, ada_cols), me, axis=1, keepdims=False)
    mod = mod.reshape(1, N_MOD * D_MODEL) + b_ada

    loss_blk, grad_x, big, small = _local_step(
        x[0], loss_target[0], mod, norm1_w, w_in_tp, conv_w_f, conv_b, dt_bias, a_log, d_skip, ssd_norm_w, q_norm_w, k_norm_w,
        w_out_f, norm2_w, w_gate_t, w_up_t, w_down_f)
    loss = lax.psum(loss_blk[0, 0], ("x", "y", "c"))

    slots = _all_to_all_big([g.reshape(N_DEV, g.shape[0] // N_DEV, D_MODEL).astype(BF16) for g in big], "scatter_grads")
    g_in_t, g_out, g_gate_t, g_up_t, g_down = [_sum_slots(s, "sum_grads_" + n) for s, n in zip(slots, ("in", "out", "gate", "up", "down"))]
    g_in, g_gate, g_up = g_in_t.T, g_gate_t.T, g_up_t.T

    parts = _all_gather_small(small, "gather_small")
    gsum = _sum_small(parts, "sum_small")
    dmod_shard = lax.dynamic_slice_in_dim(parts[:, 0, :N_MOD * D_MODEL], me * ada_cols, ada_cols, axis=1)
    g_ada = _ada_bwd(c_all, dmod_shard, "ada_bwd")
    g_conv_w = lax.dynamic_slice_in_dim(gsum[:, SM_CONV_W:SM_DT_BIAS].reshape(CONV_WIDTH, D_CONV), me * conv_cols, conv_cols, axis=1)

    def pack_small(b_ada_, norm1_, conv_b_, dt_bias_, a_log_, d_skip_, ssd_norm_, q_norm_, k_norm_, norm2_):
        return jnp.concatenate(
            [b_ada_, norm1_, conv_b_, jnp.zeros((1, CONV_WIDTH * D_CONV), F32), dt_bias_, a_log_, d_skip_,
             jnp.zeros((1, SM_SSD_NORM - SM_D_SKIP - N_HEADS), F32), ssd_norm_, q_norm_, k_norm_, norm2_], axis=1)

    def unpack_small(p):
        return {
            "b_ada": p[:, SM_B_ADA:SM_NORM1], "norm1_w": p[:, SM_NORM1:SM_CONV_B], "conv_b": p[:, SM_CONV_B:SM_CONV_W],
            "dt_bias": p[:, SM_DT_BIAS:SM_A_LOG], "a_log": p[:, SM_A_LOG:SM_D_SKIP], "d_skip": p[:, SM_D_SKIP:SM_D_SKIP + N_HEADS],
            "ssd_norm_w": p[:, SM_SSD_NORM:SM_Q_NORM], "q_norm_w": p[:, SM_Q_NORM:SM_K_NORM], "k_norm_w": p[:, SM_K_NORM:SM_NORM2],
            "norm2_w": p[:, SM_NORM2:SM_TOTAL]}

    w_small = pack_small(b_ada, norm1_w, conv_b, dt_bias, a_log, d_skip, ssd_norm_w, q_norm_w, k_norm_w, norm2_w)
    m_small = pack_small(m_b_ada, m_norm1_w, m_conv_b, m_dt_bias, m_a_log, m_d_skip, m_ssd_norm_w, m_q_norm_w, m_k_norm_w, m_norm2_w)
    v_small = pack_small(v_b_ada, v_norm1_w, v_conv_b, v_dt_bias, v_a_log, v_d_skip, v_ssd_norm_w, v_q_norm_w, v_k_norm_w, v_norm2_w)
    small_out = [unpack_small(t) for t in (gsum,) + tuple(_adamw(w_small, gsum, m_small, v_small, "adamw_small"))]

    sharded = {
        "w_ada": (w_ada[0], g_ada, m_w_ada[0], v_w_ada[0]),
        "w_in": (w_in[0], g_in, m_w_in[0], v_w_in[0]),
        "conv_w": (conv_w[0], g_conv_w, m_conv_w[0], v_conv_w[0]),
        "w_out": (w_out[0], g_out, m_w_out[0], v_w_out[0]),
        "w_gate": (w_gate[0], g_gate, m_w_gate[0], v_w_gate[0]),
        "w_up": (w_up[0], g_up, m_w_up[0], v_w_up[0]),
        "w_down": (w_down[0], g_down, m_w_down[0], v_w_down[0]),
    }
    sharded_out = {n: (t[1],) + tuple(_adamw(*t, "adamw_" + n)) for n, t in sharded.items()}

    names = ["w_ada", "b_ada", "norm1_w", "w_in", "conv_w", "conv_b", "dt_bias", "a_log", "d_skip", "ssd_norm_w", "q_norm_w",
             "k_norm_w", "w_out", "norm2_w", "w_gate", "w_up", "w_down"]
    outs = [loss, grad_x[None]]
    for kind in range(4):
        for n in names:
            outs.append(sharded_out[n][kind][None] if n in sharded_out else small_out[kind][n])
    return tuple(outs)
```

```python
import functools

import jax
import jax.numpy as jnp
from jax import lax
from jax.experimental import pallas as pl
from jax.experimental.pallas import tpu as pltpu

F32 = jnp.float32
BF16 = jnp.bfloat16
MXU_DTYPE = jnp.bfloat16
HIGHEST = lax.Precision.HIGHEST
MESH_IDS = pl.DeviceIdType.MESH

N_DEV = 8
D_MODEL = 1024
HEAD_DIM = 64
N_HEADS = 16
D_SSD = 1024
D_SB = 1024
SSD_GROUPS = 2
SSD_STATE = 128
GROUP_WIDTH = D_SSD // SSD_GROUPS
D_CONV = D_SSD + 2 * SSD_GROUPS * SSD_STATE
CONV_WIDTH = 4
CHUNK = 128
D_FF = 2816
N_MOD = 6
EPS = 1e-6
D_IN_PROJ = 5648
O_XBC = 0
O_DT = D_CONV
DT_PAD = 512
O_Z = O_DT + DT_PAD
O_Q = O_Z + D_SSD
D_IN_PAD = O_Q + 3 * D_SB
HALO = 8

ADAM_LR = 0.001
ADAM_B1 = 0.9
ADAM_B2 = 0.999
ADAM_EPS = 1e-08
ADAM_WD = 0.01
ADAM_STEP = 10

ROWS_W_IN = 706
ROWS_W_OUT = 256
ROWS_FF = 352
PACK_ROWS = 2048
SM_B_ADA = 0
SM_NORM1 = 6144
SM_CONV_B = 7168
SM_CONV_W = 8704
SM_DT_BIAS = 14848
SM_A_LOG = 14864
SM_D_SKIP = 14880
SM_SSD_NORM = 14976
SM_Q_NORM = 16000
SM_K_NORM = 16064
SM_NORM2 = 16128
SM_TOTAL = 17152


def _pick(n, cap, mult):
    if n <= cap:
        return n
    best = None
    for t in range(mult, cap + 1, mult):
        if n % t == 0:
            best = t
    assert best is not None, (n, cap, mult)
    return best


def _dot(a, b, precision=None):
    return jnp.dot(a, b, preferred_element_type=F32, precision=precision)


def _dot_nt(a, b, precision=None):
    return lax.dot_general(a, b, (((1,), (1,)), ((), ())), preferred_element_type=F32, precision=precision)


def _dot_tn(a, b, precision=None):
    return lax.dot_general(a, b, (((0,), (0,)), ((), ())), preferred_element_type=F32, precision=precision)


def _softplus(v):
    return jnp.maximum(v, 0.0) + jnp.log1p(jnp.exp(-jnp.abs(v)))


def _sigmoid(v):
    return jax.nn.sigmoid(v)


def _colsum(v):
    return jnp.sum(v, axis=0, keepdims=True)


def _split_dot(v, tri2):
    hi = v.astype(BF16)
    lo = (v - hi.astype(F32)).astype(BF16)
    return _dot(jnp.concatenate([hi, lo], axis=1), tri2)


def _position():
    x, y, c = lax.axis_index("x"), lax.axis_index("y"), lax.axis_index("c")
    return x, y, c


def _peer(x, y, c, k):
    px = 1 - x if (k >> 2) & 1 else x
    py = 1 - y if (k >> 1) & 1 else y
    pc = 1 - c if k & 1 else c
    return px, py, pc


def _all_gather_small(v, name):
    n = v.shape[1]

    def body(v_ref, out_ref, send_sems, recv_sems, local_sem):
        x, y, c = _position()
        me = 4 * x + 2 * y + c
        mine = pltpu.make_async_copy(v_ref, out_ref.at[me], local_sem)
        mine.start()
        sends = []
        for k in range(1, N_DEV):
            cp = pltpu.make_async_remote_copy(
                src_ref=v_ref, dst_ref=out_ref.at[me], send_sem=send_sems.at[k - 1], recv_sem=recv_sems.at[k - 1],
                device_id=_peer(x, y, c, k), device_id_type=MESH_IDS)
            cp.start()
            sends.append(cp)
        for k in range(1, N_DEV):
            px, py, pc = _peer(x, y, c, k)
            pltpu.make_async_remote_copy(
                src_ref=v_ref, dst_ref=out_ref.at[4 * px + 2 * py + pc], send_sem=send_sems.at[k - 1],
                recv_sem=recv_sems.at[k - 1], device_id=(px, py, pc), device_id_type=MESH_IDS).wait_recv()
        for cp in sends:
            cp.wait_send()
        mine.wait()

    return pl.pallas_call(
        body, name=name,
        out_shape=jax.ShapeDtypeStruct((N_DEV, 1, n), v.dtype),
        in_specs=[pl.BlockSpec(memory_space=pltpu.VMEM)],
        out_specs=pl.BlockSpec(memory_space=pltpu.VMEM),
        scratch_shapes=[pltpu.SemaphoreType.DMA((N_DEV - 1,)), pltpu.SemaphoreType.DMA((N_DEV - 1,)),
                        pltpu.SemaphoreType.DMA],
    )(v)


def _all_gather_big(blocks, name):
    na = len(blocks)
    copies = N_DEV - 1

    def body(*refs):
        b_refs, out_refs = refs[:na], refs[na:2 * na]
        send_sems, recv_sems, local_sems = refs[2 * na:]
        x, y, c = _position()
        me, sibling = (x, y, c), (x, y, 1 - c)
        chips = [(1 - x, y), (x, 1 - y), (1 - x, 1 - y)]

        def copy(a, k, blk, to, own=False):
            slot = out_refs[a].at[4 * blk[0] + 2 * blk[1] + blk[2]]
            return pltpu.make_async_remote_copy(
                src_ref=b_refs[a] if own else slot, dst_ref=slot,
                send_sem=send_sems.at[copies * a + k], recv_sem=recv_sems.at[copies * a + k], device_id=to, device_id_type=MESH_IDS)

        mine = [pltpu.make_async_copy(b_refs[a], out_refs[a].at[4 * x + 2 * y + c], local_sems.at[a]) for a in range(na)]
        for cp in mine:
            cp.start()
        first = [copy(a, 0, me, sibling, own=True) for a in range(na)]
        first += [copy(a, 1 + j, me, (*chip, c), own=True) for j, chip in enumerate(chips) for a in range(na)]
        for cp in first:
            cp.start()
        passed = []
        for j, chip in enumerate(chips):
            for a in range(na):
                copy(a, 1 + j, (*chip, c), me).wait_recv()
                passed.append(copy(a, 4 + j, (*chip, c), sibling))
                passed[-1].start()
        for a in range(na):
            copy(a, 0, sibling, me).wait_recv()
        for j, chip in enumerate(chips):
            for a in range(na):
                copy(a, 4 + j, (*chip, 1 - c), me).wait_recv()
        for cp in first + passed:
            cp.wait_send()
        for cp in mine:
            cp.wait()

    hbm = pl.BlockSpec(memory_space=pl.ANY)
    return pl.pallas_call(
        body, name=name,
        out_shape=[jax.ShapeDtypeStruct((N_DEV,) + b.shape, b.dtype) for b in blocks],
        in_specs=[hbm] * na, out_specs=[hbm] * na,
        scratch_shapes=[pltpu.SemaphoreType.DMA((copies * na,)), pltpu.SemaphoreType.DMA((copies * na,)),
                        pltpu.SemaphoreType.DMA((na,))],
    )(*blocks)


class _DirectExchange:
    def __init__(self, kind, src_refs, dst_refs, send_sems, recv_sems, local_sems):
        self.kind, self.src_refs, self.dst_refs = kind, src_refs, dst_refs
        self.send_sems, self.recv_sems, self.local_sems = send_sems, recv_sems, local_sems

    @staticmethod
    def scratch(n_arrays):
        copies = N_DEV - 1
        return [pltpu.SemaphoreType.DMA((copies * n_arrays,)), pltpu.SemaphoreType.DMA((copies * n_arrays,)),
                pltpu.SemaphoreType.DMA((n_arrays,))]

    def _copies(self):
        x, y, c = _position()
        me = 4 * x + 2 * y + c
        local, sends, arrivals = [], [], []
        for a, (src, dst) in enumerate(zip(self.src_refs, self.dst_refs)):
            own = src if self.kind == "gather" else src.at[me]
            local.append(pltpu.make_async_copy(own, dst.at[me], self.local_sems.at[a]))
            for k in range(1, N_DEV):
                px, py, pc = _peer(x, y, c, k)
                peer = 4 * px + 2 * py + pc
                sems = dict(send_sem=self.send_sems.at[(N_DEV - 1) * a + k - 1], recv_sem=self.recv_sems.at[(N_DEV - 1) * a + k - 1],
                            device_id=(px, py, pc), device_id_type=MESH_IDS)
                sends.append(pltpu.make_async_remote_copy(
                    src_ref=src if self.kind == "gather" else src.at[peer], dst_ref=dst.at[me], **sems))
                arrivals.append(pltpu.make_async_remote_copy(src_ref=own, dst_ref=dst.at[peer], **sems))
        return local, sends, arrivals

    def start(self):
        local, sends, _ = self._copies()
        for cp in local + sends:
            cp.start()

    def wait(self):
        local, sends, arrivals = self._copies()
        for cp in arrivals:
            cp.wait_recv()
        for cp in sends:
            cp.wait_send()
        for cp in local:
            cp.wait()


def _all_to_all_big(blocks, name):
    na = len(blocks)

    def body(*refs):
        exchange = _DirectExchange("scatter", refs[:na], refs[na:2 * na], *refs[2 * na:])
        exchange.start()
        exchange.wait()

    hbm = pl.BlockSpec(memory_space=pl.ANY)
    return pl.pallas_call(
        body, name=name,
        out_shape=[jax.ShapeDtypeStruct(b.shape, b.dtype) for b in blocks],
        in_specs=[hbm] * na, out_specs=[hbm] * na, scratch_shapes=_DirectExchange.scratch(na),
    )(*blocks)


def _sum_slots(slots, name):
    _, r, n = slots.shape
    tr = _pick(r, 256, 16) if r % 16 == 0 else r

    def body(s_ref, o_ref):
        acc = s_ref[0].astype(F32)
        for d in range(1, N_DEV):
            acc = acc + s_ref[d].astype(F32)
        o_ref[...] = acc

    return pl.pallas_call(
        body, name=name, grid=(r // tr,),
        in_specs=[pl.BlockSpec((N_DEV, tr, n), lambda i: (0, i, 0))],
        out_specs=pl.BlockSpec((tr, n), lambda i: (i, 0)),
        out_shape=jax.ShapeDtypeStruct((r, n), F32),
    )(slots)


def _matmul(a, b, name, out_dtype=F32):
    m, k = a.shape
    _, n = b.shape
    tm, tn, tk = _pick(m, 1024, 16), _pick(n, 1408, 128), _pick(k, 1408, 128)
    nk = k // tk

    def body(a_ref, b_ref, o_ref, acc_ref):
        kk = pl.program_id(2)

        @pl.when(kk == 0)
        def _():
            acc_ref[...] = jnp.zeros_like(acc_ref)

        acc_ref[...] += _dot(a_ref[...].astype(MXU_DTYPE), b_ref[...].astype(MXU_DTYPE))

        @pl.when(kk == nk - 1)
        def _():
            o_ref[...] = acc_ref[...].astype(o_ref.dtype)

    return pl.pallas_call(
        body, name=name, grid=(m // tm, n // tn, nk),
        in_specs=[pl.BlockSpec((tm, tk), lambda i, j, kk: (i, kk)), pl.BlockSpec((tk, tn), lambda i, j, kk: (kk, j))],
        out_specs=pl.BlockSpec((tm, tn), lambda i, j, kk: (i, j)),
        out_shape=jax.ShapeDtypeStruct((m, n), out_dtype),
        scratch_shapes=[pltpu.VMEM((tm, tn), F32)],
    )(a, b)


def _matmul_nt(a, bt, name, out_dtype=F32):
    m, k = a.shape
    n, _ = bt.shape
    tm, tn, tk = _pick(m, 1024, 16), _pick(n, 1408, 128), _pick(k, 1408, 128)
    nk = k // tk

    def body(a_ref, b_ref, o_ref, acc_ref):
        kk = pl.program_id(2)

        @pl.when(kk == 0)
        def _():
            acc_ref[...] = jnp.zeros_like(acc_ref)

        acc_ref[...] += _dot_nt(a_ref[...].astype(MXU_DTYPE), b_ref[...].astype(MXU_DTYPE))

        @pl.when(kk == nk - 1)
        def _():
            o_ref[...] = acc_ref[...].astype(o_ref.dtype)

    return pl.pallas_call(
        body, name=name, grid=(m // tm, n // tn, nk),
        in_specs=[pl.BlockSpec((tm, tk), lambda i, j, kk: (i, kk)), pl.BlockSpec((tn, tk), lambda i, j, kk: (j, kk))],
        out_specs=pl.BlockSpec((tm, tn), lambda i, j, kk: (i, j)),
        out_shape=jax.ShapeDtypeStruct((m, n), out_dtype),
        scratch_shapes=[pltpu.VMEM((tm, tn), F32)],
    )(a, bt)


def _matmul_sum2(a1, b1, a2, b2, name):
    m, k = a1.shape
    _, n = b1.shape
    tm, tn, tk = _pick(m, 1024, 16), _pick(n, 1408, 128), _pick(k, 1408, 128)
    nk = k // tk

    def body(a1_ref, b1_ref, a2_ref, b2_ref, o_ref):
        @pl.when(pl.program_id(2) == 0)
        def _():
            o_ref[...] = jnp.zeros_like(o_ref)

        o_ref[...] += (_dot(a1_ref[...].astype(MXU_DTYPE), b1_ref[...].astype(MXU_DTYPE))
                       + _dot(a2_ref[...].astype(MXU_DTYPE), b2_ref[...].astype(MXU_DTYPE)))

    a_spec = pl.BlockSpec((tm, tk), lambda i, j, kk: (i, kk))
    b_spec = pl.BlockSpec((tk, tn), lambda i, j, kk: (kk, j))
    return pl.pallas_call(
        body, name=name, grid=(m // tm, n // tn, nk), in_specs=[a_spec, b_spec, a_spec, b_spec],
        out_specs=pl.BlockSpec((tm, tn), lambda i, j, kk: (i, j)),
        out_shape=jax.ShapeDtypeStruct((m, n), F32),
    )(a1, b1, a2, b2)


def _matmul_tn(a, b, name, out_dtype=F32):
    l, m = a.shape
    _, n = b.shape
    tm, tn, tl = _pick(m, 1408, 128), _pick(n, 1408, 128), _pick(l, 512, 16)
    nl = l // tl

    def body(a_ref, b_ref, o_ref, acc_ref):
        ll = pl.program_id(2)

        @pl.when(ll == 0)
        def _():
            acc_ref[...] = jnp.zeros_like(acc_ref)

        acc_ref[...] += _dot_tn(a_ref[...].astype(MXU_DTYPE), b_ref[...].astype(MXU_DTYPE))

        @pl.when(ll == nl - 1)
        def _():
            o_ref[...] = acc_ref[...].astype(o_ref.dtype)

    return pl.pallas_call(
        body, name=name, grid=(m // tm, n // tn, nl),
        in_specs=[pl.BlockSpec((tl, tm), lambda i, j, ll: (ll, i)), pl.BlockSpec((tl, tn), lambda i, j, ll: (ll, j))],
        out_specs=pl.BlockSpec((tm, tn), lambda i, j, ll: (i, j)),
        out_shape=jax.ShapeDtypeStruct((m, n), out_dtype),
        scratch_shapes=[pltpu.VMEM((tm, tn), F32)],
    )(a, b)


def _row_specs(l, d, n_rows, n_vecs):
    tl = _pick(l, 512, 16)
    row = pl.BlockSpec((tl, d), lambda i: (i, 0))
    vec = pl.BlockSpec((1, d), lambda i: (0, 0))
    return tl, row, vec, [row] * n_rows + [vec] * n_vecs


def _rms_mod(x, nw, sc, sh, name):
    l, d = x.shape
    tl, row, _, in_specs = _row_specs(l, d, 1, 3)

    def body(x_ref, nw_ref, sc_ref, sh_ref, h_ref):
        xv = x_ref[...]
        r = lax.rsqrt(jnp.mean(xv * xv, axis=-1, keepdims=True) + EPS)
        h_ref[...] = (((xv * r) * nw_ref[...]) * (1.0 + sc_ref[...]) + sh_ref[...]).astype(h_ref.dtype)

    return pl.pallas_call(body, name=name, grid=(l // tl,), in_specs=in_specs, out_specs=row,
                          out_shape=jax.ShapeDtypeStruct((l, d), MXU_DTYPE))(x, nw, sc, sh)


def _residual_rms_mod(x, mix, g, nw, sc, sh, name):
    l, d = x.shape
    tl, row, _, in_specs = _row_specs(l, d, 2, 4)

    def body(x_ref, mix_ref, g_ref, nw_ref, sc_ref, sh_ref, x1_ref, h_ref):
        xv = x_ref[...] + g_ref[...] * mix_ref[...]
        x1_ref[...] = xv
        r = lax.rsqrt(jnp.mean(xv * xv, axis=-1, keepdims=True) + EPS)
        h_ref[...] = (((xv * r) * nw_ref[...]) * (1.0 + sc_ref[...]) + sh_ref[...]).astype(h_ref.dtype)

    return pl.pallas_call(body, name=name, grid=(l // tl,), in_specs=in_specs, out_specs=[row, row],
                          out_shape=[jax.ShapeDtypeStruct((l, d), F32), jax.ShapeDtypeStruct((l, d), MXU_DTYPE)],
                          )(x, mix, g, nw, sc, sh)


def _norm_bwd(x, dh, dres, nw, sc, name):
    l, d = x.shape
    tl, row, _, in_specs = _row_specs(l, d, 3, 2)

    def body(x_ref, dh_ref, dres_ref, nw_ref, sc_ref, dx_ref, st_ref):
        xv, dh_v = x_ref[...], dh_ref[...]
        r = lax.rsqrt(jnp.mean(xv * xv, axis=-1, keepdims=True) + EPS)
        xn = xv * r
        dxn = dh_v * (nw_ref[...] * (1.0 + sc_ref[...]))
        dx_ref[...] = dres_ref[...] + r * (dxn - xn * jnp.mean(dxn * xn, axis=-1, keepdims=True))

        @pl.when(pl.program_id(0) == 0)
        def _():
            st_ref[...] = jnp.zeros_like(st_ref)

        dhx = dh_v * xn
        st_ref[0:1, :] += _colsum(dhx * nw_ref[...])
        st_ref[1:2, :] += _colsum(dh_v)
        st_ref[2:3, :] += _colsum(dhx * (1.0 + sc_ref[...]))

    return pl.pallas_call(body, name=name, grid=(l // tl,), in_specs=in_specs,
                          out_specs=[row, pl.BlockSpec((8, d), lambda i: (0, 0))],
                          out_shape=[jax.ShapeDtypeStruct((l, d), F32), jax.ShapeDtypeStruct((8, d), F32)],
                          )(x, dh, dres, nw, sc)


def _gate_bwd(dres, val, g, name):
    l, d = dres.shape
    tl, row, _, in_specs = _row_specs(l, d, 2, 1)

    def body(dres_ref, val_ref, g_ref, dval_ref, st_ref):
        dr = dres_ref[...]
        dval_ref[...] = (g_ref[...] * dr).astype(dval_ref.dtype)

        @pl.when(pl.program_id(0) == 0)
        def _():
            st_ref[...] = jnp.zeros_like(st_ref)

        st_ref[0:1, :] += _colsum(dr * val_ref[...])

    return pl.pallas_call(body, name=name, grid=(l // tl,), in_specs=in_specs,
                          out_specs=[row, pl.BlockSpec((8, d), lambda i: (0, 0))],
                          out_shape=[jax.ShapeDtypeStruct((l, d), MXU_DTYPE), jax.ShapeDtypeStruct((8, d), F32)],
                          )(dres, val, g)


def _loss_head(x1, f, g, target, name):
    l, d = x1.shape
    tl, row, _, _ = _row_specs(l, d, 0, 0)
    vec = pl.BlockSpec((1, d), lambda i: (0, 0))

    def body(x1_ref, f_ref, g_ref, t_ref, dy_ref, loss_ref):
        e = x1_ref[...] + g_ref[...] * f_ref[...] - t_ref[...]
        dy_ref[...] = e * (1.0 / d)

        @pl.when(pl.program_id(0) == 0)
        def _():
            loss_ref[...] = jnp.zeros_like(loss_ref)

        s = jnp.sum(jnp.sum(e * e, axis=1, keepdims=True), axis=0, keepdims=True)
        loss_ref[...] += (0.5 / d) * s

    return pl.pallas_call(body, name=name, grid=(l // tl,), in_specs=[row, row, vec, row],
                          out_specs=[row, pl.BlockSpec((8, 128), lambda i: (0, 0))],
                          out_shape=[jax.ShapeDtypeStruct((l, d), F32), jax.ShapeDtypeStruct((8, 128), F32)],
                          )(x1, f, g, target)


def _swiglu(gate, up, name):
    l, f = gate.shape
    tl, tf = _pick(l, 512, 16), _pick(f, 1408, 128)
    spec = pl.BlockSpec((tl, tf), lambda i, j: (i, j))

    def body(g_ref, u_ref, a_ref):
        gv = g_ref[...]
        a_ref[...] = (gv * _sigmoid(gv) * u_ref[...]).astype(a_ref.dtype)

    return pl.pallas_call(body, name=name, grid=(l // tl, f // tf), in_specs=[spec, spec], out_specs=spec,
                          out_shape=jax.ShapeDtypeStruct((l, f), MXU_DTYPE))(gate, up)


def _swiglu_bwd(gate, up, da, name):
    l, f = gate.shape
    tl, tf = _pick(l, 512, 16), _pick(f, 1408, 128)
    spec = pl.BlockSpec((tl, tf), lambda i, j: (i, j))

    def body(g_ref, u_ref, da_ref, dg_ref, du_ref):
        gv, dav = g_ref[...], da_ref[...]
        s = _sigmoid(gv)
        dg_ref[...] = (dav * u_ref[...] * (s * (1.0 + gv * (1.0 - s)))).astype(dg_ref.dtype)
        du_ref[...] = (dav * (gv * s)).astype(du_ref.dtype)

    return pl.pallas_call(body, name=name, grid=(l // tl, f // tf), in_specs=[spec, spec, spec], out_specs=[spec, spec],
                          out_shape=[jax.ShapeDtypeStruct((l, f), MXU_DTYPE)] * 2)(gate, up, da)


def _conv_tile(l):
    return _pick(l, 512, 16)


def _conv_pre(buf, w_ref, b_ref, tl):
    acc = b_ref[...] + w_ref[3:4, :] * buf[HALO:HALO + tl, :]
    for k in range(CONV_WIDTH - 1):
        s = HALO - (CONV_WIDTH - 1) + k
        acc = acc + w_ref[k:k + 1, :] * buf[s:s + tl, :]
    return acc


def _fill_past(buf, u_ref, halo_ref, tl):
    i = pl.program_id(0)

    @pl.when(i == 0)
    def _():
        buf[0:HALO, :] = jnp.zeros((HALO, buf.shape[1]), F32)

    @pl.when(i > 0)
    def _():
        buf[0:HALO, :] = halo_ref[...]

    buf[HALO:HALO + tl, :] = u_ref[...]


def _conv_silu(u, w, b, name):
    l, ch = u.shape[0], w.shape[1]
    tl = _conv_tile(l)
    per = tl // HALO
    cur = pl.BlockSpec((tl, ch), lambda i: (i, 0))
    past = pl.BlockSpec((HALO, ch), lambda i: (jnp.maximum(i * per - 1, 0), 0))

    def body(u_ref, halo_ref, w_ref, b_ref, o_ref, buf):
        _fill_past(buf, u_ref, halo_ref, tl)
        pre = _conv_pre(buf, w_ref, b_ref, tl)
        o_ref[...] = pre * _sigmoid(pre)

    return pl.pallas_call(body, name=name, grid=(l // tl,),
                          in_specs=[cur, past, pl.BlockSpec((CONV_WIDTH, ch), lambda i: (0, 0)), pl.BlockSpec((1, ch), lambda i: (0, 0))],
                          out_specs=cur, out_shape=jax.ShapeDtypeStruct((l, ch), F32),
                          scratch_shapes=[pltpu.VMEM((tl + HALO, ch), F32)])(u, u, w, b)


def _conv_silu_bwd_pre(u, dxc, w, b, name):
    l, ch = u.shape[0], w.shape[1]
    tl = _conv_tile(l)
    per = tl // HALO
    cur = pl.BlockSpec((tl, ch), lambda i: (i, 0))
    past = pl.BlockSpec((HALO, ch), lambda i: (jnp.maximum(i * per - 1, 0), 0))

    def body(u_ref, halo_ref, d_ref, w_ref, b_ref, dpre_ref, st_ref, buf):
        _fill_past(buf, u_ref, halo_ref, tl)
        pre = _conv_pre(buf, w_ref, b_ref, tl)
        s = _sigmoid(pre)
        dpre = d_ref[...] * (s * (1.0 + pre * (1.0 - s)))
        dpre_ref[...] = dpre

        @pl.when(pl.program_id(0) == 0)
        def _():
            st_ref[...] = jnp.zeros_like(st_ref)

        for k in range(CONV_WIDTH):
            s0 = HALO - (CONV_WIDTH - 1) + k
            st_ref[k:k + 1, :] += _colsum(dpre * buf[s0:s0 + tl, :])
        st_ref[CONV_WIDTH:CONV_WIDTH + 1, :] += _colsum(dpre)

    return pl.pallas_call(body, name=name, grid=(l // tl,),
                          in_specs=[cur, past, cur, pl.BlockSpec((CONV_WIDTH, ch), lambda i: (0, 0)), pl.BlockSpec((1, ch), lambda i: (0, 0))],
                          out_specs=[cur, pl.BlockSpec((8, ch), lambda i: (0, 0))],
                          out_shape=[jax.ShapeDtypeStruct((l, ch), F32), jax.ShapeDtypeStruct((8, ch), F32)],
                          scratch_shapes=[pltpu.VMEM((tl + HALO, ch), F32)])(u, u, dxc, w, b)


def _conv_bwd_input(dpre, w, name):
    l, ch = dpre.shape
    tl = _conv_tile(l)
    per = tl // HALO
    nt = l // tl
    cur = pl.BlockSpec((tl, ch), lambda i: (i, 0))
    nxt = pl.BlockSpec((HALO, ch), lambda i: (jnp.minimum((i + 1) * per, l // HALO - 1), 0))

    def body(d_ref, halo_ref, w_ref, du_ref, buf):
        i = pl.program_id(0)
        buf[0:tl, :] = d_ref[...]

        @pl.when(i == nt - 1)
        def _():
            buf[tl:tl + HALO, :] = jnp.zeros((HALO, ch), F32)

        @pl.when(i < nt - 1)
        def _():
            buf[tl:tl + HALO, :] = halo_ref[...]

        acc = w_ref[3:4, :] * buf[0:tl, :]
        for k in range(CONV_WIDTH - 1):
            s = CONV_WIDTH - 1 - k
            acc = acc + w_ref[k:k + 1, :] * buf[s:s + tl, :]
        du_ref[...] = acc

    return pl.pallas_call(body, name=name, grid=(nt,),
                          in_specs=[cur, nxt, pl.BlockSpec((CONV_WIDTH, ch), lambda i: (0, 0))],
                          out_specs=cur, out_shape=jax.ShapeDtypeStruct((l, ch), F32),
                          scratch_shapes=[pltpu.VMEM((tl + HALO, ch), F32)])(dpre, dpre, w)


HEAD_TILE = 128
D_HEAD_TILES = N_HEADS * HEAD_TILE


def _chunk_iota():
    r = lax.broadcasted_iota(jnp.int32, (CHUNK, CHUNK), 0)
    c = lax.broadcasted_iota(jnp.int32, (CHUNK, CHUNK), 1)
    return r, c


def _split3_dot(v, ones_b):
    hi = v.astype(BF16)
    r1 = v - hi.astype(F32)
    mid = r1.astype(BF16)
    lo = (r1 - mid.astype(F32)).astype(BF16)
    return _dot(hi, ones_b) + _dot(mid, ones_b) + _dot(lo, ones_b)


def _split3_dot_nt(v, ones_b):
    hi = v.astype(BF16)
    r1 = v - hi.astype(F32)
    mid = r1.astype(BF16)
    lo = (r1 - mid.astype(F32)).astype(BF16)
    return _dot_nt(hi, ones_b) + _dot_nt(mid, ones_b) + _dot_nt(lo, ones_b)


def _ssd_decays(dtx_ref, dty_ref, dtbx_ref, dtby_ref, alx_ref, aly_ref, r, c):
    tri = (r >= c).astype(F32)
    dt_x = _softplus(dtx_ref[...] + dtbx_ref[...])
    a_x = -jnp.exp(alx_ref[...])
    adt_x = a_x * dt_x
    acs_x = _dot(tri, adt_x, HIGHEST)
    atot_x = _colsum(adt_x)
    acs_y = _dot(tri, -jnp.exp(aly_ref[...]) * _softplus(dty_ref[...] + dtby_ref[...]), HIGHEST)
    return dt_x, a_x, acs_y, jnp.exp(acs_x), jnp.exp(atot_x - acs_x), jnp.exp(atot_x)


def _head_decay(acs_y, e, r, c):
    col = acs_y[:, e * HEAD_TILE:(e + 1) * HEAD_TILE]
    return jnp.where(r >= c, jnp.exp(col - col.T), 0.0)


def _half_masks():
    lane = lax.broadcasted_iota(jnp.int32, (CHUNK, 2 * HEAD_DIM), 1)
    return lane < HEAD_DIM, lane >= HEAD_DIM


def _ssd_specs(nc, reverse):
    def at(i):
        return nc - 1 - i if reverse else i
    xc = pl.BlockSpec((CHUNK, D_CONV), lambda i: (at(i), 0))
    wide = pl.BlockSpec((CHUNK, D_SSD), lambda i: (at(i), 0))
    wide_y = pl.BlockSpec((CHUNK, D_HEAD_TILES), lambda i: (at(i), 0))
    vec = pl.BlockSpec((1, D_SSD), lambda i: (0, 0))
    vec_y = pl.BlockSpec((1, D_HEAD_TILES), lambda i: (0, 0))
    state = pl.BlockSpec((1, SSD_STATE, D_SSD), lambda i: (at(i), 0, 0))
    return xc, wide, wide_y, vec, vec_y, state


def _ssd_fwd(xc, dtr_x, dtr_y, dtb_x, dtb_y, al_x, al_y, name):
    l = xc.shape[0]
    nc = l // CHUNK
    xc_s, wide_s, wide_y_s, vec_s, vec_y_s, state_s = _ssd_specs(nc, False)
    pairs_per_group = GROUP_WIDTH // (2 * HEAD_DIM)

    def body(xc_ref, dtx_ref, dty_ref, dtbx_ref, dtby_ref, alx_ref, aly_ref, y_ref, sp_ref, state):
        @pl.when(pl.program_id(0) == 0)
        def _():
            state[...] = jnp.zeros_like(state)

        r, c = _chunk_iota()
        halves = _half_masks()
        dt_x, _, acs_y, ea_x, ds_x, eatot_x = _ssd_decays(dtx_ref, dty_ref, dtbx_ref, dtby_ref, alx_ref, aly_ref, r, c)
        xg = xc_ref[:, 0:D_SSD] * dt_x
        sp_ref[0] = state[...]
        for g in range(SSD_GROUPS):
            lanes = slice(g * GROUP_WIDTH, (g + 1) * GROUP_WIDTH)
            bb = xc_ref[:, D_SSD + g * SSD_STATE:D_SSD + (g + 1) * SSD_STATE].astype(MXU_DTYPE)
            cb = xc_ref[:, D_SSD + (SSD_GROUPS + g) * SSD_STATE:D_SSD + (SSD_GROUPS + g + 1) * SSD_STATE].astype(MXU_DTYPE)
            scores = _dot_nt(cb, bb)
            sg = state[:, lanes]
            ys = []
            for j in range(g * pairs_per_group, (g + 1) * pairs_per_group):
                xg_pair = xg[:, j * 2 * HEAD_DIM:(j + 1) * 2 * HEAD_DIM]
                acc = jnp.zeros((CHUNK, 2 * HEAD_DIM), F32)
                for half in range(2):
                    m = (scores * _head_decay(acs_y, 2 * j + half, r, c)).astype(MXU_DTYPE)
                    acc = acc + _dot(m, jnp.where(halves[half], xg_pair, 0.0).astype(MXU_DTYPE))
                ys.append(acc)
            y_ref[:, lanes] = jnp.concatenate(ys, axis=1) + _dot(cb, sg.astype(MXU_DTYPE)) * ea_x[:, lanes]
            state[:, lanes] = sg * eatot_x[:, lanes] + _dot_tn(bb, (xg[:, lanes] * ds_x[:, lanes]).astype(MXU_DTYPE))

    return pl.pallas_call(
        body, name=name, grid=(nc,),
        in_specs=[xc_s, wide_s, wide_y_s, vec_s, vec_y_s, vec_s, vec_y_s],
        out_specs=[wide_s, state_s],
        out_shape=[jax.ShapeDtypeStruct((l, D_SSD), F32), jax.ShapeDtypeStruct((nc, SSD_STATE, D_SSD), F32)],
        scratch_shapes=[pltpu.VMEM((SSD_STATE, D_SSD), F32)],
    )(xc, dtr_x, dtr_y, dtb_x, dtb_y, al_x, al_y)


def _ssd_bwd(xc, dtr_x, dtr_y, dtb_x, dtb_y, al_x, al_y, dy, states, head_of_lanes, name):
    l = xc.shape[0]
    nc = l // CHUNK
    xc_s, wide_s, wide_y_s, vec_s, vec_y_s, state_s = _ssd_specs(nc, True)
    pairs_per_group = GROUP_WIDTH // (2 * HEAD_DIM)

    def body(xc_ref, dtx_ref, dty_ref, dtbx_ref, dtby_ref, alx_ref, aly_ref, dy_ref, sp_ref, ones_ref,
             dxc_ref, ddtr_ref, st_ref, hs_ref, dstate):
        @pl.when(pl.program_id(0) == 0)
        def _():
            dstate[...] = jnp.zeros_like(dstate)
            st_ref[...] = jnp.zeros_like(st_ref)

        r, c = _chunk_iota()
        halves = _half_masks()
        spread = ones_ref[...]

        def head_sums(v):
            return _split3_dot(_split3_dot_nt(v, spread), spread)

        dt_x, a_x, acs_y, ea_x, ds_x, eatot_x = _ssd_decays(dtx_ref, dty_ref, dtbx_ref, dtby_ref, alx_ref, aly_ref, r, c)
        xs = xc_ref[:, 0:D_SSD]
        xg = xs * dt_x
        gy = dy_ref[...]
        s_prev = sp_ref[0]
        gea = gy * ea_x
        xds = xg * ds_x
        ds_old = dstate[...]
        later2 = jnp.concatenate([(r <= c).astype(BF16)] * 2, axis=1)
        dxg_parts, state_term, yoff_parts, dadt_parts = [], [], [], []
        for g in range(SSD_GROUPS):
            lanes = slice(g * GROUP_WIDTH, (g + 1) * GROUP_WIDTH)
            b_lo = D_SSD + g * SSD_STATE
            c_lo = D_SSD + (SSD_GROUPS + g) * SSD_STATE
            bb = xc_ref[:, b_lo:b_lo + SSD_STATE].astype(MXU_DTYPE)
            cb = xc_ref[:, c_lo:c_lo + SSD_STATE].astype(MXU_DTYPE)
            scores = _dot_nt(cb, bb)
            dsg = ds_old[:, lanes].astype(MXU_DTYPE)
            gea_b = gea[:, lanes].astype(MXU_DTYPE)
            xds_b = xds[:, lanes].astype(MXU_DTYPE)
            dxg_state = _dot(bb, dsg) * ds_x[:, lanes]
            dc = _dot_nt(gea_b, s_prev[:, lanes].astype(MXU_DTYPE))
            db = _dot_nt(xds_b, dsg)
            dscores = jnp.zeros((CHUNK, CHUNK), F32)
            diag = []
            for j in range(g * pairs_per_group, (g + 1) * pairs_per_group):
                pair = slice(j * 2 * HEAD_DIM, (j + 1) * 2 * HEAD_DIM)
                xg_pair = xg[:, pair].astype(MXU_DTYPE)
                acc = jnp.zeros((CHUNK, 2 * HEAD_DIM), F32)
                cols = []
                for half in range(2):
                    decay = _head_decay(acs_y, 2 * j + half, r, c)
                    g_e = jnp.where(halves[half], gy[:, pair], 0.0).astype(MXU_DTYPE)
                    acc = acc + _dot_tn((scores * decay).astype(MXU_DTYPE), g_e)
                    dm = _dot_nt(g_e, xg_pair) * decay
                    dscores = dscores + dm
                    wq = dm * scores
                    hi = wq.astype(BF16)
                    lo = (wq - hi.astype(F32)).astype(BF16)
                    later = _dot(later2, jnp.concatenate([hi, lo], axis=0))
                    cols.append(jnp.sum(jnp.where(c < r, later, 0.0), axis=1, keepdims=True))
                diag.append(acc)
                dadt_parts.append(jnp.where(halves[0], cols[0], cols[1]))
            dsc_b = dscores.astype(MXU_DTYPE)
            dc = dc + _dot(dsc_b, bb)
            db = db + _dot_tn(dsc_b, cb)
            dxc_ref[:, b_lo:b_lo + SSD_STATE] = db
            dxc_ref[:, c_lo:c_lo + SSD_STATE] = dc
            dxg_parts.append(jnp.concatenate(diag, axis=1) + dxg_state)
            state_term.append(dxg_state)
            yoff_parts.append(_dot(cb, s_prev[:, lanes].astype(MXU_DTYPE)) * ea_x[:, lanes])
            dstate[:, lanes] = ds_old[:, lanes] * eatot_x[:, lanes] + _dot_tn(cb, gea_b)
        dxg = jnp.concatenate(dxg_parts, axis=1)
        dxc_ref[:, 0:D_SSD] = dxg * dt_x
        through_out = _dot((r <= c).astype(F32), gy * jnp.concatenate(yoff_parts, axis=1), HIGHEST)
        through_in = _dot((c < r).astype(F32), xg * jnp.concatenate(state_term, axis=1), HIGHEST)
        carried = jnp.broadcast_to(_colsum(ds_old * s_prev) * eatot_x, (8, D_SSD))
        dadt = (jnp.concatenate(dadt_parts, axis=1) + head_sums(through_out + through_in)
                + jnp.max(head_sums(carried), axis=0, keepdims=True))
        ddt = a_x * dadt + head_sums(dxg * xs)
        draw = ddt * _sigmoid(dtx_ref[...] + dtbx_ref[...])
        ddtr_ref[...] = draw
        st_ref[0:1, :] += _colsum(dt_x * dadt) * a_x
        st_ref[1:2, :] += _colsum(draw)
        st_ref[2:3, :] += _colsum(gy * xs)

        @pl.when(pl.program_id(0) == nc - 1)
        def _():
            hs_ref[...] = head_sums(st_ref[...])

    stats = pl.BlockSpec((8, D_SSD), lambda i: (0, 0))
    return pl.pallas_call(
        body, name=name, grid=(nc,),
        in_specs=[xc_s, wide_s, wide_y_s, vec_s, vec_y_s, vec_s, vec_y_s, wide_s, state_s,
                  pl.BlockSpec((HEAD_TILE, D_SSD), lambda i: (0, 0))],
        out_specs=[xc_s, wide_s, stats, stats],
        out_shape=[jax.ShapeDtypeStruct((l, D_CONV), F32), jax.ShapeDtypeStruct((l, D_SSD), F32),
                   jax.ShapeDtypeStruct((8, D_SSD), F32), jax.ShapeDtypeStruct((8, D_SSD), F32)],
        scratch_shapes=[pltpu.VMEM((SSD_STATE, D_SSD), F32)],
    )(xc, dtr_x, dtr_y, dtb_x, dtb_y, al_x, al_y, dy, states, head_of_lanes)


def _ssd_gate_norm(ycore, xc, z, dskip_x, norm_w, name):
    l = ycore.shape[0]
    tl = _pick(l, 512, 16)
    row = pl.BlockSpec((tl, D_SSD), lambda i: (i, 0))
    vec = pl.BlockSpec((1, D_SSD), lambda i: (0, 0))

    def body(y_ref, xs_ref, z_ref, dk_ref, nw_ref, o_ref):
        zv = z_ref[...]
        yv = (y_ref[...] + dk_ref[...] * xs_ref[...]) * (zv * _sigmoid(zv))
        for g in range(SSD_GROUPS):
            lanes = slice(g * GROUP_WIDTH, (g + 1) * GROUP_WIDTH)
            yg = yv[:, lanes]
            rg = lax.rsqrt(jnp.mean(yg * yg, axis=-1, keepdims=True) + EPS)
            o_ref[:, lanes] = (yg * rg * nw_ref[:, lanes]).astype(o_ref.dtype)

    z_spec = pl.BlockSpec((tl, D_SSD), lambda i: (i, O_Z // D_SSD))
    return pl.pallas_call(body, name=name, grid=(l // tl,), in_specs=[row, row, z_spec, vec, vec], out_specs=row,
                          out_shape=jax.ShapeDtypeStruct((l, D_SSD), MXU_DTYPE))(ycore, xc, z, dskip_x, norm_w)


def _ssd_gate_norm_bwd(dout, ycore, xc, z, dskip_x, norm_w, name):
    l = ycore.shape[0]
    tl = _pick(l, 512, 16)
    nt = l // tl
    row = pl.BlockSpec((tl, D_SSD), lambda i: (i, 0))
    vec = pl.BlockSpec((1, D_SSD), lambda i: (0, 0))

    def body(do_ref, y_ref, xs_ref, z_ref, dk_ref, nw_ref, dyc_ref, dxs_ref, dz_ref, st_ref):
        @pl.when(pl.program_id(0) == 0)
        def _():
            st_ref[...] = jnp.zeros_like(st_ref)

        zv, xs = z_ref[...], xs_ref[...]
        s = _sigmoid(zv)
        gz = zv * s
        yc = y_ref[...] + dk_ref[...] * xs
        yv = yc * gz
        dov = do_ref[...]
        dnw, dyv = [], []
        for g in range(SSD_GROUPS):
            lanes = slice(g * GROUP_WIDTH, (g + 1) * GROUP_WIDTH)
            yg = yv[:, lanes]
            rg = lax.rsqrt(jnp.mean(yg * yg, axis=-1, keepdims=True) + EPS)
            yn = yg * rg
            dnw.append(_colsum(dov[:, lanes] * yn))
            dyn = dov[:, lanes] * nw_ref[:, lanes]
            dyv.append(rg * (dyn - yn * jnp.mean(dyn * yn, axis=-1, keepdims=True)))
        dy = jnp.concatenate(dyv, axis=1)
        dyc = dy * gz
        dyc_ref[...] = dyc
        dxs_ref[...] = dyc * dk_ref[...]
        dz_ref[...] = dy * yc * (s * (1.0 + zv * (1.0 - s)))
        st_ref[0:1, :] += jnp.concatenate(dnw, axis=1)

    return pl.pallas_call(
        body, name=name, grid=(nt,), in_specs=[row, row, row, pl.BlockSpec((tl, D_SSD), lambda i: (i, O_Z // D_SSD)), vec, vec],
        out_specs=[row, row, row, pl.BlockSpec((8, D_SSD), lambda i: (0, 0))],
        out_shape=[jax.ShapeDtypeStruct((l, D_SSD), F32)] * 3 + [jax.ShapeDtypeStruct((8, D_SSD), F32)],
    )(dout, ycore, xc, z, dskip_x, norm_w)


PAIR = 2 * HEAD_DIM
N_PAIRS = N_HEADS // 2
Q_TILE0 = O_Q // PAIR
SB_KEYS = 1024
SB_PAIRS = 2
SB_SCALE = HEAD_DIM ** -0.5


def _pair_sum(v, lo):
    s_lo = jnp.sum(jnp.where(lo, v, 0.0), axis=-1, keepdims=True)
    s_hi = jnp.sum(jnp.where(lo, 0.0, v), axis=-1, keepdims=True)
    return jnp.where(lo, s_lo, s_hi)


def _qkv_prep(proj, qw2, kw2, name):
    l = proj.shape[0]
    tl = _pick(l, 1024, 16)
    out = pl.BlockSpec((tl, PAIR), lambda i, j: (i, j))
    vec = pl.BlockSpec((1, PAIR), lambda i, j: (0, 0))

    def at(which):
        return pl.BlockSpec((tl, PAIR), lambda i, j: (i, Q_TILE0 + which * N_PAIRS + j))

    def body(q_ref, k_ref, v_ref, qw_ref, kw_ref, qn_ref, kn_ref, vb_ref):
        lo = lax.broadcasted_iota(jnp.int32, (tl, PAIR), 1) < HEAD_DIM
        for t_ref, w_ref, o_ref, scale in ((q_ref, qw_ref, qn_ref, SB_SCALE), (k_ref, kw_ref, kn_ref, 1.0)):
            tv = t_ref[...]
            r = lax.rsqrt(_pair_sum(tv * tv, lo) * (1.0 / HEAD_DIM) + EPS)
            o_ref[...] = ((tv * r) * w_ref[...] * scale).astype(o_ref.dtype)
        vb_ref[...] = v_ref[...].astype(vb_ref.dtype)

    return pl.pallas_call(body, name=name, grid=(l // tl, N_PAIRS), in_specs=[at(0), at(1), at(2), vec, vec],
                          out_specs=[out, out, out], out_shape=[jax.ShapeDtypeStruct((l, D_SB), MXU_DTYPE)] * 3,
                          )(proj, proj, proj, qw2, kw2)


def _qk_norm_bwd(proj, dqn, dkn, qw2, kw2, name):
    l = proj.shape[0]
    tl = _pick(l, 1024, 16)
    out = pl.BlockSpec((tl, PAIR), lambda i, j: (i, j))
    vec = pl.BlockSpec((1, PAIR), lambda i, j: (0, 0))
    stats = pl.BlockSpec((8, PAIR), lambda i, j: (0, 0))

    def at(which):
        return pl.BlockSpec((tl, PAIR), lambda i, j: (i, Q_TILE0 + which * N_PAIRS + j))

    def body(q_ref, k_ref, dqn_ref, dkn_ref, qw_ref, kw_ref, dq_ref, dk_ref, stq_ref, stk_ref):
        @pl.when((pl.program_id(0) == 0) & (pl.program_id(1) == 0))
        def _():
            stq_ref[...] = jnp.zeros_like(stq_ref)
            stk_ref[...] = jnp.zeros_like(stk_ref)

        lo = lax.broadcasted_iota(jnp.int32, (tl, PAIR), 1) < HEAD_DIM
        for t_ref, dn_ref, w_ref, d_ref, st_ref, scale in ((q_ref, dqn_ref, qw_ref, dq_ref, stq_ref, SB_SCALE),
                                                           (k_ref, dkn_ref, kw_ref, dk_ref, stk_ref, 1.0)):
            tv = t_ref[...]
            r = lax.rsqrt(_pair_sum(tv * tv, lo) * (1.0 / HEAD_DIM) + EPS)
            tn = tv * r
            dnv = dn_ref[...] * scale
            dtn = dnv * w_ref[...]
            d_ref[...] = r * (dtn - tn * (_pair_sum(dtn * tn, lo) * (1.0 / HEAD_DIM)))
            st_ref[0:1, :] += _colsum(dnv * tn)

    return pl.pallas_call(body, name=name, grid=(l // tl, N_PAIRS), in_specs=[at(0), at(1), out, out, vec, vec],
                          out_specs=[out, out, stats, stats],
                          out_shape=[jax.ShapeDtypeStruct((l, D_SB), F32)] * 2 + [jax.ShapeDtypeStruct((8, PAIR), F32)] * 2,
                          )(proj, proj, dqn, dkn, qw2, kw2)


def _sb_masks():
    r = lax.broadcasted_iota(jnp.int32, (CHUNK, CHUNK), 0)
    c = lax.broadcasted_iota(jnp.int32, (CHUNK, CHUNK), 1)
    return r, c


def _stack2(mask):
    t = mask.astype(BF16)
    return jnp.concatenate([t, t], axis=0)


def _sb_fwd(q, k, v, shards, name):
    l = q.shape[0]
    nq = l // CHUNK
    kt = _pick(l, SB_KEYS, CHUNK)
    sub = kt // CHUNK
    heads = 2 * SB_PAIRS
    steps = N_PAIRS // SB_PAIRS
    na = len(shards)
    qblk = pl.BlockSpec((CHUNK, SB_PAIRS * PAIR), lambda i, j: (j, i))
    full = pl.BlockSpec((l, SB_PAIRS * PAIR), lambda i, j: (0, i))
    hbm = pl.BlockSpec(memory_space=pl.ANY)

    def body(q_ref, k_ref, v_ref, *rest):
        o_ref, tot_ref = rest[na:na + 2]
        exchange = _DirectExchange("gather", rest[:na], rest[na + 2:2 * na + 2], *rest[2 * na + 2:])
        qb = pl.program_id(1)

        @pl.when((pl.program_id(0) == 0) & (qb == 0))
        def _():
            exchange.start()

        r, c = _sb_masks()
        after2 = _stack2(r > c)
        halves = _half_masks()
        zero = jnp.zeros((CHUNK, PAIR), q_ref.dtype)
        lanes = [slice((a // 2) * PAIR, (a // 2 + 1) * PAIR) for a in range(heads)]
        qm = [jnp.where(halves[a % 2], q_ref[:, lanes[a]], zero) for a in range(heads)]
        last = (qb * CHUNK) // kt

        def tile(t, width, carries, accs, masked):
            rows = pl.ds(pl.multiple_of(t * width, width), width)
            if masked:
                key_minus_query = (lax.broadcasted_iota(jnp.int32, (CHUNK, width), 1)
                                   - lax.broadcasted_iota(jnp.int32, (CHUNK, width), 0))
                keep = key_minus_query < qb * CHUNK - t * width

            def logits(a):
                lg = _dot_nt(qm[a], k_ref[rows, lanes[a]])
                sp = _softplus(lg)
                return lg - sp, (jnp.where(keep, -sp, 0.0) if masked else -sp)

            def sums(a, lr):
                offset, parts = carries[a], [None] * (width // CHUNK)
                for j in reversed(range(width // CHUNK)):
                    piece = lr[:, j * CHUNK:(j + 1) * CHUNK]
                    parts[j] = _split_dot(piece, after2) + offset
                    offset = offset + jnp.sum(piece, axis=1, keepdims=True)
                return jnp.concatenate(parts, axis=1), offset

            def output(a, ls, cs):
                w = jnp.exp(ls + cs)
                if masked:
                    w = jnp.where(keep, w, 0.0)
                return accs[a] + _dot(w.astype(MXU_DTYPE), v_ref[rows, lanes[a]])

            new_carries, new_accs = [None] * heads, [None] * heads
            ls, lr = logits(0)
            for a in range(heads):
                cs, new_carries[a] = sums(a, lr)
                if a + 1 < heads:
                    ls_next, lr = logits(a + 1)
                new_accs[a] = output(a, ls, cs)
                ls = ls_next
            return tuple(new_carries), tuple(new_accs)

        carries = tuple(jnp.zeros((CHUNK, 1), F32) for _ in range(heads))
        accs = tuple(jnp.zeros((CHUNK, PAIR), F32) for _ in range(heads))
        if sub % 2 == 0:
            half = kt // 2
            carries, accs = lax.cond((qb * CHUNK) % kt >= half,
                                     lambda cr, ac: tile(2 * last + 1, half, cr, ac, True), lambda cr, ac: (cr, ac), carries, accs)
            carries, accs = tile(2 * last, half, carries, accs, True)
        else:
            carries, accs = tile(last, kt, carries, accs, True)
        carries, accs = lax.fori_loop(1, last + 1, lambda i, st: tile(last - i, kt, st[0], st[1], False), (carries, accs))
        for p in range(SB_PAIRS):
            o_ref[:, lanes[2 * p]] = jnp.where(halves[0], accs[2 * p], accs[2 * p + 1])
            tot_ref[:, lanes[2 * p]] = jnp.where(halves[0], carries[2 * p], carries[2 * p + 1])

        @pl.when((pl.program_id(0) == steps - 1) & (qb == nq - 1))
        def _():
            exchange.wait()

    return pl.pallas_call(
        body, name=name, grid=(steps, nq), in_specs=[qblk, full, full] + [hbm] * na,
        out_specs=[qblk, qblk] + [hbm] * na,
        out_shape=[jax.ShapeDtypeStruct((l, D_SB), F32), jax.ShapeDtypeStruct((l, D_SB), F32)]
        + [jax.ShapeDtypeStruct((N_DEV,) + s.shape, s.dtype) for s in shards],
        scratch_shapes=_DirectExchange.scratch(na),
    )(q, k, v, *shards)


def _sb_bwd(q, k, v, dycat, tot, blocks, name):
    l = q.shape[0]
    nq = l // CHUNK
    kt = _pick(l, SB_KEYS, CHUNK)
    sub = kt // CHUNK
    heads = 2 * SB_PAIRS
    steps = N_PAIRS // SB_PAIRS
    na = len(blocks)
    width_all = SB_PAIRS * PAIR
    qblk = pl.BlockSpec((CHUNK, width_all), lambda i, j: (j, i))
    doblk = pl.BlockSpec((CHUNK, width_all), lambda i, j: (j, D_SSD // width_all + i))
    full = pl.BlockSpec((l, width_all), lambda i, j: (0, i))
    hbm = pl.BlockSpec(memory_space=pl.ANY)

    def body(q_ref, k_ref, v_ref, do_ref, tot_ref, *rest):
        dq_ref, dk_ref, dv_ref = rest[na:na + 3]
        exchange = _DirectExchange("scatter", rest[:na], rest[na + 3:2 * na + 3], *rest[2 * na + 3:])
        qb = pl.program_id(1)

        @pl.when((pl.program_id(0) == 0) & (qb == 0))
        def _():
            exchange.start()

        @pl.when(qb == 0)
        def _():
            dk_ref[...] = jnp.zeros_like(dk_ref)
            dv_ref[...] = jnp.zeros_like(dv_ref)

        r, c = _sb_masks()
        upto2 = _stack2(r <= c)
        before2 = _stack2(r < c)
        halves = _half_masks()
        lanes = [slice((a // 2) * PAIR, (a // 2 + 1) * PAIR) for a in range(heads)]
        qm = [jnp.where(halves[a % 2], q_ref[:, lanes[a]], jnp.zeros((CHUNK, PAIR), q_ref.dtype)) for a in range(heads)]
        dom = [jnp.where(halves[a % 2], do_ref[:, lanes[a]], 0.0).astype(MXU_DTYPE) for a in range(heads)]
        total = [jnp.max(jnp.where(halves[a % 2], tot_ref[:, lanes[a]], -jnp.inf), axis=1, keepdims=True) for a in range(heads)]
        last = (qb * CHUNK) // kt

        def prefix(values, tri2, offset):
            parts = []
            for j in range(values.shape[1] // CHUNK):
                piece = values[:, j * CHUNK:(j + 1) * CHUNK]
                parts.append(_split_dot(piece, tri2) + offset)
                offset = offset + jnp.sum(piece, axis=1, keepdims=True)
            return jnp.concatenate(parts, axis=1), offset

        def tile(t, width, carry_p, carry_d, dq, masked):
            rows = pl.ds(pl.multiple_of(t * width, width), width)
            if masked:
                key_minus_query = (lax.broadcasted_iota(jnp.int32, (CHUNK, width), 1)
                                   - lax.broadcasted_iota(jnp.int32, (CHUNK, width), 0))
                keep = key_minus_query < qb * CHUNK - t * width

            def weights(a):
                lg = _dot_nt(qm[a], k_ref[rows, lanes[a]])
                sp = _softplus(lg)
                lr = jnp.where(keep, -sp, 0.0) if masked else -sp
                p_incl, p_next = prefix(lr, upto2, carry_p[a])
                ls = lg - sp
                w = jnp.exp(ls + (total[a] - p_incl))
                if masked:
                    w = jnp.where(keep, w, 0.0)
                return ls, w, p_next

            def gradients(a, ls, w):
                da = _dot_nt(dom[a], v_ref[rows, lanes[a]]) * w
                d_excl, d_next = prefix(da, before2, carry_d[a])
                sig = jnp.exp(ls)
                dl = da * (1.0 - sig) - d_excl * sig
                if masked:
                    dl = jnp.where(keep, dl, 0.0)
                dl_b = dl.astype(MXU_DTYPE)
                return (d_next, dq[a] + _dot(dl_b, k_ref[rows, lanes[a]]),
                        _dot_tn(w.astype(MXU_DTYPE), dom[a]), _dot_tn(dl_b, qm[a]))

            new_p, new_d, new_dq = [None] * heads, [None] * heads, [None] * heads
            dv_upd, dk_upd = [None] * heads, [None] * heads
            ls, w, new_p[0] = weights(0)
            for a in range(heads):
                if a + 1 < heads:
                    ls_next, w_next, new_p[a + 1] = weights(a + 1)
                new_d[a], new_dq[a], dv_upd[a], dk_upd[a] = gradients(a, ls, w)
                if a % 2 == 1:
                    dv_ref[rows, lanes[a]] += dv_upd[a - 1] + dv_upd[a]
                    dk_ref[rows, lanes[a]] += dk_upd[a - 1] + dk_upd[a]
                if a + 1 < heads:
                    ls, w = ls_next, w_next
            return tuple(new_p), tuple(new_d), tuple(new_dq)

        zeros = tuple(jnp.zeros((CHUNK, 1), F32) for _ in range(heads))
        dq0 = tuple(jnp.zeros((CHUNK, PAIR), F32) for _ in range(heads))
        state = lax.fori_loop(0, last, lambda t, st: tile(t, kt, st[0], st[1], st[2], False), (zeros, zeros, dq0))
        if sub % 2 == 0:
            half = kt // 2
            state = tile(2 * last, half, *state, True)
            state = lax.cond((qb * CHUNK) % kt >= half,
                             lambda cp, cd, dq: tile(2 * last + 1, half, cp, cd, dq, True), lambda cp, cd, dq: (cp, cd, dq), *state)
        else:
            state = tile(last, kt, *state, True)
        for p in range(SB_PAIRS):
            dq_ref[:, lanes[2 * p]] = jnp.where(halves[0], state[2][2 * p], state[2][2 * p + 1])

        @pl.when((pl.program_id(0) == steps - 1) & (qb == nq - 1))
        def _():
            exchange.wait()

    return pl.pallas_call(
        body, name=name, grid=(steps, nq),
        in_specs=[qblk, full, full, doblk, qblk] + [hbm] * na,
        out_specs=[qblk, full, full] + [hbm] * na,
        out_shape=[jax.ShapeDtypeStruct((l, D_SB), F32)] * 3 + [jax.ShapeDtypeStruct(b.shape, b.dtype) for b in blocks],
        scratch_shapes=_DirectExchange.scratch(na),
    )(q, k, v, dycat, tot, *blocks)


def _ada_fwd(c_all, w_shard, name):
    def body(c_ref, w_ref, o_ref):
        cv = c_ref[...]
        o_ref[...] = _dot(cv * _sigmoid(cv), w_ref[...], HIGHEST)

    return pl.pallas_call(body, name=name, out_shape=jax.ShapeDtypeStruct((c_all.shape[0], w_shard.shape[1]), F32))(c_all, w_shard)


def _ada_bwd(c_all, dmod_shard, name):
    def body(c_ref, d_ref, o_ref):
        cv = c_ref[...]
        o_ref[...] = _dot_tn(cv * _sigmoid(cv), d_ref[...], HIGHEST)

    return pl.pallas_call(body, name=name, out_shape=jax.ShapeDtypeStruct((c_all.shape[1], dmod_shard.shape[1]), F32))(c_all, dmod_shard)


def _sum_small(parts, name):
    def body(p_ref, o_ref):
        acc = p_ref[0]
        for d in range(1, N_DEV):
            acc = acc + p_ref[d]
        o_ref[...] = acc

    return pl.pallas_call(body, name=name, out_shape=jax.ShapeDtypeStruct(parts.shape[1:], F32))(parts)


def _adamw(w, g, m, v, name):
    rows, cols = w.shape
    tr = _pick(rows, 256, 8)
    spec = pl.BlockSpec((tr, cols), lambda i: (i, 0))
    bc1 = 1.0 - ADAM_B1 ** ADAM_STEP
    bc2 = 1.0 - ADAM_B2 ** ADAM_STEP

    def body(w_ref, g_ref, m_ref, v_ref, d_ref, nm_ref, nv_ref):
        gv = g_ref[...]
        nm = ADAM_B1 * m_ref[...] + (1.0 - ADAM_B1) * gv
        nv = ADAM_B2 * v_ref[...] + (1.0 - ADAM_B2) * (gv * gv)
        nm_ref[...] = nm
        nv_ref[...] = nv
        d_ref[...] = -ADAM_LR * ((nm / bc1) / (jnp.sqrt(nv / bc2) + ADAM_EPS) + ADAM_WD * w_ref[...])

    return pl.pallas_call(body, name=name, grid=(rows // tr,), in_specs=[spec] * 4, out_specs=[spec] * 3,
                          out_shape=[jax.ShapeDtypeStruct((rows, cols), F32)] * 3)(w, g, m, v)


def _pad_w_in_t(w_in_t):
    lo = D_SSD + D_CONV
    return jnp.concatenate([w_in_t[D_SSD:lo + N_HEADS], jnp.zeros((DT_PAD - N_HEADS, w_in_t.shape[1]), w_in_t.dtype),
                            w_in_t[:D_SSD], w_in_t[lo + N_HEADS:]], axis=0)


def _unpad_w_in_t(g):
    return jnp.concatenate([g[O_Z:O_Q], g[:O_DT + N_HEADS], g[O_Q:]], axis=0)


def _local_step(x, target, mod, norm1_w, w_in_tp, conv_w, conv_b, dt_bias, a_log, d_skip, ssd_norm_w, q_norm_w, k_norm_w,
                norm2_w, later_shards):
    sh1, sc1, g1, sh2, sc2, g2 = [mod[:, i * D_MODEL:(i + 1) * D_MODEL] for i in range(N_MOD)]
    qw2, kw2 = jnp.tile(q_norm_w, (1, 2)), jnp.tile(k_norm_w, (1, 2))
    dskip_x = jnp.repeat(d_skip, HEAD_DIM, axis=1)
    dtb_x, dtb_y = jnp.repeat(dt_bias, HEAD_DIM, axis=1), jnp.repeat(dt_bias, HEAD_TILE, axis=1)
    al_x, al_y = jnp.repeat(a_log, HEAD_DIM, axis=1), jnp.repeat(a_log, HEAD_TILE, axis=1)
    head_of_lanes = (jnp.arange(HEAD_TILE, dtype=jnp.int32)[:, None]
                     == jnp.arange(D_SSD, dtype=jnp.int32)[None, :] // HEAD_DIM).astype(BF16)

    h1 = _rms_mod(x, norm1_w, sc1, sh1, "rms_mod1")
    proj = _matmul_nt(h1, w_in_tp, "in_proj")
    dtr = proj[:, O_DT:O_DT + N_HEADS]
    dtr_x, dtr_y = jnp.repeat(dtr, HEAD_DIM, axis=1), jnp.repeat(dtr, HEAD_TILE, axis=1)

    xc = _conv_silu(proj, conv_w, conv_b, "conv_silu")
    ycore, states = _ssd_fwd(xc, dtr_x, dtr_y, dtb_x, dtb_y, al_x, al_y, "ssd_fwd")
    y_ssd = _ssd_gate_norm(ycore, xc, proj, dskip_x, ssd_norm_w, "ssd_gate_norm")

    qn, kn, vb = _qkv_prep(proj, qw2, kw2, "qkv_prep")
    o_sb, tot, *gathered = _sb_fwd(qn, kn, vb, later_shards, "sb_fwd")
    w_out, w_gate_t, w_up_t, w_down = [g.reshape(N_DEV * g.shape[1], D_MODEL) for g in gathered]
    ycat = jnp.concatenate([y_ssd, o_sb.astype(MXU_DTYPE)], axis=1)

    mix = _matmul(ycat, w_out, "out_proj")
    x1, h2 = _residual_rms_mod(x, mix, g1, norm2_w, sc2, sh2, "residual_rms_mod2")
    gate = _matmul_nt(h2, w_gate_t, "ffn_gate")
    up = _matmul_nt(h2, w_up_t, "ffn_up")
    act = _swiglu(gate, up, "swiglu")
    ffn = _matmul(act, w_down, "ffn_down")
    dy, loss_blk = _loss_head(x1, ffn, g2, target, "loss_head")

    dffn, st_g2 = _gate_bwd(dy, ffn, g2, "gate2_bwd")
    g_down = _matmul_tn(act, dffn, "ffn_down_dw", MXU_DTYPE)
    dact = _matmul_nt(dffn, w_down, "ffn_down_dx")
    dgate, dup = _swiglu_bwd(gate, up, dact, "swiglu_bwd")
    g_gate_t = _matmul_tn(dgate, h2, "ffn_gate_dw", MXU_DTYPE)
    g_up_t = _matmul_tn(dup, h2, "ffn_up_dw", MXU_DTYPE)
    dh2 = _matmul_sum2(dgate, w_gate_t, dup, w_up_t, "ffn_dh")
    dx1, st_n2 = _norm_bwd(x1, dh2, dy, norm2_w, sc2, "norm2_bwd")

    dmix, st_g1 = _gate_bwd(dx1, mix, g1, "gate1_bwd")
    g_out = _matmul_tn(ycat, dmix, "out_proj_dw", MXU_DTYPE)
    dycat = _matmul_nt(dmix, w_out, "out_proj_dx")

    partials = [g.reshape(N_DEV, g.shape[0] // N_DEV, D_MODEL).astype(BF16) for g in (g_out, g_gate_t, g_up_t, g_down)]
    dqn, dkn, dv, *slots = _sb_bwd(qn, kn, vb, dycat, tot, partials, "sb_bwd")
    dq, dk, st_q, st_k = _qk_norm_bwd(proj, dqn, dkn, qw2, kw2, "qk_norm_bwd")

    dycore, dxs_skip, dz, st_gn = _ssd_gate_norm_bwd(dycat, ycore, xc, proj, dskip_x, ssd_norm_w, "ssd_gate_norm_bwd")
    dxc, ddtr_x, st_ssd, st_heads = _ssd_bwd(xc, dtr_x, dtr_y, dtb_x, dtb_y, al_x, al_y, dycore, states, head_of_lanes, "ssd_bwd")
    ddtr = ddtr_x[:, ::HEAD_DIM]
    dxc = jnp.concatenate([dxc[:, :D_SSD] + dxs_skip, dxc[:, D_SSD:]], axis=1)
    dpre, st_conv = _conv_silu_bwd_pre(proj, dxc, conv_w, conv_b, "conv_silu_bwd")
    dxbc = _conv_bwd_input(dpre, conv_w, "conv_bwd_input")

    dproj = jnp.concatenate([dxbc, ddtr, jnp.zeros((x.shape[0], DT_PAD - N_HEADS), F32), dz, dq, dk, dv], axis=1).astype(MXU_DTYPE)
    g_in_t = _unpad_w_in_t(_matmul_tn(dproj, h1, "in_proj_dw", MXU_DTYPE))
    dh1 = _matmul(dproj, w_in_tp, "in_proj_dx")
    grad_x, st_n1 = _norm_bwd(x, dh1, dx1, norm1_w, sc1, "norm1_bwd")

    pad = jnp.zeros((1, SM_SSD_NORM - SM_D_SKIP - N_HEADS), F32)
    small = jnp.concatenate(
        [st_n1[1:2], st_n1[0:1], st_g1[0:1], st_n2[1:2], st_n2[0:1], st_g2[0:1],
         st_n1[2:3], st_conv[4:5], st_conv[0:4].reshape(1, CONV_WIDTH * D_CONV),
         st_ssd[1:2, ::HEAD_DIM], st_ssd[0:1, ::HEAD_DIM], st_heads[2:3, ::HEAD_DIM], pad,
         st_gn[0:1], st_q[0:1, :HEAD_DIM] + st_q[0:1, HEAD_DIM:], st_k[0:1, :HEAD_DIM] + st_k[0:1, HEAD_DIM:], st_n2[2:3]], axis=1)
    return loss_blk, grad_x, g_in_t, slots, small


def kernel(x, c, w_ada, b_ada, norm1_w, w_in, conv_w, conv_b, dt_bias, a_log, d_skip, ssd_norm_w, q_norm_w, k_norm_w, w_out, norm2_w, w_gate, w_up, w_down, loss_target, m_w_ada, m_b_ada, m_norm1_w, m_w_in, m_conv_w, m_conv_b, m_dt_bias, m_a_log, m_d_skip, m_ssd_norm_w, m_q_norm_w, m_k_norm_w, m_w_out, m_norm2_w, m_w_gate, m_w_up, m_w_down, v_w_ada, v_b_ada, v_norm1_w, v_w_in, v_conv_w, v_conv_b, v_dt_bias, v_a_log, v_d_skip, v_ssd_norm_w, v_q_norm_w, v_k_norm_w, v_w_out, v_norm2_w, v_w_gate, v_w_up, v_w_down):
    me = 4 * lax.axis_index("x") + 2 * lax.axis_index("y") + lax.axis_index("c")
    conv_cols = D_CONV // N_DEV
    ada_cols = N_MOD * D_MODEL // N_DEV

    (w_in_t,) = _all_gather_big([w_in[0].T.astype(BF16)], "gather_w_in")
    w_in_tp = _pad_w_in_t(w_in_t.reshape(D_IN_PROJ, D_MODEL))
    later_shards = [s.astype(BF16) for s in (w_out[0], w_gate[0].T, w_up[0].T, w_down[0])]

    first = _all_gather_small(jnp.concatenate([c, conv_w[0].reshape(1, CONV_WIDTH * conv_cols)], axis=1), "gather_cond")
    c_all = first[:, 0, :D_MODEL]
    conv_w_f = first[:, 0, D_MODEL:].reshape(N_DEV, CONV_WIDTH, conv_cols).transpose(1, 0, 2).reshape(CONV_WIDTH, D_CONV)
    mod_cols = _ada_fwd(c_all, w_ada[0], "ada_fwd")
    mods = _all_gather_small(mod_cols.reshape(1, N_DEV * ada_cols), "gather_mod")
    mod = lax.dynamic_index_in_dim(mods.reshape(N_DEV, N_DEV, ada_cols), me, axis=1, keepdims=False)
    mod = mod.reshape(1, N_MOD * D_MODEL) + b_ada

    loss_blk, grad_x, g_in_full, slots, small = _local_step(
        x[0], loss_target[0], mod, norm1_w, w_in_tp, conv_w_f, conv_b, dt_bias, a_log, d_skip, ssd_norm_w, q_norm_w, k_norm_w,
        norm2_w, later_shards)
    loss = lax.psum(loss_blk[0, 0], ("x", "y", "c"))

    slots = list(_all_to_all_big([g_in_full.reshape(N_DEV, D_IN_PROJ // N_DEV, D_MODEL).astype(BF16)], "scatter_g_in")) + list(slots)
    g_in_t, g_out, g_gate_t, g_up_t, g_down = [_sum_slots(s, "sum_grads_" + n) for s, n in zip(slots, ("in", "out", "gate", "up", "down"))]
    g_in, g_gate, g_up = g_in_t.T, g_gate_t.T, g_up_t.T

    parts = _all_gather_small(small, "gather_small")
    gsum = _sum_small(parts, "sum_small")
    dmod_shard = lax.dynamic_slice_in_dim(parts[:, 0, :N_MOD * D_MODEL], me * ada_cols, ada_cols, axis=1)
    g_ada = _ada_bwd(c_all, dmod_shard, "ada_bwd")
    g_conv_w = lax.dynamic_slice_in_dim(gsum[:, SM_CONV_W:SM_DT_BIAS].reshape(CONV_WIDTH, D_CONV), me * conv_cols, conv_cols, axis=1)

    def pack_small(b_ada_, norm1_, conv_b_, dt_bias_, a_log_, d_skip_, ssd_norm_, q_norm_, k_norm_, norm2_):
        return jnp.concatenate(
            [b_ada_, norm1_, conv_b_, jnp.zeros((1, CONV_WIDTH * D_CONV), F32), dt_bias_, a_log_, d_skip_,
             jnp.zeros((1, SM_SSD_NORM - SM_D_SKIP - N_HEADS), F32), ssd_norm_, q_norm_, k_norm_, norm2_], axis=1)

    def unpack_small(p):
        return {
            "b_ada": p[:, SM_B_ADA:SM_NORM1], "norm1_w": p[:, SM_NORM1:SM_CONV_B], "conv_b": p[:, SM_CONV_B:SM_CONV_W],
            "dt_bias": p[:, SM_DT_BIAS:SM_A_LOG], "a_log": p[:, SM_A_LOG:SM_D_SKIP], "d_skip": p[:, SM_D_SKIP:SM_D_SKIP + N_HEADS],
            "ssd_norm_w": p[:, SM_SSD_NORM:SM_Q_NORM], "q_norm_w": p[:, SM_Q_NORM:SM_K_NORM], "k_norm_w": p[:, SM_K_NORM:SM_NORM2],
            "norm2_w": p[:, SM_NORM2:SM_TOTAL]}

    w_small = pack_small(b_ada, norm1_w, conv_b, dt_bias, a_log, d_skip, ssd_norm_w, q_norm_w, k_norm_w, norm2_w)
    m_small = pack_small(m_b_ada, m_norm1_w, m_conv_b, m_dt_bias, m_a_log, m_d_skip, m_ssd_norm_w, m_q_norm_w, m_k_norm_w, m_norm2_w)
    v_small = pack_small(v_b_ada, v_norm1_w, v_conv_b, v_dt_bias, v_a_log, v_d_skip, v_ssd_norm_w, v_q_norm_w, v_k_norm_w, v_norm2_w)
    small_out = [unpack_small(t) for t in (gsum,) + tuple(_adamw(w_small, gsum, m_small, v_small, "adamw_small"))]

    sharded = {
        "w_ada": (w_ada[0], g_ada, m_w_ada[0], v_w_ada[0]),
        "w_in": (w_in[0], g_in, m_w_in[0], v_w_in[0]),
        "conv_w": (conv_w[0], g_conv_w, m_conv_w[0], v_conv_w[0]),
        "w_out": (w_out[0], g_out, m_w_out[0], v_w_out[0]),
        "w_gate": (w_gate[0], g_gate, m_w_gate[0], v_w_gate[0]),
        "w_up": (w_up[0], g_up, m_w_up[0], v_w_up[0]),
        "w_down": (w_down[0], g_down, m_w_down[0], v_w_down[0]),
    }
    sharded_out = {n: (t[1],) + tuple(_adamw(*t, "adamw_" + n)) for n, t in sharded.items()}

    names = ["w_ada", "b_ada", "norm1_w", "w_in", "conv_w", "conv_b", "dt_bias", "a_log", "d_skip", "ssd_norm_w", "q_norm_w",
             "k_norm_w", "w_out", "norm2_w", "w_gate", "w_up", "w_down"]
    outs = [loss, grad_x[None]]
    for kind in range(4):
        for n in names:
            outs.append(sharded_out[n][kind][None] if n in sharded_out else small_out[kind][n])
    return tuple(outs)
```

```python
import functools

import jax
import jax.numpy as jnp
from jax import lax
from jax.experimental import pallas as pl
from jax.experimental.pallas import tpu as pltpu

F32 = jnp.float32
BF16 = jnp.bfloat16
MXU_DTYPE = jnp.bfloat16
HIGHEST = lax.Precision.HIGHEST
MESH_IDS = pl.DeviceIdType.MESH

N_DEV = 8
D_MODEL = 1024
HEAD_DIM = 64
N_HEADS = 16
D_SSD = 1024
D_SB = 1024
SSD_GROUPS = 2
SSD_STATE = 128
GROUP_WIDTH = D_SSD // SSD_GROUPS
D_CONV = D_SSD + 2 * SSD_GROUPS * SSD_STATE
CONV_WIDTH = 4
CHUNK = 128
D_FF = 2816
N_MOD = 6
EPS = 1e-6
D_IN_PROJ = 5648
O_XBC = 0
O_DT = D_CONV
DT_PAD = 512
O_Z = O_DT + DT_PAD
O_Q = O_Z + D_SSD
D_IN_PAD = O_Q + 3 * D_SB
HALO = 8

ADAM_LR = 0.001
ADAM_B1 = 0.9
ADAM_B2 = 0.999
ADAM_EPS = 1e-08
ADAM_WD = 0.01
ADAM_STEP = 10

ROWS_W_IN = 706
ROWS_W_OUT = 256
ROWS_FF = 352
PACK_ROWS = 2048
SM_B_ADA = 0
SM_NORM1 = 6144
SM_CONV_B = 7168
SM_CONV_W = 8704
SM_DT_BIAS = 14848
SM_A_LOG = 14864
SM_D_SKIP = 14880
SM_SSD_NORM = 14976
SM_Q_NORM = 16000
SM_K_NORM = 16064
SM_NORM2 = 16128
SM_TOTAL = 17152


def _pick(n, cap, mult):
    if n <= cap:
        return n
    best = None
    for t in range(mult, cap + 1, mult):
        if n % t == 0:
            best = t
    assert best is not None, (n, cap, mult)
    return best


def _dot(a, b, precision=None):
    return jnp.dot(a, b, preferred_element_type=F32, precision=precision)


def _dot_nt(a, b, precision=None):
    return lax.dot_general(a, b, (((1,), (1,)), ((), ())), preferred_element_type=F32, precision=precision)


def _dot_tn(a, b, precision=None):
    return lax.dot_general(a, b, (((0,), (0,)), ((), ())), preferred_element_type=F32, precision=precision)


def _softplus(v):
    return jnp.maximum(v, 0.0) + jnp.log1p(jnp.exp(-jnp.abs(v)))


def _softplus_logits(v):
    return jnp.maximum(v, 0.0) + jnp.log(1.0 + jnp.exp(-jnp.abs(v)))


def _tri_dot(tri_b, v):
    hi = v.astype(BF16)
    r1 = v - hi.astype(F32)
    mid = r1.astype(BF16)
    lo = (r1 - mid.astype(F32)).astype(BF16)
    return _dot(tri_b, hi) + _dot(tri_b, mid) + _dot(tri_b, lo)


def _sigmoid(v):
    return jax.nn.sigmoid(v)


def _colsum(v):
    return jnp.sum(v, axis=0, keepdims=True)


def _split_dot(v, tri2):
    hi = v.astype(BF16)
    lo = (v - hi.astype(F32)).astype(BF16)
    return _dot(jnp.concatenate([hi, lo], axis=1), tri2)


def _position():
    x, y, c = lax.axis_index("x"), lax.axis_index("y"), lax.axis_index("c")
    return x, y, c


def _peer(x, y, c, k):
    px = 1 - x if (k >> 2) & 1 else x
    py = 1 - y if (k >> 1) & 1 else y
    pc = 1 - c if k & 1 else c
    return px, py, pc


def _all_gather_small(v, name):
    n = v.shape[1]

    def body(v_ref, out_ref, send_sems, recv_sems, local_sem):
        x, y, c = _position()
        me = 4 * x + 2 * y + c
        mine = pltpu.make_async_copy(v_ref, out_ref.at[me], local_sem)
        mine.start()
        sends = []
        for k in range(1, N_DEV):
            cp = pltpu.make_async_remote_copy(
                src_ref=v_ref, dst_ref=out_ref.at[me], send_sem=send_sems.at[k - 1], recv_sem=recv_sems.at[k - 1],
                device_id=_peer(x, y, c, k), device_id_type=MESH_IDS)
            cp.start()
            sends.append(cp)
        for k in range(1, N_DEV):
            px, py, pc = _peer(x, y, c, k)
            pltpu.make_async_remote_copy(
                src_ref=v_ref, dst_ref=out_ref.at[4 * px + 2 * py + pc], send_sem=send_sems.at[k - 1],
                recv_sem=recv_sems.at[k - 1], device_id=(px, py, pc), device_id_type=MESH_IDS).wait_recv()
        for cp in sends:
            cp.wait_send()
        mine.wait()

    return pl.pallas_call(
        body, name=name,
        out_shape=jax.ShapeDtypeStruct((N_DEV, 1, n), v.dtype),
        in_specs=[pl.BlockSpec(memory_space=pltpu.VMEM)],
        out_specs=pl.BlockSpec(memory_space=pltpu.VMEM),
        scratch_shapes=[pltpu.SemaphoreType.DMA((N_DEV - 1,)), pltpu.SemaphoreType.DMA((N_DEV - 1,)),
                        pltpu.SemaphoreType.DMA],
    )(v)


def _all_gather_big(blocks, name):
    na = len(blocks)
    copies = N_DEV - 1

    def body(*refs):
        b_refs, out_refs = refs[:na], refs[na:2 * na]
        send_sems, recv_sems, local_sems = refs[2 * na:]
        x, y, c = _position()
        me, sibling = (x, y, c), (x, y, 1 - c)
        chips = [(1 - x, y), (x, 1 - y), (1 - x, 1 - y)]

        def copy(a, k, blk, to, own=False):
            slot = out_refs[a].at[4 * blk[0] + 2 * blk[1] + blk[2]]
            return pltpu.make_async_remote_copy(
                src_ref=b_refs[a] if own else slot, dst_ref=slot,
                send_sem=send_sems.at[copies * a + k], recv_sem=recv_sems.at[copies * a + k], device_id=to, device_id_type=MESH_IDS)

        mine = [pltpu.make_async_copy(b_refs[a], out_refs[a].at[4 * x + 2 * y + c], local_sems.at[a]) for a in range(na)]
        for cp in mine:
            cp.start()
        first = [copy(a, 0, me, sibling, own=True) for a in range(na)]
        first += [copy(a, 1 + j, me, (*chip, c), own=True) for j, chip in enumerate(chips) for a in range(na)]
        for cp in first:
            cp.start()
        passed = []
        for j, chip in enumerate(chips):
            for a in range(na):
                copy(a, 1 + j, (*chip, c), me).wait_recv()
                passed.append(copy(a, 4 + j, (*chip, c), sibling))
                passed[-1].start()
        for a in range(na):
            copy(a, 0, sibling, me).wait_recv()
        for j, chip in enumerate(chips):
            for a in range(na):
                copy(a, 4 + j, (*chip, 1 - c), me).wait_recv()
        for cp in first + passed:
            cp.wait_send()
        for cp in mine:
            cp.wait()

    hbm = pl.BlockSpec(memory_space=pl.ANY)
    return pl.pallas_call(
        body, name=name,
        out_shape=[jax.ShapeDtypeStruct((N_DEV,) + b.shape, b.dtype) for b in blocks],
        in_specs=[hbm] * na, out_specs=[hbm] * na,
        scratch_shapes=[pltpu.SemaphoreType.DMA((copies * na,)), pltpu.SemaphoreType.DMA((copies * na,)),
                        pltpu.SemaphoreType.DMA((na,))],
    )(*blocks)


class _DirectExchange:
    def __init__(self, kind, src_refs, dst_refs, send_sems, recv_sems, local_sems):
        self.kind, self.src_refs, self.dst_refs = kind, src_refs, dst_refs
        self.send_sems, self.recv_sems, self.local_sems = send_sems, recv_sems, local_sems

    @staticmethod
    def scratch(n_arrays):
        copies = N_DEV - 1
        return [pltpu.SemaphoreType.DMA((copies * n_arrays,)), pltpu.SemaphoreType.DMA((copies * n_arrays,)),
                pltpu.SemaphoreType.DMA((n_arrays,))]

    def _copies(self):
        x, y, c = _position()
        me = 4 * x + 2 * y + c
        local, sends, arrivals = [], [], []
        for a, (src, dst) in enumerate(zip(self.src_refs, self.dst_refs)):
            own = src if self.kind == "gather" else src.at[me]
            local.append(pltpu.make_async_copy(own, dst.at[me], self.local_sems.at[a]))
            for k in range(1, N_DEV):
                px, py, pc = _peer(x, y, c, k)
                peer = 4 * px + 2 * py + pc
                sems = dict(send_sem=self.send_sems.at[(N_DEV - 1) * a + k - 1], recv_sem=self.recv_sems.at[(N_DEV - 1) * a + k - 1],
                            device_id=(px, py, pc), device_id_type=MESH_IDS)
                sends.append(pltpu.make_async_remote_copy(
                    src_ref=src if self.kind == "gather" else src.at[peer], dst_ref=dst.at[me], **sems))
                arrivals.append(pltpu.make_async_remote_copy(src_ref=own, dst_ref=dst.at[peer], **sems))
        return local, sends, arrivals

    def start(self):
        local, sends, _ = self._copies()
        for cp in local + sends:
            cp.start()

    def wait(self):
        local, sends, arrivals = self._copies()
        for cp in arrivals:
            cp.wait_recv()
        for cp in sends:
            cp.wait_send()
        for cp in local:
            cp.wait()


def _all_to_all_big(blocks, name):
    na = len(blocks)

    def body(*refs):
        exchange = _DirectExchange("scatter", refs[:na], refs[na:2 * na], *refs[2 * na:])
        exchange.start()
        exchange.wait()

    hbm = pl.BlockSpec(memory_space=pl.ANY)
    return pl.pallas_call(
        body, name=name,
        out_shape=[jax.ShapeDtypeStruct(b.shape, b.dtype) for b in blocks],
        in_specs=[hbm] * na, out_specs=[hbm] * na, scratch_shapes=_DirectExchange.scratch(na),
    )(*blocks)


def _sum_slots(slots, name):
    _, r, n = slots.shape
    tr = _pick(r, 256, 16) if r % 16 == 0 else r

    def body(s_ref, o_ref):
        acc = s_ref[0].astype(F32)
        for d in range(1, N_DEV):
            acc = acc + s_ref[d].astype(F32)
        o_ref[...] = acc

    return pl.pallas_call(
        body, name=name, grid=(r // tr,),
        in_specs=[pl.BlockSpec((N_DEV, tr, n), lambda i: (0, i, 0))],
        out_specs=pl.BlockSpec((tr, n), lambda i: (i, 0)),
        out_shape=jax.ShapeDtypeStruct((r, n), F32),
    )(slots)


def _matmul(a, b, name, out_dtype=F32):
    m, k = a.shape
    _, n = b.shape
    tm, tn, tk = _pick(m, 1024, 16), _pick(n, 1408, 128), _pick(k, 1408, 128)
    nk = k // tk

    def body(a_ref, b_ref, o_ref, acc_ref):
        kk = pl.program_id(2)

        @pl.when(kk == 0)
        def _():
            acc_ref[...] = jnp.zeros_like(acc_ref)

        acc_ref[...] += _dot(a_ref[...].astype(MXU_DTYPE), b_ref[...].astype(MXU_DTYPE))

        @pl.when(kk == nk - 1)
        def _():
            o_ref[...] = acc_ref[...].astype(o_ref.dtype)

    return pl.pallas_call(
        body, name=name, grid=(m // tm, n // tn, nk),
        in_specs=[pl.BlockSpec((tm, tk), lambda i, j, kk: (i, kk)), pl.BlockSpec((tk, tn), lambda i, j, kk: (kk, j))],
        out_specs=pl.BlockSpec((tm, tn), lambda i, j, kk: (i, j)),
        out_shape=jax.ShapeDtypeStruct((m, n), out_dtype),
        scratch_shapes=[pltpu.VMEM((tm, tn), F32)],
    )(a, b)


def _matmul_beside_scatter(a, b, blocks, name):
    m, k = a.shape
    _, n = b.shape
    tm, tn, tk = _pick(m, 1024, 16), _pick(n, 1408, 128), _pick(k, 1408, 128)
    gm, gn, nk = m // tm, n // tn, k // tk
    na = len(blocks)
    hbm = pl.BlockSpec(memory_space=pl.ANY)

    def body(a_ref, b_ref, *rest):
        o_ref = rest[na]
        exchange = _DirectExchange("scatter", rest[:na], rest[na + 1:2 * na + 1], *rest[2 * na + 2:])
        acc_ref = rest[2 * na + 1]
        i, j, kk = pl.program_id(0), pl.program_id(1), pl.program_id(2)

        @pl.when((i == 0) & (j == 0) & (kk == 0))
        def _():
            exchange.start()

        @pl.when(kk == 0)
        def _():
            acc_ref[...] = jnp.zeros_like(acc_ref)

        acc_ref[...] += _dot(a_ref[...].astype(MXU_DTYPE), b_ref[...].astype(MXU_DTYPE))

        @pl.when(kk == nk - 1)
        def _():
            o_ref[...] = acc_ref[...]

        @pl.when((i == gm - 1) & (j == gn - 1) & (kk == nk - 1))
        def _():
            exchange.wait()

    return pl.pallas_call(
        body, name=name, grid=(gm, gn, nk),
        in_specs=[pl.BlockSpec((tm, tk), lambda i, j, kk: (i, kk)), pl.BlockSpec((tk, tn), lambda i, j, kk: (kk, j))] + [hbm] * na,
        out_specs=[pl.BlockSpec((tm, tn), lambda i, j, kk: (i, j))] + [hbm] * na,
        out_shape=[jax.ShapeDtypeStruct((m, n), F32)] + [jax.ShapeDtypeStruct(blk.shape, blk.dtype) for blk in blocks],
        scratch_shapes=[pltpu.VMEM((tm, tn), F32)] + _DirectExchange.scratch(na),
    )(a, b, *blocks)


def _matmul_nt(a, bt, name, out_dtype=F32):
    m, k = a.shape
    n, _ = bt.shape
    tm, tn, tk = _pick(m, 1024, 16), _pick(n, 1408, 128), _pick(k, 1408, 128)
    nk = k // tk

    def body(a_ref, b_ref, o_ref, acc_ref):
        kk = pl.program_id(2)

        @pl.when(kk == 0)
        def _():
            acc_ref[...] = jnp.zeros_like(acc_ref)

        acc_ref[...] += _dot_nt(a_ref[...].astype(MXU_DTYPE), b_ref[...].astype(MXU_DTYPE))

        @pl.when(kk == nk - 1)
        def _():
            o_ref[...] = acc_ref[...].astype(o_ref.dtype)

    return pl.pallas_call(
        body, name=name, grid=(m // tm, n // tn, nk),
        in_specs=[pl.BlockSpec((tm, tk), lambda i, j, kk: (i, kk)), pl.BlockSpec((tn, tk), lambda i, j, kk: (j, kk))],
        out_specs=pl.BlockSpec((tm, tn), lambda i, j, kk: (i, j)),
        out_shape=jax.ShapeDtypeStruct((m, n), out_dtype),
        scratch_shapes=[pltpu.VMEM((tm, tn), F32)],
    )(a, bt)


def _matmul_sum2(a1, b1, a2, b2, name):
    m, k = a1.shape
    _, n = b1.shape
    tm, tn, tk = _pick(m, 1024, 16), _pick(n, 1408, 128), _pick(k, 1408, 128)
    nk = k // tk

    def body(a1_ref, b1_ref, a2_ref, b2_ref, o_ref):
        @pl.when(pl.program_id(2) == 0)
        def _():
            o_ref[...] = jnp.zeros_like(o_ref)

        o_ref[...] += (_dot(a1_ref[...].astype(MXU_DTYPE), b1_ref[...].astype(MXU_DTYPE))
                       + _dot(a2_ref[...].astype(MXU_DTYPE), b2_ref[...].astype(MXU_DTYPE)))

    a_spec = pl.BlockSpec((tm, tk), lambda i, j, kk: (i, kk))
    b_spec = pl.BlockSpec((tk, tn), lambda i, j, kk: (kk, j))
    return pl.pallas_call(
        body, name=name, grid=(m // tm, n // tn, nk), in_specs=[a_spec, b_spec, a_spec, b_spec],
        out_specs=pl.BlockSpec((tm, tn), lambda i, j, kk: (i, j)),
        out_shape=jax.ShapeDtypeStruct((m, n), F32),
    )(a1, b1, a2, b2)


def _matmul_tn(a, b, name, out_dtype=F32):
    l, m = a.shape
    _, n = b.shape
    tm, tn, tl = _pick(m, 1408, 128), _pick(n, 1408, 128), _pick(l, 512, 16)
    nl = l // tl

    def body(a_ref, b_ref, o_ref, acc_ref):
        ll = pl.program_id(2)

        @pl.when(ll == 0)
        def _():
            acc_ref[...] = jnp.zeros_like(acc_ref)

        acc_ref[...] += _dot_tn(a_ref[...].astype(MXU_DTYPE), b_ref[...].astype(MXU_DTYPE))

        @pl.when(ll == nl - 1)
        def _():
            o_ref[...] = acc_ref[...].astype(o_ref.dtype)

    return pl.pallas_call(
        body, name=name, grid=(m // tm, n // tn, nl),
        in_specs=[pl.BlockSpec((tl, tm), lambda i, j, ll: (ll, i)), pl.BlockSpec((tl, tn), lambda i, j, ll: (ll, j))],
        out_specs=pl.BlockSpec((tm, tn), lambda i, j, ll: (i, j)),
        out_shape=jax.ShapeDtypeStruct((m, n), out_dtype),
        scratch_shapes=[pltpu.VMEM((tm, tn), F32)],
    )(a, b)


def _row_specs(l, d, n_rows, n_vecs):
    tl = _pick(l, 512, 16)
    row = pl.BlockSpec((tl, d), lambda i: (i, 0))
    vec = pl.BlockSpec((1, d), lambda i: (0, 0))
    return tl, row, vec, [row] * n_rows + [vec] * n_vecs


def _rms_mod(x, nw, sc, sh, name):
    l, d = x.shape
    tl, row, _, in_specs = _row_specs(l, d, 1, 3)

    def body(x_ref, nw_ref, sc_ref, sh_ref, h_ref):
        xv = x_ref[...]
        r = lax.rsqrt(jnp.mean(xv * xv, axis=-1, keepdims=True) + EPS)
        h_ref[...] = (((xv * r) * nw_ref[...]) * (1.0 + sc_ref[...]) + sh_ref[...]).astype(h_ref.dtype)

    return pl.pallas_call(body, name=name, grid=(l // tl,), in_specs=in_specs, out_specs=row,
                          out_shape=jax.ShapeDtypeStruct((l, d), MXU_DTYPE))(x, nw, sc, sh)


def _residual_rms_mod(x, mix, g, nw, sc, sh, name):
    l, d = x.shape
    tl, row, _, in_specs = _row_specs(l, d, 2, 4)

    def body(x_ref, mix_ref, g_ref, nw_ref, sc_ref, sh_ref, x1_ref, h_ref):
        xv = x_ref[...] + g_ref[...] * mix_ref[...]
        x1_ref[...] = xv
        r = lax.rsqrt(jnp.mean(xv * xv, axis=-1, keepdims=True) + EPS)
        h_ref[...] = (((xv * r) * nw_ref[...]) * (1.0 + sc_ref[...]) + sh_ref[...]).astype(h_ref.dtype)

    return pl.pallas_call(body, name=name, grid=(l // tl,), in_specs=in_specs, out_specs=[row, row],
                          out_shape=[jax.ShapeDtypeStruct((l, d), F32), jax.ShapeDtypeStruct((l, d), MXU_DTYPE)],
                          )(x, mix, g, nw, sc, sh)


def _norm_bwd(x, dh, dres, nw, sc, name):
    l, d = x.shape
    tl, row, _, in_specs = _row_specs(l, d, 3, 2)

    def body(x_ref, dh_ref, dres_ref, nw_ref, sc_ref, dx_ref, st_ref):
        xv, dh_v = x_ref[...], dh_ref[...]
        r = lax.rsqrt(jnp.mean(xv * xv, axis=-1, keepdims=True) + EPS)
        xn = xv * r
        dxn = dh_v * (nw_ref[...] * (1.0 + sc_ref[...]))
        dx_ref[...] = dres_ref[...] + r * (dxn - xn * jnp.mean(dxn * xn, axis=-1, keepdims=True))

        @pl.when(pl.program_id(0) == 0)
        def _():
            st_ref[...] = jnp.zeros_like(st_ref)

        dhx = dh_v * xn
        st_ref[0:1, :] += _colsum(dhx * nw_ref[...])
        st_ref[1:2, :] += _colsum(dh_v)
        st_ref[2:3, :] += _colsum(dhx * (1.0 + sc_ref[...]))

    return pl.pallas_call(body, name=name, grid=(l // tl,), in_specs=in_specs,
                          out_specs=[row, pl.BlockSpec((8, d), lambda i: (0, 0))],
                          out_shape=[jax.ShapeDtypeStruct((l, d), F32), jax.ShapeDtypeStruct((8, d), F32)],
                          )(x, dh, dres, nw, sc)


def _gate_bwd(dres, val, g, name):
    l, d = dres.shape
    tl, row, _, in_specs = _row_specs(l, d, 2, 1)

    def body(dres_ref, val_ref, g_ref, dval_ref, st_ref):
        dr = dres_ref[...]
        dval_ref[...] = (g_ref[...] * dr).astype(dval_ref.dtype)

        @pl.when(pl.program_id(0) == 0)
        def _():
            st_ref[...] = jnp.zeros_like(st_ref)

        st_ref[0:1, :] += _colsum(dr * val_ref[...])

    return pl.pallas_call(body, name=name, grid=(l // tl,), in_specs=in_specs,
                          out_specs=[row, pl.BlockSpec((8, d), lambda i: (0, 0))],
                          out_shape=[jax.ShapeDtypeStruct((l, d), MXU_DTYPE), jax.ShapeDtypeStruct((8, d), F32)],
                          )(dres, val, g)


def _loss_head(x1, f, g, target, name):
    l, d = x1.shape
    tl, row, _, _ = _row_specs(l, d, 0, 0)
    vec = pl.BlockSpec((1, d), lambda i: (0, 0))

    def body(x1_ref, f_ref, g_ref, t_ref, dy_ref, loss_ref):
        e = x1_ref[...] + g_ref[...] * f_ref[...] - t_ref[...]
        dy_ref[...] = e * (1.0 / d)

        @pl.when(pl.program_id(0) == 0)
        def _():
            loss_ref[...] = jnp.zeros_like(loss_ref)

        s = jnp.sum(jnp.sum(e * e, axis=1, keepdims=True), axis=0, keepdims=True)
        loss_ref[...] += (0.5 / d) * s

    return pl.pallas_call(body, name=name, grid=(l // tl,), in_specs=[row, row, vec, row],
                          out_specs=[row, pl.BlockSpec((8, 128), lambda i: (0, 0))],
                          out_shape=[jax.ShapeDtypeStruct((l, d), F32), jax.ShapeDtypeStruct((8, 128), F32)],
                          )(x1, f, g, target)


def _swiglu(gate, up, name):
    l, f = gate.shape
    tl, tf = _pick(l, 512, 16), _pick(f, 1408, 128)
    spec = pl.BlockSpec((tl, tf), lambda i, j: (i, j))

    def body(g_ref, u_ref, a_ref):
        gv = g_ref[...]
        a_ref[...] = (gv * _sigmoid(gv) * u_ref[...]).astype(a_ref.dtype)

    return pl.pallas_call(body, name=name, grid=(l // tl, f // tf), in_specs=[spec, spec], out_specs=spec,
                          out_shape=jax.ShapeDtypeStruct((l, f), MXU_DTYPE))(gate, up)


def _swiglu_bwd(gate, up, da, name):
    l, f = gate.shape
    tl, tf = _pick(l, 512, 16), _pick(f, 1408, 128)
    spec = pl.BlockSpec((tl, tf), lambda i, j: (i, j))

    def body(g_ref, u_ref, da_ref, dg_ref, du_ref):
        gv, dav = g_ref[...], da_ref[...]
        s = _sigmoid(gv)
        dg_ref[...] = (dav * u_ref[...] * (s * (1.0 + gv * (1.0 - s)))).astype(dg_ref.dtype)
        du_ref[...] = (dav * (gv * s)).astype(du_ref.dtype)

    return pl.pallas_call(body, name=name, grid=(l // tl, f // tf), in_specs=[spec, spec, spec], out_specs=[spec, spec],
                          out_shape=[jax.ShapeDtypeStruct((l, f), MXU_DTYPE)] * 2)(gate, up, da)


def _conv_tile(l):
    return _pick(l, 512, 16)


def _conv_pre(buf, w_ref, b_ref, tl):
    acc = b_ref[...] + w_ref[3:4, :] * buf[HALO:HALO + tl, :]
    for k in range(CONV_WIDTH - 1):
        s = HALO - (CONV_WIDTH - 1) + k
        acc = acc + w_ref[k:k + 1, :] * buf[s:s + tl, :]
    return acc


def _fill_past(buf, u_ref, halo_ref, tl):
    i = pl.program_id(0)

    @pl.when(i == 0)
    def _():
        buf[0:HALO, :] = jnp.zeros((HALO, buf.shape[1]), F32)

    @pl.when(i > 0)
    def _():
        buf[0:HALO, :] = halo_ref[...]

    buf[HALO:HALO + tl, :] = u_ref[...]


def _conv_silu(u, w, b, name):
    l, ch = u.shape[0], w.shape[1]
    tl = _conv_tile(l)
    per = tl // HALO
    cur = pl.BlockSpec((tl, ch), lambda i: (i, 0))
    past = pl.BlockSpec((HALO, ch), lambda i: (jnp.maximum(i * per - 1, 0), 0))

    def body(u_ref, halo_ref, w_ref, b_ref, o_ref, buf):
        _fill_past(buf, u_ref, halo_ref, tl)
        pre = _conv_pre(buf, w_ref, b_ref, tl)
        o_ref[...] = pre * _sigmoid(pre)

    return pl.pallas_call(body, name=name, grid=(l // tl,),
                          in_specs=[cur, past, pl.BlockSpec((CONV_WIDTH, ch), lambda i: (0, 0)), pl.BlockSpec((1, ch), lambda i: (0, 0))],
                          out_specs=cur, out_shape=jax.ShapeDtypeStruct((l, ch), F32),
                          scratch_shapes=[pltpu.VMEM((tl + HALO, ch), F32)])(u, u, w, b)


def _conv_silu_bwd_pre(u, dxc, w, b, name):
    l, ch = u.shape[0], w.shape[1]
    tl = _conv_tile(l)
    per = tl // HALO
    cur = pl.BlockSpec((tl, ch), lambda i: (i, 0))
    past = pl.BlockSpec((HALO, ch), lambda i: (jnp.maximum(i * per - 1, 0), 0))

    def body(u_ref, halo_ref, d_ref, w_ref, b_ref, dpre_ref, st_ref, buf):
        _fill_past(buf, u_ref, halo_ref, tl)
        pre = _conv_pre(buf, w_ref, b_ref, tl)
        s = _sigmoid(pre)
        dpre = d_ref[...] * (s * (1.0 + pre * (1.0 - s)))
        dpre_ref[...] = dpre

        @pl.when(pl.program_id(0) == 0)
        def _():
            st_ref[...] = jnp.zeros_like(st_ref)

        for k in range(CONV_WIDTH):
            s0 = HALO - (CONV_WIDTH - 1) + k
            st_ref[k:k + 1, :] += _colsum(dpre * buf[s0:s0 + tl, :])
        st_ref[CONV_WIDTH:CONV_WIDTH + 1, :] += _colsum(dpre)

    return pl.pallas_call(body, name=name, grid=(l // tl,),
                          in_specs=[cur, past, cur, pl.BlockSpec((CONV_WIDTH, ch), lambda i: (0, 0)), pl.BlockSpec((1, ch), lambda i: (0, 0))],
                          out_specs=[cur, pl.BlockSpec((8, ch), lambda i: (0, 0))],
                          out_shape=[jax.ShapeDtypeStruct((l, ch), F32), jax.ShapeDtypeStruct((8, ch), F32)],
                          scratch_shapes=[pltpu.VMEM((tl + HALO, ch), F32)])(u, u, dxc, w, b)


def _conv_bwd_input(dpre, w, name):
    l, ch = dpre.shape
    tl = _conv_tile(l)
    per = tl // HALO
    nt = l // tl
    cur = pl.BlockSpec((tl, ch), lambda i: (i, 0))
    nxt = pl.BlockSpec((HALO, ch), lambda i: (jnp.minimum((i + 1) * per, l // HALO - 1), 0))

    def body(d_ref, halo_ref, w_ref, du_ref, buf):
        i = pl.program_id(0)
        buf[0:tl, :] = d_ref[...]

        @pl.when(i == nt - 1)
        def _():
            buf[tl:tl + HALO, :] = jnp.zeros((HALO, ch), F32)

        @pl.when(i < nt - 1)
        def _():
            buf[tl:tl + HALO, :] = halo_ref[...]

        acc = w_ref[3:4, :] * buf[0:tl, :]
        for k in range(CONV_WIDTH - 1):
            s = CONV_WIDTH - 1 - k
            acc = acc + w_ref[k:k + 1, :] * buf[s:s + tl, :]
        du_ref[...] = acc

    return pl.pallas_call(body, name=name, grid=(nt,),
                          in_specs=[cur, nxt, pl.BlockSpec((CONV_WIDTH, ch), lambda i: (0, 0))],
                          out_specs=cur, out_shape=jax.ShapeDtypeStruct((l, ch), F32),
                          scratch_shapes=[pltpu.VMEM((tl + HALO, ch), F32)])(dpre, dpre, w)


HEAD_TILE = 128
D_HEAD_TILES = N_HEADS * HEAD_TILE


def _chunk_iota():
    r = lax.broadcasted_iota(jnp.int32, (CHUNK, CHUNK), 0)
    c = lax.broadcasted_iota(jnp.int32, (CHUNK, CHUNK), 1)
    return r, c


def _split3_dot(v, ones_b):
    hi = v.astype(BF16)
    r1 = v - hi.astype(F32)
    mid = r1.astype(BF16)
    lo = (r1 - mid.astype(F32)).astype(BF16)
    return _dot(hi, ones_b) + _dot(mid, ones_b) + _dot(lo, ones_b)


def _split3_dot_nt(v, ones_b):
    hi = v.astype(BF16)
    r1 = v - hi.astype(F32)
    mid = r1.astype(BF16)
    lo = (r1 - mid.astype(F32)).astype(BF16)
    return _dot_nt(hi, ones_b) + _dot_nt(mid, ones_b) + _dot_nt(lo, ones_b)


def _ssd_decays(dtx_ref, dty_ref, dtbx_ref, dtby_ref, alx_ref, aly_ref, r, c):
    tri = (r >= c).astype(BF16)
    dt_x = _softplus(dtx_ref[...] + dtbx_ref[...])
    a_x = -jnp.exp(alx_ref[...])
    adt_x = a_x * dt_x
    acs_x = _tri_dot(tri, adt_x)
    atot_x = _colsum(adt_x)
    acs_y = _tri_dot(tri, -jnp.exp(aly_ref[...]) * _softplus(dty_ref[...] + dtby_ref[...]))
    return dt_x, a_x, acs_y, jnp.exp(acs_x), jnp.exp(atot_x - acs_x), jnp.exp(atot_x)


def _head_decay(acs_y, e, r, c):
    col = acs_y[:, e * HEAD_TILE:(e + 1) * HEAD_TILE]
    return jnp.where(r >= c, jnp.exp(col - col.T), 0.0)


def _half_masks():
    lane = lax.broadcasted_iota(jnp.int32, (CHUNK, 2 * HEAD_DIM), 1)
    return lane < HEAD_DIM, lane >= HEAD_DIM


def _ssd_specs(nc, reverse):
    def at(i):
        return nc - 1 - i if reverse else i
    xc = pl.BlockSpec((CHUNK, D_CONV), lambda i: (at(i), 0))
    wide = pl.BlockSpec((CHUNK, D_SSD), lambda i: (at(i), 0))
    wide_y = pl.BlockSpec((CHUNK, D_HEAD_TILES), lambda i: (at(i), 0))
    vec = pl.BlockSpec((1, D_SSD), lambda i: (0, 0))
    vec_y = pl.BlockSpec((1, D_HEAD_TILES), lambda i: (0, 0))
    state = pl.BlockSpec((1, SSD_STATE, D_SSD), lambda i: (at(i), 0, 0))
    return xc, wide, wide_y, vec, vec_y, state


def _ssd_fwd(xc, dtr_x, dtr_y, dtb_x, dtb_y, al_x, al_y, name):
    l = xc.shape[0]
    nc = l // CHUNK
    xc_s, wide_s, wide_y_s, vec_s, vec_y_s, state_s = _ssd_specs(nc, False)
    pairs_per_group = GROUP_WIDTH // (2 * HEAD_DIM)

    def body(xc_ref, dtx_ref, dty_ref, dtbx_ref, dtby_ref, alx_ref, aly_ref, y_ref, sp_ref, state):
        @pl.when(pl.program_id(0) == 0)
        def _():
            state[...] = jnp.zeros_like(state)

        r, c = _chunk_iota()
        halves = _half_masks()
        dt_x, _, acs_y, ea_x, ds_x, eatot_x = _ssd_decays(dtx_ref, dty_ref, dtbx_ref, dtby_ref, alx_ref, aly_ref, r, c)
        xg = xc_ref[:, 0:D_SSD] * dt_x
        sp_ref[0] = state[...]
        for g in range(SSD_GROUPS):
            lanes = slice(g * GROUP_WIDTH, (g + 1) * GROUP_WIDTH)
            bb = xc_ref[:, D_SSD + g * SSD_STATE:D_SSD + (g + 1) * SSD_STATE].astype(MXU_DTYPE)
            cb = xc_ref[:, D_SSD + (SSD_GROUPS + g) * SSD_STATE:D_SSD + (SSD_GROUPS + g + 1) * SSD_STATE].astype(MXU_DTYPE)
            scores = _dot_nt(cb, bb)
            sg = state[:, lanes]
            ys = []
            for j in range(g * pairs_per_group, (g + 1) * pairs_per_group):
                xg_pair = xg[:, j * 2 * HEAD_DIM:(j + 1) * 2 * HEAD_DIM]
                acc = jnp.zeros((CHUNK, 2 * HEAD_DIM), F32)
                for half in range(2):
                    m = (scores * _head_decay(acs_y, 2 * j + half, r, c)).astype(MXU_DTYPE)
                    acc = acc + _dot(m, jnp.where(halves[half], xg_pair, 0.0).astype(MXU_DTYPE))
                ys.append(acc)
            y_ref[:, lanes] = jnp.concatenate(ys, axis=1) + _dot(cb, sg.astype(MXU_DTYPE)) * ea_x[:, lanes]
            state[:, lanes] = sg * eatot_x[:, lanes] + _dot_tn(bb, (xg[:, lanes] * ds_x[:, lanes]).astype(MXU_DTYPE))

    return pl.pallas_call(
        body, name=name, grid=(nc,),
        in_specs=[xc_s, wide_s, wide_y_s, vec_s, vec_y_s, vec_s, vec_y_s],
        out_specs=[wide_s, state_s],
        out_shape=[jax.ShapeDtypeStruct((l, D_SSD), F32), jax.ShapeDtypeStruct((nc, SSD_STATE, D_SSD), F32)],
        scratch_shapes=[pltpu.VMEM((SSD_STATE, D_SSD), F32)],
    )(xc, dtr_x, dtr_y, dtb_x, dtb_y, al_x, al_y)


def _ssd_bwd(xc, dtr_x, dtr_y, dtb_x, dtb_y, al_x, al_y, dy, states, head_of_lanes, name):
    l = xc.shape[0]
    nc = l // CHUNK
    xc_s, wide_s, wide_y_s, vec_s, vec_y_s, state_s = _ssd_specs(nc, True)
    pairs_per_group = GROUP_WIDTH // (2 * HEAD_DIM)

    def body(xc_ref, dtx_ref, dty_ref, dtbx_ref, dtby_ref, alx_ref, aly_ref, dy_ref, sp_ref, ones_ref,
             dxc_ref, ddtr_ref, st_ref, hs_ref, dstate):
        @pl.when(pl.program_id(0) == 0)
        def _():
            dstate[...] = jnp.zeros_like(dstate)
            st_ref[...] = jnp.zeros_like(st_ref)

        r, c = _chunk_iota()
        halves = _half_masks()
        spread = ones_ref[...]

        def head_sums(v):
            return _split3_dot(_split3_dot_nt(v, spread), spread)

        dt_x, a_x, acs_y, ea_x, ds_x, eatot_x = _ssd_decays(dtx_ref, dty_ref, dtbx_ref, dtby_ref, alx_ref, aly_ref, r, c)
        xs = xc_ref[:, 0:D_SSD]
        xg = xs * dt_x
        gy = dy_ref[...]
        s_prev = sp_ref[0]
        gea = gy * ea_x
        xds = xg * ds_x
        ds_old = dstate[...]
        later2 = jnp.concatenate([(r <= c).astype(BF16)] * 2, axis=1)
        dxg_parts, state_term, yoff_parts, dadt_parts = [], [], [], []
        for g in range(SSD_GROUPS):
            lanes = slice(g * GROUP_WIDTH, (g + 1) * GROUP_WIDTH)
            b_lo = D_SSD + g * SSD_STATE
            c_lo = D_SSD + (SSD_GROUPS + g) * SSD_STATE
            bb = xc_ref[:, b_lo:b_lo + SSD_STATE].astype(MXU_DTYPE)
            cb = xc_ref[:, c_lo:c_lo + SSD_STATE].astype(MXU_DTYPE)
            scores = _dot_nt(cb, bb)
            dsg = ds_old[:, lanes].astype(MXU_DTYPE)
            gea_b = gea[:, lanes].astype(MXU_DTYPE)
            xds_b = xds[:, lanes].astype(MXU_DTYPE)
            dxg_state = _dot(bb, dsg) * ds_x[:, lanes]
            dc = _dot_nt(gea_b, s_prev[:, lanes].astype(MXU_DTYPE))
            db = _dot_nt(xds_b, dsg)
            dscores = jnp.zeros((CHUNK, CHUNK), F32)
            diag = []
            for j in range(g * pairs_per_group, (g + 1) * pairs_per_group):
                pair = slice(j * 2 * HEAD_DIM, (j + 1) * 2 * HEAD_DIM)
                xg_pair = xg[:, pair].astype(MXU_DTYPE)
                acc = jnp.zeros((CHUNK, 2 * HEAD_DIM), F32)
                cols = []
                for half in range(2):
                    decay = _head_decay(acs_y, 2 * j + half, r, c)
                    g_e = jnp.where(halves[half], gy[:, pair], 0.0).astype(MXU_DTYPE)
                    acc = acc + _dot_tn((scores * decay).astype(MXU_DTYPE), g_e)
                    dm = _dot_nt(g_e, xg_pair) * decay
                    dscores = dscores + dm
                    wq = dm * scores
                    hi = wq.astype(BF16)
                    lo = (wq - hi.astype(F32)).astype(BF16)
                    later = _dot(later2, jnp.concatenate([hi, lo], axis=0))
                    cols.append(jnp.sum(jnp.where(c < r, later, 0.0), axis=1, keepdims=True))
                diag.append(acc)
                dadt_parts.append(jnp.where(halves[0], cols[0], cols[1]))
            dsc_b = dscores.astype(MXU_DTYPE)
            dc = dc + _dot(dsc_b, bb)
            db = db + _dot_tn(dsc_b, cb)
            dxc_ref[:, b_lo:b_lo + SSD_STATE] = db
            dxc_ref[:, c_lo:c_lo + SSD_STATE] = dc
            dxg_parts.append(jnp.concatenate(diag, axis=1) + dxg_state)
            state_term.append(dxg_state)
            yoff_parts.append(_dot(cb, s_prev[:, lanes].astype(MXU_DTYPE)) * ea_x[:, lanes])
            dstate[:, lanes] = ds_old[:, lanes] * eatot_x[:, lanes] + _dot_tn(cb, gea_b)
        dxg = jnp.concatenate(dxg_parts, axis=1)
        dxc_ref[:, 0:D_SSD] = dxg * dt_x
        through_out = _tri_dot((r <= c).astype(BF16), gy * jnp.concatenate(yoff_parts, axis=1))
        through_in = _tri_dot((c < r).astype(BF16), xg * jnp.concatenate(state_term, axis=1))
        carried = jnp.broadcast_to(_colsum(ds_old * s_prev) * eatot_x, (8, D_SSD))
        dadt = (jnp.concatenate(dadt_parts, axis=1) + head_sums(through_out + through_in)
                + jnp.max(head_sums(carried), axis=0, keepdims=True))
        ddt = a_x * dadt + head_sums(dxg * xs)
        draw = ddt * _sigmoid(dtx_ref[...] + dtbx_ref[...])
        ddtr_ref[...] = draw
        st_ref[0:1, :] += _colsum(dt_x * dadt) * a_x
        st_ref[1:2, :] += _colsum(draw)
        st_ref[2:3, :] += _colsum(gy * xs)

        @pl.when(pl.program_id(0) == nc - 1)
        def _():
            hs_ref[...] = head_sums(st_ref[...])

    stats = pl.BlockSpec((8, D_SSD), lambda i: (0, 0))
    return pl.pallas_call(
        body, name=name, grid=(nc,),
        in_specs=[xc_s, wide_s, wide_y_s, vec_s, vec_y_s, vec_s, vec_y_s, wide_s, state_s,
                  pl.BlockSpec((HEAD_TILE, D_SSD), lambda i: (0, 0))],
        out_specs=[xc_s, wide_s, stats, stats],
        out_shape=[jax.ShapeDtypeStruct((l, D_CONV), F32), jax.ShapeDtypeStruct((l, D_SSD), F32),
                   jax.ShapeDtypeStruct((8, D_SSD), F32), jax.ShapeDtypeStruct((8, D_SSD), F32)],
        scratch_shapes=[pltpu.VMEM((SSD_STATE, D_SSD), F32)],
    )(xc, dtr_x, dtr_y, dtb_x, dtb_y, al_x, al_y, dy, states, head_of_lanes)


def _ssd_gate_norm(ycore, xc, z, dskip_x, norm_w, name):
    l = ycore.shape[0]
    tl = _pick(l, 512, 16)
    row = pl.BlockSpec((tl, D_SSD), lambda i: (i, 0))
    vec = pl.BlockSpec((1, D_SSD), lambda i: (0, 0))

    def body(y_ref, xs_ref, z_ref, dk_ref, nw_ref, o_ref):
        zv = z_ref[...]
        yv = (y_ref[...] + dk_ref[...] * xs_ref[...]) * (zv * _sigmoid(zv))
        for g in range(SSD_GROUPS):
            lanes = slice(g * GROUP_WIDTH, (g + 1) * GROUP_WIDTH)
            yg = yv[:, lanes]
            rg = lax.rsqrt(jnp.mean(yg * yg, axis=-1, keepdims=True) + EPS)
            o_ref[:, lanes] = (yg * rg * nw_ref[:, lanes]).astype(o_ref.dtype)

    z_spec = pl.BlockSpec((tl, D_SSD), lambda i: (i, O_Z // D_SSD))
    return pl.pallas_call(body, name=name, grid=(l // tl,), in_specs=[row, row, z_spec, vec, vec], out_specs=row,
                          out_shape=jax.ShapeDtypeStruct((l, D_SSD), MXU_DTYPE))(ycore, xc, z, dskip_x, norm_w)


def _ssd_gate_norm_bwd(dout, ycore, xc, z, dskip_x, norm_w, name):
    l = ycore.shape[0]
    tl = _pick(l, 512, 16)
    nt = l // tl
    row = pl.BlockSpec((tl, D_SSD), lambda i: (i, 0))
    vec = pl.BlockSpec((1, D_SSD), lambda i: (0, 0))

    def body(do_ref, y_ref, xs_ref, z_ref, dk_ref, nw_ref, dyc_ref, dxs_ref, dz_ref, st_ref):
        @pl.when(pl.program_id(0) == 0)
        def _():
            st_ref[...] = jnp.zeros_like(st_ref)

        zv, xs = z_ref[...], xs_ref[...]
        s = _sigmoid(zv)
        gz = zv * s
        yc = y_ref[...] + dk_ref[...] * xs
        yv = yc * gz
        dov = do_ref[...]
        dnw, dyv = [], []
        for g in range(SSD_GROUPS):
            lanes = slice(g * GROUP_WIDTH, (g + 1) * GROUP_WIDTH)
            yg = yv[:, lanes]
            rg = lax.rsqrt(jnp.mean(yg * yg, axis=-1, keepdims=True) + EPS)
            yn = yg * rg
            dnw.append(_colsum(dov[:, lanes] * yn))
            dyn = dov[:, lanes] * nw_ref[:, lanes]
            dyv.append(rg * (dyn - yn * jnp.mean(dyn * yn, axis=-1, keepdims=True)))
        dy = jnp.concatenate(dyv, axis=1)
        dyc = dy * gz
        dyc_ref[...] = dyc
        dxs_ref[...] = dyc * dk_ref[...]
        dz_ref[...] = dy * yc * (s * (1.0 + zv * (1.0 - s)))
        st_ref[0:1, :] += jnp.concatenate(dnw, axis=1)

    return pl.pallas_call(
        body, name=name, grid=(nt,), in_specs=[row, row, row, pl.BlockSpec((tl, D_SSD), lambda i: (i, O_Z // D_SSD)), vec, vec],
        out_specs=[row, row, row, pl.BlockSpec((8, D_SSD), lambda i: (0, 0))],
        out_shape=[jax.ShapeDtypeStruct((l, D_SSD), F32)] * 3 + [jax.ShapeDtypeStruct((8, D_SSD), F32)],
    )(dout, ycore, xc, z, dskip_x, norm_w)


PAIR = 2 * HEAD_DIM
N_PAIRS = N_HEADS // 2
Q_TILE0 = O_Q // PAIR
SB_KEYS = 1024
SB_PAIRS = 2
SB_SCALE = HEAD_DIM ** -0.5


def _pair_sum(v, lo):
    s_lo = jnp.sum(jnp.where(lo, v, 0.0), axis=-1, keepdims=True)
    s_hi = jnp.sum(jnp.where(lo, 0.0, v), axis=-1, keepdims=True)
    return jnp.where(lo, s_lo, s_hi)


def _qkv_prep(proj, qw2, kw2, name):
    l = proj.shape[0]
    tl = _pick(l, 1024, 16)
    out = pl.BlockSpec((tl, PAIR), lambda i, j: (i, j))
    vec = pl.BlockSpec((1, PAIR), lambda i, j: (0, 0))

    def at(which):
        return pl.BlockSpec((tl, PAIR), lambda i, j: (i, Q_TILE0 + which * N_PAIRS + j))

    def body(q_ref, k_ref, v_ref, qw_ref, kw_ref, qn_ref, kn_ref, vb_ref):
        lo = lax.broadcasted_iota(jnp.int32, (tl, PAIR), 1) < HEAD_DIM
        for t_ref, w_ref, o_ref, scale in ((q_ref, qw_ref, qn_ref, SB_SCALE), (k_ref, kw_ref, kn_ref, 1.0)):
            tv = t_ref[...]
            r = lax.rsqrt(_pair_sum(tv * tv, lo) * (1.0 / HEAD_DIM) + EPS)
            o_ref[...] = ((tv * r) * w_ref[...] * scale).astype(o_ref.dtype)
        vb_ref[...] = v_ref[...].astype(vb_ref.dtype)

    return pl.pallas_call(body, name=name, grid=(l // tl, N_PAIRS), in_specs=[at(0), at(1), at(2), vec, vec],
                          out_specs=[out, out, out], out_shape=[jax.ShapeDtypeStruct((l, D_SB), MXU_DTYPE)] * 3,
                          )(proj, proj, proj, qw2, kw2)


def _qk_norm_bwd(proj, dqn, dkn, qw2, kw2, name):
    l = proj.shape[0]
    tl = _pick(l, 1024, 16)
    out = pl.BlockSpec((tl, PAIR), lambda i, j: (i, j))
    vec = pl.BlockSpec((1, PAIR), lambda i, j: (0, 0))
    stats = pl.BlockSpec((8, PAIR), lambda i, j: (0, 0))

    def at(which):
        return pl.BlockSpec((tl, PAIR), lambda i, j: (i, Q_TILE0 + which * N_PAIRS + j))

    def body(q_ref, k_ref, dqn_ref, dkn_ref, qw_ref, kw_ref, dq_ref, dk_ref, stq_ref, stk_ref):
        @pl.when((pl.program_id(0) == 0) & (pl.program_id(1) == 0))
        def _():
            stq_ref[...] = jnp.zeros_like(stq_ref)
            stk_ref[...] = jnp.zeros_like(stk_ref)

        lo = lax.broadcasted_iota(jnp.int32, (tl, PAIR), 1) < HEAD_DIM
        for t_ref, dn_ref, w_ref, d_ref, st_ref, scale in ((q_ref, dqn_ref, qw_ref, dq_ref, stq_ref, SB_SCALE),
                                                           (k_ref, dkn_ref, kw_ref, dk_ref, stk_ref, 1.0)):
            tv = t_ref[...]
            r = lax.rsqrt(_pair_sum(tv * tv, lo) * (1.0 / HEAD_DIM) + EPS)
            tn = tv * r
            dnv = dn_ref[...] * scale
            dtn = dnv * w_ref[...]
            d_ref[...] = r * (dtn - tn * (_pair_sum(dtn * tn, lo) * (1.0 / HEAD_DIM)))
            st_ref[0:1, :] += _colsum(dnv * tn)

    return pl.pallas_call(body, name=name, grid=(l // tl, N_PAIRS), in_specs=[at(0), at(1), out, out, vec, vec],
                          out_specs=[out, out, stats, stats],
                          out_shape=[jax.ShapeDtypeStruct((l, D_SB), F32)] * 2 + [jax.ShapeDtypeStruct((8, PAIR), F32)] * 2,
                          )(proj, proj, dqn, dkn, qw2, kw2)


def _sb_masks():
    r = lax.broadcasted_iota(jnp.int32, (CHUNK, CHUNK), 0)
    c = lax.broadcasted_iota(jnp.int32, (CHUNK, CHUNK), 1)
    return r, c


def _stack2(mask):
    t = mask.astype(BF16)
    return jnp.concatenate([t, t], axis=0)


def _sb_fwd(q, k, v, shards, name):
    l = q.shape[0]
    nq = l // CHUNK
    kt = _pick(l, SB_KEYS, CHUNK)
    sub = kt // CHUNK
    heads = 2 * SB_PAIRS
    steps = N_PAIRS // SB_PAIRS
    na = len(shards)
    qblk = pl.BlockSpec((CHUNK, SB_PAIRS * PAIR), lambda i, j: (j, i))
    full = pl.BlockSpec((l, SB_PAIRS * PAIR), lambda i, j: (0, i))
    hbm = pl.BlockSpec(memory_space=pl.ANY)

    def body(q_ref, k_ref, v_ref, *rest):
        o_ref, tot_ref = rest[na:na + 2]
        exchange = _DirectExchange("gather", rest[:na], rest[na + 2:2 * na + 2], *rest[2 * na + 2:])
        qb = pl.program_id(1)

        @pl.when((pl.program_id(0) == 0) & (qb == 0))
        def _():
            exchange.start()

        r, c = _sb_masks()
        after2 = _stack2(r > c)
        halves = _half_masks()
        zero = jnp.zeros((CHUNK, PAIR), q_ref.dtype)
        lanes = [slice((a // 2) * PAIR, (a // 2 + 1) * PAIR) for a in range(heads)]
        qm = [jnp.where(halves[a % 2], q_ref[:, lanes[a]], zero) for a in range(heads)]
        last = (qb * CHUNK) // kt

        def tile(t, width, carries, accs, masked):
            rows = pl.ds(pl.multiple_of(t * width, width), width)
            if masked:
                key_minus_query = (lax.broadcasted_iota(jnp.int32, (CHUNK, width), 1)
                                   - lax.broadcasted_iota(jnp.int32, (CHUNK, width), 0))
                keep = key_minus_query < qb * CHUNK - t * width

            def logits(a):
                lg = _dot_nt(qm[a], k_ref[rows, lanes[a]])
                sp = _softplus_logits(lg)
                return lg - sp, (jnp.where(keep, -sp, 0.0) if masked else -sp)

            def sums(a, lr):
                offset, parts = carries[a], [None] * (width // CHUNK)
                for j in reversed(range(width // CHUNK)):
                    piece = lr[:, j * CHUNK:(j + 1) * CHUNK]
                    parts[j] = _split_dot(piece, after2) + offset
                    offset = offset + jnp.sum(piece, axis=1, keepdims=True)
                return jnp.concatenate(parts, axis=1), offset

            def output(a, ls, cs):
                w = jnp.exp(ls + cs)
                if masked:
                    w = jnp.where(keep, w, 0.0)
                return accs[a] + _dot(w.astype(MXU_DTYPE), v_ref[rows, lanes[a]])

            new_carries, new_accs = [None] * heads, [None] * heads
            ls, lr = logits(0)
            for a in range(heads):
                cs, new_carries[a] = sums(a, lr)
                if a + 1 < heads:
                    ls_next, lr = logits(a + 1)
                new_accs[a] = output(a, ls, cs)
                ls = ls_next
            return tuple(new_carries), tuple(new_accs)

        carries = tuple(jnp.zeros((CHUNK, 1), F32) for _ in range(heads))
        accs = tuple(jnp.zeros((CHUNK, PAIR), F32) for _ in range(heads))
        if sub % 2 == 0:
            half = kt // 2
            carries, accs = lax.cond((qb * CHUNK) % kt >= half,
                                     lambda cr, ac: tile(2 * last + 1, half, cr, ac, True), lambda cr, ac: (cr, ac), carries, accs)
            carries, accs = tile(2 * last, half, carries, accs, True)
        else:
            carries, accs = tile(last, kt, carries, accs, True)
        carries, accs = lax.fori_loop(1, last + 1, lambda i, st: tile(last - i, kt, st[0], st[1], False), (carries, accs))
        for p in range(SB_PAIRS):
            o_ref[:, lanes[2 * p]] = jnp.where(halves[0], accs[2 * p], accs[2 * p + 1])
            tot_ref[:, lanes[2 * p]] = jnp.where(halves[0], carries[2 * p], carries[2 * p + 1])

        @pl.when((pl.program_id(0) == steps - 1) & (qb == nq - 1))
        def _():
            exchange.wait()

    return pl.pallas_call(
        body, name=name, grid=(steps, nq), in_specs=[qblk, full, full] + [hbm] * na,
        out_specs=[qblk, qblk] + [hbm] * na,
        out_shape=[jax.ShapeDtypeStruct((l, D_SB), F32), jax.ShapeDtypeStruct((l, D_SB), F32)]
        + [jax.ShapeDtypeStruct((N_DEV,) + s.shape, s.dtype) for s in shards],
        scratch_shapes=_DirectExchange.scratch(na),
    )(q, k, v, *shards)


def _sb_bwd(q, k, v, dycat, tot, blocks, name):
    l = q.shape[0]
    nq = l // CHUNK
    kt = _pick(l, SB_KEYS, CHUNK)
    sub = kt // CHUNK
    heads = 2 * SB_PAIRS
    steps = N_PAIRS // SB_PAIRS
    na = len(blocks)
    width_all = SB_PAIRS * PAIR
    qblk = pl.BlockSpec((CHUNK, width_all), lambda i, j: (j, i))
    doblk = pl.BlockSpec((CHUNK, width_all), lambda i, j: (j, D_SSD // width_all + i))
    full = pl.BlockSpec((l, width_all), lambda i, j: (0, i))
    hbm = pl.BlockSpec(memory_space=pl.ANY)

    def body(q_ref, k_ref, v_ref, do_ref, tot_ref, *rest):
        dq_ref, dk_ref, dv_ref = rest[na:na + 3]
        exchange = _DirectExchange("scatter", rest[:na], rest[na + 3:2 * na + 3], *rest[2 * na + 3:])
        qb = pl.program_id(1)

        @pl.when((pl.program_id(0) == 0) & (qb == 0))
        def _():
            exchange.start()

        @pl.when(qb == 0)
        def _():
            dk_ref[...] = jnp.zeros_like(dk_ref)
            dv_ref[...] = jnp.zeros_like(dv_ref)

        r, c = _sb_masks()
        upto2 = _stack2(r <= c)
        before2 = _stack2(r < c)
        halves = _half_masks()
        lanes = [slice((a // 2) * PAIR, (a // 2 + 1) * PAIR) for a in range(heads)]
        qm = [jnp.where(halves[a % 2], q_ref[:, lanes[a]], jnp.zeros((CHUNK, PAIR), q_ref.dtype)) for a in range(heads)]
        dom = [jnp.where(halves[a % 2], do_ref[:, lanes[a]], 0.0).astype(MXU_DTYPE) for a in range(heads)]
        total = [jnp.max(jnp.where(halves[a % 2], tot_ref[:, lanes[a]], -jnp.inf), axis=1, keepdims=True) for a in range(heads)]
        last = (qb * CHUNK) // kt

        def prefix(values, tri2, offset):
            parts = []
            for j in range(values.shape[1] // CHUNK):
                piece = values[:, j * CHUNK:(j + 1) * CHUNK]
                parts.append(_split_dot(piece, tri2) + offset)
                offset = offset + jnp.sum(piece, axis=1, keepdims=True)
            return jnp.concatenate(parts, axis=1), offset

        def tile(t, width, carry_p, carry_d, dq, masked):
            rows = pl.ds(pl.multiple_of(t * width, width), width)
            if masked:
                key_minus_query = (lax.broadcasted_iota(jnp.int32, (CHUNK, width), 1)
                                   - lax.broadcasted_iota(jnp.int32, (CHUNK, width), 0))
                keep = key_minus_query < qb * CHUNK - t * width

            def weights(a):
                lg = _dot_nt(qm[a], k_ref[rows, lanes[a]])
                sp = _softplus_logits(lg)
                lr = jnp.where(keep, -sp, 0.0) if masked else -sp
                p_incl, p_next = prefix(lr, upto2, carry_p[a])
                ls = lg - sp
                w = jnp.exp(ls + (total[a] - p_incl))
                if masked:
                    w = jnp.where(keep, w, 0.0)
                return ls, w, p_next

            def gradients(a, ls, w):
                da = _dot_nt(dom[a], v_ref[rows, lanes[a]]) * w
                d_excl, d_next = prefix(da, before2, carry_d[a])
                sig = jnp.exp(ls)
                dl = da * (1.0 - sig) - d_excl * sig
                if masked:
                    dl = jnp.where(keep, dl, 0.0)
                dl_b = dl.astype(MXU_DTYPE)
                return (d_next, dq[a] + _dot(dl_b, k_ref[rows, lanes[a]]),
                        _dot_tn(w.astype(MXU_DTYPE), dom[a]), _dot_tn(dl_b, qm[a]))

            new_p, new_d, new_dq = [None] * heads, [None] * heads, [None] * heads
            dv_upd, dk_upd = [None] * heads, [None] * heads
            ls, w, new_p[0] = weights(0)
            for a in range(heads):
                if a + 1 < heads:
                    ls_next, w_next, new_p[a + 1] = weights(a + 1)
                new_d[a], new_dq[a], dv_upd[a], dk_upd[a] = gradients(a, ls, w)
                if a % 2 == 1:
                    dv_ref[rows, lanes[a]] += dv_upd[a - 1] + dv_upd[a]
                    dk_ref[rows, lanes[a]] += dk_upd[a - 1] + dk_upd[a]
                if a + 1 < heads:
                    ls, w = ls_next, w_next
            return tuple(new_p), tuple(new_d), tuple(new_dq)

        zeros = tuple(jnp.zeros((CHUNK, 1), F32) for _ in range(heads))
        dq0 = tuple(jnp.zeros((CHUNK, PAIR), F32) for _ in range(heads))
        state = lax.fori_loop(0, last, lambda t, st: tile(t, kt, st[0], st[1], st[2], False), (zeros, zeros, dq0))
        if sub % 2 == 0:
            half = kt // 2
            state = tile(2 * last, half, *state, True)
            state = lax.cond((qb * CHUNK) % kt >= half,
                             lambda cp, cd, dq: tile(2 * last + 1, half, cp, cd, dq, True), lambda cp, cd, dq: (cp, cd, dq), *state)
        else:
            state = tile(last, kt, *state, True)
        for p in range(SB_PAIRS):
            dq_ref[:, lanes[2 * p]] = jnp.where(halves[0], state[2][2 * p], state[2][2 * p + 1])

        @pl.when((pl.program_id(0) == steps - 1) & (qb == nq - 1))
        def _():
            exchange.wait()

    return pl.pallas_call(
        body, name=name, grid=(steps, nq),
        in_specs=[qblk, full, full, doblk, qblk] + [hbm] * na,
        out_specs=[qblk, full, full] + [hbm] * na,
        out_shape=[jax.ShapeDtypeStruct((l, D_SB), F32)] * 3 + [jax.ShapeDtypeStruct(b.shape, b.dtype) for b in blocks],
        scratch_shapes=_DirectExchange.scratch(na),
    )(q, k, v, dycat, tot, *blocks)


def _ada_fwd(c_all, w_shard, name):
    def body(c_ref, w_ref, o_ref):
        cv = c_ref[...]
        o_ref[...] = _dot(cv * _sigmoid(cv), w_ref[...], HIGHEST)

    return pl.pallas_call(body, name=name, out_shape=jax.ShapeDtypeStruct((c_all.shape[0], w_shard.shape[1]), F32))(c_all, w_shard)


def _ada_bwd(c_all, dmod_shard, name):
    def body(c_ref, d_ref, o_ref):
        cv = c_ref[...]
        o_ref[...] = _dot_tn(cv * _sigmoid(cv), d_ref[...], HIGHEST)

    return pl.pallas_call(body, name=name, out_shape=jax.ShapeDtypeStruct((c_all.shape[1], dmod_shard.shape[1]), F32))(c_all, dmod_shard)


def _sum_small(parts, name):
    def body(p_ref, o_ref):
        acc = p_ref[0]
        for d in range(1, N_DEV):
            acc = acc + p_ref[d]
        o_ref[...] = acc

    return pl.pallas_call(body, name=name, out_shape=jax.ShapeDtypeStruct(parts.shape[1:], F32))(parts)


def _adamw(w, g, m, v, name):
    rows, cols = w.shape
    tr = _pick(rows, 256, 8)
    spec = pl.BlockSpec((tr, cols), lambda i: (i, 0))
    bc1 = 1.0 - ADAM_B1 ** ADAM_STEP
    bc2 = 1.0 - ADAM_B2 ** ADAM_STEP

    def body(w_ref, g_ref, m_ref, v_ref, d_ref, nm_ref, nv_ref):
        gv = g_ref[...]
        nm = ADAM_B1 * m_ref[...] + (1.0 - ADAM_B1) * gv
        nv = ADAM_B2 * v_ref[...] + (1.0 - ADAM_B2) * (gv * gv)
        nm_ref[...] = nm
        nv_ref[...] = nv
        d_ref[...] = -ADAM_LR * ((nm / bc1) / (jnp.sqrt(nv / bc2) + ADAM_EPS) + ADAM_WD * w_ref[...])

    return pl.pallas_call(body, name=name, grid=(rows // tr,), in_specs=[spec] * 4, out_specs=[spec] * 3,
                          out_shape=[jax.ShapeDtypeStruct((rows, cols), F32)] * 3)(w, g, m, v)


def _pad_w_in_t(w_in_t):
    lo = D_SSD + D_CONV
    return jnp.concatenate([w_in_t[D_SSD:lo + N_HEADS], jnp.zeros((DT_PAD - N_HEADS, w_in_t.shape[1]), w_in_t.dtype),
                            w_in_t[:D_SSD], w_in_t[lo + N_HEADS:]], axis=0)


def _unpad_w_in_t(g):
    return jnp.concatenate([g[O_Z:O_Q], g[:O_DT + N_HEADS], g[O_Q:]], axis=0)


def _local_step(x, target, mod, norm1_w, w_in_tp, conv_w, conv_b, dt_bias, a_log, d_skip, ssd_norm_w, q_norm_w, k_norm_w,
                norm2_w, later_shards):
    sh1, sc1, g1, sh2, sc2, g2 = [mod[:, i * D_MODEL:(i + 1) * D_MODEL] for i in range(N_MOD)]
    qw2, kw2 = jnp.tile(q_norm_w, (1, 2)), jnp.tile(k_norm_w, (1, 2))
    dskip_x = jnp.repeat(d_skip, HEAD_DIM, axis=1)
    dtb_x, dtb_y = jnp.repeat(dt_bias, HEAD_DIM, axis=1), jnp.repeat(dt_bias, HEAD_TILE, axis=1)
    al_x, al_y = jnp.repeat(a_log, HEAD_DIM, axis=1), jnp.repeat(a_log, HEAD_TILE, axis=1)
    head_of_lanes = (jnp.arange(HEAD_TILE, dtype=jnp.int32)[:, None]
                     == jnp.arange(D_SSD, dtype=jnp.int32)[None, :] // HEAD_DIM).astype(BF16)

    h1 = _rms_mod(x, norm1_w, sc1, sh1, "rms_mod1")
    proj = _matmul_nt(h1, w_in_tp, "in_proj")
    dtr = proj[:, O_DT:O_DT + N_HEADS]
    dtr_x, dtr_y = jnp.repeat(dtr, HEAD_DIM, axis=1), jnp.repeat(dtr, HEAD_TILE, axis=1)

    xc = _conv_silu(proj, conv_w, conv_b, "conv_silu")
    ycore, states = _ssd_fwd(xc, dtr_x, dtr_y, dtb_x, dtb_y, al_x, al_y, "ssd_fwd")
    y_ssd = _ssd_gate_norm(ycore, xc, proj, dskip_x, ssd_norm_w, "ssd_gate_norm")

    qn, kn, vb = _qkv_prep(proj, qw2, kw2, "qkv_prep")
    o_sb, tot, *gathered = _sb_fwd(qn, kn, vb, later_shards, "sb_fwd")
    w_out, w_gate_t, w_up_t, w_down = [g.reshape(N_DEV * g.shape[1], D_MODEL) for g in gathered]
    ycat = jnp.concatenate([y_ssd, o_sb.astype(MXU_DTYPE)], axis=1)

    mix = _matmul(ycat, w_out, "out_proj")
    x1, h2 = _residual_rms_mod(x, mix, g1, norm2_w, sc2, sh2, "residual_rms_mod2")
    gate = _matmul_nt(h2, w_gate_t, "ffn_gate")
    up = _matmul_nt(h2, w_up_t, "ffn_up")
    act = _swiglu(gate, up, "swiglu")
    ffn = _matmul(act, w_down, "ffn_down")
    dy, loss_blk = _loss_head(x1, ffn, g2, target, "loss_head")

    dffn, st_g2 = _gate_bwd(dy, ffn, g2, "gate2_bwd")
    g_down = _matmul_tn(act, dffn, "ffn_down_dw", MXU_DTYPE)
    dact = _matmul_nt(dffn, w_down, "ffn_down_dx")
    dgate, dup = _swiglu_bwd(gate, up, dact, "swiglu_bwd")
    g_gate_t = _matmul_tn(dgate, h2, "ffn_gate_dw", MXU_DTYPE)
    g_up_t = _matmul_tn(dup, h2, "ffn_up_dw", MXU_DTYPE)
    dh2 = _matmul_sum2(dgate, w_gate_t, dup, w_up_t, "ffn_dh")
    dx1, st_n2 = _norm_bwd(x1, dh2, dy, norm2_w, sc2, "norm2_bwd")

    dmix, st_g1 = _gate_bwd(dx1, mix, g1, "gate1_bwd")
    g_out = _matmul_tn(ycat, dmix, "out_proj_dw", MXU_DTYPE)
    dycat = _matmul_nt(dmix, w_out, "out_proj_dx")

    partials = [g.reshape(N_DEV, g.shape[0] // N_DEV, D_MODEL).astype(BF16) for g in (g_out, g_gate_t, g_up_t, g_down)]
    dqn, dkn, dv, *slots = _sb_bwd(qn, kn, vb, dycat, tot, partials, "sb_bwd")
    dq, dk, st_q, st_k = _qk_norm_bwd(proj, dqn, dkn, qw2, kw2, "qk_norm_bwd")

    dycore, dxs_skip, dz, st_gn = _ssd_gate_norm_bwd(dycat, ycore, xc, proj, dskip_x, ssd_norm_w, "ssd_gate_norm_bwd")
    dxc, ddtr_x, st_ssd, st_heads = _ssd_bwd(xc, dtr_x, dtr_y, dtb_x, dtb_y, al_x, al_y, dycore, states, head_of_lanes, "ssd_bwd")
    ddtr = ddtr_x[:, ::HEAD_DIM]
    dxc = jnp.concatenate([dxc[:, :D_SSD] + dxs_skip, dxc[:, D_SSD:]], axis=1)
    dpre, st_conv = _conv_silu_bwd_pre(proj, dxc, conv_w, conv_b, "conv_silu_bwd")
    dxbc = _conv_bwd_input(dpre, conv_w, "conv_bwd_input")

    dproj = jnp.concatenate([dxbc, ddtr, jnp.zeros((x.shape[0], DT_PAD - N_HEADS), F32), dz, dq, dk, dv], axis=1).astype(MXU_DTYPE)
    g_in_t = _unpad_w_in_t(_matmul_tn(dproj, h1, "in_proj_dw", MXU_DTYPE))
    g_in_blocks = g_in_t.reshape(N_DEV, g_in_t.shape[0] // N_DEV, D_MODEL).astype(BF16)
    dh1, g_in_slots = _matmul_beside_scatter(dproj, w_in_tp, [g_in_blocks], "in_proj_dx")
    slots = [g_in_slots] + list(slots)
    grad_x, st_n1 = _norm_bwd(x, dh1, dx1, norm1_w, sc1, "norm1_bwd")

    pad = jnp.zeros((1, SM_SSD_NORM - SM_D_SKIP - N_HEADS), F32)
    small = jnp.concatenate(
        [st_n1[1:2], st_n1[0:1], st_g1[0:1], st_n2[1:2], st_n2[0:1], st_g2[0:1],
         st_n1[2:3], st_conv[4:5], st_conv[0:4].reshape(1, CONV_WIDTH * D_CONV),
         st_ssd[1:2, ::HEAD_DIM], st_ssd[0:1, ::HEAD_DIM], st_heads[2:3, ::HEAD_DIM], pad,
         st_gn[0:1], st_q[0:1, :HEAD_DIM] + st_q[0:1, HEAD_DIM:], st_k[0:1, :HEAD_DIM] + st_k[0:1, HEAD_DIM:], st_n2[2:3]], axis=1)
    return loss_blk, grad_x, slots, small


def kernel(x, c, w_ada, b_ada, norm1_w, w_in, conv_w, conv_b, dt_bias, a_log, d_skip, ssd_norm_w, q_norm_w, k_norm_w, w_out, norm2_w, w_gate, w_up, w_down, loss_target, m_w_ada, m_b_ada, m_norm1_w, m_w_in, m_conv_w, m_conv_b, m_dt_bias, m_a_log, m_d_skip, m_ssd_norm_w, m_q_norm_w, m_k_norm_w, m_w_out, m_norm2_w, m_w_gate, m_w_up, m_w_down, v_w_ada, v_b_ada, v_norm1_w, v_w_in, v_conv_w, v_conv_b, v_dt_bias, v_a_log, v_d_skip, v_ssd_norm_w, v_q_norm_w, v_k_norm_w, v_w_out, v_norm2_w, v_w_gate, v_w_up, v_w_down):
    me = 4 * lax.axis_index("x") + 2 * lax.axis_index("y") + lax.axis_index("c")
    conv_cols = D_CONV // N_DEV
    ada_cols = N_MOD * D_MODEL // N_DEV

    (w_in_t,) = _all_gather_big([w_in[0].T.astype(BF16)], "gather_w_in")
    w_in_tp = _pad_w_in_t(w_in_t.reshape(D_IN_PROJ, D_MODEL))
    later_shards = [s.astype(BF16) for s in (w_out[0], w_gate[0].T, w_up[0].T, w_down[0])]

    first = _all_gather_small(jnp.concatenate([c, conv_w[0].reshape(1, CONV_WIDTH * conv_cols)], axis=1), "gather_cond")
    c_all = first[:, 0, :D_MODEL]
    conv_w_f = first[:, 0, D_MODEL:].reshape(N_DEV, CONV_WIDTH, conv_cols).transpose(1, 0, 2).reshape(CONV_WIDTH, D_CONV)
    mod_cols = _ada_fwd(c_all, w_ada[0], "ada_fwd")
    mods = _all_gather_small(mod_cols.reshape(1, N_DEV * ada_cols), "gather_mod")
    mod = lax.dynamic_index_in_dim(mods.reshape(N_DEV, N_DEV, ada_cols), me, axis=1, keepdims=False)
    mod = mod.reshape(1, N_MOD * D_MODEL) + b_ada

    loss_blk, grad_x, slots, small = _local_step(
        x[0], loss_target[0], mod, norm1_w, w_in_tp, conv_w_f, conv_b, dt_bias, a_log, d_skip, ssd_norm_w, q_norm_w, k_norm_w,
        norm2_w, later_shards)
    loss = lax.psum(loss_blk[0, 0], ("x", "y", "c"))

    g_in_t, g_out, g_gate_t, g_up_t, g_down = [_sum_slots(s, "sum_grads_" + n) for s, n in zip(slots, ("in", "out", "gate", "up", "down"))]
    g_in, g_gate, g_up = g_in_t.T, g_gate_t.T, g_up_t.T

    parts = _all_gather_small(small, "gather_small")
    gsum = _sum_small(parts, "sum_small")
    dmod_shard = lax.dynamic_slice_in_dim(parts[:, 0, :N_MOD * D_MODEL], me * ada_cols, ada_cols, axis=1)
    g_ada = _ada_bwd(c_all, dmod_shard, "ada_bwd")
    g_conv_w = lax.dynamic_slice_in_dim(gsum[:, SM_CONV_W:SM_DT_BIAS].reshape(CONV_WIDTH, D_CONV), me * conv_cols, conv_cols, axis=1)

    def pack_small(b_ada_, norm1_, conv_b_, dt_bias_, a_log_, d_skip_, ssd_norm_, q_norm_, k_norm_, norm2_):
        return jnp.concatenate(
            [b_ada_, norm1_, conv_b_, jnp.zeros((1, CONV_WIDTH * D_CONV), F32), dt_bias_, a_log_, d_skip_,
             jnp.zeros((1, SM_SSD_NORM - SM_D_SKIP - N_HEADS), F32), ssd_norm_, q_norm_, k_norm_, norm2_], axis=1)

    def unpack_small(p):
        return {
            "b_ada": p[:, SM_B_ADA:SM_NORM1], "norm1_w": p[:, SM_NORM1:SM_CONV_B], "conv_b": p[:, SM_CONV_B:SM_CONV_W],
            "dt_bias": p[:, SM_DT_BIAS:SM_A_LOG], "a_log": p[:, SM_A_LOG:SM_D_SKIP], "d_skip": p[:, SM_D_SKIP:SM_D_SKIP + N_HEADS],
            "ssd_norm_w": p[:, SM_SSD_NORM:SM_Q_NORM], "q_norm_w": p[:, SM_Q_NORM:SM_K_NORM], "k_norm_w": p[:, SM_K_NORM:SM_NORM2],
            "norm2_w": p[:, SM_NORM2:SM_TOTAL]}

    w_small = pack_small(b_ada, norm1_w, conv_b, dt_bias, a_log, d_skip, ssd_norm_w, q_norm_w, k_norm_w, norm2_w)
    m_small = pack_small(m_b_ada, m_norm1_w, m_conv_b, m_dt_bias, m_a_log, m_d_skip, m_ssd_norm_w, m_q_norm_w, m_k_norm_w, m_norm2_w)
    v_small = pack_small(v_b_ada, v_norm1_w, v_conv_b, v_dt_bias, v_a_log, v_d_skip, v_ssd_norm_w, v_q_norm_w, v_k_norm_w, v_norm2_w)
    small_out = [unpack_small(t) for t in (gsum,) + tuple(_adamw(w_small, gsum, m_small, v_small, "adamw_small"))]

    sharded = {
        "w_ada": (w_ada[0], g_ada, m_w_ada[0], v_w_ada[0]),
        "w_in": (w_in[0], g_in, m_w_in[0], v_w_in[0]),
        "conv_w": (conv_w[0], g_conv_w, m_conv_w[0], v_conv_w[0]),
        "w_out": (w_out[0], g_out, m_w_out[0], v_w_out[0]),
        "w_gate": (w_gate[0], g_gate, m_w_gate[0], v_w_gate[0]),
        "w_up": (w_up[0], g_up, m_w_up[0], v_w_up[0]),
        "w_down": (w_down[0], g_down, m_w_down[0], v_w_down[0]),
    }
    sharded_out = {n: (t[1],) + tuple(_adamw(*t, "adamw_" + n)) for n, t in sharded.items()}

    names = ["w_ada", "b_ada", "norm1_w", "w_in", "conv_w", "conv_b", "dt_bias", "a_log", "d_skip", "ssd_norm_w", "q_norm_w",
             "k_norm_w", "w_out", "norm2_w", "w_gate", "w_up", "w_down"]
    outs = [loss, grad_x[None]]
    for kind in range(4):
        for n in names:
            outs.append(sharded_out[n][kind][None] if n in sharded_out else small_out[kind][n])
    return tuple(outs)
```

```python
import functools

import jax
import jax.numpy as jnp
from jax import lax
from jax.experimental import pallas as pl
from jax.experimental.pallas import tpu as pltpu

F32 = jnp.float32
BF16 = jnp.bfloat16
MXU_DTYPE = jnp.bfloat16
HIGHEST = lax.Precision.HIGHEST
MESH_IDS = pl.DeviceIdType.MESH

N_DEV = 8
D_MODEL = 1024
HEAD_DIM = 64
N_HEADS = 16
D_SSD = 1024
D_SB = 1024
SSD_GROUPS = 2
SSD_STATE = 128
GROUP_WIDTH = D_SSD // SSD_GROUPS
D_CONV = D_SSD + 2 * SSD_GROUPS * SSD_STATE
CONV_WIDTH = 4
CHUNK = 128
D_FF = 2816
N_MOD = 6
EPS = 1e-6
D_IN_PROJ = 5648
O_XBC = 0
O_DT = D_CONV
DT_PAD = 512
O_Z = O_DT + DT_PAD
O_Q = O_Z + D_SSD
D_IN_PAD = O_Q + 3 * D_SB
HALO = 8

ADAM_LR = 0.001
ADAM_B1 = 0.9
ADAM_B2 = 0.999
ADAM_EPS = 1e-08
ADAM_WD = 0.01
ADAM_STEP = 10

ROWS_W_IN = 706
ROWS_W_OUT = 256
ROWS_FF = 352
PACK_ROWS = 2048
SM_B_ADA = 0
SM_NORM1 = 6144
SM_CONV_B = 7168
SM_CONV_W = 8704
SM_DT_BIAS = 14848
SM_A_LOG = 14864
SM_D_SKIP = 14880
SM_SSD_NORM = 14976
SM_Q_NORM = 16000
SM_K_NORM = 16064
SM_NORM2 = 16128
SM_TOTAL = 17152


def _pick(n, cap, mult):
    if n <= cap:
        return n
    best = None
    for t in range(mult, cap + 1, mult):
        if n % t == 0:
            best = t
    assert best is not None, (n, cap, mult)
    return best


def _dot(a, b, precision=None):
    return jnp.dot(a, b, preferred_element_type=F32, precision=precision)


def _dot_nt(a, b, precision=None):
    return lax.dot_general(a, b, (((1,), (1,)), ((), ())), preferred_element_type=F32, precision=precision)


def _dot_tn(a, b, precision=None):
    return lax.dot_general(a, b, (((0,), (0,)), ((), ())), preferred_element_type=F32, precision=precision)


def _softplus(v):
    return jnp.maximum(v, 0.0) + jnp.log1p(jnp.exp(-jnp.abs(v)))


def _softplus_logits(v):
    return jnp.maximum(v, 0.0) + jnp.log(1.0 + jnp.exp(-jnp.abs(v)))


def _tri_dot(tri_b, v):
    hi = v.astype(BF16)
    r1 = v - hi.astype(F32)
    mid = r1.astype(BF16)
    lo = (r1 - mid.astype(F32)).astype(BF16)
    return _dot(tri_b, hi) + _dot(tri_b, mid) + _dot(tri_b, lo)


def _sigmoid(v):
    return jax.nn.sigmoid(v)


def _colsum(v):
    return jnp.sum(v, axis=0, keepdims=True)


def _split_dot(v, tri2):
    hi = v.astype(BF16)
    lo = (v - hi.astype(F32)).astype(BF16)
    return _dot(jnp.concatenate([hi, lo], axis=1), tri2)


def _position():
    x, y, c = lax.axis_index("x"), lax.axis_index("y"), lax.axis_index("c")
    return x, y, c


def _peer(x, y, c, k):
    px = 1 - x if (k >> 2) & 1 else x
    py = 1 - y if (k >> 1) & 1 else y
    pc = 1 - c if k & 1 else c
    return px, py, pc


def _all_gather_small(v, name):
    n = v.shape[1]

    def body(v_ref, out_ref, send_sems, recv_sems, local_sem):
        x, y, c = _position()
        me = 4 * x + 2 * y + c
        mine = pltpu.make_async_copy(v_ref, out_ref.at[me], local_sem)
        mine.start()
        sends = []
        for k in range(1, N_DEV):
            cp = pltpu.make_async_remote_copy(
                src_ref=v_ref, dst_ref=out_ref.at[me], send_sem=send_sems.at[k - 1], recv_sem=recv_sems.at[k - 1],
                device_id=_peer(x, y, c, k), device_id_type=MESH_IDS)
            cp.start()
            sends.append(cp)
        for k in range(1, N_DEV):
            px, py, pc = _peer(x, y, c, k)
            pltpu.make_async_remote_copy(
                src_ref=v_ref, dst_ref=out_ref.at[4 * px + 2 * py + pc], send_sem=send_sems.at[k - 1],
                recv_sem=recv_sems.at[k - 1], device_id=(px, py, pc), device_id_type=MESH_IDS).wait_recv()
        for cp in sends:
            cp.wait_send()
        mine.wait()

    return pl.pallas_call(
        body, name=name,
        out_shape=jax.ShapeDtypeStruct((N_DEV, 1, n), v.dtype),
        in_specs=[pl.BlockSpec(memory_space=pltpu.VMEM)],
        out_specs=pl.BlockSpec(memory_space=pltpu.VMEM),
        scratch_shapes=[pltpu.SemaphoreType.DMA((N_DEV - 1,)), pltpu.SemaphoreType.DMA((N_DEV - 1,)),
                        pltpu.SemaphoreType.DMA],
    )(v)


def _all_gather_big(blocks, name):
    na = len(blocks)
    copies = N_DEV - 1

    def body(*refs):
        b_refs, out_refs = refs[:na], refs[na:2 * na]
        send_sems, recv_sems, local_sems = refs[2 * na:]
        x, y, c = _position()
        me, sibling = (x, y, c), (x, y, 1 - c)
        chips = [(1 - x, y), (x, 1 - y), (1 - x, 1 - y)]

        def copy(a, k, blk, to, own=False):
            slot = out_refs[a].at[4 * blk[0] + 2 * blk[1] + blk[2]]
            return pltpu.make_async_remote_copy(
                src_ref=b_refs[a] if own else slot, dst_ref=slot,
                send_sem=send_sems.at[copies * a + k], recv_sem=recv_sems.at[copies * a + k], device_id=to, device_id_type=MESH_IDS)

        mine = [pltpu.make_async_copy(b_refs[a], out_refs[a].at[4 * x + 2 * y + c], local_sems.at[a]) for a in range(na)]
        for cp in mine:
            cp.start()
        first = [copy(a, 0, me, sibling, own=True) for a in range(na)]
        first += [copy(a, 1 + j, me, (*chip, c), own=True) for j, chip in enumerate(chips) for a in range(na)]
        for cp in first:
            cp.start()
        passed = []
        for j, chip in enumerate(chips):
            for a in range(na):
                copy(a, 1 + j, (*chip, c), me).wait_recv()
                passed.append(copy(a, 4 + j, (*chip, c), sibling))
                passed[-1].start()
        for a in range(na):
            copy(a, 0, sibling, me).wait_recv()
        for j, chip in enumerate(chips):
            for a in range(na):
                copy(a, 4 + j, (*chip, 1 - c), me).wait_recv()
        for cp in first + passed:
            cp.wait_send()
        for cp in mine:
            cp.wait()

    hbm = pl.BlockSpec(memory_space=pl.ANY)
    return pl.pallas_call(
        body, name=name,
        out_shape=[jax.ShapeDtypeStruct((N_DEV,) + b.shape, b.dtype) for b in blocks],
        in_specs=[hbm] * na, out_specs=[hbm] * na,
        scratch_shapes=[pltpu.SemaphoreType.DMA((copies * na,)), pltpu.SemaphoreType.DMA((copies * na,)),
                        pltpu.SemaphoreType.DMA((na,))],
    )(*blocks)


class _DirectExchange:
    def __init__(self, kind, src_refs, dst_refs, send_sems, recv_sems, local_sems):
        self.kind, self.src_refs, self.dst_refs = kind, src_refs, dst_refs
        self.send_sems, self.recv_sems, self.local_sems = send_sems, recv_sems, local_sems

    @staticmethod
    def scratch(n_arrays):
        copies = N_DEV - 1
        return [pltpu.SemaphoreType.DMA((copies * n_arrays,)), pltpu.SemaphoreType.DMA((copies * n_arrays,)),
                pltpu.SemaphoreType.DMA((n_arrays,))]

    def _copies(self):
        x, y, c = _position()
        me = 4 * x + 2 * y + c
        local, sends, arrivals = [], [], []
        for a, (src, dst) in enumerate(zip(self.src_refs, self.dst_refs)):
            own = src if self.kind == "gather" else src.at[me]
            local.append(pltpu.make_async_copy(own, dst.at[me], self.local_sems.at[a]))
            for k in range(1, N_DEV):
                px, py, pc = _peer(x, y, c, k)
                peer = 4 * px + 2 * py + pc
                sems = dict(send_sem=self.send_sems.at[(N_DEV - 1) * a + k - 1], recv_sem=self.recv_sems.at[(N_DEV - 1) * a + k - 1],
                            device_id=(px, py, pc), device_id_type=MESH_IDS)
                sends.append(pltpu.make_async_remote_copy(
                    src_ref=src if self.kind == "gather" else src.at[peer], dst_ref=dst.at[me], **sems))
                arrivals.append(pltpu.make_async_remote_copy(src_ref=own, dst_ref=dst.at[peer], **sems))
        return local, sends, arrivals

    def start(self):
        local, sends, _ = self._copies()
        for cp in local + sends:
            cp.start()

    def wait(self):
        local, sends, arrivals = self._copies()
        for cp in arrivals:
            cp.wait_recv()
        for cp in sends:
            cp.wait_send()
        for cp in local:
            cp.wait()


def _all_to_all_big(blocks, name):
    na = len(blocks)

    def body(*refs):
        exchange = _DirectExchange("scatter", refs[:na], refs[na:2 * na], *refs[2 * na:])
        exchange.start()
        exchange.wait()

    hbm = pl.BlockSpec(memory_space=pl.ANY)
    return pl.pallas_call(
        body, name=name,
        out_shape=[jax.ShapeDtypeStruct(b.shape, b.dtype) for b in blocks],
        in_specs=[hbm] * na, out_specs=[hbm] * na, scratch_shapes=_DirectExchange.scratch(na),
    )(*blocks)


def _sum_slots(slots, name):
    _, r, n = slots.shape
    tr = _pick(r, 256, 16) if r % 16 == 0 else r

    def body(s_ref, o_ref):
        acc = s_ref[0].astype(F32)
        for d in range(1, N_DEV):
            acc = acc + s_ref[d].astype(F32)
        o_ref[...] = acc

    return pl.pallas_call(
        body, name=name, grid=(r // tr,),
        in_specs=[pl.BlockSpec((N_DEV, tr, n), lambda i: (0, i, 0))],
        out_specs=pl.BlockSpec((tr, n), lambda i: (i, 0)),
        out_shape=jax.ShapeDtypeStruct((r, n), F32),
    )(slots)


def _matmul(a, b, name, out_dtype=F32):
    m, k = a.shape
    _, n = b.shape
    tm, tn, tk = _pick(m, 1024, 16), _pick(n, 1408, 128), _pick(k, 1408, 128)
    nk = k // tk

    def body(a_ref, b_ref, o_ref, acc_ref):
        kk = pl.program_id(2)

        @pl.when(kk == 0)
        def _():
            acc_ref[...] = jnp.zeros_like(acc_ref)

        acc_ref[...] += _dot(a_ref[...].astype(MXU_DTYPE), b_ref[...].astype(MXU_DTYPE))

        @pl.when(kk == nk - 1)
        def _():
            o_ref[...] = acc_ref[...].astype(o_ref.dtype)

    return pl.pallas_call(
        body, name=name, grid=(m // tm, n // tn, nk),
        in_specs=[pl.BlockSpec((tm, tk), lambda i, j, kk: (i, kk)), pl.BlockSpec((tk, tn), lambda i, j, kk: (kk, j))],
        out_specs=pl.BlockSpec((tm, tn), lambda i, j, kk: (i, j)),
        out_shape=jax.ShapeDtypeStruct((m, n), out_dtype),
        scratch_shapes=[pltpu.VMEM((tm, tn), F32)],
    )(a, b)


def _matmul_pieces_beside_scatter(pieces, offsets, w, blocks, name):
    m = pieces[0].shape[0]
    n = w.shape[1]
    tm = _pick(m, 256, 16)
    gm = m // tm
    npieces, na = len(pieces), len(blocks)
    hbm = pl.BlockSpec(memory_space=pl.ANY)

    def body(*refs):
        a_refs, w_refs = refs[:npieces], refs[npieces:2 * npieces]
        rest = refs[2 * npieces:]
        o_ref = rest[na]
        exchange = _DirectExchange("scatter", rest[:na], rest[na + 1:2 * na + 1], *rest[2 * na + 1:])
        i = pl.program_id(0)

        @pl.when(i == 0)
        def _():
            exchange.start()

        acc = _dot(a_refs[0][...].astype(MXU_DTYPE), w_refs[0][...])
        for a_ref, w_ref in zip(a_refs[1:], w_refs[1:]):
            acc = acc + _dot(a_ref[...].astype(MXU_DTYPE), w_ref[...])
        o_ref[...] = acc

        @pl.when(i == gm - 1)
        def _():
            exchange.wait()

    a_specs = [pl.BlockSpec((tm, p.shape[1]), lambda i: (i, 0)) for p in pieces]
    w_specs = [pl.BlockSpec((p.shape[1], n), functools.partial(lambda i, blk: (blk, 0), blk=off // p.shape[1]))
               for p, off in zip(pieces, offsets)]
    return pl.pallas_call(
        body, name=name, grid=(gm,),
        in_specs=a_specs + w_specs + [hbm] * na,
        out_specs=[pl.BlockSpec((tm, n), lambda i: (i, 0))] + [hbm] * na,
        out_shape=[jax.ShapeDtypeStruct((m, n), F32)] + [jax.ShapeDtypeStruct(blk.shape, blk.dtype) for blk in blocks],
        scratch_shapes=_DirectExchange.scratch(na),
    )(*pieces, *([w] * npieces), *blocks)


def _matmul_nt(a, bt, name, out_dtype=F32):
    m, k = a.shape
    n, _ = bt.shape
    tm, tn, tk = _pick(m, 1024, 16), _pick(n, 1408, 128), _pick(k, 1408, 128)
    nk = k // tk

    def body(a_ref, b_ref, o_ref, acc_ref):
        kk = pl.program_id(2)

        @pl.when(kk == 0)
        def _():
            acc_ref[...] = jnp.zeros_like(acc_ref)

        acc_ref[...] += _dot_nt(a_ref[...].astype(MXU_DTYPE), b_ref[...].astype(MXU_DTYPE))

        @pl.when(kk == nk - 1)
        def _():
            o_ref[...] = acc_ref[...].astype(o_ref.dtype)

    return pl.pallas_call(
        body, name=name, grid=(m // tm, n // tn, nk),
        in_specs=[pl.BlockSpec((tm, tk), lambda i, j, kk: (i, kk)), pl.BlockSpec((tn, tk), lambda i, j, kk: (j, kk))],
        out_specs=pl.BlockSpec((tm, tn), lambda i, j, kk: (i, j)),
        out_shape=jax.ShapeDtypeStruct((m, n), out_dtype),
        scratch_shapes=[pltpu.VMEM((tm, tn), F32)],
    )(a, bt)


def _matmul_sum2(a1, b1, a2, b2, name):
    m, k = a1.shape
    _, n = b1.shape
    tm, tn, tk = _pick(m, 1024, 16), _pick(n, 1408, 128), _pick(k, 1408, 128)
    nk = k // tk

    def body(a1_ref, b1_ref, a2_ref, b2_ref, o_ref):
        @pl.when(pl.program_id(2) == 0)
        def _():
            o_ref[...] = jnp.zeros_like(o_ref)

        o_ref[...] += (_dot(a1_ref[...].astype(MXU_DTYPE), b1_ref[...].astype(MXU_DTYPE))
                       + _dot(a2_ref[...].astype(MXU_DTYPE), b2_ref[...].astype(MXU_DTYPE)))

    a_spec = pl.BlockSpec((tm, tk), lambda i, j, kk: (i, kk))
    b_spec = pl.BlockSpec((tk, tn), lambda i, j, kk: (kk, j))
    return pl.pallas_call(
        body, name=name, grid=(m // tm, n // tn, nk), in_specs=[a_spec, b_spec, a_spec, b_spec],
        out_specs=pl.BlockSpec((tm, tn), lambda i, j, kk: (i, j)),
        out_shape=jax.ShapeDtypeStruct((m, n), F32),
    )(a1, b1, a2, b2)


def _matmul_tn(a, b, name, out_dtype=F32):
    l, m = a.shape
    _, n = b.shape
    tm, tn, tl = _pick(m, 1408, 128), _pick(n, 1408, 128), _pick(l, 512, 16)
    nl = l // tl

    def body(a_ref, b_ref, o_ref, acc_ref):
        ll = pl.program_id(2)

        @pl.when(ll == 0)
        def _():
            acc_ref[...] = jnp.zeros_like(acc_ref)

        acc_ref[...] += _dot_tn(a_ref[...].astype(MXU_DTYPE), b_ref[...].astype(MXU_DTYPE))

        @pl.when(ll == nl - 1)
        def _():
            o_ref[...] = acc_ref[...].astype(o_ref.dtype)

    return pl.pallas_call(
        body, name=name, grid=(m // tm, n // tn, nl),
        in_specs=[pl.BlockSpec((tl, tm), lambda i, j, ll: (ll, i)), pl.BlockSpec((tl, tn), lambda i, j, ll: (ll, j))],
        out_specs=pl.BlockSpec((tm, tn), lambda i, j, ll: (i, j)),
        out_shape=jax.ShapeDtypeStruct((m, n), out_dtype),
        scratch_shapes=[pltpu.VMEM((tm, tn), F32)],
    )(a, b)


def _row_specs(l, d, n_rows, n_vecs):
    tl = _pick(l, 512, 16)
    row = pl.BlockSpec((tl, d), lambda i: (i, 0))
    vec = pl.BlockSpec((1, d), lambda i: (0, 0))
    return tl, row, vec, [row] * n_rows + [vec] * n_vecs


def _rms_mod(x, nw, sc, sh, name):
    l, d = x.shape
    tl, row, _, in_specs = _row_specs(l, d, 1, 3)

    def body(x_ref, nw_ref, sc_ref, sh_ref, h_ref):
        xv = x_ref[...]
        r = lax.rsqrt(jnp.mean(xv * xv, axis=-1, keepdims=True) + EPS)
        h_ref[...] = (((xv * r) * nw_ref[...]) * (1.0 + sc_ref[...]) + sh_ref[...]).astype(h_ref.dtype)

    return pl.pallas_call(body, name=name, grid=(l // tl,), in_specs=in_specs, out_specs=row,
                          out_shape=jax.ShapeDtypeStruct((l, d), MXU_DTYPE))(x, nw, sc, sh)


def _residual_rms_mod(x, mix, g, nw, sc, sh, name):
    l, d = x.shape
    tl, row, _, in_specs = _row_specs(l, d, 2, 4)

    def body(x_ref, mix_ref, g_ref, nw_ref, sc_ref, sh_ref, x1_ref, h_ref):
        xv = x_ref[...] + g_ref[...] * mix_ref[...]
        x1_ref[...] = xv
        r = lax.rsqrt(jnp.mean(xv * xv, axis=-1, keepdims=True) + EPS)
        h_ref[...] = (((xv * r) * nw_ref[...]) * (1.0 + sc_ref[...]) + sh_ref[...]).astype(h_ref.dtype)

    return pl.pallas_call(body, name=name, grid=(l // tl,), in_specs=in_specs, out_specs=[row, row],
                          out_shape=[jax.ShapeDtypeStruct((l, d), F32), jax.ShapeDtypeStruct((l, d), MXU_DTYPE)],
                          )(x, mix, g, nw, sc, sh)


def _norm_bwd(x, dh, dres, nw, sc, name, gated=None):
    l, d = x.shape
    tl, row, vec, in_specs = _row_specs(l, d, 3, 2)
    stats = pl.BlockSpec((8, d), lambda i: (0, 0))

    def body(x_ref, dh_ref, dres_ref, nw_ref, sc_ref, *rest):
        dx_ref, st_ref = rest[-2:] if gated is None else (rest[2], rest[4])
        xv, dh_v = x_ref[...], dh_ref[...]
        r = lax.rsqrt(jnp.mean(xv * xv, axis=-1, keepdims=True) + EPS)
        xn = xv * r
        dxn = dh_v * (nw_ref[...] * (1.0 + sc_ref[...]))
        dx = dres_ref[...] + r * (dxn - xn * jnp.mean(dxn * xn, axis=-1, keepdims=True))
        dx_ref[...] = dx

        @pl.when(pl.program_id(0) == 0)
        def _():
            st_ref[...] = jnp.zeros_like(st_ref)

        dhx = dh_v * xn
        st_ref[0:1, :] += _colsum(dhx * nw_ref[...])
        st_ref[1:2, :] += _colsum(dh_v)
        st_ref[2:3, :] += _colsum(dhx * (1.0 + sc_ref[...]))
        if gated is not None:
            val_ref, g_ref, dval_ref = rest[0], rest[1], rest[3]
            dval_ref[...] = (g_ref[...] * dx).astype(dval_ref.dtype)
            st_ref[3:4, :] += _colsum(dx * val_ref[...])

    if gated is None:
        return pl.pallas_call(body, name=name, grid=(l // tl,), in_specs=in_specs, out_specs=[row, stats],
                              out_shape=[jax.ShapeDtypeStruct((l, d), F32), jax.ShapeDtypeStruct((8, d), F32)],
                              )(x, dh, dres, nw, sc)
    return pl.pallas_call(body, name=name, grid=(l // tl,), in_specs=in_specs + [row, vec], out_specs=[row, row, stats],
                          out_shape=[jax.ShapeDtypeStruct((l, d), F32), jax.ShapeDtypeStruct((l, d), MXU_DTYPE),
                                     jax.ShapeDtypeStruct((8, d), F32)],
                          )(x, dh, dres, nw, sc, *gated)


def _gate_bwd(dres, val, g, name):
    l, d = dres.shape
    tl, row, _, in_specs = _row_specs(l, d, 2, 1)

    def body(dres_ref, val_ref, g_ref, dval_ref, st_ref):
        dr = dres_ref[...]
        dval_ref[...] = (g_ref[...] * dr).astype(dval_ref.dtype)

        @pl.when(pl.program_id(0) == 0)
        def _():
            st_ref[...] = jnp.zeros_like(st_ref)

        st_ref[0:1, :] += _colsum(dr * val_ref[...])

    return pl.pallas_call(body, name=name, grid=(l // tl,), in_specs=in_specs,
                          out_specs=[row, pl.BlockSpec((8, d), lambda i: (0, 0))],
                          out_shape=[jax.ShapeDtypeStruct((l, d), MXU_DTYPE), jax.ShapeDtypeStruct((8, d), F32)],
                          )(dres, val, g)


def _loss_head(x1, f, g, target, name):
    l, d = x1.shape
    tl, row, _, _ = _row_specs(l, d, 0, 0)
    vec = pl.BlockSpec((1, d), lambda i: (0, 0))

    def body(x1_ref, f_ref, g_ref, t_ref, dy_ref, df_ref, st_ref, loss_ref):
        fv = f_ref[...]
        e = x1_ref[...] + g_ref[...] * fv - t_ref[...]
        dy = e * (1.0 / d)
        dy_ref[...] = dy
        df_ref[...] = (g_ref[...] * dy).astype(df_ref.dtype)

        @pl.when(pl.program_id(0) == 0)
        def _():
            loss_ref[...] = jnp.zeros_like(loss_ref)
            st_ref[...] = jnp.zeros_like(st_ref)

        st_ref[0:1, :] += _colsum(dy * fv)
        s = jnp.sum(jnp.sum(e * e, axis=1, keepdims=True), axis=0, keepdims=True)
        loss_ref[...] += (0.5 / d) * s

    return pl.pallas_call(body, name=name, grid=(l // tl,), in_specs=[row, row, vec, row],
                          out_specs=[row, row, pl.BlockSpec((8, d), lambda i: (0, 0)), pl.BlockSpec((8, 128), lambda i: (0, 0))],
                          out_shape=[jax.ShapeDtypeStruct((l, d), F32), jax.ShapeDtypeStruct((l, d), MXU_DTYPE),
                                     jax.ShapeDtypeStruct((8, d), F32), jax.ShapeDtypeStruct((8, 128), F32)],
                          )(x1, f, g, target)


def _ffn_in(h, w_gate_t, w_up_t, name):
    l, k = h.shape
    f = w_gate_t.shape[0]
    tl, tf = _pick(l, 512, 16), _pick(f, 1408, 128)
    h_spec = pl.BlockSpec((tl, k), lambda i, j: (i, 0))
    w_spec = pl.BlockSpec((tf, k), lambda i, j: (j, 0))
    spec = pl.BlockSpec((tl, tf), lambda i, j: (i, j))

    def body(h_ref, wg_ref, wu_ref, g_ref, u_ref, a_ref):
        hv = h_ref[...].astype(MXU_DTYPE)
        gv = _dot_nt(hv, wg_ref[...].astype(MXU_DTYPE))
        uv = _dot_nt(hv, wu_ref[...].astype(MXU_DTYPE))
        g_ref[...] = gv
        u_ref[...] = uv
        a_ref[...] = (gv * _sigmoid(gv) * uv).astype(a_ref.dtype)

    return pl.pallas_call(body, name=name, grid=(l // tl, f // tf), in_specs=[h_spec, w_spec, w_spec], out_specs=[spec, spec, spec],
                          out_shape=[jax.ShapeDtypeStruct((l, f), F32)] * 2 + [jax.ShapeDtypeStruct((l, f), MXU_DTYPE)],
                          )(h, w_gate_t, w_up_t)


def _swiglu_bwd(gate, up, da, name):
    l, f = gate.shape
    tl, tf = _pick(l, 512, 16), _pick(f, 1408, 128)
    spec = pl.BlockSpec((tl, tf), lambda i, j: (i, j))

    def body(g_ref, u_ref, da_ref, dg_ref, du_ref):
        gv, dav = g_ref[...], da_ref[...]
        s = _sigmoid(gv)
        dg_ref[...] = (dav * u_ref[...] * (s * (1.0 + gv * (1.0 - s)))).astype(dg_ref.dtype)
        du_ref[...] = (dav * (gv * s)).astype(du_ref.dtype)

    return pl.pallas_call(body, name=name, grid=(l // tl, f // tf), in_specs=[spec, spec, spec], out_specs=[spec, spec],
                          out_shape=[jax.ShapeDtypeStruct((l, f), MXU_DTYPE)] * 2)(gate, up, da)


def _conv_tile(l):
    return _pick(l, 512, 16)


def _conv_pre(buf, w_ref, b_ref, tl):
    acc = b_ref[...] + w_ref[3:4, :] * buf[HALO:HALO + tl, :]
    for k in range(CONV_WIDTH - 1):
        s = HALO - (CONV_WIDTH - 1) + k
        acc = acc + w_ref[k:k + 1, :] * buf[s:s + tl, :]
    return acc


def _fill_past(buf, u_ref, halo_ref, tl):
    i = pl.program_id(0)

    @pl.when(i == 0)
    def _():
        buf[0:HALO, :] = jnp.zeros((HALO, buf.shape[1]), F32)

    @pl.when(i > 0)
    def _():
        buf[0:HALO, :] = halo_ref[...]

    buf[HALO:HALO + tl, :] = u_ref[...]


def _conv_silu(u, w, b, name):
    l, ch = u.shape[0], w.shape[1]
    tl = _conv_tile(l)
    per = tl // HALO
    cur = pl.BlockSpec((tl, ch), lambda i: (i, 0))
    past = pl.BlockSpec((HALO, ch), lambda i: (jnp.maximum(i * per - 1, 0), 0))

    def body(u_ref, halo_ref, w_ref, b_ref, o_ref, buf):
        _fill_past(buf, u_ref, halo_ref, tl)
        pre = _conv_pre(buf, w_ref, b_ref, tl)
        o_ref[...] = pre * _sigmoid(pre)

    return pl.pallas_call(body, name=name, grid=(l // tl,),
                          in_specs=[cur, past, pl.BlockSpec((CONV_WIDTH, ch), lambda i: (0, 0)), pl.BlockSpec((1, ch), lambda i: (0, 0))],
                          out_specs=cur, out_shape=jax.ShapeDtypeStruct((l, ch), F32),
                          scratch_shapes=[pltpu.VMEM((tl + HALO, ch), F32)])(u, u, w, b)


def _conv_silu_bwd_pre(u, dxc, w, b, name):
    l, ch = u.shape[0], w.shape[1]
    tl = _conv_tile(l)
    per = tl // HALO
    cur = pl.BlockSpec((tl, ch), lambda i: (i, 0))
    past = pl.BlockSpec((HALO, ch), lambda i: (jnp.maximum(i * per - 1, 0), 0))

    def body(u_ref, halo_ref, d_ref, w_ref, b_ref, dpre_ref, st_ref, buf):
        _fill_past(buf, u_ref, halo_ref, tl)
        pre = _conv_pre(buf, w_ref, b_ref, tl)
        s = _sigmoid(pre)
        dpre = d_ref[...] * (s * (1.0 + pre * (1.0 - s)))
        dpre_ref[...] = dpre

        @pl.when(pl.program_id(0) == 0)
        def _():
            st_ref[...] = jnp.zeros_like(st_ref)

        for k in range(CONV_WIDTH):
            s0 = HALO - (CONV_WIDTH - 1) + k
            st_ref[k:k + 1, :] += _colsum(dpre * buf[s0:s0 + tl, :])
        st_ref[CONV_WIDTH:CONV_WIDTH + 1, :] += _colsum(dpre)

    return pl.pallas_call(body, name=name, grid=(l // tl,),
                          in_specs=[cur, past, cur, pl.BlockSpec((CONV_WIDTH, ch), lambda i: (0, 0)), pl.BlockSpec((1, ch), lambda i: (0, 0))],
                          out_specs=[cur, pl.BlockSpec((8, ch), lambda i: (0, 0))],
                          out_shape=[jax.ShapeDtypeStruct((l, ch), F32), jax.ShapeDtypeStruct((8, ch), F32)],
                          scratch_shapes=[pltpu.VMEM((tl + HALO, ch), F32)])(u, u, dxc, w, b)


def _conv_bwd_input(dpre, w, name):
    l, ch = dpre.shape
    tl = _conv_tile(l)
    per = tl // HALO
    nt = l // tl
    cur = pl.BlockSpec((tl, ch), lambda i: (i, 0))
    nxt = pl.BlockSpec((HALO, ch), lambda i: (jnp.minimum((i + 1) * per, l // HALO - 1), 0))

    def body(d_ref, halo_ref, w_ref, du_ref, buf):
        i = pl.program_id(0)
        buf[0:tl, :] = d_ref[...]

        @pl.when(i == nt - 1)
        def _():
            buf[tl:tl + HALO, :] = jnp.zeros((HALO, ch), F32)

        @pl.when(i < nt - 1)
        def _():
            buf[tl:tl + HALO, :] = halo_ref[...]

        acc = w_ref[3:4, :] * buf[0:tl, :]
        for k in range(CONV_WIDTH - 1):
            s = CONV_WIDTH - 1 - k
            acc = acc + w_ref[k:k + 1, :] * buf[s:s + tl, :]
        du_ref[...] = acc

    return pl.pallas_call(body, name=name, grid=(nt,),
                          in_specs=[cur, nxt, pl.BlockSpec((CONV_WIDTH, ch), lambda i: (0, 0))],
                          out_specs=cur, out_shape=jax.ShapeDtypeStruct((l, ch), F32),
                          scratch_shapes=[pltpu.VMEM((tl + HALO, ch), F32)])(dpre, dpre, w)


HEAD_TILE = 128
D_HEAD_TILES = N_HEADS * HEAD_TILE


def _chunk_iota():
    r = lax.broadcasted_iota(jnp.int32, (CHUNK, CHUNK), 0)
    c = lax.broadcasted_iota(jnp.int32, (CHUNK, CHUNK), 1)
    return r, c


def _split3_dot(v, ones_b):
    hi = v.astype(BF16)
    r1 = v - hi.astype(F32)
    mid = r1.astype(BF16)
    lo = (r1 - mid.astype(F32)).astype(BF16)
    return _dot(hi, ones_b) + _dot(mid, ones_b) + _dot(lo, ones_b)


def _split3_dot_nt(v, ones_b):
    hi = v.astype(BF16)
    r1 = v - hi.astype(F32)
    mid = r1.astype(BF16)
    lo = (r1 - mid.astype(F32)).astype(BF16)
    return _dot_nt(hi, ones_b) + _dot_nt(mid, ones_b) + _dot_nt(lo, ones_b)


def _ssd_decays(dtx_ref, dty_ref, dtbx_ref, dtby_ref, alx_ref, aly_ref, r, c):
    tri = (r >= c).astype(BF16)
    dt_x = _softplus(dtx_ref[...] + dtbx_ref[...])
    a_x = -jnp.exp(alx_ref[...])
    adt_x = a_x * dt_x
    acs_x = _tri_dot(tri, adt_x)
    atot_x = _colsum(adt_x)
    acs_y = _tri_dot(tri, -jnp.exp(aly_ref[...]) * _softplus(dty_ref[...] + dtby_ref[...]))
    return dt_x, a_x, acs_y, jnp.exp(acs_x), jnp.exp(atot_x - acs_x), jnp.exp(atot_x)


def _head_decay(acs_y, e, r, c):
    col = acs_y[:, e * HEAD_TILE:(e + 1) * HEAD_TILE]
    return jnp.where(r >= c, jnp.exp(col - col.T), 0.0)


def _half_masks():
    lane = lax.broadcasted_iota(jnp.int32, (CHUNK, 2 * HEAD_DIM), 1)
    return lane < HEAD_DIM, lane >= HEAD_DIM


def _ssd_specs(nc, reverse):
    def at(i):
        return nc - 1 - i if reverse else i
    xc = pl.BlockSpec((CHUNK, D_CONV), lambda i: (at(i), 0))
    wide = pl.BlockSpec((CHUNK, D_SSD), lambda i: (at(i), 0))
    wide_y = pl.BlockSpec((CHUNK, D_HEAD_TILES), lambda i: (at(i), 0))
    vec = pl.BlockSpec((1, D_SSD), lambda i: (0, 0))
    vec_y = pl.BlockSpec((1, D_HEAD_TILES), lambda i: (0, 0))
    state = pl.BlockSpec((1, SSD_STATE, D_SSD), lambda i: (at(i), 0, 0))
    return xc, wide, wide_y, vec, vec_y, state


def _ssd_fwd(xc, dtr_x, dtr_y, dtb_x, dtb_y, al_x, al_y, name):
    l = xc.shape[0]
    nc = l // CHUNK
    xc_s, wide_s, wide_y_s, vec_s, vec_y_s, state_s = _ssd_specs(nc, False)
    pairs_per_group = GROUP_WIDTH // (2 * HEAD_DIM)

    def body(xc_ref, dtx_ref, dty_ref, dtbx_ref, dtby_ref, alx_ref, aly_ref, y_ref, sp_ref, state):
        @pl.when(pl.program_id(0) == 0)
        def _():
            state[...] = jnp.zeros_like(state)

        r, c = _chunk_iota()
        halves = _half_masks()
        dt_x, _, acs_y, ea_x, ds_x, eatot_x = _ssd_decays(dtx_ref, dty_ref, dtbx_ref, dtby_ref, alx_ref, aly_ref, r, c)
        xg = xc_ref[:, 0:D_SSD] * dt_x
        sp_ref[0] = state[...]
        for g in range(SSD_GROUPS):
            lanes = slice(g * GROUP_WIDTH, (g + 1) * GROUP_WIDTH)
            bb = xc_ref[:, D_SSD + g * SSD_STATE:D_SSD + (g + 1) * SSD_STATE].astype(MXU_DTYPE)
            cb = xc_ref[:, D_SSD + (SSD_GROUPS + g) * SSD_STATE:D_SSD + (SSD_GROUPS + g + 1) * SSD_STATE].astype(MXU_DTYPE)
            scores = _dot_nt(cb, bb)
            sg = state[:, lanes]
            ys = []
            for j in range(g * pairs_per_group, (g + 1) * pairs_per_group):
                xg_pair = xg[:, j * 2 * HEAD_DIM:(j + 1) * 2 * HEAD_DIM]
                acc = jnp.zeros((CHUNK, 2 * HEAD_DIM), F32)
                for half in range(2):
                    m = (scores * _head_decay(acs_y, 2 * j + half, r, c)).astype(MXU_DTYPE)
                    acc = acc + _dot(m, jnp.where(halves[half], xg_pair, 0.0).astype(MXU_DTYPE))
                ys.append(acc)
            y_ref[:, lanes] = jnp.concatenate(ys, axis=1) + _dot(cb, sg.astype(MXU_DTYPE)) * ea_x[:, lanes]
            state[:, lanes] = sg * eatot_x[:, lanes] + _dot_tn(bb, (xg[:, lanes] * ds_x[:, lanes]).astype(MXU_DTYPE))

    return pl.pallas_call(
        body, name=name, grid=(nc,),
        in_specs=[xc_s, wide_s, wide_y_s, vec_s, vec_y_s, vec_s, vec_y_s],
        out_specs=[wide_s, state_s],
        out_shape=[jax.ShapeDtypeStruct((l, D_SSD), F32), jax.ShapeDtypeStruct((nc, SSD_STATE, D_SSD), F32)],
        scratch_shapes=[pltpu.VMEM((SSD_STATE, D_SSD), F32)],
    )(xc, dtr_x, dtr_y, dtb_x, dtb_y, al_x, al_y)


def _ssd_bwd(xc, dtr_x, dtr_y, dtb_x, dtb_y, al_x, al_y, dy, states, head_of_lanes, name):
    l = xc.shape[0]
    nc = l // CHUNK
    xc_s, wide_s, wide_y_s, vec_s, vec_y_s, state_s = _ssd_specs(nc, True)
    pairs_per_group = GROUP_WIDTH // (2 * HEAD_DIM)

    def body(xc_ref, dtx_ref, dty_ref, dtbx_ref, dtby_ref, alx_ref, aly_ref, dy_ref, sp_ref, ones_ref,
             dxc_ref, ddtr_ref, st_ref, hs_ref, dstate):
        @pl.when(pl.program_id(0) == 0)
        def _():
            dstate[...] = jnp.zeros_like(dstate)
            st_ref[...] = jnp.zeros_like(st_ref)

        r, c = _chunk_iota()
        halves = _half_masks()
        spread = ones_ref[...]

        def head_sums(v):
            return _split3_dot(_split3_dot_nt(v, spread), spread)

        dt_x, a_x, acs_y, ea_x, ds_x, eatot_x = _ssd_decays(dtx_ref, dty_ref, dtbx_ref, dtby_ref, alx_ref, aly_ref, r, c)
        xs = xc_ref[:, 0:D_SSD]
        xg = xs * dt_x
        gy = dy_ref[...]
        s_prev = sp_ref[0]
        gea = gy * ea_x
        xds = xg * ds_x
        ds_old = dstate[...]
        later2 = jnp.concatenate([(r <= c).astype(BF16)] * 2, axis=1)
        dxg_parts, state_term, yoff_parts, dadt_parts = [], [], [], []
        for g in range(SSD_GROUPS):
            lanes = slice(g * GROUP_WIDTH, (g + 1) * GROUP_WIDTH)
            b_lo = D_SSD + g * SSD_STATE
            c_lo = D_SSD + (SSD_GROUPS + g) * SSD_STATE
            bb = xc_ref[:, b_lo:b_lo + SSD_STATE].astype(MXU_DTYPE)
            cb = xc_ref[:, c_lo:c_lo + SSD_STATE].astype(MXU_DTYPE)
            scores = _dot_nt(cb, bb)
            dsg = ds_old[:, lanes].astype(MXU_DTYPE)
            gea_b = gea[:, lanes].astype(MXU_DTYPE)
            xds_b = xds[:, lanes].astype(MXU_DTYPE)
            dxg_state = _dot(bb, dsg) * ds_x[:, lanes]
            dc = _dot_nt(gea_b, s_prev[:, lanes].astype(MXU_DTYPE))
            db = _dot_nt(xds_b, dsg)
            dscores = jnp.zeros((CHUNK, CHUNK), F32)
            diag = []
            for j in range(g * pairs_per_group, (g + 1) * pairs_per_group):
                pair = slice(j * 2 * HEAD_DIM, (j + 1) * 2 * HEAD_DIM)
                xg_pair = xg[:, pair].astype(MXU_DTYPE)
                acc = jnp.zeros((CHUNK, 2 * HEAD_DIM), F32)
                cols = []
                for half in range(2):
                    decay = _head_decay(acs_y, 2 * j + half, r, c)
                    g_e = jnp.where(halves[half], gy[:, pair], 0.0).astype(MXU_DTYPE)
                    acc = acc + _dot_tn((scores * decay).astype(MXU_DTYPE), g_e)
                    dm = _dot_nt(g_e, xg_pair) * decay
                    dscores = dscores + dm
                    wq = dm * scores
                    hi = wq.astype(BF16)
                    lo = (wq - hi.astype(F32)).astype(BF16)
                    later = _dot(later2, jnp.concatenate([hi, lo], axis=0))
                    cols.append(jnp.sum(jnp.where(c < r, later, 0.0), axis=1, keepdims=True))
                diag.append(acc)
                dadt_parts.append(jnp.where(halves[0], cols[0], cols[1]))
            dsc_b = dscores.astype(MXU_DTYPE)
            dc = dc + _dot(dsc_b, bb)
            db = db + _dot_tn(dsc_b, cb)
            dxc_ref[:, b_lo:b_lo + SSD_STATE] = db
            dxc_ref[:, c_lo:c_lo + SSD_STATE] = dc
            dxg_parts.append(jnp.concatenate(diag, axis=1) + dxg_state)
            state_term.append(dxg_state)
            yoff_parts.append(_dot(cb, s_prev[:, lanes].astype(MXU_DTYPE)) * ea_x[:, lanes])
            dstate[:, lanes] = ds_old[:, lanes] * eatot_x[:, lanes] + _dot_tn(cb, gea_b)
        dxg = jnp.concatenate(dxg_parts, axis=1)
        dxc_ref[:, 0:D_SSD] = dxg * dt_x
        through_out = _tri_dot((r <= c).astype(BF16), gy * jnp.concatenate(yoff_parts, axis=1))
        through_in = _tri_dot((c < r).astype(BF16), xg * jnp.concatenate(state_term, axis=1))
        carried = jnp.broadcast_to(_colsum(ds_old * s_prev) * eatot_x, (8, D_SSD))
        dadt = (jnp.concatenate(dadt_parts, axis=1) + head_sums(through_out + through_in)
                + jnp.max(head_sums(carried), axis=0, keepdims=True))
        ddt = a_x * dadt + head_sums(dxg * xs)
        draw = ddt * _sigmoid(dtx_ref[...] + dtbx_ref[...])
        ddtr_ref[...] = draw
        st_ref[0:1, :] += _colsum(dt_x * dadt) * a_x
        st_ref[1:2, :] += _colsum(draw)
        st_ref[2:3, :] += _colsum(gy * xs)

        @pl.when(pl.program_id(0) == nc - 1)
        def _():
            hs_ref[...] = head_sums(st_ref[...])

    stats = pl.BlockSpec((8, D_SSD), lambda i: (0, 0))
    return pl.pallas_call(
        body, name=name, grid=(nc,),
        in_specs=[xc_s, wide_s, wide_y_s, vec_s, vec_y_s, vec_s, vec_y_s, wide_s, state_s,
                  pl.BlockSpec((HEAD_TILE, D_SSD), lambda i: (0, 0))],
        out_specs=[xc_s, wide_s, stats, stats],
        out_shape=[jax.ShapeDtypeStruct((l, D_CONV), F32), jax.ShapeDtypeStruct((l, D_SSD), F32),
                   jax.ShapeDtypeStruct((8, D_SSD), F32), jax.ShapeDtypeStruct((8, D_SSD), F32)],
        scratch_shapes=[pltpu.VMEM((SSD_STATE, D_SSD), F32)],
    )(xc, dtr_x, dtr_y, dtb_x, dtb_y, al_x, al_y, dy, states, head_of_lanes)


def _ssd_gate_norm(ycore, xc, z, dskip_x, norm_w, name):
    l = ycore.shape[0]
    tl = _pick(l, 512, 16)
    row = pl.BlockSpec((tl, D_SSD), lambda i: (i, 0))
    vec = pl.BlockSpec((1, D_SSD), lambda i: (0, 0))

    def body(y_ref, xs_ref, z_ref, dk_ref, nw_ref, o_ref):
        zv = z_ref[...]
        yv = (y_ref[...] + dk_ref[...] * xs_ref[...]) * (zv * _sigmoid(zv))
        for g in range(SSD_GROUPS):
            lanes = slice(g * GROUP_WIDTH, (g + 1) * GROUP_WIDTH)
            yg = yv[:, lanes]
            rg = lax.rsqrt(jnp.mean(yg * yg, axis=-1, keepdims=True) + EPS)
            o_ref[:, lanes] = (yg * rg * nw_ref[:, lanes]).astype(o_ref.dtype)

    z_spec = pl.BlockSpec((tl, D_SSD), lambda i: (i, O_Z // D_SSD))
    return pl.pallas_call(body, name=name, grid=(l // tl,), in_specs=[row, row, z_spec, vec, vec], out_specs=row,
                          out_shape=jax.ShapeDtypeStruct((l, D_SSD), MXU_DTYPE))(ycore, xc, z, dskip_x, norm_w)


def _ssd_gate_norm_bwd(dout, ycore, xc, z, dskip_x, norm_w, name):
    l = ycore.shape[0]
    tl = _pick(l, 512, 16)
    nt = l // tl
    row = pl.BlockSpec((tl, D_SSD), lambda i: (i, 0))
    vec = pl.BlockSpec((1, D_SSD), lambda i: (0, 0))

    def body(do_ref, y_ref, xs_ref, z_ref, dk_ref, nw_ref, dyc_ref, dxs_ref, dz_ref, st_ref):
        @pl.when(pl.program_id(0) == 0)
        def _():
            st_ref[...] = jnp.zeros_like(st_ref)

        zv, xs = z_ref[...], xs_ref[...]
        s = _sigmoid(zv)
        gz = zv * s
        yc = y_ref[...] + dk_ref[...] * xs
        yv = yc * gz
        dov = do_ref[...]
        dnw, dyv = [], []
        for g in range(SSD_GROUPS):
            lanes = slice(g * GROUP_WIDTH, (g + 1) * GROUP_WIDTH)
            yg = yv[:, lanes]
            rg = lax.rsqrt(jnp.mean(yg * yg, axis=-1, keepdims=True) + EPS)
            yn = yg * rg
            dnw.append(_colsum(dov[:, lanes] * yn))
            dyn = dov[:, lanes] * nw_ref[:, lanes]
            dyv.append(rg * (dyn - yn * jnp.mean(dyn * yn, axis=-1, keepdims=True)))
        dy = jnp.concatenate(dyv, axis=1)
        dyc = dy * gz
        dyc_ref[...] = dyc
        dxs_ref[...] = dyc * dk_ref[...]
        dz_ref[...] = dy * yc * (s * (1.0 + zv * (1.0 - s)))
        st_ref[0:1, :] += jnp.concatenate(dnw, axis=1)

    return pl.pallas_call(
        body, name=name, grid=(nt,), in_specs=[row, row, row, pl.BlockSpec((tl, D_SSD), lambda i: (i, O_Z // D_SSD)), vec, vec],
        out_specs=[row, row, row, pl.BlockSpec((8, D_SSD), lambda i: (0, 0))],
        out_shape=[jax.ShapeDtypeStruct((l, D_SSD), F32)] * 3 + [jax.ShapeDtypeStruct((8, D_SSD), F32)],
    )(dout, ycore, xc, z, dskip_x, norm_w)


PAIR = 2 * HEAD_DIM
N_PAIRS = N_HEADS // 2
Q_TILE0 = O_Q // PAIR
SB_KEYS = 1024
SB_PAIRS = 2
SB_PAIRS_FWD = 4
SB_SCALE = HEAD_DIM ** -0.5


def _pair_sum(v, lo):
    s_lo = jnp.sum(jnp.where(lo, v, 0.0), axis=-1, keepdims=True)
    s_hi = jnp.sum(jnp.where(lo, 0.0, v), axis=-1, keepdims=True)
    return jnp.where(lo, s_lo, s_hi)


def _qkv_prep(proj, qw2, kw2, name):
    l = proj.shape[0]
    tl = _pick(l, 1024, 16)
    out = pl.BlockSpec((tl, PAIR), lambda i, j: (i, j))
    vec = pl.BlockSpec((1, PAIR), lambda i, j: (0, 0))

    def at(which):
        return pl.BlockSpec((tl, PAIR), lambda i, j: (i, Q_TILE0 + which * N_PAIRS + j))

    def body(q_ref, k_ref, v_ref, qw_ref, kw_ref, qn_ref, kn_ref, vb_ref):
        lo = lax.broadcasted_iota(jnp.int32, (tl, PAIR), 1) < HEAD_DIM
        for t_ref, w_ref, o_ref, scale in ((q_ref, qw_ref, qn_ref, SB_SCALE), (k_ref, kw_ref, kn_ref, 1.0)):
            tv = t_ref[...]
            r = lax.rsqrt(_pair_sum(tv * tv, lo) * (1.0 / HEAD_DIM) + EPS)
            o_ref[...] = ((tv * r) * w_ref[...] * scale).astype(o_ref.dtype)
        vb_ref[...] = v_ref[...].astype(vb_ref.dtype)

    return pl.pallas_call(body, name=name, grid=(l // tl, N_PAIRS), in_specs=[at(0), at(1), at(2), vec, vec],
                          out_specs=[out, out, out], out_shape=[jax.ShapeDtypeStruct((l, D_SB), MXU_DTYPE)] * 3,
                          )(proj, proj, proj, qw2, kw2)


def _qk_norm_bwd(proj, dqn, dkn, qw2, kw2, name):
    l = proj.shape[0]
    tl = _pick(l, 1024, 16)
    out = pl.BlockSpec((tl, PAIR), lambda i, j: (i, j))
    vec = pl.BlockSpec((1, PAIR), lambda i, j: (0, 0))
    stats = pl.BlockSpec((8, PAIR), lambda i, j: (0, 0))

    def at(which):
        return pl.BlockSpec((tl, PAIR), lambda i, j: (i, Q_TILE0 + which * N_PAIRS + j))

    def body(q_ref, k_ref, dqn_ref, dkn_ref, qw_ref, kw_ref, dq_ref, dk_ref, stq_ref, stk_ref):
        @pl.when((pl.program_id(0) == 0) & (pl.program_id(1) == 0))
        def _():
            stq_ref[...] = jnp.zeros_like(stq_ref)
            stk_ref[...] = jnp.zeros_like(stk_ref)

        lo = lax.broadcasted_iota(jnp.int32, (tl, PAIR), 1) < HEAD_DIM
        for t_ref, dn_ref, w_ref, d_ref, st_ref, scale in ((q_ref, dqn_ref, qw_ref, dq_ref, stq_ref, SB_SCALE),
                                                           (k_ref, dkn_ref, kw_ref, dk_ref, stk_ref, 1.0)):
            tv = t_ref[...]
            r = lax.rsqrt(_pair_sum(tv * tv, lo) * (1.0 / HEAD_DIM) + EPS)
            tn = tv * r
            dnv = dn_ref[...] * scale
            dtn = dnv * w_ref[...]
            d_ref[...] = r * (dtn - tn * (_pair_sum(dtn * tn, lo) * (1.0 / HEAD_DIM)))
            st_ref[0:1, :] += _colsum(dnv * tn)

    return pl.pallas_call(body, name=name, grid=(l // tl, N_PAIRS), in_specs=[at(0), at(1), out, out, vec, vec],
                          out_specs=[out, out, stats, stats],
                          out_shape=[jax.ShapeDtypeStruct((l, D_SB), F32)] * 2 + [jax.ShapeDtypeStruct((8, PAIR), F32)] * 2,
                          )(proj, proj, dqn, dkn, qw2, kw2)


def _sb_masks():
    r = lax.broadcasted_iota(jnp.int32, (CHUNK, CHUNK), 0)
    c = lax.broadcasted_iota(jnp.int32, (CHUNK, CHUNK), 1)
    return r, c


def _stack2(mask):
    t = mask.astype(BF16)
    return jnp.concatenate([t, t], axis=0)


def _sb_fwd(q, k, v, shards, name):
    l = q.shape[0]
    nq = l // CHUNK
    kt = _pick(l, SB_KEYS, CHUNK)
    sub = kt // CHUNK
    pairs = SB_PAIRS_FWD
    heads = 2 * pairs
    steps = N_PAIRS // pairs
    na = len(shards)
    qblk = pl.BlockSpec((CHUNK, pairs * PAIR), lambda i, j: (j, i))
    full = pl.BlockSpec((l, pairs * PAIR), lambda i, j: (0, i))
    hbm = pl.BlockSpec(memory_space=pl.ANY)

    def body(q_ref, k_ref, v_ref, *rest):
        o_ref, tot_ref = rest[na:na + 2]
        exchange = _DirectExchange("gather", rest[:na], rest[na + 2:2 * na + 2], *rest[2 * na + 2:])
        qb = pl.program_id(1)

        @pl.when((pl.program_id(0) == 0) & (qb == 0))
        def _():
            exchange.start()

        r, c = _sb_masks()
        after2 = _stack2(r > c)
        halves = _half_masks()
        zero = jnp.zeros((CHUNK, PAIR), q_ref.dtype)
        lanes = [slice((a // 2) * PAIR, (a // 2 + 1) * PAIR) for a in range(heads)]
        qm = [jnp.where(halves[a % 2], q_ref[:, lanes[a]], zero) for a in range(heads)]
        last = (qb * CHUNK) // kt

        def tile(t, width, carries, accs, masked):
            rows = pl.ds(pl.multiple_of(t * width, width), width)
            if masked:
                key_minus_query = (lax.broadcasted_iota(jnp.int32, (CHUNK, width), 1)
                                   - lax.broadcasted_iota(jnp.int32, (CHUNK, width), 0))
                keep = key_minus_query < qb * CHUNK - t * width

            def logits(a):
                lg = _dot_nt(qm[a], k_ref[rows, lanes[a]])
                sp = _softplus_logits(lg)
                return lg - sp, (jnp.where(keep, -sp, 0.0) if masked else -sp)

            def sums(a, lr):
                offset, parts = carries[a], [None] * (width // CHUNK)
                for j in reversed(range(width // CHUNK)):
                    piece = lr[:, j * CHUNK:(j + 1) * CHUNK]
                    parts[j] = _split_dot(piece, after2) + offset
                    offset = offset + jnp.sum(piece, axis=1, keepdims=True)
                return jnp.concatenate(parts, axis=1), offset

            def output(a, ls, cs):
                w = jnp.exp(ls + cs)
                if masked:
                    w = jnp.where(keep, w, 0.0)
                return accs[a] + _dot(w.astype(MXU_DTYPE), v_ref[rows, lanes[a]])

            new_carries, new_accs = [None] * heads, [None] * heads
            ls, lr = logits(0)
            for a in range(heads):
                cs, new_carries[a] = sums(a, lr)
                if a + 1 < heads:
                    ls_next, lr = logits(a + 1)
                new_accs[a] = output(a, ls, cs)
                ls = ls_next
            return tuple(new_carries), tuple(new_accs)

        carries = tuple(jnp.zeros((CHUNK, 1), F32) for _ in range(heads))
        accs = tuple(jnp.zeros((CHUNK, PAIR), F32) for _ in range(heads))
        if sub % 2 == 0:
            half = kt // 2
            carries, accs = lax.cond((qb * CHUNK) % kt >= half,
                                     lambda cr, ac: tile(2 * last + 1, half, cr, ac, True), lambda cr, ac: (cr, ac), carries, accs)
            carries, accs = tile(2 * last, half, carries, accs, True)
        else:
            carries, accs = tile(last, kt, carries, accs, True)
        carries, accs = lax.fori_loop(1, last + 1, lambda i, st: tile(last - i, kt, st[0], st[1], False), (carries, accs))
        for p in range(pairs):
            o_ref[:, lanes[2 * p]] = jnp.where(halves[0], accs[2 * p], accs[2 * p + 1])
            tot_ref[:, lanes[2 * p]] = jnp.where(halves[0], carries[2 * p], carries[2 * p + 1])

        @pl.when((pl.program_id(0) == steps - 1) & (qb == nq - 1))
        def _():
            exchange.wait()

    return pl.pallas_call(
        body, name=name, grid=(steps, nq), in_specs=[qblk, full, full] + [hbm] * na,
        out_specs=[qblk, qblk] + [hbm] * na,
        out_shape=[jax.ShapeDtypeStruct((l, D_SB), F32), jax.ShapeDtypeStruct((l, D_SB), F32)]
        + [jax.ShapeDtypeStruct((N_DEV,) + s.shape, s.dtype) for s in shards],
        scratch_shapes=_DirectExchange.scratch(na),
    )(q, k, v, *shards)


def _sb_bwd(q, k, v, dycat, tot, blocks, name):
    l = q.shape[0]
    nq = l // CHUNK
    kt = _pick(l, SB_KEYS, CHUNK)
    sub = kt // CHUNK
    heads = 2 * SB_PAIRS
    steps = N_PAIRS // SB_PAIRS
    na = len(blocks)
    width_all = SB_PAIRS * PAIR
    qblk = pl.BlockSpec((CHUNK, width_all), lambda i, j: (j, i))
    doblk = pl.BlockSpec((CHUNK, width_all), lambda i, j: (j, D_SSD // width_all + i))
    full = pl.BlockSpec((l, width_all), lambda i, j: (0, i))
    hbm = pl.BlockSpec(memory_space=pl.ANY)

    def body(q_ref, k_ref, v_ref, do_ref, tot_ref, *rest):
        dq_ref, dk_ref, dv_ref = rest[na:na + 3]
        exchange = _DirectExchange("scatter", rest[:na], rest[na + 3:2 * na + 3], *rest[2 * na + 3:])
        qb = pl.program_id(1)

        @pl.when((pl.program_id(0) == 0) & (qb == 0))
        def _():
            exchange.start()

        @pl.when(qb == 0)
        def _():
            dk_ref[...] = jnp.zeros_like(dk_ref)
            dv_ref[...] = jnp.zeros_like(dv_ref)

        r, c = _sb_masks()
        upto2 = _stack2(r <= c)
        before2 = _stack2(r < c)
        halves = _half_masks()
        lanes = [slice((a // 2) * PAIR, (a // 2 + 1) * PAIR) for a in range(heads)]
        qm = [jnp.where(halves[a % 2], q_ref[:, lanes[a]], jnp.zeros((CHUNK, PAIR), q_ref.dtype)) for a in range(heads)]
        dom = [jnp.where(halves[a % 2], do_ref[:, lanes[a]], 0.0).astype(MXU_DTYPE) for a in range(heads)]
        total = [jnp.max(jnp.where(halves[a % 2], tot_ref[:, lanes[a]], -jnp.inf), axis=1, keepdims=True) for a in range(heads)]
        last = (qb * CHUNK) // kt

        def prefix(values, tri2, offset):
            parts = []
            for j in range(values.shape[1] // CHUNK):
                piece = values[:, j * CHUNK:(j + 1) * CHUNK]
                parts.append(_split_dot(piece, tri2) + offset)
                offset = offset + jnp.sum(piece, axis=1, keepdims=True)
            return jnp.concatenate(parts, axis=1), offset

        def tile(t, width, carry_p, carry_d, dq, masked):
            rows = pl.ds(pl.multiple_of(t * width, width), width)
            if masked:
                key_minus_query = (lax.broadcasted_iota(jnp.int32, (CHUNK, width), 1)
                                   - lax.broadcasted_iota(jnp.int32, (CHUNK, width), 0))
                keep = key_minus_query < qb * CHUNK - t * width

            def log_terms(a):
                lg = _dot_nt(qm[a], k_ref[rows, lanes[a]])
                sp = _softplus_logits(lg)
                return lg - sp, (jnp.where(keep, -sp, 0.0) if masked else -sp)

            def weights(a, ls, lr):
                p_incl, p_next = prefix(lr, upto2, carry_p[a])
                w = jnp.exp(ls + (total[a] - p_incl))
                if masked:
                    w = jnp.where(keep, w, 0.0)
                return w, p_next

            def weight_grads(a, w):
                return _dot_nt(dom[a], v_ref[rows, lanes[a]]) * w

            def gradients(a, ls, w, da):
                d_excl, d_next = prefix(da, before2, carry_d[a])
                sig = jnp.exp(ls)
                dl = da * (1.0 - sig) - d_excl * sig
                if masked:
                    dl = jnp.where(keep, dl, 0.0)
                dl_b = dl.astype(MXU_DTYPE)
                return (d_next, dq[a] + _dot(dl_b, k_ref[rows, lanes[a]]),
                        _dot_tn(w.astype(MXU_DTYPE), dom[a]), _dot_tn(dl_b, qm[a]))

            new_p, new_d, new_dq = [None] * heads, [None] * heads, [None] * heads
            dv_upd, dk_upd = [None] * heads, [None] * heads
            ls, lr = log_terms(0)
            w, new_p[0] = weights(0, ls, lr)
            for a in range(heads):
                if a + 1 < heads:
                    ls_next, lr_next = log_terms(a + 1)
                da = weight_grads(a, w)
                if a + 1 < heads:
                    w_next, new_p[a + 1] = weights(a + 1, ls_next, lr_next)
                new_d[a], new_dq[a], dv_upd[a], dk_upd[a] = gradients(a, ls, w, da)
                if a % 2 == 1:
                    dv_ref[rows, lanes[a]] += dv_upd[a - 1] + dv_upd[a]
                    dk_ref[rows, lanes[a]] += dk_upd[a - 1] + dk_upd[a]
                if a + 1 < heads:
                    ls, w = ls_next, w_next
            return tuple(new_p), tuple(new_d), tuple(new_dq)

        zeros = tuple(jnp.zeros((CHUNK, 1), F32) for _ in range(heads))
        dq0 = tuple(jnp.zeros((CHUNK, PAIR), F32) for _ in range(heads))
        state = lax.fori_loop(0, last, lambda t, st: tile(t, kt, st[0], st[1], st[2], False), (zeros, zeros, dq0))
        if sub % 2 == 0:
            half = kt // 2
            state = tile(2 * last, half, *state, True)
            state = lax.cond((qb * CHUNK) % kt >= half,
                             lambda cp, cd, dq: tile(2 * last + 1, half, cp, cd, dq, True), lambda cp, cd, dq: (cp, cd, dq), *state)
        else:
            state = tile(last, kt, *state, True)
        for p in range(SB_PAIRS):
            dq_ref[:, lanes[2 * p]] = jnp.where(halves[0], state[2][2 * p], state[2][2 * p + 1])

        @pl.when((pl.program_id(0) == steps - 1) & (qb == nq - 1))
        def _():
            exchange.wait()

    return pl.pallas_call(
        body, name=name, grid=(steps, nq),
        in_specs=[qblk, full, full, doblk, qblk] + [hbm] * na,
        out_specs=[qblk, full, full] + [hbm] * na,
        out_shape=[jax.ShapeDtypeStruct((l, D_SB), F32)] * 3 + [jax.ShapeDtypeStruct(b.shape, b.dtype) for b in blocks],
        scratch_shapes=_DirectExchange.scratch(na),
    )(q, k, v, dycat, tot, *blocks)


def _ada_fwd(c_all, w_shard, name):
    def body(c_ref, w_ref, o_ref):
        cv = c_ref[...]
        o_ref[...] = _dot(cv * _sigmoid(cv), w_ref[...], HIGHEST)

    return pl.pallas_call(body, name=name, out_shape=jax.ShapeDtypeStruct((c_all.shape[0], w_shard.shape[1]), F32))(c_all, w_shard)


def _ada_bwd(c_all, dmod_shard, name):
    def body(c_ref, d_ref, o_ref):
        cv = c_ref[...]
        o_ref[...] = _dot_tn(cv * _sigmoid(cv), d_ref[...], HIGHEST)

    return pl.pallas_call(body, name=name, out_shape=jax.ShapeDtypeStruct((c_all.shape[1], dmod_shard.shape[1]), F32))(c_all, dmod_shard)


def _sum_small(parts, name):
    def body(p_ref, o_ref):
        acc = p_ref[0]
        for d in range(1, N_DEV):
            acc = acc + p_ref[d]
        o_ref[...] = acc

    return pl.pallas_call(body, name=name, out_shape=jax.ShapeDtypeStruct(parts.shape[1:], F32))(parts)


def _adamw(w, g, m, v, name):
    rows, cols = w.shape
    tr = _pick(rows, 256, 8)
    spec = pl.BlockSpec((tr, cols), lambda i: (i, 0))
    bc1 = 1.0 - ADAM_B1 ** ADAM_STEP
    bc2 = 1.0 - ADAM_B2 ** ADAM_STEP

    def body(w_ref, g_ref, m_ref, v_ref, d_ref, nm_ref, nv_ref):
        gv = g_ref[...]
        nm = ADAM_B1 * m_ref[...] + (1.0 - ADAM_B1) * gv
        nv = ADAM_B2 * v_ref[...] + (1.0 - ADAM_B2) * (gv * gv)
        nm_ref[...] = nm
        nv_ref[...] = nv
        d_ref[...] = -ADAM_LR * ((nm / bc1) / (jnp.sqrt(nv / bc2) + ADAM_EPS) + ADAM_WD * w_ref[...])

    return pl.pallas_call(body, name=name, grid=(rows // tr,), in_specs=[spec] * 4, out_specs=[spec] * 3,
                          out_shape=[jax.ShapeDtypeStruct((rows, cols), F32)] * 3)(w, g, m, v)


def _pad_w_in_t(w_in_t):
    lo = D_SSD + D_CONV
    return jnp.concatenate([w_in_t[D_SSD:lo + N_HEADS], jnp.zeros((DT_PAD - N_HEADS, w_in_t.shape[1]), w_in_t.dtype),
                            w_in_t[:D_SSD], w_in_t[lo + N_HEADS:]], axis=0)


def _unpad_w_in_t(g):
    return jnp.concatenate([g[O_Z:O_Q], g[:O_DT + N_HEADS], g[O_Q:]], axis=0)


def _local_step(x, target, mod, norm1_w, w_in_tp, conv_w, conv_b, dt_bias, a_log, d_skip, ssd_norm_w, q_norm_w, k_norm_w,
                norm2_w, later_shards):
    sh1, sc1, g1, sh2, sc2, g2 = [mod[:, i * D_MODEL:(i + 1) * D_MODEL] for i in range(N_MOD)]
    qw2, kw2 = jnp.tile(q_norm_w, (1, 2)), jnp.tile(k_norm_w, (1, 2))
    dskip_x = jnp.repeat(d_skip, HEAD_DIM, axis=1)
    dtb_x, dtb_y = jnp.repeat(dt_bias, HEAD_DIM, axis=1), jnp.repeat(dt_bias, HEAD_TILE, axis=1)
    al_x, al_y = jnp.repeat(a_log, HEAD_DIM, axis=1), jnp.repeat(a_log, HEAD_TILE, axis=1)
    head_of_lanes = (jnp.arange(HEAD_TILE, dtype=jnp.int32)[:, None]
                     == jnp.arange(D_SSD, dtype=jnp.int32)[None, :] // HEAD_DIM).astype(BF16)

    h1 = _rms_mod(x, norm1_w, sc1, sh1, "rms_mod1")
    proj = _matmul_nt(h1, w_in_tp, "in_proj")
    dtr = proj[:, O_DT:O_DT + N_HEADS]
    dtr_x, dtr_y = jnp.repeat(dtr, HEAD_DIM, axis=1), jnp.repeat(dtr, HEAD_TILE, axis=1)

    xc = _conv_silu(proj, conv_w, conv_b, "conv_silu")
    ycore, states = _ssd_fwd(xc, dtr_x, dtr_y, dtb_x, dtb_y, al_x, al_y, "ssd_fwd")
    y_ssd = _ssd_gate_norm(ycore, xc, proj, dskip_x, ssd_norm_w, "ssd_gate_norm")

    qn, kn, vb = _qkv_prep(proj, qw2, kw2, "qkv_prep")
    o_sb, tot, *gathered = _sb_fwd(qn, kn, vb, later_shards, "sb_fwd")
    w_out, w_gate_t, w_up_t, w_down = [g.reshape(N_DEV * g.shape[1], D_MODEL) for g in gathered]
    ycat = jnp.concatenate([y_ssd, o_sb.astype(MXU_DTYPE)], axis=1)

    mix = _matmul(ycat, w_out, "out_proj")
    x1, h2 = _residual_rms_mod(x, mix, g1, norm2_w, sc2, sh2, "residual_rms_mod2")
    gate, up, act = _ffn_in(h2, w_gate_t, w_up_t, "ffn_in")
    ffn = _matmul(act, w_down, "ffn_down")
    dy, dffn, st_g2, loss_blk = _loss_head(x1, ffn, g2, target, "loss_head")

    g_down = _matmul_tn(act, dffn, "ffn_down_dw", MXU_DTYPE)
    dact = _matmul_nt(dffn, w_down, "ffn_down_dx")
    dgate, dup = _swiglu_bwd(gate, up, dact, "swiglu_bwd")
    g_gate_t = _matmul_tn(dgate, h2, "ffn_gate_dw", MXU_DTYPE)
    g_up_t = _matmul_tn(dup, h2, "ffn_up_dw", MXU_DTYPE)
    dh2 = _matmul_sum2(dgate, w_gate_t, dup, w_up_t, "ffn_dh")
    dx1, dmix, st_n2 = _norm_bwd(x1, dh2, dy, norm2_w, sc2, "norm2_bwd", gated=(mix, g1))

    g_out = _matmul_tn(ycat, dmix, "out_proj_dw", MXU_DTYPE)
    dycat = _matmul_nt(dmix, w_out, "out_proj_dx")

    partials = [g.reshape(N_DEV, g.shape[0] // N_DEV, D_MODEL).astype(BF16) for g in (g_out, g_gate_t, g_up_t, g_down)]
    dqn, dkn, dv, *slots = _sb_bwd(qn, kn, vb, dycat, tot, partials, "sb_bwd")
    dq, dk, st_q, st_k = _qk_norm_bwd(proj, dqn, dkn, qw2, kw2, "qk_norm_bwd")

    dycore, dxs_skip, dz, st_gn = _ssd_gate_norm_bwd(dycat, ycore, xc, proj, dskip_x, ssd_norm_w, "ssd_gate_norm_bwd")
    dxc, ddtr_x, st_ssd, st_heads = _ssd_bwd(xc, dtr_x, dtr_y, dtb_x, dtb_y, al_x, al_y, dycore, states, head_of_lanes, "ssd_bwd")
    ddtr = ddtr_x[:, ::HEAD_DIM]
    dxc = jnp.concatenate([dxc[:, :D_SSD] + dxs_skip, dxc[:, D_SSD:]], axis=1)
    dpre, st_conv = _conv_silu_bwd_pre(proj, dxc, conv_w, conv_b, "conv_silu_bwd")
    dxbc = _conv_bwd_input(dpre, conv_w, "conv_bwd_input")

    pieces = {"z": (dz, O_Z), "xbc": (dxbc, O_XBC), "dt": (ddtr, O_DT), "q": (dq, O_Q), "k": (dk, O_Q + D_SB), "v": (dv, O_Q + 2 * D_SB)}
    g_in_t = jnp.concatenate([_matmul_tn(p, h1, "in_proj_dw_" + n, MXU_DTYPE) for n, (p, _) in pieces.items()], axis=0)
    g_in_blocks = g_in_t.reshape(N_DEV, g_in_t.shape[0] // N_DEV, D_MODEL).astype(BF16)
    dh1, g_in_slots = _matmul_pieces_beside_scatter([p for p, _ in pieces.values()], [o for _, o in pieces.values()], w_in_tp,
                                                    [g_in_blocks], "in_proj_dx")
    slots = [g_in_slots] + list(slots)
    grad_x, st_n1 = _norm_bwd(x, dh1, dx1, norm1_w, sc1, "norm1_bwd")

    pad = jnp.zeros((1, SM_SSD_NORM - SM_D_SKIP - N_HEADS), F32)
    small = jnp.concatenate(
        [st_n1[1:2], st_n1[0:1], st_n2[3:4], st_n2[1:2], st_n2[0:1], st_g2[0:1],
         st_n1[2:3], st_conv[4:5], st_conv[0:4].reshape(1, CONV_WIDTH * D_CONV),
         st_ssd[1:2, ::HEAD_DIM], st_ssd[0:1, ::HEAD_DIM], st_heads[2:3, ::HEAD_DIM], pad,
         st_gn[0:1], st_q[0:1, :HEAD_DIM] + st_q[0:1, HEAD_DIM:], st_k[0:1, :HEAD_DIM] + st_k[0:1, HEAD_DIM:], st_n2[2:3]], axis=1)
    return loss_blk, grad_x, slots, small


def kernel(x, c, w_ada, b_ada, norm1_w, w_in, conv_w, conv_b, dt_bias, a_log, d_skip, ssd_norm_w, q_norm_w, k_norm_w, w_out, norm2_w, w_gate, w_up, w_down, loss_target, m_w_ada, m_b_ada, m_norm1_w, m_w_in, m_conv_w, m_conv_b, m_dt_bias, m_a_log, m_d_skip, m_ssd_norm_w, m_q_norm_w, m_k_norm_w, m_w_out, m_norm2_w, m_w_gate, m_w_up, m_w_down, v_w_ada, v_b_ada, v_norm1_w, v_w_in, v_conv_w, v_conv_b, v_dt_bias, v_a_log, v_d_skip, v_ssd_norm_w, v_q_norm_w, v_k_norm_w, v_w_out, v_norm2_w, v_w_gate, v_w_up, v_w_down):
    me = 4 * lax.axis_index("x") + 2 * lax.axis_index("y") + lax.axis_index("c")
    conv_cols = D_CONV // N_DEV
    ada_cols = N_MOD * D_MODEL // N_DEV

    (w_in_t,) = _all_gather_big([w_in[0].T.astype(BF16)], "gather_w_in")
    w_in_tp = _pad_w_in_t(w_in_t.reshape(D_IN_PROJ, D_MODEL))
    later_shards = [s.astype(BF16) for s in (w_out[0], w_gate[0].T, w_up[0].T, w_down[0])]

    first = _all_gather_small(jnp.concatenate([c, conv_w[0].reshape(1, CONV_WIDTH * conv_cols)], axis=1), "gather_cond")
    c_all = first[:, 0, :D_MODEL]
    conv_w_f = first[:, 0, D_MODEL:].reshape(N_DEV, CONV_WIDTH, conv_cols).transpose(1, 0, 2).reshape(CONV_WIDTH, D_CONV)
    mod_cols = _ada_fwd(c_all, w_ada[0], "ada_fwd")
    mods = _all_gather_small(mod_cols.reshape(1, N_DEV * ada_cols), "gather_mod")
    mod = lax.dynamic_index_in_dim(mods.reshape(N_DEV, N_DEV, ada_cols), me, axis=1, keepdims=False)
    mod = mod.reshape(1, N_MOD * D_MODEL) + b_ada

    loss_blk, grad_x, slots, small = _local_step(
        x[0], loss_target[0], mod, norm1_w, w_in_tp, conv_w_f, conv_b, dt_bias, a_log, d_skip, ssd_norm_w, q_norm_w, k_norm_w,
        norm2_w, later_shards)
    loss = lax.psum(loss_blk[0, 0], ("x", "y", "c"))

    g_in_t, g_out, g_gate_t, g_up_t, g_down = [_sum_slots(s, "sum_grads_" + n) for s, n in zip(slots, ("in", "out", "gate", "up", "down"))]
    g_in, g_gate, g_up = g_in_t.T, g_gate_t.T, g_up_t.T

    parts = _all_gather_small(small, "gather_small")
    gsum = _sum_small(parts, "sum_small")
    dmod_shard = lax.dynamic_slice_in_dim(parts[:, 0, :N_MOD * D_MODEL], me * ada_cols, ada_cols, axis=1)
    g_ada = _ada_bwd(c_all, dmod_shard, "ada_bwd")
    g_conv_w = lax.dynamic_slice_in_dim(gsum[:, SM_CONV_W:SM_DT_BIAS].reshape(CONV_WIDTH, D_CONV), me * conv_cols, conv_cols, axis=1)

    def pack_small(b_ada_, norm1_, conv_b_, dt_bias_, a_log_, d_skip_, ssd_norm_, q_norm_, k_norm_, norm2_):
        return jnp.concatenate(
            [b_ada_, norm1_, conv_b_, jnp.zeros((1, CONV_WIDTH * D_CONV), F32), dt_bias_, a_log_, d_skip_,
             jnp.zeros((1, SM_SSD_NORM - SM_D_SKIP - N_HEADS), F32), ssd_norm_, q_norm_, k_norm_, norm2_], axis=1)

    def unpack_small(p):
        return {
            "b_ada": p[:, SM_B_ADA:SM_NORM1], "norm1_w": p[:, SM_NORM1:SM_CONV_B], "conv_b": p[:, SM_CONV_B:SM_CONV_W],
            "dt_bias": p[:, SM_DT_BIAS:SM_A_LOG], "a_log": p[:, SM_A_LOG:SM_D_SKIP], "d_skip": p[:, SM_D_SKIP:SM_D_SKIP + N_HEADS],
            "ssd_norm_w": p[:, SM_SSD_NORM:SM_Q_NORM], "q_norm_w": p[:, SM_Q_NORM:SM_K_NORM], "k_norm_w": p[:, SM_K_NORM:SM_NORM2],
            "norm2_w": p[:, SM_NORM2:SM_TOTAL]}

    w_small = pack_small(b_ada, norm1_w, conv_b, dt_bias, a_log, d_skip, ssd_norm_w, q_norm_w, k_norm_w, norm2_w)
    m_small = pack_small(m_b_ada, m_norm1_w, m_conv_b, m_dt_bias, m_a_log, m_d_skip, m_ssd_norm_w, m_q_norm_w, m_k_norm_w, m_norm2_w)
    v_small = pack_small(v_b_ada, v_norm1_w, v_conv_b, v_dt_bias, v_a_log, v_d_skip, v_ssd_norm_w, v_q_norm_w, v_k_norm_w, v_norm2_w)
    small_out = [unpack_small(t) for t in (gsum,) + tuple(_adamw(w_small, gsum, m_small, v_small, "adamw_small"))]

    sharded = {
        "w_ada": (w_ada[0], g_ada, m_w_ada[0], v_w_ada[0]),
        "w_in": (w_in[0], g_in, m_w_in[0], v_w_in[0]),
        "conv_w": (conv_w[0], g_conv_w, m_conv_w[0], v_conv_w[0]),
        "w_out": (w_out[0], g_out, m_w_out[0], v_w_out[0]),
        "w_gate": (w_gate[0], g_gate, m_w_gate[0], v_w_gate[0]),
        "w_up": (w_up[0], g_up, m_w_up[0], v_w_up[0]),
        "w_down": (w_down[0], g_down, m_w_down[0], v_w_down[0]),
    }
    sharded_out = {n: (t[1],) + tuple(_adamw(*t, "adamw_" + n)) for n, t in sharded.items()}

    names = ["w_ada", "b_ada", "norm1_w", "w_in", "conv_w", "conv_b", "dt_bias", "a_log", "d_skip", "ssd_norm_w", "q_norm_w",
             "k_norm_w", "w_out", "norm2_w", "w_gate", "w_up", "w_down"]
    outs = [loss, grad_x[None]]
    for kind in range(4):
        for n in names:
            outs.append(sharded_out[n][kind][None] if n in sharded_out else small_out[kind][n])
    return tuple(outs)
```

```python
import functools

import jax
import jax.numpy as jnp
from jax import lax
from jax.experimental import pallas as pl
from jax.experimental.pallas import tpu as pltpu

F32 = jnp.float32
BF16 = jnp.bfloat16
MXU_DTYPE = jnp.bfloat16
HIGHEST = lax.Precision.HIGHEST
MESH_IDS = pl.DeviceIdType.MESH

N_DEV = 8
D_MODEL = 1024
HEAD_DIM = 64
N_HEADS = 16
D_SSD = 1024
D_SB = 1024
SSD_GROUPS = 2
SSD_STATE = 128
GROUP_WIDTH = D_SSD // SSD_GROUPS
D_CONV = D_SSD + 2 * SSD_GROUPS * SSD_STATE
CONV_WIDTH = 4
CHUNK = 128
D_FF = 2816
N_MOD = 6
EPS = 1e-6
D_IN_PROJ = 5648
O_XBC = 0
O_DT = D_CONV
DT_PAD = 512
O_Z = O_DT + DT_PAD
O_Q = O_Z + D_SSD
D_IN_PAD = O_Q + 3 * D_SB
HALO = 8

ADAM_LR = 0.001
ADAM_B1 = 0.9
ADAM_B2 = 0.999
ADAM_EPS = 1e-08
ADAM_WD = 0.01
ADAM_STEP = 10

ROWS_W_IN = 706
ROWS_W_OUT = 256
ROWS_FF = 352
PACK_ROWS = 2048
SM_B_ADA = 0
SM_NORM1 = 6144
SM_CONV_B = 7168
SM_CONV_W = 8704
SM_DT_BIAS = 14848
SM_A_LOG = 14864
SM_D_SKIP = 14880
SM_SSD_NORM = 14976
SM_Q_NORM = 16000
SM_K_NORM = 16064
SM_NORM2 = 16128
SM_TOTAL = 17152


def _pick(n, cap, mult):
    if n <= cap:
        return n
    best = None
    for t in range(mult, cap + 1, mult):
        if n % t == 0:
            best = t
    assert best is not None, (n, cap, mult)
    return best


def _dot(a, b, precision=None):
    return jnp.dot(a, b, preferred_element_type=F32, precision=precision)


def _dot_nt(a, b, precision=None):
    return lax.dot_general(a, b, (((1,), (1,)), ((), ())), preferred_element_type=F32, precision=precision)


def _dot_tn(a, b, precision=None):
    return lax.dot_general(a, b, (((0,), (0,)), ((), ())), preferred_element_type=F32, precision=precision)


def _softplus(v):
    return jnp.maximum(v, 0.0) + jnp.log1p(jnp.exp(-jnp.abs(v)))


def _softplus_logits(v):
    return jnp.maximum(v, 0.0) + jnp.log(1.0 + jnp.exp(-jnp.abs(v)))


def _tri_dot(tri_b, v):
    hi = v.astype(BF16)
    r1 = v - hi.astype(F32)
    mid = r1.astype(BF16)
    lo = (r1 - mid.astype(F32)).astype(BF16)
    return _dot(tri_b, hi) + _dot(tri_b, mid) + _dot(tri_b, lo)


def _sigmoid(v):
    return jax.nn.sigmoid(v)


def _colsum(v):
    return jnp.sum(v, axis=0, keepdims=True)


def _split_dot(v, tri2):
    hi = v.astype(BF16)
    lo = (v - hi.astype(F32)).astype(BF16)
    return _dot(jnp.concatenate([hi, lo], axis=1), tri2)


def _position():
    x, y, c = lax.axis_index("x"), lax.axis_index("y"), lax.axis_index("c")
    return x, y, c


def _peer(x, y, c, k):
    px = 1 - x if (k >> 2) & 1 else x
    py = 1 - y if (k >> 1) & 1 else y
    pc = 1 - c if k & 1 else c
    return px, py, pc


def _all_gather_small(v, name):
    n = v.shape[1]

    def body(v_ref, out_ref, send_sems, recv_sems, local_sem):
        x, y, c = _position()
        me = 4 * x + 2 * y + c
        mine = pltpu.make_async_copy(v_ref, out_ref.at[me], local_sem)
        mine.start()
        sends = []
        for k in range(1, N_DEV):
            cp = pltpu.make_async_remote_copy(
                src_ref=v_ref, dst_ref=out_ref.at[me], send_sem=send_sems.at[k - 1], recv_sem=recv_sems.at[k - 1],
                device_id=_peer(x, y, c, k), device_id_type=MESH_IDS)
            cp.start()
            sends.append(cp)
        for k in range(1, N_DEV):
            px, py, pc = _peer(x, y, c, k)
            pltpu.make_async_remote_copy(
                src_ref=v_ref, dst_ref=out_ref.at[4 * px + 2 * py + pc], send_sem=send_sems.at[k - 1],
                recv_sem=recv_sems.at[k - 1], device_id=(px, py, pc), device_id_type=MESH_IDS).wait_recv()
        for cp in sends:
            cp.wait_send()
        mine.wait()

    return pl.pallas_call(
        body, name=name,
        out_shape=jax.ShapeDtypeStruct((N_DEV, 1, n), v.dtype),
        in_specs=[pl.BlockSpec(memory_space=pltpu.VMEM)],
        out_specs=pl.BlockSpec(memory_space=pltpu.VMEM),
        scratch_shapes=[pltpu.SemaphoreType.DMA((N_DEV - 1,)), pltpu.SemaphoreType.DMA((N_DEV - 1,)),
                        pltpu.SemaphoreType.DMA],
    )(v)


def _all_gather_big(blocks, name):
    na = len(blocks)
    copies = N_DEV - 1

    def body(*refs):
        b_refs, out_refs = refs[:na], refs[na:2 * na]
        send_sems, recv_sems, local_sems = refs[2 * na:]
        x, y, c = _position()
        me, sibling = (x, y, c), (x, y, 1 - c)
        chips = [(1 - x, y), (x, 1 - y), (1 - x, 1 - y)]

        def copy(a, k, blk, to, own=False):
            slot = out_refs[a].at[4 * blk[0] + 2 * blk[1] + blk[2]]
            return pltpu.make_async_remote_copy(
                src_ref=b_refs[a] if own else slot, dst_ref=slot,
                send_sem=send_sems.at[copies * a + k], recv_sem=recv_sems.at[copies * a + k], device_id=to, device_id_type=MESH_IDS)

        mine = [pltpu.make_async_copy(b_refs[a], out_refs[a].at[4 * x + 2 * y + c], local_sems.at[a]) for a in range(na)]
        for cp in mine:
            cp.start()
        first = [copy(a, 0, me, sibling, own=True) for a in range(na)]
        first += [copy(a, 1 + j, me, (*chip, c), own=True) for j, chip in enumerate(chips) for a in range(na)]
        for cp in first:
            cp.start()
        passed = []
        for j, chip in enumerate(chips):
            for a in range(na):
                copy(a, 1 + j, (*chip, c), me).wait_recv()
                passed.append(copy(a, 4 + j, (*chip, c), sibling))
                passed[-1].start()
        for a in range(na):
            copy(a, 0, sibling, me).wait_recv()
        for j, chip in enumerate(chips):
            for a in range(na):
                copy(a, 4 + j, (*chip, 1 - c), me).wait_recv()
        for cp in first + passed:
            cp.wait_send()
        for cp in mine:
            cp.wait()

    hbm = pl.BlockSpec(memory_space=pl.ANY)
    return pl.pallas_call(
        body, name=name,
        out_shape=[jax.ShapeDtypeStruct((N_DEV,) + b.shape, b.dtype) for b in blocks],
        in_specs=[hbm] * na, out_specs=[hbm] * na,
        scratch_shapes=[pltpu.SemaphoreType.DMA((copies * na,)), pltpu.SemaphoreType.DMA((copies * na,)),
                        pltpu.SemaphoreType.DMA((na,))],
    )(*blocks)


class _DirectExchange:
    def __init__(self, kind, src_refs, dst_refs, send_sems, recv_sems, local_sems):
        self.kind, self.src_refs, self.dst_refs = kind, src_refs, dst_refs
        self.send_sems, self.recv_sems, self.local_sems = send_sems, recv_sems, local_sems

    @staticmethod
    def scratch(n_arrays):
        copies = N_DEV - 1
        return [pltpu.SemaphoreType.DMA((copies * n_arrays,)), pltpu.SemaphoreType.DMA((copies * n_arrays,)),
                pltpu.SemaphoreType.DMA((n_arrays,))]

    def _copies(self):
        x, y, c = _position()
        me = 4 * x + 2 * y + c
        local, sends, arrivals = [], [], []
        for a, (src, dst) in enumerate(zip(self.src_refs, self.dst_refs)):
            own = src if self.kind == "gather" else src.at[me]
            local.append(pltpu.make_async_copy(own, dst.at[me], self.local_sems.at[a]))
            for k in range(1, N_DEV):
                px, py, pc = _peer(x, y, c, k)
                peer = 4 * px + 2 * py + pc
                sems = dict(send_sem=self.send_sems.at[(N_DEV - 1) * a + k - 1], recv_sem=self.recv_sems.at[(N_DEV - 1) * a + k - 1],
                            device_id=(px, py, pc), device_id_type=MESH_IDS)
                sends.append(pltpu.make_async_remote_copy(
                    src_ref=src if self.kind == "gather" else src.at[peer], dst_ref=dst.at[me], **sems))
                arrivals.append(pltpu.make_async_remote_copy(src_ref=own, dst_ref=dst.at[peer], **sems))
        return local, sends, arrivals

    def start(self):
        local, sends, _ = self._copies()
        for cp in local + sends:
            cp.start()

    def wait(self):
        local, sends, arrivals = self._copies()
        for cp in arrivals:
            cp.wait_recv()
        for cp in sends:
            cp.wait_send()
        for cp in local:
            cp.wait()


def _all_to_all_big(blocks, name):
    na = len(blocks)

    def body(*refs):
        exchange = _DirectExchange("scatter", refs[:na], refs[na:2 * na], *refs[2 * na:])
        exchange.start()
        exchange.wait()

    hbm = pl.BlockSpec(memory_space=pl.ANY)
    return pl.pallas_call(
        body, name=name,
        out_shape=[jax.ShapeDtypeStruct(b.shape, b.dtype) for b in blocks],
        in_specs=[hbm] * na, out_specs=[hbm] * na, scratch_shapes=_DirectExchange.scratch(na),
    )(*blocks)


def _sum_slots(slots, name):
    _, r, n = slots.shape
    tr = _pick(r, 256, 16) if r % 16 == 0 else r

    def body(s_ref, o_ref):
        acc = s_ref[0].astype(F32)
        for d in range(1, N_DEV):
            acc = acc + s_ref[d].astype(F32)
        o_ref[...] = acc

    return pl.pallas_call(
        body, name=name, grid=(r // tr,),
        in_specs=[pl.BlockSpec((N_DEV, tr, n), lambda i: (0, i, 0))],
        out_specs=pl.BlockSpec((tr, n), lambda i: (i, 0)),
        out_shape=jax.ShapeDtypeStruct((r, n), F32),
    )(slots)


def _matmul(a, b, name, out_dtype=F32):
    m, k = a.shape
    _, n = b.shape
    tm, tn, tk = _pick(m, 1024, 16), _pick(n, 1408, 128), _pick(k, 1408, 128)
    nk = k // tk

    def body(a_ref, b_ref, o_ref, acc_ref):
        kk = pl.program_id(2)

        @pl.when(kk == 0)
        def _():
            acc_ref[...] = jnp.zeros_like(acc_ref)

        acc_ref[...] += _dot(a_ref[...].astype(MXU_DTYPE), b_ref[...].astype(MXU_DTYPE))

        @pl.when(kk == nk - 1)
        def _():
            o_ref[...] = acc_ref[...].astype(o_ref.dtype)

    return pl.pallas_call(
        body, name=name, grid=(m // tm, n // tn, nk),
        in_specs=[pl.BlockSpec((tm, tk), lambda i, j, kk: (i, kk)), pl.BlockSpec((tk, tn), lambda i, j, kk: (kk, j))],
        out_specs=pl.BlockSpec((tm, tn), lambda i, j, kk: (i, j)),
        out_shape=jax.ShapeDtypeStruct((m, n), out_dtype),
        scratch_shapes=[pltpu.VMEM((tm, tn), F32)],
    )(a, b)


def _matmul_pieces_beside_scatter(pieces, offsets, w, blocks, name):
    m = pieces[0].shape[0]
    n = w.shape[1]
    tm = _pick(m, 256, 16)
    gm = m // tm
    npieces, na = len(pieces), len(blocks)
    hbm = pl.BlockSpec(memory_space=pl.ANY)

    def body(*refs):
        a_refs, w_refs = refs[:npieces], refs[npieces:2 * npieces]
        rest = refs[2 * npieces:]
        o_ref = rest[na]
        exchange = _DirectExchange("scatter", rest[:na], rest[na + 1:2 * na + 1], *rest[2 * na + 1:])
        i = pl.program_id(0)

        @pl.when(i == 0)
        def _():
            exchange.start()

        acc = _dot(a_refs[0][...].astype(MXU_DTYPE), w_refs[0][...])
        for a_ref, w_ref in zip(a_refs[1:], w_refs[1:]):
            acc = acc + _dot(a_ref[...].astype(MXU_DTYPE), w_ref[...])
        o_ref[...] = acc

        @pl.when(i == gm - 1)
        def _():
            exchange.wait()

    a_specs = [pl.BlockSpec((tm, p.shape[1]), lambda i: (i, 0)) for p in pieces]
    w_specs = [pl.BlockSpec((p.shape[1], n), functools.partial(lambda i, blk: (blk, 0), blk=off // p.shape[1]))
               for p, off in zip(pieces, offsets)]
    return pl.pallas_call(
        body, name=name, grid=(gm,),
        in_specs=a_specs + w_specs + [hbm] * na,
        out_specs=[pl.BlockSpec((tm, n), lambda i: (i, 0))] + [hbm] * na,
        out_shape=[jax.ShapeDtypeStruct((m, n), F32)] + [jax.ShapeDtypeStruct(blk.shape, blk.dtype) for blk in blocks],
        scratch_shapes=_DirectExchange.scratch(na),
    )(*pieces, *([w] * npieces), *blocks)


def _matmul_nt(a, bt, name, out_dtype=F32):
    m, k = a.shape
    n, _ = bt.shape
    tm, tn, tk = _pick(m, 1024, 16), _pick(n, 1408, 128), _pick(k, 1408, 128)
    nk = k // tk

    def body(a_ref, b_ref, o_ref, acc_ref):
        kk = pl.program_id(2)

        @pl.when(kk == 0)
        def _():
            acc_ref[...] = jnp.zeros_like(acc_ref)

        acc_ref[...] += _dot_nt(a_ref[...].astype(MXU_DTYPE), b_ref[...].astype(MXU_DTYPE))

        @pl.when(kk == nk - 1)
        def _():
            o_ref[...] = acc_ref[...].astype(o_ref.dtype)

    return pl.pallas_call(
        body, name=name, grid=(m // tm, n // tn, nk),
        in_specs=[pl.BlockSpec((tm, tk), lambda i, j, kk: (i, kk)), pl.BlockSpec((tn, tk), lambda i, j, kk: (j, kk))],
        out_specs=pl.BlockSpec((tm, tn), lambda i, j, kk: (i, j)),
        out_shape=jax.ShapeDtypeStruct((m, n), out_dtype),
        scratch_shapes=[pltpu.VMEM((tm, tn), F32)],
    )(a, bt)


def _matmul_sum2(a1, b1, a2, b2, name):
    m, k = a1.shape
    _, n = b1.shape
    tm, tn, tk = _pick(m, 1024, 16), _pick(n, 1408, 128), _pick(k, 1408, 128)
    nk = k // tk

    def body(a1_ref, b1_ref, a2_ref, b2_ref, o_ref):
        @pl.when(pl.program_id(2) == 0)
        def _():
            o_ref[...] = jnp.zeros_like(o_ref)

        o_ref[...] += (_dot(a1_ref[...].astype(MXU_DTYPE), b1_ref[...].astype(MXU_DTYPE))
                       + _dot(a2_ref[...].astype(MXU_DTYPE), b2_ref[...].astype(MXU_DTYPE)))

    a_spec = pl.BlockSpec((tm, tk), lambda i, j, kk: (i, kk))
    b_spec = pl.BlockSpec((tk, tn), lambda i, j, kk: (kk, j))
    return pl.pallas_call(
        body, name=name, grid=(m // tm, n // tn, nk), in_specs=[a_spec, b_spec, a_spec, b_spec],
        out_specs=pl.BlockSpec((tm, tn), lambda i, j, kk: (i, j)),
        out_shape=jax.ShapeDtypeStruct((m, n), F32),
    )(a1, b1, a2, b2)


def _matmul_tn(a, b, name, out_dtype=F32):
    l, m = a.shape
    _, n = b.shape
    tm, tn, tl = _pick(m, 1408, 128), _pick(n, 1408, 128), _pick(l, 512, 16)
    nl = l // tl

    def body(a_ref, b_ref, o_ref, acc_ref):
        ll = pl.program_id(2)

        @pl.when(ll == 0)
        def _():
            acc_ref[...] = jnp.zeros_like(acc_ref)

        acc_ref[...] += _dot_tn(a_ref[...].astype(MXU_DTYPE), b_ref[...].astype(MXU_DTYPE))

        @pl.when(ll == nl - 1)
        def _():
            o_ref[...] = acc_ref[...].astype(o_ref.dtype)

    return pl.pallas_call(
        body, name=name, grid=(m // tm, n // tn, nl),
        in_specs=[pl.BlockSpec((tl, tm), lambda i, j, ll: (ll, i)), pl.BlockSpec((tl, tn), lambda i, j, ll: (ll, j))],
        out_specs=pl.BlockSpec((tm, tn), lambda i, j, ll: (i, j)),
        out_shape=jax.ShapeDtypeStruct((m, n), out_dtype),
        scratch_shapes=[pltpu.VMEM((tm, tn), F32)],
    )(a, b)


def _row_specs(l, d, n_rows, n_vecs):
    tl = _pick(l, 512, 16)
    row = pl.BlockSpec((tl, d), lambda i: (i, 0))
    vec = pl.BlockSpec((1, d), lambda i: (0, 0))
    return tl, row, vec, [row] * n_rows + [vec] * n_vecs


def _rms_mod(x, nw, sc, sh, name):
    l, d = x.shape
    tl, row, _, in_specs = _row_specs(l, d, 1, 3)

    def body(x_ref, nw_ref, sc_ref, sh_ref, h_ref):
        xv = x_ref[...]
        r = lax.rsqrt(jnp.mean(xv * xv, axis=-1, keepdims=True) + EPS)
        h_ref[...] = (((xv * r) * nw_ref[...]) * (1.0 + sc_ref[...]) + sh_ref[...]).astype(h_ref.dtype)

    return pl.pallas_call(body, name=name, grid=(l // tl,), in_specs=in_specs, out_specs=row,
                          out_shape=jax.ShapeDtypeStruct((l, d), MXU_DTYPE))(x, nw, sc, sh)


def _residual_rms_mod(x, mix, g, nw, sc, sh, name):
    l, d = x.shape
    tl, row, _, in_specs = _row_specs(l, d, 2, 4)

    def body(x_ref, mix_ref, g_ref, nw_ref, sc_ref, sh_ref, x1_ref, h_ref):
        xv = x_ref[...] + g_ref[...] * mix_ref[...]
        x1_ref[...] = xv
        r = lax.rsqrt(jnp.mean(xv * xv, axis=-1, keepdims=True) + EPS)
        h_ref[...] = (((xv * r) * nw_ref[...]) * (1.0 + sc_ref[...]) + sh_ref[...]).astype(h_ref.dtype)

    return pl.pallas_call(body, name=name, grid=(l // tl,), in_specs=in_specs, out_specs=[row, row],
                          out_shape=[jax.ShapeDtypeStruct((l, d), F32), jax.ShapeDtypeStruct((l, d), MXU_DTYPE)],
                          )(x, mix, g, nw, sc, sh)


def _norm_bwd(x, dh, dres, nw, sc, name, gated=None):
    l, d = x.shape
    tl, row, vec, in_specs = _row_specs(l, d, 3, 2)
    stats = pl.BlockSpec((8, d), lambda i: (0, 0))

    def body(x_ref, dh_ref, dres_ref, nw_ref, sc_ref, *rest):
        dx_ref, st_ref = rest[-2:] if gated is None else (rest[2], rest[4])
        xv, dh_v = x_ref[...], dh_ref[...]
        r = lax.rsqrt(jnp.mean(xv * xv, axis=-1, keepdims=True) + EPS)
        xn = xv * r
        dxn = dh_v * (nw_ref[...] * (1.0 + sc_ref[...]))
        dx = dres_ref[...] + r * (dxn - xn * jnp.mean(dxn * xn, axis=-1, keepdims=True))
        dx_ref[...] = dx

        @pl.when(pl.program_id(0) == 0)
        def _():
            st_ref[...] = jnp.zeros_like(st_ref)

        dhx = dh_v * xn
        st_ref[0:1, :] += _colsum(dhx * nw_ref[...])
        st_ref[1:2, :] += _colsum(dh_v)
        st_ref[2:3, :] += _colsum(dhx * (1.0 + sc_ref[...]))
        if gated is not None:
            val_ref, g_ref, dval_ref = rest[0], rest[1], rest[3]
            dval_ref[...] = (g_ref[...] * dx).astype(dval_ref.dtype)
            st_ref[3:4, :] += _colsum(dx * val_ref[...])

    if gated is None:
        return pl.pallas_call(body, name=name, grid=(l // tl,), in_specs=in_specs, out_specs=[row, stats],
                              out_shape=[jax.ShapeDtypeStruct((l, d), F32), jax.ShapeDtypeStruct((8, d), F32)],
                              )(x, dh, dres, nw, sc)
    return pl.pallas_call(body, name=name, grid=(l // tl,), in_specs=in_specs + [row, vec], out_specs=[row, row, stats],
                          out_shape=[jax.ShapeDtypeStruct((l, d), F32), jax.ShapeDtypeStruct((l, d), MXU_DTYPE),
                                     jax.ShapeDtypeStruct((8, d), F32)],
                          )(x, dh, dres, nw, sc, *gated)


def _gate_bwd(dres, val, g, name):
    l, d = dres.shape
    tl, row, _, in_specs = _row_specs(l, d, 2, 1)

    def body(dres_ref, val_ref, g_ref, dval_ref, st_ref):
        dr = dres_ref[...]
        dval_ref[...] = (g_ref[...] * dr).astype(dval_ref.dtype)

        @pl.when(pl.program_id(0) == 0)
        def _():
            st_ref[...] = jnp.zeros_like(st_ref)

        st_ref[0:1, :] += _colsum(dr * val_ref[...])

    return pl.pallas_call(body, name=name, grid=(l // tl,), in_specs=in_specs,
                          out_specs=[row, pl.BlockSpec((8, d), lambda i: (0, 0))],
                          out_shape=[jax.ShapeDtypeStruct((l, d), MXU_DTYPE), jax.ShapeDtypeStruct((8, d), F32)],
                          )(dres, val, g)


def _loss_head(x1, f, g, target, name):
    l, d = x1.shape
    tl, row, _, _ = _row_specs(l, d, 0, 0)
    vec = pl.BlockSpec((1, d), lambda i: (0, 0))

    def body(x1_ref, f_ref, g_ref, t_ref, dy_ref, df_ref, st_ref, loss_ref):
        fv = f_ref[...]
        e = x1_ref[...] + g_ref[...] * fv - t_ref[...]
        dy = e * (1.0 / d)
        dy_ref[...] = dy
        df_ref[...] = (g_ref[...] * dy).astype(df_ref.dtype)

        @pl.when(pl.program_id(0) == 0)
        def _():
            loss_ref[...] = jnp.zeros_like(loss_ref)
            st_ref[...] = jnp.zeros_like(st_ref)

        st_ref[0:1, :] += _colsum(dy * fv)
        s = jnp.sum(jnp.sum(e * e, axis=1, keepdims=True), axis=0, keepdims=True)
        loss_ref[...] += (0.5 / d) * s

    return pl.pallas_call(body, name=name, grid=(l // tl,), in_specs=[row, row, vec, row],
                          out_specs=[row, row, pl.BlockSpec((8, d), lambda i: (0, 0)), pl.BlockSpec((8, 128), lambda i: (0, 0))],
                          out_shape=[jax.ShapeDtypeStruct((l, d), F32), jax.ShapeDtypeStruct((l, d), MXU_DTYPE),
                                     jax.ShapeDtypeStruct((8, d), F32), jax.ShapeDtypeStruct((8, 128), F32)],
                          )(x1, f, g, target)


def _ffn_in(h, w_gate_t, w_up_t, name):
    l, k = h.shape
    f = w_gate_t.shape[0]
    tl, tf = _pick(l, 512, 16), _pick(f, 1408, 128)
    h_spec = pl.BlockSpec((tl, k), lambda i, j: (i, 0))
    w_spec = pl.BlockSpec((tf, k), lambda i, j: (j, 0))
    spec = pl.BlockSpec((tl, tf), lambda i, j: (i, j))

    def body(h_ref, wg_ref, wu_ref, g_ref, u_ref, a_ref):
        hv = h_ref[...].astype(MXU_DTYPE)
        gv = _dot_nt(hv, wg_ref[...].astype(MXU_DTYPE))
        uv = _dot_nt(hv, wu_ref[...].astype(MXU_DTYPE))
        g_ref[...] = gv
        u_ref[...] = uv
        a_ref[...] = (gv * _sigmoid(gv) * uv).astype(a_ref.dtype)

    return pl.pallas_call(body, name=name, grid=(l // tl, f // tf), in_specs=[h_spec, w_spec, w_spec], out_specs=[spec, spec, spec],
                          out_shape=[jax.ShapeDtypeStruct((l, f), F32)] * 2 + [jax.ShapeDtypeStruct((l, f), MXU_DTYPE)],
                          )(h, w_gate_t, w_up_t)


def _ffn_down_bwd(dffn, w_down, gate, up, name):
    l, f = gate.shape
    k = dffn.shape[1]
    tl, tf = _pick(l, 512, 16), _pick(f, 1408, 128)
    spec = pl.BlockSpec((tl, tf), lambda i, j: (i, j))

    def body(d_ref, w_ref, g_ref, u_ref, dg_ref, du_ref):
        dav = _dot_nt(d_ref[...].astype(MXU_DTYPE), w_ref[...].astype(MXU_DTYPE))
        gv = g_ref[...]
        s = _sigmoid(gv)
        dg_ref[...] = (dav * u_ref[...] * (s * (1.0 + gv * (1.0 - s)))).astype(dg_ref.dtype)
        du_ref[...] = (dav * (gv * s)).astype(du_ref.dtype)

    return pl.pallas_call(body, name=name, grid=(l // tl, f // tf),
                          in_specs=[pl.BlockSpec((tl, k), lambda i, j: (i, 0)), pl.BlockSpec((tf, k), lambda i, j: (j, 0)), spec, spec],
                          out_specs=[spec, spec], out_shape=[jax.ShapeDtypeStruct((l, f), MXU_DTYPE)] * 2)(dffn, w_down, gate, up)


def _conv_tile(l):
    return _pick(l, 512, 16)


def _conv_pre(buf, w_ref, b_ref, tl):
    acc = b_ref[...] + w_ref[3:4, :] * buf[HALO:HALO + tl, :]
    for k in range(CONV_WIDTH - 1):
        s = HALO - (CONV_WIDTH - 1) + k
        acc = acc + w_ref[k:k + 1, :] * buf[s:s + tl, :]
    return acc


def _fill_past(buf, u_ref, halo_ref, tl):
    i = pl.program_id(0)

    @pl.when(i == 0)
    def _():
        buf[0:HALO, :] = jnp.zeros((HALO, buf.shape[1]), F32)

    @pl.when(i > 0)
    def _():
        buf[0:HALO, :] = halo_ref[...]

    buf[HALO:HALO + tl, :] = u_ref[...]


def _conv_silu(u, w, b, name):
    l, ch = u.shape[0], w.shape[1]
    tl = _conv_tile(l)
    per = tl // HALO
    cur = pl.BlockSpec((tl, ch), lambda i: (i, 0))
    past = pl.BlockSpec((HALO, ch), lambda i: (jnp.maximum(i * per - 1, 0), 0))

    def body(u_ref, halo_ref, w_ref, b_ref, o_ref, buf):
        _fill_past(buf, u_ref, halo_ref, tl)
        pre = _conv_pre(buf, w_ref, b_ref, tl)
        o_ref[...] = pre * _sigmoid(pre)

    return pl.pallas_call(body, name=name, grid=(l // tl,),
                          in_specs=[cur, past, pl.BlockSpec((CONV_WIDTH, ch), lambda i: (0, 0)), pl.BlockSpec((1, ch), lambda i: (0, 0))],
                          out_specs=cur, out_shape=jax.ShapeDtypeStruct((l, ch), F32),
                          scratch_shapes=[pltpu.VMEM((tl + HALO, ch), F32)])(u, u, w, b)


def _conv_silu_bwd_pre(u, dxc, w, b, name):
    l, ch = u.shape[0], w.shape[1]
    tl = _conv_tile(l)
    per = tl // HALO
    cur = pl.BlockSpec((tl, ch), lambda i: (i, 0))
    past = pl.BlockSpec((HALO, ch), lambda i: (jnp.maximum(i * per - 1, 0), 0))

    def body(u_ref, halo_ref, d_ref, w_ref, b_ref, dpre_ref, st_ref, buf):
        _fill_past(buf, u_ref, halo_ref, tl)
        pre = _conv_pre(buf, w_ref, b_ref, tl)
        s = _sigmoid(pre)
        dpre = d_ref[...] * (s * (1.0 + pre * (1.0 - s)))
        dpre_ref[...] = dpre

        @pl.when(pl.program_id(0) == 0)
        def _():
            st_ref[...] = jnp.zeros_like(st_ref)

        for k in range(CONV_WIDTH):
            s0 = HALO - (CONV_WIDTH - 1) + k
            st_ref[k:k + 1, :] += _colsum(dpre * buf[s0:s0 + tl, :])
        st_ref[CONV_WIDTH:CONV_WIDTH + 1, :] += _colsum(dpre)

    return pl.pallas_call(body, name=name, grid=(l // tl,),
                          in_specs=[cur, past, cur, pl.BlockSpec((CONV_WIDTH, ch), lambda i: (0, 0)), pl.BlockSpec((1, ch), lambda i: (0, 0))],
                          out_specs=[cur, pl.BlockSpec((8, ch), lambda i: (0, 0))],
                          out_shape=[jax.ShapeDtypeStruct((l, ch), F32), jax.ShapeDtypeStruct((8, ch), F32)],
                          scratch_shapes=[pltpu.VMEM((tl + HALO, ch), F32)])(u, u, dxc, w, b)


def _conv_bwd_input(dpre, w, name):
    l, ch = dpre.shape
    tl = _conv_tile(l)
    per = tl // HALO
    nt = l // tl
    cur = pl.BlockSpec((tl, ch), lambda i: (i, 0))
    nxt = pl.BlockSpec((HALO, ch), lambda i: (jnp.minimum((i + 1) * per, l // HALO - 1), 0))

    def body(d_ref, halo_ref, w_ref, du_ref, buf):
        i = pl.program_id(0)
        buf[0:tl, :] = d_ref[...]

        @pl.when(i == nt - 1)
        def _():
            buf[tl:tl + HALO, :] = jnp.zeros((HALO, ch), F32)

        @pl.when(i < nt - 1)
        def _():
            buf[tl:tl + HALO, :] = halo_ref[...]

        acc = w_ref[3:4, :] * buf[0:tl, :]
        for k in range(CONV_WIDTH - 1):
            s = CONV_WIDTH - 1 - k
            acc = acc + w_ref[k:k + 1, :] * buf[s:s + tl, :]
        du_ref[...] = acc

    return pl.pallas_call(body, name=name, grid=(nt,),
                          in_specs=[cur, nxt, pl.BlockSpec((CONV_WIDTH, ch), lambda i: (0, 0))],
                          out_specs=cur, out_shape=jax.ShapeDtypeStruct((l, ch), F32),
                          scratch_shapes=[pltpu.VMEM((tl + HALO, ch), F32)])(dpre, dpre, w)


HEAD_TILE = 128
D_HEAD_TILES = N_HEADS * HEAD_TILE


def _chunk_iota():
    r = lax.broadcasted_iota(jnp.int32, (CHUNK, CHUNK), 0)
    c = lax.broadcasted_iota(jnp.int32, (CHUNK, CHUNK), 1)
    return r, c


def _split3_dot(v, ones_b):
    hi = v.astype(BF16)
    r1 = v - hi.astype(F32)
    mid = r1.astype(BF16)
    lo = (r1 - mid.astype(F32)).astype(BF16)
    return _dot(hi, ones_b) + _dot(mid, ones_b) + _dot(lo, ones_b)


def _split3_dot_nt(v, ones_b):
    hi = v.astype(BF16)
    r1 = v - hi.astype(F32)
    mid = r1.astype(BF16)
    lo = (r1 - mid.astype(F32)).astype(BF16)
    return _dot_nt(hi, ones_b) + _dot_nt(mid, ones_b) + _dot_nt(lo, ones_b)


def _ssd_decays(dt_ref, sx_ref, sy_ref, dtbx_ref, dtby_ref, alx_ref, aly_ref, r, c):
    tri = (r >= c).astype(BF16)
    dtv = dt_ref[...]
    raw_x = _split3_dot(dtv, sx_ref[...]) + dtbx_ref[...]
    dt_x = _softplus(raw_x)
    a_x = -jnp.exp(alx_ref[...])
    adt_x = a_x * dt_x
    acs_x = _tri_dot(tri, adt_x)
    atot_x = _colsum(adt_x)
    acs_y = _tri_dot(tri, -jnp.exp(aly_ref[...]) * _softplus(_split3_dot(dtv, sy_ref[...]) + dtby_ref[...]))
    return raw_x, dt_x, a_x, acs_y, jnp.exp(acs_x), jnp.exp(atot_x - acs_x), jnp.exp(atot_x)


def _head_decay(acs_y, e, r, c):
    col = acs_y[:, e * HEAD_TILE:(e + 1) * HEAD_TILE]
    return jnp.where(r >= c, jnp.exp(col - col.T), 0.0)


def _half_masks():
    lane = lax.broadcasted_iota(jnp.int32, (CHUNK, 2 * HEAD_DIM), 1)
    return lane < HEAD_DIM, lane >= HEAD_DIM


def _ssd_specs(nc, reverse):
    def at(i):
        return nc - 1 - i if reverse else i
    xc = pl.BlockSpec((CHUNK, D_CONV), lambda i: (at(i), 0))
    wide = pl.BlockSpec((CHUNK, D_SSD), lambda i: (at(i), 0))
    dt_tile = pl.BlockSpec((CHUNK, HEAD_TILE), lambda i: (at(i), O_DT // HEAD_TILE))
    dt_out = pl.BlockSpec((CHUNK, HEAD_TILE), lambda i: (at(i), 0))
    spreads = [pl.BlockSpec((HEAD_TILE, D_SSD), lambda i: (0, 0)), pl.BlockSpec((HEAD_TILE, D_HEAD_TILES), lambda i: (0, 0))]
    vec = pl.BlockSpec((1, D_SSD), lambda i: (0, 0))
    vec_y = pl.BlockSpec((1, D_HEAD_TILES), lambda i: (0, 0))
    state = pl.BlockSpec((1, SSD_STATE, D_SSD), lambda i: (at(i), 0, 0))
    return xc, wide, dt_tile, dt_out, spreads, vec, vec_y, state


def _ssd_fwd(xc, proj, spread_x, spread_y, dtb_x, dtb_y, al_x, al_y, name):
    l = xc.shape[0]
    nc = l // CHUNK
    xc_s, wide_s, dt_s, _, spread_s, vec_s, vec_y_s, state_s = _ssd_specs(nc, False)
    pairs_per_group = GROUP_WIDTH // (2 * HEAD_DIM)

    def body(xc_ref, dt_ref, sx_ref, sy_ref, dtbx_ref, dtby_ref, alx_ref, aly_ref, y_ref, sp_ref, state):
        @pl.when(pl.program_id(0) == 0)
        def _():
            state[...] = jnp.zeros_like(state)

        r, c = _chunk_iota()
        halves = _half_masks()
        _, dt_x, _, acs_y, ea_x, ds_x, eatot_x = _ssd_decays(dt_ref, sx_ref, sy_ref, dtbx_ref, dtby_ref, alx_ref, aly_ref, r, c)
        xg = xc_ref[:, 0:D_SSD] * dt_x
        sp_ref[0] = state[...]
        for g in range(SSD_GROUPS):
            lanes = slice(g * GROUP_WIDTH, (g + 1) * GROUP_WIDTH)
            bb = xc_ref[:, D_SSD + g * SSD_STATE:D_SSD + (g + 1) * SSD_STATE].astype(MXU_DTYPE)
            cb = xc_ref[:, D_SSD + (SSD_GROUPS + g) * SSD_STATE:D_SSD + (SSD_GROUPS + g + 1) * SSD_STATE].astype(MXU_DTYPE)
            scores = _dot_nt(cb, bb)
            sg = state[:, lanes]
            ys = []
            for j in range(g * pairs_per_group, (g + 1) * pairs_per_group):
                xg_pair = xg[:, j * 2 * HEAD_DIM:(j + 1) * 2 * HEAD_DIM]
                acc = jnp.zeros((CHUNK, 2 * HEAD_DIM), F32)
                for half in range(2):
                    m = (scores * _head_decay(acs_y, 2 * j + half, r, c)).astype(MXU_DTYPE)
                    acc = acc + _dot(m, jnp.where(halves[half], xg_pair, 0.0).astype(MXU_DTYPE))
                ys.append(acc)
            y_ref[:, lanes] = jnp.concatenate(ys, axis=1) + _dot(cb, sg.astype(MXU_DTYPE)) * ea_x[:, lanes]
            state[:, lanes] = sg * eatot_x[:, lanes] + _dot_tn(bb, (xg[:, lanes] * ds_x[:, lanes]).astype(MXU_DTYPE))

    return pl.pallas_call(
        body, name=name, grid=(nc,),
        in_specs=[xc_s, dt_s] + spread_s + [vec_s, vec_y_s, vec_s, vec_y_s],
        out_specs=[wide_s, state_s],
        out_shape=[jax.ShapeDtypeStruct((l, D_SSD), F32), jax.ShapeDtypeStruct((nc, SSD_STATE, D_SSD), F32)],
        scratch_shapes=[pltpu.VMEM((SSD_STATE, D_SSD), F32)],
    )(xc, proj, spread_x, spread_y, dtb_x, dtb_y, al_x, al_y)


def _ssd_bwd(xc, proj, spread_x, spread_y, dtb_x, dtb_y, al_x, al_y, dy, states, name):
    l = xc.shape[0]
    nc = l // CHUNK
    xc_s, wide_s, dt_s, dt_out_s, spread_s, vec_s, vec_y_s, state_s = _ssd_specs(nc, True)
    pairs_per_group = GROUP_WIDTH // (2 * HEAD_DIM)

    def body(xc_ref, dt_ref, sx_ref, sy_ref, dtbx_ref, dtby_ref, alx_ref, aly_ref, dy_ref, sp_ref,
             dxc_ref, ddtr_ref, st_ref, hs_ref, dstate):
        @pl.when(pl.program_id(0) == 0)
        def _():
            dstate[...] = jnp.zeros_like(dstate)
            st_ref[...] = jnp.zeros_like(st_ref)

        r, c = _chunk_iota()
        halves = _half_masks()
        spread = sx_ref[...]

        def head_sums(v):
            return _split3_dot(_split3_dot_nt(v, spread), spread)

        raw_x, dt_x, a_x, acs_y, ea_x, ds_x, eatot_x = _ssd_decays(dt_ref, sx_ref, sy_ref, dtbx_ref, dtby_ref, alx_ref, aly_ref, r, c)
        xs = xc_ref[:, 0:D_SSD]
        xg = xs * dt_x
        gy = dy_ref[...]
        s_prev = sp_ref[0]
        gea = gy * ea_x
        xds = xg * ds_x
        ds_old = dstate[...]
        later2 = jnp.concatenate([(r <= c).astype(BF16)] * 2, axis=1)
        dxg_parts, state_term, yoff_parts, dadt_parts = [], [], [], []
        for g in range(SSD_GROUPS):
            lanes = slice(g * GROUP_WIDTH, (g + 1) * GROUP_WIDTH)
            b_lo = D_SSD + g * SSD_STATE
            c_lo = D_SSD + (SSD_GROUPS + g) * SSD_STATE
            bb = xc_ref[:, b_lo:b_lo + SSD_STATE].astype(MXU_DTYPE)
            cb = xc_ref[:, c_lo:c_lo + SSD_STATE].astype(MXU_DTYPE)
            scores = _dot_nt(cb, bb)
            dsg = ds_old[:, lanes].astype(MXU_DTYPE)
            gea_b = gea[:, lanes].astype(MXU_DTYPE)
            xds_b = xds[:, lanes].astype(MXU_DTYPE)
            dxg_state = _dot(bb, dsg) * ds_x[:, lanes]
            dc = _dot_nt(gea_b, s_prev[:, lanes].astype(MXU_DTYPE))
            db = _dot_nt(xds_b, dsg)
            dscores = jnp.zeros((CHUNK, CHUNK), F32)
            diag = []
            for j in range(g * pairs_per_group, (g + 1) * pairs_per_group):
                pair = slice(j * 2 * HEAD_DIM, (j + 1) * 2 * HEAD_DIM)
                xg_pair = xg[:, pair].astype(MXU_DTYPE)
                acc = jnp.zeros((CHUNK, 2 * HEAD_DIM), F32)
                cols = []
                for half in range(2):
                    decay = _head_decay(acs_y, 2 * j + half, r, c)
                    g_e = jnp.where(halves[half], gy[:, pair], 0.0).astype(MXU_DTYPE)
                    acc = acc + _dot_tn((scores * decay).astype(MXU_DTYPE), g_e)
                    dm = _dot_nt(g_e, xg_pair) * decay
                    dscores = dscores + dm
                    wq = dm * scores
                    hi = wq.astype(BF16)
                    lo = (wq - hi.astype(F32)).astype(BF16)
                    later = _dot(later2, jnp.concatenate([hi, lo], axis=0))
                    cols.append(jnp.sum(jnp.where(c < r, later, 0.0), axis=1, keepdims=True))
                diag.append(acc)
                dadt_parts.append(jnp.where(halves[0], cols[0], cols[1]))
            dsc_b = dscores.astype(MXU_DTYPE)
            dc = dc + _dot(dsc_b, bb)
            db = db + _dot_tn(dsc_b, cb)
            dxc_ref[:, b_lo:b_lo + SSD_STATE] = db
            dxc_ref[:, c_lo:c_lo + SSD_STATE] = dc
            dxg_parts.append(jnp.concatenate(diag, axis=1) + dxg_state)
            state_term.append(dxg_state)
            yoff_parts.append(_dot(cb, s_prev[:, lanes].astype(MXU_DTYPE)) * ea_x[:, lanes])
            dstate[:, lanes] = ds_old[:, lanes] * eatot_x[:, lanes] + _dot_tn(cb, gea_b)
        dxg = jnp.concatenate(dxg_parts, axis=1)
        dxc_ref[:, 0:D_SSD] = dxg * dt_x
        through_out = _tri_dot((r <= c).astype(BF16), gy * jnp.concatenate(yoff_parts, axis=1))
        through_in = _tri_dot((c < r).astype(BF16), xg * jnp.concatenate(state_term, axis=1))
        carried = jnp.broadcast_to(_colsum(ds_old * s_prev) * eatot_x, (8, D_SSD))
        dadt = (jnp.concatenate(dadt_parts, axis=1) + head_sums(through_out + through_in)
                + jnp.max(head_sums(carried), axis=0, keepdims=True))
        ddt = a_x * dadt + head_sums(dxg * xs)
        draw = ddt * _sigmoid(raw_x)
        ddtr_ref[...] = _split3_dot_nt(draw, spread) * (1.0 / HEAD_DIM)
        st_ref[0:1, :] += _colsum(dt_x * dadt) * a_x
        st_ref[1:2, :] += _colsum(draw)
        st_ref[2:3, :] += _colsum(gy * xs)

        @pl.when(pl.program_id(0) == nc - 1)
        def _():
            hs_ref[...] = head_sums(st_ref[...])

    stats = pl.BlockSpec((8, D_SSD), lambda i: (0, 0))
    return pl.pallas_call(
        body, name=name, grid=(nc,),
        in_specs=[xc_s, dt_s] + spread_s + [vec_s, vec_y_s, vec_s, vec_y_s, wide_s, state_s],
        out_specs=[xc_s, dt_out_s, stats, stats],
        out_shape=[jax.ShapeDtypeStruct((l, D_CONV), F32), jax.ShapeDtypeStruct((l, HEAD_TILE), F32),
                   jax.ShapeDtypeStruct((8, D_SSD), F32), jax.ShapeDtypeStruct((8, D_SSD), F32)],
        scratch_shapes=[pltpu.VMEM((SSD_STATE, D_SSD), F32)],
    )(xc, proj, spread_x, spread_y, dtb_x, dtb_y, al_x, al_y, dy, states)


def _ssd_gate_norm(ycore, xc, z, dskip_x, norm_w, name):
    l = ycore.shape[0]
    tl = _pick(l, 512, 16)
    row = pl.BlockSpec((tl, D_SSD), lambda i: (i, 0))
    vec = pl.BlockSpec((1, D_SSD), lambda i: (0, 0))

    def body(y_ref, xs_ref, z_ref, dk_ref, nw_ref, o_ref):
        zv = z_ref[...]
        yv = (y_ref[...] + dk_ref[...] * xs_ref[...]) * (zv * _sigmoid(zv))
        for g in range(SSD_GROUPS):
            lanes = slice(g * GROUP_WIDTH, (g + 1) * GROUP_WIDTH)
            yg = yv[:, lanes]
            rg = lax.rsqrt(jnp.mean(yg * yg, axis=-1, keepdims=True) + EPS)
            o_ref[:, lanes] = (yg * rg * nw_ref[:, lanes]).astype(o_ref.dtype)

    z_spec = pl.BlockSpec((tl, D_SSD), lambda i: (i, O_Z // D_SSD))
    return pl.pallas_call(body, name=name, grid=(l // tl,), in_specs=[row, row, z_spec, vec, vec], out_specs=row,
                          out_shape=jax.ShapeDtypeStruct((l, D_SSD), MXU_DTYPE))(ycore, xc, z, dskip_x, norm_w)


def _ssd_gate_norm_bwd(dout, ycore, xc, z, dskip_x, norm_w, name):
    l = ycore.shape[0]
    tl = _pick(l, 512, 16)
    nt = l // tl
    row = pl.BlockSpec((tl, D_SSD), lambda i: (i, 0))
    vec = pl.BlockSpec((1, D_SSD), lambda i: (0, 0))

    def body(do_ref, y_ref, xs_ref, z_ref, dk_ref, nw_ref, dyc_ref, dxs_ref, dz_ref, st_ref):
        @pl.when(pl.program_id(0) == 0)
        def _():
            st_ref[...] = jnp.zeros_like(st_ref)

        zv, xs = z_ref[...], xs_ref[...]
        s = _sigmoid(zv)
        gz = zv * s
        yc = y_ref[...] + dk_ref[...] * xs
        yv = yc * gz
        dov = do_ref[...]
        dnw, dyv = [], []
        for g in range(SSD_GROUPS):
            lanes = slice(g * GROUP_WIDTH, (g + 1) * GROUP_WIDTH)
            yg = yv[:, lanes]
            rg = lax.rsqrt(jnp.mean(yg * yg, axis=-1, keepdims=True) + EPS)
            yn = yg * rg
            dnw.append(_colsum(dov[:, lanes] * yn))
            dyn = dov[:, lanes] * nw_ref[:, lanes]
            dyv.append(rg * (dyn - yn * jnp.mean(dyn * yn, axis=-1, keepdims=True)))
        dy = jnp.concatenate(dyv, axis=1)
        dyc = dy * gz
        dyc_ref[...] = dyc
        dxs_ref[...] = dyc * dk_ref[...]
        dz_ref[...] = dy * yc * (s * (1.0 + zv * (1.0 - s)))
        st_ref[0:1, :] += jnp.concatenate(dnw, axis=1)

    return pl.pallas_call(
        body, name=name, grid=(nt,), in_specs=[row, row, row, pl.BlockSpec((tl, D_SSD), lambda i: (i, O_Z // D_SSD)), vec, vec],
        out_specs=[row, row, row, pl.BlockSpec((8, D_SSD), lambda i: (0, 0))],
        out_shape=[jax.ShapeDtypeStruct((l, D_SSD), F32)] * 3 + [jax.ShapeDtypeStruct((8, D_SSD), F32)],
    )(dout, ycore, xc, z, dskip_x, norm_w)


PAIR = 2 * HEAD_DIM
N_PAIRS = N_HEADS // 2
Q_TILE0 = O_Q // PAIR
SB_KEYS = 1024
SB_PAIRS = 4
SB_PAIRS_FWD = 4
SB_SCALE = HEAD_DIM ** -0.5


def _pair_sum(v, lo):
    s_lo = jnp.sum(jnp.where(lo, v, 0.0), axis=-1, keepdims=True)
    s_hi = jnp.sum(jnp.where(lo, 0.0, v), axis=-1, keepdims=True)
    return jnp.where(lo, s_lo, s_hi)


def _qkv_prep(proj, qw2, kw2, name):
    l = proj.shape[0]
    tl = _pick(l, 1024, 16)
    out = pl.BlockSpec((tl, PAIR), lambda i, j: (i, j))
    vec = pl.BlockSpec((1, PAIR), lambda i, j: (0, 0))

    def at(which):
        return pl.BlockSpec((tl, PAIR), lambda i, j: (i, Q_TILE0 + which * N_PAIRS + j))

    def body(q_ref, k_ref, v_ref, qw_ref, kw_ref, qn_ref, kn_ref, vb_ref):
        lo = lax.broadcasted_iota(jnp.int32, (tl, PAIR), 1) < HEAD_DIM
        for t_ref, w_ref, o_ref, scale in ((q_ref, qw_ref, qn_ref, SB_SCALE), (k_ref, kw_ref, kn_ref, 1.0)):
            tv = t_ref[...]
            r = lax.rsqrt(_pair_sum(tv * tv, lo) * (1.0 / HEAD_DIM) + EPS)
            o_ref[...] = ((tv * r) * w_ref[...] * scale).astype(o_ref.dtype)
        vb_ref[...] = v_ref[...].astype(vb_ref.dtype)

    return pl.pallas_call(body, name=name, grid=(l // tl, N_PAIRS), in_specs=[at(0), at(1), at(2), vec, vec],
                          out_specs=[out, out, out], out_shape=[jax.ShapeDtypeStruct((l, D_SB), MXU_DTYPE)] * 3,
                          )(proj, proj, proj, qw2, kw2)


def _qk_norm_bwd(proj, dqn, dkn, qw2, kw2, name):
    l = proj.shape[0]
    tl = _pick(l, 1024, 16)
    out = pl.BlockSpec((tl, PAIR), lambda i, j: (i, j))
    vec = pl.BlockSpec((1, PAIR), lambda i, j: (0, 0))
    stats = pl.BlockSpec((8, PAIR), lambda i, j: (0, 0))

    def at(which):
        return pl.BlockSpec((tl, PAIR), lambda i, j: (i, Q_TILE0 + which * N_PAIRS + j))

    def body(q_ref, k_ref, dqn_ref, dkn_ref, qw_ref, kw_ref, dq_ref, dk_ref, stq_ref, stk_ref):
        @pl.when((pl.program_id(0) == 0) & (pl.program_id(1) == 0))
        def _():
            stq_ref[...] = jnp.zeros_like(stq_ref)
            stk_ref[...] = jnp.zeros_like(stk_ref)

        lo = lax.broadcasted_iota(jnp.int32, (tl, PAIR), 1) < HEAD_DIM
        for t_ref, dn_ref, w_ref, d_ref, st_ref, scale in ((q_ref, dqn_ref, qw_ref, dq_ref, stq_ref, SB_SCALE),
                                                           (k_ref, dkn_ref, kw_ref, dk_ref, stk_ref, 1.0)):
            tv = t_ref[...]
            r = lax.rsqrt(_pair_sum(tv * tv, lo) * (1.0 / HEAD_DIM) + EPS)
            tn = tv * r
            dnv = dn_ref[...] * scale
            dtn = dnv * w_ref[...]
            d_ref[...] = r * (dtn - tn * (_pair_sum(dtn * tn, lo) * (1.0 / HEAD_DIM)))
            st_ref[0:1, :] += _colsum(dnv * tn)

    return pl.pallas_call(body, name=name, grid=(l // tl, N_PAIRS), in_specs=[at(0), at(1), out, out, vec, vec],
                          out_specs=[out, out, stats, stats],
                          out_shape=[jax.ShapeDtypeStruct((l, D_SB), F32)] * 2 + [jax.ShapeDtypeStruct((8, PAIR), F32)] * 2,
                          )(proj, proj, dqn, dkn, qw2, kw2)


def _sb_masks():
    r = lax.broadcasted_iota(jnp.int32, (CHUNK, CHUNK), 0)
    c = lax.broadcasted_iota(jnp.int32, (CHUNK, CHUNK), 1)
    return r, c


def _stack2(mask):
    t = mask.astype(BF16)
    return jnp.concatenate([t, t], axis=0)


def _sb_fwd(q, k, v, shards, name):
    l = q.shape[0]
    nq = l // CHUNK
    kt = _pick(l, SB_KEYS, CHUNK)
    sub = kt // CHUNK
    pairs = SB_PAIRS_FWD
    heads = 2 * pairs
    steps = N_PAIRS // pairs
    na = len(shards)
    qblk = pl.BlockSpec((CHUNK, pairs * PAIR), lambda i, j: (j, i))
    full = pl.BlockSpec((l, pairs * PAIR), lambda i, j: (0, i))
    hbm = pl.BlockSpec(memory_space=pl.ANY)

    def body(q_ref, k_ref, v_ref, *rest):
        o_ref, tot_ref = rest[na:na + 2]
        exchange = _DirectExchange("gather", rest[:na], rest[na + 2:2 * na + 2], *rest[2 * na + 2:])
        qb = pl.program_id(1)

        @pl.when((pl.program_id(0) == 0) & (qb == 0))
        def _():
            exchange.start()

        r, c = _sb_masks()
        after2 = _stack2(r > c)
        halves = _half_masks()
        zero = jnp.zeros((CHUNK, PAIR), q_ref.dtype)
        lanes = [slice((a // 2) * PAIR, (a // 2 + 1) * PAIR) for a in range(heads)]
        qm = [jnp.where(halves[a % 2], q_ref[:, lanes[a]], zero) for a in range(heads)]
        last = (qb * CHUNK) // kt

        def tile(t, width, carries, accs, masked):
            rows = pl.ds(pl.multiple_of(t * width, width), width)
            if masked:
                key_minus_query = (lax.broadcasted_iota(jnp.int32, (CHUNK, width), 1)
                                   - lax.broadcasted_iota(jnp.int32, (CHUNK, width), 0))
                keep = key_minus_query < qb * CHUNK - t * width

            def logits(a):
                lg = _dot_nt(qm[a], k_ref[rows, lanes[a]])
                sp = _softplus_logits(lg)
                return lg - sp, (jnp.where(keep, -sp, 0.0) if masked else -sp)

            def sums(a, lr):
                offset, parts = carries[a], [None] * (width // CHUNK)
                for j in reversed(range(width // CHUNK)):
                    piece = lr[:, j * CHUNK:(j + 1) * CHUNK]
                    parts[j] = _split_dot(piece, after2) + offset
                    offset = offset + jnp.sum(piece, axis=1, keepdims=True)
                return jnp.concatenate(parts, axis=1), offset

            def output(a, ls, cs):
                w = jnp.exp(ls + cs)
                if masked:
                    w = jnp.where(keep, w, 0.0)
                return accs[a] + _dot(w.astype(MXU_DTYPE), v_ref[rows, lanes[a]])

            new_carries, new_accs = [None] * heads, [None] * heads
            ls, lr = logits(0)
            for a in range(heads):
                cs, new_carries[a] = sums(a, lr)
                if a + 1 < heads:
                    ls_next, lr = logits(a + 1)
                new_accs[a] = output(a, ls, cs)
                ls = ls_next
            return tuple(new_carries), tuple(new_accs)

        carries = tuple(jnp.zeros((CHUNK, 1), F32) for _ in range(heads))
        accs = tuple(jnp.zeros((CHUNK, PAIR), F32) for _ in range(heads))
        if sub % 2 == 0:
            half = kt // 2
            carries, accs = lax.cond((qb * CHUNK) % kt >= half,
                                     lambda cr, ac: tile(2 * last + 1, half, cr, ac, True), lambda cr, ac: (cr, ac), carries, accs)
            carries, accs = tile(2 * last, half, carries, accs, True)
        else:
            carries, accs = tile(last, kt, carries, accs, True)
        carries, accs = lax.fori_loop(1, last + 1, lambda i, st: tile(last - i, kt, st[0], st[1], False), (carries, accs))
        for p in range(pairs):
            o_ref[:, lanes[2 * p]] = jnp.where(halves[0], accs[2 * p], accs[2 * p + 1])
            tot_ref[:, lanes[2 * p]] = jnp.where(halves[0], carries[2 * p], carries[2 * p + 1])

        @pl.when((pl.program_id(0) == steps - 1) & (qb == nq - 1))
        def _():
            exchange.wait()

    return pl.pallas_call(
        body, name=name, grid=(steps, nq), in_specs=[qblk, full, full] + [hbm] * na,
        out_specs=[qblk, qblk] + [hbm] * na,
        out_shape=[jax.ShapeDtypeStruct((l, D_SB), F32), jax.ShapeDtypeStruct((l, D_SB), F32)]
        + [jax.ShapeDtypeStruct((N_DEV,) + s.shape, s.dtype) for s in shards],
        scratch_shapes=_DirectExchange.scratch(na),
    )(q, k, v, *shards)


def _sb_bwd(q, k, v, dycat, tot, blocks, name):
    l = q.shape[0]
    nq = l // CHUNK
    kt = _pick(l, SB_KEYS, CHUNK)
    sub = kt // CHUNK
    heads = 2 * SB_PAIRS
    steps = N_PAIRS // SB_PAIRS
    na = len(blocks)
    width_all = SB_PAIRS * PAIR
    qblk = pl.BlockSpec((CHUNK, width_all), lambda i, j: (j, i))
    doblk = pl.BlockSpec((CHUNK, width_all), lambda i, j: (j, D_SSD // width_all + i))
    full = pl.BlockSpec((l, width_all), lambda i, j: (0, i))
    resident = pl.BlockSpec((l, width_all), lambda i, j: (0, i), pipeline_mode=pl.Buffered(1))
    hbm = pl.BlockSpec(memory_space=pl.ANY)

    def body(q_ref, k_ref, v_ref, do_ref, tot_ref, *rest):
        dq_ref, dk_ref, dv_ref = rest[na:na + 3]
        exchange = _DirectExchange("scatter", rest[:na], rest[na + 3:2 * na + 3], *rest[2 * na + 3:])
        qb = pl.program_id(1)

        @pl.when((pl.program_id(0) == 0) & (qb == 0))
        def _():
            exchange.start()

        @pl.when(qb == 0)
        def _():
            dk_ref[...] = jnp.zeros_like(dk_ref)
            dv_ref[...] = jnp.zeros_like(dv_ref)

        r, c = _sb_masks()
        upto2 = _stack2(r <= c)
        before2 = _stack2(r < c)
        halves = _half_masks()
        lanes = [slice((a // 2) * PAIR, (a // 2 + 1) * PAIR) for a in range(heads)]
        qm = [jnp.where(halves[a % 2], q_ref[:, lanes[a]], jnp.zeros((CHUNK, PAIR), q_ref.dtype)) for a in range(heads)]
        dom = [jnp.where(halves[a % 2], do_ref[:, lanes[a]], 0.0).astype(MXU_DTYPE) for a in range(heads)]
        total = [jnp.max(jnp.where(halves[a % 2], tot_ref[:, lanes[a]], -jnp.inf), axis=1, keepdims=True) for a in range(heads)]
        last = (qb * CHUNK) // kt

        def prefix(values, tri2, offset):
            parts = []
            for j in range(values.shape[1] // CHUNK):
                piece = values[:, j * CHUNK:(j + 1) * CHUNK]
                parts.append(_split_dot(piece, tri2) + offset)
                offset = offset + jnp.sum(piece, axis=1, keepdims=True)
            return jnp.concatenate(parts, axis=1), offset

        def tile(t, width, carry_p, carry_d, dq, masked):
            rows = pl.ds(pl.multiple_of(t * width, width), width)
            if masked:
                key_minus_query = (lax.broadcasted_iota(jnp.int32, (CHUNK, width), 1)
                                   - lax.broadcasted_iota(jnp.int32, (CHUNK, width), 0))
                keep = key_minus_query < qb * CHUNK - t * width

            def log_terms(a):
                lg = _dot_nt(qm[a], k_ref[rows, lanes[a]])
                sp = _softplus_logits(lg)
                return lg - sp, (jnp.where(keep, -sp, 0.0) if masked else -sp)

            def weights(a, ls, lr):
                p_incl, p_next = prefix(lr, upto2, carry_p[a])
                w = jnp.exp(ls + (total[a] - p_incl))
                if masked:
                    w = jnp.where(keep, w, 0.0)
                return w, p_next

            def weight_grads(a, w):
                return _dot_nt(dom[a], v_ref[rows, lanes[a]]) * w

            def gradients(a, ls, w, da):
                d_excl, d_next = prefix(da, before2, carry_d[a])
                sig = jnp.exp(ls)
                dl = da * (1.0 - sig) - d_excl * sig
                if masked:
                    dl = jnp.where(keep, dl, 0.0)
                dl_b = dl.astype(MXU_DTYPE)
                return (d_next, dq[a] + _dot(dl_b, k_ref[rows, lanes[a]]),
                        _dot_tn(w.astype(MXU_DTYPE), dom[a]), _dot_tn(dl_b, qm[a]))

            new_p, new_d, new_dq = [None] * heads, [None] * heads, [None] * heads
            dv_upd, dk_upd = [None] * heads, [None] * heads
            ls, lr = log_terms(0)
            w, new_p[0] = weights(0, ls, lr)
            for a in range(heads):
                if a + 1 < heads:
                    ls_next, lr_next = log_terms(a + 1)
                da = weight_grads(a, w)
                if a + 1 < heads:
                    w_next, new_p[a + 1] = weights(a + 1, ls_next, lr_next)
                new_d[a], new_dq[a], dv_upd[a], dk_upd[a] = gradients(a, ls, w, da)
                if a % 2 == 1:
                    dv_ref[rows, lanes[a]] += dv_upd[a - 1] + dv_upd[a]
                    dk_ref[rows, lanes[a]] += dk_upd[a - 1] + dk_upd[a]
                if a + 1 < heads:
                    ls, w = ls_next, w_next
            return tuple(new_p), tuple(new_d), tuple(new_dq)

        zeros = tuple(jnp.zeros((CHUNK, 1), F32) for _ in range(heads))
        dq0 = tuple(jnp.zeros((CHUNK, PAIR), F32) for _ in range(heads))
        state = lax.fori_loop(0, last, lambda t, st: tile(t, kt, st[0], st[1], st[2], False), (zeros, zeros, dq0))
        if sub % 2 == 0:
            half = kt // 2
            state = tile(2 * last, half, *state, True)
            state = lax.cond((qb * CHUNK) % kt >= half,
                             lambda cp, cd, dq: tile(2 * last + 1, half, cp, cd, dq, True), lambda cp, cd, dq: (cp, cd, dq), *state)
        else:
            state = tile(last, kt, *state, True)
        for p in range(SB_PAIRS):
            dq_ref[:, lanes[2 * p]] = jnp.where(halves[0], state[2][2 * p], state[2][2 * p + 1])

        @pl.when((pl.program_id(0) == steps - 1) & (qb == nq - 1))
        def _():
            exchange.wait()

    return pl.pallas_call(
        body, name=name, grid=(steps, nq),
        in_specs=[qblk, full, full, doblk, qblk] + [hbm] * na,
        out_specs=[qblk, resident, resident] + [hbm] * na,
        out_shape=[jax.ShapeDtypeStruct((l, D_SB), F32)] * 3 + [jax.ShapeDtypeStruct(b.shape, b.dtype) for b in blocks],
        scratch_shapes=_DirectExchange.scratch(na),
    )(q, k, v, dycat, tot, *blocks)


def _ada_fwd(c_all, w_shard, name):
    def body(c_ref, w_ref, o_ref):
        cv = c_ref[...]
        o_ref[...] = _dot(cv * _sigmoid(cv), w_ref[...], HIGHEST)

    return pl.pallas_call(body, name=name, out_shape=jax.ShapeDtypeStruct((c_all.shape[0], w_shard.shape[1]), F32))(c_all, w_shard)


def _ada_bwd(c_all, dmod_shard, name):
    def body(c_ref, d_ref, o_ref):
        cv = c_ref[...]
        o_ref[...] = _dot_tn(cv * _sigmoid(cv), d_ref[...], HIGHEST)

    return pl.pallas_call(body, name=name, out_shape=jax.ShapeDtypeStruct((c_all.shape[1], dmod_shard.shape[1]), F32))(c_all, dmod_shard)


def _sum_small(parts, name):
    def body(p_ref, o_ref):
        acc = p_ref[0]
        for d in range(1, N_DEV):
            acc = acc + p_ref[d]
        o_ref[...] = acc

    return pl.pallas_call(body, name=name, out_shape=jax.ShapeDtypeStruct(parts.shape[1:], F32))(parts)


def _adamw(w, g, m, v, name):
    rows, cols = w.shape
    tr = _pick(rows, 256, 8)
    spec = pl.BlockSpec((tr, cols), lambda i: (i, 0))
    bc1 = 1.0 - ADAM_B1 ** ADAM_STEP
    bc2 = 1.0 - ADAM_B2 ** ADAM_STEP

    def body(w_ref, g_ref, m_ref, v_ref, d_ref, nm_ref, nv_ref):
        gv = g_ref[...]
        nm = ADAM_B1 * m_ref[...] + (1.0 - ADAM_B1) * gv
        nv = ADAM_B2 * v_ref[...] + (1.0 - ADAM_B2) * (gv * gv)
        nm_ref[...] = nm
        nv_ref[...] = nv
        d_ref[...] = -ADAM_LR * ((nm / bc1) / (jnp.sqrt(nv / bc2) + ADAM_EPS) + ADAM_WD * w_ref[...])

    return pl.pallas_call(body, name=name, grid=(rows // tr,), in_specs=[spec] * 4, out_specs=[spec] * 3,
                          out_shape=[jax.ShapeDtypeStruct((rows, cols), F32)] * 3)(w, g, m, v)


def _pad_w_in_t(w_in_t):
    lo = D_SSD + D_CONV
    return jnp.concatenate([w_in_t[D_SSD:lo + N_HEADS], jnp.zeros((DT_PAD - N_HEADS, w_in_t.shape[1]), w_in_t.dtype),
                            w_in_t[:D_SSD], w_in_t[lo + N_HEADS:]], axis=0)


def _unpad_w_in_t(g):
    return jnp.concatenate([g[O_Z:O_Q], g[:O_DT + N_HEADS], g[O_Q:]], axis=0)


def _local_step(x, target, mod, norm1_w, w_in_tp, conv_w, conv_b, dt_bias, a_log, d_skip, ssd_norm_w, q_norm_w, k_norm_w,
                norm2_w, later_shards):
    sh1, sc1, g1, sh2, sc2, g2 = [mod[:, i * D_MODEL:(i + 1) * D_MODEL] for i in range(N_MOD)]
    qw2, kw2 = jnp.tile(q_norm_w, (1, 2)), jnp.tile(k_norm_w, (1, 2))
    dskip_x = jnp.repeat(d_skip, HEAD_DIM, axis=1)
    dtb_x, dtb_y = jnp.repeat(dt_bias, HEAD_DIM, axis=1), jnp.repeat(dt_bias, HEAD_TILE, axis=1)
    al_x, al_y = jnp.repeat(a_log, HEAD_DIM, axis=1), jnp.repeat(a_log, HEAD_TILE, axis=1)
    head_ids = jnp.arange(HEAD_TILE, dtype=jnp.int32)[:, None]
    spread_x = (head_ids == jnp.arange(D_SSD, dtype=jnp.int32)[None, :] // HEAD_DIM).astype(BF16)
    spread_y = (head_ids == jnp.arange(D_HEAD_TILES, dtype=jnp.int32)[None, :] // HEAD_TILE).astype(BF16)

    h1 = _rms_mod(x, norm1_w, sc1, sh1, "rms_mod1")
    proj = _matmul_nt(h1, w_in_tp, "in_proj")

    xc = _conv_silu(proj, conv_w, conv_b, "conv_silu")
    ycore, states = _ssd_fwd(xc, proj, spread_x, spread_y, dtb_x, dtb_y, al_x, al_y, "ssd_fwd")
    y_ssd = _ssd_gate_norm(ycore, xc, proj, dskip_x, ssd_norm_w, "ssd_gate_norm")

    qn, kn, vb = _qkv_prep(proj, qw2, kw2, "qkv_prep")
    o_sb, tot, *gathered = _sb_fwd(qn, kn, vb, later_shards, "sb_fwd")
    w_out, w_gate_t, w_up_t, w_down = [g.reshape(N_DEV * g.shape[1], D_MODEL) for g in gathered]
    ycat = jnp.concatenate([y_ssd, o_sb.astype(MXU_DTYPE)], axis=1)

    mix = _matmul(ycat, w_out, "out_proj")
    x1, h2 = _residual_rms_mod(x, mix, g1, norm2_w, sc2, sh2, "residual_rms_mod2")
    gate, up, act = _ffn_in(h2, w_gate_t, w_up_t, "ffn_in")
    ffn = _matmul(act, w_down, "ffn_down")
    dy, dffn, st_g2, loss_blk = _loss_head(x1, ffn, g2, target, "loss_head")

    g_down = _matmul_tn(act, dffn, "ffn_down_dw", MXU_DTYPE)
    dgate, dup = _ffn_down_bwd(dffn, w_down, gate, up, "ffn_down_dx")
    g_gate_t = _matmul_tn(dgate, h2, "ffn_gate_dw", MXU_DTYPE)
    g_up_t = _matmul_tn(dup, h2, "ffn_up_dw", MXU_DTYPE)
    dh2 = _matmul_sum2(dgate, w_gate_t, dup, w_up_t, "ffn_dh")
    dx1, dmix, st_n2 = _norm_bwd(x1, dh2, dy, norm2_w, sc2, "norm2_bwd", gated=(mix, g1))

    g_out = _matmul_tn(ycat, dmix, "out_proj_dw", MXU_DTYPE)
    dycat = _matmul_nt(dmix, w_out, "out_proj_dx")

    partials = [g.reshape(N_DEV, g.shape[0] // N_DEV, D_MODEL).astype(BF16) for g in (g_out, g_gate_t, g_up_t, g_down)]
    dqn, dkn, dv, *slots = _sb_bwd(qn, kn, vb, dycat, tot, partials, "sb_bwd")
    dq, dk, st_q, st_k = _qk_norm_bwd(proj, dqn, dkn, qw2, kw2, "qk_norm_bwd")

    dycore, dxs_skip, dz, st_gn = _ssd_gate_norm_bwd(dycat, ycore, xc, proj, dskip_x, ssd_norm_w, "ssd_gate_norm_bwd")
    dxc, ddt_tile, st_ssd, st_heads = _ssd_bwd(xc, proj, spread_x, spread_y, dtb_x, dtb_y, al_x, al_y, dycore, states, "ssd_bwd")
    dxc = jnp.concatenate([dxc[:, :D_SSD] + dxs_skip, dxc[:, D_SSD:]], axis=1)
    dpre, st_conv = _conv_silu_bwd_pre(proj, dxc, conv_w, conv_b, "conv_silu_bwd")
    dxbc = _conv_bwd_input(dpre, conv_w, "conv_bwd_input")

    pieces = {"z": (dz, O_Z), "xbc": (dxbc, O_XBC), "dt": (ddt_tile, O_DT), "q": (dq, O_Q), "k": (dk, O_Q + D_SB), "v": (dv, O_Q + 2 * D_SB)}
    g_rows = {n: _matmul_tn(p, h1, "in_proj_dw_" + n, MXU_DTYPE) for n, (p, _) in pieces.items()}
    g_rows["dt"] = g_rows["dt"][:N_HEADS]
    g_in_t = jnp.concatenate(list(g_rows.values()), axis=0)
    g_in_blocks = g_in_t.reshape(N_DEV, g_in_t.shape[0] // N_DEV, D_MODEL).astype(BF16)
    dh1, g_in_slots = _matmul_pieces_beside_scatter([p for p, _ in pieces.values()], [o for _, o in pieces.values()], w_in_tp,
                                                    [g_in_blocks], "in_proj_dx")
    slots = [g_in_slots] + list(slots)
    grad_x, st_n1 = _norm_bwd(x, dh1, dx1, norm1_w, sc1, "norm1_bwd")

    pad = jnp.zeros((1, SM_SSD_NORM - SM_D_SKIP - N_HEADS), F32)
    small = jnp.concatenate(
        [st_n1[1:2], st_n1[0:1], st_n2[3:4], st_n2[1:2], st_n2[0:1], st_g2[0:1],
         st_n1[2:3], st_conv[4:5], st_conv[0:4].reshape(1, CONV_WIDTH * D_CONV),
         st_ssd[1:2, ::HEAD_DIM], st_ssd[0:1, ::HEAD_DIM], st_heads[2:3, ::HEAD_DIM], pad,
         st_gn[0:1], st_q[0:1, :HEAD_DIM] + st_q[0:1, HEAD_DIM:], st_k[0:1, :HEAD_DIM] + st_k[0:1, HEAD_DIM:], st_n2[2:3]], axis=1)
    return loss_blk, grad_x, slots, small


def kernel(x, c, w_ada, b_ada, norm1_w, w_in, conv_w, conv_b, dt_bias, a_log, d_skip, ssd_norm_w, q_norm_w, k_norm_w, w_out, norm2_w, w_gate, w_up, w_down, loss_target, m_w_ada, m_b_ada, m_norm1_w, m_w_in, m_conv_w, m_conv_b, m_dt_bias, m_a_log, m_d_skip, m_ssd_norm_w, m_q_norm_w, m_k_norm_w, m_w_out, m_norm2_w, m_w_gate, m_w_up, m_w_down, v_w_ada, v_b_ada, v_norm1_w, v_w_in, v_conv_w, v_conv_b, v_dt_bias, v_a_log, v_d_skip, v_ssd_norm_w, v_q_norm_w, v_k_norm_w, v_w_out, v_norm2_w, v_w_gate, v_w_up, v_w_down):
    me = 4 * lax.axis_index("x") + 2 * lax.axis_index("y") + lax.axis_index("c")
    conv_cols = D_CONV // N_DEV
    ada_cols = N_MOD * D_MODEL // N_DEV

    (w_in_t,) = _all_gather_big([w_in[0].T.astype(BF16)], "gather_w_in")
    w_in_tp = _pad_w_in_t(w_in_t.reshape(D_IN_PROJ, D_MODEL))
    later_shards = [s.astype(BF16) for s in (w_out[0], w_gate[0].T, w_up[0].T, w_down[0])]

    first = _all_gather_small(jnp.concatenate([c, conv_w[0].reshape(1, CONV_WIDTH * conv_cols)], axis=1), "gather_cond")
    c_all = first[:, 0, :D_MODEL]
    conv_w_f = first[:, 0, D_MODEL:].reshape(N_DEV, CONV_WIDTH, conv_cols).transpose(1, 0, 2).reshape(CONV_WIDTH, D_CONV)
    mod_cols = _ada_fwd(c_all, w_ada[0], "ada_fwd")
    mods = _all_gather_small(mod_cols.reshape(1, N_DEV * ada_cols), "gather_mod")
    mod = lax.dynamic_index_in_dim(mods.reshape(N_DEV, N_DEV, ada_cols), me, axis=1, keepdims=False)
    mod = mod.reshape(1, N_MOD * D_MODEL) + b_ada

    loss_blk, grad_x, slots, small = _local_step(
        x[0], loss_target[0], mod, norm1_w, w_in_tp, conv_w_f, conv_b, dt_bias, a_log, d_skip, ssd_norm_w, q_norm_w, k_norm_w,
        norm2_w, later_shards)
    loss = lax.psum(loss_blk[0, 0], ("x", "y", "c"))

    g_in_t, g_out, g_gate_t, g_up_t, g_down = [_sum_slots(s, "sum_grads_" + n) for s, n in zip(slots, ("in", "out", "gate", "up", "down"))]
    g_in, g_gate, g_up = g_in_t.T, g_gate_t.T, g_up_t.T

    parts = _all_gather_small(small, "gather_small")
    gsum = _sum_small(parts, "sum_small")
    dmod_shard = lax.dynamic_slice_in_dim(parts[:, 0, :N_MOD * D_MODEL], me * ada_cols, ada_cols, axis=1)
    g_ada = _ada_bwd(c_all, dmod_shard, "ada_bwd")
    g_conv_w = lax.dynamic_slice_in_dim(gsum[:, SM_CONV_W:SM_DT_BIAS].reshape(CONV_WIDTH, D_CONV), me * conv_cols, conv_cols, axis=1)

    def pack_small(b_ada_, norm1_, conv_b_, dt_bias_, a_log_, d_skip_, ssd_norm_, q_norm_, k_norm_, norm2_):
        return jnp.concatenate(
            [b_ada_, norm1_, conv_b_, jnp.zeros((1, CONV_WIDTH * D_CONV), F32), dt_bias_, a_log_, d_skip_,
             jnp.zeros((1, SM_SSD_NORM - SM_D_SKIP - N_HEADS), F32), ssd_norm_, q_norm_, k_norm_, norm2_], axis=1)

    def unpack_small(p):
        return {
            "b_ada": p[:, SM_B_ADA:SM_NORM1], "norm1_w": p[:, SM_NORM1:SM_CONV_B], "conv_b": p[:, SM_CONV_B:SM_CONV_W],
            "dt_bias": p[:, SM_DT_BIAS:SM_A_LOG], "a_log": p[:, SM_A_LOG:SM_D_SKIP], "d_skip": p[:, SM_D_SKIP:SM_D_SKIP + N_HEADS],
            "ssd_norm_w": p[:, SM_SSD_NORM:SM_Q_NORM], "q_norm_w": p[:, SM_Q_NORM:SM_K_NORM], "k_norm_w": p[:, SM_K_NORM:SM_NORM2],
            "norm2_w": p[:, SM_NORM2:SM_TOTAL]}

    w_small = pack_small(b_ada, norm1_w, conv_b, dt_bias, a_log, d_skip, ssd_norm_w, q_norm_w, k_norm_w, norm2_w)
    m_small = pack_small(m_b_ada, m_norm1_w, m_conv_b, m_dt_bias, m_a_log, m_d_skip, m_ssd_norm_w, m_q_norm_w, m_k_norm_w, m_norm2_w)
    v_small = pack_small(v_b_ada, v_norm1_w, v_conv_b, v_dt_bias, v_a_log, v_d_skip, v_ssd_norm_w, v_q_norm_w, v_k_norm_w, v_norm2_w)
    small_out = [unpack_small(t) for t in (gsum,) + tuple(_adamw(w_small, gsum, m_small, v_small, "adamw_small"))]

    sharded = {
        "w_ada": (w_ada[0], g_ada, m_w_ada[0], v_w_ada[0]),
        "w_in": (w_in[0], g_in, m_w_in[0], v_w_in[0]),
        "conv_w": (conv_w[0], g_conv_w, m_conv_w[0], v_conv_w[0]),
        "w_out": (w_out[0], g_out, m_w_out[0], v_w_out[0]),
        "w_gate": (w_gate[0], g_gate, m_w_gate[0], v_w_gate[0]),
        "w_up": (w_up[0], g_up, m_w_up[0], v_w_up[0]),
        "w_down": (w_down[0], g_down, m_w_down[0], v_w_down[0]),
    }
    sharded_out = {n: (t[1],) + tuple(_adamw(*t, "adamw_" + n)) for n, t in sharded.items()}

    names = ["w_ada", "b_ada", "norm1_w", "w_in", "conv_w", "conv_b", "dt_bias", "a_log", "d_skip", "ssd_norm_w", "q_norm_w",
             "k_norm_w", "w_out", "norm2_w", "w_gate", "w_up", "w_down"]
    outs = [loss, grad_x[None]]
    for kind in range(4):
        for n in names:
            outs.append(sharded_out[n][kind][None] if n in sharded_out else small_out[kind][n])
    return tuple(outs)
```

```python
import functools

import jax
import jax.numpy as jnp
from jax import lax
from jax.experimental import pallas as pl
from jax.experimental.pallas import tpu as pltpu

F32 = jnp.float32
BF16 = jnp.bfloat16
MXU_DTYPE = jnp.bfloat16
HIGHEST = lax.Precision.HIGHEST
MESH_IDS = pl.DeviceIdType.MESH

N_DEV = 8
D_MODEL = 1024
HEAD_DIM = 64
N_HEADS = 16
D_SSD = 1024
D_SB = 1024
SSD_GROUPS = 2
SSD_STATE = 128
GROUP_WIDTH = D_SSD // SSD_GROUPS
D_CONV = D_SSD + 2 * SSD_GROUPS * SSD_STATE
CONV_WIDTH = 4
CHUNK = 128
D_FF = 2816
N_MOD = 6
EPS = 1e-6
D_IN_PROJ = 5648
O_XBC = 0
O_DT = D_CONV
DT_PAD = 512
O_Z = O_DT + DT_PAD
O_Q = O_Z + D_SSD
D_IN_PAD = O_Q + 3 * D_SB
HALO = 8

ADAM_LR = 0.001
ADAM_B1 = 0.9
ADAM_B2 = 0.999
ADAM_EPS = 1e-08
ADAM_WD = 0.01
ADAM_STEP = 10

SM_B_ADA = 0
SM_NORM1 = 6144
SM_CONV_B = 7168
SM_CONV_W = 8704
SM_DT_BIAS = 14848
SM_A_LOG = 14864
SM_D_SKIP = 14880
SM_SSD_NORM = 14976
SM_Q_NORM = 16000
SM_K_NORM = 16064
SM_NORM2 = 16128
SM_TOTAL = 17152


def _pick(n, cap, mult):
    if n <= cap:
        return n
    best = None
    for t in range(mult, cap + 1, mult):
        if n % t == 0:
            best = t
    assert best is not None, (n, cap, mult)
    return best


def _dot(a, b, precision=None):
    return jnp.dot(a, b, preferred_element_type=F32, precision=precision)


def _dot_nt(a, b, precision=None):
    return lax.dot_general(a, b, (((1,), (1,)), ((), ())), preferred_element_type=F32, precision=precision)


def _dot_tn(a, b, precision=None):
    return lax.dot_general(a, b, (((0,), (0,)), ((), ())), preferred_element_type=F32, precision=precision)


def _softplus(v):
    return jnp.maximum(v, 0.0) + jnp.log1p(jnp.exp(-jnp.abs(v)))


def _softplus_logits(v):
    return jnp.maximum(v, 0.0) + jnp.log(1.0 + jnp.exp(-jnp.abs(v)))


def _tri_dot(tri_b, v):
    hi = v.astype(BF16)
    r1 = v - hi.astype(F32)
    mid = r1.astype(BF16)
    lo = (r1 - mid.astype(F32)).astype(BF16)
    return _dot(tri_b, hi) + _dot(tri_b, mid) + _dot(tri_b, lo)


def _sigmoid(v):
    return jax.nn.sigmoid(v)


def _colsum(v):
    return jnp.sum(v, axis=0, keepdims=True)


def _split_dot(v, tri2):
    hi = v.astype(BF16)
    lo = (v - hi.astype(F32)).astype(BF16)
    return _dot(jnp.concatenate([hi, lo], axis=1), tri2)


def _position():
    x, y, c = lax.axis_index("x"), lax.axis_index("y"), lax.axis_index("c")
    return x, y, c


def _peer(x, y, c, k):
    px = 1 - x if (k >> 2) & 1 else x
    py = 1 - y if (k >> 1) & 1 else y
    pc = 1 - c if k & 1 else c
    return px, py, pc


def _all_gather_small(v, name):
    n = v.shape[1]

    def body(v_ref, out_ref, send_sems, recv_sems, local_sem):
        x, y, c = _position()
        me = 4 * x + 2 * y + c
        mine = pltpu.make_async_copy(v_ref, out_ref.at[me], local_sem)
        mine.start()
        sends = []
        for k in range(1, N_DEV):
            cp = pltpu.make_async_remote_copy(
                src_ref=v_ref, dst_ref=out_ref.at[me], send_sem=send_sems.at[k - 1], recv_sem=recv_sems.at[k - 1],
                device_id=_peer(x, y, c, k), device_id_type=MESH_IDS)
            cp.start()
            sends.append(cp)
        for k in range(1, N_DEV):
            px, py, pc = _peer(x, y, c, k)
            pltpu.make_async_remote_copy(
                src_ref=v_ref, dst_ref=out_ref.at[4 * px + 2 * py + pc], send_sem=send_sems.at[k - 1],
                recv_sem=recv_sems.at[k - 1], device_id=(px, py, pc), device_id_type=MESH_IDS).wait_recv()
        for cp in sends:
            cp.wait_send()
        mine.wait()

    return pl.pallas_call(
        body, name=name,
        out_shape=jax.ShapeDtypeStruct((N_DEV, 1, n), v.dtype),
        in_specs=[pl.BlockSpec(memory_space=pltpu.VMEM)],
        out_specs=pl.BlockSpec(memory_space=pltpu.VMEM),
        scratch_shapes=[pltpu.SemaphoreType.DMA((N_DEV - 1,)), pltpu.SemaphoreType.DMA((N_DEV - 1,)),
                        pltpu.SemaphoreType.DMA],
    )(v)


def _prologue(cond_conv, w_ada_shard, w_in_shard, name):
    def body(cc_ref, wada_ref, win_ref, first_ref, mods_ref, wout_ref, mod_cols, conds_all, big_send, big_recv, big_local,
             a_send, a_recv, a_local, b_send, b_recv, b_local):
        x, y, c = _position()
        me, sibling = (x, y, c), (x, y, 1 - c)
        chips = [(1 - x, y), (x, 1 - y), (1 - x, 1 - y)]

        def big(k, blk, to, own=False):
            slot = wout_ref.at[4 * blk[0] + 2 * blk[1] + blk[2]]
            return pltpu.make_async_remote_copy(src_ref=win_ref if own else slot, dst_ref=slot, send_sem=big_send.at[k],
                                                recv_sem=big_recv.at[k], device_id=to, device_id_type=MESH_IDS)

        mine = pltpu.make_async_copy(win_ref, wout_ref.at[4 * x + 2 * y + c], big_local.at[0])
        mine.start()
        first = [big(0, me, sibling, own=True)] + [big(1 + j, me, (*chip, c), own=True) for j, chip in enumerate(chips)]
        for cp in first:
            cp.start()

        conds = _DirectExchange("gather", [cc_ref], [first_ref], a_send, a_recv, a_local)
        conds.start()
        conds.wait()
        for d in range(N_DEV):
            conds_all[d:d + 1, :] = first_ref[d, :, 0:D_MODEL]
        cv = conds_all[...]
        mod_cols[...] = _dot(cv * _sigmoid(cv), wada_ref[...], HIGHEST)
        mods = _DirectExchange("gather", [mod_cols], [mods_ref], b_send, b_recv, b_local)
        mods.start()
        mods.wait()

        passed = []
        for j, chip in enumerate(chips):
            big(1 + j, (*chip, c), me).wait_recv()
            passed.append(big(4 + j, (*chip, c), sibling))
            passed[-1].start()
        big(0, sibling, me).wait_recv()
        for j, chip in enumerate(chips):
            big(4 + j, (*chip, 1 - c), me).wait_recv()
        for cp in first + passed:
            cp.wait_send()
        mine.wait()

    vmem = pl.BlockSpec(memory_space=pltpu.VMEM)
    hbm = pl.BlockSpec(memory_space=pl.ANY)
    cols = w_ada_shard.shape[1]
    return pl.pallas_call(
        body, name=name,
        out_shape=[jax.ShapeDtypeStruct((N_DEV,) + cond_conv.shape, F32), jax.ShapeDtypeStruct((N_DEV, N_DEV, cols), F32),
                   jax.ShapeDtypeStruct((N_DEV,) + w_in_shard.shape, w_in_shard.dtype)],
        in_specs=[vmem, vmem, hbm], out_specs=[vmem, vmem, hbm],
        scratch_shapes=[pltpu.VMEM((N_DEV, cols), F32), pltpu.VMEM((N_DEV, D_MODEL), F32)] + _DirectExchange.scratch(1) * 3,
    )(cond_conv, w_ada_shard, w_in_shard)


class _DirectExchange:
    def __init__(self, kind, src_refs, dst_refs, send_sems, recv_sems, local_sems):
        self.kind, self.src_refs, self.dst_refs = kind, src_refs, dst_refs
        self.send_sems, self.recv_sems, self.local_sems = send_sems, recv_sems, local_sems

    @staticmethod
    def scratch(n_arrays):
        copies = N_DEV - 1
        return [pltpu.SemaphoreType.DMA((copies * n_arrays,)), pltpu.SemaphoreType.DMA((copies * n_arrays,)),
                pltpu.SemaphoreType.DMA((n_arrays,))]

    def _copies(self):
        x, y, c = _position()
        me = 4 * x + 2 * y + c
        local, sends, arrivals = [], [], []
        for a, (src, dst) in enumerate(zip(self.src_refs, self.dst_refs)):
            own = src if self.kind == "gather" else src.at[me]
            local.append(pltpu.make_async_copy(own, dst.at[me], self.local_sems.at[a]))
            for k in range(1, N_DEV):
                px, py, pc = _peer(x, y, c, k)
                peer = 4 * px + 2 * py + pc
                sems = dict(send_sem=self.send_sems.at[(N_DEV - 1) * a + k - 1], recv_sem=self.recv_sems.at[(N_DEV - 1) * a + k - 1],
                            device_id=(px, py, pc), device_id_type=MESH_IDS)
                sends.append(pltpu.make_async_remote_copy(
                    src_ref=src if self.kind == "gather" else src.at[peer], dst_ref=dst.at[me], **sems))
                arrivals.append(pltpu.make_async_remote_copy(src_ref=own, dst_ref=dst.at[peer], **sems))
        return local, sends, arrivals

    def start(self):
        local, sends, _ = self._copies()
        for cp in local + sends:
            cp.start()

    def wait(self):
        local, sends, arrivals = self._copies()
        for cp in arrivals:
            cp.wait_recv()
        for cp in sends:
            cp.wait_send()
        for cp in local:
            cp.wait()


def _sum_slots(slots, name):
    _, r, n = slots.shape
    tr = _pick(r, 256, 16) if r % 16 == 0 else r

    def body(s_ref, o_ref):
        acc = s_ref[0].astype(F32)
        for d in range(1, N_DEV):
            acc = acc + s_ref[d].astype(F32)
        o_ref[...] = acc

    return pl.pallas_call(
        body, name=name, grid=(r // tr,),
        in_specs=[pl.BlockSpec((N_DEV, tr, n), lambda i: (0, i, 0))],
        out_specs=pl.BlockSpec((tr, n), lambda i: (i, 0)),
        out_shape=jax.ShapeDtypeStruct((r, n), F32),
    )(slots)


def _matmul(a, b, name, out_dtype=F32):
    m, k = a.shape
    _, n = b.shape
    tm, tn, tk = _pick(m, 1024, 16), _pick(n, 1408, 128), _pick(k, 1408, 128)
    nk = k // tk

    def body(a_ref, b_ref, o_ref, acc_ref):
        kk = pl.program_id(2)

        @pl.when(kk == 0)
        def _():
            acc_ref[...] = jnp.zeros_like(acc_ref)

        acc_ref[...] += _dot(a_ref[...].astype(MXU_DTYPE), b_ref[...].astype(MXU_DTYPE))

        @pl.when(kk == nk - 1)
        def _():
            o_ref[...] = acc_ref[...].astype(o_ref.dtype)

    return pl.pallas_call(
        body, name=name, grid=(m // tm, n // tn, nk),
        in_specs=[pl.BlockSpec((tm, tk), lambda i, j, kk: (i, kk)), pl.BlockSpec((tk, tn), lambda i, j, kk: (kk, j))],
        out_specs=pl.BlockSpec((tm, tn), lambda i, j, kk: (i, j)),
        out_shape=jax.ShapeDtypeStruct((m, n), out_dtype),
        scratch_shapes=[pltpu.VMEM((tm, tn), F32)],
    )(a, b)


def _matmul_pieces_beside_scatter(pieces, offsets, w, blocks, name):
    m = pieces[0].shape[0]
    n = w.shape[1]
    tm = _pick(m, 256, 16)
    gm = m // tm
    npieces, na = len(pieces), len(blocks)
    hbm = pl.BlockSpec(memory_space=pl.ANY)

    def body(*refs):
        a_refs, w_refs = refs[:npieces], refs[npieces:2 * npieces]
        rest = refs[2 * npieces:]
        o_ref = rest[na]
        exchange = _DirectExchange("scatter", rest[:na], rest[na + 1:2 * na + 1], *rest[2 * na + 1:])
        i = pl.program_id(0)

        @pl.when(i == 0)
        def _():
            exchange.start()

        acc = _dot(a_refs[0][...].astype(MXU_DTYPE), w_refs[0][...])
        for a_ref, w_ref in zip(a_refs[1:], w_refs[1:]):
            acc = acc + _dot(a_ref[...].astype(MXU_DTYPE), w_ref[...])
        o_ref[...] = acc

        @pl.when(i == gm - 1)
        def _():
            exchange.wait()

    a_specs = [pl.BlockSpec((tm, p.shape[1]), lambda i: (i, 0)) for p in pieces]
    w_specs = [pl.BlockSpec((p.shape[1], n), functools.partial(lambda i, blk: (blk, 0), blk=off // p.shape[1]))
               for p, off in zip(pieces, offsets)]
    return pl.pallas_call(
        body, name=name, grid=(gm,),
        in_specs=a_specs + w_specs + [hbm] * na,
        out_specs=[pl.BlockSpec((tm, n), lambda i: (i, 0))] + [hbm] * na,
        out_shape=[jax.ShapeDtypeStruct((m, n), F32)] + [jax.ShapeDtypeStruct(blk.shape, blk.dtype) for blk in blocks],
        scratch_shapes=_DirectExchange.scratch(na),
    )(*pieces, *([w] * npieces), *blocks)


def _matmul_nt(a, bt, name, out_dtype=F32):
    m, k = a.shape
    n, _ = bt.shape
    tm, tn, tk = _pick(m, 1024, 16), _pick(n, 1408, 128), _pick(k, 1408, 128)
    nk = k // tk

    def body(a_ref, b_ref, o_ref, acc_ref):
        kk = pl.program_id(2)

        @pl.when(kk == 0)
        def _():
            acc_ref[...] = jnp.zeros_like(acc_ref)

        acc_ref[...] += _dot_nt(a_ref[...].astype(MXU_DTYPE), b_ref[...].astype(MXU_DTYPE))

        @pl.when(kk == nk - 1)
        def _():
            o_ref[...] = acc_ref[...].astype(o_ref.dtype)

    return pl.pallas_call(
        body, name=name, grid=(m // tm, n // tn, nk),
        in_specs=[pl.BlockSpec((tm, tk), lambda i, j, kk: (i, kk)), pl.BlockSpec((tn, tk), lambda i, j, kk: (j, kk))],
        out_specs=pl.BlockSpec((tm, tn), lambda i, j, kk: (i, j)),
        out_shape=jax.ShapeDtypeStruct((m, n), out_dtype),
        scratch_shapes=[pltpu.VMEM((tm, tn), F32)],
    )(a, bt)


def _matmul_sum2(a1, b1, a2, b2, name):
    m, k = a1.shape
    _, n = b1.shape
    tm, tn, tk = _pick(m, 1024, 16), _pick(n, 1408, 128), _pick(k, 1408, 128)
    nk = k // tk

    def body(a1_ref, b1_ref, a2_ref, b2_ref, o_ref):
        @pl.when(pl.program_id(2) == 0)
        def _():
            o_ref[...] = jnp.zeros_like(o_ref)

        o_ref[...] += (_dot(a1_ref[...].astype(MXU_DTYPE), b1_ref[...].astype(MXU_DTYPE))
                       + _dot(a2_ref[...].astype(MXU_DTYPE), b2_ref[...].astype(MXU_DTYPE)))

    a_spec = pl.BlockSpec((tm, tk), lambda i, j, kk: (i, kk))
    b_spec = pl.BlockSpec((tk, tn), lambda i, j, kk: (kk, j))
    return pl.pallas_call(
        body, name=name, grid=(m // tm, n // tn, nk), in_specs=[a_spec, b_spec, a_spec, b_spec],
        out_specs=pl.BlockSpec((tm, tn), lambda i, j, kk: (i, j)),
        out_shape=jax.ShapeDtypeStruct((m, n), F32),
    )(a1, b1, a2, b2)


def _matmul_tn(a, b, name, out_dtype=F32):
    l, m = a.shape
    _, n = b.shape
    tm, tn, tl = _pick(m, 1408, 128), _pick(n, 1408, 128), _pick(l, 512, 16)
    nl = l // tl

    def body(a_ref, b_ref, o_ref, acc_ref):
        ll = pl.program_id(2)

        @pl.when(ll == 0)
        def _():
            acc_ref[...] = jnp.zeros_like(acc_ref)

        acc_ref[...] += _dot_tn(a_ref[...].astype(MXU_DTYPE), b_ref[...].astype(MXU_DTYPE))

        @pl.when(ll == nl - 1)
        def _():
            o_ref[...] = acc_ref[...].astype(o_ref.dtype)

    return pl.pallas_call(
        body, name=name, grid=(m // tm, n // tn, nl),
        in_specs=[pl.BlockSpec((tl, tm), lambda i, j, ll: (ll, i)), pl.BlockSpec((tl, tn), lambda i, j, ll: (ll, j))],
        out_specs=pl.BlockSpec((tm, tn), lambda i, j, ll: (i, j)),
        out_shape=jax.ShapeDtypeStruct((m, n), out_dtype),
        scratch_shapes=[pltpu.VMEM((tm, tn), F32)],
    )(a, b)


def _row_specs(l, d, n_rows, n_vecs):
    tl = _pick(l, 512, 16)
    row = pl.BlockSpec((tl, d), lambda i: (i, 0))
    vec = pl.BlockSpec((1, d), lambda i: (0, 0))
    return tl, row, vec, [row] * n_rows + [vec] * n_vecs


def _rms_mod(x, nw, sc, sh, name):
    l, d = x.shape
    tl, row, _, in_specs = _row_specs(l, d, 1, 3)

    def body(x_ref, nw_ref, sc_ref, sh_ref, h_ref):
        xv = x_ref[...]
        r = lax.rsqrt(jnp.mean(xv * xv, axis=-1, keepdims=True) + EPS)
        h_ref[...] = (((xv * r) * nw_ref[...]) * (1.0 + sc_ref[...]) + sh_ref[...]).astype(h_ref.dtype)

    return pl.pallas_call(body, name=name, grid=(l // tl,), in_specs=in_specs, out_specs=row,
                          out_shape=jax.ShapeDtypeStruct((l, d), MXU_DTYPE))(x, nw, sc, sh)


def _residual_rms_mod(x, mix, g, nw, sc, sh, name):
    l, d = x.shape
    tl, row, _, in_specs = _row_specs(l, d, 2, 4)

    def body(x_ref, mix_ref, g_ref, nw_ref, sc_ref, sh_ref, x1_ref, h_ref):
        xv = x_ref[...] + g_ref[...] * mix_ref[...]
        x1_ref[...] = xv
        r = lax.rsqrt(jnp.mean(xv * xv, axis=-1, keepdims=True) + EPS)
        h_ref[...] = (((xv * r) * nw_ref[...]) * (1.0 + sc_ref[...]) + sh_ref[...]).astype(h_ref.dtype)

    return pl.pallas_call(body, name=name, grid=(l // tl,), in_specs=in_specs, out_specs=[row, row],
                          out_shape=[jax.ShapeDtypeStruct((l, d), F32), jax.ShapeDtypeStruct((l, d), MXU_DTYPE)],
                          )(x, mix, g, nw, sc, sh)


def _norm_bwd(x, dh, dres, nw, sc, name, gated=None):
    l, d = x.shape
    tl, row, vec, in_specs = _row_specs(l, d, 3, 2)
    stats = pl.BlockSpec((8, d), lambda i: (0, 0))

    def body(x_ref, dh_ref, dres_ref, nw_ref, sc_ref, *rest):
        dx_ref, st_ref = rest[-2:] if gated is None else (rest[2], rest[4])
        xv, dh_v = x_ref[...], dh_ref[...]
        r = lax.rsqrt(jnp.mean(xv * xv, axis=-1, keepdims=True) + EPS)
        xn = xv * r
        dxn = dh_v * (nw_ref[...] * (1.0 + sc_ref[...]))
        dx = dres_ref[...] + r * (dxn - xn * jnp.mean(dxn * xn, axis=-1, keepdims=True))
        dx_ref[...] = dx

        @pl.when(pl.program_id(0) == 0)
        def _():
            st_ref[...] = jnp.zeros_like(st_ref)

        dhx = dh_v * xn
        st_ref[0:1, :] += _colsum(dhx * nw_ref[...])
        st_ref[1:2, :] += _colsum(dh_v)
        st_ref[2:3, :] += _colsum(dhx * (1.0 + sc_ref[...]))
        if gated is not None:
            val_ref, g_ref, dval_ref = rest[0], rest[1], rest[3]
            dval_ref[...] = (g_ref[...] * dx).astype(dval_ref.dtype)
            st_ref[3:4, :] += _colsum(dx * val_ref[...])

    if gated is None:
        return pl.pallas_call(body, name=name, grid=(l // tl,), in_specs=in_specs, out_specs=[row, stats],
                              out_shape=[jax.ShapeDtypeStruct((l, d), F32), jax.ShapeDtypeStruct((8, d), F32)],
                              )(x, dh, dres, nw, sc)
    return pl.pallas_call(body, name=name, grid=(l // tl,), in_specs=in_specs + [row, vec], out_specs=[row, row, stats],
                          out_shape=[jax.ShapeDtypeStruct((l, d), F32), jax.ShapeDtypeStruct((l, d), MXU_DTYPE),
                                     jax.ShapeDtypeStruct((8, d), F32)],
                          )(x, dh, dres, nw, sc, *gated)


def _loss_head(x1, f, g, target, name):
    l, d = x1.shape
    tl, row, _, _ = _row_specs(l, d, 0, 0)
    vec = pl.BlockSpec((1, d), lambda i: (0, 0))

    def body(x1_ref, f_ref, g_ref, t_ref, dy_ref, df_ref, st_ref, loss_ref):
        fv = f_ref[...]
        e = x1_ref[...] + g_ref[...] * fv - t_ref[...]
        dy = e * (1.0 / d)
        dy_ref[...] = dy
        df_ref[...] = (g_ref[...] * dy).astype(df_ref.dtype)

        @pl.when(pl.program_id(0) == 0)
        def _():
            loss_ref[...] = jnp.zeros_like(loss_ref)
            st_ref[...] = jnp.zeros_like(st_ref)

        st_ref[0:1, :] += _colsum(dy * fv)
        s = jnp.sum(jnp.sum(e * e, axis=1, keepdims=True), axis=0, keepdims=True)
        loss_ref[...] += (0.5 / d) * s

    return pl.pallas_call(body, name=name, grid=(l // tl,), in_specs=[row, row, vec, row],
                          out_specs=[row, row, pl.BlockSpec((8, d), lambda i: (0, 0)), pl.BlockSpec((8, 128), lambda i: (0, 0))],
                          out_shape=[jax.ShapeDtypeStruct((l, d), F32), jax.ShapeDtypeStruct((l, d), MXU_DTYPE),
                                     jax.ShapeDtypeStruct((8, d), F32), jax.ShapeDtypeStruct((8, 128), F32)],
                          )(x1, f, g, target)


def _ffn_in(h, w_gate_t, w_up_t, name):
    l, k = h.shape
    f = w_gate_t.shape[0]
    tl, tf = _pick(l, 512, 16), _pick(f, 1408, 128)
    h_spec = pl.BlockSpec((tl, k), lambda i, j: (i, 0))
    w_spec = pl.BlockSpec((tf, k), lambda i, j: (j, 0))
    spec = pl.BlockSpec((tl, tf), lambda i, j: (i, j))

    def body(h_ref, wg_ref, wu_ref, g_ref, u_ref, a_ref):
        hv = h_ref[...].astype(MXU_DTYPE)
        gv = _dot_nt(hv, wg_ref[...].astype(MXU_DTYPE))
        uv = _dot_nt(hv, wu_ref[...].astype(MXU_DTYPE))
        g_ref[...] = gv
        u_ref[...] = uv
        a_ref[...] = (gv * _sigmoid(gv) * uv).astype(a_ref.dtype)

    return pl.pallas_call(body, name=name, grid=(l // tl, f // tf), in_specs=[h_spec, w_spec, w_spec], out_specs=[spec, spec, spec],
                          out_shape=[jax.ShapeDtypeStruct((l, f), F32)] * 2 + [jax.ShapeDtypeStruct((l, f), MXU_DTYPE)],
                          )(h, w_gate_t, w_up_t)


def _ffn_down_bwd(dffn, w_down, gate, up, name):
    l, f = gate.shape
    k = dffn.shape[1]
    tl, tf = _pick(l, 512, 16), _pick(f, 1408, 128)
    spec = pl.BlockSpec((tl, tf), lambda i, j: (i, j))

    def body(d_ref, w_ref, g_ref, u_ref, dg_ref, du_ref):
        dav = _dot_nt(d_ref[...].astype(MXU_DTYPE), w_ref[...].astype(MXU_DTYPE))
        gv = g_ref[...]
        s = _sigmoid(gv)
        dg_ref[...] = (dav * u_ref[...] * (s * (1.0 + gv * (1.0 - s)))).astype(dg_ref.dtype)
        du_ref[...] = (dav * (gv * s)).astype(du_ref.dtype)

    return pl.pallas_call(body, name=name, grid=(l // tl, f // tf),
                          in_specs=[pl.BlockSpec((tl, k), lambda i, j: (i, 0)), pl.BlockSpec((tf, k), lambda i, j: (j, 0)), spec, spec],
                          out_specs=[spec, spec], out_shape=[jax.ShapeDtypeStruct((l, f), MXU_DTYPE)] * 2)(dffn, w_down, gate, up)


def _conv_tile(l):
    return _pick(l, 512, 16)


def _conv_pre(buf, w_ref, b_ref, tl):
    acc = b_ref[...] + w_ref[3:4, :] * buf[HALO:HALO + tl, :]
    for k in range(CONV_WIDTH - 1):
        s = HALO - (CONV_WIDTH - 1) + k
        acc = acc + w_ref[k:k + 1, :] * buf[s:s + tl, :]
    return acc


def _fill_past(buf, u_ref, halo_ref, tl):
    i = pl.program_id(0)

    @pl.when(i == 0)
    def _():
        buf[0:HALO, :] = jnp.zeros((HALO, buf.shape[1]), F32)

    @pl.when(i > 0)
    def _():
        buf[0:HALO, :] = halo_ref[...]

    buf[HALO:HALO + tl, :] = u_ref[...]


def _conv_silu(u, w, b, name):
    l, ch = u.shape[0], w.shape[1]
    tl = _conv_tile(l)
    per = tl // HALO
    cur = pl.BlockSpec((tl, ch), lambda i: (i, 0))
    past = pl.BlockSpec((HALO, ch), lambda i: (jnp.maximum(i * per - 1, 0), 0))

    def body(u_ref, halo_ref, w_ref, b_ref, o_ref, buf):
        _fill_past(buf, u_ref, halo_ref, tl)
        pre = _conv_pre(buf, w_ref, b_ref, tl)
        o_ref[...] = pre * _sigmoid(pre)

    return pl.pallas_call(body, name=name, grid=(l // tl,),
                          in_specs=[cur, past, pl.BlockSpec((CONV_WIDTH, ch), lambda i: (0, 0)), pl.BlockSpec((1, ch), lambda i: (0, 0))],
                          out_specs=cur, out_shape=jax.ShapeDtypeStruct((l, ch), F32),
                          scratch_shapes=[pltpu.VMEM((tl + HALO, ch), F32)])(u, u, w, b)


def _conv_silu_bwd_pre(u, dxc, w, b, name):
    l, ch = u.shape[0], w.shape[1]
    tl = _conv_tile(l)
    per = tl // HALO
    cur = pl.BlockSpec((tl, ch), lambda i: (i, 0))
    past = pl.BlockSpec((HALO, ch), lambda i: (jnp.maximum(i * per - 1, 0), 0))

    def body(u_ref, halo_ref, d_ref, w_ref, b_ref, dpre_ref, st_ref, buf):
        _fill_past(buf, u_ref, halo_ref, tl)
        pre = _conv_pre(buf, w_ref, b_ref, tl)
        s = _sigmoid(pre)
        dpre = d_ref[...] * (s * (1.0 + pre * (1.0 - s)))
        dpre_ref[...] = dpre

        @pl.when(pl.program_id(0) == 0)
        def _():
            st_ref[...] = jnp.zeros_like(st_ref)

        for k in range(CONV_WIDTH):
            s0 = HALO - (CONV_WIDTH - 1) + k
            st_ref[k:k + 1, :] += _colsum(dpre * buf[s0:s0 + tl, :])
        st_ref[CONV_WIDTH:CONV_WIDTH + 1, :] += _colsum(dpre)

    return pl.pallas_call(body, name=name, grid=(l // tl,),
                          in_specs=[cur, past, cur, pl.BlockSpec((CONV_WIDTH, ch), lambda i: (0, 0)), pl.BlockSpec((1, ch), lambda i: (0, 0))],
                          out_specs=[cur, pl.BlockSpec((8, ch), lambda i: (0, 0))],
                          out_shape=[jax.ShapeDtypeStruct((l, ch), F32), jax.ShapeDtypeStruct((8, ch), F32)],
                          scratch_shapes=[pltpu.VMEM((tl + HALO, ch), F32)])(u, u, dxc, w, b)


def _conv_bwd_input(dpre, w, name):
    l, ch = dpre.shape
    tl = _conv_tile(l)
    per = tl // HALO
    nt = l // tl
    cur = pl.BlockSpec((tl, ch), lambda i: (i, 0))
    nxt = pl.BlockSpec((HALO, ch), lambda i: (jnp.minimum((i + 1) * per, l // HALO - 1), 0))

    def body(d_ref, halo_ref, w_ref, du_ref, buf):
        i = pl.program_id(0)
        buf[0:tl, :] = d_ref[...]

        @pl.when(i == nt - 1)
        def _():
            buf[tl:tl + HALO, :] = jnp.zeros((HALO, ch), F32)

        @pl.when(i < nt - 1)
        def _():
            buf[tl:tl + HALO, :] = halo_ref[...]

        acc = w_ref[3:4, :] * buf[0:tl, :]
        for k in range(CONV_WIDTH - 1):
            s = CONV_WIDTH - 1 - k
            acc = acc + w_ref[k:k + 1, :] * buf[s:s + tl, :]
        du_ref[...] = acc

    return pl.pallas_call(body, name=name, grid=(nt,),
                          in_specs=[cur, nxt, pl.BlockSpec((CONV_WIDTH, ch), lambda i: (0, 0))],
                          out_specs=cur, out_shape=jax.ShapeDtypeStruct((l, ch), F32),
                          scratch_shapes=[pltpu.VMEM((tl + HALO, ch), F32)])(dpre, dpre, w)


HEAD_TILE = 128
D_HEAD_TILES = N_HEADS * HEAD_TILE


def _chunk_iota():
    r = lax.broadcasted_iota(jnp.int32, (CHUNK, CHUNK), 0)
    c = lax.broadcasted_iota(jnp.int32, (CHUNK, CHUNK), 1)
    return r, c


def _split3_dot(v, ones_b):
    hi = v.astype(BF16)
    r1 = v - hi.astype(F32)
    mid = r1.astype(BF16)
    lo = (r1 - mid.astype(F32)).astype(BF16)
    return _dot(hi, ones_b) + _dot(mid, ones_b) + _dot(lo, ones_b)


def _split3_dot_nt(v, ones_b):
    hi = v.astype(BF16)
    r1 = v - hi.astype(F32)
    mid = r1.astype(BF16)
    lo = (r1 - mid.astype(F32)).astype(BF16)
    return _dot_nt(hi, ones_b) + _dot_nt(mid, ones_b) + _dot_nt(lo, ones_b)


def _ssd_decays(dt_ref, sx_ref, sy_ref, dtbx_ref, dtby_ref, alx_ref, aly_ref, r, c):
    tri = (r >= c).astype(BF16)
    dtv = dt_ref[...]
    raw_x = _split3_dot(dtv, sx_ref[...]) + dtbx_ref[...]
    dt_x = _softplus(raw_x)
    a_x = -jnp.exp(alx_ref[...])
    adt_x = a_x * dt_x
    acs_x = _tri_dot(tri, adt_x)
    atot_x = _colsum(adt_x)
    acs_y = _tri_dot(tri, -jnp.exp(aly_ref[...]) * _softplus(_split3_dot(dtv, sy_ref[...]) + dtby_ref[...]))
    return raw_x, dt_x, a_x, acs_y, jnp.exp(acs_x), jnp.exp(atot_x - acs_x), jnp.exp(atot_x)


def _head_decay(acs_y, e, r, c):
    col = acs_y[:, e * HEAD_TILE:(e + 1) * HEAD_TILE]
    return jnp.where(r >= c, jnp.exp(col - col.T), 0.0)


def _half_masks():
    lane = lax.broadcasted_iota(jnp.int32, (CHUNK, 2 * HEAD_DIM), 1)
    return lane < HEAD_DIM, lane >= HEAD_DIM


def _ssd_specs(nc, reverse):
    def at(i):
        return nc - 1 - i if reverse else i
    xc = pl.BlockSpec((CHUNK, D_CONV), lambda i: (at(i), 0))
    wide = pl.BlockSpec((CHUNK, D_SSD), lambda i: (at(i), 0))
    dt_tile = pl.BlockSpec((CHUNK, HEAD_TILE), lambda i: (at(i), O_DT // HEAD_TILE))
    dt_out = pl.BlockSpec((CHUNK, HEAD_TILE), lambda i: (at(i), 0))
    spreads = [pl.BlockSpec((HEAD_TILE, D_SSD), lambda i: (0, 0)), pl.BlockSpec((HEAD_TILE, D_HEAD_TILES), lambda i: (0, 0))]
    vec = pl.BlockSpec((1, D_SSD), lambda i: (0, 0))
    vec_y = pl.BlockSpec((1, D_HEAD_TILES), lambda i: (0, 0))
    state = pl.BlockSpec((1, SSD_STATE, D_SSD), lambda i: (at(i), 0, 0))
    return xc, wide, dt_tile, dt_out, spreads, vec, vec_y, state


def _ssd_fwd(xc, proj, spread_x, spread_y, dtb_x, dtb_y, al_x, al_y, name):
    l = xc.shape[0]
    nc = l // CHUNK
    xc_s, wide_s, dt_s, _, spread_s, vec_s, vec_y_s, state_s = _ssd_specs(nc, False)
    pairs_per_group = GROUP_WIDTH // (2 * HEAD_DIM)

    def body(xc_ref, dt_ref, sx_ref, sy_ref, dtbx_ref, dtby_ref, alx_ref, aly_ref, y_ref, sp_ref, state):
        @pl.when(pl.program_id(0) == 0)
        def _():
            state[...] = jnp.zeros_like(state)

        r, c = _chunk_iota()
        halves = _half_masks()
        _, dt_x, _, acs_y, ea_x, ds_x, eatot_x = _ssd_decays(dt_ref, sx_ref, sy_ref, dtbx_ref, dtby_ref, alx_ref, aly_ref, r, c)
        xg = xc_ref[:, 0:D_SSD] * dt_x
        sp_ref[0] = state[...]
        for g in range(SSD_GROUPS):
            lanes = slice(g * GROUP_WIDTH, (g + 1) * GROUP_WIDTH)
            bb = xc_ref[:, D_SSD + g * SSD_STATE:D_SSD + (g + 1) * SSD_STATE].astype(MXU_DTYPE)
            cb = xc_ref[:, D_SSD + (SSD_GROUPS + g) * SSD_STATE:D_SSD + (SSD_GROUPS + g + 1) * SSD_STATE].astype(MXU_DTYPE)
            scores = _dot_nt(cb, bb)
            sg = state[:, lanes]
            ys = []
            for j in range(g * pairs_per_group, (g + 1) * pairs_per_group):
                xg_pair = xg[:, j * 2 * HEAD_DIM:(j + 1) * 2 * HEAD_DIM]
                acc = jnp.zeros((CHUNK, 2 * HEAD_DIM), F32)
                for half in range(2):
                    m = (scores * _head_decay(acs_y, 2 * j + half, r, c)).astype(MXU_DTYPE)
                    acc = acc + _dot(m, jnp.where(halves[half], xg_pair, 0.0).astype(MXU_DTYPE))
                ys.append(acc)
            y_ref[:, lanes] = jnp.concatenate(ys, axis=1) + _dot(cb, sg.astype(MXU_DTYPE)) * ea_x[:, lanes]
            state[:, lanes] = sg * eatot_x[:, lanes] + _dot_tn(bb, (xg[:, lanes] * ds_x[:, lanes]).astype(MXU_DTYPE))

    return pl.pallas_call(
        body, name=name, grid=(nc,),
        in_specs=[xc_s, dt_s] + spread_s + [vec_s, vec_y_s, vec_s, vec_y_s],
        out_specs=[wide_s, state_s],
        out_shape=[jax.ShapeDtypeStruct((l, D_SSD), F32), jax.ShapeDtypeStruct((nc, SSD_STATE, D_SSD), F32)],
        scratch_shapes=[pltpu.VMEM((SSD_STATE, D_SSD), F32)],
    )(xc, proj, spread_x, spread_y, dtb_x, dtb_y, al_x, al_y)


def _ssd_bwd(xc, proj, spread_x, spread_y, dtb_x, dtb_y, al_x, al_y, dy, states, name):
    l = xc.shape[0]
    nc = l // CHUNK
    xc_s, wide_s, dt_s, dt_out_s, spread_s, vec_s, vec_y_s, state_s = _ssd_specs(nc, True)
    pairs_per_group = GROUP_WIDTH // (2 * HEAD_DIM)

    def body(xc_ref, dt_ref, sx_ref, sy_ref, dtbx_ref, dtby_ref, alx_ref, aly_ref, dy_ref, sp_ref,
             dxc_ref, ddtr_ref, st_ref, hs_ref, dstate):
        @pl.when(pl.program_id(0) == 0)
        def _():
            dstate[...] = jnp.zeros_like(dstate)
            st_ref[...] = jnp.zeros_like(st_ref)

        r, c = _chunk_iota()
        halves = _half_masks()
        spread = sx_ref[...]

        def head_sums(v):
            return _split3_dot(_split3_dot_nt(v, spread), spread)

        raw_x, dt_x, a_x, acs_y, ea_x, ds_x, eatot_x = _ssd_decays(dt_ref, sx_ref, sy_ref, dtbx_ref, dtby_ref, alx_ref, aly_ref, r, c)
        xs = xc_ref[:, 0:D_SSD]
        xg = xs * dt_x
        gy = dy_ref[...]
        s_prev = sp_ref[0]
        gea = gy * ea_x
        xds = xg * ds_x
        ds_old = dstate[...]
        later2 = jnp.concatenate([(r <= c).astype(BF16)] * 2, axis=1)
        dxg_parts, state_term, yoff_parts, dadt_parts = [], [], [], []
        for g in range(SSD_GROUPS):
            lanes = slice(g * GROUP_WIDTH, (g + 1) * GROUP_WIDTH)
            b_lo = D_SSD + g * SSD_STATE
            c_lo = D_SSD + (SSD_GROUPS + g) * SSD_STATE
            bb = xc_ref[:, b_lo:b_lo + SSD_STATE].astype(MXU_DTYPE)
            cb = xc_ref[:, c_lo:c_lo + SSD_STATE].astype(MXU_DTYPE)
            scores = _dot_nt(cb, bb)
            dsg = ds_old[:, lanes].astype(MXU_DTYPE)
            gea_b = gea[:, lanes].astype(MXU_DTYPE)
            xds_b = xds[:, lanes].astype(MXU_DTYPE)
            dxg_state = _dot(bb, dsg) * ds_x[:, lanes]
            dc = _dot_nt(gea_b, s_prev[:, lanes].astype(MXU_DTYPE))
            db = _dot_nt(xds_b, dsg)
            dscores = jnp.zeros((CHUNK, CHUNK), F32)
            diag = []
            for j in range(g * pairs_per_group, (g + 1) * pairs_per_group):
                pair = slice(j * 2 * HEAD_DIM, (j + 1) * 2 * HEAD_DIM)
                xg_pair = xg[:, pair].astype(MXU_DTYPE)
                acc = jnp.zeros((CHUNK, 2 * HEAD_DIM), F32)
                cols = []
                for half in range(2):
                    decay = _head_decay(acs_y, 2 * j + half, r, c)
                    g_e = jnp.where(halves[half], gy[:, pair], 0.0).astype(MXU_DTYPE)
                    acc = acc + _dot_tn((scores * decay).astype(MXU_DTYPE), g_e)
                    dm = _dot_nt(g_e, xg_pair) * decay
                    dscores = dscores + dm
                    wq = dm * scores
                    hi = wq.astype(BF16)
                    lo = (wq - hi.astype(F32)).astype(BF16)
                    later = _dot(later2, jnp.concatenate([hi, lo], axis=0))
                    cols.append(jnp.sum(jnp.where(c < r, later, 0.0), axis=1, keepdims=True))
                diag.append(acc)
                dadt_parts.append(jnp.where(halves[0], cols[0], cols[1]))
            dsc_b = dscores.astype(MXU_DTYPE)
            dc = dc + _dot(dsc_b, bb)
            db = db + _dot_tn(dsc_b, cb)
            dxc_ref[:, b_lo:b_lo + SSD_STATE] = db
            dxc_ref[:, c_lo:c_lo + SSD_STATE] = dc
            dxg_parts.append(jnp.concatenate(diag, axis=1) + dxg_state)
            state_term.append(dxg_state)
            yoff_parts.append(_dot(cb, s_prev[:, lanes].astype(MXU_DTYPE)) * ea_x[:, lanes])
            dstate[:, lanes] = ds_old[:, lanes] * eatot_x[:, lanes] + _dot_tn(cb, gea_b)
        dxg = jnp.concatenate(dxg_parts, axis=1)
        dxc_ref[:, 0:D_SSD] = dxg * dt_x
        through_out = _tri_dot((r <= c).astype(BF16), gy * jnp.concatenate(yoff_parts, axis=1))
        through_in = _tri_dot((c < r).astype(BF16), xg * jnp.concatenate(state_term, axis=1))
        carried = jnp.broadcast_to(_colsum(ds_old * s_prev) * eatot_x, (8, D_SSD))
        dadt = (jnp.concatenate(dadt_parts, axis=1) + head_sums(through_out + through_in)
                + jnp.max(head_sums(carried), axis=0, keepdims=True))
        ddt = a_x * dadt + head_sums(dxg * xs)
        draw = ddt * _sigmoid(raw_x)
        ddtr_ref[...] = _split3_dot_nt(draw, spread) * (1.0 / HEAD_DIM)
        st_ref[0:1, :] += _colsum(dt_x * dadt) * a_x
        st_ref[1:2, :] += _colsum(draw)
        st_ref[2:3, :] += _colsum(gy * xs)

        @pl.when(pl.program_id(0) == nc - 1)
        def _():
            hs_ref[...] = head_sums(st_ref[...])

    stats = pl.BlockSpec((8, D_SSD), lambda i: (0, 0))
    return pl.pallas_call(
        body, name=name, grid=(nc,),
        in_specs=[xc_s, dt_s] + spread_s + [vec_s, vec_y_s, vec_s, vec_y_s, wide_s, state_s],
        out_specs=[xc_s, dt_out_s, stats, stats],
        out_shape=[jax.ShapeDtypeStruct((l, D_CONV), F32), jax.ShapeDtypeStruct((l, HEAD_TILE), F32),
                   jax.ShapeDtypeStruct((8, D_SSD), F32), jax.ShapeDtypeStruct((8, D_SSD), F32)],
        scratch_shapes=[pltpu.VMEM((SSD_STATE, D_SSD), F32)],
    )(xc, proj, spread_x, spread_y, dtb_x, dtb_y, al_x, al_y, dy, states)


def _ssd_gate_norm(ycore, xc, z, dskip_x, norm_w, name):
    l = ycore.shape[0]
    tl = _pick(l, 512, 16)
    row = pl.BlockSpec((tl, D_SSD), lambda i: (i, 0))
    vec = pl.BlockSpec((1, D_SSD), lambda i: (0, 0))

    def body(y_ref, xs_ref, z_ref, dk_ref, nw_ref, o_ref):
        zv = z_ref[...]
        yv = (y_ref[...] + dk_ref[...] * xs_ref[...]) * (zv * _sigmoid(zv))
        for g in range(SSD_GROUPS):
            lanes = slice(g * GROUP_WIDTH, (g + 1) * GROUP_WIDTH)
            yg = yv[:, lanes]
            rg = lax.rsqrt(jnp.mean(yg * yg, axis=-1, keepdims=True) + EPS)
            o_ref[:, lanes] = (yg * rg * nw_ref[:, lanes]).astype(o_ref.dtype)

    z_spec = pl.BlockSpec((tl, D_SSD), lambda i: (i, O_Z // D_SSD))
    return pl.pallas_call(body, name=name, grid=(l // tl,), in_specs=[row, row, z_spec, vec, vec], out_specs=row,
                          out_shape=jax.ShapeDtypeStruct((l, D_SSD), MXU_DTYPE))(ycore, xc, z, dskip_x, norm_w)


def _ssd_gate_norm_bwd(dout, ycore, xc, z, dskip_x, norm_w, name):
    l = ycore.shape[0]
    tl = _pick(l, 512, 16)
    nt = l // tl
    row = pl.BlockSpec((tl, D_SSD), lambda i: (i, 0))
    vec = pl.BlockSpec((1, D_SSD), lambda i: (0, 0))

    def body(do_ref, y_ref, xs_ref, z_ref, dk_ref, nw_ref, dyc_ref, dxs_ref, dz_ref, st_ref):
        @pl.when(pl.program_id(0) == 0)
        def _():
            st_ref[...] = jnp.zeros_like(st_ref)

        zv, xs = z_ref[...], xs_ref[...]
        s = _sigmoid(zv)
        gz = zv * s
        yc = y_ref[...] + dk_ref[...] * xs
        yv = yc * gz
        dov = do_ref[...]
        dnw, dyv = [], []
        for g in range(SSD_GROUPS):
            lanes = slice(g * GROUP_WIDTH, (g + 1) * GROUP_WIDTH)
            yg = yv[:, lanes]
            rg = lax.rsqrt(jnp.mean(yg * yg, axis=-1, keepdims=True) + EPS)
            yn = yg * rg
            dnw.append(_colsum(dov[:, lanes] * yn))
            dyn = dov[:, lanes] * nw_ref[:, lanes]
            dyv.append(rg * (dyn - yn * jnp.mean(dyn * yn, axis=-1, keepdims=True)))
        dy = jnp.concatenate(dyv, axis=1)
        dyc = dy * gz
        dyc_ref[...] = dyc
        dxs_ref[...] = dyc * dk_ref[...]
        dz_ref[...] = dy * yc * (s * (1.0 + zv * (1.0 - s)))
        st_ref[0:1, :] += jnp.concatenate(dnw, axis=1)

    return pl.pallas_call(
        body, name=name, grid=(nt,), in_specs=[row, row, row, pl.BlockSpec((tl, D_SSD), lambda i: (i, O_Z // D_SSD)), vec, vec],
        out_specs=[row, row, row, pl.BlockSpec((8, D_SSD), lambda i: (0, 0))],
        out_shape=[jax.ShapeDtypeStruct((l, D_SSD), F32)] * 3 + [jax.ShapeDtypeStruct((8, D_SSD), F32)],
    )(dout, ycore, xc, z, dskip_x, norm_w)


PAIR = 2 * HEAD_DIM
N_PAIRS = N_HEADS // 2
Q_TILE0 = O_Q // PAIR
SB_KEYS = 1024
SB_PAIRS = 4
SB_PAIRS_FWD = 4
SB_SCALE = HEAD_DIM ** -0.5


def _pair_sum(v, lo):
    s_lo = jnp.sum(jnp.where(lo, v, 0.0), axis=-1, keepdims=True)
    s_hi = jnp.sum(jnp.where(lo, 0.0, v), axis=-1, keepdims=True)
    return jnp.where(lo, s_lo, s_hi)


def _qkv_prep(proj, qw2, kw2, name):
    l = proj.shape[0]
    tl = _pick(l, 1024, 16)
    out = pl.BlockSpec((tl, PAIR), lambda i, j: (i, j))
    vec = pl.BlockSpec((1, PAIR), lambda i, j: (0, 0))

    def at(which):
        return pl.BlockSpec((tl, PAIR), lambda i, j: (i, Q_TILE0 + which * N_PAIRS + j))

    def body(q_ref, k_ref, v_ref, qw_ref, kw_ref, qn_ref, kn_ref, vb_ref):
        lo = lax.broadcasted_iota(jnp.int32, (tl, PAIR), 1) < HEAD_DIM
        for t_ref, w_ref, o_ref, scale in ((q_ref, qw_ref, qn_ref, SB_SCALE), (k_ref, kw_ref, kn_ref, 1.0)):
            tv = t_ref[...]
            r = lax.rsqrt(_pair_sum(tv * tv, lo) * (1.0 / HEAD_DIM) + EPS)
            o_ref[...] = ((tv * r) * w_ref[...] * scale).astype(o_ref.dtype)
        vb_ref[...] = v_ref[...].astype(vb_ref.dtype)

    return pl.pallas_call(body, name=name, grid=(l // tl, N_PAIRS), in_specs=[at(0), at(1), at(2), vec, vec],
                          out_specs=[out, out, out], out_shape=[jax.ShapeDtypeStruct((l, D_SB), MXU_DTYPE)] * 3,
                          )(proj, proj, proj, qw2, kw2)


def _qk_norm_bwd(proj, dqn, dkn, qw2, kw2, name):
    l = proj.shape[0]
    tl = _pick(l, 1024, 16)
    out = pl.BlockSpec((tl, PAIR), lambda i, j: (i, j))
    vec = pl.BlockSpec((1, PAIR), lambda i, j: (0, 0))
    stats = pl.BlockSpec((8, PAIR), lambda i, j: (0, 0))

    def at(which):
        return pl.BlockSpec((tl, PAIR), lambda i, j: (i, Q_TILE0 + which * N_PAIRS + j))

    def body(q_ref, k_ref, dqn_ref, dkn_ref, qw_ref, kw_ref, dq_ref, dk_ref, stq_ref, stk_ref):
        @pl.when((pl.program_id(0) == 0) & (pl.program_id(1) == 0))
        def _():
            stq_ref[...] = jnp.zeros_like(stq_ref)
            stk_ref[...] = jnp.zeros_like(stk_ref)

        lo = lax.broadcasted_iota(jnp.int32, (tl, PAIR), 1) < HEAD_DIM
        for t_ref, dn_ref, w_ref, d_ref, st_ref, scale in ((q_ref, dqn_ref, qw_ref, dq_ref, stq_ref, SB_SCALE),
                                                           (k_ref, dkn_ref, kw_ref, dk_ref, stk_ref, 1.0)):
            tv = t_ref[...]
            r = lax.rsqrt(_pair_sum(tv * tv, lo) * (1.0 / HEAD_DIM) + EPS)
            tn = tv * r
            dnv = dn_ref[...] * scale
            dtn = dnv * w_ref[...]
            d_ref[...] = r * (dtn - tn * (_pair_sum(dtn * tn, lo) * (1.0 / HEAD_DIM)))
            st_ref[0:1, :] += _colsum(dnv * tn)

    return pl.pallas_call(body, name=name, grid=(l // tl, N_PAIRS), in_specs=[at(0), at(1), out, out, vec, vec],
                          out_specs=[out, out, stats, stats],
                          out_shape=[jax.ShapeDtypeStruct((l, D_SB), F32)] * 2 + [jax.ShapeDtypeStruct((8, PAIR), F32)] * 2,
                          )(proj, proj, dqn, dkn, qw2, kw2)


def _sb_masks():
    r = lax.broadcasted_iota(jnp.int32, (CHUNK, CHUNK), 0)
    c = lax.broadcasted_iota(jnp.int32, (CHUNK, CHUNK), 1)
    return r, c


def _stack2(mask):
    t = mask.astype(BF16)
    return jnp.concatenate([t, t], axis=0)


def _sb_fwd(q, k, v, shards, name):
    l = q.shape[0]
    nq = l // CHUNK
    kt = _pick(l, SB_KEYS, CHUNK)
    sub = kt // CHUNK
    pairs = SB_PAIRS_FWD
    heads = 2 * pairs
    steps = N_PAIRS // pairs
    na = len(shards)
    qblk = pl.BlockSpec((CHUNK, pairs * PAIR), lambda i, j: (j, i))
    full = pl.BlockSpec((l, pairs * PAIR), lambda i, j: (0, i))
    hbm = pl.BlockSpec(memory_space=pl.ANY)

    def body(q_ref, k_ref, v_ref, *rest):
        o_ref, tot_ref = rest[na:na + 2]
        exchange = _DirectExchange("gather", rest[:na], rest[na + 2:2 * na + 2], *rest[2 * na + 2:])
        qb = pl.program_id(1)

        @pl.when((pl.program_id(0) == 0) & (qb == 0))
        def _():
            exchange.start()

        r, c = _sb_masks()
        after2 = _stack2(r > c)
        halves = _half_masks()
        zero = jnp.zeros((CHUNK, PAIR), q_ref.dtype)
        lanes = [slice((a // 2) * PAIR, (a // 2 + 1) * PAIR) for a in range(heads)]
        qm = [jnp.where(halves[a % 2], q_ref[:, lanes[a]], zero) for a in range(heads)]
        last = (qb * CHUNK) // kt

        def tile(t, width, carries, accs, masked):
            rows = pl.ds(pl.multiple_of(t * width, width), width)
            if masked:
                key_minus_query = (lax.broadcasted_iota(jnp.int32, (CHUNK, width), 1)
                                   - lax.broadcasted_iota(jnp.int32, (CHUNK, width), 0))
                keep = key_minus_query < qb * CHUNK - t * width

            def logits(a):
                lg = _dot_nt(qm[a], k_ref[rows, lanes[a]])
                sp = _softplus_logits(lg)
                return lg - sp, (jnp.where(keep, -sp, 0.0) if masked else -sp)

            def sums(a, lr):
                offset, parts = carries[a], [None] * (width // CHUNK)
                for j in reversed(range(width // CHUNK)):
                    piece = lr[:, j * CHUNK:(j + 1) * CHUNK]
                    parts[j] = _split_dot(piece, after2) + offset
                    offset = offset + jnp.sum(piece, axis=1, keepdims=True)
                return jnp.concatenate(parts, axis=1), offset

            def output(a, ls, cs):
                w = jnp.exp(ls + cs)
                if masked:
                    w = jnp.where(keep, w, 0.0)
                return accs[a] + _dot(w.astype(MXU_DTYPE), v_ref[rows, lanes[a]])

            new_carries, new_accs = [None] * heads, [None] * heads
            ls, lr = logits(0)
            for a in range(heads):
                cs, new_carries[a] = sums(a, lr)
                if a + 1 < heads:
                    ls_next, lr = logits(a + 1)
                new_accs[a] = output(a, ls, cs)
                ls = ls_next
            return tuple(new_carries), tuple(new_accs)

        carries = tuple(jnp.zeros((CHUNK, 1), F32) for _ in range(heads))
        accs = tuple(jnp.zeros((CHUNK, PAIR), F32) for _ in range(heads))
        if sub % 2 == 0:
            half = kt // 2
            carries, accs = lax.cond((qb * CHUNK) % kt >= half,
                                     lambda cr, ac: tile(2 * last + 1, half, cr, ac, True), lambda cr, ac: (cr, ac), carries, accs)
            carries, accs = tile(2 * last, half, carries, accs, True)
        else:
            carries, accs = tile(last, kt, carries, accs, True)
        carries, accs = lax.fori_loop(1, last + 1, lambda i, st: tile(last - i, kt, st[0], st[1], False), (carries, accs))
        for p in range(pairs):
            o_ref[:, lanes[2 * p]] = jnp.where(halves[0], accs[2 * p], accs[2 * p + 1])
            tot_ref[:, lanes[2 * p]] = jnp.where(halves[0], carries[2 * p], carries[2 * p + 1])

        @pl.when((pl.program_id(0) == steps - 1) & (qb == nq - 1))
        def _():
            exchange.wait()

    return pl.pallas_call(
        body, name=name, grid=(steps, nq), in_specs=[qblk, full, full] + [hbm] * na,
        out_specs=[qblk, qblk] + [hbm] * na,
        out_shape=[jax.ShapeDtypeStruct((l, D_SB), F32), jax.ShapeDtypeStruct((l, D_SB), F32)]
        + [jax.ShapeDtypeStruct((N_DEV,) + s.shape, s.dtype) for s in shards],
        scratch_shapes=_DirectExchange.scratch(na),
    )(q, k, v, *shards)


def _sb_bwd(q, k, v, dycat, tot, blocks, name):
    l = q.shape[0]
    nq = l // CHUNK
    kt = _pick(l, SB_KEYS, CHUNK)
    sub = kt // CHUNK
    heads = 2 * SB_PAIRS
    steps = N_PAIRS // SB_PAIRS
    na = len(blocks)
    width_all = SB_PAIRS * PAIR
    qblk = pl.BlockSpec((CHUNK, width_all), lambda i, j: (j, i))
    doblk = pl.BlockSpec((CHUNK, width_all), lambda i, j: (j, D_SSD // width_all + i))
    full = pl.BlockSpec((l, width_all), lambda i, j: (0, i))
    resident = pl.BlockSpec((l, width_all), lambda i, j: (0, i), pipeline_mode=pl.Buffered(1))
    hbm = pl.BlockSpec(memory_space=pl.ANY)

    def body(q_ref, k_ref, v_ref, do_ref, tot_ref, *rest):
        dq_ref, dk_ref, dv_ref = rest[na:na + 3]
        exchange = _DirectExchange("scatter", rest[:na], rest[na + 3:2 * na + 3], *rest[2 * na + 3:])
        qb = pl.program_id(1)

        @pl.when((pl.program_id(0) == 0) & (qb == 0))
        def _():
            exchange.start()

        @pl.when(qb == 0)
        def _():
            dk_ref[...] = jnp.zeros_like(dk_ref)
            dv_ref[...] = jnp.zeros_like(dv_ref)

        r, c = _sb_masks()
        upto2 = _stack2(r <= c)
        before2 = _stack2(r < c)
        halves = _half_masks()
        lanes = [slice((a // 2) * PAIR, (a // 2 + 1) * PAIR) for a in range(heads)]
        qm = [jnp.where(halves[a % 2], q_ref[:, lanes[a]], jnp.zeros((CHUNK, PAIR), q_ref.dtype)) for a in range(heads)]
        dom = [jnp.where(halves[a % 2], do_ref[:, lanes[a]], 0.0).astype(MXU_DTYPE) for a in range(heads)]
        total = [jnp.max(jnp.where(halves[a % 2], tot_ref[:, lanes[a]], -jnp.inf), axis=1, keepdims=True) for a in range(heads)]
        last = (qb * CHUNK) // kt

        def prefix(values, tri2, offset):
            parts = []
            for j in range(values.shape[1] // CHUNK):
                piece = values[:, j * CHUNK:(j + 1) * CHUNK]
                parts.append(_split_dot(piece, tri2) + offset)
                offset = offset + jnp.sum(piece, axis=1, keepdims=True)
            return jnp.concatenate(parts, axis=1), offset

        def tile(t, width, carry_p, carry_d, dq, masked):
            rows = pl.ds(pl.multiple_of(t * width, width), width)
            if masked:
                key_minus_query = (lax.broadcasted_iota(jnp.int32, (CHUNK, width), 1)
                                   - lax.broadcasted_iota(jnp.int32, (CHUNK, width), 0))
                keep = key_minus_query < qb * CHUNK - t * width

            def log_terms(a):
                lg = _dot_nt(qm[a], k_ref[rows, lanes[a]])
                sp = _softplus_logits(lg)
                return lg - sp, (jnp.where(keep, -sp, 0.0) if masked else -sp)

            def weights(a, ls, lr):
                p_incl, p_next = prefix(lr, upto2, carry_p[a])
                w = jnp.exp(ls + (total[a] - p_incl))
                if masked:
                    w = jnp.where(keep, w, 0.0)
                return w, p_next

            def weight_grads(a, w):
                return _dot_nt(dom[a], v_ref[rows, lanes[a]]) * w

            def gradients(a, ls, w, da):
                d_excl, d_next = prefix(da, before2, carry_d[a])
                sig = jnp.exp(ls)
                dl = da * (1.0 - sig) - d_excl * sig
                if masked:
                    dl = jnp.where(keep, dl, 0.0)
                dl_b = dl.astype(MXU_DTYPE)
                return (d_next, dq[a] + _dot(dl_b, k_ref[rows, lanes[a]]),
                        _dot_tn(w.astype(MXU_DTYPE), dom[a]), _dot_tn(dl_b, qm[a]))

            new_p, new_d, new_dq = [None] * heads, [None] * heads, [None] * heads
            dv_upd, dk_upd = [None] * heads, [None] * heads
            ls, lr = log_terms(0)
            w, new_p[0] = weights(0, ls, lr)
            for a in range(heads):
                if a + 1 < heads:
                    ls_next, lr_next = log_terms(a + 1)
                da = weight_grads(a, w)
                if a + 1 < heads:
                    w_next, new_p[a + 1] = weights(a + 1, ls_next, lr_next)
                new_d[a], new_dq[a], dv_upd[a], dk_upd[a] = gradients(a, ls, w, da)
                if a % 2 == 1:
                    dv_ref[rows, lanes[a]] += dv_upd[a - 1] + dv_upd[a]
                    dk_ref[rows, lanes[a]] += dk_upd[a - 1] + dk_upd[a]
                if a + 1 < heads:
                    ls, w = ls_next, w_next
            return tuple(new_p), tuple(new_d), tuple(new_dq)

        zeros = tuple(jnp.zeros((CHUNK, 1), F32) for _ in range(heads))
        dq0 = tuple(jnp.zeros((CHUNK, PAIR), F32) for _ in range(heads))
        state = lax.fori_loop(0, last, lambda t, st: tile(t, kt, st[0], st[1], st[2], False), (zeros, zeros, dq0))
        if sub % 2 == 0:
            half = kt // 2
            state = tile(2 * last, half, *state, True)
            state = lax.cond((qb * CHUNK) % kt >= half,
                             lambda cp, cd, dq: tile(2 * last + 1, half, cp, cd, dq, True), lambda cp, cd, dq: (cp, cd, dq), *state)
        else:
            state = tile(last, kt, *state, True)
        for p in range(SB_PAIRS):
            dq_ref[:, lanes[2 * p]] = jnp.where(halves[0], state[2][2 * p], state[2][2 * p + 1])

        @pl.when((pl.program_id(0) == steps - 1) & (qb == nq - 1))
        def _():
            exchange.wait()

    return pl.pallas_call(
        body, name=name, grid=(steps, nq),
        in_specs=[qblk, full, full, doblk, qblk] + [hbm] * na,
        out_specs=[qblk, resident, resident] + [hbm] * na,
        out_shape=[jax.ShapeDtypeStruct((l, D_SB), F32)] * 3 + [jax.ShapeDtypeStruct(b.shape, b.dtype) for b in blocks],
        scratch_shapes=_DirectExchange.scratch(na),
    )(q, k, v, dycat, tot, *blocks)


def _ada_bwd(c_all, dmod_shard, name):
    def body(c_ref, d_ref, o_ref):
        cv = c_ref[...]
        o_ref[...] = _dot_tn(cv * _sigmoid(cv), d_ref[...], HIGHEST)

    return pl.pallas_call(body, name=name, out_shape=jax.ShapeDtypeStruct((c_all.shape[1], dmod_shard.shape[1]), F32))(c_all, dmod_shard)


def _sum_small(parts, name):
    def body(p_ref, o_ref):
        acc = p_ref[0]
        for d in range(1, N_DEV):
            acc = acc + p_ref[d]
        o_ref[...] = acc

    return pl.pallas_call(body, name=name, out_shape=jax.ShapeDtypeStruct(parts.shape[1:], F32))(parts)


def _adamw(w, g, m, v, name):
    rows, cols = w.shape
    tr = _pick(rows, 256, 8)
    spec = pl.BlockSpec((tr, cols), lambda i: (i, 0))
    bc1 = 1.0 - ADAM_B1 ** ADAM_STEP
    bc2 = 1.0 - ADAM_B2 ** ADAM_STEP

    def body(w_ref, g_ref, m_ref, v_ref, d_ref, nm_ref, nv_ref):
        gv = g_ref[...]
        nm = ADAM_B1 * m_ref[...] + (1.0 - ADAM_B1) * gv
        nv = ADAM_B2 * v_ref[...] + (1.0 - ADAM_B2) * (gv * gv)
        nm_ref[...] = nm
        nv_ref[...] = nv
        d_ref[...] = -ADAM_LR * ((nm / bc1) / (jnp.sqrt(nv / bc2) + ADAM_EPS) + ADAM_WD * w_ref[...])

    return pl.pallas_call(body, name=name, grid=(rows // tr,), in_specs=[spec] * 4, out_specs=[spec] * 3,
                          out_shape=[jax.ShapeDtypeStruct((rows, cols), F32)] * 3)(w, g, m, v)


def _pad_w_in_t(w_in_t):
    lo = D_SSD + D_CONV
    return jnp.concatenate([w_in_t[D_SSD:lo + N_HEADS], jnp.zeros((DT_PAD - N_HEADS, w_in_t.shape[1]), w_in_t.dtype),
                            w_in_t[:D_SSD], w_in_t[lo + N_HEADS:]], axis=0)


def _local_step(x, target, mod, norm1_w, w_in_tp, conv_w, conv_b, dt_bias, a_log, d_skip, ssd_norm_w, q_norm_w, k_norm_w,
                norm2_w, later_shards):
    sh1, sc1, g1, sh2, sc2, g2 = [mod[:, i * D_MODEL:(i + 1) * D_MODEL] for i in range(N_MOD)]
    qw2, kw2 = jnp.tile(q_norm_w, (1, 2)), jnp.tile(k_norm_w, (1, 2))
    dskip_x = jnp.repeat(d_skip, HEAD_DIM, axis=1)
    dtb_x, dtb_y = jnp.repeat(dt_bias, HEAD_DIM, axis=1), jnp.repeat(dt_bias, HEAD_TILE, axis=1)
    al_x, al_y = jnp.repeat(a_log, HEAD_DIM, axis=1), jnp.repeat(a_log, HEAD_TILE, axis=1)
    head_ids = jnp.arange(HEAD_TILE, dtype=jnp.int32)[:, None]
    spread_x = (head_ids == jnp.arange(D_SSD, dtype=jnp.int32)[None, :] // HEAD_DIM).astype(BF16)
    spread_y = (head_ids == jnp.arange(D_HEAD_TILES, dtype=jnp.int32)[None, :] // HEAD_TILE).astype(BF16)

    h1 = _rms_mod(x, norm1_w, sc1, sh1, "rms_mod1")
    proj = _matmul_nt(h1, w_in_tp, "in_proj")

    xc = _conv_silu(proj, conv_w, conv_b, "conv_silu")
    ycore, states = _ssd_fwd(xc, proj, spread_x, spread_y, dtb_x, dtb_y, al_x, al_y, "ssd_fwd")
    y_ssd = _ssd_gate_norm(ycore, xc, proj, dskip_x, ssd_norm_w, "ssd_gate_norm")

    qn, kn, vb = _qkv_prep(proj, qw2, kw2, "qkv_prep")
    o_sb, tot, *gathered = _sb_fwd(qn, kn, vb, later_shards, "sb_fwd")
    w_out, w_gate_t, w_up_t, w_down = [g.reshape(N_DEV * g.shape[1], D_MODEL) for g in gathered]
    ycat = jnp.concatenate([y_ssd, o_sb.astype(MXU_DTYPE)], axis=1)

    mix = _matmul(ycat, w_out, "out_proj")
    x1, h2 = _residual_rms_mod(x, mix, g1, norm2_w, sc2, sh2, "residual_rms_mod2")
    gate, up, act = _ffn_in(h2, w_gate_t, w_up_t, "ffn_in")
    ffn = _matmul(act, w_down, "ffn_down")
    dy, dffn, st_g2, loss_blk = _loss_head(x1, ffn, g2, target, "loss_head")

    g_down = _matmul_tn(act, dffn, "ffn_down_dw", MXU_DTYPE)
    dgate, dup = _ffn_down_bwd(dffn, w_down, gate, up, "ffn_down_dx")
    g_gate_t = _matmul_tn(dgate, h2, "ffn_gate_dw", MXU_DTYPE)
    g_up_t = _matmul_tn(dup, h2, "ffn_up_dw", MXU_DTYPE)
    dh2 = _matmul_sum2(dgate, w_gate_t, dup, w_up_t, "ffn_dh")
    dx1, dmix, st_n2 = _norm_bwd(x1, dh2, dy, norm2_w, sc2, "norm2_bwd", gated=(mix, g1))

    g_out = _matmul_tn(ycat, dmix, "out_proj_dw", MXU_DTYPE)
    dycat = _matmul_nt(dmix, w_out, "out_proj_dx")

    partials = [g.reshape(N_DEV, g.shape[0] // N_DEV, D_MODEL).astype(BF16) for g in (g_out, g_gate_t, g_up_t, g_down)]
    dqn, dkn, dv, *slots = _sb_bwd(qn, kn, vb, dycat, tot, partials, "sb_bwd")
    dq, dk, st_q, st_k = _qk_norm_bwd(proj, dqn, dkn, qw2, kw2, "qk_norm_bwd")

    dycore, dxs_skip, dz, st_gn = _ssd_gate_norm_bwd(dycat, ycore, xc, proj, dskip_x, ssd_norm_w, "ssd_gate_norm_bwd")
    dxc, ddt_tile, st_ssd, st_heads = _ssd_bwd(xc, proj, spread_x, spread_y, dtb_x, dtb_y, al_x, al_y, dycore, states, "ssd_bwd")
    dxc = jnp.concatenate([dxc[:, :D_SSD] + dxs_skip, dxc[:, D_SSD:]], axis=1)
    dpre, st_conv = _conv_silu_bwd_pre(proj, dxc, conv_w, conv_b, "conv_silu_bwd")
    dxbc = _conv_bwd_input(dpre, conv_w, "conv_bwd_input")

    pieces = {"z": (dz, O_Z), "xbc": (dxbc, O_XBC), "dt": (ddt_tile, O_DT), "q": (dq, O_Q), "k": (dk, O_Q + D_SB), "v": (dv, O_Q + 2 * D_SB)}
    g_rows = {n: _matmul_tn(p, h1, "in_proj_dw_" + n, MXU_DTYPE) for n, (p, _) in pieces.items()}
    g_rows["dt"] = g_rows["dt"][:N_HEADS]
    g_in_t = jnp.concatenate(list(g_rows.values()), axis=0)
    g_in_blocks = g_in_t.reshape(N_DEV, g_in_t.shape[0] // N_DEV, D_MODEL).astype(BF16)
    dh1, g_in_slots = _matmul_pieces_beside_scatter([p for p, _ in pieces.values()], [o for _, o in pieces.values()], w_in_tp,
                                                    [g_in_blocks], "in_proj_dx")
    slots = [g_in_slots] + list(slots)
    grad_x, st_n1 = _norm_bwd(x, dh1, dx1, norm1_w, sc1, "norm1_bwd")

    pad = jnp.zeros((1, SM_SSD_NORM - SM_D_SKIP - N_HEADS), F32)
    small = jnp.concatenate(
        [st_n1[1:2], st_n1[0:1], st_n2[3:4], st_n2[1:2], st_n2[0:1], st_g2[0:1],
         st_n1[2:3], st_conv[4:5], st_conv[0:4].reshape(1, CONV_WIDTH * D_CONV),
         st_ssd[1:2, ::HEAD_DIM], st_ssd[0:1, ::HEAD_DIM], st_heads[2:3, ::HEAD_DIM], pad,
         st_gn[0:1], st_q[0:1, :HEAD_DIM] + st_q[0:1, HEAD_DIM:], st_k[0:1, :HEAD_DIM] + st_k[0:1, HEAD_DIM:], st_n2[2:3]], axis=1)
    return loss_blk, grad_x, slots, small


def kernel(x, c, w_ada, b_ada, norm1_w, w_in, conv_w, conv_b, dt_bias, a_log, d_skip, ssd_norm_w, q_norm_w, k_norm_w, w_out, norm2_w, w_gate, w_up, w_down, loss_target, m_w_ada, m_b_ada, m_norm1_w, m_w_in, m_conv_w, m_conv_b, m_dt_bias, m_a_log, m_d_skip, m_ssd_norm_w, m_q_norm_w, m_k_norm_w, m_w_out, m_norm2_w, m_w_gate, m_w_up, m_w_down, v_w_ada, v_b_ada, v_norm1_w, v_w_in, v_conv_w, v_conv_b, v_dt_bias, v_a_log, v_d_skip, v_ssd_norm_w, v_q_norm_w, v_k_norm_w, v_w_out, v_norm2_w, v_w_gate, v_w_up, v_w_down):
    me = 4 * lax.axis_index("x") + 2 * lax.axis_index("y") + lax.axis_index("c")
    conv_cols = D_CONV // N_DEV
    ada_cols = N_MOD * D_MODEL // N_DEV

    cond_conv = jnp.concatenate([c, conv_w[0].reshape(1, CONV_WIDTH * conv_cols)], axis=1)
    first, mods, w_in_t = _prologue(cond_conv, w_ada[0], w_in[0].T.astype(BF16), "prologue")
    w_in_tp = _pad_w_in_t(w_in_t.reshape(D_IN_PROJ, D_MODEL))
    later_shards = [s.astype(BF16) for s in (w_out[0], w_gate[0].T, w_up[0].T, w_down[0])]
    c_all = first[:, 0, :D_MODEL]
    conv_w_f = first[:, 0, D_MODEL:].reshape(N_DEV, CONV_WIDTH, conv_cols).transpose(1, 0, 2).reshape(CONV_WIDTH, D_CONV)
    mod = lax.dynamic_index_in_dim(mods, me, axis=1, keepdims=False)
    mod = mod.reshape(1, N_MOD * D_MODEL) + b_ada

    loss_blk, grad_x, slots, small = _local_step(
        x[0], loss_target[0], mod, norm1_w, w_in_tp, conv_w_f, conv_b, dt_bias, a_log, d_skip, ssd_norm_w, q_norm_w, k_norm_w,
        norm2_w, later_shards)
    loss = lax.psum(loss_blk[0, 0], ("x", "y", "c"))

    g_in_t, g_out, g_gate_t, g_up_t, g_down = [_sum_slots(s, "sum_grads_" + n) for s, n in zip(slots, ("in", "out", "gate", "up", "down"))]
    g_in, g_gate, g_up = g_in_t.T, g_gate_t.T, g_up_t.T

    parts = _all_gather_small(small, "gather_small")
    gsum = _sum_small(parts, "sum_small")
    dmod_shard = lax.dynamic_slice_in_dim(parts[:, 0, :N_MOD * D_MODEL], me * ada_cols, ada_cols, axis=1)
    g_ada = _ada_bwd(c_all, dmod_shard, "ada_bwd")
    g_conv_w = lax.dynamic_slice_in_dim(gsum[:, SM_CONV_W:SM_DT_BIAS].reshape(CONV_WIDTH, D_CONV), me * conv_cols, conv_cols, axis=1)

    def pack_small(b_ada_, norm1_, conv_b_, dt_bias_, a_log_, d_skip_, ssd_norm_, q_norm_, k_norm_, norm2_):
        return jnp.concatenate(
            [b_ada_, norm1_, conv_b_, jnp.zeros((1, CONV_WIDTH * D_CONV), F32), dt_bias_, a_log_, d_skip_,
             jnp.zeros((1, SM_SSD_NORM - SM_D_SKIP - N_HEADS), F32), ssd_norm_, q_norm_, k_norm_, norm2_], axis=1)

    def unpack_small(p):
        return {
            "b_ada": p[:, SM_B_ADA:SM_NORM1], "norm1_w": p[:, SM_NORM1:SM_CONV_B], "conv_b": p[:, SM_CONV_B:SM_CONV_W],
            "dt_bias": p[:, SM_DT_BIAS:SM_A_LOG], "a_log": p[:, SM_A_LOG:SM_D_SKIP], "d_skip": p[:, SM_D_SKIP:SM_D_SKIP + N_HEADS],
            "ssd_norm_w": p[:, SM_SSD_NORM:SM_Q_NORM], "q_norm_w": p[:, SM_Q_NORM:SM_K_NORM], "k_norm_w": p[:, SM_K_NORM:SM_NORM2],
            "norm2_w": p[:, SM_NORM2:SM_TOTAL]}

    w_small = pack_small(b_ada, norm1_w, conv_b, dt_bias, a_log, d_skip, ssd_norm_w, q_norm_w, k_norm_w, norm2_w)
    m_small = pack_small(m_b_ada, m_norm1_w, m_conv_b, m_dt_bias, m_a_log, m_d_skip, m_ssd_norm_w, m_q_norm_w, m_k_norm_w, m_norm2_w)
    v_small = pack_small(v_b_ada, v_norm1_w, v_conv_b, v_dt_bias, v_a_log, v_d_skip, v_ssd_norm_w, v_q_norm_w, v_k_norm_w, v_norm2_w)
    small_out = [unpack_small(t) for t in (gsum,) + tuple(_adamw(w_small, gsum, m_small, v_small, "adamw_small"))]

    sharded = {
        "w_ada": (w_ada[0], g_ada, m_w_ada[0], v_w_ada[0]),
        "w_in": (w_in[0], g_in, m_w_in[0], v_w_in[0]),
        "conv_w": (conv_w[0], g_conv_w, m_conv_w[0], v_conv_w[0]),
        "w_out": (w_out[0], g_out, m_w_out[0], v_w_out[0]),
        "w_gate": (w_gate[0], g_gate, m_w_gate[0], v_w_gate[0]),
        "w_up": (w_up[0], g_up, m_w_up[0], v_w_up[0]),
        "w_down": (w_down[0], g_down, m_w_down[0], v_w_down[0]),
    }
    sharded_out = {n: (t[1],) + tuple(_adamw(*t, "adamw_" + n)) for n, t in sharded.items()}

    names = ["w_ada", "b_ada", "norm1_w", "w_in", "conv_w", "conv_b", "dt_bias", "a_log", "d_skip", "ssd_norm_w", "q_norm_w",
             "k_norm_w", "w_out", "norm2_w", "w_gate", "w_up", "w_down"]
    outs = [loss, grad_x[None]]
    for kind in range(4):
        for n in names:
            outs.append(sharded_out[n][kind][None] if n in sharded_out else small_out[kind][n])
    return tuple(outs)
```

```python
import functools

import jax
import jax.numpy as jnp
from jax import lax
from jax.experimental import pallas as pl
from jax.experimental.pallas import tpu as pltpu

F32 = jnp.float32
BF16 = jnp.bfloat16
MXU_DTYPE = jnp.bfloat16
HIGHEST = lax.Precision.HIGHEST
MESH_IDS = pl.DeviceIdType.MESH

N_DEV = 8
D_MODEL = 1024
HEAD_DIM = 64
N_HEADS = 16
D_SSD = 1024
D_SB = 1024
SSD_GROUPS = 2
SSD_STATE = 128
GROUP_WIDTH = D_SSD // SSD_GROUPS
D_CONV = D_SSD + 2 * SSD_GROUPS * SSD_STATE
CONV_WIDTH = 4
CHUNK = 128
D_FF = 2816
N_MOD = 6
EPS = 1e-6
D_IN_PROJ = 5648
O_XBC = 0
O_DT = D_CONV
DT_PAD = 512
O_Z = O_DT + DT_PAD
O_Q = O_Z + D_SSD
D_IN_PAD = O_Q + 3 * D_SB
HALO = 8

ADAM_LR = 0.001
ADAM_B1 = 0.9
ADAM_B2 = 0.999
ADAM_EPS = 1e-08
ADAM_WD = 0.01
ADAM_STEP = 10

SM_B_ADA = 0
SM_NORM1 = 6144
SM_CONV_B = 7168
SM_CONV_W = 8704
SM_DT_BIAS = 14848
SM_A_LOG = 14864
SM_D_SKIP = 14880
SM_SSD_NORM = 14976
SM_Q_NORM = 16000
SM_K_NORM = 16064
SM_NORM2 = 16128
SM_TOTAL = 17152


def _pick(n, cap, mult):
    if n <= cap:
        return n
    best = None
    for t in range(mult, cap + 1, mult):
        if n % t == 0:
            best = t
    assert best is not None, (n, cap, mult)
    return best


def _dot(a, b, precision=None):
    return jnp.dot(a, b, preferred_element_type=F32, precision=precision)


def _dot_nt(a, b, precision=None):
    return lax.dot_general(a, b, (((1,), (1,)), ((), ())), preferred_element_type=F32, precision=precision)


def _dot_tn(a, b, precision=None):
    return lax.dot_general(a, b, (((0,), (0,)), ((), ())), preferred_element_type=F32, precision=precision)


def _softplus(v):
    return jnp.maximum(v, 0.0) + jnp.log1p(jnp.exp(-jnp.abs(v)))


def _softplus_logits(v):
    return jnp.maximum(v, 0.0) + jnp.log(1.0 + jnp.exp(-jnp.abs(v)))


def _tri_dot(tri_b, v):
    hi = v.astype(BF16)
    r1 = v - hi.astype(F32)
    mid = r1.astype(BF16)
    lo = (r1 - mid.astype(F32)).astype(BF16)
    return _dot(tri_b, hi) + _dot(tri_b, mid) + _dot(tri_b, lo)


def _sigmoid(v):
    return jax.nn.sigmoid(v)


def _colsum(v):
    return jnp.sum(v, axis=0, keepdims=True)


def _split_dot(v, tri2):
    hi = v.astype(BF16)
    lo = (v - hi.astype(F32)).astype(BF16)
    return _dot(jnp.concatenate([hi, lo], axis=1), tri2)


def _position():
    x, y, c = lax.axis_index("x"), lax.axis_index("y"), lax.axis_index("c")
    return x, y, c


def _peer(x, y, c, k):
    px = 1 - x if (k >> 2) & 1 else x
    py = 1 - y if (k >> 1) & 1 else y
    pc = 1 - c if k & 1 else c
    return px, py, pc


def _all_gather_small(v, name):
    n = v.shape[1]

    def body(v_ref, out_ref, send_sems, recv_sems, local_sem):
        x, y, c = _position()
        me = 4 * x + 2 * y + c
        mine = pltpu.make_async_copy(v_ref, out_ref.at[me], local_sem)
        mine.start()
        sends = []
        for k in range(1, N_DEV):
            cp = pltpu.make_async_remote_copy(
                src_ref=v_ref, dst_ref=out_ref.at[me], send_sem=send_sems.at[k - 1], recv_sem=recv_sems.at[k - 1],
                device_id=_peer(x, y, c, k), device_id_type=MESH_IDS)
            cp.start()
            sends.append(cp)
        for k in range(1, N_DEV):
            px, py, pc = _peer(x, y, c, k)
            pltpu.make_async_remote_copy(
                src_ref=v_ref, dst_ref=out_ref.at[4 * px + 2 * py + pc], send_sem=send_sems.at[k - 1],
                recv_sem=recv_sems.at[k - 1], device_id=(px, py, pc), device_id_type=MESH_IDS).wait_recv()
        for cp in sends:
            cp.wait_send()
        mine.wait()

    return pl.pallas_call(
        body, name=name,
        out_shape=jax.ShapeDtypeStruct((N_DEV, 1, n), v.dtype),
        in_specs=[pl.BlockSpec(memory_space=pltpu.VMEM)],
        out_specs=pl.BlockSpec(memory_space=pltpu.VMEM),
        scratch_shapes=[pltpu.SemaphoreType.DMA((N_DEV - 1,)), pltpu.SemaphoreType.DMA((N_DEV - 1,)),
                        pltpu.SemaphoreType.DMA],
    )(v)


def _prologue(cond_conv, w_ada_shard, w_in_shard, name):
    def body(cc_ref, wada_ref, win_ref, first_ref, mods_ref, wout_ref, mod_cols, conds_all, big_send, big_recv, big_local,
             a_send, a_recv, a_local, b_send, b_recv, b_local):
        x, y, c = _position()
        me, sibling = (x, y, c), (x, y, 1 - c)
        chips = [(1 - x, y), (x, 1 - y), (1 - x, 1 - y)]

        def big(k, blk, to, own=False):
            slot = wout_ref.at[4 * blk[0] + 2 * blk[1] + blk[2]]
            return pltpu.make_async_remote_copy(src_ref=win_ref if own else slot, dst_ref=slot, send_sem=big_send.at[k],
                                                recv_sem=big_recv.at[k], device_id=to, device_id_type=MESH_IDS)

        mine = pltpu.make_async_copy(win_ref, wout_ref.at[4 * x + 2 * y + c], big_local.at[0])
        mine.start()
        first = [big(0, me, sibling, own=True)] + [big(1 + j, me, (*chip, c), own=True) for j, chip in enumerate(chips)]
        for cp in first:
            cp.start()

        conds = _DirectExchange("gather", [cc_ref], [first_ref], a_send, a_recv, a_local)
        conds.start()
        conds.wait()
        for d in range(N_DEV):
            conds_all[d:d + 1, :] = first_ref[d, :, 0:D_MODEL]
        cv = conds_all[...]
        mod_cols[...] = _dot(cv * _sigmoid(cv), wada_ref[...], HIGHEST)
        mods = _DirectExchange("gather", [mod_cols], [mods_ref], b_send, b_recv, b_local)
        mods.start()
        mods.wait()

        passed = []
        for j, chip in enumerate(chips):
            big(1 + j, (*chip, c), me).wait_recv()
            passed.append(big(4 + j, (*chip, c), sibling))
            passed[-1].start()
        big(0, sibling, me).wait_recv()
        for j, chip in enumerate(chips):
            big(4 + j, (*chip, 1 - c), me).wait_recv()
        for cp in first + passed:
            cp.wait_send()
        mine.wait()

    vmem = pl.BlockSpec(memory_space=pltpu.VMEM)
    hbm = pl.BlockSpec(memory_space=pl.ANY)
    cols = w_ada_shard.shape[1]
    return pl.pallas_call(
        body, name=name,
        out_shape=[jax.ShapeDtypeStruct((N_DEV,) + cond_conv.shape, F32), jax.ShapeDtypeStruct((N_DEV, N_DEV, cols), F32),
                   jax.ShapeDtypeStruct((N_DEV,) + w_in_shard.shape, w_in_shard.dtype)],
        in_specs=[vmem, vmem, hbm], out_specs=[vmem, vmem, hbm],
        scratch_shapes=[pltpu.VMEM((N_DEV, cols), F32), pltpu.VMEM((N_DEV, D_MODEL), F32)] + _DirectExchange.scratch(1) * 3,
    )(cond_conv, w_ada_shard, w_in_shard)


class _DirectExchange:
    def __init__(self, kind, src_refs, dst_refs, send_sems, recv_sems, local_sems):
        self.kind, self.src_refs, self.dst_refs = kind, src_refs, dst_refs
        self.send_sems, self.recv_sems, self.local_sems = send_sems, recv_sems, local_sems

    @staticmethod
    def scratch(n_arrays):
        copies = N_DEV - 1
        return [pltpu.SemaphoreType.DMA((copies * n_arrays,)), pltpu.SemaphoreType.DMA((copies * n_arrays,)),
                pltpu.SemaphoreType.DMA((n_arrays,))]

    def _copies(self):
        x, y, c = _position()
        me = 4 * x + 2 * y + c
        local, sends, arrivals = [], [], []
        for a, (src, dst) in enumerate(zip(self.src_refs, self.dst_refs)):
            own = src if self.kind == "gather" else src.at[me]
            local.append(pltpu.make_async_copy(own, dst.at[me], self.local_sems.at[a]))
            for k in range(1, N_DEV):
                px, py, pc = _peer(x, y, c, k)
                peer = 4 * px + 2 * py + pc
                sems = dict(send_sem=self.send_sems.at[(N_DEV - 1) * a + k - 1], recv_sem=self.recv_sems.at[(N_DEV - 1) * a + k - 1],
                            device_id=(px, py, pc), device_id_type=MESH_IDS)
                sends.append(pltpu.make_async_remote_copy(
                    src_ref=src if self.kind == "gather" else src.at[peer], dst_ref=dst.at[me], **sems))
                arrivals.append(pltpu.make_async_remote_copy(src_ref=own, dst_ref=dst.at[peer], **sems))
        return local, sends, arrivals

    def start(self):
        local, sends, _ = self._copies()
        for cp in local + sends:
            cp.start()

    def wait(self):
        local, sends, arrivals = self._copies()
        for cp in arrivals:
            cp.wait_recv()
        for cp in sends:
            cp.wait_send()
        for cp in local:
            cp.wait()


def _sum_slots(slots, name):
    _, r, n = slots.shape
    tr = _pick(r, 256, 16) if r % 16 == 0 else r

    def body(s_ref, o_ref):
        acc = s_ref[0].astype(F32)
        for d in range(1, N_DEV):
            acc = acc + s_ref[d].astype(F32)
        o_ref[...] = acc

    return pl.pallas_call(
        body, name=name, grid=(r // tr,),
        in_specs=[pl.BlockSpec((N_DEV, tr, n), lambda i: (0, i, 0))],
        out_specs=pl.BlockSpec((tr, n), lambda i: (i, 0)),
        out_shape=jax.ShapeDtypeStruct((r, n), F32),
    )(slots)


def _matmul(a, b, name, out_dtype=F32):
    m, k = a.shape
    _, n = b.shape
    tm, tn, tk = _pick(m, 1024, 16), _pick(n, 1408, 128), _pick(k, 1408, 128)
    nk = k // tk

    def body(a_ref, b_ref, o_ref, acc_ref):
        kk = pl.program_id(2)

        @pl.when(kk == 0)
        def _():
            acc_ref[...] = jnp.zeros_like(acc_ref)

        acc_ref[...] += _dot(a_ref[...].astype(MXU_DTYPE), b_ref[...].astype(MXU_DTYPE))

        @pl.when(kk == nk - 1)
        def _():
            o_ref[...] = acc_ref[...].astype(o_ref.dtype)

    return pl.pallas_call(
        body, name=name, grid=(m // tm, n // tn, nk),
        in_specs=[pl.BlockSpec((tm, tk), lambda i, j, kk: (i, kk)), pl.BlockSpec((tk, tn), lambda i, j, kk: (kk, j))],
        out_specs=pl.BlockSpec((tm, tn), lambda i, j, kk: (i, j)),
        out_shape=jax.ShapeDtypeStruct((m, n), out_dtype),
        scratch_shapes=[pltpu.VMEM((tm, tn), F32)],
    )(a, b)


def _matmul_pieces_beside_scatter(pieces, offsets, w, blocks, name):
    m = pieces[0].shape[0]
    n = w.shape[1]
    tm = _pick(m, 256, 16)
    gm = m // tm
    npieces, na = len(pieces), len(blocks)
    hbm = pl.BlockSpec(memory_space=pl.ANY)

    def body(*refs):
        a_refs, w_refs = refs[:npieces], refs[npieces:2 * npieces]
        rest = refs[2 * npieces:]
        o_ref = rest[na]
        exchange = _DirectExchange("scatter", rest[:na], rest[na + 1:2 * na + 1], *rest[2 * na + 1:])
        i = pl.program_id(0)

        @pl.when(i == 0)
        def _():
            exchange.start()

        acc = _dot(a_refs[0][...].astype(MXU_DTYPE), w_refs[0][...])
        for a_ref, w_ref in zip(a_refs[1:], w_refs[1:]):
            acc = acc + _dot(a_ref[...].astype(MXU_DTYPE), w_ref[...])
        o_ref[...] = acc

        @pl.when(i == gm - 1)
        def _():
            exchange.wait()

    a_specs = [pl.BlockSpec((tm, p.shape[1]), lambda i: (i, 0)) for p in pieces]
    w_specs = [pl.BlockSpec((p.shape[1], n), functools.partial(lambda i, blk: (blk, 0), blk=off // p.shape[1]))
               for p, off in zip(pieces, offsets)]
    return pl.pallas_call(
        body, name=name, grid=(gm,),
        in_specs=a_specs + w_specs + [hbm] * na,
        out_specs=[pl.BlockSpec((tm, n), lambda i: (i, 0))] + [hbm] * na,
        out_shape=[jax.ShapeDtypeStruct((m, n), F32)] + [jax.ShapeDtypeStruct(blk.shape, blk.dtype) for blk in blocks],
        scratch_shapes=_DirectExchange.scratch(na),
    )(*pieces, *([w] * npieces), *blocks)


def _matmul_nt(a, bt, name, out_dtype=F32):
    m, k = a.shape
    n, _ = bt.shape
    tm, tn, tk = _pick(m, 1024, 16), _pick(n, 1408, 128), _pick(k, 1408, 128)
    nk = k // tk

    def body(a_ref, b_ref, o_ref, acc_ref):
        kk = pl.program_id(2)

        @pl.when(kk == 0)
        def _():
            acc_ref[...] = jnp.zeros_like(acc_ref)

        acc_ref[...] += _dot_nt(a_ref[...].astype(MXU_DTYPE), b_ref[...].astype(MXU_DTYPE))

        @pl.when(kk == nk - 1)
        def _():
            o_ref[...] = acc_ref[...].astype(o_ref.dtype)

    return pl.pallas_call(
        body, name=name, grid=(m // tm, n // tn, nk),
        in_specs=[pl.BlockSpec((tm, tk), lambda i, j, kk: (i, kk)), pl.BlockSpec((tn, tk), lambda i, j, kk: (j, kk))],
        out_specs=pl.BlockSpec((tm, tn), lambda i, j, kk: (i, j)),
        out_shape=jax.ShapeDtypeStruct((m, n), out_dtype),
        scratch_shapes=[pltpu.VMEM((tm, tn), F32)],
    )(a, bt)


def _matmul_sum2(a1, b1, a2, b2, name):
    m, k = a1.shape
    _, n = b1.shape
    tm, tn, tk = _pick(m, 1024, 16), _pick(n, 1408, 128), _pick(k, 1408, 128)
    nk = k // tk

    def body(a1_ref, b1_ref, a2_ref, b2_ref, o_ref):
        @pl.when(pl.program_id(2) == 0)
        def _():
            o_ref[...] = jnp.zeros_like(o_ref)

        o_ref[...] += (_dot(a1_ref[...].astype(MXU_DTYPE), b1_ref[...].astype(MXU_DTYPE))
                       + _dot(a2_ref[...].astype(MXU_DTYPE), b2_ref[...].astype(MXU_DTYPE)))

    a_spec = pl.BlockSpec((tm, tk), lambda i, j, kk: (i, kk))
    b_spec = pl.BlockSpec((tk, tn), lambda i, j, kk: (kk, j))
    return pl.pallas_call(
        body, name=name, grid=(m // tm, n // tn, nk), in_specs=[a_spec, b_spec, a_spec, b_spec],
        out_specs=pl.BlockSpec((tm, tn), lambda i, j, kk: (i, j)),
        out_shape=jax.ShapeDtypeStruct((m, n), F32),
    )(a1, b1, a2, b2)


def _matmul_tn(a, b, name, out_dtype=F32):
    l, m = a.shape
    _, n = b.shape
    tm, tn, tl = _pick(m, 1408, 128), _pick(n, 1408, 128), _pick(l, 512, 16)
    nl = l // tl

    def body(a_ref, b_ref, o_ref, acc_ref):
        ll = pl.program_id(2)

        @pl.when(ll == 0)
        def _():
            acc_ref[...] = jnp.zeros_like(acc_ref)

        acc_ref[...] += _dot_tn(a_ref[...].astype(MXU_DTYPE), b_ref[...].astype(MXU_DTYPE))

        @pl.when(ll == nl - 1)
        def _():
            o_ref[...] = acc_ref[...].astype(o_ref.dtype)

    return pl.pallas_call(
        body, name=name, grid=(m // tm, n // tn, nl),
        in_specs=[pl.BlockSpec((tl, tm), lambda i, j, ll: (ll, i)), pl.BlockSpec((tl, tn), lambda i, j, ll: (ll, j))],
        out_specs=pl.BlockSpec((tm, tn), lambda i, j, ll: (i, j)),
        out_shape=jax.ShapeDtypeStruct((m, n), out_dtype),
        scratch_shapes=[pltpu.VMEM((tm, tn), F32)],
    )(a, b)


def _row_specs(l, d, n_rows, n_vecs):
    tl = _pick(l, 512, 16)
    row = pl.BlockSpec((tl, d), lambda i: (i, 0))
    vec = pl.BlockSpec((1, d), lambda i: (0, 0))
    return tl, row, vec, [row] * n_rows + [vec] * n_vecs


def _rms_mod(x, nw, sc, sh, name):
    l, d = x.shape
    tl, row, _, in_specs = _row_specs(l, d, 1, 3)

    def body(x_ref, nw_ref, sc_ref, sh_ref, h_ref):
        xv = x_ref[...]
        r = lax.rsqrt(jnp.mean(xv * xv, axis=-1, keepdims=True) + EPS)
        h_ref[...] = (((xv * r) * nw_ref[...]) * (1.0 + sc_ref[...]) + sh_ref[...]).astype(h_ref.dtype)

    return pl.pallas_call(body, name=name, grid=(l // tl,), in_specs=in_specs, out_specs=row,
                          out_shape=jax.ShapeDtypeStruct((l, d), MXU_DTYPE))(x, nw, sc, sh)


def _residual_rms_mod(x, mix, g, nw, sc, sh, name):
    l, d = x.shape
    tl, row, _, in_specs = _row_specs(l, d, 2, 4)

    def body(x_ref, mix_ref, g_ref, nw_ref, sc_ref, sh_ref, x1_ref, h_ref):
        xv = x_ref[...] + g_ref[...] * mix_ref[...]
        x1_ref[...] = xv
        r = lax.rsqrt(jnp.mean(xv * xv, axis=-1, keepdims=True) + EPS)
        h_ref[...] = (((xv * r) * nw_ref[...]) * (1.0 + sc_ref[...]) + sh_ref[...]).astype(h_ref.dtype)

    return pl.pallas_call(body, name=name, grid=(l // tl,), in_specs=in_specs, out_specs=[row, row],
                          out_shape=[jax.ShapeDtypeStruct((l, d), F32), jax.ShapeDtypeStruct((l, d), MXU_DTYPE)],
                          )(x, mix, g, nw, sc, sh)


def _norm_bwd(x, dh, dres, nw, sc, name, gated=None):
    l, d = x.shape
    tl, row, vec, in_specs = _row_specs(l, d, 3, 2)
    stats = pl.BlockSpec((8, d), lambda i: (0, 0))

    def body(x_ref, dh_ref, dres_ref, nw_ref, sc_ref, *rest):
        dx_ref, st_ref = rest[-2:] if gated is None else (rest[2], rest[4])
        xv, dh_v = x_ref[...], dh_ref[...]
        r = lax.rsqrt(jnp.mean(xv * xv, axis=-1, keepdims=True) + EPS)
        xn = xv * r
        dxn = dh_v * (nw_ref[...] * (1.0 + sc_ref[...]))
        dx = dres_ref[...] + r * (dxn - xn * jnp.mean(dxn * xn, axis=-1, keepdims=True))
        dx_ref[...] = dx

        @pl.when(pl.program_id(0) == 0)
        def _():
            st_ref[...] = jnp.zeros_like(st_ref)

        dhx = dh_v * xn
        st_ref[0:1, :] += _colsum(dhx * nw_ref[...])
        st_ref[1:2, :] += _colsum(dh_v)
        st_ref[2:3, :] += _colsum(dhx * (1.0 + sc_ref[...]))
        if gated is not None:
            val_ref, g_ref, dval_ref = rest[0], rest[1], rest[3]
            dval_ref[...] = (g_ref[...] * dx).astype(dval_ref.dtype)
            st_ref[3:4, :] += _colsum(dx * val_ref[...])

    if gated is None:
        return pl.pallas_call(body, name=name, grid=(l // tl,), in_specs=in_specs, out_specs=[row, stats],
                              out_shape=[jax.ShapeDtypeStruct((l, d), F32), jax.ShapeDtypeStruct((8, d), F32)],
                              )(x, dh, dres, nw, sc)
    return pl.pallas_call(body, name=name, grid=(l // tl,), in_specs=in_specs + [row, vec], out_specs=[row, row, stats],
                          out_shape=[jax.ShapeDtypeStruct((l, d), F32), jax.ShapeDtypeStruct((l, d), MXU_DTYPE),
                                     jax.ShapeDtypeStruct((8, d), F32)],
                          )(x, dh, dres, nw, sc, *gated)


def _loss_head(x1, f, g, target, name):
    l, d = x1.shape
    tl, row, _, _ = _row_specs(l, d, 0, 0)
    vec = pl.BlockSpec((1, d), lambda i: (0, 0))

    def body(x1_ref, f_ref, g_ref, t_ref, dy_ref, df_ref, st_ref, loss_ref):
        fv = f_ref[...]
        e = x1_ref[...] + g_ref[...] * fv - t_ref[...]
        dy = e * (1.0 / d)
        dy_ref[...] = dy
        df_ref[...] = (g_ref[...] * dy).astype(df_ref.dtype)

        @pl.when(pl.program_id(0) == 0)
        def _():
            loss_ref[...] = jnp.zeros_like(loss_ref)
            st_ref[...] = jnp.zeros_like(st_ref)

        st_ref[0:1, :] += _colsum(dy * fv)
        s = jnp.sum(jnp.sum(e * e, axis=1, keepdims=True), axis=0, keepdims=True)
        loss_ref[...] += (0.5 / d) * s

    return pl.pallas_call(body, name=name, grid=(l // tl,), in_specs=[row, row, vec, row],
                          out_specs=[row, row, pl.BlockSpec((8, d), lambda i: (0, 0)), pl.BlockSpec((8, 128), lambda i: (0, 0))],
                          out_shape=[jax.ShapeDtypeStruct((l, d), F32), jax.ShapeDtypeStruct((l, d), MXU_DTYPE),
                                     jax.ShapeDtypeStruct((8, d), F32), jax.ShapeDtypeStruct((8, 128), F32)],
                          )(x1, f, g, target)


def _ffn_in(h, w_gate_t, w_up_t, name):
    l, k = h.shape
    f = w_gate_t.shape[0]
    tl, tf = _pick(l, 512, 16), _pick(f, 1408, 128)
    h_spec = pl.BlockSpec((tl, k), lambda i, j: (i, 0))
    w_spec = pl.BlockSpec((tf, k), lambda i, j: (j, 0))
    spec = pl.BlockSpec((tl, tf), lambda i, j: (i, j))

    def body(h_ref, wg_ref, wu_ref, g_ref, u_ref, a_ref):
        hv = h_ref[...].astype(MXU_DTYPE)
        gv = _dot_nt(hv, wg_ref[...].astype(MXU_DTYPE))
        uv = _dot_nt(hv, wu_ref[...].astype(MXU_DTYPE))
        g_ref[...] = gv
        u_ref[...] = uv
        a_ref[...] = (gv * _sigmoid(gv) * uv).astype(a_ref.dtype)

    return pl.pallas_call(body, name=name, grid=(l // tl, f // tf), in_specs=[h_spec, w_spec, w_spec], out_specs=[spec, spec, spec],
                          out_shape=[jax.ShapeDtypeStruct((l, f), F32)] * 2 + [jax.ShapeDtypeStruct((l, f), MXU_DTYPE)],
                          )(h, w_gate_t, w_up_t)


def _ffn_down_bwd(dffn, w_down, gate, up, name):
    l, f = gate.shape
    k = dffn.shape[1]
    tl, tf = _pick(l, 512, 16), _pick(f, 1408, 128)
    spec = pl.BlockSpec((tl, tf), lambda i, j: (i, j))

    def body(d_ref, w_ref, g_ref, u_ref, dg_ref, du_ref):
        dav = _dot_nt(d_ref[...].astype(MXU_DTYPE), w_ref[...].astype(MXU_DTYPE))
        gv = g_ref[...]
        s = _sigmoid(gv)
        dg_ref[...] = (dav * u_ref[...] * (s * (1.0 + gv * (1.0 - s)))).astype(dg_ref.dtype)
        du_ref[...] = (dav * (gv * s)).astype(du_ref.dtype)

    return pl.pallas_call(body, name=name, grid=(l // tl, f // tf),
                          in_specs=[pl.BlockSpec((tl, k), lambda i, j: (i, 0)), pl.BlockSpec((tf, k), lambda i, j: (j, 0)), spec, spec],
                          out_specs=[spec, spec], out_shape=[jax.ShapeDtypeStruct((l, f), MXU_DTYPE)] * 2)(dffn, w_down, gate, up)


def _conv_tile(l):
    return _pick(l, 512, 16)


def _conv_pre(buf, w_ref, b_ref, tl):
    acc = b_ref[...] + w_ref[3:4, :] * buf[HALO:HALO + tl, :]
    for k in range(CONV_WIDTH - 1):
        s = HALO - (CONV_WIDTH - 1) + k
        acc = acc + w_ref[k:k + 1, :] * buf[s:s + tl, :]
    return acc


def _fill_past(buf, u_ref, halo_ref, tl):
    i = pl.program_id(0)

    @pl.when(i == 0)
    def _():
        buf[0:HALO, :] = jnp.zeros((HALO, buf.shape[1]), F32)

    @pl.when(i > 0)
    def _():
        buf[0:HALO, :] = halo_ref[...]

    buf[HALO:HALO + tl, :] = u_ref[...]


def _conv_silu(u, w, b, name):
    l, ch = u.shape[0], w.shape[1]
    tl = _conv_tile(l)
    per = tl // HALO
    cur = pl.BlockSpec((tl, ch), lambda i: (i, 0))
    past = pl.BlockSpec((HALO, ch), lambda i: (jnp.maximum(i * per - 1, 0), 0))

    def body(u_ref, halo_ref, w_ref, b_ref, o_ref, buf):
        _fill_past(buf, u_ref, halo_ref, tl)
        pre = _conv_pre(buf, w_ref, b_ref, tl)
        o_ref[...] = pre * _sigmoid(pre)

    return pl.pallas_call(body, name=name, grid=(l // tl,),
                          in_specs=[cur, past, pl.BlockSpec((CONV_WIDTH, ch), lambda i: (0, 0)), pl.BlockSpec((1, ch), lambda i: (0, 0))],
                          out_specs=cur, out_shape=jax.ShapeDtypeStruct((l, ch), F32),
                          scratch_shapes=[pltpu.VMEM((tl + HALO, ch), F32)])(u, u, w, b)


def _conv_silu_bwd_pre(u, dxc, w, b, name):
    l, ch = u.shape[0], w.shape[1]
    tl = _conv_tile(l)
    per = tl // HALO
    cur = pl.BlockSpec((tl, ch), lambda i: (i, 0))
    past = pl.BlockSpec((HALO, ch), lambda i: (jnp.maximum(i * per - 1, 0), 0))

    def body(u_ref, halo_ref, d_ref, w_ref, b_ref, dpre_ref, st_ref, buf):
        _fill_past(buf, u_ref, halo_ref, tl)
        pre = _conv_pre(buf, w_ref, b_ref, tl)
        s = _sigmoid(pre)
        dpre = d_ref[...] * (s * (1.0 + pre * (1.0 - s)))
        dpre_ref[...] = dpre

        @pl.when(pl.program_id(0) == 0)
        def _():
            st_ref[...] = jnp.zeros_like(st_ref)

        for k in range(CONV_WIDTH):
            s0 = HALO - (CONV_WIDTH - 1) + k
            st_ref[k:k + 1, :] += _colsum(dpre * buf[s0:s0 + tl, :])
        st_ref[CONV_WIDTH:CONV_WIDTH + 1, :] += _colsum(dpre)

    return pl.pallas_call(body, name=name, grid=(l // tl,),
                          in_specs=[cur, past, cur, pl.BlockSpec((CONV_WIDTH, ch), lambda i: (0, 0)), pl.BlockSpec((1, ch), lambda i: (0, 0))],
                          out_specs=[cur, pl.BlockSpec((8, ch), lambda i: (0, 0))],
                          out_shape=[jax.ShapeDtypeStruct((l, ch), F32), jax.ShapeDtypeStruct((8, ch), F32)],
                          scratch_shapes=[pltpu.VMEM((tl + HALO, ch), F32)])(u, u, dxc, w, b)


def _conv_bwd_input(dpre, w, name):
    l, ch = dpre.shape
    tl = _conv_tile(l)
    per = tl // HALO
    nt = l // tl
    cur = pl.BlockSpec((tl, ch), lambda i: (i, 0))
    nxt = pl.BlockSpec((HALO, ch), lambda i: (jnp.minimum((i + 1) * per, l // HALO - 1), 0))

    def body(d_ref, halo_ref, w_ref, du_ref, buf):
        i = pl.program_id(0)
        buf[0:tl, :] = d_ref[...]

        @pl.when(i == nt - 1)
        def _():
            buf[tl:tl + HALO, :] = jnp.zeros((HALO, ch), F32)

        @pl.when(i < nt - 1)
        def _():
            buf[tl:tl + HALO, :] = halo_ref[...]

        acc = w_ref[3:4, :] * buf[0:tl, :]
        for k in range(CONV_WIDTH - 1):
            s = CONV_WIDTH - 1 - k
            acc = acc + w_ref[k:k + 1, :] * buf[s:s + tl, :]
        du_ref[...] = acc

    return pl.pallas_call(body, name=name, grid=(nt,),
                          in_specs=[cur, nxt, pl.BlockSpec((CONV_WIDTH, ch), lambda i: (0, 0))],
                          out_specs=cur, out_shape=jax.ShapeDtypeStruct((l, ch), F32),
                          scratch_shapes=[pltpu.VMEM((tl + HALO, ch), F32)])(dpre, dpre, w)


HEAD_TILE = 128


def _chunk_iota():
    r = lax.broadcasted_iota(jnp.int32, (CHUNK, CHUNK), 0)
    c = lax.broadcasted_iota(jnp.int32, (CHUNK, CHUNK), 1)
    return r, c


def _split3_dot(v, ones_b):
    hi = v.astype(BF16)
    r1 = v - hi.astype(F32)
    mid = r1.astype(BF16)
    lo = (r1 - mid.astype(F32)).astype(BF16)
    return _dot(hi, ones_b) + _dot(mid, ones_b) + _dot(lo, ones_b)


def _split3_dot_nt(v, ones_b):
    hi = v.astype(BF16)
    r1 = v - hi.astype(F32)
    mid = r1.astype(BF16)
    lo = (r1 - mid.astype(F32)).astype(BF16)
    return _dot_nt(hi, ones_b) + _dot_nt(mid, ones_b) + _dot_nt(lo, ones_b)


def _ssd_decays(dt_ref, sx_ref, dtbx_ref, alx_ref, r, c):
    tri = (r >= c).astype(BF16)
    raw_x = _split3_dot(dt_ref[...], sx_ref[...]) + dtbx_ref[...]
    dt_x = _softplus(raw_x)
    a_x = -jnp.exp(alx_ref[...])
    adt_x = a_x * dt_x
    acs_x = _tri_dot(tri, adt_x)
    atot_x = _colsum(adt_x)
    return raw_x, dt_x, a_x, acs_x, jnp.exp(acs_x), jnp.exp(atot_x - acs_x), jnp.exp(atot_x)


def _head_decay(acs_x, e, r, c, halves):
    tile = acs_x[:, (e // 2) * 2 * HEAD_DIM:(e // 2 + 1) * 2 * HEAD_DIM]
    col = jnp.broadcast_to(jnp.max(jnp.where(halves[e % 2], tile, -jnp.inf), axis=1, keepdims=True), (CHUNK, CHUNK))
    return jnp.where(r >= c, jnp.exp(col - col.T), 0.0)


def _half_masks():
    lane = lax.broadcasted_iota(jnp.int32, (CHUNK, 2 * HEAD_DIM), 1)
    return lane < HEAD_DIM, lane >= HEAD_DIM


def _ssd_specs(nc, reverse):
    def at(i):
        return nc - 1 - i if reverse else i
    xc = pl.BlockSpec((CHUNK, D_CONV), lambda i: (at(i), 0))
    wide = pl.BlockSpec((CHUNK, D_SSD), lambda i: (at(i), 0))
    dt_tile = pl.BlockSpec((CHUNK, HEAD_TILE), lambda i: (at(i), O_DT // HEAD_TILE))
    dt_out = pl.BlockSpec((CHUNK, HEAD_TILE), lambda i: (at(i), 0))
    spread = pl.BlockSpec((HEAD_TILE, D_SSD), lambda i: (0, 0))
    vec = pl.BlockSpec((1, D_SSD), lambda i: (0, 0))
    state = pl.BlockSpec((1, SSD_STATE, D_SSD), lambda i: (at(i), 0, 0))
    return xc, wide, dt_tile, dt_out, spread, vec, state


def _ssd_fwd(xc, proj, spread_x, dtb_x, al_x, name):
    l = xc.shape[0]
    nc = l // CHUNK
    xc_s, wide_s, dt_s, _, spread_s, vec_s, state_s = _ssd_specs(nc, False)
    pairs_per_group = GROUP_WIDTH // (2 * HEAD_DIM)

    def body(xc_ref, dt_ref, sx_ref, dtbx_ref, alx_ref, y_ref, sp_ref, state):
        @pl.when(pl.program_id(0) == 0)
        def _():
            state[...] = jnp.zeros_like(state)

        r, c = _chunk_iota()
        halves = _half_masks()
        _, dt_x, _, acs_x, ea_x, ds_x, eatot_x = _ssd_decays(dt_ref, sx_ref, dtbx_ref, alx_ref, r, c)
        xg = xc_ref[:, 0:D_SSD] * dt_x
        sp_ref[0] = state[...]
        for g in range(SSD_GROUPS):
            lanes = slice(g * GROUP_WIDTH, (g + 1) * GROUP_WIDTH)
            bb = xc_ref[:, D_SSD + g * SSD_STATE:D_SSD + (g + 1) * SSD_STATE].astype(MXU_DTYPE)
            cb = xc_ref[:, D_SSD + (SSD_GROUPS + g) * SSD_STATE:D_SSD + (SSD_GROUPS + g + 1) * SSD_STATE].astype(MXU_DTYPE)
            scores = _dot_nt(cb, bb)
            sg = state[:, lanes]
            ys = []
            for j in range(g * pairs_per_group, (g + 1) * pairs_per_group):
                xg_pair = xg[:, j * 2 * HEAD_DIM:(j + 1) * 2 * HEAD_DIM]
                acc = jnp.zeros((CHUNK, 2 * HEAD_DIM), F32)
                for half in range(2):
                    m = (scores * _head_decay(acs_x, 2 * j + half, r, c, halves)).astype(MXU_DTYPE)
                    acc = acc + _dot(m, jnp.where(halves[half], xg_pair, 0.0).astype(MXU_DTYPE))
                ys.append(acc)
            y_ref[:, lanes] = jnp.concatenate(ys, axis=1) + _dot(cb, sg.astype(MXU_DTYPE)) * ea_x[:, lanes]
            state[:, lanes] = sg * eatot_x[:, lanes] + _dot_tn(bb, (xg[:, lanes] * ds_x[:, lanes]).astype(MXU_DTYPE))

    return pl.pallas_call(
        body, name=name, grid=(nc,),
        in_specs=[xc_s, dt_s, spread_s, vec_s, vec_s],
        out_specs=[wide_s, state_s],
        out_shape=[jax.ShapeDtypeStruct((l, D_SSD), F32), jax.ShapeDtypeStruct((nc, SSD_STATE, D_SSD), F32)],
        scratch_shapes=[pltpu.VMEM((SSD_STATE, D_SSD), F32)],
    )(xc, proj, spread_x, dtb_x, al_x)


def _ssd_bwd(xc, proj, spread_x, dtb_x, al_x, dy, states, name):
    l = xc.shape[0]
    nc = l // CHUNK
    xc_s, wide_s, dt_s, dt_out_s, spread_s, vec_s, state_s = _ssd_specs(nc, True)
    pairs_per_group = GROUP_WIDTH // (2 * HEAD_DIM)

    def body(xc_ref, dt_ref, sx_ref, dtbx_ref, alx_ref, dy_ref, sp_ref,
             dxc_ref, ddtr_ref, st_ref, hs_ref, dstate):
        @pl.when(pl.program_id(0) == 0)
        def _():
            dstate[...] = jnp.zeros_like(dstate)
            st_ref[...] = jnp.zeros_like(st_ref)

        r, c = _chunk_iota()
        halves = _half_masks()
        spread = sx_ref[...]

        def head_sums(v):
            return _split3_dot(_split3_dot_nt(v, spread), spread)

        raw_x, dt_x, a_x, acs_x, ea_x, ds_x, eatot_x = _ssd_decays(dt_ref, sx_ref, dtbx_ref, alx_ref, r, c)
        xs = xc_ref[:, 0:D_SSD]
        xg = xs * dt_x
        gy = dy_ref[...]
        s_prev = sp_ref[0]
        gea = gy * ea_x
        xds = xg * ds_x
        ds_old = dstate[...]
        later2 = jnp.concatenate([(r <= c).astype(BF16)] * 2, axis=1)
        dxg_parts, state_term, yoff_parts, dadt_parts = [], [], [], []
        for g in range(SSD_GROUPS):
            lanes = slice(g * GROUP_WIDTH, (g + 1) * GROUP_WIDTH)
            b_lo = D_SSD + g * SSD_STATE
            c_lo = D_SSD + (SSD_GROUPS + g) * SSD_STATE
            bb = xc_ref[:, b_lo:b_lo + SSD_STATE].astype(MXU_DTYPE)
            cb = xc_ref[:, c_lo:c_lo + SSD_STATE].astype(MXU_DTYPE)
            scores = _dot_nt(cb, bb)
            dsg = ds_old[:, lanes].astype(MXU_DTYPE)
            gea_b = gea[:, lanes].astype(MXU_DTYPE)
            xds_b = xds[:, lanes].astype(MXU_DTYPE)
            dxg_state = _dot(bb, dsg) * ds_x[:, lanes]
            dc = _dot_nt(gea_b, s_prev[:, lanes].astype(MXU_DTYPE))
            db = _dot_nt(xds_b, dsg)
            dscores = jnp.zeros((CHUNK, CHUNK), F32)
            diag = []
            for j in range(g * pairs_per_group, (g + 1) * pairs_per_group):
                pair = slice(j * 2 * HEAD_DIM, (j + 1) * 2 * HEAD_DIM)
                xg_pair = xg[:, pair].astype(MXU_DTYPE)
                acc = jnp.zeros((CHUNK, 2 * HEAD_DIM), F32)
                cols = []
                for half in range(2):
                    decay = _head_decay(acs_x, 2 * j + half, r, c, halves)
                    g_e = jnp.where(halves[half], gy[:, pair], 0.0).astype(MXU_DTYPE)
                    acc = acc + _dot_tn((scores * decay).astype(MXU_DTYPE), g_e)
                    dm = _dot_nt(g_e, xg_pair) * decay
                    dscores = dscores + dm
                    wq = dm * scores
                    hi = wq.astype(BF16)
                    lo = (wq - hi.astype(F32)).astype(BF16)
                    later = _dot(later2, jnp.concatenate([hi, lo], axis=0))
                    cols.append(jnp.sum(jnp.where(c < r, later, 0.0), axis=1, keepdims=True))
                diag.append(acc)
                dadt_parts.append(jnp.where(halves[0], cols[0], cols[1]))
            dsc_b = dscores.astype(MXU_DTYPE)
            dc = dc + _dot(dsc_b, bb)
            db = db + _dot_tn(dsc_b, cb)
            dxc_ref[:, b_lo:b_lo + SSD_STATE] = db
            dxc_ref[:, c_lo:c_lo + SSD_STATE] = dc
            dxg_parts.append(jnp.concatenate(diag, axis=1) + dxg_state)
            state_term.append(dxg_state)
            yoff_parts.append(_dot(cb, s_prev[:, lanes].astype(MXU_DTYPE)) * ea_x[:, lanes])
            dstate[:, lanes] = ds_old[:, lanes] * eatot_x[:, lanes] + _dot_tn(cb, gea_b)
        dxg = jnp.concatenate(dxg_parts, axis=1)
        dxc_ref[:, 0:D_SSD] = dxg * dt_x
        through_out = _tri_dot((r <= c).astype(BF16), gy * jnp.concatenate(yoff_parts, axis=1))
        through_in = _tri_dot((c < r).astype(BF16), xg * jnp.concatenate(state_term, axis=1))
        carried = jnp.broadcast_to(_colsum(ds_old * s_prev) * eatot_x, (8, D_SSD))
        dadt = (jnp.concatenate(dadt_parts, axis=1) + head_sums(through_out + through_in)
                + jnp.max(head_sums(carried), axis=0, keepdims=True))
        ddt = a_x * dadt + head_sums(dxg * xs)
        draw = ddt * _sigmoid(raw_x)
        ddtr_ref[...] = _split3_dot_nt(draw, spread) * (1.0 / HEAD_DIM)
        st_ref[0:1, :] += _colsum(dt_x * dadt) * a_x
        st_ref[1:2, :] += _colsum(draw)
        st_ref[2:3, :] += _colsum(gy * xs)

        @pl.when(pl.program_id(0) == nc - 1)
        def _():
            hs_ref[...] = head_sums(st_ref[...])

    stats = pl.BlockSpec((8, D_SSD), lambda i: (0, 0))
    return pl.pallas_call(
        body, name=name, grid=(nc,),
        in_specs=[xc_s, dt_s, spread_s, vec_s, vec_s, wide_s, state_s],
        out_specs=[xc_s, dt_out_s, stats, stats],
        out_shape=[jax.ShapeDtypeStruct((l, D_CONV), F32), jax.ShapeDtypeStruct((l, HEAD_TILE), F32),
                   jax.ShapeDtypeStruct((8, D_SSD), F32), jax.ShapeDtypeStruct((8, D_SSD), F32)],
        scratch_shapes=[pltpu.VMEM((SSD_STATE, D_SSD), F32)],
    )(xc, proj, spread_x, dtb_x, al_x, dy, states)


def _ssd_gate_norm(ycore, xc, z, dskip_x, norm_w, name):
    l = ycore.shape[0]
    tl = _pick(l, 512, 16)
    row = pl.BlockSpec((tl, D_SSD), lambda i: (i, 0))
    vec = pl.BlockSpec((1, D_SSD), lambda i: (0, 0))

    def body(y_ref, xs_ref, z_ref, dk_ref, nw_ref, o_ref):
        zv = z_ref[...]
        yv = (y_ref[...] + dk_ref[...] * xs_ref[...]) * (zv * _sigmoid(zv))
        for g in range(SSD_GROUPS):
            lanes = slice(g * GROUP_WIDTH, (g + 1) * GROUP_WIDTH)
            yg = yv[:, lanes]
            rg = lax.rsqrt(jnp.mean(yg * yg, axis=-1, keepdims=True) + EPS)
            o_ref[:, lanes] = (yg * rg * nw_ref[:, lanes]).astype(o_ref.dtype)

    z_spec = pl.BlockSpec((tl, D_SSD), lambda i: (i, O_Z // D_SSD))
    return pl.pallas_call(body, name=name, grid=(l // tl,), in_specs=[row, row, z_spec, vec, vec], out_specs=row,
                          out_shape=jax.ShapeDtypeStruct((l, D_SSD), MXU_DTYPE))(ycore, xc, z, dskip_x, norm_w)


def _ssd_gate_norm_bwd(dout, ycore, xc, z, dskip_x, norm_w, name):
    l = ycore.shape[0]
    tl = _pick(l, 512, 16)
    nt = l // tl
    row = pl.BlockSpec((tl, D_SSD), lambda i: (i, 0))
    vec = pl.BlockSpec((1, D_SSD), lambda i: (0, 0))

    def body(do_ref, y_ref, xs_ref, z_ref, dk_ref, nw_ref, dyc_ref, dxs_ref, dz_ref, st_ref):
        @pl.when(pl.program_id(0) == 0)
        def _():
            st_ref[...] = jnp.zeros_like(st_ref)

        zv, xs = z_ref[...], xs_ref[...]
        s = _sigmoid(zv)
        gz = zv * s
        yc = y_ref[...] + dk_ref[...] * xs
        yv = yc * gz
        dov = do_ref[...]
        dnw, dyv = [], []
        for g in range(SSD_GROUPS):
            lanes = slice(g * GROUP_WIDTH, (g + 1) * GROUP_WIDTH)
            yg = yv[:, lanes]
            rg = lax.rsqrt(jnp.mean(yg * yg, axis=-1, keepdims=True) + EPS)
            yn = yg * rg
            dnw.append(_colsum(dov[:, lanes] * yn))
            dyn = dov[:, lanes] * nw_ref[:, lanes]
            dyv.append(rg * (dyn - yn * jnp.mean(dyn * yn, axis=-1, keepdims=True)))
        dy = jnp.concatenate(dyv, axis=1)
        dyc = dy * gz
        dyc_ref[...] = dyc
        dxs_ref[...] = dyc * dk_ref[...]
        dz_ref[...] = dy * yc * (s * (1.0 + zv * (1.0 - s)))
        st_ref[0:1, :] += jnp.concatenate(dnw, axis=1)

    return pl.pallas_call(
        body, name=name, grid=(nt,), in_specs=[row, row, row, pl.BlockSpec((tl, D_SSD), lambda i: (i, O_Z // D_SSD)), vec, vec],
        out_specs=[row, row, row, pl.BlockSpec((8, D_SSD), lambda i: (0, 0))],
        out_shape=[jax.ShapeDtypeStruct((l, D_SSD), F32)] * 3 + [jax.ShapeDtypeStruct((8, D_SSD), F32)],
    )(dout, ycore, xc, z, dskip_x, norm_w)


PAIR = 2 * HEAD_DIM
N_PAIRS = N_HEADS // 2
Q_TILE0 = O_Q // PAIR
SB_KEYS = 1024
SB_PAIRS = 4
SB_PAIRS_FWD = 4
SB_SCALE = HEAD_DIM ** -0.5


def _pair_sum(v, lo):
    s_lo = jnp.sum(jnp.where(lo, v, 0.0), axis=-1, keepdims=True)
    s_hi = jnp.sum(jnp.where(lo, 0.0, v), axis=-1, keepdims=True)
    return jnp.where(lo, s_lo, s_hi)


def _qkv_prep(proj, qw2, kw2, name):
    l = proj.shape[0]
    tl = _pick(l, 1024, 16)
    out = pl.BlockSpec((tl, PAIR), lambda i, j: (i, j))
    vec = pl.BlockSpec((1, PAIR), lambda i, j: (0, 0))

    def at(which):
        return pl.BlockSpec((tl, PAIR), lambda i, j: (i, Q_TILE0 + which * N_PAIRS + j))

    def body(q_ref, k_ref, v_ref, qw_ref, kw_ref, qn_ref, kn_ref, vb_ref):
        lo = lax.broadcasted_iota(jnp.int32, (tl, PAIR), 1) < HEAD_DIM
        for t_ref, w_ref, o_ref, scale in ((q_ref, qw_ref, qn_ref, SB_SCALE), (k_ref, kw_ref, kn_ref, 1.0)):
            tv = t_ref[...]
            r = lax.rsqrt(_pair_sum(tv * tv, lo) * (1.0 / HEAD_DIM) + EPS)
            o_ref[...] = ((tv * r) * w_ref[...] * scale).astype(o_ref.dtype)
        vb_ref[...] = v_ref[...].astype(vb_ref.dtype)

    return pl.pallas_call(body, name=name, grid=(l // tl, N_PAIRS), in_specs=[at(0), at(1), at(2), vec, vec],
                          out_specs=[out, out, out], out_shape=[jax.ShapeDtypeStruct((l, D_SB), MXU_DTYPE)] * 3,
                          )(proj, proj, proj, qw2, kw2)


def _qk_norm_bwd(proj, dqn, dkn, qw2, kw2, name):
    l = proj.shape[0]
    tl = _pick(l, 1024, 16)
    out = pl.BlockSpec((tl, PAIR), lambda i, j: (i, j))
    vec = pl.BlockSpec((1, PAIR), lambda i, j: (0, 0))
    stats = pl.BlockSpec((8, PAIR), lambda i, j: (0, 0))

    def at(which):
        return pl.BlockSpec((tl, PAIR), lambda i, j: (i, Q_TILE0 + which * N_PAIRS + j))

    def body(q_ref, k_ref, dqn_ref, dkn_ref, qw_ref, kw_ref, dq_ref, dk_ref, stq_ref, stk_ref):
        @pl.when((pl.program_id(0) == 0) & (pl.program_id(1) == 0))
        def _():
            stq_ref[...] = jnp.zeros_like(stq_ref)
            stk_ref[...] = jnp.zeros_like(stk_ref)

        lo = lax.broadcasted_iota(jnp.int32, (tl, PAIR), 1) < HEAD_DIM
        for t_ref, dn_ref, w_ref, d_ref, st_ref, scale in ((q_ref, dqn_ref, qw_ref, dq_ref, stq_ref, SB_SCALE),
                                                           (k_ref, dkn_ref, kw_ref, dk_ref, stk_ref, 1.0)):
            tv = t_ref[...]
            r = lax.rsqrt(_pair_sum(tv * tv, lo) * (1.0 / HEAD_DIM) + EPS)
            tn = tv * r
            dnv = dn_ref[...] * scale
            dtn = dnv * w_ref[...]
            d_ref[...] = r * (dtn - tn * (_pair_sum(dtn * tn, lo) * (1.0 / HEAD_DIM)))
            st_ref[0:1, :] += _colsum(dnv * tn)

    return pl.pallas_call(body, name=name, grid=(l // tl, N_PAIRS), in_specs=[at(0), at(1), out, out, vec, vec],
                          out_specs=[out, out, stats, stats],
                          out_shape=[jax.ShapeDtypeStruct((l, D_SB), F32)] * 2 + [jax.ShapeDtypeStruct((8, PAIR), F32)] * 2,
                          )(proj, proj, dqn, dkn, qw2, kw2)


def _sb_masks():
    r = lax.broadcasted_iota(jnp.int32, (CHUNK, CHUNK), 0)
    c = lax.broadcasted_iota(jnp.int32, (CHUNK, CHUNK), 1)
    return r, c


def _stack2(mask):
    t = mask.astype(BF16)
    return jnp.concatenate([t, t], axis=0)


def _sb_fwd(q, k, v, shards, name):
    l = q.shape[0]
    nq = l // CHUNK
    kt = _pick(l, SB_KEYS, CHUNK)
    sub = kt // CHUNK
    pairs = SB_PAIRS_FWD
    heads = 2 * pairs
    steps = N_PAIRS // pairs
    na = len(shards)
    qblk = pl.BlockSpec((CHUNK, pairs * PAIR), lambda i, j: (j, i))
    full = pl.BlockSpec((l, pairs * PAIR), lambda i, j: (0, i))
    hbm = pl.BlockSpec(memory_space=pl.ANY)

    def body(q_ref, k_ref, v_ref, *rest):
        o_ref, tot_ref = rest[na:na + 2]
        exchange = _DirectExchange("gather", rest[:na], rest[na + 2:2 * na + 2], *rest[2 * na + 2:])
        qb = pl.program_id(1)

        @pl.when((pl.program_id(0) == 0) & (qb == 0))
        def _():
            exchange.start()

        r, c = _sb_masks()
        after2 = _stack2(r > c)
        halves = _half_masks()
        zero = jnp.zeros((CHUNK, PAIR), q_ref.dtype)
        lanes = [slice((a // 2) * PAIR, (a // 2 + 1) * PAIR) for a in range(heads)]
        qm = [jnp.where(halves[a % 2], q_ref[:, lanes[a]], zero) for a in range(heads)]
        last = (qb * CHUNK) // kt

        def tile(t, width, carries, accs, masked):
            rows = pl.ds(pl.multiple_of(t * width, width), width)
            if masked:
                key_minus_query = (lax.broadcasted_iota(jnp.int32, (CHUNK, width), 1)
                                   - lax.broadcasted_iota(jnp.int32, (CHUNK, width), 0))
                keep = key_minus_query < qb * CHUNK - t * width

            def logits(a):
                lg = _dot_nt(qm[a], k_ref[rows, lanes[a]])
                sp = _softplus_logits(lg)
                return lg - sp, (jnp.where(keep, -sp, 0.0) if masked else -sp)

            def sums(a, lr):
                offset, parts = carries[a], [None] * (width // CHUNK)
                for j in reversed(range(width // CHUNK)):
                    piece = lr[:, j * CHUNK:(j + 1) * CHUNK]
                    parts[j] = _split_dot(piece, after2) + offset
                    offset = offset + jnp.sum(piece, axis=1, keepdims=True)
                return jnp.concatenate(parts, axis=1), offset

            def output(a, ls, cs):
                w = jnp.exp(ls + cs)
                if masked:
                    w = jnp.where(keep, w, 0.0)
                return accs[a] + _dot(w.astype(MXU_DTYPE), v_ref[rows, lanes[a]])

            new_carries, new_accs = [None] * heads, [None] * heads
            ls, lr = logits(0)
            for a in range(heads):
                cs, new_carries[a] = sums(a, lr)
                if a + 1 < heads:
                    ls_next, lr = logits(a + 1)
                new_accs[a] = output(a, ls, cs)
                ls = ls_next
            return tuple(new_carries), tuple(new_accs)

        carries = tuple(jnp.zeros((CHUNK, 1), F32) for _ in range(heads))
        accs = tuple(jnp.zeros((CHUNK, PAIR), F32) for _ in range(heads))
        if sub % 2 == 0:
            half = kt // 2
            carries, accs = lax.cond((qb * CHUNK) % kt >= half,
                                     lambda cr, ac: tile(2 * last + 1, half, cr, ac, True), lambda cr, ac: (cr, ac), carries, accs)
            carries, accs = tile(2 * last, half, carries, accs, True)
        else:
            carries, accs = tile(last, kt, carries, accs, True)
        carries, accs = lax.fori_loop(1, last + 1, lambda i, st: tile(last - i, kt, st[0], st[1], False), (carries, accs))
        for p in range(pairs):
            o_ref[:, lanes[2 * p]] = jnp.where(halves[0], accs[2 * p], accs[2 * p + 1])
            tot_ref[:, lanes[2 * p]] = jnp.where(halves[0], carries[2 * p], carries[2 * p + 1])

        @pl.when((pl.program_id(0) == steps - 1) & (qb == nq - 1))
        def _():
            exchange.wait()

    return pl.pallas_call(
        body, name=name, grid=(steps, nq), in_specs=[qblk, full, full] + [hbm] * na,
        out_specs=[qblk, qblk] + [hbm] * na,
        out_shape=[jax.ShapeDtypeStruct((l, D_SB), F32), jax.ShapeDtypeStruct((l, D_SB), F32)]
        + [jax.ShapeDtypeStruct((N_DEV,) + s.shape, s.dtype) for s in shards],
        scratch_shapes=_DirectExchange.scratch(na),
    )(q, k, v, *shards)


def _sb_bwd(q, k, v, dycat, tot, blocks, name):
    l = q.shape[0]
    nq = l // CHUNK
    kt = _pick(l, SB_KEYS, CHUNK)
    sub = kt // CHUNK
    heads = 2 * SB_PAIRS
    steps = N_PAIRS // SB_PAIRS
    na = len(blocks)
    width_all = SB_PAIRS * PAIR
    qblk = pl.BlockSpec((CHUNK, width_all), lambda i, j: (j, i))
    doblk = pl.BlockSpec((CHUNK, width_all), lambda i, j: (j, D_SSD // width_all + i))
    full = pl.BlockSpec((l, width_all), lambda i, j: (0, i))
    resident = pl.BlockSpec((l, width_all), lambda i, j: (0, i), pipeline_mode=pl.Buffered(1))
    hbm = pl.BlockSpec(memory_space=pl.ANY)

    def body(q_ref, k_ref, v_ref, do_ref, tot_ref, *rest):
        dq_ref, dk_ref, dv_ref = rest[na:na + 3]
        exchange = _DirectExchange("scatter", rest[:na], rest[na + 3:2 * na + 3], *rest[2 * na + 3:])
        qb = pl.program_id(1)

        @pl.when((pl.program_id(0) == 0) & (qb == 0))
        def _():
            exchange.start()

        @pl.when(qb == 0)
        def _():
            dk_ref[...] = jnp.zeros_like(dk_ref)
            dv_ref[...] = jnp.zeros_like(dv_ref)

        r, c = _sb_masks()
        upto2 = _stack2(r <= c)
        before2 = _stack2(r < c)
        halves = _half_masks()
        lanes = [slice((a // 2) * PAIR, (a // 2 + 1) * PAIR) for a in range(heads)]
        qm = [jnp.where(halves[a % 2], q_ref[:, lanes[a]], jnp.zeros((CHUNK, PAIR), q_ref.dtype)) for a in range(heads)]
        dom = [jnp.where(halves[a % 2], do_ref[:, lanes[a]], 0.0).astype(MXU_DTYPE) for a in range(heads)]
        total = [jnp.max(jnp.where(halves[a % 2], tot_ref[:, lanes[a]], -jnp.inf), axis=1, keepdims=True) for a in range(heads)]
        last = (qb * CHUNK) // kt

        def prefix(values, tri2, offset):
            parts = []
            for j in range(values.shape[1] // CHUNK):
                piece = values[:, j * CHUNK:(j + 1) * CHUNK]
                parts.append(_split_dot(piece, tri2) + offset)
                offset = offset + jnp.sum(piece, axis=1, keepdims=True)
            return jnp.concatenate(parts, axis=1), offset

        def tile(t, width, carry_p, carry_d, dq, masked):
            rows = pl.ds(pl.multiple_of(t * width, width), width)
            if masked:
                key_minus_query = (lax.broadcasted_iota(jnp.int32, (CHUNK, width), 1)
                                   - lax.broadcasted_iota(jnp.int32, (CHUNK, width), 0))
                keep = key_minus_query < qb * CHUNK - t * width

            def log_terms(a):
                lg = _dot_nt(qm[a], k_ref[rows, lanes[a]])
                sp = _softplus_logits(lg)
                return lg - sp, (jnp.where(keep, -sp, 0.0) if masked else -sp)

            def weights(a, ls, lr):
                p_incl, p_next = prefix(lr, upto2, carry_p[a])
                w = jnp.exp(ls + (total[a] - p_incl))
                if masked:
                    w = jnp.where(keep, w, 0.0)
                return w, p_next

            def weight_grads(a, w):
                return _dot_nt(dom[a], v_ref[rows, lanes[a]]) * w

            def gradients(a, ls, w, da):
                d_excl, d_next = prefix(da, before2, carry_d[a])
                sig = jnp.exp(ls)
                dl = da * (1.0 - sig) - d_excl * sig
                if masked:
                    dl = jnp.where(keep, dl, 0.0)
                dl_b = dl.astype(MXU_DTYPE)
                return (d_next, dq[a] + _dot(dl_b, k_ref[rows, lanes[a]]),
                        _dot_tn(w.astype(MXU_DTYPE), dom[a]), _dot_tn(dl_b, qm[a]))

            new_p, new_d, new_dq = [None] * heads, [None] * heads, [None] * heads
            dv_upd, dk_upd = [None] * heads, [None] * heads
            ls, lr = log_terms(0)
            w, new_p[0] = weights(0, ls, lr)
            for a in range(heads):
                if a + 1 < heads:
                    ls_next, lr_next = log_terms(a + 1)
                da = weight_grads(a, w)
                if a + 1 < heads:
                    w_next, new_p[a + 1] = weights(a + 1, ls_next, lr_next)
                new_d[a], new_dq[a], dv_upd[a], dk_upd[a] = gradients(a, ls, w, da)
                if a % 2 == 1:
                    dv_ref[rows, lanes[a]] += dv_upd[a - 1] + dv_upd[a]
                    dk_ref[rows, lanes[a]] += dk_upd[a - 1] + dk_upd[a]
                if a + 1 < heads:
                    ls, w = ls_next, w_next
            return tuple(new_p), tuple(new_d), tuple(new_dq)

        zeros = tuple(jnp.zeros((CHUNK, 1), F32) for _ in range(heads))
        dq0 = tuple(jnp.zeros((CHUNK, PAIR), F32) for _ in range(heads))
        state = lax.fori_loop(0, last, lambda t, st: tile(t, kt, st[0], st[1], st[2], False), (zeros, zeros, dq0))
        if sub % 2 == 0:
            half = kt // 2
            state = tile(2 * last, half, *state, True)
            state = lax.cond((qb * CHUNK) % kt >= half,
                             lambda cp, cd, dq: tile(2 * last + 1, half, cp, cd, dq, True), lambda cp, cd, dq: (cp, cd, dq), *state)
        else:
            state = tile(last, kt, *state, True)
        for p in range(SB_PAIRS):
            dq_ref[:, lanes[2 * p]] = jnp.where(halves[0], state[2][2 * p], state[2][2 * p + 1])

        @pl.when((pl.program_id(0) == steps - 1) & (qb == nq - 1))
        def _():
            exchange.wait()

    return pl.pallas_call(
        body, name=name, grid=(steps, nq),
        in_specs=[qblk, full, full, doblk, qblk] + [hbm] * na,
        out_specs=[qblk, resident, resident] + [hbm] * na,
        out_shape=[jax.ShapeDtypeStruct((l, D_SB), F32)] * 3 + [jax.ShapeDtypeStruct(b.shape, b.dtype) for b in blocks],
        scratch_shapes=_DirectExchange.scratch(na),
    )(q, k, v, dycat, tot, *blocks)


def _ada_bwd(c_all, dmod_shard, name):
    def body(c_ref, d_ref, o_ref):
        cv = c_ref[...]
        o_ref[...] = _dot_tn(cv * _sigmoid(cv), d_ref[...], HIGHEST)

    return pl.pallas_call(body, name=name, out_shape=jax.ShapeDtypeStruct((c_all.shape[1], dmod_shard.shape[1]), F32))(c_all, dmod_shard)


def _sum_small(parts, name):
    def body(p_ref, o_ref):
        acc = p_ref[0]
        for d in range(1, N_DEV):
            acc = acc + p_ref[d]
        o_ref[...] = acc

    return pl.pallas_call(body, name=name, out_shape=jax.ShapeDtypeStruct(parts.shape[1:], F32))(parts)


def _adamw(w, g, m, v, name):
    rows, cols = w.shape
    tr = _pick(rows, 256, 8)
    spec = pl.BlockSpec((tr, cols), lambda i: (i, 0))
    bc1 = 1.0 - ADAM_B1 ** ADAM_STEP
    bc2 = 1.0 - ADAM_B2 ** ADAM_STEP

    def body(w_ref, g_ref, m_ref, v_ref, d_ref, nm_ref, nv_ref):
        gv = g_ref[...]
        nm = ADAM_B1 * m_ref[...] + (1.0 - ADAM_B1) * gv
        nv = ADAM_B2 * v_ref[...] + (1.0 - ADAM_B2) * (gv * gv)
        nm_ref[...] = nm
        nv_ref[...] = nv
        d_ref[...] = -ADAM_LR * ((nm / bc1) / (jnp.sqrt(nv / bc2) + ADAM_EPS) + ADAM_WD * w_ref[...])

    return pl.pallas_call(body, name=name, grid=(rows // tr,), in_specs=[spec] * 4, out_specs=[spec] * 3,
                          out_shape=[jax.ShapeDtypeStruct((rows, cols), F32)] * 3)(w, g, m, v)


def _pad_w_in_t(w_in_t):
    lo = D_SSD + D_CONV
    return jnp.concatenate([w_in_t[D_SSD:lo + N_HEADS], jnp.zeros((DT_PAD - N_HEADS, w_in_t.shape[1]), w_in_t.dtype),
                            w_in_t[:D_SSD], w_in_t[lo + N_HEADS:]], axis=0)


def _local_step(x, target, mod, norm1_w, w_in_tp, conv_w, conv_b, dt_bias, a_log, d_skip, ssd_norm_w, q_norm_w, k_norm_w,
                norm2_w, later_shards):
    sh1, sc1, g1, sh2, sc2, g2 = [mod[:, i * D_MODEL:(i + 1) * D_MODEL] for i in range(N_MOD)]
    qw2, kw2 = jnp.tile(q_norm_w, (1, 2)), jnp.tile(k_norm_w, (1, 2))
    dskip_x = jnp.repeat(d_skip, HEAD_DIM, axis=1)
    dtb_x, al_x = jnp.repeat(dt_bias, HEAD_DIM, axis=1), jnp.repeat(a_log, HEAD_DIM, axis=1)
    head_ids = jnp.arange(HEAD_TILE, dtype=jnp.int32)[:, None]
    spread_x = (head_ids == jnp.arange(D_SSD, dtype=jnp.int32)[None, :] // HEAD_DIM).astype(BF16)

    h1 = _rms_mod(x, norm1_w, sc1, sh1, "rms_mod1")
    proj = _matmul_nt(h1, w_in_tp, "in_proj")

    xc = _conv_silu(proj, conv_w, conv_b, "conv_silu")
    ycore, states = _ssd_fwd(xc, proj, spread_x, dtb_x, al_x, "ssd_fwd")
    y_ssd = _ssd_gate_norm(ycore, xc, proj, dskip_x, ssd_norm_w, "ssd_gate_norm")

    qn, kn, vb = _qkv_prep(proj, qw2, kw2, "qkv_prep")
    o_sb, tot, *gathered = _sb_fwd(qn, kn, vb, later_shards, "sb_fwd")
    w_out, w_gate_t, w_up_t, w_down = [g.reshape(N_DEV * g.shape[1], D_MODEL) for g in gathered]
    ycat = jnp.concatenate([y_ssd, o_sb.astype(MXU_DTYPE)], axis=1)

    mix = _matmul(ycat, w_out, "out_proj")
    x1, h2 = _residual_rms_mod(x, mix, g1, norm2_w, sc2, sh2, "residual_rms_mod2")
    gate, up, act = _ffn_in(h2, w_gate_t, w_up_t, "ffn_in")
    ffn = _matmul(act, w_down, "ffn_down")
    dy, dffn, st_g2, loss_blk = _loss_head(x1, ffn, g2, target, "loss_head")

    g_down = _matmul_tn(act, dffn, "ffn_down_dw", MXU_DTYPE)
    dgate, dup = _ffn_down_bwd(dffn, w_down, gate, up, "ffn_down_dx")
    g_gate_t = _matmul_tn(dgate, h2, "ffn_gate_dw", MXU_DTYPE)
    g_up_t = _matmul_tn(dup, h2, "ffn_up_dw", MXU_DTYPE)
    dh2 = _matmul_sum2(dgate, w_gate_t, dup, w_up_t, "ffn_dh")
    dx1, dmix, st_n2 = _norm_bwd(x1, dh2, dy, norm2_w, sc2, "norm2_bwd", gated=(mix, g1))

    g_out = _matmul_tn(ycat, dmix, "out_proj_dw", MXU_DTYPE)
    dycat = _matmul_nt(dmix, w_out, "out_proj_dx")

    partials = [g.reshape(N_DEV, g.shape[0] // N_DEV, D_MODEL).astype(BF16) for g in (g_out, g_gate_t, g_up_t, g_down)]
    dqn, dkn, dv, *slots = _sb_bwd(qn, kn, vb, dycat, tot, partials, "sb_bwd")
    dq, dk, st_q, st_k = _qk_norm_bwd(proj, dqn, dkn, qw2, kw2, "qk_norm_bwd")

    dycore, dxs_skip, dz, st_gn = _ssd_gate_norm_bwd(dycat, ycore, xc, proj, dskip_x, ssd_norm_w, "ssd_gate_norm_bwd")
    dxc, ddt_tile, st_ssd, st_heads = _ssd_bwd(xc, proj, spread_x, dtb_x, al_x, dycore, states, "ssd_bwd")
    dxc = jnp.concatenate([dxc[:, :D_SSD] + dxs_skip, dxc[:, D_SSD:]], axis=1)
    dpre, st_conv = _conv_silu_bwd_pre(proj, dxc, conv_w, conv_b, "conv_silu_bwd")
    dxbc = _conv_bwd_input(dpre, conv_w, "conv_bwd_input")

    pieces = {"z": (dz, O_Z), "xbc": (dxbc, O_XBC), "dt": (ddt_tile, O_DT), "q": (dq, O_Q), "k": (dk, O_Q + D_SB), "v": (dv, O_Q + 2 * D_SB)}
    g_rows = {n: _matmul_tn(p, h1, "in_proj_dw_" + n, MXU_DTYPE) for n, (p, _) in pieces.items()}
    g_rows["dt"] = g_rows["dt"][:N_HEADS]
    g_in_t = jnp.concatenate(list(g_rows.values()), axis=0)
    g_in_blocks = g_in_t.reshape(N_DEV, g_in_t.shape[0] // N_DEV, D_MODEL).astype(BF16)
    dh1, g_in_slots = _matmul_pieces_beside_scatter([p for p, _ in pieces.values()], [o for _, o in pieces.values()], w_in_tp,
                                                    [g_in_blocks], "in_proj_dx")
    slots = [g_in_slots] + list(slots)
    grad_x, st_n1 = _norm_bwd(x, dh1, dx1, norm1_w, sc1, "norm1_bwd")

    pad = jnp.zeros((1, SM_SSD_NORM - SM_D_SKIP - N_HEADS), F32)
    small = jnp.concatenate(
        [st_n1[1:2], st_n1[0:1], st_n2[3:4], st_n2[1:2], st_n2[0:1], st_g2[0:1],
         st_n1[2:3], st_conv[4:5], st_conv[0:4].reshape(1, CONV_WIDTH * D_CONV),
         st_ssd[1:2, ::HEAD_DIM], st_ssd[0:1, ::HEAD_DIM], st_heads[2:3, ::HEAD_DIM], pad,
         st_gn[0:1], st_q[0:1, :HEAD_DIM] + st_q[0:1, HEAD_DIM:], st_k[0:1, :HEAD_DIM] + st_k[0:1, HEAD_DIM:], st_n2[2:3]], axis=1)
    return loss_blk, grad_x, slots, small


def kernel(x, c, w_ada, b_ada, norm1_w, w_in, conv_w, conv_b, dt_bias, a_log, d_skip, ssd_norm_w, q_norm_w, k_norm_w, w_out, norm2_w, w_gate, w_up, w_down, loss_target, m_w_ada, m_b_ada, m_norm1_w, m_w_in, m_conv_w, m_conv_b, m_dt_bias, m_a_log, m_d_skip, m_ssd_norm_w, m_q_norm_w, m_k_norm_w, m_w_out, m_norm2_w, m_w_gate, m_w_up, m_w_down, v_w_ada, v_b_ada, v_norm1_w, v_w_in, v_conv_w, v_conv_b, v_dt_bias, v_a_log, v_d_skip, v_ssd_norm_w, v_q_norm_w, v_k_norm_w, v_w_out, v_norm2_w, v_w_gate, v_w_up, v_w_down):
    me = 4 * lax.axis_index("x") + 2 * lax.axis_index("y") + lax.axis_index("c")
    conv_cols = D_CONV // N_DEV
    ada_cols = N_MOD * D_MODEL // N_DEV

    cond_conv = jnp.concatenate([c, conv_w[0].reshape(1, CONV_WIDTH * conv_cols)], axis=1)
    first, mods, w_in_t = _prologue(cond_conv, w_ada[0], w_in[0].T.astype(BF16), "prologue")
    w_in_tp = _pad_w_in_t(w_in_t.reshape(D_IN_PROJ, D_MODEL))
    later_shards = [s.astype(BF16) for s in (w_out[0], w_gate[0].T, w_up[0].T, w_down[0])]
    c_all = first[:, 0, :D_MODEL]
    conv_w_f = first[:, 0, D_MODEL:].reshape(N_DEV, CONV_WIDTH, conv_cols).transpose(1, 0, 2).reshape(CONV_WIDTH, D_CONV)
    mod = lax.dynamic_index_in_dim(mods, me, axis=1, keepdims=False)
    mod = mod.reshape(1, N_MOD * D_MODEL) + b_ada

    loss_blk, grad_x, slots, small = _local_step(
        x[0], loss_target[0], mod, norm1_w, w_in_tp, conv_w_f, conv_b, dt_bias, a_log, d_skip, ssd_norm_w, q_norm_w, k_norm_w,
        norm2_w, later_shards)
    loss = lax.psum(loss_blk[0, 0], ("x", "y", "c"))

    g_in_t, g_out, g_gate_t, g_up_t, g_down = [_sum_slots(s, "sum_grads_" + n) for s, n in zip(slots, ("in", "out", "gate", "up", "down"))]
    g_in, g_gate, g_up = g_in_t.T, g_gate_t.T, g_up_t.T

    parts = _all_gather_small(small, "gather_small")
    gsum = _sum_small(parts, "sum_small")
    dmod_shard = lax.dynamic_slice_in_dim(parts[:, 0, :N_MOD * D_MODEL], me * ada_cols, ada_cols, axis=1)
    g_ada = _ada_bwd(c_all, dmod_shard, "ada_bwd")
    g_conv_w = lax.dynamic_slice_in_dim(gsum[:, SM_CONV_W:SM_DT_BIAS].reshape(CONV_WIDTH, D_CONV), me * conv_cols, conv_cols, axis=1)

    def pack_small(b_ada_, norm1_, conv_b_, dt_bias_, a_log_, d_skip_, ssd_norm_, q_norm_, k_norm_, norm2_):
        return jnp.concatenate(
            [b_ada_, norm1_, conv_b_, jnp.zeros((1, CONV_WIDTH * D_CONV), F32), dt_bias_, a_log_, d_skip_,
             jnp.zeros((1, SM_SSD_NORM - SM_D_SKIP - N_HEADS), F32), ssd_norm_, q_norm_, k_norm_, norm2_], axis=1)

    def unpack_small(p):
        return {
            "b_ada": p[:, SM_B_ADA:SM_NORM1], "norm1_w": p[:, SM_NORM1:SM_CONV_B], "conv_b": p[:, SM_CONV_B:SM_CONV_W],
            "dt_bias": p[:, SM_DT_BIAS:SM_A_LOG], "a_log": p[:, SM_A_LOG:SM_D_SKIP], "d_skip": p[:, SM_D_SKIP:SM_D_SKIP + N_HEADS],
            "ssd_norm_w": p[:, SM_SSD_NORM:SM_Q_NORM], "q_norm_w": p[:, SM_Q_NORM:SM_K_NORM], "k_norm_w": p[:, SM_K_NORM:SM_NORM2],
            "norm2_w": p[:, SM_NORM2:SM_TOTAL]}

    w_small = pack_small(b_ada, norm1_w, conv_b, dt_bias, a_log, d_skip, ssd_norm_w, q_norm_w, k_norm_w, norm2_w)
    m_small = pack_small(m_b_ada, m_norm1_w, m_conv_b, m_dt_bias, m_a_log, m_d_skip, m_ssd_norm_w, m_q_norm_w, m_k_norm_w, m_norm2_w)
    v_small = pack_small(v_b_ada, v_norm1_w, v_conv_b, v_dt_bias, v_a_log, v_d_skip, v_ssd_norm_w, v_q_norm_w, v_k_norm_w, v_norm2_w)
    small_out = [unpack_small(t) for t in (gsum,) + tuple(_adamw(w_small, gsum, m_small, v_small, "adamw_small"))]

    sharded = {
        "w_ada": (w_ada[0], g_ada, m_w_ada[0], v_w_ada[0]),
        "w_in": (w_in[0], g_in, m_w_in[0], v_w_in[0]),
        "conv_w": (conv_w[0], g_conv_w, m_conv_w[0], v_conv_w[0]),
        "w_out": (w_out[0], g_out, m_w_out[0], v_w_out[0]),
        "w_gate": (w_gate[0], g_gate, m_w_gate[0], v_w_gate[0]),
        "w_up": (w_up[0], g_up, m_w_up[0], v_w_up[0]),
        "w_down": (w_down[0], g_down, m_w_down[0], v_w_down[0]),
    }
    sharded_out = {n: (t[1],) + tuple(_adamw(*t, "adamw_" + n)) for n, t in sharded.items()}

    names = ["w_ada", "b_ada", "norm1_w", "w_in", "conv_w", "conv_b", "dt_bias", "a_log", "d_skip", "ssd_norm_w", "q_norm_w",
             "k_norm_w", "w_out", "norm2_w", "w_gate", "w_up", "w_down"]
    outs = [loss, grad_x[None]]
    for kind in range(4):
        for n in names:
            outs.append(sharded_out[n][kind][None] if n in sharded_out else small_out[kind][n])
    return tuple(outs)
```

```python
import functools

import jax
import jax.numpy as jnp
from jax import lax
from jax.experimental import pallas as pl
from jax.experimental.pallas import tpu as pltpu

F32 = jnp.float32
BF16 = jnp.bfloat16
MXU_DTYPE = jnp.bfloat16
HIGHEST = lax.Precision.HIGHEST
MESH_IDS = pl.DeviceIdType.MESH

N_DEV = 8
D_MODEL = 1024
HEAD_DIM = 64
N_HEADS = 16
D_SSD = 1024
D_SB = 1024
SSD_GROUPS = 2
SSD_STATE = 128
GROUP_WIDTH = D_SSD // SSD_GROUPS
D_CONV = D_SSD + 2 * SSD_GROUPS * SSD_STATE
CONV_WIDTH = 4
CHUNK = 128
D_FF = 2816
N_MOD = 6
EPS = 1e-6
D_IN_PROJ = 5648
O_XBC = 0
O_DT = D_CONV
DT_PAD = 512
O_Z = O_DT + DT_PAD
O_Q = O_Z + D_SSD
D_IN_PAD = O_Q + 3 * D_SB
HALO = 8

ADAM_LR = 0.001
ADAM_B1 = 0.9
ADAM_B2 = 0.999
ADAM_EPS = 1e-08
ADAM_WD = 0.01
ADAM_STEP = 10

SM_B_ADA = 0
SM_NORM1 = 6144
SM_CONV_B = 7168
SM_CONV_W = 8704
SM_DT_BIAS = 14848
SM_A_LOG = 14864
SM_D_SKIP = 14880
SM_SSD_NORM = 14976
SM_Q_NORM = 16000
SM_K_NORM = 16064
SM_NORM2 = 16128
SM_TOTAL = 17152


def _pick(n, cap, mult):
    if n <= cap:
        return n
    best = None
    for t in range(mult, cap + 1, mult):
        if n % t == 0:
            best = t
    assert best is not None, (n, cap, mult)
    return best


def _dot(a, b, precision=None):
    return jnp.dot(a, b, preferred_element_type=F32, precision=precision)


def _dot_nt(a, b, precision=None):
    return lax.dot_general(a, b, (((1,), (1,)), ((), ())), preferred_element_type=F32, precision=precision)


def _dot_tn(a, b, precision=None):
    return lax.dot_general(a, b, (((0,), (0,)), ((), ())), preferred_element_type=F32, precision=precision)


def _softplus(v):
    return jnp.maximum(v, 0.0) + jnp.log1p(jnp.exp(-jnp.abs(v)))


def _softplus_logits(v):
    return jnp.maximum(v, 0.0) + jnp.log(1.0 + jnp.exp(-jnp.abs(v)))


def _tri_dot(tri_b, v):
    hi = v.astype(BF16)
    r1 = v - hi.astype(F32)
    mid = r1.astype(BF16)
    lo = (r1 - mid.astype(F32)).astype(BF16)
    return _dot(tri_b, hi) + _dot(tri_b, mid) + _dot(tri_b, lo)


def _sigmoid(v):
    return jax.nn.sigmoid(v)


def _colsum(v):
    return jnp.sum(v, axis=0, keepdims=True)


def _split_dot(v, tri2):
    hi = v.astype(BF16)
    lo = (v - hi.astype(F32)).astype(BF16)
    return _dot(jnp.concatenate([hi, lo], axis=1), tri2)


def _position():
    x, y, c = lax.axis_index("x"), lax.axis_index("y"), lax.axis_index("c")
    return x, y, c


def _peer(x, y, c, k):
    px = 1 - x if (k >> 2) & 1 else x
    py = 1 - y if (k >> 1) & 1 else y
    pc = 1 - c if k & 1 else c
    return px, py, pc


def _all_gather_small(v, name):
    n = v.shape[1]

    def body(v_ref, out_ref, send_sems, recv_sems, local_sem):
        x, y, c = _position()
        me = 4 * x + 2 * y + c
        mine = pltpu.make_async_copy(v_ref, out_ref.at[me], local_sem)
        mine.start()
        sends = []
        for k in range(1, N_DEV):
            cp = pltpu.make_async_remote_copy(
                src_ref=v_ref, dst_ref=out_ref.at[me], send_sem=send_sems.at[k - 1], recv_sem=recv_sems.at[k - 1],
                device_id=_peer(x, y, c, k), device_id_type=MESH_IDS)
            cp.start()
            sends.append(cp)
        for k in range(1, N_DEV):
            px, py, pc = _peer(x, y, c, k)
            pltpu.make_async_remote_copy(
                src_ref=v_ref, dst_ref=out_ref.at[4 * px + 2 * py + pc], send_sem=send_sems.at[k - 1],
                recv_sem=recv_sems.at[k - 1], device_id=(px, py, pc), device_id_type=MESH_IDS).wait_recv()
        for cp in sends:
            cp.wait_send()
        mine.wait()

    return pl.pallas_call(
        body, name=name,
        out_shape=jax.ShapeDtypeStruct((N_DEV, 1, n), v.dtype),
        in_specs=[pl.BlockSpec(memory_space=pltpu.VMEM)],
        out_specs=pl.BlockSpec(memory_space=pltpu.VMEM),
        scratch_shapes=[pltpu.SemaphoreType.DMA((N_DEV - 1,)), pltpu.SemaphoreType.DMA((N_DEV - 1,)),
                        pltpu.SemaphoreType.DMA],
    )(v)


def _prologue(cond_conv, w_ada_shard, w_in_shard, name):
    def body(cc_ref, wada_ref, win_ref, first_ref, mods_ref, wout_ref, mod_cols, conds_all, big_send, big_recv, big_local,
             a_send, a_recv, a_local, b_send, b_recv, b_local):
        x, y, c = _position()
        me, sibling = (x, y, c), (x, y, 1 - c)
        chips = [(1 - x, y), (x, 1 - y), (1 - x, 1 - y)]

        def big(k, blk, to, own=False):
            slot = wout_ref.at[4 * blk[0] + 2 * blk[1] + blk[2]]
            return pltpu.make_async_remote_copy(src_ref=win_ref if own else slot, dst_ref=slot, send_sem=big_send.at[k],
                                                recv_sem=big_recv.at[k], device_id=to, device_id_type=MESH_IDS)

        mine = pltpu.make_async_copy(win_ref, wout_ref.at[4 * x + 2 * y + c], big_local.at[0])
        mine.start()
        first = [big(0, me, sibling, own=True)] + [big(1 + j, me, (*chip, c), own=True) for j, chip in enumerate(chips)]
        for cp in first:
            cp.start()

        conds = _DirectExchange("gather", [cc_ref], [first_ref], a_send, a_recv, a_local)
        conds.start()
        conds.wait()
        for d in range(N_DEV):
            conds_all[d:d + 1, :] = first_ref[d, :, 0:D_MODEL]
        cv = conds_all[...]
        mod_cols[...] = _dot(cv * _sigmoid(cv), wada_ref[...], HIGHEST)
        mods = _DirectExchange("gather", [mod_cols], [mods_ref], b_send, b_recv, b_local)
        mods.start()
        mods.wait()

        passed = []
        for j, chip in enumerate(chips):
            big(1 + j, (*chip, c), me).wait_recv()
            passed.append(big(4 + j, (*chip, c), sibling))
            passed[-1].start()
        big(0, sibling, me).wait_recv()
        for j, chip in enumerate(chips):
            big(4 + j, (*chip, 1 - c), me).wait_recv()
        for cp in first + passed:
            cp.wait_send()
        mine.wait()

    vmem = pl.BlockSpec(memory_space=pltpu.VMEM)
    hbm = pl.BlockSpec(memory_space=pl.ANY)
    cols = w_ada_shard.shape[1]
    return pl.pallas_call(
        body, name=name,
        out_shape=[jax.ShapeDtypeStruct((N_DEV,) + cond_conv.shape, F32), jax.ShapeDtypeStruct((N_DEV, N_DEV, cols), F32),
                   jax.ShapeDtypeStruct((N_DEV,) + w_in_shard.shape, w_in_shard.dtype)],
        in_specs=[vmem, vmem, hbm], out_specs=[vmem, vmem, hbm],
        scratch_shapes=[pltpu.VMEM((N_DEV, cols), F32), pltpu.VMEM((N_DEV, D_MODEL), F32)] + _DirectExchange.scratch(1) * 3,
    )(cond_conv, w_ada_shard, w_in_shard)


class _DirectExchange:
    def __init__(self, kind, src_refs, dst_refs, send_sems, recv_sems, local_sems):
        self.kind, self.src_refs, self.dst_refs = kind, src_refs, dst_refs
        self.send_sems, self.recv_sems, self.local_sems = send_sems, recv_sems, local_sems

    @staticmethod
    def scratch(n_arrays):
        copies = N_DEV - 1
        return [pltpu.SemaphoreType.DMA((copies * n_arrays,)), pltpu.SemaphoreType.DMA((copies * n_arrays,)),
                pltpu.SemaphoreType.DMA((n_arrays,))]

    def _copies(self):
        x, y, c = _position()
        me = 4 * x + 2 * y + c
        local, sends, arrivals = [], [], []
        for a, (src, dst) in enumerate(zip(self.src_refs, self.dst_refs)):
            own = src if self.kind == "gather" else src.at[me]
            local.append(pltpu.make_async_copy(own, dst.at[me], self.local_sems.at[a]))
            for k in range(1, N_DEV):
                px, py, pc = _peer(x, y, c, k)
                peer = 4 * px + 2 * py + pc
                sems = dict(send_sem=self.send_sems.at[(N_DEV - 1) * a + k - 1], recv_sem=self.recv_sems.at[(N_DEV - 1) * a + k - 1],
                            device_id=(px, py, pc), device_id_type=MESH_IDS)
                sends.append(pltpu.make_async_remote_copy(
                    src_ref=src if self.kind == "gather" else src.at[peer], dst_ref=dst.at[me], **sems))
                arrivals.append(pltpu.make_async_remote_copy(src_ref=own, dst_ref=dst.at[peer], **sems))
        return local, sends, arrivals

    def start(self):
        local, sends, _ = self._copies()
        for cp in local + sends:
            cp.start()

    def wait(self):
        local, sends, arrivals = self._copies()
        for cp in arrivals:
            cp.wait_recv()
        for cp in sends:
            cp.wait_send()
        for cp in local:
            cp.wait()


def _sum_slots(slots, name):
    _, r, n = slots.shape
    tr = _pick(r, 256, 16) if r % 16 == 0 else r

    def body(s_ref, o_ref):
        acc = s_ref[0].astype(F32)
        for d in range(1, N_DEV):
            acc = acc + s_ref[d].astype(F32)
        o_ref[...] = acc

    return pl.pallas_call(
        body, name=name, grid=(r // tr,),
        in_specs=[pl.BlockSpec((N_DEV, tr, n), lambda i: (0, i, 0))],
        out_specs=pl.BlockSpec((tr, n), lambda i: (i, 0)),
        out_shape=jax.ShapeDtypeStruct((r, n), F32),
    )(slots)


def _matmul(a, b, name, out_dtype=F32):
    m, k = a.shape
    _, n = b.shape
    tm, tn, tk = _pick(m, 1024, 16), _pick(n, 1408, 128), _pick(k, 1408, 128)
    nk = k // tk

    def body(a_ref, b_ref, o_ref, acc_ref):
        kk = pl.program_id(2)

        @pl.when(kk == 0)
        def _():
            acc_ref[...] = jnp.zeros_like(acc_ref)

        acc_ref[...] += _dot(a_ref[...].astype(MXU_DTYPE), b_ref[...].astype(MXU_DTYPE))

        @pl.when(kk == nk - 1)
        def _():
            o_ref[...] = acc_ref[...].astype(o_ref.dtype)

    return pl.pallas_call(
        body, name=name, grid=(m // tm, n // tn, nk),
        in_specs=[pl.BlockSpec((tm, tk), lambda i, j, kk: (i, kk)), pl.BlockSpec((tk, tn), lambda i, j, kk: (kk, j))],
        out_specs=pl.BlockSpec((tm, tn), lambda i, j, kk: (i, j)),
        out_shape=jax.ShapeDtypeStruct((m, n), out_dtype),
        scratch_shapes=[pltpu.VMEM((tm, tn), F32)],
    )(a, b)


def _matmul_pieces_beside_scatter(pieces, offsets, w, blocks, name):
    m = pieces[0].shape[0]
    n = w.shape[1]
    tm = _pick(m, 256, 16)
    gm = m // tm
    npieces, na = len(pieces), len(blocks)
    hbm = pl.BlockSpec(memory_space=pl.ANY)

    def body(*refs):
        a_refs, w_refs = refs[:npieces], refs[npieces:2 * npieces]
        rest = refs[2 * npieces:]
        o_ref = rest[na]
        exchange = _DirectExchange("scatter", rest[:na], rest[na + 1:2 * na + 1], *rest[2 * na + 1:])
        i = pl.program_id(0)

        @pl.when(i == 0)
        def _():
            exchange.start()

        acc = _dot(a_refs[0][...].astype(MXU_DTYPE), w_refs[0][...])
        for a_ref, w_ref in zip(a_refs[1:], w_refs[1:]):
            acc = acc + _dot(a_ref[...].astype(MXU_DTYPE), w_ref[...])
        o_ref[...] = acc

        @pl.when(i == gm - 1)
        def _():
            exchange.wait()

    a_specs = [pl.BlockSpec((tm, p.shape[1]), lambda i: (i, 0)) for p in pieces]
    w_specs = [pl.BlockSpec((p.shape[1], n), functools.partial(lambda i, blk: (blk, 0), blk=off // p.shape[1]))
               for p, off in zip(pieces, offsets)]
    return pl.pallas_call(
        body, name=name, grid=(gm,),
        in_specs=a_specs + w_specs + [hbm] * na,
        out_specs=[pl.BlockSpec((tm, n), lambda i: (i, 0))] + [hbm] * na,
        out_shape=[jax.ShapeDtypeStruct((m, n), F32)] + [jax.ShapeDtypeStruct(blk.shape, blk.dtype) for blk in blocks],
        scratch_shapes=_DirectExchange.scratch(na),
    )(*pieces, *([w] * npieces), *blocks)


def _matmul_nt(a, bt, name, out_dtype=F32):
    m, k = a.shape
    n, _ = bt.shape
    tm, tn, tk = _pick(m, 1024, 16), _pick(n, 1408, 128), _pick(k, 1408, 128)
    nk = k // tk

    def body(a_ref, b_ref, o_ref, acc_ref):
        kk = pl.program_id(2)

        @pl.when(kk == 0)
        def _():
            acc_ref[...] = jnp.zeros_like(acc_ref)

        acc_ref[...] += _dot_nt(a_ref[...].astype(MXU_DTYPE), b_ref[...].astype(MXU_DTYPE))

        @pl.when(kk == nk - 1)
        def _():
            o_ref[...] = acc_ref[...].astype(o_ref.dtype)

    return pl.pallas_call(
        body, name=name, grid=(m // tm, n // tn, nk),
        in_specs=[pl.BlockSpec((tm, tk), lambda i, j, kk: (i, kk)), pl.BlockSpec((tn, tk), lambda i, j, kk: (j, kk))],
        out_specs=pl.BlockSpec((tm, tn), lambda i, j, kk: (i, j)),
        out_shape=jax.ShapeDtypeStruct((m, n), out_dtype),
        scratch_shapes=[pltpu.VMEM((tm, tn), F32)],
    )(a, bt)


def _matmul_sum2(a1, b1, a2, b2, name):
    m, k = a1.shape
    _, n = b1.shape
    tm, tn, tk = _pick(m, 1024, 16), _pick(n, 1408, 128), _pick(k, 1408, 128)
    nk = k // tk

    def body(a1_ref, b1_ref, a2_ref, b2_ref, o_ref):
        @pl.when(pl.program_id(2) == 0)
        def _():
            o_ref[...] = jnp.zeros_like(o_ref)

        o_ref[...] += (_dot(a1_ref[...].astype(MXU_DTYPE), b1_ref[...].astype(MXU_DTYPE))
                       + _dot(a2_ref[...].astype(MXU_DTYPE), b2_ref[...].astype(MXU_DTYPE)))

    a_spec = pl.BlockSpec((tm, tk), lambda i, j, kk: (i, kk))
    b_spec = pl.BlockSpec((tk, tn), lambda i, j, kk: (kk, j))
    return pl.pallas_call(
        body, name=name, grid=(m // tm, n // tn, nk), in_specs=[a_spec, b_spec, a_spec, b_spec],
        out_specs=pl.BlockSpec((tm, tn), lambda i, j, kk: (i, j)),
        out_shape=jax.ShapeDtypeStruct((m, n), F32),
    )(a1, b1, a2, b2)


def _matmul_tn(a, b, name, out_dtype=F32):
    l, m = a.shape
    _, n = b.shape
    tm, tn, tl = _pick(m, 1408, 128), _pick(n, 1408, 128), _pick(l, 512, 16)
    nl = l // tl

    def body(a_ref, b_ref, o_ref, acc_ref):
        ll = pl.program_id(2)

        @pl.when(ll == 0)
        def _():
            acc_ref[...] = jnp.zeros_like(acc_ref)

        acc_ref[...] += _dot_tn(a_ref[...].astype(MXU_DTYPE), b_ref[...].astype(MXU_DTYPE))

        @pl.when(ll == nl - 1)
        def _():
            o_ref[...] = acc_ref[...].astype(o_ref.dtype)

    return pl.pallas_call(
        body, name=name, grid=(m // tm, n // tn, nl),
        in_specs=[pl.BlockSpec((tl, tm), lambda i, j, ll: (ll, i)), pl.BlockSpec((tl, tn), lambda i, j, ll: (ll, j))],
        out_specs=pl.BlockSpec((tm, tn), lambda i, j, ll: (i, j)),
        out_shape=jax.ShapeDtypeStruct((m, n), out_dtype),
        scratch_shapes=[pltpu.VMEM((tm, tn), F32)],
    )(a, b)


def _row_specs(l, d, n_rows, n_vecs):
    tl = _pick(l, 512, 16)
    row = pl.BlockSpec((tl, d), lambda i: (i, 0))
    vec = pl.BlockSpec((1, d), lambda i: (0, 0))
    return tl, row, vec, [row] * n_rows + [vec] * n_vecs


def _rms_mod(x, nw, sc, sh, name):
    l, d = x.shape
    tl, row, _, in_specs = _row_specs(l, d, 1, 3)

    def body(x_ref, nw_ref, sc_ref, sh_ref, h_ref):
        xv = x_ref[...]
        r = lax.rsqrt(jnp.mean(xv * xv, axis=-1, keepdims=True) + EPS)
        h_ref[...] = (((xv * r) * nw_ref[...]) * (1.0 + sc_ref[...]) + sh_ref[...]).astype(h_ref.dtype)

    return pl.pallas_call(body, name=name, grid=(l // tl,), in_specs=in_specs, out_specs=row,
                          out_shape=jax.ShapeDtypeStruct((l, d), MXU_DTYPE))(x, nw, sc, sh)


def _residual_rms_mod(x, mix, g, nw, sc, sh, name):
    l, d = x.shape
    tl, row, _, in_specs = _row_specs(l, d, 2, 4)

    def body(x_ref, mix_ref, g_ref, nw_ref, sc_ref, sh_ref, x1_ref, h_ref):
        xv = x_ref[...] + g_ref[...] * mix_ref[...]
        x1_ref[...] = xv
        r = lax.rsqrt(jnp.mean(xv * xv, axis=-1, keepdims=True) + EPS)
        h_ref[...] = (((xv * r) * nw_ref[...]) * (1.0 + sc_ref[...]) + sh_ref[...]).astype(h_ref.dtype)

    return pl.pallas_call(body, name=name, grid=(l // tl,), in_specs=in_specs, out_specs=[row, row],
                          out_shape=[jax.ShapeDtypeStruct((l, d), F32), jax.ShapeDtypeStruct((l, d), MXU_DTYPE)],
                          )(x, mix, g, nw, sc, sh)


def _norm_bwd(x, dh, dres, nw, sc, name, gated=None):
    l, d = x.shape
    tl, row, vec, in_specs = _row_specs(l, d, 3, 2)
    stats = pl.BlockSpec((8, d), lambda i: (0, 0))

    def body(x_ref, dh_ref, dres_ref, nw_ref, sc_ref, *rest):
        dx_ref, st_ref = rest[-2:] if gated is None else (rest[2], rest[4])
        xv, dh_v = x_ref[...], dh_ref[...]
        r = lax.rsqrt(jnp.mean(xv * xv, axis=-1, keepdims=True) + EPS)
        xn = xv * r
        dxn = dh_v * (nw_ref[...] * (1.0 + sc_ref[...]))
        dx = dres_ref[...] + r * (dxn - xn * jnp.mean(dxn * xn, axis=-1, keepdims=True))
        dx_ref[...] = dx

        @pl.when(pl.program_id(0) == 0)
        def _():
            st_ref[...] = jnp.zeros_like(st_ref)

        dhx = dh_v * xn
        st_ref[0:1, :] += _colsum(dhx * nw_ref[...])
        st_ref[1:2, :] += _colsum(dh_v)
        st_ref[2:3, :] += _colsum(dhx * (1.0 + sc_ref[...]))
        if gated is not None:
            val_ref, g_ref, dval_ref = rest[0], rest[1], rest[3]
            dval_ref[...] = (g_ref[...] * dx).astype(dval_ref.dtype)
            st_ref[3:4, :] += _colsum(dx * val_ref[...])

    if gated is None:
        return pl.pallas_call(body, name=name, grid=(l // tl,), in_specs=in_specs, out_specs=[row, stats],
                              out_shape=[jax.ShapeDtypeStruct((l, d), F32), jax.ShapeDtypeStruct((8, d), F32)],
                              )(x, dh, dres, nw, sc)
    return pl.pallas_call(body, name=name, grid=(l // tl,), in_specs=in_specs + [row, vec], out_specs=[row, row, stats],
                          out_shape=[jax.ShapeDtypeStruct((l, d), F32), jax.ShapeDtypeStruct((l, d), MXU_DTYPE),
                                     jax.ShapeDtypeStruct((8, d), F32)],
                          )(x, dh, dres, nw, sc, *gated)


def _loss_head(x1, f, g, target, name):
    l, d = x1.shape
    tl, row, _, _ = _row_specs(l, d, 0, 0)
    vec = pl.BlockSpec((1, d), lambda i: (0, 0))

    def body(x1_ref, f_ref, g_ref, t_ref, dy_ref, df_ref, st_ref, loss_ref):
        fv = f_ref[...]
        e = x1_ref[...] + g_ref[...] * fv - t_ref[...]
        dy = e * (1.0 / d)
        dy_ref[...] = dy
        df_ref[...] = (g_ref[...] * dy).astype(df_ref.dtype)

        @pl.when(pl.program_id(0) == 0)
        def _():
            loss_ref[...] = jnp.zeros_like(loss_ref)
            st_ref[...] = jnp.zeros_like(st_ref)

        st_ref[0:1, :] += _colsum(dy * fv)
        s = jnp.sum(jnp.sum(e * e, axis=1, keepdims=True), axis=0, keepdims=True)
        loss_ref[...] += (0.5 / d) * s

    return pl.pallas_call(body, name=name, grid=(l // tl,), in_specs=[row, row, vec, row],
                          out_specs=[row, row, pl.BlockSpec((8, d), lambda i: (0, 0)), pl.BlockSpec((8, 128), lambda i: (0, 0))],
                          out_shape=[jax.ShapeDtypeStruct((l, d), F32), jax.ShapeDtypeStruct((l, d), MXU_DTYPE),
                                     jax.ShapeDtypeStruct((8, d), F32), jax.ShapeDtypeStruct((8, 128), F32)],
                          )(x1, f, g, target)


def _ffn_in(h, w_gate_t, w_up_t, name):
    l, k = h.shape
    f = w_gate_t.shape[0]
    tl, tf = _pick(l, 512, 16), _pick(f, 1408, 128)
    h_spec = pl.BlockSpec((tl, k), lambda i, j: (i, 0))
    w_spec = pl.BlockSpec((tf, k), lambda i, j: (j, 0))
    spec = pl.BlockSpec((tl, tf), lambda i, j: (i, j))

    def body(h_ref, wg_ref, wu_ref, g_ref, u_ref, a_ref):
        hv = h_ref[...].astype(MXU_DTYPE)
        gv = _dot_nt(hv, wg_ref[...].astype(MXU_DTYPE))
        uv = _dot_nt(hv, wu_ref[...].astype(MXU_DTYPE))
        g_ref[...] = gv
        u_ref[...] = uv
        a_ref[...] = (gv * _sigmoid(gv) * uv).astype(a_ref.dtype)

    return pl.pallas_call(body, name=name, grid=(l // tl, f // tf), in_specs=[h_spec, w_spec, w_spec], out_specs=[spec, spec, spec],
                          out_shape=[jax.ShapeDtypeStruct((l, f), F32)] * 2 + [jax.ShapeDtypeStruct((l, f), MXU_DTYPE)],
                          )(h, w_gate_t, w_up_t)


def _ffn_down_bwd(dffn, w_down, gate, up, name):
    l, f = gate.shape
    k = dffn.shape[1]
    tl, tf = _pick(l, 512, 16), _pick(f, 1408, 128)
    spec = pl.BlockSpec((tl, tf), lambda i, j: (i, j))

    def body(d_ref, w_ref, g_ref, u_ref, dg_ref, du_ref):
        dav = _dot_nt(d_ref[...].astype(MXU_DTYPE), w_ref[...].astype(MXU_DTYPE))
        gv = g_ref[...]
        s = _sigmoid(gv)
        dg_ref[...] = (dav * u_ref[...] * (s * (1.0 + gv * (1.0 - s)))).astype(dg_ref.dtype)
        du_ref[...] = (dav * (gv * s)).astype(du_ref.dtype)

    return pl.pallas_call(body, name=name, grid=(l // tl, f // tf),
                          in_specs=[pl.BlockSpec((tl, k), lambda i, j: (i, 0)), pl.BlockSpec((tf, k), lambda i, j: (j, 0)), spec, spec],
                          out_specs=[spec, spec], out_shape=[jax.ShapeDtypeStruct((l, f), MXU_DTYPE)] * 2)(dffn, w_down, gate, up)


def _conv_tile(l):
    return _pick(l, 512, 16)


def _conv_pre(buf, w_ref, b_ref, tl):
    acc = b_ref[...] + w_ref[3:4, :] * buf[HALO:HALO + tl, :]
    for k in range(CONV_WIDTH - 1):
        s = HALO - (CONV_WIDTH - 1) + k
        acc = acc + w_ref[k:k + 1, :] * buf[s:s + tl, :]
    return acc


def _fill_past(buf, u_ref, halo_ref, tl):
    i = pl.program_id(0)

    @pl.when(i == 0)
    def _():
        buf[0:HALO, :] = jnp.zeros((HALO, buf.shape[1]), F32)

    @pl.when(i > 0)
    def _():
        buf[0:HALO, :] = halo_ref[...]

    buf[HALO:HALO + tl, :] = u_ref[...]


def _conv_silu(u, w, b, name):
    l, ch = u.shape[0], w.shape[1]
    tl = _conv_tile(l)
    per = tl // HALO
    cur = pl.BlockSpec((tl, ch), lambda i: (i, 0))
    past = pl.BlockSpec((HALO, ch), lambda i: (jnp.maximum(i * per - 1, 0), 0))

    def body(u_ref, halo_ref, w_ref, b_ref, o_ref, buf):
        _fill_past(buf, u_ref, halo_ref, tl)
        pre = _conv_pre(buf, w_ref, b_ref, tl)
        o_ref[...] = pre * _sigmoid(pre)

    return pl.pallas_call(body, name=name, grid=(l // tl,),
                          in_specs=[cur, past, pl.BlockSpec((CONV_WIDTH, ch), lambda i: (0, 0)), pl.BlockSpec((1, ch), lambda i: (0, 0))],
                          out_specs=cur, out_shape=jax.ShapeDtypeStruct((l, ch), F32),
                          scratch_shapes=[pltpu.VMEM((tl + HALO, ch), F32)])(u, u, w, b)


def _conv_silu_bwd_pre(u, dxc, w, b, name):
    l, ch = u.shape[0], w.shape[1]
    tl = _conv_tile(l)
    per = tl // HALO
    cur = pl.BlockSpec((tl, ch), lambda i: (i, 0))
    past = pl.BlockSpec((HALO, ch), lambda i: (jnp.maximum(i * per - 1, 0), 0))

    def body(u_ref, halo_ref, d_ref, w_ref, b_ref, dpre_ref, st_ref, buf):
        _fill_past(buf, u_ref, halo_ref, tl)
        pre = _conv_pre(buf, w_ref, b_ref, tl)
        s = _sigmoid(pre)
        dpre = d_ref[...] * (s * (1.0 + pre * (1.0 - s)))
        dpre_ref[...] = dpre

        @pl.when(pl.program_id(0) == 0)
        def _():
            st_ref[...] = jnp.zeros_like(st_ref)

        for k in range(CONV_WIDTH):
            s0 = HALO - (CONV_WIDTH - 1) + k
            st_ref[k:k + 1, :] += _colsum(dpre * buf[s0:s0 + tl, :])
        st_ref[CONV_WIDTH:CONV_WIDTH + 1, :] += _colsum(dpre)

    return pl.pallas_call(body, name=name, grid=(l // tl,),
                          in_specs=[cur, past, cur, pl.BlockSpec((CONV_WIDTH, ch), lambda i: (0, 0)), pl.BlockSpec((1, ch), lambda i: (0, 0))],
                          out_specs=[cur, pl.BlockSpec((8, ch), lambda i: (0, 0))],
                          out_shape=[jax.ShapeDtypeStruct((l, ch), F32), jax.ShapeDtypeStruct((8, ch), F32)],
                          scratch_shapes=[pltpu.VMEM((tl + HALO, ch), F32)])(u, u, dxc, w, b)


def _conv_bwd_input(dpre, w, name):
    l, ch = dpre.shape
    tl = _conv_tile(l)
    per = tl // HALO
    nt = l // tl
    cur = pl.BlockSpec((tl, ch), lambda i: (i, 0))
    nxt = pl.BlockSpec((HALO, ch), lambda i: (jnp.minimum((i + 1) * per, l // HALO - 1), 0))

    def body(d_ref, halo_ref, w_ref, du_ref, buf):
        i = pl.program_id(0)
        buf[0:tl, :] = d_ref[...]

        @pl.when(i == nt - 1)
        def _():
            buf[tl:tl + HALO, :] = jnp.zeros((HALO, ch), F32)

        @pl.when(i < nt - 1)
        def _():
            buf[tl:tl + HALO, :] = halo_ref[...]

        acc = w_ref[3:4, :] * buf[0:tl, :]
        for k in range(CONV_WIDTH - 1):
            s = CONV_WIDTH - 1 - k
            acc = acc + w_ref[k:k + 1, :] * buf[s:s + tl, :]
        du_ref[...] = acc

    return pl.pallas_call(body, name=name, grid=(nt,),
                          in_specs=[cur, nxt, pl.BlockSpec((CONV_WIDTH, ch), lambda i: (0, 0))],
                          out_specs=cur, out_shape=jax.ShapeDtypeStruct((l, ch), F32),
                          scratch_shapes=[pltpu.VMEM((tl + HALO, ch), F32)])(dpre, dpre, w)


HEAD_TILE = 128


def _chunk_iota():
    r = lax.broadcasted_iota(jnp.int32, (CHUNK, CHUNK), 0)
    c = lax.broadcasted_iota(jnp.int32, (CHUNK, CHUNK), 1)
    return r, c


def _split3_dot(v, ones_b):
    hi = v.astype(BF16)
    r1 = v - hi.astype(F32)
    mid = r1.astype(BF16)
    lo = (r1 - mid.astype(F32)).astype(BF16)
    return _dot(hi, ones_b) + _dot(mid, ones_b) + _dot(lo, ones_b)


def _split3_dot_nt(v, ones_b):
    hi = v.astype(BF16)
    r1 = v - hi.astype(F32)
    mid = r1.astype(BF16)
    lo = (r1 - mid.astype(F32)).astype(BF16)
    return _dot_nt(hi, ones_b) + _dot_nt(mid, ones_b) + _dot_nt(lo, ones_b)


def _ssd_decays(dt_ref, sx_ref, dtbx_ref, alx_ref, r, c):
    tri = (r >= c).astype(BF16)
    raw_x = _split3_dot(dt_ref[...], sx_ref[...]) + dtbx_ref[...]
    dt_x = _softplus(raw_x)
    a_x = -jnp.exp(alx_ref[...])
    adt_x = a_x * dt_x
    acs_x = _tri_dot(tri, adt_x)
    atot_x = _colsum(adt_x)
    return raw_x, dt_x, a_x, acs_x, jnp.exp(acs_x), jnp.exp(atot_x - acs_x), jnp.exp(atot_x)


def _head_decay(acs_x, e, r, c, halves):
    tile = acs_x[:, (e // 2) * 2 * HEAD_DIM:(e // 2 + 1) * 2 * HEAD_DIM]
    col = jnp.broadcast_to(jnp.max(jnp.where(halves[e % 2], tile, -jnp.inf), axis=1, keepdims=True), (CHUNK, CHUNK))
    return jnp.where(r >= c, jnp.exp(col - col.T), 0.0)


def _half_masks():
    lane = lax.broadcasted_iota(jnp.int32, (CHUNK, 2 * HEAD_DIM), 1)
    return lane < HEAD_DIM, lane >= HEAD_DIM


def _ssd_specs(nc, reverse):
    def at(i):
        return nc - 1 - i if reverse else i
    xc = pl.BlockSpec((CHUNK, D_CONV), lambda i: (at(i), 0))
    wide = pl.BlockSpec((CHUNK, D_SSD), lambda i: (at(i), 0))
    dt_tile = pl.BlockSpec((CHUNK, HEAD_TILE), lambda i: (at(i), O_DT // HEAD_TILE))
    dt_out = pl.BlockSpec((CHUNK, HEAD_TILE), lambda i: (at(i), 0))
    spread = pl.BlockSpec((HEAD_TILE, D_SSD), lambda i: (0, 0))
    vec = pl.BlockSpec((1, D_SSD), lambda i: (0, 0))
    state = pl.BlockSpec((1, SSD_STATE, D_SSD), lambda i: (at(i), 0, 0))
    return xc, wide, dt_tile, dt_out, spread, vec, state


def _ssd_fwd(xc, proj, spread_x, dtb_x, al_x, name):
    l = xc.shape[0]
    nc = l // CHUNK
    xc_s, wide_s, dt_s, _, spread_s, vec_s, state_s = _ssd_specs(nc, False)
    pairs_per_group = GROUP_WIDTH // (2 * HEAD_DIM)

    def body(xc_ref, dt_ref, sx_ref, dtbx_ref, alx_ref, y_ref, sp_ref, state):
        @pl.when(pl.program_id(0) == 0)
        def _():
            state[...] = jnp.zeros_like(state)

        r, c = _chunk_iota()
        halves = _half_masks()
        _, dt_x, _, acs_x, ea_x, ds_x, eatot_x = _ssd_decays(dt_ref, sx_ref, dtbx_ref, alx_ref, r, c)
        xg = xc_ref[:, 0:D_SSD] * dt_x
        sp_ref[0] = state[...]
        for g in range(SSD_GROUPS):
            lanes = slice(g * GROUP_WIDTH, (g + 1) * GROUP_WIDTH)
            bb = xc_ref[:, D_SSD + g * SSD_STATE:D_SSD + (g + 1) * SSD_STATE].astype(MXU_DTYPE)
            cb = xc_ref[:, D_SSD + (SSD_GROUPS + g) * SSD_STATE:D_SSD + (SSD_GROUPS + g + 1) * SSD_STATE].astype(MXU_DTYPE)
            scores = _dot_nt(cb, bb)
            sg = state[:, lanes]
            ys = []
            for j in range(g * pairs_per_group, (g + 1) * pairs_per_group):
                xg_pair = xg[:, j * 2 * HEAD_DIM:(j + 1) * 2 * HEAD_DIM]
                acc = jnp.zeros((CHUNK, 2 * HEAD_DIM), F32)
                for half in range(2):
                    m = (scores * _head_decay(acs_x, 2 * j + half, r, c, halves)).astype(MXU_DTYPE)
                    acc = acc + _dot(m, jnp.where(halves[half], xg_pair, 0.0).astype(MXU_DTYPE))
                ys.append(acc)
            y_ref[:, lanes] = jnp.concatenate(ys, axis=1) + _dot(cb, sg.astype(MXU_DTYPE)) * ea_x[:, lanes]
            state[:, lanes] = sg * eatot_x[:, lanes] + _dot_tn(bb, (xg[:, lanes] * ds_x[:, lanes]).astype(MXU_DTYPE))

    return pl.pallas_call(
        body, name=name, grid=(nc,),
        in_specs=[xc_s, dt_s, spread_s, vec_s, vec_s],
        out_specs=[wide_s, state_s],
        out_shape=[jax.ShapeDtypeStruct((l, D_SSD), F32), jax.ShapeDtypeStruct((nc, SSD_STATE, D_SSD), F32)],
        scratch_shapes=[pltpu.VMEM((SSD_STATE, D_SSD), F32)],
    )(xc, proj, spread_x, dtb_x, al_x)


def _ssd_bwd(xc, proj, spread_x, dtb_x, al_x, dy, states, name):
    l = xc.shape[0]
    nc = l // CHUNK
    xc_s, wide_s, dt_s, dt_out_s, spread_s, vec_s, state_s = _ssd_specs(nc, True)
    pairs_per_group = GROUP_WIDTH // (2 * HEAD_DIM)

    def body(xc_ref, dt_ref, sx_ref, dtbx_ref, alx_ref, dy_ref, sp_ref,
             dxc_ref, ddtr_ref, st_ref, hs_ref, dstate):
        @pl.when(pl.program_id(0) == 0)
        def _():
            dstate[...] = jnp.zeros_like(dstate)
            st_ref[...] = jnp.zeros_like(st_ref)

        r, c = _chunk_iota()
        halves = _half_masks()
        spread = sx_ref[...]

        def head_sums(v):
            return _split3_dot(_split3_dot_nt(v, spread), spread)

        raw_x, dt_x, a_x, acs_x, ea_x, ds_x, eatot_x = _ssd_decays(dt_ref, sx_ref, dtbx_ref, alx_ref, r, c)
        xs = xc_ref[:, 0:D_SSD]
        xg = xs * dt_x
        gy = dy_ref[...]
        s_prev = sp_ref[0]
        gea = gy * ea_x
        xds = xg * ds_x
        ds_old = dstate[...]
        later2 = jnp.concatenate([(r <= c).astype(BF16)] * 2, axis=1)
        dxg_parts, state_term, yoff_parts, dadt_parts = [], [], [], []
        for g in range(SSD_GROUPS):
            lanes = slice(g * GROUP_WIDTH, (g + 1) * GROUP_WIDTH)
            b_lo = D_SSD + g * SSD_STATE
            c_lo = D_SSD + (SSD_GROUPS + g) * SSD_STATE
            bb = xc_ref[:, b_lo:b_lo + SSD_STATE].astype(MXU_DTYPE)
            cb = xc_ref[:, c_lo:c_lo + SSD_STATE].astype(MXU_DTYPE)
            scores = _dot_nt(cb, bb)
            dsg = ds_old[:, lanes].astype(MXU_DTYPE)
            gea_b = gea[:, lanes].astype(MXU_DTYPE)
            xds_b = xds[:, lanes].astype(MXU_DTYPE)
            dxg_state = _dot(bb, dsg) * ds_x[:, lanes]
            dc = _dot_nt(gea_b, s_prev[:, lanes].astype(MXU_DTYPE))
            db = _dot_nt(xds_b, dsg)
            dscores = jnp.zeros((CHUNK, CHUNK), F32)
            diag = []
            for j in range(g * pairs_per_group, (g + 1) * pairs_per_group):
                pair = slice(j * 2 * HEAD_DIM, (j + 1) * 2 * HEAD_DIM)
                xg_pair = xg[:, pair].astype(MXU_DTYPE)
                acc = jnp.zeros((CHUNK, 2 * HEAD_DIM), F32)
                cols = []
                for half in range(2):
                    decay = _head_decay(acs_x, 2 * j + half, r, c, halves)
                    g_e = jnp.where(halves[half], gy[:, pair], 0.0).astype(MXU_DTYPE)
                    acc = acc + _dot_tn((scores * decay).astype(MXU_DTYPE), g_e)
                    dm = _dot_nt(g_e, xg_pair) * decay
                    dscores = dscores + dm
                    wq = dm * scores
                    hi = wq.astype(BF16)
                    lo = (wq - hi.astype(F32)).astype(BF16)
                    later = _dot(later2, jnp.concatenate([hi, lo], axis=0))
                    cols.append(jnp.sum(jnp.where(c < r, later, 0.0), axis=1, keepdims=True))
                diag.append(acc)
                dadt_parts.append(jnp.where(halves[0], cols[0], cols[1]))
            dsc_b = dscores.astype(MXU_DTYPE)
            dc = dc + _dot(dsc_b, bb)
            db = db + _dot_tn(dsc_b, cb)
            dxc_ref[:, b_lo:b_lo + SSD_STATE] = db
            dxc_ref[:, c_lo:c_lo + SSD_STATE] = dc
            dxg_parts.append(jnp.concatenate(diag, axis=1) + dxg_state)
            state_term.append(dxg_state)
            yoff_parts.append(_dot(cb, s_prev[:, lanes].astype(MXU_DTYPE)) * ea_x[:, lanes])
            dstate[:, lanes] = ds_old[:, lanes] * eatot_x[:, lanes] + _dot_tn(cb, gea_b)
        dxg = jnp.concatenate(dxg_parts, axis=1)
        dxc_ref[:, 0:D_SSD] = dxg * dt_x
        through_out = _tri_dot((r <= c).astype(BF16), gy * jnp.concatenate(yoff_parts, axis=1))
        through_in = _tri_dot((c < r).astype(BF16), xg * jnp.concatenate(state_term, axis=1))
        carried = jnp.broadcast_to(_colsum(ds_old * s_prev) * eatot_x, (8, D_SSD))
        dadt = (jnp.concatenate(dadt_parts, axis=1) + head_sums(through_out + through_in)
                + jnp.max(head_sums(carried), axis=0, keepdims=True))
        ddt = a_x * dadt + head_sums(dxg * xs)
        draw = ddt * _sigmoid(raw_x)
        ddtr_ref[...] = _split3_dot_nt(draw, spread) * (1.0 / HEAD_DIM)
        st_ref[0:1, :] += _colsum(dt_x * dadt) * a_x
        st_ref[1:2, :] += _colsum(draw)
        st_ref[2:3, :] += _colsum(gy * xs)

        @pl.when(pl.program_id(0) == nc - 1)
        def _():
            hs_ref[...] = head_sums(st_ref[...])

    stats = pl.BlockSpec((8, D_SSD), lambda i: (0, 0))
    return pl.pallas_call(
        body, name=name, grid=(nc,),
        in_specs=[xc_s, dt_s, spread_s, vec_s, vec_s, wide_s, state_s],
        out_specs=[xc_s, dt_out_s, stats, stats],
        out_shape=[jax.ShapeDtypeStruct((l, D_CONV), F32), jax.ShapeDtypeStruct((l, HEAD_TILE), F32),
                   jax.ShapeDtypeStruct((8, D_SSD), F32), jax.ShapeDtypeStruct((8, D_SSD), F32)],
        scratch_shapes=[pltpu.VMEM((SSD_STATE, D_SSD), F32)],
    )(xc, proj, spread_x, dtb_x, al_x, dy, states)


def _ssd_gate_norm(ycore, xc, z, dskip_x, norm_w, name):
    l = ycore.shape[0]
    tl = _pick(l, 512, 16)
    row = pl.BlockSpec((tl, D_SSD), lambda i: (i, 0))
    vec = pl.BlockSpec((1, D_SSD), lambda i: (0, 0))

    def body(y_ref, xs_ref, z_ref, dk_ref, nw_ref, o_ref):
        zv = z_ref[...]
        yv = (y_ref[...] + dk_ref[...] * xs_ref[...]) * (zv * _sigmoid(zv))
        for g in range(SSD_GROUPS):
            lanes = slice(g * GROUP_WIDTH, (g + 1) * GROUP_WIDTH)
            yg = yv[:, lanes]
            rg = lax.rsqrt(jnp.mean(yg * yg, axis=-1, keepdims=True) + EPS)
            o_ref[:, lanes] = (yg * rg * nw_ref[:, lanes]).astype(o_ref.dtype)

    z_spec = pl.BlockSpec((tl, D_SSD), lambda i: (i, O_Z // D_SSD))
    return pl.pallas_call(body, name=name, grid=(l // tl,), in_specs=[row, row, z_spec, vec, vec], out_specs=row,
                          out_shape=jax.ShapeDtypeStruct((l, D_SSD), MXU_DTYPE))(ycore, xc, z, dskip_x, norm_w)


def _ssd_gate_norm_bwd(dout, ycore, xc, z, dskip_x, norm_w, name):
    l = ycore.shape[0]
    tl = _pick(l, 512, 16)
    nt = l // tl
    row = pl.BlockSpec((tl, D_SSD), lambda i: (i, 0))
    vec = pl.BlockSpec((1, D_SSD), lambda i: (0, 0))

    def body(do_ref, y_ref, xs_ref, z_ref, dk_ref, nw_ref, dyc_ref, dxs_ref, dz_ref, st_ref):
        @pl.when(pl.program_id(0) == 0)
        def _():
            st_ref[...] = jnp.zeros_like(st_ref)

        zv, xs = z_ref[...], xs_ref[...]
        s = _sigmoid(zv)
        gz = zv * s
        yc = y_ref[...] + dk_ref[...] * xs
        yv = yc * gz
        dov = do_ref[...]
        dnw, dyv = [], []
        for g in range(SSD_GROUPS):
            lanes = slice(g * GROUP_WIDTH, (g + 1) * GROUP_WIDTH)
            yg = yv[:, lanes]
            rg = lax.rsqrt(jnp.mean(yg * yg, axis=-1, keepdims=True) + EPS)
            yn = yg * rg
            dnw.append(_colsum(dov[:, lanes] * yn))
            dyn = dov[:, lanes] * nw_ref[:, lanes]
            dyv.append(rg * (dyn - yn * jnp.mean(dyn * yn, axis=-1, keepdims=True)))
        dy = jnp.concatenate(dyv, axis=1)
        dyc = dy * gz
        dyc_ref[...] = dyc
        dxs_ref[...] = dyc * dk_ref[...]
        dz_ref[...] = dy * yc * (s * (1.0 + zv * (1.0 - s)))
        st_ref[0:1, :] += jnp.concatenate(dnw, axis=1)

    return pl.pallas_call(
        body, name=name, grid=(nt,), in_specs=[row, row, row, pl.BlockSpec((tl, D_SSD), lambda i: (i, O_Z // D_SSD)), vec, vec],
        out_specs=[row, row, row, pl.BlockSpec((8, D_SSD), lambda i: (0, 0))],
        out_shape=[jax.ShapeDtypeStruct((l, D_SSD), F32)] * 3 + [jax.ShapeDtypeStruct((8, D_SSD), F32)],
    )(dout, ycore, xc, z, dskip_x, norm_w)


PAIR = 2 * HEAD_DIM
N_PAIRS = N_HEADS // 2
Q_TILE0 = O_Q // PAIR
SB_KEYS = 1024
SB_PAIRS = 4
SB_PAIRS_FWD = 4
SB_SCALE = HEAD_DIM ** -0.5


def _pair_sum(v, lo):
    s_lo = jnp.sum(jnp.where(lo, v, 0.0), axis=-1, keepdims=True)
    s_hi = jnp.sum(jnp.where(lo, 0.0, v), axis=-1, keepdims=True)
    return jnp.where(lo, s_lo, s_hi)


def _qkv_prep(proj, qw2, kw2, name):
    l = proj.shape[0]
    tl = _pick(l, 1024, 16)
    out = pl.BlockSpec((tl, PAIR), lambda i, j: (i, j))
    vec = pl.BlockSpec((1, PAIR), lambda i, j: (0, 0))

    def at(which):
        return pl.BlockSpec((tl, PAIR), lambda i, j: (i, Q_TILE0 + which * N_PAIRS + j))

    def body(q_ref, k_ref, v_ref, qw_ref, kw_ref, qn_ref, kn_ref, vb_ref):
        lo = lax.broadcasted_iota(jnp.int32, (tl, PAIR), 1) < HEAD_DIM
        for t_ref, w_ref, o_ref, scale in ((q_ref, qw_ref, qn_ref, SB_SCALE), (k_ref, kw_ref, kn_ref, 1.0)):
            tv = t_ref[...]
            r = lax.rsqrt(_pair_sum(tv * tv, lo) * (1.0 / HEAD_DIM) + EPS)
            o_ref[...] = ((tv * r) * w_ref[...] * scale).astype(o_ref.dtype)
        vb_ref[...] = v_ref[...].astype(vb_ref.dtype)

    return pl.pallas_call(body, name=name, grid=(l // tl, N_PAIRS), in_specs=[at(0), at(1), at(2), vec, vec],
                          out_specs=[out, out, out], out_shape=[jax.ShapeDtypeStruct((l, D_SB), MXU_DTYPE)] * 3,
                          )(proj, proj, proj, qw2, kw2)


def _qk_norm_bwd(proj, dqn, dkn, qw2, kw2, name):
    l = proj.shape[0]
    tl = _pick(l, 1024, 16)
    out = pl.BlockSpec((tl, PAIR), lambda i, j: (i, j))
    vec = pl.BlockSpec((1, PAIR), lambda i, j: (0, 0))
    stats = pl.BlockSpec((8, PAIR), lambda i, j: (0, 0))

    def at(which):
        return pl.BlockSpec((tl, PAIR), lambda i, j: (i, Q_TILE0 + which * N_PAIRS + j))

    def body(q_ref, k_ref, dqn_ref, dkn_ref, qw_ref, kw_ref, dq_ref, dk_ref, stq_ref, stk_ref):
        @pl.when((pl.program_id(0) == 0) & (pl.program_id(1) == 0))
        def _():
            stq_ref[...] = jnp.zeros_like(stq_ref)
            stk_ref[...] = jnp.zeros_like(stk_ref)

        lo = lax.broadcasted_iota(jnp.int32, (tl, PAIR), 1) < HEAD_DIM
        for t_ref, dn_ref, w_ref, d_ref, st_ref, scale in ((q_ref, dqn_ref, qw_ref, dq_ref, stq_ref, SB_SCALE),
                                                           (k_ref, dkn_ref, kw_ref, dk_ref, stk_ref, 1.0)):
            tv = t_ref[...]
            r = lax.rsqrt(_pair_sum(tv * tv, lo) * (1.0 / HEAD_DIM) + EPS)
            tn = tv * r
            dnv = dn_ref[...] * scale
            dtn = dnv * w_ref[...]
            d_ref[...] = r * (dtn - tn * (_pair_sum(dtn * tn, lo) * (1.0 / HEAD_DIM)))
            st_ref[0:1, :] += _colsum(dnv * tn)

    return pl.pallas_call(body, name=name, grid=(l // tl, N_PAIRS), in_specs=[at(0), at(1), out, out, vec, vec],
                          out_specs=[out, out, stats, stats],
                          out_shape=[jax.ShapeDtypeStruct((l, D_SB), F32)] * 2 + [jax.ShapeDtypeStruct((8, PAIR), F32)] * 2,
                          )(proj, proj, dqn, dkn, qw2, kw2)


def _sb_masks():
    r = lax.broadcasted_iota(jnp.int32, (CHUNK, CHUNK), 0)
    c = lax.broadcasted_iota(jnp.int32, (CHUNK, CHUNK), 1)
    return r, c


def _stack2(mask):
    t = mask.astype(BF16)
    return jnp.concatenate([t, t], axis=0)


def _sb_fwd(q, k, v, shards, name):
    l = q.shape[0]
    nq = l // CHUNK
    kt = _pick(l, SB_KEYS, CHUNK)
    sub = kt // CHUNK
    pairs = SB_PAIRS_FWD
    heads = 2 * pairs
    steps = N_PAIRS // pairs
    na = len(shards)
    qblk = pl.BlockSpec((CHUNK, pairs * PAIR), lambda i, j: (j, i))
    full = pl.BlockSpec((l, pairs * PAIR), lambda i, j: (0, i))
    hbm = pl.BlockSpec(memory_space=pl.ANY)

    def body(q_ref, k_ref, v_ref, *rest):
        o_ref, tot_ref = rest[na:na + 2]
        exchange = _DirectExchange("gather", rest[:na], rest[na + 2:2 * na + 2], *rest[2 * na + 2:])
        qb = pl.program_id(1)

        @pl.when((pl.program_id(0) == 0) & (qb == 0))
        def _():
            exchange.start()

        r, c = _sb_masks()
        after2 = _stack2(r > c)
        halves = _half_masks()
        zero = jnp.zeros((CHUNK, PAIR), q_ref.dtype)
        lanes = [slice((a // 2) * PAIR, (a // 2 + 1) * PAIR) for a in range(heads)]
        qm = [jnp.where(halves[a % 2], q_ref[:, lanes[a]], zero) for a in range(heads)]
        last = (qb * CHUNK) // kt

        def tile(t, width, carries, accs, masked):
            rows = pl.ds(pl.multiple_of(t * width, width), width)
            if masked:
                key_minus_query = (lax.broadcasted_iota(jnp.int32, (CHUNK, width), 1)
                                   - lax.broadcasted_iota(jnp.int32, (CHUNK, width), 0))
                keep = key_minus_query < qb * CHUNK - t * width

            def logits(a):
                lg = _dot_nt(qm[a], k_ref[rows, lanes[a]])
                sp = _softplus_logits(lg)
                return lg - sp, (jnp.where(keep, -sp, 0.0) if masked else -sp)

            def sums(a, lr):
                offset, parts = carries[a], [None] * (width // CHUNK)
                for j in reversed(range(width // CHUNK)):
                    piece = lr[:, j * CHUNK:(j + 1) * CHUNK]
                    parts[j] = _split_dot(piece, after2) + offset
                    offset = offset + jnp.sum(piece, axis=1, keepdims=True)
                return jnp.concatenate(parts, axis=1), offset

            def output(a, ls, cs):
                w = jnp.exp(ls + cs)
                if masked:
                    w = jnp.where(keep, w, 0.0)
                return accs[a] + _dot(w.astype(MXU_DTYPE), v_ref[rows, lanes[a]])

            new_carries, new_accs = [None] * heads, [None] * heads
            ls, lr = logits(0)
            for a in range(heads):
                cs, new_carries[a] = sums(a, lr)
                if a + 1 < heads:
                    ls_next, lr = logits(a + 1)
                new_accs[a] = output(a, ls, cs)
                ls = ls_next
            return tuple(new_carries), tuple(new_accs)

        carries = tuple(jnp.zeros((CHUNK, 1), F32) for _ in range(heads))
        accs = tuple(jnp.zeros((CHUNK, PAIR), F32) for _ in range(heads))
        if sub % 2 == 0:
            half = kt // 2
            carries, accs = lax.cond((qb * CHUNK) % kt >= half,
                                     lambda cr, ac: tile(2 * last + 1, half, cr, ac, True), lambda cr, ac: (cr, ac), carries, accs)
            carries, accs = tile(2 * last, half, carries, accs, True)
        else:
            carries, accs = tile(last, kt, carries, accs, True)
        carries, accs = lax.fori_loop(1, last + 1, lambda i, st: tile(last - i, kt, st[0], st[1], False), (carries, accs))
        for p in range(pairs):
            o_ref[:, lanes[2 * p]] = jnp.where(halves[0], accs[2 * p], accs[2 * p + 1])
            tot_ref[:, lanes[2 * p]] = jnp.where(halves[0], carries[2 * p], carries[2 * p + 1])

        @pl.when((pl.program_id(0) == steps - 1) & (qb == nq - 1))
        def _():
            exchange.wait()

    return pl.pallas_call(
        body, name=name, grid=(steps, nq), in_specs=[qblk, full, full] + [hbm] * na,
        out_specs=[qblk, qblk] + [hbm] * na,
        out_shape=[jax.ShapeDtypeStruct((l, D_SB), F32), jax.ShapeDtypeStruct((l, D_SB), F32)]
        + [jax.ShapeDtypeStruct((N_DEV,) + s.shape, s.dtype) for s in shards],
        scratch_shapes=_DirectExchange.scratch(na),
    )(q, k, v, *shards)


def _sb_bwd(q, k, v, dycat, tot, blocks, name):
    l = q.shape[0]
    nq = l // CHUNK
    kt = _pick(l, SB_KEYS, CHUNK)
    sub = kt // CHUNK
    heads = 2 * SB_PAIRS
    steps = N_PAIRS // SB_PAIRS
    na = len(blocks)
    width_all = SB_PAIRS * PAIR
    qblk = pl.BlockSpec((CHUNK, width_all), lambda i, j: (j, i))
    doblk = pl.BlockSpec((CHUNK, width_all), lambda i, j: (j, D_SSD // width_all + i))
    full = pl.BlockSpec((l, width_all), lambda i, j: (0, i))
    resident = pl.BlockSpec((l, width_all), lambda i, j: (0, i), pipeline_mode=pl.Buffered(1))
    hbm = pl.BlockSpec(memory_space=pl.ANY)

    def body(q_ref, k_ref, v_ref, do_ref, tot_ref, *rest):
        dq_ref, dk_ref, dv_ref = rest[na:na + 3]
        exchange = _DirectExchange("scatter", rest[:na], rest[na + 3:2 * na + 3], *rest[2 * na + 3:])
        qb = pl.program_id(1)

        @pl.when((pl.program_id(0) == 0) & (qb == 0))
        def _():
            exchange.start()

        @pl.when(qb == 0)
        def _():
            dk_ref[...] = jnp.zeros_like(dk_ref)
            dv_ref[...] = jnp.zeros_like(dv_ref)

        r, c = _sb_masks()
        upto1 = (r <= c).astype(BF16)
        upto2 = jnp.concatenate([upto1, upto1], axis=0)
        halves = _half_masks()
        lanes = [slice((a // 2) * PAIR, (a // 2 + 1) * PAIR) for a in range(heads)]
        qm = [jnp.where(halves[a % 2], q_ref[:, lanes[a]], jnp.zeros((CHUNK, PAIR), q_ref.dtype)) for a in range(heads)]
        dom = [jnp.where(halves[a % 2], do_ref[:, lanes[a]], 0.0).astype(MXU_DTYPE) for a in range(heads)]
        total = [jnp.max(jnp.where(halves[a % 2], tot_ref[:, lanes[a]], -jnp.inf), axis=1, keepdims=True) for a in range(heads)]
        last = (qb * CHUNK) // kt

        def prefix(values, offset, sign, exact=True):
            parts = []
            for j in range(values.shape[1] // CHUNK):
                piece = values[:, j * CHUNK:(j + 1) * CHUNK]
                within = _split_dot(piece, upto2) if exact else _dot(piece.astype(BF16), upto1)
                parts.append(offset + within if sign > 0 else offset - within)
                step = jnp.sum(piece, axis=1, keepdims=True)
                offset = offset + step if sign > 0 else offset - step
            return jnp.concatenate(parts, axis=1), offset

        def tile(t, width, carry_p, carry_d, dq, masked):
            rows = pl.ds(pl.multiple_of(t * width, width), width)
            if masked:
                key_minus_query = (lax.broadcasted_iota(jnp.int32, (CHUNK, width), 1)
                                   - lax.broadcasted_iota(jnp.int32, (CHUNK, width), 0))
                keep = key_minus_query < qb * CHUNK - t * width

            def log_terms(a):
                lg = _dot_nt(qm[a], k_ref[rows, lanes[a]])
                sp = _softplus_logits(lg)
                return lg - sp, (jnp.where(keep, -sp, 0.0) if masked else -sp)

            def weights(a, ls, lr):
                remaining, p_next = prefix(lr, carry_p[a], -1)
                w = jnp.exp(ls + remaining)
                if masked:
                    w = jnp.where(keep, w, 0.0)
                return w, p_next

            def weight_grads(a, w):
                return _dot_nt(dom[a], v_ref[rows, lanes[a]]) * w

            def gradients(a, ls, w, da):
                d_incl, d_next = prefix(da, carry_d[a], 1, exact=False)
                sig = jnp.exp(ls)
                dl = da - sig * d_incl
                if masked:
                    dl = jnp.where(keep, dl, 0.0)
                dl_b = dl.astype(MXU_DTYPE)
                return (d_next, dq[a] + _dot(dl_b, k_ref[rows, lanes[a]]),
                        _dot_tn(w.astype(MXU_DTYPE), dom[a]), _dot_tn(dl_b, qm[a]))

            new_p, new_d, new_dq = [None] * heads, [None] * heads, [None] * heads
            dv_upd, dk_upd = [None] * heads, [None] * heads
            ls, lr = log_terms(0)
            w, new_p[0] = weights(0, ls, lr)
            for a in range(heads):
                if a + 1 < heads:
                    ls_next, lr_next = log_terms(a + 1)
                da = weight_grads(a, w)
                if a + 1 < heads:
                    w_next, new_p[a + 1] = weights(a + 1, ls_next, lr_next)
                new_d[a], new_dq[a], dv_upd[a], dk_upd[a] = gradients(a, ls, w, da)
                if a % 2 == 1:
                    dv_ref[rows, lanes[a]] += dv_upd[a - 1] + dv_upd[a]
                    dk_ref[rows, lanes[a]] += dk_upd[a - 1] + dk_upd[a]
                if a + 1 < heads:
                    ls, w = ls_next, w_next
            return tuple(new_p), tuple(new_d), tuple(new_dq)

        zeros = tuple(jnp.zeros((CHUNK, 1), F32) for _ in range(heads))
        dq0 = tuple(jnp.zeros((CHUNK, PAIR), F32) for _ in range(heads))
        state = lax.fori_loop(0, last, lambda t, st: tile(t, kt, st[0], st[1], st[2], False), (tuple(total), zeros, dq0))
        if sub % 2 == 0:
            half = kt // 2
            state = tile(2 * last, half, *state, True)
            state = lax.cond((qb * CHUNK) % kt >= half,
                             lambda cp, cd, dq: tile(2 * last + 1, half, cp, cd, dq, True), lambda cp, cd, dq: (cp, cd, dq), *state)
        else:
            state = tile(last, kt, *state, True)
        for p in range(SB_PAIRS):
            dq_ref[:, lanes[2 * p]] = jnp.where(halves[0], state[2][2 * p], state[2][2 * p + 1])

        @pl.when((pl.program_id(0) == steps - 1) & (qb == nq - 1))
        def _():
            exchange.wait()

    return pl.pallas_call(
        body, name=name, grid=(steps, nq),
        in_specs=[qblk, full, full, doblk, qblk] + [hbm] * na,
        out_specs=[qblk, resident, resident] + [hbm] * na,
        out_shape=[jax.ShapeDtypeStruct((l, D_SB), F32)] * 3 + [jax.ShapeDtypeStruct(b.shape, b.dtype) for b in blocks],
        scratch_shapes=_DirectExchange.scratch(na),
    )(q, k, v, dycat, tot, *blocks)


def _ada_bwd(c_all, dmod_shard, name):
    def body(c_ref, d_ref, o_ref):
        cv = c_ref[...]
        o_ref[...] = _dot_tn(cv * _sigmoid(cv), d_ref[...], HIGHEST)

    return pl.pallas_call(body, name=name, out_shape=jax.ShapeDtypeStruct((c_all.shape[1], dmod_shard.shape[1]), F32))(c_all, dmod_shard)


def _sum_small(parts, name):
    def body(p_ref, o_ref):
        acc = p_ref[0]
        for d in range(1, N_DEV):
            acc = acc + p_ref[d]
        o_ref[...] = acc

    return pl.pallas_call(body, name=name, out_shape=jax.ShapeDtypeStruct(parts.shape[1:], F32))(parts)


def _adamw(w, g, m, v, name):
    rows, cols = w.shape
    tr = _pick(rows, 256, 8)
    spec = pl.BlockSpec((tr, cols), lambda i: (i, 0))
    bc1 = 1.0 - ADAM_B1 ** ADAM_STEP
    bc2 = 1.0 - ADAM_B2 ** ADAM_STEP

    def body(w_ref, g_ref, m_ref, v_ref, d_ref, nm_ref, nv_ref):
        gv = g_ref[...]
        nm = ADAM_B1 * m_ref[...] + (1.0 - ADAM_B1) * gv
        nv = ADAM_B2 * v_ref[...] + (1.0 - ADAM_B2) * (gv * gv)
        nm_ref[...] = nm
        nv_ref[...] = nv
        d_ref[...] = -ADAM_LR * ((nm / bc1) / (jnp.sqrt(nv / bc2) + ADAM_EPS) + ADAM_WD * w_ref[...])

    return pl.pallas_call(body, name=name, grid=(rows // tr,), in_specs=[spec] * 4, out_specs=[spec] * 3,
                          out_shape=[jax.ShapeDtypeStruct((rows, cols), F32)] * 3)(w, g, m, v)


def _pad_w_in_t(w_in_t):
    lo = D_SSD + D_CONV
    return jnp.concatenate([w_in_t[D_SSD:lo + N_HEADS], jnp.zeros((DT_PAD - N_HEADS, w_in_t.shape[1]), w_in_t.dtype),
                            w_in_t[:D_SSD], w_in_t[lo + N_HEADS:]], axis=0)


def _local_step(x, target, mod, norm1_w, w_in_tp, conv_w, conv_b, dt_bias, a_log, d_skip, ssd_norm_w, q_norm_w, k_norm_w,
                norm2_w, later_shards):
    sh1, sc1, g1, sh2, sc2, g2 = [mod[:, i * D_MODEL:(i + 1) * D_MODEL] for i in range(N_MOD)]
    qw2, kw2 = jnp.tile(q_norm_w, (1, 2)), jnp.tile(k_norm_w, (1, 2))
    dskip_x = jnp.repeat(d_skip, HEAD_DIM, axis=1)
    dtb_x, al_x = jnp.repeat(dt_bias, HEAD_DIM, axis=1), jnp.repeat(a_log, HEAD_DIM, axis=1)
    head_ids = jnp.arange(HEAD_TILE, dtype=jnp.int32)[:, None]
    spread_x = (head_ids == jnp.arange(D_SSD, dtype=jnp.int32)[None, :] // HEAD_DIM).astype(BF16)

    h1 = _rms_mod(x, norm1_w, sc1, sh1, "rms_mod1")
    proj = _matmul_nt(h1, w_in_tp, "in_proj")

    xc = _conv_silu(proj, conv_w, conv_b, "conv_silu")
    ycore, states = _ssd_fwd(xc, proj, spread_x, dtb_x, al_x, "ssd_fwd")
    y_ssd = _ssd_gate_norm(ycore, xc, proj, dskip_x, ssd_norm_w, "ssd_gate_norm")

    qn, kn, vb = _qkv_prep(proj, qw2, kw2, "qkv_prep")
    o_sb, tot, *gathered = _sb_fwd(qn, kn, vb, later_shards, "sb_fwd")
    w_out, w_gate_t, w_up_t, w_down = [g.reshape(N_DEV * g.shape[1], D_MODEL) for g in gathered]
    ycat = jnp.concatenate([y_ssd, o_sb.astype(MXU_DTYPE)], axis=1)

    mix = _matmul(ycat, w_out, "out_proj")
    x1, h2 = _residual_rms_mod(x, mix, g1, norm2_w, sc2, sh2, "residual_rms_mod2")
    gate, up, act = _ffn_in(h2, w_gate_t, w_up_t, "ffn_in")
    ffn = _matmul(act, w_down, "ffn_down")
    dy, dffn, st_g2, loss_blk = _loss_head(x1, ffn, g2, target, "loss_head")

    g_down = _matmul_tn(act, dffn, "ffn_down_dw", MXU_DTYPE)
    dgate, dup = _ffn_down_bwd(dffn, w_down, gate, up, "ffn_down_dx")
    g_gate_t = _matmul_tn(dgate, h2, "ffn_gate_dw", MXU_DTYPE)
    g_up_t = _matmul_tn(dup, h2, "ffn_up_dw", MXU_DTYPE)
    dh2 = _matmul_sum2(dgate, w_gate_t, dup, w_up_t, "ffn_dh")
    dx1, dmix, st_n2 = _norm_bwd(x1, dh2, dy, norm2_w, sc2, "norm2_bwd", gated=(mix, g1))

    g_out = _matmul_tn(ycat, dmix, "out_proj_dw", MXU_DTYPE)
    dycat = _matmul_nt(dmix, w_out, "out_proj_dx")

    partials = [g.reshape(N_DEV, g.shape[0] // N_DEV, D_MODEL).astype(BF16) for g in (g_out, g_gate_t, g_up_t, g_down)]
    dqn, dkn, dv, *slots = _sb_bwd(qn, kn, vb, dycat, tot, partials, "sb_bwd")
    dq, dk, st_q, st_k = _qk_norm_bwd(proj, dqn, dkn, qw2, kw2, "qk_norm_bwd")

    dycore, dxs_skip, dz, st_gn = _ssd_gate_norm_bwd(dycat, ycore, xc, proj, dskip_x, ssd_norm_w, "ssd_gate_norm_bwd")
    dxc, ddt_tile, st_ssd, st_heads = _ssd_bwd(xc, proj, spread_x, dtb_x, al_x, dycore, states, "ssd_bwd")
    dxc = jnp.concatenate([dxc[:, :D_SSD] + dxs_skip, dxc[:, D_SSD:]], axis=1)
    dpre, st_conv = _conv_silu_bwd_pre(proj, dxc, conv_w, conv_b, "conv_silu_bwd")
    dxbc = _conv_bwd_input(dpre, conv_w, "conv_bwd_input")

    pieces = {"z": (dz, O_Z), "xbc": (dxbc, O_XBC), "dt": (ddt_tile, O_DT), "q": (dq, O_Q), "k": (dk, O_Q + D_SB), "v": (dv, O_Q + 2 * D_SB)}
    g_rows = {n: _matmul_tn(p, h1, "in_proj_dw_" + n, MXU_DTYPE) for n, (p, _) in pieces.items()}
    g_rows["dt"] = g_rows["dt"][:N_HEADS]
    g_in_t = jnp.concatenate(list(g_rows.values()), axis=0)
    g_in_blocks = g_in_t.reshape(N_DEV, g_in_t.shape[0] // N_DEV, D_MODEL).astype(BF16)
    dh1, g_in_slots = _matmul_pieces_beside_scatter([p for p, _ in pieces.values()], [o for _, o in pieces.values()], w_in_tp,
                                                    [g_in_blocks], "in_proj_dx")
    slots = [g_in_slots] + list(slots)
    grad_x, st_n1 = _norm_bwd(x, dh1, dx1, norm1_w, sc1, "norm1_bwd")

    pad = jnp.zeros((1, SM_SSD_NORM - SM_D_SKIP - N_HEADS), F32)
    small = jnp.concatenate(
        [st_n1[1:2], st_n1[0:1], st_n2[3:4], st_n2[1:2], st_n2[0:1], st_g2[0:1],
         st_n1[2:3], st_conv[4:5], st_conv[0:4].reshape(1, CONV_WIDTH * D_CONV),
         st_ssd[1:2, ::HEAD_DIM], st_ssd[0:1, ::HEAD_DIM], st_heads[2:3, ::HEAD_DIM], pad,
         st_gn[0:1], st_q[0:1, :HEAD_DIM] + st_q[0:1, HEAD_DIM:], st_k[0:1, :HEAD_DIM] + st_k[0:1, HEAD_DIM:], st_n2[2:3]], axis=1)
    return loss_blk, grad_x, slots, small


def kernel(x, c, w_ada, b_ada, norm1_w, w_in, conv_w, conv_b, dt_bias, a_log, d_skip, ssd_norm_w, q_norm_w, k_norm_w, w_out, norm2_w, w_gate, w_up, w_down, loss_target, m_w_ada, m_b_ada, m_norm1_w, m_w_in, m_conv_w, m_conv_b, m_dt_bias, m_a_log, m_d_skip, m_ssd_norm_w, m_q_norm_w, m_k_norm_w, m_w_out, m_norm2_w, m_w_gate, m_w_up, m_w_down, v_w_ada, v_b_ada, v_norm1_w, v_w_in, v_conv_w, v_conv_b, v_dt_bias, v_a_log, v_d_skip, v_ssd_norm_w, v_q_norm_w, v_k_norm_w, v_w_out, v_norm2_w, v_w_gate, v_w_up, v_w_down):
    me = 4 * lax.axis_index("x") + 2 * lax.axis_index("y") + lax.axis_index("c")
    conv_cols = D_CONV // N_DEV
    ada_cols = N_MOD * D_MODEL // N_DEV

    cond_conv = jnp.concatenate([c, conv_w[0].reshape(1, CONV_WIDTH * conv_cols)], axis=1)
    first, mods, w_in_t = _prologue(cond_conv, w_ada[0], w_in[0].T.astype(BF16), "prologue")
    w_in_tp = _pad_w_in_t(w_in_t.reshape(D_IN_PROJ, D_MODEL))
    later_shards = [s.astype(BF16) for s in (w_out[0], w_gate[0].T, w_up[0].T, w_down[0])]
    c_all = first[:, 0, :D_MODEL]
    conv_w_f = first[:, 0, D_MODEL:].reshape(N_DEV, CONV_WIDTH, conv_cols).transpose(1, 0, 2).reshape(CONV_WIDTH, D_CONV)
    mod = lax.dynamic_index_in_dim(mods, me, axis=1, keepdims=False)
    mod = mod.reshape(1, N_MOD * D_MODEL) + b_ada

    loss_blk, grad_x, slots, small = _local_step(
        x[0], loss_target[0], mod, norm1_w, w_in_tp, conv_w_f, conv_b, dt_bias, a_log, d_skip, ssd_norm_w, q_norm_w, k_norm_w,
        norm2_w, later_shards)
    loss = lax.psum(loss_blk[0, 0], ("x", "y", "c"))

    g_in_t, g_out, g_gate_t, g_up_t, g_down = [_sum_slots(s, "sum_grads_" + n) for s, n in zip(slots, ("in", "out", "gate", "up", "down"))]
    g_in, g_gate, g_up = g_in_t.T, g_gate_t.T, g_up_t.T

    parts = _all_gather_small(small, "gather_small")
    gsum = _sum_small(parts, "sum_small")
    dmod_shard = lax.dynamic_slice_in_dim(parts[:, 0, :N_MOD * D_MODEL], me * ada_cols, ada_cols, axis=1)
    g_ada = _ada_bwd(c_all, dmod_shard, "ada_bwd")
    g_conv_w = lax.dynamic_slice_in_dim(gsum[:, SM_CONV_W:SM_DT_BIAS].reshape(CONV_WIDTH, D_CONV), me * conv_cols, conv_cols, axis=1)

    def pack_small(b_ada_, norm1_, conv_b_, dt_bias_, a_log_, d_skip_, ssd_norm_, q_norm_, k_norm_, norm2_):
        return jnp.concatenate(
            [b_ada_, norm1_, conv_b_, jnp.zeros((1, CONV_WIDTH * D_CONV), F32), dt_bias_, a_log_, d_skip_,
             jnp.zeros((1, SM_SSD_NORM - SM_D_SKIP - N_HEADS), F32), ssd_norm_, q_norm_, k_norm_, norm2_], axis=1)

    def unpack_small(p):
        return {
            "b_ada": p[:, SM_B_ADA:SM_NORM1], "norm1_w": p[:, SM_NORM1:SM_CONV_B], "conv_b": p[:, SM_CONV_B:SM_CONV_W],
            "dt_bias": p[:, SM_DT_BIAS:SM_A_LOG], "a_log": p[:, SM_A_LOG:SM_D_SKIP], "d_skip": p[:, SM_D_SKIP:SM_D_SKIP + N_HEADS],
            "ssd_norm_w": p[:, SM_SSD_NORM:SM_Q_NORM], "q_norm_w": p[:, SM_Q_NORM:SM_K_NORM], "k_norm_w": p[:, SM_K_NORM:SM_NORM2],
            "norm2_w": p[:, SM_NORM2:SM_TOTAL]}

    w_small = pack_small(b_ada, norm1_w, conv_b, dt_bias, a_log, d_skip, ssd_norm_w, q_norm_w, k_norm_w, norm2_w)
    m_small = pack_small(m_b_ada, m_norm1_w, m_conv_b, m_dt_bias, m_a_log, m_d_skip, m_ssd_norm_w, m_q_norm_w, m_k_norm_w, m_norm2_w)
    v_small = pack_small(v_b_ada, v_norm1_w, v_conv_b, v_dt_bias, v_a_log, v_d_skip, v_ssd_norm_w, v_q_norm_w, v_k_norm_w, v_norm2_w)
    small_out = [unpack_small(t) for t in (gsum,) + tuple(_adamw(w_small, gsum, m_small, v_small, "adamw_small"))]

    sharded = {
        "w_ada": (w_ada[0], g_ada, m_w_ada[0], v_w_ada[0]),
        "w_in": (w_in[0], g_in, m_w_in[0], v_w_in[0]),
        "conv_w": (conv_w[0], g_conv_w, m_conv_w[0], v_conv_w[0]),
        "w_out": (w_out[0], g_out, m_w_out[0], v_w_out[0]),
        "w_gate": (w_gate[0], g_gate, m_w_gate[0], v_w_gate[0]),
        "w_up": (w_up[0], g_up, m_w_up[0], v_w_up[0]),
        "w_down": (w_down[0], g_down, m_w_down[0], v_w_down[0]),
    }
    sharded_out = {n: (t[1],) + tuple(_adamw(*t, "adamw_" + n)) for n, t in sharded.items()}

    names = ["w_ada", "b_ada", "norm1_w", "w_in", "conv_w", "conv_b", "dt_bias", "a_log", "d_skip", "ssd_norm_w", "q_norm_w",
             "k_norm_w", "w_out", "norm2_w", "w_gate", "w_up", "w_down"]
    outs = [loss, grad_x[None]]
    for kind in range(4):
        for n in names:
            outs.append(sharded_out[n][kind][None] if n in sharded_out else small_out[kind][n])
    return tuple(outs)
```

```python
import functools

import jax
import jax.numpy as jnp
from jax import lax
from jax.experimental import pallas as pl
from jax.experimental.pallas import tpu as pltpu

F32 = jnp.float32
BF16 = jnp.bfloat16
MXU_DTYPE = jnp.bfloat16
HIGHEST = lax.Precision.HIGHEST
MESH_IDS = pl.DeviceIdType.MESH

N_DEV = 8
D_MODEL = 1024
HEAD_DIM = 64
N_HEADS = 16
D_SSD = 1024
D_SB = 1024
SSD_GROUPS = 2
SSD_STATE = 128
GROUP_WIDTH = D_SSD // SSD_GROUPS
D_CONV = D_SSD + 2 * SSD_GROUPS * SSD_STATE
CONV_WIDTH = 4
CHUNK = 128
D_FF = 2816
N_MOD = 6
EPS = 1e-6
D_IN_PROJ = 5648
O_XBC = 0
O_DT = D_CONV
DT_PAD = 512
O_Z = O_DT + DT_PAD
O_Q = O_Z + D_SSD
D_IN_PAD = O_Q + 3 * D_SB
HALO = 8

ADAM_LR = 0.001
ADAM_B1 = 0.9
ADAM_B2 = 0.999
ADAM_EPS = 1e-08
ADAM_WD = 0.01
ADAM_STEP = 10

SM_B_ADA = 0
SM_NORM1 = 6144
SM_CONV_B = 7168
SM_CONV_W = 8704
SM_DT_BIAS = 14848
SM_A_LOG = 14864
SM_D_SKIP = 14880
SM_SSD_NORM = 14976
SM_Q_NORM = 16000
SM_K_NORM = 16064
SM_NORM2 = 16128
SM_TOTAL = 17152


def _pick(n, cap, mult):
    if n <= cap:
        return n
    best = None
    for t in range(mult, cap + 1, mult):
        if n % t == 0:
            best = t
    assert best is not None, (n, cap, mult)
    return best


def _dot(a, b, precision=None):
    return jnp.dot(a, b, preferred_element_type=F32, precision=precision)


def _dot_nt(a, b, precision=None):
    return lax.dot_general(a, b, (((1,), (1,)), ((), ())), preferred_element_type=F32, precision=precision)


def _dot_tn(a, b, precision=None):
    return lax.dot_general(a, b, (((0,), (0,)), ((), ())), preferred_element_type=F32, precision=precision)


def _softplus(v):
    return jnp.maximum(v, 0.0) + jnp.log1p(jnp.exp(-jnp.abs(v)))


def _softplus_logits(v):
    return jnp.maximum(v, 0.0) + jnp.log(1.0 + jnp.exp(-jnp.abs(v)))


def _tri_dot(tri_b, v):
    hi = v.astype(BF16)
    r1 = v - hi.astype(F32)
    mid = r1.astype(BF16)
    lo = (r1 - mid.astype(F32)).astype(BF16)
    return _dot(tri_b, hi) + _dot(tri_b, mid) + _dot(tri_b, lo)


def _sigmoid(v):
    return jax.nn.sigmoid(v)


def _colsum(v):
    return jnp.sum(v, axis=0, keepdims=True)


def _split_dot(v, tri2):
    hi = v.astype(BF16)
    lo = (v - hi.astype(F32)).astype(BF16)
    return _dot(jnp.concatenate([hi, lo], axis=1), tri2)


def _position():
    x, y, c = lax.axis_index("x"), lax.axis_index("y"), lax.axis_index("c")
    return x, y, c


def _peer(x, y, c, k):
    px = 1 - x if (k >> 2) & 1 else x
    py = 1 - y if (k >> 1) & 1 else y
    pc = 1 - c if k & 1 else c
    return px, py, pc


def _all_gather_small(v, name):
    n = v.shape[1]

    def body(v_ref, out_ref, send_sems, recv_sems, local_sem):
        x, y, c = _position()
        me = 4 * x + 2 * y + c
        mine = pltpu.make_async_copy(v_ref, out_ref.at[me], local_sem)
        mine.start()
        sends = []
        for k in range(1, N_DEV):
            cp = pltpu.make_async_remote_copy(
                src_ref=v_ref, dst_ref=out_ref.at[me], send_sem=send_sems.at[k - 1], recv_sem=recv_sems.at[k - 1],
                device_id=_peer(x, y, c, k), device_id_type=MESH_IDS)
            cp.start()
            sends.append(cp)
        for k in range(1, N_DEV):
            px, py, pc = _peer(x, y, c, k)
            pltpu.make_async_remote_copy(
                src_ref=v_ref, dst_ref=out_ref.at[4 * px + 2 * py + pc], send_sem=send_sems.at[k - 1],
                recv_sem=recv_sems.at[k - 1], device_id=(px, py, pc), device_id_type=MESH_IDS).wait_recv()
        for cp in sends:
            cp.wait_send()
        mine.wait()

    return pl.pallas_call(
        body, name=name,
        out_shape=jax.ShapeDtypeStruct((N_DEV, 1, n), v.dtype),
        in_specs=[pl.BlockSpec(memory_space=pltpu.VMEM)],
        out_specs=pl.BlockSpec(memory_space=pltpu.VMEM),
        scratch_shapes=[pltpu.SemaphoreType.DMA((N_DEV - 1,)), pltpu.SemaphoreType.DMA((N_DEV - 1,)),
                        pltpu.SemaphoreType.DMA],
    )(v)


def _prologue(cond_conv, w_ada_shard, w_in_shard, name):
    def body(cc_ref, wada_ref, win_ref, first_ref, mods_ref, wout_ref, mod_cols, conds_all, big_send, big_recv, big_local,
             a_send, a_recv, a_local, b_send, b_recv, b_local):
        x, y, c = _position()
        me, sibling = (x, y, c), (x, y, 1 - c)
        chips = [(1 - x, y), (x, 1 - y), (1 - x, 1 - y)]

        def big(k, blk, to, own=False):
            slot = wout_ref.at[4 * blk[0] + 2 * blk[1] + blk[2]]
            return pltpu.make_async_remote_copy(src_ref=win_ref if own else slot, dst_ref=slot, send_sem=big_send.at[k],
                                                recv_sem=big_recv.at[k], device_id=to, device_id_type=MESH_IDS)

        mine = pltpu.make_async_copy(win_ref, wout_ref.at[4 * x + 2 * y + c], big_local.at[0])
        mine.start()
        first = [big(0, me, sibling, own=True)] + [big(1 + j, me, (*chip, c), own=True) for j, chip in enumerate(chips)]
        for cp in first:
            cp.start()

        conds = _DirectExchange("gather", [cc_ref], [first_ref], a_send, a_recv, a_local)
        conds.start()
        conds.wait()
        for d in range(N_DEV):
            conds_all[d:d + 1, :] = first_ref[d, :, 0:D_MODEL]
        cv = conds_all[...]
        mod_cols[...] = _dot(cv * _sigmoid(cv), wada_ref[...], HIGHEST)
        mods = _DirectExchange("gather", [mod_cols], [mods_ref], b_send, b_recv, b_local)
        mods.start()
        mods.wait()

        passed = []
        for j, chip in enumerate(chips):
            big(1 + j, (*chip, c), me).wait_recv()
            passed.append(big(4 + j, (*chip, c), sibling))
            passed[-1].start()
        big(0, sibling, me).wait_recv()
        for j, chip in enumerate(chips):
            big(4 + j, (*chip, 1 - c), me).wait_recv()
        for cp in first + passed:
            cp.wait_send()
        mine.wait()

    vmem = pl.BlockSpec(memory_space=pltpu.VMEM)
    hbm = pl.BlockSpec(memory_space=pl.ANY)
    cols = w_ada_shard.shape[1]
    return pl.pallas_call(
        body, name=name,
        out_shape=[jax.ShapeDtypeStruct((N_DEV,) + cond_conv.shape, F32), jax.ShapeDtypeStruct((N_DEV, N_DEV, cols), F32),
                   jax.ShapeDtypeStruct((N_DEV,) + w_in_shard.shape, w_in_shard.dtype)],
        in_specs=[vmem, vmem, hbm], out_specs=[vmem, vmem, hbm],
        scratch_shapes=[pltpu.VMEM((N_DEV, cols), F32), pltpu.VMEM((N_DEV, D_MODEL), F32)] + _DirectExchange.scratch(1) * 3,
    )(cond_conv, w_ada_shard, w_in_shard)


class _DirectExchange:
    def __init__(self, kind, src_refs, dst_refs, send_sems, recv_sems, local_sems):
        self.kind, self.src_refs, self.dst_refs = kind, src_refs, dst_refs
        self.send_sems, self.recv_sems, self.local_sems = send_sems, recv_sems, local_sems

    @staticmethod
    def scratch(n_arrays):
        copies = N_DEV - 1
        return [pltpu.SemaphoreType.DMA((copies * n_arrays,)), pltpu.SemaphoreType.DMA((copies * n_arrays,)),
                pltpu.SemaphoreType.DMA((n_arrays,))]

    def _copies(self):
        x, y, c = _position()
        me = 4 * x + 2 * y + c
        local, sends, arrivals = [], [], []
        for a, (src, dst) in enumerate(zip(self.src_refs, self.dst_refs)):
            own = src if self.kind == "gather" else src.at[me]
            local.append(pltpu.make_async_copy(own, dst.at[me], self.local_sems.at[a]))
            for k in range(1, N_DEV):
                px, py, pc = _peer(x, y, c, k)
                peer = 4 * px + 2 * py + pc
                sems = dict(send_sem=self.send_sems.at[(N_DEV - 1) * a + k - 1], recv_sem=self.recv_sems.at[(N_DEV - 1) * a + k - 1],
                            device_id=(px, py, pc), device_id_type=MESH_IDS)
                sends.append(pltpu.make_async_remote_copy(
                    src_ref=src if self.kind == "gather" else src.at[peer], dst_ref=dst.at[me], **sems))
                arrivals.append(pltpu.make_async_remote_copy(src_ref=own, dst_ref=dst.at[peer], **sems))
        return local, sends, arrivals

    def start(self):
        local, sends, _ = self._copies()
        for cp in local + sends:
            cp.start()

    def wait(self):
        local, sends, arrivals = self._copies()
        for cp in arrivals:
            cp.wait_recv()
        for cp in sends:
            cp.wait_send()
        for cp in local:
            cp.wait()


def _sum_slots(slots, name):
    _, r, n = slots.shape
    tr = _pick(r, 256, 16) if r % 16 == 0 else r

    def body(s_ref, o_ref):
        acc = s_ref[0].astype(F32)
        for d in range(1, N_DEV):
            acc = acc + s_ref[d].astype(F32)
        o_ref[...] = acc

    return pl.pallas_call(
        body, name=name, grid=(r // tr,),
        in_specs=[pl.BlockSpec((N_DEV, tr, n), lambda i: (0, i, 0))],
        out_specs=pl.BlockSpec((tr, n), lambda i: (i, 0)),
        out_shape=jax.ShapeDtypeStruct((r, n), F32),
    )(slots)


def _matmul(a, b, name, out_dtype=F32):
    m, k = a.shape
    _, n = b.shape
    tm, tn, tk = _pick(m, 1024, 16), _pick(n, 1408, 128), _pick(k, 1408, 128)
    nk = k // tk

    def body(a_ref, b_ref, o_ref, acc_ref):
        kk = pl.program_id(2)

        @pl.when(kk == 0)
        def _():
            acc_ref[...] = jnp.zeros_like(acc_ref)

        acc_ref[...] += _dot(a_ref[...].astype(MXU_DTYPE), b_ref[...].astype(MXU_DTYPE))

        @pl.when(kk == nk - 1)
        def _():
            o_ref[...] = acc_ref[...].astype(o_ref.dtype)

    return pl.pallas_call(
        body, name=name, grid=(m // tm, n // tn, nk),
        in_specs=[pl.BlockSpec((tm, tk), lambda i, j, kk: (i, kk)), pl.BlockSpec((tk, tn), lambda i, j, kk: (kk, j))],
        out_specs=pl.BlockSpec((tm, tn), lambda i, j, kk: (i, j)),
        out_shape=jax.ShapeDtypeStruct((m, n), out_dtype),
        scratch_shapes=[pltpu.VMEM((tm, tn), F32)],
    )(a, b)


def _matmul_pieces_beside_scatter(pieces, offsets, w, blocks, name):
    m = pieces[0].shape[0]
    n = w.shape[1]
    tm = _pick(m, 256, 16)
    gm = m // tm
    npieces, na = len(pieces), len(blocks)
    hbm = pl.BlockSpec(memory_space=pl.ANY)

    def body(*refs):
        a_refs, w_refs = refs[:npieces], refs[npieces:2 * npieces]
        rest = refs[2 * npieces:]
        o_ref = rest[na]
        exchange = _DirectExchange("scatter", rest[:na], rest[na + 1:2 * na + 1], *rest[2 * na + 1:])
        i = pl.program_id(0)

        @pl.when(i == 0)
        def _():
            exchange.start()

        acc = _dot(a_refs[0][...].astype(MXU_DTYPE), w_refs[0][...])
        for a_ref, w_ref in zip(a_refs[1:], w_refs[1:]):
            acc = acc + _dot(a_ref[...].astype(MXU_DTYPE), w_ref[...])
        o_ref[...] = acc

        @pl.when(i == gm - 1)
        def _():
            exchange.wait()

    a_specs = [pl.BlockSpec((tm, p.shape[1]), lambda i: (i, 0)) for p in pieces]
    w_specs = [pl.BlockSpec((p.shape[1], n), functools.partial(lambda i, blk: (blk, 0), blk=off // p.shape[1]))
               for p, off in zip(pieces, offsets)]
    return pl.pallas_call(
        body, name=name, grid=(gm,),
        in_specs=a_specs + w_specs + [hbm] * na,
        out_specs=[pl.BlockSpec((tm, n), lambda i: (i, 0))] + [hbm] * na,
        out_shape=[jax.ShapeDtypeStruct((m, n), F32)] + [jax.ShapeDtypeStruct(blk.shape, blk.dtype) for blk in blocks],
        scratch_shapes=_DirectExchange.scratch(na),
    )(*pieces, *([w] * npieces), *blocks)


def _matmul_nt(a, bt, name, out_dtype=F32):
    m, k = a.shape
    n, _ = bt.shape
    tm, tn, tk = _pick(m, 1024, 16), _pick(n, 1408, 128), _pick(k, 1408, 128)
    nk = k // tk

    def body(a_ref, b_ref, o_ref, acc_ref):
        kk = pl.program_id(2)

        @pl.when(kk == 0)
        def _():
            acc_ref[...] = jnp.zeros_like(acc_ref)

        acc_ref[...] += _dot_nt(a_ref[...].astype(MXU_DTYPE), b_ref[...].astype(MXU_DTYPE))

        @pl.when(kk == nk - 1)
        def _():
            o_ref[...] = acc_ref[...].astype(o_ref.dtype)

    return pl.pallas_call(
        body, name=name, grid=(m // tm, n // tn, nk),
        in_specs=[pl.BlockSpec((tm, tk), lambda i, j, kk: (i, kk)), pl.BlockSpec((tn, tk), lambda i, j, kk: (j, kk))],
        out_specs=pl.BlockSpec((tm, tn), lambda i, j, kk: (i, j)),
        out_shape=jax.ShapeDtypeStruct((m, n), out_dtype),
        scratch_shapes=[pltpu.VMEM((tm, tn), F32)],
    )(a, bt)


def _matmul_sum2(a1, b1, a2, b2, name):
    m, k = a1.shape
    _, n = b1.shape
    tm, tn, tk = _pick(m, 1024, 16), _pick(n, 1408, 128), _pick(k, 1408, 128)
    nk = k // tk

    def body(a1_ref, b1_ref, a2_ref, b2_ref, o_ref):
        @pl.when(pl.program_id(2) == 0)
        def _():
            o_ref[...] = jnp.zeros_like(o_ref)

        o_ref[...] += (_dot(a1_ref[...].astype(MXU_DTYPE), b1_ref[...].astype(MXU_DTYPE))
                       + _dot(a2_ref[...].astype(MXU_DTYPE), b2_ref[...].astype(MXU_DTYPE)))

    a_spec = pl.BlockSpec((tm, tk), lambda i, j, kk: (i, kk))
    b_spec = pl.BlockSpec((tk, tn), lambda i, j, kk: (kk, j))
    return pl.pallas_call(
        body, name=name, grid=(m // tm, n // tn, nk), in_specs=[a_spec, b_spec, a_spec, b_spec],
        out_specs=pl.BlockSpec((tm, tn), lambda i, j, kk: (i, j)),
        out_shape=jax.ShapeDtypeStruct((m, n), F32),
    )(a1, b1, a2, b2)


def _matmul_tn(a, b, name, out_dtype=F32):
    l, m = a.shape
    _, n = b.shape
    tm, tn, tl = _pick(m, 1408, 128), _pick(n, 1408, 128), _pick(l, 512, 16)
    nl = l // tl

    def body(a_ref, b_ref, o_ref, acc_ref):
        ll = pl.program_id(2)

        @pl.when(ll == 0)
        def _():
            acc_ref[...] = jnp.zeros_like(acc_ref)

        acc_ref[...] += _dot_tn(a_ref[...].astype(MXU_DTYPE), b_ref[...].astype(MXU_DTYPE))

        @pl.when(ll == nl - 1)
        def _():
            o_ref[...] = acc_ref[...].astype(o_ref.dtype)

    return pl.pallas_call(
        body, name=name, grid=(m // tm, n // tn, nl),
        in_specs=[pl.BlockSpec((tl, tm), lambda i, j, ll: (ll, i)), pl.BlockSpec((tl, tn), lambda i, j, ll: (ll, j))],
        out_specs=pl.BlockSpec((tm, tn), lambda i, j, ll: (i, j)),
        out_shape=jax.ShapeDtypeStruct((m, n), out_dtype),
        scratch_shapes=[pltpu.VMEM((tm, tn), F32)],
    )(a, b)


def _row_specs(l, d, n_rows, n_vecs):
    tl = _pick(l, 512, 16)
    row = pl.BlockSpec((tl, d), lambda i: (i, 0))
    vec = pl.BlockSpec((1, d), lambda i: (0, 0))
    return tl, row, vec, [row] * n_rows + [vec] * n_vecs


def _rms_mod(x, nw, sc, sh, name):
    l, d = x.shape
    tl, row, _, in_specs = _row_specs(l, d, 1, 3)

    def body(x_ref, nw_ref, sc_ref, sh_ref, h_ref):
        xv = x_ref[...]
        r = lax.rsqrt(jnp.mean(xv * xv, axis=-1, keepdims=True) + EPS)
        h_ref[...] = (((xv * r) * nw_ref[...]) * (1.0 + sc_ref[...]) + sh_ref[...]).astype(h_ref.dtype)

    return pl.pallas_call(body, name=name, grid=(l // tl,), in_specs=in_specs, out_specs=row,
                          out_shape=jax.ShapeDtypeStruct((l, d), MXU_DTYPE))(x, nw, sc, sh)


def _residual_rms_mod(x, mix, g, nw, sc, sh, name):
    l, d = x.shape
    tl, row, _, in_specs = _row_specs(l, d, 2, 4)

    def body(x_ref, mix_ref, g_ref, nw_ref, sc_ref, sh_ref, x1_ref, h_ref):
        xv = x_ref[...] + g_ref[...] * mix_ref[...]
        x1_ref[...] = xv
        r = lax.rsqrt(jnp.mean(xv * xv, axis=-1, keepdims=True) + EPS)
        h_ref[...] = (((xv * r) * nw_ref[...]) * (1.0 + sc_ref[...]) + sh_ref[...]).astype(h_ref.dtype)

    return pl.pallas_call(body, name=name, grid=(l // tl,), in_specs=in_specs, out_specs=[row, row],
                          out_shape=[jax.ShapeDtypeStruct((l, d), F32), jax.ShapeDtypeStruct((l, d), MXU_DTYPE)],
                          )(x, mix, g, nw, sc, sh)


def _norm_bwd(x, dh, dres, nw, sc, name, gated=None):
    l, d = x.shape
    tl, row, vec, in_specs = _row_specs(l, d, 3, 2)
    stats = pl.BlockSpec((8, d), lambda i: (0, 0))

    def body(x_ref, dh_ref, dres_ref, nw_ref, sc_ref, *rest):
        dx_ref, st_ref = rest[-2:] if gated is None else (rest[2], rest[4])
        xv, dh_v = x_ref[...], dh_ref[...]
        r = lax.rsqrt(jnp.mean(xv * xv, axis=-1, keepdims=True) + EPS)
        xn = xv * r
        dxn = dh_v * (nw_ref[...] * (1.0 + sc_ref[...]))
        dx = dres_ref[...] + r * (dxn - xn * jnp.mean(dxn * xn, axis=-1, keepdims=True))
        dx_ref[...] = dx

        @pl.when(pl.program_id(0) == 0)
        def _():
            st_ref[...] = jnp.zeros_like(st_ref)

        dhx = dh_v * xn
        st_ref[0:1, :] += _colsum(dhx * nw_ref[...])
        st_ref[1:2, :] += _colsum(dh_v)
        st_ref[2:3, :] += _colsum(dhx * (1.0 + sc_ref[...]))
        if gated is not None:
            val_ref, g_ref, dval_ref = rest[0], rest[1], rest[3]
            dval_ref[...] = (g_ref[...] * dx).astype(dval_ref.dtype)
            st_ref[3:4, :] += _colsum(dx * val_ref[...])

    if gated is None:
        return pl.pallas_call(body, name=name, grid=(l // tl,), in_specs=in_specs, out_specs=[row, stats],
                              out_shape=[jax.ShapeDtypeStruct((l, d), F32), jax.ShapeDtypeStruct((8, d), F32)],
                              )(x, dh, dres, nw, sc)
    return pl.pallas_call(body, name=name, grid=(l // tl,), in_specs=in_specs + [row, vec], out_specs=[row, row, stats],
                          out_shape=[jax.ShapeDtypeStruct((l, d), F32), jax.ShapeDtypeStruct((l, d), MXU_DTYPE),
                                     jax.ShapeDtypeStruct((8, d), F32)],
                          )(x, dh, dres, nw, sc, *gated)


def _loss_head(x1, f, g, target, name):
    l, d = x1.shape
    tl, row, _, _ = _row_specs(l, d, 0, 0)
    vec = pl.BlockSpec((1, d), lambda i: (0, 0))

    def body(x1_ref, f_ref, g_ref, t_ref, dy_ref, df_ref, st_ref, loss_ref):
        fv = f_ref[...]
        e = x1_ref[...] + g_ref[...] * fv - t_ref[...]
        dy = e * (1.0 / d)
        dy_ref[...] = dy
        df_ref[...] = (g_ref[...] * dy).astype(df_ref.dtype)

        @pl.when(pl.program_id(0) == 0)
        def _():
            loss_ref[...] = jnp.zeros_like(loss_ref)
            st_ref[...] = jnp.zeros_like(st_ref)

        st_ref[0:1, :] += _colsum(dy * fv)
        s = jnp.sum(jnp.sum(e * e, axis=1, keepdims=True), axis=0, keepdims=True)
        loss_ref[...] += (0.5 / d) * s

    return pl.pallas_call(body, name=name, grid=(l // tl,), in_specs=[row, row, vec, row],
                          out_specs=[row, row, pl.BlockSpec((8, d), lambda i: (0, 0)), pl.BlockSpec((8, 128), lambda i: (0, 0))],
                          out_shape=[jax.ShapeDtypeStruct((l, d), F32), jax.ShapeDtypeStruct((l, d), MXU_DTYPE),
                                     jax.ShapeDtypeStruct((8, d), F32), jax.ShapeDtypeStruct((8, 128), F32)],
                          )(x1, f, g, target)


def _ffn_in(h, w_gate_t, w_up_t, name):
    l, k = h.shape
    f = w_gate_t.shape[0]
    tl, tf = _pick(l, 512, 16), _pick(f, 1408, 128)
    h_spec = pl.BlockSpec((tl, k), lambda j, i: (i, 0))
    w_spec = pl.BlockSpec((tf, k), lambda j, i: (j, 0))
    spec = pl.BlockSpec((tl, tf), lambda j, i: (i, j))

    def body(h_ref, wg_ref, wu_ref, g_ref, u_ref, a_ref):
        hv = h_ref[...].astype(MXU_DTYPE)
        gv = _dot_nt(hv, wg_ref[...].astype(MXU_DTYPE))
        uv = _dot_nt(hv, wu_ref[...].astype(MXU_DTYPE))
        g_ref[...] = gv
        u_ref[...] = uv
        a_ref[...] = (gv * _sigmoid(gv) * uv).astype(a_ref.dtype)

    return pl.pallas_call(body, name=name, grid=(f // tf, l // tl), in_specs=[h_spec, w_spec, w_spec], out_specs=[spec, spec, spec],
                          out_shape=[jax.ShapeDtypeStruct((l, f), F32)] * 2 + [jax.ShapeDtypeStruct((l, f), MXU_DTYPE)],
                          )(h, w_gate_t, w_up_t)


def _ffn_down_bwd(dffn, w_down, gate, up, name):
    l, f = gate.shape
    k = dffn.shape[1]
    tl, tf = _pick(l, 512, 16), _pick(f, 1408, 128)
    spec = pl.BlockSpec((tl, tf), lambda i, j: (i, j))

    def body(d_ref, w_ref, g_ref, u_ref, dg_ref, du_ref):
        dav = _dot_nt(d_ref[...].astype(MXU_DTYPE), w_ref[...].astype(MXU_DTYPE))
        gv = g_ref[...]
        s = _sigmoid(gv)
        dg_ref[...] = (dav * u_ref[...] * (s * (1.0 + gv * (1.0 - s)))).astype(dg_ref.dtype)
        du_ref[...] = (dav * (gv * s)).astype(du_ref.dtype)

    return pl.pallas_call(body, name=name, grid=(l // tl, f // tf),
                          in_specs=[pl.BlockSpec((tl, k), lambda i, j: (i, 0)), pl.BlockSpec((tf, k), lambda i, j: (j, 0)), spec, spec],
                          out_specs=[spec, spec], out_shape=[jax.ShapeDtypeStruct((l, f), MXU_DTYPE)] * 2)(dffn, w_down, gate, up)


def _conv_tile(l):
    return _pick(l, 512, 16)


def _conv_pre(buf, w_ref, b_ref, tl):
    acc = b_ref[...] + w_ref[3:4, :] * buf[HALO:HALO + tl, :]
    for k in range(CONV_WIDTH - 1):
        s = HALO - (CONV_WIDTH - 1) + k
        acc = acc + w_ref[k:k + 1, :] * buf[s:s + tl, :]
    return acc


def _fill_past(buf, u_ref, halo_ref, tl):
    i = pl.program_id(0)

    @pl.when(i == 0)
    def _():
        buf[0:HALO, :] = jnp.zeros((HALO, buf.shape[1]), F32)

    @pl.when(i > 0)
    def _():
        buf[0:HALO, :] = halo_ref[...]

    buf[HALO:HALO + tl, :] = u_ref[...]


def _conv_silu(u, w, b, name):
    l, ch = u.shape[0], w.shape[1]
    tl = _conv_tile(l)
    per = tl // HALO
    cur = pl.BlockSpec((tl, ch), lambda i: (i, 0))
    past = pl.BlockSpec((HALO, ch), lambda i: (jnp.maximum(i * per - 1, 0), 0))

    def body(u_ref, halo_ref, w_ref, b_ref, o_ref, buf):
        _fill_past(buf, u_ref, halo_ref, tl)
        pre = _conv_pre(buf, w_ref, b_ref, tl)
        o_ref[...] = pre * _sigmoid(pre)

    return pl.pallas_call(body, name=name, grid=(l // tl,),
                          in_specs=[cur, past, pl.BlockSpec((CONV_WIDTH, ch), lambda i: (0, 0)), pl.BlockSpec((1, ch), lambda i: (0, 0))],
                          out_specs=cur, out_shape=jax.ShapeDtypeStruct((l, ch), F32),
                          scratch_shapes=[pltpu.VMEM((tl + HALO, ch), F32)])(u, u, w, b)


def _conv_silu_bwd_pre(u, dxc, w, b, name):
    l, ch = u.shape[0], w.shape[1]
    tl = _conv_tile(l)
    per = tl // HALO
    cur = pl.BlockSpec((tl, ch), lambda i: (i, 0))
    past = pl.BlockSpec((HALO, ch), lambda i: (jnp.maximum(i * per - 1, 0), 0))

    def body(u_ref, halo_ref, d_ref, w_ref, b_ref, dpre_ref, st_ref, buf):
        _fill_past(buf, u_ref, halo_ref, tl)
        pre = _conv_pre(buf, w_ref, b_ref, tl)
        s = _sigmoid(pre)
        dpre = d_ref[...] * (s * (1.0 + pre * (1.0 - s)))
        dpre_ref[...] = dpre

        @pl.when(pl.program_id(0) == 0)
        def _():
            st_ref[...] = jnp.zeros_like(st_ref)

        for k in range(CONV_WIDTH):
            s0 = HALO - (CONV_WIDTH - 1) + k
            st_ref[k:k + 1, :] += _colsum(dpre * buf[s0:s0 + tl, :])
        st_ref[CONV_WIDTH:CONV_WIDTH + 1, :] += _colsum(dpre)

    return pl.pallas_call(body, name=name, grid=(l // tl,),
                          in_specs=[cur, past, cur, pl.BlockSpec((CONV_WIDTH, ch), lambda i: (0, 0)), pl.BlockSpec((1, ch), lambda i: (0, 0))],
                          out_specs=[cur, pl.BlockSpec((8, ch), lambda i: (0, 0))],
                          out_shape=[jax.ShapeDtypeStruct((l, ch), F32), jax.ShapeDtypeStruct((8, ch), F32)],
                          scratch_shapes=[pltpu.VMEM((tl + HALO, ch), F32)])(u, u, dxc, w, b)


def _conv_bwd_input(dpre, w, name):
    l, ch = dpre.shape
    tl = _conv_tile(l)
    per = tl // HALO
    nt = l // tl
    cur = pl.BlockSpec((tl, ch), lambda i: (i, 0))
    nxt = pl.BlockSpec((HALO, ch), lambda i: (jnp.minimum((i + 1) * per, l // HALO - 1), 0))

    def body(d_ref, halo_ref, w_ref, du_ref, buf):
        i = pl.program_id(0)
        buf[0:tl, :] = d_ref[...]

        @pl.when(i == nt - 1)
        def _():
            buf[tl:tl + HALO, :] = jnp.zeros((HALO, ch), F32)

        @pl.when(i < nt - 1)
        def _():
            buf[tl:tl + HALO, :] = halo_ref[...]

        acc = w_ref[3:4, :] * buf[0:tl, :]
        for k in range(CONV_WIDTH - 1):
            s = CONV_WIDTH - 1 - k
            acc = acc + w_ref[k:k + 1, :] * buf[s:s + tl, :]
        du_ref[...] = acc

    return pl.pallas_call(body, name=name, grid=(nt,),
                          in_specs=[cur, nxt, pl.BlockSpec((CONV_WIDTH, ch), lambda i: (0, 0))],
                          out_specs=cur, out_shape=jax.ShapeDtypeStruct((l, ch), F32),
                          scratch_shapes=[pltpu.VMEM((tl + HALO, ch), F32)])(dpre, dpre, w)


HEAD_TILE = 128


def _chunk_iota():
    r = lax.broadcasted_iota(jnp.int32, (CHUNK, CHUNK), 0)
    c = lax.broadcasted_iota(jnp.int32, (CHUNK, CHUNK), 1)
    return r, c


def _split3_dot(v, ones_b):
    hi = v.astype(BF16)
    r1 = v - hi.astype(F32)
    mid = r1.astype(BF16)
    lo = (r1 - mid.astype(F32)).astype(BF16)
    return _dot(hi, ones_b) + _dot(mid, ones_b) + _dot(lo, ones_b)


def _split3_dot_nt(v, ones_b):
    hi = v.astype(BF16)
    r1 = v - hi.astype(F32)
    mid = r1.astype(BF16)
    lo = (r1 - mid.astype(F32)).astype(BF16)
    return _dot_nt(hi, ones_b) + _dot_nt(mid, ones_b) + _dot_nt(lo, ones_b)


def _ssd_decays(dt_ref, sx_ref, dtbx_ref, alx_ref, r, c):
    tri = (r >= c).astype(BF16)
    raw_x = _split3_dot(dt_ref[...], sx_ref[...]) + dtbx_ref[...]
    dt_x = _softplus(raw_x)
    a_x = -jnp.exp(alx_ref[...])
    adt_x = a_x * dt_x
    acs_x = _tri_dot(tri, adt_x)
    atot_x = _colsum(adt_x)
    return raw_x, dt_x, a_x, acs_x, jnp.exp(acs_x), jnp.exp(atot_x - acs_x), jnp.exp(atot_x)


def _head_decay(acs_x, e, r, c, halves):
    tile = acs_x[:, (e // 2) * 2 * HEAD_DIM:(e // 2 + 1) * 2 * HEAD_DIM]
    col = jnp.broadcast_to(jnp.max(jnp.where(halves[e % 2], tile, -jnp.inf), axis=1, keepdims=True), (CHUNK, CHUNK))
    return jnp.where(r >= c, jnp.exp(col - col.T), 0.0)


def _half_masks():
    lane = lax.broadcasted_iota(jnp.int32, (CHUNK, 2 * HEAD_DIM), 1)
    return lane < HEAD_DIM, lane >= HEAD_DIM


def _ssd_specs(nc, reverse):
    def at(i):
        return nc - 1 - i if reverse else i
    xc = pl.BlockSpec((CHUNK, D_CONV), lambda i: (at(i), 0))
    wide = pl.BlockSpec((CHUNK, D_SSD), lambda i: (at(i), 0))
    dt_tile = pl.BlockSpec((CHUNK, HEAD_TILE), lambda i: (at(i), O_DT // HEAD_TILE))
    dt_out = pl.BlockSpec((CHUNK, HEAD_TILE), lambda i: (at(i), 0))
    spread = pl.BlockSpec((HEAD_TILE, D_SSD), lambda i: (0, 0))
    vec = pl.BlockSpec((1, D_SSD), lambda i: (0, 0))
    state = pl.BlockSpec((1, SSD_STATE, D_SSD), lambda i: (at(i), 0, 0))
    return xc, wide, dt_tile, dt_out, spread, vec, state


def _ssd_fwd(xc, proj, spread_x, dtb_x, al_x, name):
    l = xc.shape[0]
    nc = l // CHUNK
    xc_s, wide_s, dt_s, _, spread_s, vec_s, state_s = _ssd_specs(nc, False)
    pairs_per_group = GROUP_WIDTH // (2 * HEAD_DIM)

    def body(xc_ref, dt_ref, sx_ref, dtbx_ref, alx_ref, y_ref, sp_ref, state):
        @pl.when(pl.program_id(0) == 0)
        def _():
            state[...] = jnp.zeros_like(state)

        r, c = _chunk_iota()
        halves = _half_masks()
        _, dt_x, _, acs_x, ea_x, ds_x, eatot_x = _ssd_decays(dt_ref, sx_ref, dtbx_ref, alx_ref, r, c)
        xg = xc_ref[:, 0:D_SSD] * dt_x
        sp_ref[0] = state[...]
        for g in range(SSD_GROUPS):
            lanes = slice(g * GROUP_WIDTH, (g + 1) * GROUP_WIDTH)
            bb = xc_ref[:, D_SSD + g * SSD_STATE:D_SSD + (g + 1) * SSD_STATE].astype(MXU_DTYPE)
            cb = xc_ref[:, D_SSD + (SSD_GROUPS + g) * SSD_STATE:D_SSD + (SSD_GROUPS + g + 1) * SSD_STATE].astype(MXU_DTYPE)
            scores = _dot_nt(cb, bb)
            sg = state[:, lanes]
            ys = []
            for j in range(g * pairs_per_group, (g + 1) * pairs_per_group):
                xg_pair = xg[:, j * 2 * HEAD_DIM:(j + 1) * 2 * HEAD_DIM]
                acc = jnp.zeros((CHUNK, 2 * HEAD_DIM), F32)
                for half in range(2):
                    m = (scores * _head_decay(acs_x, 2 * j + half, r, c, halves)).astype(MXU_DTYPE)
                    acc = acc + _dot(m, jnp.where(halves[half], xg_pair, 0.0).astype(MXU_DTYPE))
                ys.append(acc)
            y_ref[:, lanes] = jnp.concatenate(ys, axis=1) + _dot(cb, sg.astype(MXU_DTYPE)) * ea_x[:, lanes]
            state[:, lanes] = sg * eatot_x[:, lanes] + _dot_tn(bb, (xg[:, lanes] * ds_x[:, lanes]).astype(MXU_DTYPE))

    return pl.pallas_call(
        body, name=name, grid=(nc,),
        in_specs=[xc_s, dt_s, spread_s, vec_s, vec_s],
        out_specs=[wide_s, state_s],
        out_shape=[jax.ShapeDtypeStruct((l, D_SSD), F32), jax.ShapeDtypeStruct((nc, SSD_STATE, D_SSD), F32)],
        scratch_shapes=[pltpu.VMEM((SSD_STATE, D_SSD), F32)],
    )(xc, proj, spread_x, dtb_x, al_x)


def _ssd_bwd(xc, proj, spread_x, dtb_x, al_x, dy, states, name):
    l = xc.shape[0]
    nc = l // CHUNK
    xc_s, wide_s, dt_s, dt_out_s, spread_s, vec_s, state_s = _ssd_specs(nc, True)
    pairs_per_group = GROUP_WIDTH // (2 * HEAD_DIM)

    def body(xc_ref, dt_ref, sx_ref, dtbx_ref, alx_ref, dy_ref, sp_ref,
             dxc_ref, ddtr_ref, st_ref, hs_ref, dstate):
        @pl.when(pl.program_id(0) == 0)
        def _():
            dstate[...] = jnp.zeros_like(dstate)
            st_ref[...] = jnp.zeros_like(st_ref)

        r, c = _chunk_iota()
        halves = _half_masks()
        spread = sx_ref[...]

        def head_sums(v):
            return _split3_dot(_split3_dot_nt(v, spread), spread)

        raw_x, dt_x, a_x, acs_x, ea_x, ds_x, eatot_x = _ssd_decays(dt_ref, sx_ref, dtbx_ref, alx_ref, r, c)
        xs = xc_ref[:, 0:D_SSD]
        xg = xs * dt_x
        gy = dy_ref[...]
        s_prev = sp_ref[0]
        gea = gy * ea_x
        xds = xg * ds_x
        ds_old = dstate[...]
        later2 = jnp.concatenate([(r <= c).astype(BF16)] * 2, axis=1)
        dxg_parts, state_term, yoff_parts, dadt_parts = [], [], [], []
        for g in range(SSD_GROUPS):
            lanes = slice(g * GROUP_WIDTH, (g + 1) * GROUP_WIDTH)
            b_lo = D_SSD + g * SSD_STATE
            c_lo = D_SSD + (SSD_GROUPS + g) * SSD_STATE
            bb = xc_ref[:, b_lo:b_lo + SSD_STATE].astype(MXU_DTYPE)
            cb = xc_ref[:, c_lo:c_lo + SSD_STATE].astype(MXU_DTYPE)
            scores = _dot_nt(cb, bb)
            dsg = ds_old[:, lanes].astype(MXU_DTYPE)
            gea_b = gea[:, lanes].astype(MXU_DTYPE)
            xds_b = xds[:, lanes].astype(MXU_DTYPE)
            dxg_state = _dot(bb, dsg) * ds_x[:, lanes]
            dc = _dot_nt(gea_b, s_prev[:, lanes].astype(MXU_DTYPE))
            db = _dot_nt(xds_b, dsg)
            dscores = jnp.zeros((CHUNK, CHUNK), F32)
            diag = []
            for j in range(g * pairs_per_group, (g + 1) * pairs_per_group):
                pair = slice(j * 2 * HEAD_DIM, (j + 1) * 2 * HEAD_DIM)
                xg_pair = xg[:, pair].astype(MXU_DTYPE)
                acc = jnp.zeros((CHUNK, 2 * HEAD_DIM), F32)
                cols = []
                for half in range(2):
                    decay = _head_decay(acs_x, 2 * j + half, r, c, halves)
                    g_e = jnp.where(halves[half], gy[:, pair], 0.0).astype(MXU_DTYPE)
                    acc = acc + _dot_tn((scores * decay).astype(MXU_DTYPE), g_e)
                    dm = _dot_nt(g_e, xg_pair) * decay
                    dscores = dscores + dm
                    wq = dm * scores
                    hi = wq.astype(BF16)
                    lo = (wq - hi.astype(F32)).astype(BF16)
                    later = _dot(later2, jnp.concatenate([hi, lo], axis=0))
                    cols.append(jnp.sum(jnp.where(c < r, later, 0.0), axis=1, keepdims=True))
                diag.append(acc)
                dadt_parts.append(jnp.where(halves[0], cols[0], cols[1]))
            dsc_b = dscores.astype(MXU_DTYPE)
            dc = dc + _dot(dsc_b, bb)
            db = db + _dot_tn(dsc_b, cb)
            dxc_ref[:, b_lo:b_lo + SSD_STATE] = db
            dxc_ref[:, c_lo:c_lo + SSD_STATE] = dc
            dxg_parts.append(jnp.concatenate(diag, axis=1) + dxg_state)
            state_term.append(dxg_state)
            yoff_parts.append(_dot(cb, s_prev[:, lanes].astype(MXU_DTYPE)) * ea_x[:, lanes])
            dstate[:, lanes] = ds_old[:, lanes] * eatot_x[:, lanes] + _dot_tn(cb, gea_b)
        dxg = jnp.concatenate(dxg_parts, axis=1)
        dxc_ref[:, 0:D_SSD] = dxg * dt_x
        through_out = _tri_dot((r <= c).astype(BF16), gy * jnp.concatenate(yoff_parts, axis=1))
        through_in = _tri_dot((c < r).astype(BF16), xg * jnp.concatenate(state_term, axis=1))
        carried = jnp.broadcast_to(_colsum(ds_old * s_prev) * eatot_x, (8, D_SSD))
        dadt = (jnp.concatenate(dadt_parts, axis=1) + head_sums(through_out + through_in)
                + jnp.max(head_sums(carried), axis=0, keepdims=True))
        ddt = a_x * dadt + head_sums(dxg * xs)
        draw = ddt * _sigmoid(raw_x)
        ddtr_ref[...] = _split3_dot_nt(draw, spread) * (1.0 / HEAD_DIM)
        st_ref[0:1, :] += _colsum(dt_x * dadt) * a_x
        st_ref[1:2, :] += _colsum(draw)
        st_ref[2:3, :] += _colsum(gy * xs)

        @pl.when(pl.program_id(0) == nc - 1)
        def _():
            hs_ref[...] = head_sums(st_ref[...])

    stats = pl.BlockSpec((8, D_SSD), lambda i: (0, 0))
    return pl.pallas_call(
        body, name=name, grid=(nc,),
        in_specs=[xc_s, dt_s, spread_s, vec_s, vec_s, wide_s, state_s],
        out_specs=[xc_s, dt_out_s, stats, stats],
        out_shape=[jax.ShapeDtypeStruct((l, D_CONV), F32), jax.ShapeDtypeStruct((l, HEAD_TILE), F32),
                   jax.ShapeDtypeStruct((8, D_SSD), F32), jax.ShapeDtypeStruct((8, D_SSD), F32)],
        scratch_shapes=[pltpu.VMEM((SSD_STATE, D_SSD), F32)],
    )(xc, proj, spread_x, dtb_x, al_x, dy, states)


def _ssd_gate_norm(ycore, xc, z, dskip_x, norm_w, name):
    l = ycore.shape[0]
    tl = _pick(l, 512, 16)
    row = pl.BlockSpec((tl, D_SSD), lambda i: (i, 0))
    vec = pl.BlockSpec((1, D_SSD), lambda i: (0, 0))

    def body(y_ref, xs_ref, z_ref, dk_ref, nw_ref, o_ref):
        zv = z_ref[...]
        yv = (y_ref[...] + dk_ref[...] * xs_ref[...]) * (zv * _sigmoid(zv))
        for g in range(SSD_GROUPS):
            lanes = slice(g * GROUP_WIDTH, (g + 1) * GROUP_WIDTH)
            yg = yv[:, lanes]
            rg = lax.rsqrt(jnp.mean(yg * yg, axis=-1, keepdims=True) + EPS)
            o_ref[:, lanes] = (yg * rg * nw_ref[:, lanes]).astype(o_ref.dtype)

    z_spec = pl.BlockSpec((tl, D_SSD), lambda i: (i, O_Z // D_SSD))
    return pl.pallas_call(body, name=name, grid=(l // tl,), in_specs=[row, row, z_spec, vec, vec], out_specs=row,
                          out_shape=jax.ShapeDtypeStruct((l, D_SSD), MXU_DTYPE))(ycore, xc, z, dskip_x, norm_w)


def _ssd_gate_norm_bwd(dout, ycore, xc, z, dskip_x, norm_w, name):
    l = ycore.shape[0]
    tl = _pick(l, 512, 16)
    nt = l // tl
    row = pl.BlockSpec((tl, D_SSD), lambda i: (i, 0))
    vec = pl.BlockSpec((1, D_SSD), lambda i: (0, 0))

    def body(do_ref, y_ref, xs_ref, z_ref, dk_ref, nw_ref, dyc_ref, dxs_ref, dz_ref, st_ref):
        @pl.when(pl.program_id(0) == 0)
        def _():
            st_ref[...] = jnp.zeros_like(st_ref)

        zv, xs = z_ref[...], xs_ref[...]
        s = _sigmoid(zv)
        gz = zv * s
        yc = y_ref[...] + dk_ref[...] * xs
        yv = yc * gz
        dov = do_ref[...]
        dnw, dyv = [], []
        for g in range(SSD_GROUPS):
            lanes = slice(g * GROUP_WIDTH, (g + 1) * GROUP_WIDTH)
            yg = yv[:, lanes]
            rg = lax.rsqrt(jnp.mean(yg * yg, axis=-1, keepdims=True) + EPS)
            yn = yg * rg
            dnw.append(_colsum(dov[:, lanes] * yn))
            dyn = dov[:, lanes] * nw_ref[:, lanes]
            dyv.append(rg * (dyn - yn * jnp.mean(dyn * yn, axis=-1, keepdims=True)))
        dy = jnp.concatenate(dyv, axis=1)
        dyc = dy * gz
        dyc_ref[...] = dyc
        dxs_ref[...] = dyc * dk_ref[...]
        dz_ref[...] = dy * yc * (s * (1.0 + zv * (1.0 - s)))
        st_ref[0:1, :] += jnp.concatenate(dnw, axis=1)

    return pl.pallas_call(
        body, name=name, grid=(nt,), in_specs=[row, row, row, pl.BlockSpec((tl, D_SSD), lambda i: (i, O_Z // D_SSD)), vec, vec],
        out_specs=[row, row, row, pl.BlockSpec((8, D_SSD), lambda i: (0, 0))],
        out_shape=[jax.ShapeDtypeStruct((l, D_SSD), F32)] * 3 + [jax.ShapeDtypeStruct((8, D_SSD), F32)],
    )(dout, ycore, xc, z, dskip_x, norm_w)


PAIR = 2 * HEAD_DIM
N_PAIRS = N_HEADS // 2
Q_TILE0 = O_Q // PAIR
SB_KEYS = 1024
SB_PAIRS = 4
SB_PAIRS_FWD = 4
SB_SCALE = HEAD_DIM ** -0.5


def _pair_sum(v, lo):
    s_lo = jnp.sum(jnp.where(lo, v, 0.0), axis=-1, keepdims=True)
    s_hi = jnp.sum(jnp.where(lo, 0.0, v), axis=-1, keepdims=True)
    return jnp.where(lo, s_lo, s_hi)


def _qkv_prep(proj, qw2, kw2, name):
    l = proj.shape[0]
    tl = _pick(l, 1024, 16)
    out = pl.BlockSpec((tl, PAIR), lambda i, j: (i, j))
    vec = pl.BlockSpec((1, PAIR), lambda i, j: (0, 0))

    def at(which):
        return pl.BlockSpec((tl, PAIR), lambda i, j: (i, Q_TILE0 + which * N_PAIRS + j))

    def body(q_ref, k_ref, v_ref, qw_ref, kw_ref, qn_ref, kn_ref, vb_ref):
        lo = lax.broadcasted_iota(jnp.int32, (tl, PAIR), 1) < HEAD_DIM
        for t_ref, w_ref, o_ref, scale in ((q_ref, qw_ref, qn_ref, SB_SCALE), (k_ref, kw_ref, kn_ref, 1.0)):
            tv = t_ref[...]
            r = lax.rsqrt(_pair_sum(tv * tv, lo) * (1.0 / HEAD_DIM) + EPS)
            o_ref[...] = ((tv * r) * w_ref[...] * scale).astype(o_ref.dtype)
        vb_ref[...] = v_ref[...].astype(vb_ref.dtype)

    return pl.pallas_call(body, name=name, grid=(l // tl, N_PAIRS), in_specs=[at(0), at(1), at(2), vec, vec],
                          out_specs=[out, out, out], out_shape=[jax.ShapeDtypeStruct((l, D_SB), MXU_DTYPE)] * 3,
                          )(proj, proj, proj, qw2, kw2)


def _qk_norm_bwd(proj, dqn, dkn, qw2, kw2, name):
    l = proj.shape[0]
    tl = _pick(l, 1024, 16)
    out = pl.BlockSpec((tl, PAIR), lambda i, j: (i, j))
    vec = pl.BlockSpec((1, PAIR), lambda i, j: (0, 0))
    stats = pl.BlockSpec((8, PAIR), lambda i, j: (0, 0))

    def at(which):
        return pl.BlockSpec((tl, PAIR), lambda i, j: (i, Q_TILE0 + which * N_PAIRS + j))

    def body(q_ref, k_ref, dqn_ref, dkn_ref, qw_ref, kw_ref, dq_ref, dk_ref, stq_ref, stk_ref):
        @pl.when((pl.program_id(0) == 0) & (pl.program_id(1) == 0))
        def _():
            stq_ref[...] = jnp.zeros_like(stq_ref)
            stk_ref[...] = jnp.zeros_like(stk_ref)

        lo = lax.broadcasted_iota(jnp.int32, (tl, PAIR), 1) < HEAD_DIM
        for t_ref, dn_ref, w_ref, d_ref, st_ref, scale in ((q_ref, dqn_ref, qw_ref, dq_ref, stq_ref, SB_SCALE),
                                                           (k_ref, dkn_ref, kw_ref, dk_ref, stk_ref, 1.0)):
            tv = t_ref[...]
            r = lax.rsqrt(_pair_sum(tv * tv, lo) * (1.0 / HEAD_DIM) + EPS)
            tn = tv * r
            dnv = dn_ref[...] * scale
            dtn = dnv * w_ref[...]
            d_ref[...] = r * (dtn - tn * (_pair_sum(dtn * tn, lo) * (1.0 / HEAD_DIM)))
            st_ref[0:1, :] += _colsum(dnv * tn)

    return pl.pallas_call(body, name=name, grid=(l // tl, N_PAIRS), in_specs=[at(0), at(1), out, out, vec, vec],
                          out_specs=[out, out, stats, stats],
                          out_shape=[jax.ShapeDtypeStruct((l, D_SB), F32)] * 2 + [jax.ShapeDtypeStruct((8, PAIR), F32)] * 2,
                          )(proj, proj, dqn, dkn, qw2, kw2)


def _sb_masks():
    r = lax.broadcasted_iota(jnp.int32, (CHUNK, CHUNK), 0)
    c = lax.broadcasted_iota(jnp.int32, (CHUNK, CHUNK), 1)
    return r, c


def _stack2(mask):
    t = mask.astype(BF16)
    return jnp.concatenate([t, t], axis=0)


def _sb_fwd(q, k, v, shards, name):
    l = q.shape[0]
    nq = l // CHUNK
    kt = _pick(l, SB_KEYS, CHUNK)
    sub = kt // CHUNK
    pairs = SB_PAIRS_FWD
    heads = 2 * pairs
    steps = N_PAIRS // pairs
    na = len(shards)
    qblk = pl.BlockSpec((CHUNK, pairs * PAIR), lambda i, j: (j, i))
    full = pl.BlockSpec((l, pairs * PAIR), lambda i, j: (0, i))
    hbm = pl.BlockSpec(memory_space=pl.ANY)

    def body(q_ref, k_ref, v_ref, *rest):
        o_ref, tot_ref = rest[na:na + 2]
        exchange = _DirectExchange("gather", rest[:na], rest[na + 2:2 * na + 2], *rest[2 * na + 2:])
        qb = pl.program_id(1)

        @pl.when((pl.program_id(0) == 0) & (qb == 0))
        def _():
            exchange.start()

        r, c = _sb_masks()
        after2 = _stack2(r > c)
        halves = _half_masks()
        zero = jnp.zeros((CHUNK, PAIR), q_ref.dtype)
        lanes = [slice((a // 2) * PAIR, (a // 2 + 1) * PAIR) for a in range(heads)]
        qm = [jnp.where(halves[a % 2], q_ref[:, lanes[a]], zero) for a in range(heads)]
        last = (qb * CHUNK) // kt

        def tile(t, width, carries, accs, masked):
            rows = pl.ds(pl.multiple_of(t * width, width), width)
            if masked:
                key_minus_query = (lax.broadcasted_iota(jnp.int32, (CHUNK, width), 1)
                                   - lax.broadcasted_iota(jnp.int32, (CHUNK, width), 0))
                keep = key_minus_query < qb * CHUNK - t * width

            def logits(a):
                lg = _dot_nt(qm[a], k_ref[rows, lanes[a]])
                sp = _softplus_logits(lg)
                return lg - sp, (jnp.where(keep, -sp, 0.0) if masked else -sp)

            def sums(a, lr):
                offset, parts = carries[a], [None] * (width // CHUNK)
                for j in reversed(range(width // CHUNK)):
                    piece = lr[:, j * CHUNK:(j + 1) * CHUNK]
                    parts[j] = _split_dot(piece, after2) + offset
                    offset = offset + jnp.sum(piece, axis=1, keepdims=True)
                return jnp.concatenate(parts, axis=1), offset

            def output(a, ls, cs):
                w = jnp.exp(ls + cs)
                if masked:
                    w = jnp.where(keep, w, 0.0)
                return accs[a] + _dot(w.astype(MXU_DTYPE), v_ref[rows, lanes[a]])

            new_carries, new_accs = [None] * heads, [None] * heads
            ls, lr = logits(0)
            for a in range(heads):
                cs, new_carries[a] = sums(a, lr)
                if a + 1 < heads:
                    ls_next, lr = logits(a + 1)
                new_accs[a] = output(a, ls, cs)
                ls = ls_next
            return tuple(new_carries), tuple(new_accs)

        carries = tuple(jnp.zeros((CHUNK, 1), F32) for _ in range(heads))
        accs = tuple(jnp.zeros((CHUNK, PAIR), F32) for _ in range(heads))
        if sub % 2 == 0:
            half = kt // 2
            carries, accs = lax.cond((qb * CHUNK) % kt >= half,
                                     lambda cr, ac: tile(2 * last + 1, half, cr, ac, True), lambda cr, ac: (cr, ac), carries, accs)
            carries, accs = tile(2 * last, half, carries, accs, True)
        else:
            carries, accs = tile(last, kt, carries, accs, True)
        carries, accs = lax.fori_loop(1, last + 1, lambda i, st: tile(last - i, kt, st[0], st[1], False), (carries, accs))
        for p in range(pairs):
            o_ref[:, lanes[2 * p]] = jnp.where(halves[0], accs[2 * p], accs[2 * p + 1])
            tot_ref[:, lanes[2 * p]] = jnp.where(halves[0], carries[2 * p], carries[2 * p + 1])

        @pl.when((pl.program_id(0) == steps - 1) & (qb == nq - 1))
        def _():
            exchange.wait()

    return pl.pallas_call(
        body, name=name, grid=(steps, nq), in_specs=[qblk, full, full] + [hbm] * na,
        out_specs=[qblk, qblk] + [hbm] * na,
        out_shape=[jax.ShapeDtypeStruct((l, D_SB), F32), jax.ShapeDtypeStruct((l, D_SB), F32)]
        + [jax.ShapeDtypeStruct((N_DEV,) + s.shape, s.dtype) for s in shards],
        scratch_shapes=_DirectExchange.scratch(na),
    )(q, k, v, *shards)


def _sb_bwd(q, k, v, dycat, tot, blocks, name):
    l = q.shape[0]
    nq = l // CHUNK
    kt = _pick(l, SB_KEYS, CHUNK)
    sub = kt // CHUNK
    heads = 2 * SB_PAIRS
    steps = N_PAIRS // SB_PAIRS
    na = len(blocks)
    width_all = SB_PAIRS * PAIR
    qblk = pl.BlockSpec((CHUNK, width_all), lambda i, j: (j, i))
    doblk = pl.BlockSpec((CHUNK, width_all), lambda i, j: (j, D_SSD // width_all + i))
    full = pl.BlockSpec((l, width_all), lambda i, j: (0, i))
    resident = pl.BlockSpec((l, width_all), lambda i, j: (0, i), pipeline_mode=pl.Buffered(1))
    hbm = pl.BlockSpec(memory_space=pl.ANY)

    def body(q_ref, k_ref, v_ref, do_ref, tot_ref, *rest):
        dq_ref, dk_ref, dv_ref = rest[na:na + 3]
        exchange = _DirectExchange("scatter", rest[:na], rest[na + 3:2 * na + 3], *rest[2 * na + 3:])
        qb = pl.program_id(1)

        @pl.when((pl.program_id(0) == 0) & (qb == 0))
        def _():
            exchange.start()

        @pl.when(qb == 0)
        def _():
            dk_ref[...] = jnp.zeros_like(dk_ref)
            dv_ref[...] = jnp.zeros_like(dv_ref)

        r, c = _sb_masks()
        upto1 = (r <= c).astype(BF16)
        upto2 = jnp.concatenate([upto1, upto1], axis=0)
        halves = _half_masks()
        lanes = [slice((a // 2) * PAIR, (a // 2 + 1) * PAIR) for a in range(heads)]
        qm = [jnp.where(halves[a % 2], q_ref[:, lanes[a]], jnp.zeros((CHUNK, PAIR), q_ref.dtype)) for a in range(heads)]
        dom = [jnp.where(halves[a % 2], do_ref[:, lanes[a]], 0.0).astype(MXU_DTYPE) for a in range(heads)]
        total = [jnp.max(jnp.where(halves[a % 2], tot_ref[:, lanes[a]], -jnp.inf), axis=1, keepdims=True) for a in range(heads)]
        last = (qb * CHUNK) // kt

        def prefix(values, offset, sign, exact=True):
            parts = []
            for j in range(values.shape[1] // CHUNK):
                piece = values[:, j * CHUNK:(j + 1) * CHUNK]
                within = _split_dot(piece, upto2) if exact else _dot(piece.astype(BF16), upto1)
                parts.append(offset + within if sign > 0 else offset - within)
                step = jnp.sum(piece, axis=1, keepdims=True)
                offset = offset + step if sign > 0 else offset - step
            return jnp.concatenate(parts, axis=1), offset

        def tile(t, width, carry_p, carry_d, dq, masked):
            rows = pl.ds(pl.multiple_of(t * width, width), width)
            if masked:
                key_minus_query = (lax.broadcasted_iota(jnp.int32, (CHUNK, width), 1)
                                   - lax.broadcasted_iota(jnp.int32, (CHUNK, width), 0))
                keep = key_minus_query < qb * CHUNK - t * width

            def log_terms(a):
                lg = _dot_nt(qm[a], k_ref[rows, lanes[a]])
                sp = _softplus_logits(lg)
                return lg - sp, (jnp.where(keep, -sp, 0.0) if masked else -sp)

            def weights(a, ls, lr):
                remaining, p_next = prefix(lr, carry_p[a], -1)
                w = jnp.exp(ls + remaining)
                if masked:
                    w = jnp.where(keep, w, 0.0)
                return w, p_next

            def weight_grads(a, w):
                return _dot_nt(dom[a], v_ref[rows, lanes[a]]) * w

            def gradients(a, ls, w, da):
                d_incl, d_next = prefix(da, carry_d[a], 1, exact=False)
                sig = jnp.exp(ls)
                dl = da - sig * d_incl
                if masked:
                    dl = jnp.where(keep, dl, 0.0)
                dl_b = dl.astype(MXU_DTYPE)
                return (d_next, dq[a] + _dot(dl_b, k_ref[rows, lanes[a]]),
                        _dot_tn(w.astype(MXU_DTYPE), dom[a]), _dot_tn(dl_b, qm[a]))

            new_p, new_d, new_dq = [None] * heads, [None] * heads, [None] * heads
            dv_upd, dk_upd = [None] * heads, [None] * heads
            ls, lr = log_terms(0)
            w, new_p[0] = weights(0, ls, lr)
            for a in range(heads):
                if a + 1 < heads:
                    ls_next, lr_next = log_terms(a + 1)
                da = weight_grads(a, w)
                if a + 1 < heads:
                    w_next, new_p[a + 1] = weights(a + 1, ls_next, lr_next)
                new_d[a], new_dq[a], dv_upd[a], dk_upd[a] = gradients(a, ls, w, da)
                if a % 2 == 1:
                    dv_ref[rows, lanes[a]] += dv_upd[a - 1] + dv_upd[a]
                    dk_ref[rows, lanes[a]] += dk_upd[a - 1] + dk_upd[a]
                if a + 1 < heads:
                    ls, w = ls_next, w_next
            return tuple(new_p), tuple(new_d), tuple(new_dq)

        zeros = tuple(jnp.zeros((CHUNK, 1), F32) for _ in range(heads))
        dq0 = tuple(jnp.zeros((CHUNK, PAIR), F32) for _ in range(heads))
        state = lax.fori_loop(0, last, lambda t, st: tile(t, kt, st[0], st[1], st[2], False), (tuple(total), zeros, dq0))
        if sub % 2 == 0:
            half = kt // 2
            state = tile(2 * last, half, *state, True)
            state = lax.cond((qb * CHUNK) % kt >= half,
                             lambda cp, cd, dq: tile(2 * last + 1, half, cp, cd, dq, True), lambda cp, cd, dq: (cp, cd, dq), *state)
        else:
            state = tile(last, kt, *state, True)
        for p in range(SB_PAIRS):
            dq_ref[:, lanes[2 * p]] = jnp.where(halves[0], state[2][2 * p], state[2][2 * p + 1])

        @pl.when((pl.program_id(0) == steps - 1) & (qb == nq - 1))
        def _():
            exchange.wait()

    return pl.pallas_call(
        body, name=name, grid=(steps, nq),
        in_specs=[qblk, full, full, doblk, qblk] + [hbm] * na,
        out_specs=[qblk, resident, resident] + [hbm] * na,
        out_shape=[jax.ShapeDtypeStruct((l, D_SB), F32)] * 3 + [jax.ShapeDtypeStruct(b.shape, b.dtype) for b in blocks],
        scratch_shapes=_DirectExchange.scratch(na),
    )(q, k, v, dycat, tot, *blocks)


def _ada_bwd(c_all, dmod_shard, name):
    def body(c_ref, d_ref, o_ref):
        cv = c_ref[...]
        o_ref[...] = _dot_tn(cv * _sigmoid(cv), d_ref[...], HIGHEST)

    return pl.pallas_call(body, name=name, out_shape=jax.ShapeDtypeStruct((c_all.shape[1], dmod_shard.shape[1]), F32))(c_all, dmod_shard)


def _sum_small(parts, name):
    def body(p_ref, o_ref):
        acc = p_ref[0]
        for d in range(1, N_DEV):
            acc = acc + p_ref[d]
        o_ref[...] = acc

    return pl.pallas_call(body, name=name, out_shape=jax.ShapeDtypeStruct(parts.shape[1:], F32))(parts)


def _adamw(w, g, m, v, name):
    rows, cols = w.shape
    tr = _pick(rows, 256, 8)
    spec = pl.BlockSpec((tr, cols), lambda i: (i, 0))
    bc1 = 1.0 - ADAM_B1 ** ADAM_STEP
    bc2 = 1.0 - ADAM_B2 ** ADAM_STEP

    def body(w_ref, g_ref, m_ref, v_ref, d_ref, nm_ref, nv_ref):
        gv = g_ref[...]
        nm = ADAM_B1 * m_ref[...] + (1.0 - ADAM_B1) * gv
        nv = ADAM_B2 * v_ref[...] + (1.0 - ADAM_B2) * (gv * gv)
        nm_ref[...] = nm
        nv_ref[...] = nv
        d_ref[...] = -ADAM_LR * ((nm / bc1) / (jnp.sqrt(nv / bc2) + ADAM_EPS) + ADAM_WD * w_ref[...])

    return pl.pallas_call(body, name=name, grid=(rows // tr,), in_specs=[spec] * 4, out_specs=[spec] * 3,
                          out_shape=[jax.ShapeDtypeStruct((rows, cols), F32)] * 3)(w, g, m, v)


def _pad_w_in_t(w_in_t):
    lo = D_SSD + D_CONV
    return jnp.concatenate([w_in_t[D_SSD:lo + N_HEADS], jnp.zeros((DT_PAD - N_HEADS, w_in_t.shape[1]), w_in_t.dtype),
                            w_in_t[:D_SSD], w_in_t[lo + N_HEADS:]], axis=0)


def _local_step(x, target, mod, norm1_w, w_in_tp, conv_w, conv_b, dt_bias, a_log, d_skip, ssd_norm_w, q_norm_w, k_norm_w,
                norm2_w, later_shards):
    sh1, sc1, g1, sh2, sc2, g2 = [mod[:, i * D_MODEL:(i + 1) * D_MODEL] for i in range(N_MOD)]
    qw2, kw2 = jnp.tile(q_norm_w, (1, 2)), jnp.tile(k_norm_w, (1, 2))
    dskip_x = jnp.repeat(d_skip, HEAD_DIM, axis=1)
    dtb_x, al_x = jnp.repeat(dt_bias, HEAD_DIM, axis=1), jnp.repeat(a_log, HEAD_DIM, axis=1)
    head_ids = jnp.arange(HEAD_TILE, dtype=jnp.int32)[:, None]
    spread_x = (head_ids == jnp.arange(D_SSD, dtype=jnp.int32)[None, :] // HEAD_DIM).astype(BF16)

    h1 = _rms_mod(x, norm1_w, sc1, sh1, "rms_mod1")
    proj = _matmul_nt(h1, w_in_tp, "in_proj")

    xc = _conv_silu(proj, conv_w, conv_b, "conv_silu")
    ycore, states = _ssd_fwd(xc, proj, spread_x, dtb_x, al_x, "ssd_fwd")
    y_ssd = _ssd_gate_norm(ycore, xc, proj, dskip_x, ssd_norm_w, "ssd_gate_norm")

    qn, kn, vb = _qkv_prep(proj, qw2, kw2, "qkv_prep")
    o_sb, tot, *gathered = _sb_fwd(qn, kn, vb, later_shards, "sb_fwd")
    w_out, w_gate_t, w_up_t, w_down = [g.reshape(N_DEV * g.shape[1], D_MODEL) for g in gathered]
    ycat = jnp.concatenate([y_ssd, o_sb.astype(MXU_DTYPE)], axis=1)

    mix = _matmul(ycat, w_out, "out_proj")
    x1, h2 = _residual_rms_mod(x, mix, g1, norm2_w, sc2, sh2, "residual_rms_mod2")
    gate, up, act = _ffn_in(h2, w_gate_t, w_up_t, "ffn_in")
    ffn = _matmul(act, w_down, "ffn_down")
    dy, dffn, st_g2, loss_blk = _loss_head(x1, ffn, g2, target, "loss_head")

    g_down = _matmul_tn(act, dffn, "ffn_down_dw", MXU_DTYPE)
    dgate, dup = _ffn_down_bwd(dffn, w_down, gate, up, "ffn_down_dx")
    g_gate_t = _matmul_tn(dgate, h2, "ffn_gate_dw", MXU_DTYPE)
    g_up_t = _matmul_tn(dup, h2, "ffn_up_dw", MXU_DTYPE)
    dh2 = _matmul_sum2(dgate, w_gate_t, dup, w_up_t, "ffn_dh")
    dx1, dmix, st_n2 = _norm_bwd(x1, dh2, dy, norm2_w, sc2, "norm2_bwd", gated=(mix, g1))

    g_out = _matmul_tn(ycat, dmix, "out_proj_dw", MXU_DTYPE)
    dycat = _matmul_nt(dmix, w_out, "out_proj_dx")

    partials = [g.reshape(N_DEV, g.shape[0] // N_DEV, D_MODEL).astype(BF16) for g in (g_out, g_gate_t, g_up_t, g_down)]
    dqn, dkn, dv, *slots = _sb_bwd(qn, kn, vb, dycat, tot, partials, "sb_bwd")
    dq, dk, st_q, st_k = _qk_norm_bwd(proj, dqn, dkn, qw2, kw2, "qk_norm_bwd")

    dycore, dxs_skip, dz, st_gn = _ssd_gate_norm_bwd(dycat, ycore, xc, proj, dskip_x, ssd_norm_w, "ssd_gate_norm_bwd")
    dxc, ddt_tile, st_ssd, st_heads = _ssd_bwd(xc, proj, spread_x, dtb_x, al_x, dycore, states, "ssd_bwd")
    dxc = jnp.concatenate([dxc[:, :D_SSD] + dxs_skip, dxc[:, D_SSD:]], axis=1)
    dpre, st_conv = _conv_silu_bwd_pre(proj, dxc, conv_w, conv_b, "conv_silu_bwd")
    dxbc = _conv_bwd_input(dpre, conv_w, "conv_bwd_input")

    pieces = {"z": (dz, O_Z), "xbc": (dxbc, O_XBC), "dt": (ddt_tile, O_DT), "q": (dq, O_Q), "k": (dk, O_Q + D_SB), "v": (dv, O_Q + 2 * D_SB)}
    g_rows = {n: _matmul_tn(p, h1, "in_proj_dw_" + n, MXU_DTYPE) for n, (p, _) in pieces.items()}
    g_rows["dt"] = g_rows["dt"][:N_HEADS]
    g_in_t = jnp.concatenate(list(g_rows.values()), axis=0)
    g_in_blocks = g_in_t.reshape(N_DEV, g_in_t.shape[0] // N_DEV, D_MODEL).astype(BF16)
    dh1, g_in_slots = _matmul_pieces_beside_scatter([p for p, _ in pieces.values()], [o for _, o in pieces.values()], w_in_tp,
                                                    [g_in_blocks], "in_proj_dx")
    slots = [g_in_slots] + list(slots)
    grad_x, st_n1 = _norm_bwd(x, dh1, dx1, norm1_w, sc1, "norm1_bwd")

    pad = jnp.zeros((1, SM_SSD_NORM - SM_D_SKIP - N_HEADS), F32)
    small = jnp.concatenate(
        [st_n1[1:2], st_n1[0:1], st_n2[3:4], st_n2[1:2], st_n2[0:1], st_g2[0:1],
         st_n1[2:3], st_conv[4:5], st_conv[0:4].reshape(1, CONV_WIDTH * D_CONV),
         st_ssd[1:2, ::HEAD_DIM], st_ssd[0:1, ::HEAD_DIM], st_heads[2:3, ::HEAD_DIM], pad,
         st_gn[0:1], st_q[0:1, :HEAD_DIM] + st_q[0:1, HEAD_DIM:], st_k[0:1, :HEAD_DIM] + st_k[0:1, HEAD_DIM:], st_n2[2:3]], axis=1)
    return loss_blk, grad_x, slots, small


def kernel(x, c, w_ada, b_ada, norm1_w, w_in, conv_w, conv_b, dt_bias, a_log, d_skip, ssd_norm_w, q_norm_w, k_norm_w, w_out, norm2_w, w_gate, w_up, w_down, loss_target, m_w_ada, m_b_ada, m_norm1_w, m_w_in, m_conv_w, m_conv_b, m_dt_bias, m_a_log, m_d_skip, m_ssd_norm_w, m_q_norm_w, m_k_norm_w, m_w_out, m_norm2_w, m_w_gate, m_w_up, m_w_down, v_w_ada, v_b_ada, v_norm1_w, v_w_in, v_conv_w, v_conv_b, v_dt_bias, v_a_log, v_d_skip, v_ssd_norm_w, v_q_norm_w, v_k_norm_w, v_w_out, v_norm2_w, v_w_gate, v_w_up, v_w_down):
    me = 4 * lax.axis_index("x") + 2 * lax.axis_index("y") + lax.axis_index("c")
    conv_cols = D_CONV // N_DEV
    ada_cols = N_MOD * D_MODEL // N_DEV

    cond_conv = jnp.concatenate([c, conv_w[0].reshape(1, CONV_WIDTH * conv_cols)], axis=1)
    first, mods, w_in_t = _prologue(cond_conv, w_ada[0], w_in[0].T.astype(BF16), "prologue")
    w_in_tp = _pad_w_in_t(w_in_t.reshape(D_IN_PROJ, D_MODEL))
    later_shards = [s.astype(BF16) for s in (w_out[0], w_gate[0].T, w_up[0].T, w_down[0])]
    c_all = first[:, 0, :D_MODEL]
    conv_w_f = first[:, 0, D_MODEL:].reshape(N_DEV, CONV_WIDTH, conv_cols).transpose(1, 0, 2).reshape(CONV_WIDTH, D_CONV)
    mod = lax.dynamic_index_in_dim(mods, me, axis=1, keepdims=False)
    mod = mod.reshape(1, N_MOD * D_MODEL) + b_ada

    loss_blk, grad_x, slots, small = _local_step(
        x[0], loss_target[0], mod, norm1_w, w_in_tp, conv_w_f, conv_b, dt_bias, a_log, d_skip, ssd_norm_w, q_norm_w, k_norm_w,
        norm2_w, later_shards)
    loss = lax.psum(loss_blk[0, 0], ("x", "y", "c"))

    g_in_t, g_out, g_gate_t, g_up_t, g_down = [_sum_slots(s, "sum_grads_" + n) for s, n in zip(slots, ("in", "out", "gate", "up", "down"))]
    g_in, g_gate, g_up = g_in_t.T, g_gate_t.T, g_up_t.T

    parts = _all_gather_small(small, "gather_small")
    gsum = _sum_small(parts, "sum_small")
    dmod_shard = lax.dynamic_slice_in_dim(parts[:, 0, :N_MOD * D_MODEL], me * ada_cols, ada_cols, axis=1)
    g_ada = _ada_bwd(c_all, dmod_shard, "ada_bwd")
    g_conv_w = lax.dynamic_slice_in_dim(gsum[:, SM_CONV_W:SM_DT_BIAS].reshape(CONV_WIDTH, D_CONV), me * conv_cols, conv_cols, axis=1)

    def pack_small(b_ada_, norm1_, conv_b_, dt_bias_, a_log_, d_skip_, ssd_norm_, q_norm_, k_norm_, norm2_):
        return jnp.concatenate(
            [b_ada_, norm1_, conv_b_, jnp.zeros((1, CONV_WIDTH * D_CONV), F32), dt_bias_, a_log_, d_skip_,
             jnp.zeros((1, SM_SSD_NORM - SM_D_SKIP - N_HEADS), F32), ssd_norm_, q_norm_, k_norm_, norm2_], axis=1)

    def unpack_small(p):
        return {
            "b_ada": p[:, SM_B_ADA:SM_NORM1], "norm1_w": p[:, SM_NORM1:SM_CONV_B], "conv_b": p[:, SM_CONV_B:SM_CONV_W],
            "dt_bias": p[:, SM_DT_BIAS:SM_A_LOG], "a_log": p[:, SM_A_LOG:SM_D_SKIP], "d_skip": p[:, SM_D_SKIP:SM_D_SKIP + N_HEADS],
            "ssd_norm_w": p[:, SM_SSD_NORM:SM_Q_NORM], "q_norm_w": p[:, SM_Q_NORM:SM_K_NORM], "k_norm_w": p[:, SM_K_NORM:SM_NORM2],
            "norm2_w": p[:, SM_NORM2:SM_TOTAL]}

    w_small = pack_small(b_ada, norm1_w, conv_b, dt_bias, a_log, d_skip, ssd_norm_w, q_norm_w, k_norm_w, norm2_w)
    m_small = pack_small(m_b_ada, m_norm1_w, m_conv_b, m_dt_bias, m_a_log, m_d_skip, m_ssd_norm_w, m_q_norm_w, m_k_norm_w, m_norm2_w)
    v_small = pack_small(v_b_ada, v_norm1_w, v_conv_b, v_dt_bias, v_a_log, v_d_skip, v_ssd_norm_w, v_q_norm_w, v_k_norm_w, v_norm2_w)
    small_out = [unpack_small(t) for t in (gsum,) + tuple(_adamw(w_small, gsum, m_small, v_small, "adamw_small"))]

    sharded = {
        "w_ada": (w_ada[0], g_ada, m_w_ada[0], v_w_ada[0]),
        "w_in": (w_in[0], g_in, m_w_in[0], v_w_in[0]),
        "conv_w": (conv_w[0], g_conv_w, m_conv_w[0], v_conv_w[0]),
        "w_out": (w_out[0], g_out, m_w_out[0], v_w_out[0]),
        "w_gate": (w_gate[0], g_gate, m_w_gate[0], v_w_gate[0]),
        "w_up": (w_up[0], g_up, m_w_up[0], v_w_up[0]),
        "w_down": (w_down[0], g_down, m_w_down[0], v_w_down[0]),
    }
    sharded_out = {n: (t[1],) + tuple(_adamw(*t, "adamw_" + n)) for n, t in sharded.items()}

    names = ["w_ada", "b_ada", "norm1_w", "w_in", "conv_w", "conv_b", "dt_bias", "a_log", "d_skip", "ssd_norm_w", "q_norm_w",
             "k_norm_w", "w_out", "norm2_w", "w_gate", "w_up", "w_down"]
    outs = [loss, grad_x[None]]
    for kind in range(4):
        for n in names:
            outs.append(sharded_out[n][kind][None] if n in sharded_out else small_out[kind][n])
    return tuple(outs)
```

```python
import functools

import jax
import jax.numpy as jnp
from jax import lax
from jax.experimental import pallas as pl
from jax.experimental.pallas import tpu as pltpu

F32 = jnp.float32
BF16 = jnp.bfloat16
MXU_DTYPE = jnp.bfloat16
HIGHEST = lax.Precision.HIGHEST
MESH_IDS = pl.DeviceIdType.MESH

N_DEV = 8
D_MODEL = 1024
HEAD_DIM = 64
N_HEADS = 16
D_SSD = 1024
D_SB = 1024
SSD_GROUPS = 2
SSD_STATE = 128
GROUP_WIDTH = D_SSD // SSD_GROUPS
D_CONV = D_SSD + 2 * SSD_GROUPS * SSD_STATE
CONV_WIDTH = 4
CHUNK = 128
D_FF = 2816
N_MOD = 6
EPS = 1e-6
D_IN_PROJ = 5648
O_XBC = 0
O_DT = D_CONV
DT_PAD = 512
O_Z = O_DT + DT_PAD
O_Q = O_Z + D_SSD
D_IN_PAD = O_Q + 3 * D_SB
HALO = 8

ADAM_LR = 0.001
ADAM_B1 = 0.9
ADAM_B2 = 0.999
ADAM_EPS = 1e-08
ADAM_WD = 0.01
ADAM_STEP = 10

SM_B_ADA = 0
SM_NORM1 = 6144
SM_CONV_B = 7168
SM_CONV_W = 8704
SM_DT_BIAS = 14848
SM_A_LOG = 14864
SM_D_SKIP = 14880
SM_SSD_NORM = 14976
SM_Q_NORM = 16000
SM_K_NORM = 16064
SM_NORM2 = 16128
SM_TOTAL = 17152


def _pick(n, cap, mult):
    if n <= cap:
        return n
    best = None
    for t in range(mult, cap + 1, mult):
        if n % t == 0:
            best = t
    assert best is not None, (n, cap, mult)
    return best


def _dot(a, b, precision=None):
    return jnp.dot(a, b, preferred_element_type=F32, precision=precision)


def _dot_nt(a, b, precision=None):
    return lax.dot_general(a, b, (((1,), (1,)), ((), ())), preferred_element_type=F32, precision=precision)


def _dot_tn(a, b, precision=None):
    return lax.dot_general(a, b, (((0,), (0,)), ((), ())), preferred_element_type=F32, precision=precision)


def _softplus(v):
    return jnp.maximum(v, 0.0) + jnp.log1p(jnp.exp(-jnp.abs(v)))


def _softplus_logits(v):
    return jnp.maximum(v, 0.0) + jnp.log(1.0 + jnp.exp(-jnp.abs(v)))


def _tri_dot(tri_b, v):
    hi = v.astype(BF16)
    r1 = v - hi.astype(F32)
    mid = r1.astype(BF16)
    lo = (r1 - mid.astype(F32)).astype(BF16)
    return _dot(tri_b, hi) + _dot(tri_b, mid) + _dot(tri_b, lo)


def _sigmoid(v):
    return jax.nn.sigmoid(v)


def _colsum(v):
    return jnp.sum(v, axis=0, keepdims=True)


def _split_dot(v, tri2):
    hi = v.astype(BF16)
    lo = (v - hi.astype(F32)).astype(BF16)
    return _dot(jnp.concatenate([hi, lo], axis=1), tri2)


def _position():
    x, y, c = lax.axis_index("x"), lax.axis_index("y"), lax.axis_index("c")
    return x, y, c


def _peer(x, y, c, k):
    px = 1 - x if (k >> 2) & 1 else x
    py = 1 - y if (k >> 1) & 1 else y
    pc = 1 - c if k & 1 else c
    return px, py, pc


def _all_gather_small(v, name):
    n = v.shape[1]

    def body(v_ref, out_ref, send_sems, recv_sems, local_sem):
        x, y, c = _position()
        me = 4 * x + 2 * y + c
        mine = pltpu.make_async_copy(v_ref, out_ref.at[me], local_sem)
        mine.start()
        sends = []
        for k in range(1, N_DEV):
            cp = pltpu.make_async_remote_copy(
                src_ref=v_ref, dst_ref=out_ref.at[me], send_sem=send_sems.at[k - 1], recv_sem=recv_sems.at[k - 1],
                device_id=_peer(x, y, c, k), device_id_type=MESH_IDS)
            cp.start()
            sends.append(cp)
        for k in range(1, N_DEV):
            px, py, pc = _peer(x, y, c, k)
            pltpu.make_async_remote_copy(
                src_ref=v_ref, dst_ref=out_ref.at[4 * px + 2 * py + pc], send_sem=send_sems.at[k - 1],
                recv_sem=recv_sems.at[k - 1], device_id=(px, py, pc), device_id_type=MESH_IDS).wait_recv()
        for cp in sends:
            cp.wait_send()
        mine.wait()

    return pl.pallas_call(
        body, name=name,
        out_shape=jax.ShapeDtypeStruct((N_DEV, 1, n), v.dtype),
        in_specs=[pl.BlockSpec(memory_space=pltpu.VMEM)],
        out_specs=pl.BlockSpec(memory_space=pltpu.VMEM),
        scratch_shapes=[pltpu.SemaphoreType.DMA((N_DEV - 1,)), pltpu.SemaphoreType.DMA((N_DEV - 1,)),
                        pltpu.SemaphoreType.DMA],
    )(v)


def _prologue(cond_conv, w_ada_shard, w_in_shard, name):
    def body(cc_ref, wada_ref, win_ref, first_ref, mods_ref, wout_ref, mod_cols, conds_all, big_send, big_recv, big_local,
             a_send, a_recv, a_local, b_send, b_recv, b_local):
        x, y, c = _position()
        me, sibling = (x, y, c), (x, y, 1 - c)
        chips = [(1 - x, y), (x, 1 - y), (1 - x, 1 - y)]

        def big(k, blk, to, own=False):
            slot = wout_ref.at[4 * blk[0] + 2 * blk[1] + blk[2]]
            return pltpu.make_async_remote_copy(src_ref=win_ref if own else slot, dst_ref=slot, send_sem=big_send.at[k],
                                                recv_sem=big_recv.at[k], device_id=to, device_id_type=MESH_IDS)

        mine = pltpu.make_async_copy(win_ref, wout_ref.at[4 * x + 2 * y + c], big_local.at[0])
        mine.start()
        first = [big(0, me, sibling, own=True)] + [big(1 + j, me, (*chip, c), own=True) for j, chip in enumerate(chips)]
        for cp in first:
            cp.start()

        conds = _DirectExchange("gather", [cc_ref], [first_ref], a_send, a_recv, a_local)
        conds.start()
        conds.wait()
        for d in range(N_DEV):
            conds_all[d:d + 1, :] = first_ref[d, :, 0:D_MODEL]
        cv = conds_all[...]
        mod_cols[...] = _dot(cv * _sigmoid(cv), wada_ref[...], HIGHEST)
        mods = _DirectExchange("gather", [mod_cols], [mods_ref], b_send, b_recv, b_local)
        mods.start()
        mods.wait()

        passed = []
        for j, chip in enumerate(chips):
            big(1 + j, (*chip, c), me).wait_recv()
            passed.append(big(4 + j, (*chip, c), sibling))
            passed[-1].start()
        big(0, sibling, me).wait_recv()
        for j, chip in enumerate(chips):
            big(4 + j, (*chip, 1 - c), me).wait_recv()
        for cp in first + passed:
            cp.wait_send()
        mine.wait()

    vmem = pl.BlockSpec(memory_space=pltpu.VMEM)
    hbm = pl.BlockSpec(memory_space=pl.ANY)
    cols = w_ada_shard.shape[1]
    return pl.pallas_call(
        body, name=name,
        out_shape=[jax.ShapeDtypeStruct((N_DEV,) + cond_conv.shape, F32), jax.ShapeDtypeStruct((N_DEV, N_DEV, cols), F32),
                   jax.ShapeDtypeStruct((N_DEV,) + w_in_shard.shape, w_in_shard.dtype)],
        in_specs=[vmem, vmem, hbm], out_specs=[vmem, vmem, hbm],
        scratch_shapes=[pltpu.VMEM((N_DEV, cols), F32), pltpu.VMEM((N_DEV, D_MODEL), F32)] + _DirectExchange.scratch(1) * 3,
    )(cond_conv, w_ada_shard, w_in_shard)


class _DirectExchange:
    def __init__(self, kind, src_refs, dst_refs, send_sems, recv_sems, local_sems):
        self.kind, self.src_refs, self.dst_refs = kind, src_refs, dst_refs
        self.send_sems, self.recv_sems, self.local_sems = send_sems, recv_sems, local_sems

    @staticmethod
    def scratch(n_arrays):
        copies = N_DEV - 1
        return [pltpu.SemaphoreType.DMA((copies * n_arrays,)), pltpu.SemaphoreType.DMA((copies * n_arrays,)),
                pltpu.SemaphoreType.DMA((n_arrays,))]

    def _copies(self):
        x, y, c = _position()
        me = 4 * x + 2 * y + c
        local, sends, arrivals = [], [], []
        for a, (src, dst) in enumerate(zip(self.src_refs, self.dst_refs)):
            own = src if self.kind == "gather" else src.at[me]
            local.append(pltpu.make_async_copy(own, dst.at[me], self.local_sems.at[a]))
            for k in range(1, N_DEV):
                px, py, pc = _peer(x, y, c, k)
                peer = 4 * px + 2 * py + pc
                sems = dict(send_sem=self.send_sems.at[(N_DEV - 1) * a + k - 1], recv_sem=self.recv_sems.at[(N_DEV - 1) * a + k - 1],
                            device_id=(px, py, pc), device_id_type=MESH_IDS)
                sends.append(pltpu.make_async_remote_copy(
                    src_ref=src if self.kind == "gather" else src.at[peer], dst_ref=dst.at[me], **sems))
                arrivals.append(pltpu.make_async_remote_copy(src_ref=own, dst_ref=dst.at[peer], **sems))
        return local, sends, arrivals

    def start(self):
        local, sends, _ = self._copies()
        for cp in local + sends:
            cp.start()

    def wait(self):
        local, sends, arrivals = self._copies()
        for cp in arrivals:
            cp.wait_recv()
        for cp in sends:
            cp.wait_send()
        for cp in local:
            cp.wait()


def _sum_slots(slots, name):
    _, r, n = slots.shape
    tr = _pick(r, 256, 16) if r % 16 == 0 else r

    def body(s_ref, o_ref):
        acc = s_ref[0].astype(F32)
        for d in range(1, N_DEV):
            acc = acc + s_ref[d].astype(F32)
        o_ref[...] = acc

    return pl.pallas_call(
        body, name=name, grid=(r // tr,),
        in_specs=[pl.BlockSpec((N_DEV, tr, n), lambda i: (0, i, 0))],
        out_specs=pl.BlockSpec((tr, n), lambda i: (i, 0)),
        out_shape=jax.ShapeDtypeStruct((r, n), F32),
    )(slots)


def _matmul(a, b, name, out_dtype=F32):
    m, k = a.shape
    _, n = b.shape
    tm, tn, tk = _pick(m, 1024, 16), _pick(n, 1408, 128), _pick(k, 1408, 128)
    nk = k // tk

    def body(a_ref, b_ref, o_ref, acc_ref):
        kk = pl.program_id(2)

        @pl.when(kk == 0)
        def _():
            acc_ref[...] = jnp.zeros_like(acc_ref)

        acc_ref[...] += _dot(a_ref[...].astype(MXU_DTYPE), b_ref[...].astype(MXU_DTYPE))

        @pl.when(kk == nk - 1)
        def _():
            o_ref[...] = acc_ref[...].astype(o_ref.dtype)

    return pl.pallas_call(
        body, name=name, grid=(m // tm, n // tn, nk),
        in_specs=[pl.BlockSpec((tm, tk), lambda i, j, kk: (i, kk)), pl.BlockSpec((tk, tn), lambda i, j, kk: (kk, j))],
        out_specs=pl.BlockSpec((tm, tn), lambda i, j, kk: (i, j)),
        out_shape=jax.ShapeDtypeStruct((m, n), out_dtype),
        scratch_shapes=[pltpu.VMEM((tm, tn), F32)],
    )(a, b)


def _matmul_pieces_beside_scatter(pieces, offsets, w, blocks, name):
    m = pieces[0].shape[0]
    n = w.shape[1]
    tm = _pick(m, 256, 16)
    gm = m // tm
    npieces, na = len(pieces), len(blocks)
    hbm = pl.BlockSpec(memory_space=pl.ANY)

    def body(*refs):
        a_refs, w_refs = refs[:npieces], refs[npieces:2 * npieces]
        rest = refs[2 * npieces:]
        o_ref = rest[na]
        exchange = _DirectExchange("scatter", rest[:na], rest[na + 1:2 * na + 1], *rest[2 * na + 1:])
        i = pl.program_id(0)

        @pl.when(i == 0)
        def _():
            exchange.start()

        acc = _dot(a_refs[0][...].astype(MXU_DTYPE), w_refs[0][...])
        for a_ref, w_ref in zip(a_refs[1:], w_refs[1:]):
            acc = acc + _dot(a_ref[...].astype(MXU_DTYPE), w_ref[...])
        o_ref[...] = acc

        @pl.when(i == gm - 1)
        def _():
            exchange.wait()

    a_specs = [pl.BlockSpec((tm, p.shape[1]), lambda i: (i, 0)) for p in pieces]
    w_specs = [pl.BlockSpec((p.shape[1], n), functools.partial(lambda i, blk: (blk, 0), blk=off // p.shape[1]))
               for p, off in zip(pieces, offsets)]
    return pl.pallas_call(
        body, name=name, grid=(gm,),
        in_specs=a_specs + w_specs + [hbm] * na,
        out_specs=[pl.BlockSpec((tm, n), lambda i: (i, 0))] + [hbm] * na,
        out_shape=[jax.ShapeDtypeStruct((m, n), F32)] + [jax.ShapeDtypeStruct(blk.shape, blk.dtype) for blk in blocks],
        scratch_shapes=_DirectExchange.scratch(na),
    )(*pieces, *([w] * npieces), *blocks)


def _matmul_nt(a, bt, name, out_dtype=F32):
    m, k = a.shape
    n, _ = bt.shape
    tm, tn, tk = _pick(m, 1024, 16), _pick(n, 1408, 128), _pick(k, 1408, 128)
    nk = k // tk

    def body(a_ref, b_ref, o_ref, acc_ref):
        kk = pl.program_id(2)

        @pl.when(kk == 0)
        def _():
            acc_ref[...] = jnp.zeros_like(acc_ref)

        acc_ref[...] += _dot_nt(a_ref[...].astype(MXU_DTYPE), b_ref[...].astype(MXU_DTYPE))

        @pl.when(kk == nk - 1)
        def _():
            o_ref[...] = acc_ref[...].astype(o_ref.dtype)

    return pl.pallas_call(
        body, name=name, grid=(m // tm, n // tn, nk),
        in_specs=[pl.BlockSpec((tm, tk), lambda i, j, kk: (i, kk)), pl.BlockSpec((tn, tk), lambda i, j, kk: (j, kk))],
        out_specs=pl.BlockSpec((tm, tn), lambda i, j, kk: (i, j)),
        out_shape=jax.ShapeDtypeStruct((m, n), out_dtype),
        scratch_shapes=[pltpu.VMEM((tm, tn), F32)],
    )(a, bt)


def _matmul_sum2(a1, b1, a2, b2, name):
    m, k = a1.shape
    _, n = b1.shape
    tm, tn, tk = _pick(m, 1024, 16), _pick(n, 1408, 128), _pick(k, 1408, 128)
    nk = k // tk

    def body(a1_ref, b1_ref, a2_ref, b2_ref, o_ref):
        @pl.when(pl.program_id(2) == 0)
        def _():
            o_ref[...] = jnp.zeros_like(o_ref)

        o_ref[...] += (_dot(a1_ref[...].astype(MXU_DTYPE), b1_ref[...].astype(MXU_DTYPE))
                       + _dot(a2_ref[...].astype(MXU_DTYPE), b2_ref[...].astype(MXU_DTYPE)))

    a_spec = pl.BlockSpec((tm, tk), lambda i, j, kk: (i, kk))
    b_spec = pl.BlockSpec((tk, tn), lambda i, j, kk: (kk, j))
    return pl.pallas_call(
        body, name=name, grid=(m // tm, n // tn, nk), in_specs=[a_spec, b_spec, a_spec, b_spec],
        out_specs=pl.BlockSpec((tm, tn), lambda i, j, kk: (i, j)),
        out_shape=jax.ShapeDtypeStruct((m, n), F32),
    )(a1, b1, a2, b2)


def _matmul_tn(a, b, name, out_dtype=F32):
    l, m = a.shape
    _, n = b.shape
    tm, tn, tl = _pick(m, 1408, 128), _pick(n, 1408, 128), _pick(l, 512, 16)
    nl = l // tl

    def body(a_ref, b_ref, o_ref, acc_ref):
        ll = pl.program_id(2)

        @pl.when(ll == 0)
        def _():
            acc_ref[...] = jnp.zeros_like(acc_ref)

        acc_ref[...] += _dot_tn(a_ref[...].astype(MXU_DTYPE), b_ref[...].astype(MXU_DTYPE))

        @pl.when(ll == nl - 1)
        def _():
            o_ref[...] = acc_ref[...].astype(o_ref.dtype)

    return pl.pallas_call(
        body, name=name, grid=(m // tm, n // tn, nl),
        in_specs=[pl.BlockSpec((tl, tm), lambda i, j, ll: (ll, i)), pl.BlockSpec((tl, tn), lambda i, j, ll: (ll, j))],
        out_specs=pl.BlockSpec((tm, tn), lambda i, j, ll: (i, j)),
        out_shape=jax.ShapeDtypeStruct((m, n), out_dtype),
        scratch_shapes=[pltpu.VMEM((tm, tn), F32)],
    )(a, b)


def _row_specs(l, d, n_rows, n_vecs):
    tl = _pick(l, 512, 16)
    row = pl.BlockSpec((tl, d), lambda i: (i, 0))
    vec = pl.BlockSpec((1, d), lambda i: (0, 0))
    return tl, row, vec, [row] * n_rows + [vec] * n_vecs


def _rms_mod(x, nw, sc, sh, name):
    l, d = x.shape
    tl, row, _, in_specs = _row_specs(l, d, 1, 3)

    def body(x_ref, nw_ref, sc_ref, sh_ref, h_ref):
        xv = x_ref[...]
        r = lax.rsqrt(jnp.mean(xv * xv, axis=-1, keepdims=True) + EPS)
        h_ref[...] = (((xv * r) * nw_ref[...]) * (1.0 + sc_ref[...]) + sh_ref[...]).astype(h_ref.dtype)

    return pl.pallas_call(body, name=name, grid=(l // tl,), in_specs=in_specs, out_specs=row,
                          out_shape=jax.ShapeDtypeStruct((l, d), MXU_DTYPE))(x, nw, sc, sh)


def _residual_rms_mod(x, mix, g, nw, sc, sh, name):
    l, d = x.shape
    tl, row, _, in_specs = _row_specs(l, d, 2, 4)

    def body(x_ref, mix_ref, g_ref, nw_ref, sc_ref, sh_ref, x1_ref, h_ref):
        xv = x_ref[...] + g_ref[...] * mix_ref[...]
        x1_ref[...] = xv
        r = lax.rsqrt(jnp.mean(xv * xv, axis=-1, keepdims=True) + EPS)
        h_ref[...] = (((xv * r) * nw_ref[...]) * (1.0 + sc_ref[...]) + sh_ref[...]).astype(h_ref.dtype)

    return pl.pallas_call(body, name=name, grid=(l // tl,), in_specs=in_specs, out_specs=[row, row],
                          out_shape=[jax.ShapeDtypeStruct((l, d), F32), jax.ShapeDtypeStruct((l, d), MXU_DTYPE)],
                          )(x, mix, g, nw, sc, sh)


def _norm_bwd(x, dh, dres, nw, sc, name, gated=None):
    l, d = x.shape
    tl, row, vec, in_specs = _row_specs(l, d, 3, 2)
    stats = pl.BlockSpec((8, d), lambda i: (0, 0))

    def body(x_ref, dh_ref, dres_ref, nw_ref, sc_ref, *rest):
        dx_ref, st_ref = rest[-2:] if gated is None else (rest[2], rest[4])
        xv, dh_v = x_ref[...], dh_ref[...]
        r = lax.rsqrt(jnp.mean(xv * xv, axis=-1, keepdims=True) + EPS)
        xn = xv * r
        dxn = dh_v * (nw_ref[...] * (1.0 + sc_ref[...]))
        dx = dres_ref[...] + r * (dxn - xn * jnp.mean(dxn * xn, axis=-1, keepdims=True))
        dx_ref[...] = dx

        @pl.when(pl.program_id(0) == 0)
        def _():
            st_ref[...] = jnp.zeros_like(st_ref)

        dhx = dh_v * xn
        st_ref[0:1, :] += _colsum(dhx * nw_ref[...])
        st_ref[1:2, :] += _colsum(dh_v)
        st_ref[2:3, :] += _colsum(dhx * (1.0 + sc_ref[...]))
        if gated is not None:
            val_ref, g_ref, dval_ref = rest[0], rest[1], rest[3]
            dval_ref[...] = (g_ref[...] * dx).astype(dval_ref.dtype)
            st_ref[3:4, :] += _colsum(dx * val_ref[...])

    if gated is None:
        return pl.pallas_call(body, name=name, grid=(l // tl,), in_specs=in_specs, out_specs=[row, stats],
                              out_shape=[jax.ShapeDtypeStruct((l, d), F32), jax.ShapeDtypeStruct((8, d), F32)],
                              )(x, dh, dres, nw, sc)
    return pl.pallas_call(body, name=name, grid=(l // tl,), in_specs=in_specs + [row, vec], out_specs=[row, row, stats],
                          out_shape=[jax.ShapeDtypeStruct((l, d), F32), jax.ShapeDtypeStruct((l, d), MXU_DTYPE),
                                     jax.ShapeDtypeStruct((8, d), F32)],
                          )(x, dh, dres, nw, sc, *gated)


def _loss_head(x1, f, g, target, name):
    l, d = x1.shape
    tl, row, _, _ = _row_specs(l, d, 0, 0)
    vec = pl.BlockSpec((1, d), lambda i: (0, 0))

    def body(x1_ref, f_ref, g_ref, t_ref, dy_ref, df_ref, st_ref, loss_ref):
        fv = f_ref[...]
        e = x1_ref[...] + g_ref[...] * fv - t_ref[...]
        dy = e * (1.0 / d)
        dy_ref[...] = dy
        df_ref[...] = (g_ref[...] * dy).astype(df_ref.dtype)

        @pl.when(pl.program_id(0) == 0)
        def _():
            loss_ref[...] = jnp.zeros_like(loss_ref)
            st_ref[...] = jnp.zeros_like(st_ref)

        st_ref[0:1, :] += _colsum(dy * fv)
        s = jnp.sum(jnp.sum(e * e, axis=1, keepdims=True), axis=0, keepdims=True)
        loss_ref[...] += (0.5 / d) * s

    return pl.pallas_call(body, name=name, grid=(l // tl,), in_specs=[row, row, vec, row],
                          out_specs=[row, row, pl.BlockSpec((8, d), lambda i: (0, 0)), pl.BlockSpec((8, 128), lambda i: (0, 0))],
                          out_shape=[jax.ShapeDtypeStruct((l, d), F32), jax.ShapeDtypeStruct((l, d), MXU_DTYPE),
                                     jax.ShapeDtypeStruct((8, d), F32), jax.ShapeDtypeStruct((8, 128), F32)],
                          )(x1, f, g, target)


def _ffn_in(h, w_gate_t, w_up_t, name):
    l, k = h.shape
    f = w_gate_t.shape[0]
    tl, tf = _pick(l, 512, 16), _pick(f, 1408, 128)
    h_spec = pl.BlockSpec((tl, k), lambda j, i: (i, 0))
    w_spec = pl.BlockSpec((tf, k), lambda j, i: (j, 0))
    spec = pl.BlockSpec((tl, tf), lambda j, i: (i, j))

    def body(h_ref, wg_ref, wu_ref, g_ref, u_ref, a_ref):
        hv = h_ref[...].astype(MXU_DTYPE)
        gv = _dot_nt(hv, wg_ref[...].astype(MXU_DTYPE))
        uv = _dot_nt(hv, wu_ref[...].astype(MXU_DTYPE))
        g_ref[...] = gv
        u_ref[...] = uv
        a_ref[...] = (gv * _sigmoid(gv) * uv).astype(a_ref.dtype)

    return pl.pallas_call(body, name=name, grid=(f // tf, l // tl), in_specs=[h_spec, w_spec, w_spec], out_specs=[spec, spec, spec],
                          out_shape=[jax.ShapeDtypeStruct((l, f), F32)] * 2 + [jax.ShapeDtypeStruct((l, f), MXU_DTYPE)],
                          )(h, w_gate_t, w_up_t)


def _ffn_down_bwd(dffn, w_down, gate, up, name):
    l, f = gate.shape
    k = dffn.shape[1]
    tl, tf = _pick(l, 512, 16), _pick(f, 1408, 128)
    spec = pl.BlockSpec((tl, tf), lambda i, j: (i, j))

    def body(d_ref, w_ref, g_ref, u_ref, dg_ref, du_ref):
        dav = _dot_nt(d_ref[...].astype(MXU_DTYPE), w_ref[...].astype(MXU_DTYPE))
        gv = g_ref[...]
        s = _sigmoid(gv)
        dg_ref[...] = (dav * u_ref[...] * (s * (1.0 + gv * (1.0 - s)))).astype(dg_ref.dtype)
        du_ref[...] = (dav * (gv * s)).astype(du_ref.dtype)

    return pl.pallas_call(body, name=name, grid=(l // tl, f // tf),
                          in_specs=[pl.BlockSpec((tl, k), lambda i, j: (i, 0)), pl.BlockSpec((tf, k), lambda i, j: (j, 0)), spec, spec],
                          out_specs=[spec, spec], out_shape=[jax.ShapeDtypeStruct((l, f), MXU_DTYPE)] * 2)(dffn, w_down, gate, up)


def _conv_tile(l):
    return _pick(l, 512, 16)


def _conv_pre(buf, w_ref, b_ref, tl):
    acc = b_ref[...] + w_ref[3:4, :] * buf[HALO:HALO + tl, :]
    for k in range(CONV_WIDTH - 1):
        s = HALO - (CONV_WIDTH - 1) + k
        acc = acc + w_ref[k:k + 1, :] * buf[s:s + tl, :]
    return acc


def _fill_past(buf, u_ref, halo_ref, tl):
    i = pl.program_id(0)

    @pl.when(i == 0)
    def _():
        buf[0:HALO, :] = jnp.zeros((HALO, buf.shape[1]), F32)

    @pl.when(i > 0)
    def _():
        buf[0:HALO, :] = halo_ref[...]

    buf[HALO:HALO + tl, :] = u_ref[...]


def _conv_silu(u, w, b, name):
    l, ch = u.shape[0], w.shape[1]
    tl = _conv_tile(l)
    per = tl // HALO
    cur = pl.BlockSpec((tl, ch), lambda i: (i, 0))
    past = pl.BlockSpec((HALO, ch), lambda i: (jnp.maximum(i * per - 1, 0), 0))

    def body(u_ref, halo_ref, w_ref, b_ref, o_ref, buf):
        _fill_past(buf, u_ref, halo_ref, tl)
        pre = _conv_pre(buf, w_ref, b_ref, tl)
        o_ref[...] = pre * _sigmoid(pre)

    return pl.pallas_call(body, name=name, grid=(l // tl,),
                          in_specs=[cur, past, pl.BlockSpec((CONV_WIDTH, ch), lambda i: (0, 0)), pl.BlockSpec((1, ch), lambda i: (0, 0))],
                          out_specs=cur, out_shape=jax.ShapeDtypeStruct((l, ch), F32),
                          scratch_shapes=[pltpu.VMEM((tl + HALO, ch), F32)])(u, u, w, b)


def _conv_silu_bwd_pre(u, dxc, w, b, name):
    l, ch = u.shape[0], w.shape[1]
    tl = _conv_tile(l)
    per = tl // HALO
    cur = pl.BlockSpec((tl, ch), lambda i: (i, 0))
    past = pl.BlockSpec((HALO, ch), lambda i: (jnp.maximum(i * per - 1, 0), 0))

    def body(u_ref, halo_ref, d_ref, w_ref, b_ref, dpre_ref, st_ref, buf):
        _fill_past(buf, u_ref, halo_ref, tl)
        pre = _conv_pre(buf, w_ref, b_ref, tl)
        s = _sigmoid(pre)
        dpre = d_ref[...] * (s * (1.0 + pre * (1.0 - s)))
        dpre_ref[...] = dpre

        @pl.when(pl.program_id(0) == 0)
        def _():
            st_ref[...] = jnp.zeros_like(st_ref)

        for k in range(CONV_WIDTH):
            s0 = HALO - (CONV_WIDTH - 1) + k
            st_ref[k:k + 1, :] += _colsum(dpre * buf[s0:s0 + tl, :])
        st_ref[CONV_WIDTH:CONV_WIDTH + 1, :] += _colsum(dpre)

    return pl.pallas_call(body, name=name, grid=(l // tl,),
                          in_specs=[cur, past, cur, pl.BlockSpec((CONV_WIDTH, ch), lambda i: (0, 0)), pl.BlockSpec((1, ch), lambda i: (0, 0))],
                          out_specs=[cur, pl.BlockSpec((8, ch), lambda i: (0, 0))],
                          out_shape=[jax.ShapeDtypeStruct((l, ch), F32), jax.ShapeDtypeStruct((8, ch), F32)],
                          scratch_shapes=[pltpu.VMEM((tl + HALO, ch), F32)])(u, u, dxc, w, b)


def _conv_bwd_input(dpre, w, name):
    l, ch = dpre.shape
    tl = _conv_tile(l)
    per = tl // HALO
    nt = l // tl
    cur = pl.BlockSpec((tl, ch), lambda i: (i, 0))
    nxt = pl.BlockSpec((HALO, ch), lambda i: (jnp.minimum((i + 1) * per, l // HALO - 1), 0))

    def body(d_ref, halo_ref, w_ref, du_ref, buf):
        i = pl.program_id(0)
        buf[0:tl, :] = d_ref[...]

        @pl.when(i == nt - 1)
        def _():
            buf[tl:tl + HALO, :] = jnp.zeros((HALO, ch), F32)

        @pl.when(i < nt - 1)
        def _():
            buf[tl:tl + HALO, :] = halo_ref[...]

        acc = w_ref[3:4, :] * buf[0:tl, :]
        for k in range(CONV_WIDTH - 1):
            s = CONV_WIDTH - 1 - k
            acc = acc + w_ref[k:k + 1, :] * buf[s:s + tl, :]
        du_ref[...] = acc

    return pl.pallas_call(body, name=name, grid=(nt,),
                          in_specs=[cur, nxt, pl.BlockSpec((CONV_WIDTH, ch), lambda i: (0, 0))],
                          out_specs=cur, out_shape=jax.ShapeDtypeStruct((l, ch), F32),
                          scratch_shapes=[pltpu.VMEM((tl + HALO, ch), F32)])(dpre, dpre, w)


HEAD_TILE = 128


def _chunk_iota():
    r = lax.broadcasted_iota(jnp.int32, (CHUNK, CHUNK), 0)
    c = lax.broadcasted_iota(jnp.int32, (CHUNK, CHUNK), 1)
    return r, c


def _split3_dot(v, ones_b):
    hi = v.astype(BF16)
    r1 = v - hi.astype(F32)
    mid = r1.astype(BF16)
    lo = (r1 - mid.astype(F32)).astype(BF16)
    return _dot(hi, ones_b) + _dot(mid, ones_b) + _dot(lo, ones_b)


def _split3_dot_nt(v, ones_b):
    hi = v.astype(BF16)
    r1 = v - hi.astype(F32)
    mid = r1.astype(BF16)
    lo = (r1 - mid.astype(F32)).astype(BF16)
    return _dot_nt(hi, ones_b) + _dot_nt(mid, ones_b) + _dot_nt(lo, ones_b)


def _ssd_decays(dt_ref, sx_ref, dtbx_ref, alx_ref, r, c):
    tri = (r >= c).astype(BF16)
    raw_x = _split3_dot(dt_ref[...], sx_ref[...]) + dtbx_ref[...]
    dt_x = _softplus(raw_x)
    a_x = -jnp.exp(alx_ref[...])
    adt_x = a_x * dt_x
    acs_x = _tri_dot(tri, adt_x)
    atot_x = _colsum(adt_x)
    return raw_x, dt_x, a_x, acs_x, jnp.exp(acs_x), jnp.exp(atot_x - acs_x), jnp.exp(atot_x)


def _head_decay(acs_x, e, r, c, halves):
    tile = acs_x[:, (e // 2) * 2 * HEAD_DIM:(e // 2 + 1) * 2 * HEAD_DIM]
    col = jnp.broadcast_to(jnp.max(jnp.where(halves[e % 2], tile, -jnp.inf), axis=1, keepdims=True), (CHUNK, CHUNK))
    return jnp.where(r >= c, jnp.exp(col - col.T), 0.0)


def _half_masks():
    lane = lax.broadcasted_iota(jnp.int32, (CHUNK, 2 * HEAD_DIM), 1)
    return lane < HEAD_DIM, lane >= HEAD_DIM


def _ssd_specs(nc, reverse):
    def at(i):
        return nc - 1 - i if reverse else i
    xc = pl.BlockSpec((CHUNK, D_CONV), lambda i: (at(i), 0))
    wide = pl.BlockSpec((CHUNK, D_SSD), lambda i: (at(i), 0))
    dt_tile = pl.BlockSpec((CHUNK, HEAD_TILE), lambda i: (at(i), O_DT // HEAD_TILE))
    dt_out = pl.BlockSpec((CHUNK, HEAD_TILE), lambda i: (at(i), 0))
    spread = pl.BlockSpec((HEAD_TILE, D_SSD), lambda i: (0, 0))
    vec = pl.BlockSpec((1, D_SSD), lambda i: (0, 0))
    state = pl.BlockSpec((1, SSD_STATE, D_SSD), lambda i: (at(i), 0, 0))
    return xc, wide, dt_tile, dt_out, spread, vec, state


def _ssd_fwd(xc, proj, spread_x, dtb_x, al_x, name):
    l = xc.shape[0]
    nc = l // CHUNK
    xc_s, wide_s, dt_s, _, spread_s, vec_s, state_s = _ssd_specs(nc, False)
    pairs_per_group = GROUP_WIDTH // (2 * HEAD_DIM)

    def body(xc_ref, dt_ref, sx_ref, dtbx_ref, alx_ref, y_ref, sp_ref, state):
        @pl.when(pl.program_id(0) == 0)
        def _():
            state[...] = jnp.zeros_like(state)

        r, c = _chunk_iota()
        halves = _half_masks()
        _, dt_x, _, acs_x, ea_x, ds_x, eatot_x = _ssd_decays(dt_ref, sx_ref, dtbx_ref, alx_ref, r, c)
        xg = xc_ref[:, 0:D_SSD] * dt_x
        sp_ref[0] = state[...]
        for g in range(SSD_GROUPS):
            lanes = slice(g * GROUP_WIDTH, (g + 1) * GROUP_WIDTH)
            bb = xc_ref[:, D_SSD + g * SSD_STATE:D_SSD + (g + 1) * SSD_STATE].astype(MXU_DTYPE)
            cb = xc_ref[:, D_SSD + (SSD_GROUPS + g) * SSD_STATE:D_SSD + (SSD_GROUPS + g + 1) * SSD_STATE].astype(MXU_DTYPE)
            scores = _dot_nt(cb, bb)
            sg = state[:, lanes]
            ys = []
            for j in range(g * pairs_per_group, (g + 1) * pairs_per_group):
                xg_pair = xg[:, j * 2 * HEAD_DIM:(j + 1) * 2 * HEAD_DIM]
                acc = jnp.zeros((CHUNK, 2 * HEAD_DIM), F32)
                for half in range(2):
                    m = (scores * _head_decay(acs_x, 2 * j + half, r, c, halves)).astype(MXU_DTYPE)
                    acc = acc + _dot(m, jnp.where(halves[half], xg_pair, 0.0).astype(MXU_DTYPE))
                ys.append(acc)
            y_ref[:, lanes] = jnp.concatenate(ys, axis=1) + _dot(cb, sg.astype(MXU_DTYPE)) * ea_x[:, lanes]
            state[:, lanes] = sg * eatot_x[:, lanes] + _dot_tn(bb, (xg[:, lanes] * ds_x[:, lanes]).astype(MXU_DTYPE))

    return pl.pallas_call(
        body, name=name, grid=(nc,),
        in_specs=[xc_s, dt_s, spread_s, vec_s, vec_s],
        out_specs=[wide_s, state_s],
        out_shape=[jax.ShapeDtypeStruct((l, D_SSD), F32), jax.ShapeDtypeStruct((nc, SSD_STATE, D_SSD), F32)],
        scratch_shapes=[pltpu.VMEM((SSD_STATE, D_SSD), F32)],
    )(xc, proj, spread_x, dtb_x, al_x)


def _ssd_bwd(xc, proj, spread_x, dtb_x, al_x, dy, states, name):
    l = xc.shape[0]
    nc = l // CHUNK
    xc_s, wide_s, dt_s, dt_out_s, spread_s, vec_s, state_s = _ssd_specs(nc, True)
    pairs_per_group = GROUP_WIDTH // (2 * HEAD_DIM)

    def body(xc_ref, dt_ref, sx_ref, dtbx_ref, alx_ref, dy_ref, sp_ref,
             dxc_ref, ddtr_ref, st_ref, hs_ref, dstate):
        @pl.when(pl.program_id(0) == 0)
        def _():
            dstate[...] = jnp.zeros_like(dstate)
            st_ref[...] = jnp.zeros_like(st_ref)

        r, c = _chunk_iota()
        halves = _half_masks()
        spread = sx_ref[...]

        def head_sums(v):
            return _split3_dot(_split3_dot_nt(v, spread), spread)

        raw_x, dt_x, a_x, acs_x, ea_x, ds_x, eatot_x = _ssd_decays(dt_ref, sx_ref, dtbx_ref, alx_ref, r, c)
        xs = xc_ref[:, 0:D_SSD]
        xg = xs * dt_x
        gy = dy_ref[...]
        s_prev = sp_ref[0]
        gea = gy * ea_x
        xds = xg * ds_x
        ds_old = dstate[...]
        later2 = jnp.concatenate([(r <= c).astype(BF16)] * 2, axis=1)
        dxg_parts, state_term, yoff_parts, dadt_parts = [], [], [], []
        for g in range(SSD_GROUPS):
            lanes = slice(g * GROUP_WIDTH, (g + 1) * GROUP_WIDTH)
            b_lo = D_SSD + g * SSD_STATE
            c_lo = D_SSD + (SSD_GROUPS + g) * SSD_STATE
            bb = xc_ref[:, b_lo:b_lo + SSD_STATE].astype(MXU_DTYPE)
            cb = xc_ref[:, c_lo:c_lo + SSD_STATE].astype(MXU_DTYPE)
            scores = _dot_nt(cb, bb)
            dsg = ds_old[:, lanes].astype(MXU_DTYPE)
            gea_b = gea[:, lanes].astype(MXU_DTYPE)
            xds_b = xds[:, lanes].astype(MXU_DTYPE)
            dxg_state = _dot(bb, dsg) * ds_x[:, lanes]
            dc = _dot_nt(gea_b, s_prev[:, lanes].astype(MXU_DTYPE))
            db = _dot_nt(xds_b, dsg)
            dscores = jnp.zeros((CHUNK, CHUNK), F32)
            diag = []
            for j in range(g * pairs_per_group, (g + 1) * pairs_per_group):
                pair = slice(j * 2 * HEAD_DIM, (j + 1) * 2 * HEAD_DIM)
                xg_pair = xg[:, pair].astype(MXU_DTYPE)
                acc = jnp.zeros((CHUNK, 2 * HEAD_DIM), F32)
                cols = []
                for half in range(2):
                    decay = _head_decay(acs_x, 2 * j + half, r, c, halves)
                    g_e = jnp.where(halves[half], gy[:, pair], 0.0).astype(MXU_DTYPE)
                    acc = acc + _dot_tn((scores * decay).astype(MXU_DTYPE), g_e)
                    dm = _dot_nt(g_e, xg_pair) * decay
                    dscores = dscores + dm
                    wq = dm * scores
                    hi = wq.astype(BF16)
                    lo = (wq - hi.astype(F32)).astype(BF16)
                    later = _dot(later2, jnp.concatenate([hi, lo], axis=0))
                    cols.append(jnp.sum(jnp.where(c < r, later, 0.0), axis=1, keepdims=True))
                diag.append(acc)
                dadt_parts.append(jnp.where(halves[0], cols[0], cols[1]))
            dsc_b = dscores.astype(MXU_DTYPE)
            dc = dc + _dot(dsc_b, bb)
            db = db + _dot_tn(dsc_b, cb)
            dxc_ref[:, b_lo:b_lo + SSD_STATE] = db
            dxc_ref[:, c_lo:c_lo + SSD_STATE] = dc
            dxg_parts.append(jnp.concatenate(diag, axis=1) + dxg_state)
            state_term.append(dxg_state)
            yoff_parts.append(_dot(cb, s_prev[:, lanes].astype(MXU_DTYPE)) * ea_x[:, lanes])
            dstate[:, lanes] = ds_old[:, lanes] * eatot_x[:, lanes] + _dot_tn(cb, gea_b)
        dxg = jnp.concatenate(dxg_parts, axis=1)
        dxc_ref[:, 0:D_SSD] = dxg * dt_x
        through_out = _tri_dot((r <= c).astype(BF16), gy * jnp.concatenate(yoff_parts, axis=1))
        through_in = _tri_dot((c < r).astype(BF16), xg * jnp.concatenate(state_term, axis=1))
        carried = jnp.broadcast_to(_colsum(ds_old * s_prev) * eatot_x, (8, D_SSD))
        dadt = (jnp.concatenate(dadt_parts, axis=1) + head_sums(through_out + through_in)
                + jnp.max(head_sums(carried), axis=0, keepdims=True))
        ddt = a_x * dadt + head_sums(dxg * xs)
        draw = ddt * _sigmoid(raw_x)
        ddtr_ref[...] = _split3_dot_nt(draw, spread) * (1.0 / HEAD_DIM)
        st_ref[0:1, :] += _colsum(dt_x * dadt) * a_x
        st_ref[1:2, :] += _colsum(draw)
        st_ref[2:3, :] += _colsum(gy * xs)

        @pl.when(pl.program_id(0) == nc - 1)
        def _():
            hs_ref[...] = head_sums(st_ref[...])

    stats = pl.BlockSpec((8, D_SSD), lambda i: (0, 0))
    return pl.pallas_call(
        body, name=name, grid=(nc,),
        in_specs=[xc_s, dt_s, spread_s, vec_s, vec_s, wide_s, state_s],
        out_specs=[xc_s, dt_out_s, stats, stats],
        out_shape=[jax.ShapeDtypeStruct((l, D_CONV), F32), jax.ShapeDtypeStruct((l, HEAD_TILE), F32),
                   jax.ShapeDtypeStruct((8, D_SSD), F32), jax.ShapeDtypeStruct((8, D_SSD), F32)],
        scratch_shapes=[pltpu.VMEM((SSD_STATE, D_SSD), F32)],
    )(xc, proj, spread_x, dtb_x, al_x, dy, states)


def _ssd_gate_norm(ycore, xc, z, dskip_x, norm_w, name):
    l = ycore.shape[0]
    tl = _pick(l, 512, 16)
    row = pl.BlockSpec((tl, D_SSD), lambda i: (i, 0))
    vec = pl.BlockSpec((1, D_SSD), lambda i: (0, 0))

    def body(y_ref, xs_ref, z_ref, dk_ref, nw_ref, o_ref):
        zv = z_ref[...]
        yv = (y_ref[...] + dk_ref[...] * xs_ref[...]) * (zv * _sigmoid(zv))
        for g in range(SSD_GROUPS):
            lanes = slice(g * GROUP_WIDTH, (g + 1) * GROUP_WIDTH)
            yg = yv[:, lanes]
            rg = lax.rsqrt(jnp.mean(yg * yg, axis=-1, keepdims=True) + EPS)
            o_ref[:, lanes] = (yg * rg * nw_ref[:, lanes]).astype(o_ref.dtype)

    z_spec = pl.BlockSpec((tl, D_SSD), lambda i: (i, O_Z // D_SSD))
    return pl.pallas_call(body, name=name, grid=(l // tl,), in_specs=[row, row, z_spec, vec, vec], out_specs=row,
                          out_shape=jax.ShapeDtypeStruct((l, D_SSD), MXU_DTYPE))(ycore, xc, z, dskip_x, norm_w)


def _ssd_gate_norm_bwd(dout, ycore, xc, z, dskip_x, norm_w, name):
    l = ycore.shape[0]
    tl = _pick(l, 512, 16)
    nt = l // tl
    row = pl.BlockSpec((tl, D_SSD), lambda i: (i, 0))
    vec = pl.BlockSpec((1, D_SSD), lambda i: (0, 0))

    def body(do_ref, y_ref, xs_ref, z_ref, dk_ref, nw_ref, dyc_ref, dxs_ref, dz_ref, st_ref):
        @pl.when(pl.program_id(0) == 0)
        def _():
            st_ref[...] = jnp.zeros_like(st_ref)

        zv, xs = z_ref[...], xs_ref[...]
        s = _sigmoid(zv)
        gz = zv * s
        yc = y_ref[...] + dk_ref[...] * xs
        yv = yc * gz
        dov = do_ref[...]
        dnw, dyv = [], []
        for g in range(SSD_GROUPS):
            lanes = slice(g * GROUP_WIDTH, (g + 1) * GROUP_WIDTH)
            yg = yv[:, lanes]
            rg = lax.rsqrt(jnp.mean(yg * yg, axis=-1, keepdims=True) + EPS)
            yn = yg * rg
            dnw.append(_colsum(dov[:, lanes] * yn))
            dyn = dov[:, lanes] * nw_ref[:, lanes]
            dyv.append(rg * (dyn - yn * jnp.mean(dyn * yn, axis=-1, keepdims=True)))
        dy = jnp.concatenate(dyv, axis=1)
        dyc = dy * gz
        dyc_ref[...] = dyc
        dxs_ref[...] = dyc * dk_ref[...]
        dz_ref[...] = dy * yc * (s * (1.0 + zv * (1.0 - s)))
        st_ref[0:1, :] += jnp.concatenate(dnw, axis=1)

    return pl.pallas_call(
        body, name=name, grid=(nt,), in_specs=[row, row, row, pl.BlockSpec((tl, D_SSD), lambda i: (i, O_Z // D_SSD)), vec, vec],
        out_specs=[row, row, row, pl.BlockSpec((8, D_SSD), lambda i: (0, 0))],
        out_shape=[jax.ShapeDtypeStruct((l, D_SSD), F32)] * 3 + [jax.ShapeDtypeStruct((8, D_SSD), F32)],
    )(dout, ycore, xc, z, dskip_x, norm_w)


PAIR = 2 * HEAD_DIM
N_PAIRS = N_HEADS // 2
Q_TILE0 = O_Q // PAIR
SB_KEYS = 1024
SB_PAIRS = 4
SB_PAIRS_FWD = 4
SB_SCALE = HEAD_DIM ** -0.5


def _pair_sum(v, lo):
    s_lo = jnp.sum(jnp.where(lo, v, 0.0), axis=-1, keepdims=True)
    s_hi = jnp.sum(jnp.where(lo, 0.0, v), axis=-1, keepdims=True)
    return jnp.where(lo, s_lo, s_hi)


def _qkv_prep(proj, qw2, kw2, name):
    l = proj.shape[0]
    tl = _pick(l, 1024, 16)
    out = pl.BlockSpec((tl, PAIR), lambda i, j: (i, j))
    vec = pl.BlockSpec((1, PAIR), lambda i, j: (0, 0))

    def at(which):
        return pl.BlockSpec((tl, PAIR), lambda i, j: (i, Q_TILE0 + which * N_PAIRS + j))

    def body(q_ref, k_ref, v_ref, qw_ref, kw_ref, qn_ref, kn_ref, vb_ref):
        lo = lax.broadcasted_iota(jnp.int32, (tl, PAIR), 1) < HEAD_DIM
        for t_ref, w_ref, o_ref, scale in ((q_ref, qw_ref, qn_ref, SB_SCALE), (k_ref, kw_ref, kn_ref, 1.0)):
            tv = t_ref[...]
            r = lax.rsqrt(_pair_sum(tv * tv, lo) * (1.0 / HEAD_DIM) + EPS)
            o_ref[...] = ((tv * r) * w_ref[...] * scale).astype(o_ref.dtype)
        vb_ref[...] = v_ref[...].astype(vb_ref.dtype)

    return pl.pallas_call(body, name=name, grid=(l // tl, N_PAIRS), in_specs=[at(0), at(1), at(2), vec, vec],
                          out_specs=[out, out, out], out_shape=[jax.ShapeDtypeStruct((l, D_SB), MXU_DTYPE)] * 3,
                          )(proj, proj, proj, qw2, kw2)


def _qk_norm_bwd(proj, dqn, dkn, qw2, kw2, name):
    l = proj.shape[0]
    tl = _pick(l, 1024, 16)
    out = pl.BlockSpec((tl, PAIR), lambda i, j: (i, j))
    vec = pl.BlockSpec((1, PAIR), lambda i, j: (0, 0))
    stats = pl.BlockSpec((8, PAIR), lambda i, j: (0, 0))

    def at(which):
        return pl.BlockSpec((tl, PAIR), lambda i, j: (i, Q_TILE0 + which * N_PAIRS + j))

    def body(q_ref, k_ref, dqn_ref, dkn_ref, qw_ref, kw_ref, dq_ref, dk_ref, stq_ref, stk_ref):
        @pl.when((pl.program_id(0) == 0) & (pl.program_id(1) == 0))
        def _():
            stq_ref[...] = jnp.zeros_like(stq_ref)
            stk_ref[...] = jnp.zeros_like(stk_ref)

        lo = lax.broadcasted_iota(jnp.int32, (tl, PAIR), 1) < HEAD_DIM
        for t_ref, dn_ref, w_ref, d_ref, st_ref, scale in ((q_ref, dqn_ref, qw_ref, dq_ref, stq_ref, SB_SCALE),
                                                           (k_ref, dkn_ref, kw_ref, dk_ref, stk_ref, 1.0)):
            tv = t_ref[...]
            r = lax.rsqrt(_pair_sum(tv * tv, lo) * (1.0 / HEAD_DIM) + EPS)
            tn = tv * r
            dnv = dn_ref[...] * scale
            dtn = dnv * w_ref[...]
            d_ref[...] = r * (dtn - tn * (_pair_sum(dtn * tn, lo) * (1.0 / HEAD_DIM)))
            st_ref[0:1, :] += _colsum(dnv * tn)

    return pl.pallas_call(body, name=name, grid=(l // tl, N_PAIRS), in_specs=[at(0), at(1), out, out, vec, vec],
                          out_specs=[out, out, stats, stats],
                          out_shape=[jax.ShapeDtypeStruct((l, D_SB), F32)] * 2 + [jax.ShapeDtypeStruct((8, PAIR), F32)] * 2,
                          )(proj, proj, dqn, dkn, qw2, kw2)


def _sb_masks():
    r = lax.broadcasted_iota(jnp.int32, (CHUNK, CHUNK), 0)
    c = lax.broadcasted_iota(jnp.int32, (CHUNK, CHUNK), 1)
    return r, c


def _stack2(mask):
    t = mask.astype(BF16)
    return jnp.concatenate([t, t], axis=0)


def _sb_fwd(q, k, v, shards, name):
    l = q.shape[0]
    nq = l // CHUNK
    kt = _pick(l, SB_KEYS, CHUNK)
    sub = kt // CHUNK
    pairs = SB_PAIRS_FWD
    heads = 2 * pairs
    steps = N_PAIRS // pairs
    na = len(shards)
    qblk = pl.BlockSpec((CHUNK, pairs * PAIR), lambda i, j: (j, i))
    full = pl.BlockSpec((l, pairs * PAIR), lambda i, j: (0, i))
    hbm = pl.BlockSpec(memory_space=pl.ANY)

    def body(q_ref, k_ref, v_ref, *rest):
        o_ref, tot_ref = rest[na:na + 2]
        exchange = _DirectExchange("gather", rest[:na], rest[na + 2:2 * na + 2], *rest[2 * na + 2:])
        qb = pl.program_id(1)

        @pl.when((pl.program_id(0) == 0) & (qb == 0))
        def _():
            exchange.start()

        r, c = _sb_masks()
        after2 = _stack2(r > c)
        halves = _half_masks()
        zero = jnp.zeros((CHUNK, PAIR), q_ref.dtype)
        lanes = [slice((a // 2) * PAIR, (a // 2 + 1) * PAIR) for a in range(heads)]
        qm = [jnp.where(halves[a % 2], q_ref[:, lanes[a]], zero) for a in range(heads)]
        last = (qb * CHUNK) // kt

        def tile(t, width, carries, accs, masked):
            rows = pl.ds(pl.multiple_of(t * width, width), width)
            if masked:
                key_minus_query = (lax.broadcasted_iota(jnp.int32, (CHUNK, width), 1)
                                   - lax.broadcasted_iota(jnp.int32, (CHUNK, width), 0))
                keep = key_minus_query < qb * CHUNK - t * width

            def logits(a):
                lg = _dot_nt(qm[a], k_ref[rows, lanes[a]])
                sp = _softplus_logits(lg)
                return lg - sp, (jnp.where(keep, -sp, 0.0) if masked else -sp)

            def sums(a, lr):
                offset, parts = carries[a], [None] * (width // CHUNK)
                for j in reversed(range(width // CHUNK)):
                    piece = lr[:, j * CHUNK:(j + 1) * CHUNK]
                    parts[j] = _split_dot(piece, after2) + offset
                    offset = offset + jnp.sum(piece, axis=1, keepdims=True)
                return jnp.concatenate(parts, axis=1), offset

            def output(a, ls, cs):
                w = jnp.exp(ls + cs)
                if masked:
                    w = jnp.where(keep, w, 0.0)
                return accs[a] + _dot(w.astype(MXU_DTYPE), v_ref[rows, lanes[a]])

            new_carries, new_accs = [None] * heads, [None] * heads
            ls, lr = logits(0)
            for a in range(heads):
                cs, new_carries[a] = sums(a, lr)
                if a + 1 < heads:
                    ls_next, lr = logits(a + 1)
                new_accs[a] = output(a, ls, cs)
                ls = ls_next
            return tuple(new_carries), tuple(new_accs)

        carries = tuple(jnp.zeros((CHUNK, 1), F32) for _ in range(heads))
        accs = tuple(jnp.zeros((CHUNK, PAIR), F32) for _ in range(heads))
        if sub % 2 == 0:
            half = kt // 2
            carries, accs = lax.cond((qb * CHUNK) % kt >= half,
                                     lambda cr, ac: tile(2 * last + 1, half, cr, ac, True), lambda cr, ac: (cr, ac), carries, accs)
            carries, accs = lax.cond((qb * CHUNK) % kt >= half,
                                     lambda cr, ac: tile(2 * last, half, cr, ac, False), lambda cr, ac: tile(2 * last, half, cr, ac, True),
                                     carries, accs)
        else:
            carries, accs = tile(last, kt, carries, accs, True)
        carries, accs = lax.fori_loop(1, last + 1, lambda i, st: tile(last - i, kt, st[0], st[1], False), (carries, accs))
        for p in range(pairs):
            o_ref[:, lanes[2 * p]] = jnp.where(halves[0], accs[2 * p], accs[2 * p + 1])
            tot_ref[:, lanes[2 * p]] = jnp.where(halves[0], carries[2 * p], carries[2 * p + 1])

        @pl.when((pl.program_id(0) == steps - 1) & (qb == nq - 1))
        def _():
            exchange.wait()

    return pl.pallas_call(
        body, name=name, grid=(steps, nq), in_specs=[qblk, full, full] + [hbm] * na,
        out_specs=[qblk, qblk] + [hbm] * na,
        out_shape=[jax.ShapeDtypeStruct((l, D_SB), F32), jax.ShapeDtypeStruct((l, D_SB), F32)]
        + [jax.ShapeDtypeStruct((N_DEV,) + s.shape, s.dtype) for s in shards],
        scratch_shapes=_DirectExchange.scratch(na),
    )(q, k, v, *shards)


def _sb_bwd(q, k, v, dycat, tot, blocks, name):
    l = q.shape[0]
    nq = l // CHUNK
    kt = _pick(l, SB_KEYS, CHUNK)
    sub = kt // CHUNK
    heads = 2 * SB_PAIRS
    steps = N_PAIRS // SB_PAIRS
    na = len(blocks)
    width_all = SB_PAIRS * PAIR
    qblk = pl.BlockSpec((CHUNK, width_all), lambda i, j: (j, i))
    doblk = pl.BlockSpec((CHUNK, width_all), lambda i, j: (j, D_SSD // width_all + i))
    full = pl.BlockSpec((l, width_all), lambda i, j: (0, i))
    resident = pl.BlockSpec((l, width_all), lambda i, j: (0, i), pipeline_mode=pl.Buffered(1))
    hbm = pl.BlockSpec(memory_space=pl.ANY)

    def body(q_ref, k_ref, v_ref, do_ref, tot_ref, *rest):
        dq_ref, dk_ref, dv_ref = rest[na:na + 3]
        exchange = _DirectExchange("scatter", rest[:na], rest[na + 3:2 * na + 3], *rest[2 * na + 3:])
        qb = pl.program_id(1)

        @pl.when((pl.program_id(0) == 0) & (qb == 0))
        def _():
            exchange.start()

        @pl.when(qb == 0)
        def _():
            dk_ref[...] = jnp.zeros_like(dk_ref)
            dv_ref[...] = jnp.zeros_like(dv_ref)

        r, c = _sb_masks()
        upto1 = (r <= c).astype(BF16)
        upto2 = jnp.concatenate([upto1, upto1], axis=0)
        halves = _half_masks()
        lanes = [slice((a // 2) * PAIR, (a // 2 + 1) * PAIR) for a in range(heads)]
        qm = [jnp.where(halves[a % 2], q_ref[:, lanes[a]], jnp.zeros((CHUNK, PAIR), q_ref.dtype)) for a in range(heads)]
        dom = [jnp.where(halves[a % 2], do_ref[:, lanes[a]], 0.0).astype(MXU_DTYPE) for a in range(heads)]
        total = [jnp.max(jnp.where(halves[a % 2], tot_ref[:, lanes[a]], -jnp.inf), axis=1, keepdims=True) for a in range(heads)]
        last = (qb * CHUNK) // kt

        def prefix(values, offset, sign, exact=True):
            parts = []
            for j in range(values.shape[1] // CHUNK):
                piece = values[:, j * CHUNK:(j + 1) * CHUNK]
                within = _split_dot(piece, upto2) if exact else _dot(piece.astype(BF16), upto1)
                parts.append(offset + within if sign > 0 else offset - within)
                step = jnp.sum(piece, axis=1, keepdims=True)
                offset = offset + step if sign > 0 else offset - step
            return jnp.concatenate(parts, axis=1), offset

        def tile(t, width, carry_p, carry_d, dq, masked):
            rows = pl.ds(pl.multiple_of(t * width, width), width)
            if masked:
                key_minus_query = (lax.broadcasted_iota(jnp.int32, (CHUNK, width), 1)
                                   - lax.broadcasted_iota(jnp.int32, (CHUNK, width), 0))
                keep = key_minus_query < qb * CHUNK - t * width

            def log_terms(a):
                lg = _dot_nt(qm[a], k_ref[rows, lanes[a]])
                sp = _softplus_logits(lg)
                return lg - sp, (jnp.where(keep, -sp, 0.0) if masked else -sp)

            def weights(a, ls, lr):
                remaining, p_next = prefix(lr, carry_p[a], -1)
                w = jnp.exp(ls + remaining)
                if masked:
                    w = jnp.where(keep, w, 0.0)
                return w, p_next

            def weight_grads(a, w):
                return _dot_nt(dom[a], v_ref[rows, lanes[a]]) * w

            def gradients(a, ls, w, da):
                d_incl, d_next = prefix(da, carry_d[a], 1, exact=False)
                sig = jnp.exp(ls)
                dl = da - sig * d_incl
                if masked:
                    dl = jnp.where(keep, dl, 0.0)
                dl_b = dl.astype(MXU_DTYPE)
                return (d_next, dq[a] + _dot(dl_b, k_ref[rows, lanes[a]]),
                        _dot_tn(w.astype(MXU_DTYPE), dom[a]), _dot_tn(dl_b, qm[a]))

            new_p, new_d, new_dq = [None] * heads, [None] * heads, [None] * heads
            dv_upd, dk_upd = [None] * heads, [None] * heads
            ls, lr = log_terms(0)
            w, new_p[0] = weights(0, ls, lr)
            for a in range(heads):
                if a + 1 < heads:
                    ls_next, lr_next = log_terms(a + 1)
                da = weight_grads(a, w)
                if a + 1 < heads:
                    w_next, new_p[a + 1] = weights(a + 1, ls_next, lr_next)
                new_d[a], new_dq[a], dv_upd[a], dk_upd[a] = gradients(a, ls, w, da)
                if a % 2 == 1:
                    dv_ref[rows, lanes[a]] += dv_upd[a - 1] + dv_upd[a]
                    dk_ref[rows, lanes[a]] += dk_upd[a - 1] + dk_upd[a]
                if a + 1 < heads:
                    ls, w = ls_next, w_next
            return tuple(new_p), tuple(new_d), tuple(new_dq)

        zeros = tuple(jnp.zeros((CHUNK, 1), F32) for _ in range(heads))
        dq0 = tuple(jnp.zeros((CHUNK, PAIR), F32) for _ in range(heads))
        state = lax.fori_loop(0, last, lambda t, st: tile(t, kt, st[0], st[1], st[2], False), (tuple(total), zeros, dq0))
        if sub % 2 == 0:
            half = kt // 2
            state = lax.cond((qb * CHUNK) % kt >= half,
                             lambda cp, cd, dq: tile(2 * last, half, cp, cd, dq, False),
                             lambda cp, cd, dq: tile(2 * last, half, cp, cd, dq, True), *state)
            state = lax.cond((qb * CHUNK) % kt >= half,
                             lambda cp, cd, dq: tile(2 * last + 1, half, cp, cd, dq, True), lambda cp, cd, dq: (cp, cd, dq), *state)
        else:
            state = tile(last, kt, *state, True)
        for p in range(SB_PAIRS):
            dq_ref[:, lanes[2 * p]] = jnp.where(halves[0], state[2][2 * p], state[2][2 * p + 1])

        @pl.when((pl.program_id(0) == steps - 1) & (qb == nq - 1))
        def _():
            exchange.wait()

    return pl.pallas_call(
        body, name=name, grid=(steps, nq),
        in_specs=[qblk, full, full, doblk, qblk] + [hbm] * na,
        out_specs=[qblk, resident, resident] + [hbm] * na,
        out_shape=[jax.ShapeDtypeStruct((l, D_SB), F32)] * 3 + [jax.ShapeDtypeStruct(b.shape, b.dtype) for b in blocks],
        scratch_shapes=_DirectExchange.scratch(na),
    )(q, k, v, dycat, tot, *blocks)


def _ada_bwd(c_all, dmod_shard, name):
    def body(c_ref, d_ref, o_ref):
        cv = c_ref[...]
        o_ref[...] = _dot_tn(cv * _sigmoid(cv), d_ref[...], HIGHEST)

    return pl.pallas_call(body, name=name, out_shape=jax.ShapeDtypeStruct((c_all.shape[1], dmod_shard.shape[1]), F32))(c_all, dmod_shard)


def _sum_small(parts, name):
    def body(p_ref, o_ref):
        acc = p_ref[0]
        for d in range(1, N_DEV):
            acc = acc + p_ref[d]
        o_ref[...] = acc

    return pl.pallas_call(body, name=name, out_shape=jax.ShapeDtypeStruct(parts.shape[1:], F32))(parts)


def _adamw(w, g, m, v, name):
    rows, cols = w.shape
    tr = _pick(rows, 256, 8)
    spec = pl.BlockSpec((tr, cols), lambda i: (i, 0))
    bc1 = 1.0 - ADAM_B1 ** ADAM_STEP
    bc2 = 1.0 - ADAM_B2 ** ADAM_STEP

    def body(w_ref, g_ref, m_ref, v_ref, d_ref, nm_ref, nv_ref):
        gv = g_ref[...]
        nm = ADAM_B1 * m_ref[...] + (1.0 - ADAM_B1) * gv
        nv = ADAM_B2 * v_ref[...] + (1.0 - ADAM_B2) * (gv * gv)
        nm_ref[...] = nm
        nv_ref[...] = nv
        d_ref[...] = -ADAM_LR * ((nm / bc1) / (jnp.sqrt(nv / bc2) + ADAM_EPS) + ADAM_WD * w_ref[...])

    return pl.pallas_call(body, name=name, grid=(rows // tr,), in_specs=[spec] * 4, out_specs=[spec] * 3,
                          out_shape=[jax.ShapeDtypeStruct((rows, cols), F32)] * 3)(w, g, m, v)


def _pad_w_in_t(w_in_t):
    lo = D_SSD + D_CONV
    return jnp.concatenate([w_in_t[D_SSD:lo + N_HEADS], jnp.zeros((DT_PAD - N_HEADS, w_in_t.shape[1]), w_in_t.dtype),
                            w_in_t[:D_SSD], w_in_t[lo + N_HEADS:]], axis=0)


def _local_step(x, target, mod, norm1_w, w_in_tp, conv_w, conv_b, dt_bias, a_log, d_skip, ssd_norm_w, q_norm_w, k_norm_w,
                norm2_w, later_shards):
    sh1, sc1, g1, sh2, sc2, g2 = [mod[:, i * D_MODEL:(i + 1) * D_MODEL] for i in range(N_MOD)]
    qw2, kw2 = jnp.tile(q_norm_w, (1, 2)), jnp.tile(k_norm_w, (1, 2))
    dskip_x = jnp.repeat(d_skip, HEAD_DIM, axis=1)
    dtb_x, al_x = jnp.repeat(dt_bias, HEAD_DIM, axis=1), jnp.repeat(a_log, HEAD_DIM, axis=1)
    head_ids = jnp.arange(HEAD_TILE, dtype=jnp.int32)[:, None]
    spread_x = (head_ids == jnp.arange(D_SSD, dtype=jnp.int32)[None, :] // HEAD_DIM).astype(BF16)

    h1 = _rms_mod(x, norm1_w, sc1, sh1, "rms_mod1")
    proj = _matmul_nt(h1, w_in_tp, "in_proj")

    xc = _conv_silu(proj, conv_w, conv_b, "conv_silu")
    ycore, states = _ssd_fwd(xc, proj, spread_x, dtb_x, al_x, "ssd_fwd")
    y_ssd = _ssd_gate_norm(ycore, xc, proj, dskip_x, ssd_norm_w, "ssd_gate_norm")

    qn, kn, vb = _qkv_prep(proj, qw2, kw2, "qkv_prep")
    o_sb, tot, *gathered = _sb_fwd(qn, kn, vb, later_shards, "sb_fwd")
    w_out, w_gate_t, w_up_t, w_down = [g.reshape(N_DEV * g.shape[1], D_MODEL) for g in gathered]
    ycat = jnp.concatenate([y_ssd, o_sb.astype(MXU_DTYPE)], axis=1)

    mix = _matmul(ycat, w_out, "out_proj")
    x1, h2 = _residual_rms_mod(x, mix, g1, norm2_w, sc2, sh2, "residual_rms_mod2")
    gate, up, act = _ffn_in(h2, w_gate_t, w_up_t, "ffn_in")
    ffn = _matmul(act, w_down, "ffn_down")
    dy, dffn, st_g2, loss_blk = _loss_head(x1, ffn, g2, target, "loss_head")

    g_down = _matmul_tn(act, dffn, "ffn_down_dw", MXU_DTYPE)
    dgate, dup = _ffn_down_bwd(dffn, w_down, gate, up, "ffn_down_dx")
    g_gate_t = _matmul_tn(dgate, h2, "ffn_gate_dw", MXU_DTYPE)
    g_up_t = _matmul_tn(dup, h2, "ffn_up_dw", MXU_DTYPE)
    dh2 = _matmul_sum2(dgate, w_gate_t, dup, w_up_t, "ffn_dh")
    dx1, dmix, st_n2 = _norm_bwd(x1, dh2, dy, norm2_w, sc2, "norm2_bwd", gated=(mix, g1))

    g_out = _matmul_tn(ycat, dmix, "out_proj_dw", MXU_DTYPE)
    dycat = _matmul_nt(dmix, w_out, "out_proj_dx")

    partials = [g.reshape(N_DEV, g.shape[0] // N_DEV, D_MODEL).astype(BF16) for g in (g_out, g_gate_t, g_up_t, g_down)]
    dqn, dkn, dv, *slots = _sb_bwd(qn, kn, vb, dycat, tot, partials, "sb_bwd")
    dq, dk, st_q, st_k = _qk_norm_bwd(proj, dqn, dkn, qw2, kw2, "qk_norm_bwd")

    dycore, dxs_skip, dz, st_gn = _ssd_gate_norm_bwd(dycat, ycore, xc, proj, dskip_x, ssd_norm_w, "ssd_gate_norm_bwd")
    dxc, ddt_tile, st_ssd, st_heads = _ssd_bwd(xc, proj, spread_x, dtb_x, al_x, dycore, states, "ssd_bwd")
    dxc = jnp.concatenate([dxc[:, :D_SSD] + dxs_skip, dxc[:, D_SSD:]], axis=1)
    dpre, st_conv = _conv_silu_bwd_pre(proj, dxc, conv_w, conv_b, "conv_silu_bwd")
    dxbc = _conv_bwd_input(dpre, conv_w, "conv_bwd_input")

    pieces = {"z": (dz, O_Z), "xbc": (dxbc, O_XBC), "dt": (ddt_tile, O_DT), "q": (dq, O_Q), "k": (dk, O_Q + D_SB), "v": (dv, O_Q + 2 * D_SB)}
    g_rows = {n: _matmul_tn(p, h1, "in_proj_dw_" + n, MXU_DTYPE) for n, (p, _) in pieces.items()}
    g_rows["dt"] = g_rows["dt"][:N_HEADS]
    g_in_t = jnp.concatenate(list(g_rows.values()), axis=0)
    g_in_blocks = g_in_t.reshape(N_DEV, g_in_t.shape[0] // N_DEV, D_MODEL).astype(BF16)
    dh1, g_in_slots = _matmul_pieces_beside_scatter([p for p, _ in pieces.values()], [o for _, o in pieces.values()], w_in_tp,
                                                    [g_in_blocks], "in_proj_dx")
    slots = [g_in_slots] + list(slots)
    grad_x, st_n1 = _norm_bwd(x, dh1, dx1, norm1_w, sc1, "norm1_bwd")

    pad = jnp.zeros((1, SM_SSD_NORM - SM_D_SKIP - N_HEADS), F32)
    small = jnp.concatenate(
        [st_n1[1:2], st_n1[0:1], st_n2[3:4], st_n2[1:2], st_n2[0:1], st_g2[0:1],
         st_n1[2:3], st_conv[4:5], st_conv[0:4].reshape(1, CONV_WIDTH * D_CONV),
         st_ssd[1:2, ::HEAD_DIM], st_ssd[0:1, ::HEAD_DIM], st_heads[2:3, ::HEAD_DIM], pad,
         st_gn[0:1], st_q[0:1, :HEAD_DIM] + st_q[0:1, HEAD_DIM:], st_k[0:1, :HEAD_DIM] + st_k[0:1, HEAD_DIM:], st_n2[2:3]], axis=1)
    return loss_blk, grad_x, slots, small


def kernel(x, c, w_ada, b_ada, norm1_w, w_in, conv_w, conv_b, dt_bias, a_log, d_skip, ssd_norm_w, q_norm_w, k_norm_w, w_out, norm2_w, w_gate, w_up, w_down, loss_target, m_w_ada, m_b_ada, m_norm1_w, m_w_in, m_conv_w, m_conv_b, m_dt_bias, m_a_log, m_d_skip, m_ssd_norm_w, m_q_norm_w, m_k_norm_w, m_w_out, m_norm2_w, m_w_gate, m_w_up, m_w_down, v_w_ada, v_b_ada, v_norm1_w, v_w_in, v_conv_w, v_conv_b, v_dt_bias, v_a_log, v_d_skip, v_ssd_norm_w, v_q_norm_w, v_k_norm_w, v_w_out, v_norm2_w, v_w_gate, v_w_up, v_w_down):
    me = 4 * lax.axis_index("x") + 2 * lax.axis_index("y") + lax.axis_index("c")
    conv_cols = D_CONV // N_DEV
    ada_cols = N_MOD * D_MODEL // N_DEV

    cond_conv = jnp.concatenate([c, conv_w[0].reshape(1, CONV_WIDTH * conv_cols)], axis=1)
    first, mods, w_in_t = _prologue(cond_conv, w_ada[0], w_in[0].T.astype(BF16), "prologue")
    w_in_tp = _pad_w_in_t(w_in_t.reshape(D_IN_PROJ, D_MODEL))
    later_shards = [s.astype(BF16) for s in (w_out[0], w_gate[0].T, w_up[0].T, w_down[0])]
    c_all = first[:, 0, :D_MODEL]
    conv_w_f = first[:, 0, D_MODEL:].reshape(N_DEV, CONV_WIDTH, conv_cols).transpose(1, 0, 2).reshape(CONV_WIDTH, D_CONV)
    mod = lax.dynamic_index_in_dim(mods, me, axis=1, keepdims=False)
    mod = mod.reshape(1, N_MOD * D_MODEL) + b_ada

    loss_blk, grad_x, slots, small = _local_step(
        x[0], loss_target[0], mod, norm1_w, w_in_tp, conv_w_f, conv_b, dt_bias, a_log, d_skip, ssd_norm_w, q_norm_w, k_norm_w,
        norm2_w, later_shards)
    loss = lax.psum(loss_blk[0, 0], ("x", "y", "c"))

    g_in_t, g_out, g_gate_t, g_up_t, g_down = [_sum_slots(s, "sum_grads_" + n) for s, n in zip(slots, ("in", "out", "gate", "up", "down"))]
    g_in, g_gate, g_up = g_in_t.T, g_gate_t.T, g_up_t.T

    parts = _all_gather_small(small, "gather_small")
    gsum = _sum_small(parts, "sum_small")
    dmod_shard = lax.dynamic_slice_in_dim(parts[:, 0, :N_MOD * D_MODEL], me * ada_cols, ada_cols, axis=1)
    g_ada = _ada_bwd(c_all, dmod_shard, "ada_bwd")
    g_conv_w = lax.dynamic_slice_in_dim(gsum[:, SM_CONV_W:SM_DT_BIAS].reshape(CONV_WIDTH, D_CONV), me * conv_cols, conv_cols, axis=1)

    def pack_small(b_ada_, norm1_, conv_b_, dt_bias_, a_log_, d_skip_, ssd_norm_, q_norm_, k_norm_, norm2_):
        return jnp.concatenate(
            [b_ada_, norm1_, conv_b_, jnp.zeros((1, CONV_WIDTH * D_CONV), F32), dt_bias_, a_log_, d_skip_,
             jnp.zeros((1, SM_SSD_NORM - SM_D_SKIP - N_HEADS), F32), ssd_norm_, q_norm_, k_norm_, norm2_], axis=1)

    def unpack_small(p):
        return {
            "b_ada": p[:, SM_B_ADA:SM_NORM1], "norm1_w": p[:, SM_NORM1:SM_CONV_B], "conv_b": p[:, SM_CONV_B:SM_CONV_W],
            "dt_bias": p[:, SM_DT_BIAS:SM_A_LOG], "a_log": p[:, SM_A_LOG:SM_D_SKIP], "d_skip": p[:, SM_D_SKIP:SM_D_SKIP + N_HEADS],
            "ssd_norm_w": p[:, SM_SSD_NORM:SM_Q_NORM], "q_norm_w": p[:, SM_Q_NORM:SM_K_NORM], "k_norm_w": p[:, SM_K_NORM:SM_NORM2],
            "norm2_w": p[:, SM_NORM2:SM_TOTAL]}

    w_small = pack_small(b_ada, norm1_w, conv_b, dt_bias, a_log, d_skip, ssd_norm_w, q_norm_w, k_norm_w, norm2_w)
    m_small = pack_small(m_b_ada, m_norm1_w, m_conv_b, m_dt_bias, m_a_log, m_d_skip, m_ssd_norm_w, m_q_norm_w, m_k_norm_w, m_norm2_w)
    v_small = pack_small(v_b_ada, v_norm1_w, v_conv_b, v_dt_bias, v_a_log, v_d_skip, v_ssd_norm_w, v_q_norm_w, v_k_norm_w, v_norm2_w)
    small_out = [unpack_small(t) for t in (gsum,) + tuple(_adamw(w_small, gsum, m_small, v_small, "adamw_small"))]

    sharded = {
        "w_ada": (w_ada[0], g_ada, m_w_ada[0], v_w_ada[0]),
        "w_in": (w_in[0], g_in, m_w_in[0], v_w_in[0]),
        "conv_w": (conv_w[0], g_conv_w, m_conv_w[0], v_conv_w[0]),
        "w_out": (w_out[0], g_out, m_w_out[0], v_w_out[0]),
        "w_gate": (w_gate[0], g_gate, m_w_gate[0], v_w_gate[0]),
        "w_up": (w_up[0], g_up, m_w_up[0], v_w_up[0]),
        "w_down": (w_down[0], g_down, m_w_down[0], v_w_down[0]),
    }
    sharded_out = {n: (t[1],) + tuple(_adamw(*t, "adamw_" + n)) for n, t in sharded.items()}

    names = ["w_ada", "b_ada", "norm1_w", "w_in", "conv_w", "conv_b", "dt_bias", "a_log", "d_skip", "ssd_norm_w", "q_norm_w",
             "k_norm_w", "w_out", "norm2_w", "w_gate", "w_up", "w_down"]
    outs = [loss, grad_x[None]]
    for kind in range(4):
        for n in names:
            outs.append(sharded_out[n][kind][None] if n in sharded_out else small_out[kind][n])
    return tuple(outs)
```

```python
import functools

import jax
import jax.numpy as jnp
from jax import lax
from jax.experimental import pallas as pl
from jax.experimental.pallas import tpu as pltpu

F32 = jnp.float32
BF16 = jnp.bfloat16
MXU_DTYPE = jnp.bfloat16
HIGHEST = lax.Precision.HIGHEST
MESH_IDS = pl.DeviceIdType.MESH

N_DEV = 8
D_MODEL = 1024
HEAD_DIM = 64
N_HEADS = 16
D_SSD = 1024
D_SB = 1024
SSD_GROUPS = 2
SSD_STATE = 128
GROUP_WIDTH = D_SSD // SSD_GROUPS
D_CONV = D_SSD + 2 * SSD_GROUPS * SSD_STATE
CONV_WIDTH = 4
CHUNK = 128
D_FF = 2816
N_MOD = 6
EPS = 1e-6
D_IN_PROJ = 5648
O_XBC = 0
O_DT = D_CONV
DT_PAD = 512
O_Z = O_DT + DT_PAD
O_Q = O_Z + D_SSD
D_IN_PAD = O_Q + 3 * D_SB
HALO = 8

ADAM_LR = 0.001
ADAM_B1 = 0.9
ADAM_B2 = 0.999
ADAM_EPS = 1e-08
ADAM_WD = 0.01
ADAM_STEP = 10

SM_B_ADA = 0
SM_NORM1 = 6144
SM_CONV_B = 7168
SM_CONV_W = 8704
SM_DT_BIAS = 14848
SM_A_LOG = 14864
SM_D_SKIP = 14880
SM_SSD_NORM = 14976
SM_Q_NORM = 16000
SM_K_NORM = 16064
SM_NORM2 = 16128
SM_TOTAL = 17152


def _pick(n, cap, mult):
    if n <= cap:
        return n
    best = None
    for t in range(mult, cap + 1, mult):
        if n % t == 0:
            best = t
    assert best is not None, (n, cap, mult)
    return best


def _dot(a, b, precision=None):
    return jnp.dot(a, b, preferred_element_type=F32, precision=precision)


def _dot_nt(a, b, precision=None):
    return lax.dot_general(a, b, (((1,), (1,)), ((), ())), preferred_element_type=F32, precision=precision)


def _dot_tn(a, b, precision=None):
    return lax.dot_general(a, b, (((0,), (0,)), ((), ())), preferred_element_type=F32, precision=precision)


def _softplus(v):
    return jnp.maximum(v, 0.0) + jnp.log1p(jnp.exp(-jnp.abs(v)))


def _softplus_logits(v):
    return jnp.maximum(v, 0.0) + jnp.log(1.0 + jnp.exp(-jnp.abs(v)))


def _tri_dot(tri_b, v):
    hi = v.astype(BF16)
    r1 = v - hi.astype(F32)
    mid = r1.astype(BF16)
    lo = (r1 - mid.astype(F32)).astype(BF16)
    return _dot(tri_b, hi) + _dot(tri_b, mid) + _dot(tri_b, lo)


def _sigmoid(v):
    return jax.nn.sigmoid(v)


def _colsum(v):
    return jnp.sum(v, axis=0, keepdims=True)


def _split_dot(v, tri2):
    hi = v.astype(BF16)
    lo = (v - hi.astype(F32)).astype(BF16)
    return _dot(jnp.concatenate([hi, lo], axis=1), tri2)


def _position():
    x, y, c = lax.axis_index("x"), lax.axis_index("y"), lax.axis_index("c")
    return x, y, c


def _peer(x, y, c, k):
    px = 1 - x if (k >> 2) & 1 else x
    py = 1 - y if (k >> 1) & 1 else y
    pc = 1 - c if k & 1 else c
    return px, py, pc


def _all_gather_small(v, name):
    n = v.shape[1]

    def body(v_ref, out_ref, send_sems, recv_sems, local_sem):
        x, y, c = _position()
        me = 4 * x + 2 * y + c
        mine = pltpu.make_async_copy(v_ref, out_ref.at[me], local_sem)
        mine.start()
        sends = []
        for k in range(1, N_DEV):
            cp = pltpu.make_async_remote_copy(
                src_ref=v_ref, dst_ref=out_ref.at[me], send_sem=send_sems.at[k - 1], recv_sem=recv_sems.at[k - 1],
                device_id=_peer(x, y, c, k), device_id_type=MESH_IDS)
            cp.start()
            sends.append(cp)
        for k in range(1, N_DEV):
            px, py, pc = _peer(x, y, c, k)
            pltpu.make_async_remote_copy(
                src_ref=v_ref, dst_ref=out_ref.at[4 * px + 2 * py + pc], send_sem=send_sems.at[k - 1],
                recv_sem=recv_sems.at[k - 1], device_id=(px, py, pc), device_id_type=MESH_IDS).wait_recv()
        for cp in sends:
            cp.wait_send()
        mine.wait()

    return pl.pallas_call(
        body, name=name,
        out_shape=jax.ShapeDtypeStruct((N_DEV, 1, n), v.dtype),
        in_specs=[pl.BlockSpec(memory_space=pltpu.VMEM)],
        out_specs=pl.BlockSpec(memory_space=pltpu.VMEM),
        scratch_shapes=[pltpu.SemaphoreType.DMA((N_DEV - 1,)), pltpu.SemaphoreType.DMA((N_DEV - 1,)),
                        pltpu.SemaphoreType.DMA],
    )(v)


def _prologue(cond_conv, w_ada_shard, w_in_shard, name):
    def body(cc_ref, wada_ref, win_ref, first_ref, mods_ref, wout_ref, mod_cols, conds_all, big_send, big_recv, big_local,
             a_send, a_recv, a_local, b_send, b_recv, b_local):
        x, y, c = _position()
        me, sibling = (x, y, c), (x, y, 1 - c)
        chips = [(1 - x, y), (x, 1 - y), (1 - x, 1 - y)]

        def big(k, blk, to, own=False):
            slot = wout_ref.at[4 * blk[0] + 2 * blk[1] + blk[2]]
            return pltpu.make_async_remote_copy(src_ref=win_ref if own else slot, dst_ref=slot, send_sem=big_send.at[k],
                                                recv_sem=big_recv.at[k], device_id=to, device_id_type=MESH_IDS)

        mine = pltpu.make_async_copy(win_ref, wout_ref.at[4 * x + 2 * y + c], big_local.at[0])
        mine.start()
        first = [big(0, me, sibling, own=True)] + [big(1 + j, me, (*chip, c), own=True) for j, chip in enumerate(chips)]
        for cp in first:
            cp.start()

        conds = _DirectExchange("gather", [cc_ref], [first_ref], a_send, a_recv, a_local)
        conds.start()
        conds.wait()
        for d in range(N_DEV):
            conds_all[d:d + 1, :] = first_ref[d, :, 0:D_MODEL]
        cv = conds_all[...]
        mod_cols[...] = _dot(cv * _sigmoid(cv), wada_ref[...], HIGHEST)
        mods = _DirectExchange("gather", [mod_cols], [mods_ref], b_send, b_recv, b_local)
        mods.start()
        mods.wait()

        passed = []
        for j, chip in enumerate(chips):
            big(1 + j, (*chip, c), me).wait_recv()
            passed.append(big(4 + j, (*chip, c), sibling))
            passed[-1].start()
        big(0, sibling, me).wait_recv()
        for j, chip in enumerate(chips):
            big(4 + j, (*chip, 1 - c), me).wait_recv()
        for cp in first + passed:
            cp.wait_send()
        mine.wait()

    vmem = pl.BlockSpec(memory_space=pltpu.VMEM)
    hbm = pl.BlockSpec(memory_space=pl.ANY)
    cols = w_ada_shard.shape[1]
    return pl.pallas_call(
        body, name=name,
        out_shape=[jax.ShapeDtypeStruct((N_DEV,) + cond_conv.shape, F32), jax.ShapeDtypeStruct((N_DEV, N_DEV, cols), F32),
                   jax.ShapeDtypeStruct((N_DEV,) + w_in_shard.shape, w_in_shard.dtype)],
        in_specs=[vmem, vmem, hbm], out_specs=[vmem, vmem, hbm],
        scratch_shapes=[pltpu.VMEM((N_DEV, cols), F32), pltpu.VMEM((N_DEV, D_MODEL), F32)] + _DirectExchange.scratch(1) * 3,
    )(cond_conv, w_ada_shard, w_in_shard)


class _DirectExchange:
    def __init__(self, kind, src_refs, dst_refs, send_sems, recv_sems, local_sems):
        self.kind, self.src_refs, self.dst_refs = kind, src_refs, dst_refs
        self.send_sems, self.recv_sems, self.local_sems = send_sems, recv_sems, local_sems

    @staticmethod
    def scratch(n_arrays):
        copies = N_DEV - 1
        return [pltpu.SemaphoreType.DMA((copies * n_arrays,)), pltpu.SemaphoreType.DMA((copies * n_arrays,)),
                pltpu.SemaphoreType.DMA((n_arrays,))]

    def _copies(self):
        x, y, c = _position()
        me = 4 * x + 2 * y + c
        local, sends, arrivals = [], [], []
        for a, (src, dst) in enumerate(zip(self.src_refs, self.dst_refs)):
            own = src if self.kind == "gather" else src.at[me]
            local.append(pltpu.make_async_copy(own, dst.at[me], self.local_sems.at[a]))
            for k in range(1, N_DEV):
                px, py, pc = _peer(x, y, c, k)
                peer = 4 * px + 2 * py + pc
                sems = dict(send_sem=self.send_sems.at[(N_DEV - 1) * a + k - 1], recv_sem=self.recv_sems.at[(N_DEV - 1) * a + k - 1],
                            device_id=(px, py, pc), device_id_type=MESH_IDS)
                sends.append(pltpu.make_async_remote_copy(
                    src_ref=src if self.kind == "gather" else src.at[peer], dst_ref=dst.at[me], **sems))
                arrivals.append(pltpu.make_async_remote_copy(src_ref=own, dst_ref=dst.at[peer], **sems))
        return local, sends, arrivals

    def start(self):
        local, sends, _ = self._copies()
        for cp in local + sends:
            cp.start()

    def wait(self):
        local, sends, arrivals = self._copies()
        for cp in arrivals:
            cp.wait_recv()
        for cp in sends:
            cp.wait_send()
        for cp in local:
            cp.wait()


def _sum_slots(slots, name):
    _, r, n = slots.shape
    tr = _pick(r, 256, 16) if r % 16 == 0 else r

    def body(s_ref, o_ref):
        acc = s_ref[0].astype(F32)
        for d in range(1, N_DEV):
            acc = acc + s_ref[d].astype(F32)
        o_ref[...] = acc

    return pl.pallas_call(
        body, name=name, grid=(r // tr,),
        in_specs=[pl.BlockSpec((N_DEV, tr, n), lambda i: (0, i, 0))],
        out_specs=pl.BlockSpec((tr, n), lambda i: (i, 0)),
        out_shape=jax.ShapeDtypeStruct((r, n), F32),
    )(slots)


def _matmul(a, b, name, out_dtype=F32):
    m, k = a.shape
    _, n = b.shape
    tm, tn, tk = _pick(m, 1024, 16), _pick(n, 1408, 128), _pick(k, 1408, 128)
    nk = k // tk

    def body(a_ref, b_ref, o_ref, acc_ref):
        kk = pl.program_id(2)

        @pl.when(kk == 0)
        def _():
            acc_ref[...] = jnp.zeros_like(acc_ref)

        acc_ref[...] += _dot(a_ref[...].astype(MXU_DTYPE), b_ref[...].astype(MXU_DTYPE))

        @pl.when(kk == nk - 1)
        def _():
            o_ref[...] = acc_ref[...].astype(o_ref.dtype)

    return pl.pallas_call(
        body, name=name, grid=(m // tm, n // tn, nk),
        in_specs=[pl.BlockSpec((tm, tk), lambda i, j, kk: (i, kk)), pl.BlockSpec((tk, tn), lambda i, j, kk: (kk, j))],
        out_specs=pl.BlockSpec((tm, tn), lambda i, j, kk: (i, j)),
        out_shape=jax.ShapeDtypeStruct((m, n), out_dtype),
        scratch_shapes=[pltpu.VMEM((tm, tn), F32)],
    )(a, b)


def _matmul_pieces_beside_scatter(pieces, offsets, w, blocks, name):
    m = pieces[0].shape[0]
    n = w.shape[1]
    tm = _pick(m, 256, 16)
    gm = m // tm
    npieces, na = len(pieces), len(blocks)
    hbm = pl.BlockSpec(memory_space=pl.ANY)

    def body(*refs):
        a_refs, w_refs = refs[:npieces], refs[npieces:2 * npieces]
        rest = refs[2 * npieces:]
        o_ref = rest[na]
        exchange = _DirectExchange("scatter", rest[:na], rest[na + 1:2 * na + 1], *rest[2 * na + 1:])
        i = pl.program_id(0)

        @pl.when(i == 0)
        def _():
            exchange.start()

        acc = _dot(a_refs[0][...].astype(MXU_DTYPE), w_refs[0][...])
        for a_ref, w_ref in zip(a_refs[1:], w_refs[1:]):
            acc = acc + _dot(a_ref[...].astype(MXU_DTYPE), w_ref[...])
        o_ref[...] = acc

        @pl.when(i == gm - 1)
        def _():
            exchange.wait()

    a_specs = [pl.BlockSpec((tm, p.shape[1]), lambda i: (i, 0)) for p in pieces]
    w_specs = [pl.BlockSpec((p.shape[1], n), functools.partial(lambda i, blk: (blk, 0), blk=off // p.shape[1]))
               for p, off in zip(pieces, offsets)]
    return pl.pallas_call(
        body, name=name, grid=(gm,),
        in_specs=a_specs + w_specs + [hbm] * na,
        out_specs=[pl.BlockSpec((tm, n), lambda i: (i, 0))] + [hbm] * na,
        out_shape=[jax.ShapeDtypeStruct((m, n), F32)] + [jax.ShapeDtypeStruct(blk.shape, blk.dtype) for blk in blocks],
        scratch_shapes=_DirectExchange.scratch(na),
    )(*pieces, *([w] * npieces), *blocks)


def _matmul_nt(a, bt, name, out_dtype=F32):
    m, k = a.shape
    n, _ = bt.shape
    tm, tn, tk = _pick(m, 1024, 16), _pick(n, 1408, 128), _pick(k, 1408, 128)
    nk = k // tk

    def body(a_ref, b_ref, o_ref, acc_ref):
        kk = pl.program_id(2)

        @pl.when(kk == 0)
        def _():
            acc_ref[...] = jnp.zeros_like(acc_ref)

        acc_ref[...] += _dot_nt(a_ref[...].astype(MXU_DTYPE), b_ref[...].astype(MXU_DTYPE))

        @pl.when(kk == nk - 1)
        def _():
            o_ref[...] = acc_ref[...].astype(o_ref.dtype)

    return pl.pallas_call(
        body, name=name, grid=(m // tm, n // tn, nk),
        in_specs=[pl.BlockSpec((tm, tk), lambda i, j, kk: (i, kk)), pl.BlockSpec((tn, tk), lambda i, j, kk: (j, kk))],
        out_specs=pl.BlockSpec((tm, tn), lambda i, j, kk: (i, j)),
        out_shape=jax.ShapeDtypeStruct((m, n), out_dtype),
        scratch_shapes=[pltpu.VMEM((tm, tn), F32)],
    )(a, bt)


def _matmul_sum2(a1, b1, a2, b2, name):
    m, k = a1.shape
    _, n = b1.shape
    tm, tn, tk = _pick(m, 1024, 16), _pick(n, 1408, 128), _pick(k, 1408, 128)
    nk = k // tk

    def body(a1_ref, b1_ref, a2_ref, b2_ref, o_ref):
        @pl.when(pl.program_id(2) == 0)
        def _():
            o_ref[...] = jnp.zeros_like(o_ref)

        o_ref[...] += (_dot(a1_ref[...].astype(MXU_DTYPE), b1_ref[...].astype(MXU_DTYPE))
                       + _dot(a2_ref[...].astype(MXU_DTYPE), b2_ref[...].astype(MXU_DTYPE)))

    a_spec = pl.BlockSpec((tm, tk), lambda i, j, kk: (i, kk))
    b_spec = pl.BlockSpec((tk, tn), lambda i, j, kk: (kk, j))
    return pl.pallas_call(
        body, name=name, grid=(m // tm, n // tn, nk), in_specs=[a_spec, b_spec, a_spec, b_spec],
        out_specs=pl.BlockSpec((tm, tn), lambda i, j, kk: (i, j)),
        out_shape=jax.ShapeDtypeStruct((m, n), F32),
    )(a1, b1, a2, b2)


def _matmul_tn(a, b, name, out_dtype=F32):
    l, m = a.shape
    _, n = b.shape
    tm, tn, tl = _pick(m, 1408, 128), _pick(n, 1408, 128), _pick(l, 512, 16)
    nl = l // tl

    def body(a_ref, b_ref, o_ref, acc_ref):
        ll = pl.program_id(2)

        @pl.when(ll == 0)
        def _():
            acc_ref[...] = jnp.zeros_like(acc_ref)

        acc_ref[...] += _dot_tn(a_ref[...].astype(MXU_DTYPE), b_ref[...].astype(MXU_DTYPE))

        @pl.when(ll == nl - 1)
        def _():
            o_ref[...] = acc_ref[...].astype(o_ref.dtype)

    return pl.pallas_call(
        body, name=name, grid=(m // tm, n // tn, nl),
        in_specs=[pl.BlockSpec((tl, tm), lambda i, j, ll: (ll, i)), pl.BlockSpec((tl, tn), lambda i, j, ll: (ll, j))],
        out_specs=pl.BlockSpec((tm, tn), lambda i, j, ll: (i, j)),
        out_shape=jax.ShapeDtypeStruct((m, n), out_dtype),
        scratch_shapes=[pltpu.VMEM((tm, tn), F32)],
    )(a, b)


def _row_specs(l, d, n_rows, n_vecs):
    tl = _pick(l, 512, 16)
    row = pl.BlockSpec((tl, d), lambda i: (i, 0))
    vec = pl.BlockSpec((1, d), lambda i: (0, 0))
    return tl, row, vec, [row] * n_rows + [vec] * n_vecs


def _rms_mod(x, nw, sc, sh, name):
    l, d = x.shape
    tl, row, _, in_specs = _row_specs(l, d, 1, 3)

    def body(x_ref, nw_ref, sc_ref, sh_ref, h_ref):
        xv = x_ref[...]
        r = lax.rsqrt(jnp.mean(xv * xv, axis=-1, keepdims=True) + EPS)
        h_ref[...] = (((xv * r) * nw_ref[...]) * (1.0 + sc_ref[...]) + sh_ref[...]).astype(h_ref.dtype)

    return pl.pallas_call(body, name=name, grid=(l // tl,), in_specs=in_specs, out_specs=row,
                          out_shape=jax.ShapeDtypeStruct((l, d), MXU_DTYPE))(x, nw, sc, sh)


def _residual_rms_mod(x, mix, g, nw, sc, sh, name):
    l, d = x.shape
    tl, row, _, in_specs = _row_specs(l, d, 2, 4)

    def body(x_ref, mix_ref, g_ref, nw_ref, sc_ref, sh_ref, x1_ref, h_ref):
        xv = x_ref[...] + g_ref[...] * mix_ref[...]
        x1_ref[...] = xv
        r = lax.rsqrt(jnp.mean(xv * xv, axis=-1, keepdims=True) + EPS)
        h_ref[...] = (((xv * r) * nw_ref[...]) * (1.0 + sc_ref[...]) + sh_ref[...]).astype(h_ref.dtype)

    return pl.pallas_call(body, name=name, grid=(l // tl,), in_specs=in_specs, out_specs=[row, row],
                          out_shape=[jax.ShapeDtypeStruct((l, d), F32), jax.ShapeDtypeStruct((l, d), MXU_DTYPE)],
                          )(x, mix, g, nw, sc, sh)


def _norm_bwd(x, dh, dres, nw, sc, name, gated=None):
    l, d = x.shape
    tl, row, vec, in_specs = _row_specs(l, d, 3, 2)
    stats = pl.BlockSpec((8, d), lambda i: (0, 0))

    def body(x_ref, dh_ref, dres_ref, nw_ref, sc_ref, *rest):
        dx_ref, st_ref = rest[-2:] if gated is None else (rest[2], rest[4])
        xv, dh_v = x_ref[...], dh_ref[...]
        r = lax.rsqrt(jnp.mean(xv * xv, axis=-1, keepdims=True) + EPS)
        xn = xv * r
        dxn = dh_v * (nw_ref[...] * (1.0 + sc_ref[...]))
        dx = dres_ref[...] + r * (dxn - xn * jnp.mean(dxn * xn, axis=-1, keepdims=True))
        dx_ref[...] = dx

        @pl.when(pl.program_id(0) == 0)
        def _():
            st_ref[...] = jnp.zeros_like(st_ref)

        dhx = dh_v * xn
        st_ref[0:1, :] += _colsum(dhx * nw_ref[...])
        st_ref[1:2, :] += _colsum(dh_v)
        st_ref[2:3, :] += _colsum(dhx * (1.0 + sc_ref[...]))
        if gated is not None:
            val_ref, g_ref, dval_ref = rest[0], rest[1], rest[3]
            dval_ref[...] = (g_ref[...] * dx).astype(dval_ref.dtype)
            st_ref[3:4, :] += _colsum(dx * val_ref[...])

    if gated is None:
        return pl.pallas_call(body, name=name, grid=(l // tl,), in_specs=in_specs, out_specs=[row, stats],
                              out_shape=[jax.ShapeDtypeStruct((l, d), F32), jax.ShapeDtypeStruct((8, d), F32)],
                              )(x, dh, dres, nw, sc)
    return pl.pallas_call(body, name=name, grid=(l // tl,), in_specs=in_specs + [row, vec], out_specs=[row, row, stats],
                          out_shape=[jax.ShapeDtypeStruct((l, d), F32), jax.ShapeDtypeStruct((l, d), MXU_DTYPE),
                                     jax.ShapeDtypeStruct((8, d), F32)],
                          )(x, dh, dres, nw, sc, *gated)


def _loss_head(x1, f, g, target, name):
    l, d = x1.shape
    tl, row, _, _ = _row_specs(l, d, 0, 0)
    vec = pl.BlockSpec((1, d), lambda i: (0, 0))

    def body(x1_ref, f_ref, g_ref, t_ref, dy_ref, df_ref, st_ref, loss_ref):
        fv = f_ref[...]
        e = x1_ref[...] + g_ref[...] * fv - t_ref[...]
        dy = e * (1.0 / d)
        dy_ref[...] = dy
        df_ref[...] = (g_ref[...] * dy).astype(df_ref.dtype)

        @pl.when(pl.program_id(0) == 0)
        def _():
            loss_ref[...] = jnp.zeros_like(loss_ref)
            st_ref[...] = jnp.zeros_like(st_ref)

        st_ref[0:1, :] += _colsum(dy * fv)
        s = jnp.sum(jnp.sum(e * e, axis=1, keepdims=True), axis=0, keepdims=True)
        loss_ref[...] += (0.5 / d) * s

    return pl.pallas_call(body, name=name, grid=(l // tl,), in_specs=[row, row, vec, row],
                          out_specs=[row, row, pl.BlockSpec((8, d), lambda i: (0, 0)), pl.BlockSpec((8, 128), lambda i: (0, 0))],
                          out_shape=[jax.ShapeDtypeStruct((l, d), F32), jax.ShapeDtypeStruct((l, d), MXU_DTYPE),
                                     jax.ShapeDtypeStruct((8, d), F32), jax.ShapeDtypeStruct((8, 128), F32)],
                          )(x1, f, g, target)


def _ffn_in(h, w_gate_t, w_up_t, name):
    l, k = h.shape
    f = w_gate_t.shape[0]
    tl, tf = _pick(l, 512, 16), _pick(f, 1408, 128)
    h_spec = pl.BlockSpec((tl, k), lambda j, i: (i, 0))
    w_spec = pl.BlockSpec((tf, k), lambda j, i: (j, 0))
    spec = pl.BlockSpec((tl, tf), lambda j, i: (i, j))

    def body(h_ref, wg_ref, wu_ref, g_ref, u_ref, a_ref):
        hv = h_ref[...].astype(MXU_DTYPE)
        gv = _dot_nt(hv, wg_ref[...].astype(MXU_DTYPE))
        uv = _dot_nt(hv, wu_ref[...].astype(MXU_DTYPE))
        g_ref[...] = gv
        u_ref[...] = uv
        a_ref[...] = (gv * _sigmoid(gv) * uv).astype(a_ref.dtype)

    return pl.pallas_call(body, name=name, grid=(f // tf, l // tl), in_specs=[h_spec, w_spec, w_spec], out_specs=[spec, spec, spec],
                          out_shape=[jax.ShapeDtypeStruct((l, f), F32)] * 2 + [jax.ShapeDtypeStruct((l, f), MXU_DTYPE)],
                          )(h, w_gate_t, w_up_t)


def _ffn_down_bwd(dffn, w_down, gate, up, name):
    l, f = gate.shape
    k = dffn.shape[1]
    tl, tf = _pick(l, 512, 16), _pick(f, 1408, 128)
    spec = pl.BlockSpec((tl, tf), lambda j, i: (i, j))

    def body(d_ref, w_ref, g_ref, u_ref, dg_ref, du_ref):
        dav = _dot_nt(d_ref[...].astype(MXU_DTYPE), w_ref[...].astype(MXU_DTYPE))
        gv = g_ref[...]
        s = _sigmoid(gv)
        dg_ref[...] = (dav * u_ref[...] * (s * (1.0 + gv * (1.0 - s)))).astype(dg_ref.dtype)
        du_ref[...] = (dav * (gv * s)).astype(du_ref.dtype)

    return pl.pallas_call(body, name=name, grid=(f // tf, l // tl),
                          in_specs=[pl.BlockSpec((tl, k), lambda j, i: (i, 0)), pl.BlockSpec((tf, k), lambda j, i: (j, 0)), spec, spec],
                          out_specs=[spec, spec], out_shape=[jax.ShapeDtypeStruct((l, f), MXU_DTYPE)] * 2)(dffn, w_down, gate, up)


def _conv_tile(l):
    return _pick(l, 512, 16)


def _conv_pre(buf, w_ref, b_ref, tl):
    acc = b_ref[...] + w_ref[3:4, :] * buf[HALO:HALO + tl, :]
    for k in range(CONV_WIDTH - 1):
        s = HALO - (CONV_WIDTH - 1) + k
        acc = acc + w_ref[k:k + 1, :] * buf[s:s + tl, :]
    return acc


def _fill_past(buf, u_ref, halo_ref, tl):
    i = pl.program_id(0)

    @pl.when(i == 0)
    def _():
        buf[0:HALO, :] = jnp.zeros((HALO, buf.shape[1]), F32)

    @pl.when(i > 0)
    def _():
        buf[0:HALO, :] = halo_ref[...]

    buf[HALO:HALO + tl, :] = u_ref[...]


def _conv_silu(u, w, b, name):
    l, ch = u.shape[0], w.shape[1]
    tl = _conv_tile(l)
    per = tl // HALO
    cur = pl.BlockSpec((tl, ch), lambda i: (i, 0))
    past = pl.BlockSpec((HALO, ch), lambda i: (jnp.maximum(i * per - 1, 0), 0))

    def body(u_ref, halo_ref, w_ref, b_ref, o_ref, buf):
        _fill_past(buf, u_ref, halo_ref, tl)
        pre = _conv_pre(buf, w_ref, b_ref, tl)
        o_ref[...] = pre * _sigmoid(pre)

    return pl.pallas_call(body, name=name, grid=(l // tl,),
                          in_specs=[cur, past, pl.BlockSpec((CONV_WIDTH, ch), lambda i: (0, 0)), pl.BlockSpec((1, ch), lambda i: (0, 0))],
                          out_specs=cur, out_shape=jax.ShapeDtypeStruct((l, ch), F32),
                          scratch_shapes=[pltpu.VMEM((tl + HALO, ch), F32)])(u, u, w, b)


def _conv_silu_bwd_pre(u, dxc, w, b, name):
    l, ch = u.shape[0], w.shape[1]
    tl = _conv_tile(l)
    per = tl // HALO
    cur = pl.BlockSpec((tl, ch), lambda i: (i, 0))
    past = pl.BlockSpec((HALO, ch), lambda i: (jnp.maximum(i * per - 1, 0), 0))

    def body(u_ref, halo_ref, d_ref, w_ref, b_ref, dpre_ref, st_ref, buf):
        _fill_past(buf, u_ref, halo_ref, tl)
        pre = _conv_pre(buf, w_ref, b_ref, tl)
        s = _sigmoid(pre)
        dpre = d_ref[...] * (s * (1.0 + pre * (1.0 - s)))
        dpre_ref[...] = dpre

        @pl.when(pl.program_id(0) == 0)
        def _():
            st_ref[...] = jnp.zeros_like(st_ref)

        for k in range(CONV_WIDTH):
            s0 = HALO - (CONV_WIDTH - 1) + k
            st_ref[k:k + 1, :] += _colsum(dpre * buf[s0:s0 + tl, :])
        st_ref[CONV_WIDTH:CONV_WIDTH + 1, :] += _colsum(dpre)

    return pl.pallas_call(body, name=name, grid=(l // tl,),
                          in_specs=[cur, past, cur, pl.BlockSpec((CONV_WIDTH, ch), lambda i: (0, 0)), pl.BlockSpec((1, ch), lambda i: (0, 0))],
                          out_specs=[cur, pl.BlockSpec((8, ch), lambda i: (0, 0))],
                          out_shape=[jax.ShapeDtypeStruct((l, ch), F32), jax.ShapeDtypeStruct((8, ch), F32)],
                          scratch_shapes=[pltpu.VMEM((tl + HALO, ch), F32)])(u, u, dxc, w, b)


def _conv_bwd_input(dpre, w, name):
    l, ch = dpre.shape
    tl = _conv_tile(l)
    per = tl // HALO
    nt = l // tl
    cur = pl.BlockSpec((tl, ch), lambda i: (i, 0))
    nxt = pl.BlockSpec((HALO, ch), lambda i: (jnp.minimum((i + 1) * per, l // HALO - 1), 0))

    def body(d_ref, halo_ref, w_ref, du_ref, buf):
        i = pl.program_id(0)
        buf[0:tl, :] = d_ref[...]

        @pl.when(i == nt - 1)
        def _():
            buf[tl:tl + HALO, :] = jnp.zeros((HALO, ch), F32)

        @pl.when(i < nt - 1)
        def _():
            buf[tl:tl + HALO, :] = halo_ref[...]

        acc = w_ref[3:4, :] * buf[0:tl, :]
        for k in range(CONV_WIDTH - 1):
            s = CONV_WIDTH - 1 - k
            acc = acc + w_ref[k:k + 1, :] * buf[s:s + tl, :]
        du_ref[...] = acc

    return pl.pallas_call(body, name=name, grid=(nt,),
                          in_specs=[cur, nxt, pl.BlockSpec((CONV_WIDTH, ch), lambda i: (0, 0))],
                          out_specs=cur, out_shape=jax.ShapeDtypeStruct((l, ch), F32),
                          scratch_shapes=[pltpu.VMEM((tl + HALO, ch), F32)])(dpre, dpre, w)


HEAD_TILE = 128


def _chunk_iota():
    r = lax.broadcasted_iota(jnp.int32, (CHUNK, CHUNK), 0)
    c = lax.broadcasted_iota(jnp.int32, (CHUNK, CHUNK), 1)
    return r, c


def _split3_dot(v, ones_b):
    hi = v.astype(BF16)
    r1 = v - hi.astype(F32)
    mid = r1.astype(BF16)
    lo = (r1 - mid.astype(F32)).astype(BF16)
    return _dot(hi, ones_b) + _dot(mid, ones_b) + _dot(lo, ones_b)


def _split3_dot_nt(v, ones_b):
    hi = v.astype(BF16)
    r1 = v - hi.astype(F32)
    mid = r1.astype(BF16)
    lo = (r1 - mid.astype(F32)).astype(BF16)
    return _dot_nt(hi, ones_b) + _dot_nt(mid, ones_b) + _dot_nt(lo, ones_b)


def _ssd_decays(dt_ref, sx_ref, dtbx_ref, alx_ref, r, c):
    tri = (r >= c).astype(BF16)
    raw_x = _split3_dot(dt_ref[...], sx_ref[...]) + dtbx_ref[...]
    dt_x = _softplus(raw_x)
    a_x = -jnp.exp(alx_ref[...])
    adt_x = a_x * dt_x
    acs_x = _tri_dot(tri, adt_x)
    atot_x = _colsum(adt_x)
    return raw_x, dt_x, a_x, acs_x, jnp.exp(acs_x), jnp.exp(atot_x - acs_x), jnp.exp(atot_x)


def _head_decay(acs_x, e, r, c, halves):
    tile = acs_x[:, (e // 2) * 2 * HEAD_DIM:(e // 2 + 1) * 2 * HEAD_DIM]
    col = jnp.broadcast_to(jnp.max(jnp.where(halves[e % 2], tile, -jnp.inf), axis=1, keepdims=True), (CHUNK, CHUNK))
    return jnp.where(r >= c, jnp.exp(col - col.T), 0.0)


def _half_masks():
    lane = lax.broadcasted_iota(jnp.int32, (CHUNK, 2 * HEAD_DIM), 1)
    return lane < HEAD_DIM, lane >= HEAD_DIM


def _ssd_specs(nc, reverse):
    def at(i):
        return nc - 1 - i if reverse else i
    xc = pl.BlockSpec((CHUNK, D_CONV), lambda i: (at(i), 0))
    wide = pl.BlockSpec((CHUNK, D_SSD), lambda i: (at(i), 0))
    dt_tile = pl.BlockSpec((CHUNK, HEAD_TILE), lambda i: (at(i), O_DT // HEAD_TILE))
    dt_out = pl.BlockSpec((CHUNK, HEAD_TILE), lambda i: (at(i), 0))
    spread = pl.BlockSpec((HEAD_TILE, D_SSD), lambda i: (0, 0))
    vec = pl.BlockSpec((1, D_SSD), lambda i: (0, 0))
    state = pl.BlockSpec((1, SSD_STATE, D_SSD), lambda i: (at(i), 0, 0))
    return xc, wide, dt_tile, dt_out, spread, vec, state


def _ssd_fwd(xc, proj, spread_x, dtb_x, al_x, name):
    l = xc.shape[0]
    nc = l // CHUNK
    xc_s, wide_s, dt_s, _, spread_s, vec_s, state_s = _ssd_specs(nc, False)
    pairs_per_group = GROUP_WIDTH // (2 * HEAD_DIM)

    def body(xc_ref, dt_ref, sx_ref, dtbx_ref, alx_ref, y_ref, sp_ref, state):
        @pl.when(pl.program_id(0) == 0)
        def _():
            state[...] = jnp.zeros_like(state)

        r, c = _chunk_iota()
        halves = _half_masks()
        _, dt_x, _, acs_x, ea_x, ds_x, eatot_x = _ssd_decays(dt_ref, sx_ref, dtbx_ref, alx_ref, r, c)
        xg = xc_ref[:, 0:D_SSD] * dt_x
        sp_ref[0] = state[...]
        for g in range(SSD_GROUPS):
            lanes = slice(g * GROUP_WIDTH, (g + 1) * GROUP_WIDTH)
            bb = xc_ref[:, D_SSD + g * SSD_STATE:D_SSD + (g + 1) * SSD_STATE].astype(MXU_DTYPE)
            cb = xc_ref[:, D_SSD + (SSD_GROUPS + g) * SSD_STATE:D_SSD + (SSD_GROUPS + g + 1) * SSD_STATE].astype(MXU_DTYPE)
            scores = _dot_nt(cb, bb)
            sg = state[:, lanes]
            ys = []
            for j in range(g * pairs_per_group, (g + 1) * pairs_per_group):
                xg_pair = xg[:, j * 2 * HEAD_DIM:(j + 1) * 2 * HEAD_DIM]
                acc = jnp.zeros((CHUNK, 2 * HEAD_DIM), F32)
                for half in range(2):
                    m = (scores * _head_decay(acs_x, 2 * j + half, r, c, halves)).astype(MXU_DTYPE)
                    acc = acc + _dot(m, jnp.where(halves[half], xg_pair, 0.0).astype(MXU_DTYPE))
                ys.append(acc)
            y_ref[:, lanes] = jnp.concatenate(ys, axis=1) + _dot(cb, sg.astype(MXU_DTYPE)) * ea_x[:, lanes]
            state[:, lanes] = sg * eatot_x[:, lanes] + _dot_tn(bb, (xg[:, lanes] * ds_x[:, lanes]).astype(MXU_DTYPE))

    return pl.pallas_call(
        body, name=name, grid=(nc,),
        in_specs=[xc_s, dt_s, spread_s, vec_s, vec_s],
        out_specs=[wide_s, state_s],
        out_shape=[jax.ShapeDtypeStruct((l, D_SSD), F32), jax.ShapeDtypeStruct((nc, SSD_STATE, D_SSD), F32)],
        scratch_shapes=[pltpu.VMEM((SSD_STATE, D_SSD), F32)],
    )(xc, proj, spread_x, dtb_x, al_x)


def _ssd_bwd(xc, proj, spread_x, dtb_x, al_x, dy, states, name):
    l = xc.shape[0]
    nc = l // CHUNK
    xc_s, wide_s, dt_s, dt_out_s, spread_s, vec_s, state_s = _ssd_specs(nc, True)
    pairs_per_group = GROUP_WIDTH // (2 * HEAD_DIM)

    def body(xc_ref, dt_ref, sx_ref, dtbx_ref, alx_ref, dy_ref, sp_ref,
             dxc_ref, ddtr_ref, st_ref, hs_ref, dstate):
        @pl.when(pl.program_id(0) == 0)
        def _():
            dstate[...] = jnp.zeros_like(dstate)
            st_ref[...] = jnp.zeros_like(st_ref)

        r, c = _chunk_iota()
        halves = _half_masks()
        spread = sx_ref[...]

        def head_sums(v):
            return _split3_dot(_split3_dot_nt(v, spread), spread)

        raw_x, dt_x, a_x, acs_x, ea_x, ds_x, eatot_x = _ssd_decays(dt_ref, sx_ref, dtbx_ref, alx_ref, r, c)
        xs = xc_ref[:, 0:D_SSD]
        xg = xs * dt_x
        gy = dy_ref[...]
        s_prev = sp_ref[0]
        gea = gy * ea_x
        xds = xg * ds_x
        ds_old = dstate[...]
        later2 = jnp.concatenate([(r <= c).astype(BF16)] * 2, axis=1)
        dxg_parts, state_term, yoff_parts, dadt_parts = [], [], [], []
        for g in range(SSD_GROUPS):
            lanes = slice(g * GROUP_WIDTH, (g + 1) * GROUP_WIDTH)
            b_lo = D_SSD + g * SSD_STATE
            c_lo = D_SSD + (SSD_GROUPS + g) * SSD_STATE
            bb = xc_ref[:, b_lo:b_lo + SSD_STATE].astype(MXU_DTYPE)
            cb = xc_ref[:, c_lo:c_lo + SSD_STATE].astype(MXU_DTYPE)
            scores = _dot_nt(cb, bb)
            dsg = ds_old[:, lanes].astype(MXU_DTYPE)
            gea_b = gea[:, lanes].astype(MXU_DTYPE)
            xds_b = xds[:, lanes].astype(MXU_DTYPE)
            dxg_state = _dot(bb, dsg) * ds_x[:, lanes]
            dc = _dot_nt(gea_b, s_prev[:, lanes].astype(MXU_DTYPE))
            db = _dot_nt(xds_b, dsg)
            dscores = jnp.zeros((CHUNK, CHUNK), F32)
            diag = []
            for j in range(g * pairs_per_group, (g + 1) * pairs_per_group):
                pair = slice(j * 2 * HEAD_DIM, (j + 1) * 2 * HEAD_DIM)
                xg_pair = xg[:, pair].astype(MXU_DTYPE)
                acc = jnp.zeros((CHUNK, 2 * HEAD_DIM), F32)
                cols = []
                for half in range(2):
                    decay = _head_decay(acs_x, 2 * j + half, r, c, halves)
                    g_e = jnp.where(halves[half], gy[:, pair], 0.0).astype(MXU_DTYPE)
                    acc = acc + _dot_tn((scores * decay).astype(MXU_DTYPE), g_e)
                    dm = _dot_nt(g_e, xg_pair) * decay
                    dscores = dscores + dm
                    wq = dm * scores
                    hi = wq.astype(BF16)
                    lo = (wq - hi.astype(F32)).astype(BF16)
                    later = _dot(later2, jnp.concatenate([hi, lo], axis=0))
                    cols.append(jnp.sum(jnp.where(c < r, later, 0.0), axis=1, keepdims=True))
                diag.append(acc)
                dadt_parts.append(jnp.where(halves[0], cols[0], cols[1]))
            dsc_b = dscores.astype(MXU_DTYPE)
            dc = dc + _dot(dsc_b, bb)
            db = db + _dot_tn(dsc_b, cb)
            dxc_ref[:, b_lo:b_lo + SSD_STATE] = db
            dxc_ref[:, c_lo:c_lo + SSD_STATE] = dc
            dxg_parts.append(jnp.concatenate(diag, axis=1) + dxg_state)
            state_term.append(dxg_state)
            yoff_parts.append(_dot(cb, s_prev[:, lanes].astype(MXU_DTYPE)) * ea_x[:, lanes])
            dstate[:, lanes] = ds_old[:, lanes] * eatot_x[:, lanes] + _dot_tn(cb, gea_b)
        dxg = jnp.concatenate(dxg_parts, axis=1)
        dxc_ref[:, 0:D_SSD] = dxg * dt_x
        through_out = _tri_dot((r <= c).astype(BF16), gy * jnp.concatenate(yoff_parts, axis=1))
        through_in = _tri_dot((c < r).astype(BF16), xg * jnp.concatenate(state_term, axis=1))
        carried = jnp.broadcast_to(_colsum(ds_old * s_prev) * eatot_x, (8, D_SSD))
        dadt = (jnp.concatenate(dadt_parts, axis=1) + head_sums(through_out + through_in)
                + jnp.max(head_sums(carried), axis=0, keepdims=True))
        ddt = a_x * dadt + head_sums(dxg * xs)
        draw = ddt * _sigmoid(raw_x)
        ddtr_ref[...] = _split3_dot_nt(draw, spread) * (1.0 / HEAD_DIM)
        st_ref[0:1, :] += _colsum(dt_x * dadt) * a_x
        st_ref[1:2, :] += _colsum(draw)
        st_ref[2:3, :] += _colsum(gy * xs)

        @pl.when(pl.program_id(0) == nc - 1)
        def _():
            hs_ref[...] = head_sums(st_ref[...])

    stats = pl.BlockSpec((8, D_SSD), lambda i: (0, 0))
    return pl.pallas_call(
        body, name=name, grid=(nc,),
        in_specs=[xc_s, dt_s, spread_s, vec_s, vec_s, wide_s, state_s],
        out_specs=[xc_s, dt_out_s, stats, stats],
        out_shape=[jax.ShapeDtypeStruct((l, D_CONV), F32), jax.ShapeDtypeStruct((l, HEAD_TILE), F32),
                   jax.ShapeDtypeStruct((8, D_SSD), F32), jax.ShapeDtypeStruct((8, D_SSD), F32)],
        scratch_shapes=[pltpu.VMEM((SSD_STATE, D_SSD), F32)],
    )(xc, proj, spread_x, dtb_x, al_x, dy, states)


def _ssd_gate_norm(ycore, xc, z, dskip_x, norm_w, name):
    l = ycore.shape[0]
    tl = _pick(l, 512, 16)
    row = pl.BlockSpec((tl, D_SSD), lambda i: (i, 0))
    vec = pl.BlockSpec((1, D_SSD), lambda i: (0, 0))

    def body(y_ref, xs_ref, z_ref, dk_ref, nw_ref, o_ref):
        zv = z_ref[...]
        yv = (y_ref[...] + dk_ref[...] * xs_ref[...]) * (zv * _sigmoid(zv))
        for g in range(SSD_GROUPS):
            lanes = slice(g * GROUP_WIDTH, (g + 1) * GROUP_WIDTH)
            yg = yv[:, lanes]
            rg = lax.rsqrt(jnp.mean(yg * yg, axis=-1, keepdims=True) + EPS)
            o_ref[:, lanes] = (yg * rg * nw_ref[:, lanes]).astype(o_ref.dtype)

    z_spec = pl.BlockSpec((tl, D_SSD), lambda i: (i, O_Z // D_SSD))
    return pl.pallas_call(body, name=name, grid=(l // tl,), in_specs=[row, row, z_spec, vec, vec], out_specs=row,
                          out_shape=jax.ShapeDtypeStruct((l, D_SSD), MXU_DTYPE))(ycore, xc, z, dskip_x, norm_w)


def _ssd_gate_norm_bwd(dout, ycore, xc, z, dskip_x, norm_w, name):
    l = ycore.shape[0]
    tl = _pick(l, 512, 16)
    nt = l // tl
    row = pl.BlockSpec((tl, D_SSD), lambda i: (i, 0))
    vec = pl.BlockSpec((1, D_SSD), lambda i: (0, 0))

    def body(do_ref, y_ref, xs_ref, z_ref, dk_ref, nw_ref, dyc_ref, dxs_ref, dz_ref, st_ref):
        @pl.when(pl.program_id(0) == 0)
        def _():
            st_ref[...] = jnp.zeros_like(st_ref)

        zv, xs = z_ref[...], xs_ref[...]
        s = _sigmoid(zv)
        gz = zv * s
        yc = y_ref[...] + dk_ref[...] * xs
        yv = yc * gz
        dov = do_ref[...]
        dnw, dyv = [], []
        for g in range(SSD_GROUPS):
            lanes = slice(g * GROUP_WIDTH, (g + 1) * GROUP_WIDTH)
            yg = yv[:, lanes]
            rg = lax.rsqrt(jnp.mean(yg * yg, axis=-1, keepdims=True) + EPS)
            yn = yg * rg
            dnw.append(_colsum(dov[:, lanes] * yn))
            dyn = dov[:, lanes] * nw_ref[:, lanes]
            dyv.append(rg * (dyn - yn * jnp.mean(dyn * yn, axis=-1, keepdims=True)))
        dy = jnp.concatenate(dyv, axis=1)
        dyc = dy * gz
        dyc_ref[...] = dyc
        dxs_ref[...] = dyc * dk_ref[...]
        dz_ref[...] = dy * yc * (s * (1.0 + zv * (1.0 - s)))
        st_ref[0:1, :] += jnp.concatenate(dnw, axis=1)

    return pl.pallas_call(
        body, name=name, grid=(nt,), in_specs=[row, row, row, pl.BlockSpec((tl, D_SSD), lambda i: (i, O_Z // D_SSD)), vec, vec],
        out_specs=[row, row, row, pl.BlockSpec((8, D_SSD), lambda i: (0, 0))],
        out_shape=[jax.ShapeDtypeStruct((l, D_SSD), F32)] * 3 + [jax.ShapeDtypeStruct((8, D_SSD), F32)],
    )(dout, ycore, xc, z, dskip_x, norm_w)


PAIR = 2 * HEAD_DIM
N_PAIRS = N_HEADS // 2
Q_TILE0 = O_Q // PAIR
SB_KEYS = 1024
SB_PAIRS = 4
SB_PAIRS_FWD = 4
SB_SCALE = HEAD_DIM ** -0.5


def _pair_sum(v, lo):
    s_lo = jnp.sum(jnp.where(lo, v, 0.0), axis=-1, keepdims=True)
    s_hi = jnp.sum(jnp.where(lo, 0.0, v), axis=-1, keepdims=True)
    return jnp.where(lo, s_lo, s_hi)


def _qkv_prep(proj, qw2, kw2, name):
    l = proj.shape[0]
    tl = _pick(l, 1024, 16)
    out = pl.BlockSpec((tl, PAIR), lambda i, j: (i, j))
    vec = pl.BlockSpec((1, PAIR), lambda i, j: (0, 0))

    def at(which):
        return pl.BlockSpec((tl, PAIR), lambda i, j: (i, Q_TILE0 + which * N_PAIRS + j))

    def body(q_ref, k_ref, v_ref, qw_ref, kw_ref, qn_ref, kn_ref, vb_ref):
        lo = lax.broadcasted_iota(jnp.int32, (tl, PAIR), 1) < HEAD_DIM
        for t_ref, w_ref, o_ref, scale in ((q_ref, qw_ref, qn_ref, SB_SCALE), (k_ref, kw_ref, kn_ref, 1.0)):
            tv = t_ref[...]
            r = lax.rsqrt(_pair_sum(tv * tv, lo) * (1.0 / HEAD_DIM) + EPS)
            o_ref[...] = ((tv * r) * w_ref[...] * scale).astype(o_ref.dtype)
        vb_ref[...] = v_ref[...].astype(vb_ref.dtype)

    return pl.pallas_call(body, name=name, grid=(l // tl, N_PAIRS), in_specs=[at(0), at(1), at(2), vec, vec],
                          out_specs=[out, out, out], out_shape=[jax.ShapeDtypeStruct((l, D_SB), MXU_DTYPE)] * 3,
                          )(proj, proj, proj, qw2, kw2)


def _qk_norm_bwd(proj, dqn, dkn, qw2, kw2, name):
    l = proj.shape[0]
    tl = _pick(l, 1024, 16)
    out = pl.BlockSpec((tl, PAIR), lambda i, j: (i, j))
    vec = pl.BlockSpec((1, PAIR), lambda i, j: (0, 0))
    stats = pl.BlockSpec((8, PAIR), lambda i, j: (0, 0))

    def at(which):
        return pl.BlockSpec((tl, PAIR), lambda i, j: (i, Q_TILE0 + which * N_PAIRS + j))

    def body(q_ref, k_ref, dqn_ref, dkn_ref, qw_ref, kw_ref, dq_ref, dk_ref, stq_ref, stk_ref):
        @pl.when((pl.program_id(0) == 0) & (pl.program_id(1) == 0))
        def _():
            stq_ref[...] = jnp.zeros_like(stq_ref)
            stk_ref[...] = jnp.zeros_like(stk_ref)

        lo = lax.broadcasted_iota(jnp.int32, (tl, PAIR), 1) < HEAD_DIM
        for t_ref, dn_ref, w_ref, d_ref, st_ref, scale in ((q_ref, dqn_ref, qw_ref, dq_ref, stq_ref, SB_SCALE),
                                                           (k_ref, dkn_ref, kw_ref, dk_ref, stk_ref, 1.0)):
            tv = t_ref[...]
            r = lax.rsqrt(_pair_sum(tv * tv, lo) * (1.0 / HEAD_DIM) + EPS)
            tn = tv * r
            dnv = dn_ref[...] * scale
            dtn = dnv * w_ref[...]
            d_ref[...] = r * (dtn - tn * (_pair_sum(dtn * tn, lo) * (1.0 / HEAD_DIM)))
            st_ref[0:1, :] += _colsum(dnv * tn)

    return pl.pallas_call(body, name=name, grid=(l // tl, N_PAIRS), in_specs=[at(0), at(1), out, out, vec, vec],
                          out_specs=[out, out, stats, stats],
                          out_shape=[jax.ShapeDtypeStruct((l, D_SB), F32)] * 2 + [jax.ShapeDtypeStruct((8, PAIR), F32)] * 2,
                          )(proj, proj, dqn, dkn, qw2, kw2)


def _sb_masks():
    r = lax.broadcasted_iota(jnp.int32, (CHUNK, CHUNK), 0)
    c = lax.broadcasted_iota(jnp.int32, (CHUNK, CHUNK), 1)
    return r, c


def _stack2(mask):
    t = mask.astype(BF16)
    return jnp.concatenate([t, t], axis=0)


def _sb_fwd(q, k, v, shards, name):
    l = q.shape[0]
    nq = l // CHUNK
    kt = _pick(l, SB_KEYS, CHUNK)
    sub = kt // CHUNK
    pairs = SB_PAIRS_FWD
    heads = 2 * pairs
    steps = N_PAIRS // pairs
    na = len(shards)
    qblk = pl.BlockSpec((CHUNK, pairs * PAIR), lambda i, j: (j, i))
    full = pl.BlockSpec((l, pairs * PAIR), lambda i, j: (0, i))
    hbm = pl.BlockSpec(memory_space=pl.ANY)

    def body(q_ref, k_ref, v_ref, *rest):
        o_ref, tot_ref = rest[na:na + 2]
        exchange = _DirectExchange("gather", rest[:na], rest[na + 2:2 * na + 2], *rest[2 * na + 2:])
        qb = pl.program_id(1)

        @pl.when((pl.program_id(0) == 0) & (qb == 0))
        def _():
            exchange.start()

        r, c = _sb_masks()
        after2 = _stack2(r > c)
        halves = _half_masks()
        zero = jnp.zeros((CHUNK, PAIR), q_ref.dtype)
        lanes = [slice((a // 2) * PAIR, (a // 2 + 1) * PAIR) for a in range(heads)]
        qm = [jnp.where(halves[a % 2], q_ref[:, lanes[a]], zero) for a in range(heads)]
        last = (qb * CHUNK) // kt

        def tile(t, width, carries, accs, masked):
            rows = pl.ds(pl.multiple_of(t * width, width), width)
            if masked:
                key_minus_query = (lax.broadcasted_iota(jnp.int32, (CHUNK, width), 1)
                                   - lax.broadcasted_iota(jnp.int32, (CHUNK, width), 0))
                keep = key_minus_query < qb * CHUNK - t * width

            def logits(a):
                lg = _dot_nt(qm[a], k_ref[rows, lanes[a]])
                sp = _softplus_logits(lg)
                return lg - sp, (jnp.where(keep, -sp, 0.0) if masked else -sp)

            def sums(a, lr):
                offset, parts = carries[a], [None] * (width // CHUNK)
                for j in reversed(range(width // CHUNK)):
                    piece = lr[:, j * CHUNK:(j + 1) * CHUNK]
                    parts[j] = _split_dot(piece, after2) + offset
                    offset = offset + jnp.sum(piece, axis=1, keepdims=True)
                return jnp.concatenate(parts, axis=1), offset

            def output(a, ls, cs):
                w = jnp.exp(ls + cs)
                if masked:
                    w = jnp.where(keep, w, 0.0)
                return accs[a] + _dot(w.astype(MXU_DTYPE), v_ref[rows, lanes[a]])

            new_carries, new_accs = [None] * heads, [None] * heads
            ls, lr = logits(0)
            for a in range(heads):
                cs, new_carries[a] = sums(a, lr)
                if a + 1 < heads:
                    ls_next, lr = logits(a + 1)
                new_accs[a] = output(a, ls, cs)
                ls = ls_next
            return tuple(new_carries), tuple(new_accs)

        carries = tuple(jnp.zeros((CHUNK, 1), F32) for _ in range(heads))
        accs = tuple(jnp.zeros((CHUNK, PAIR), F32) for _ in range(heads))
        if sub % 2 == 0:
            half = kt // 2
            carries, accs = lax.cond((qb * CHUNK) % kt >= half,
                                     lambda cr, ac: tile(2 * last + 1, half, cr, ac, True), lambda cr, ac: (cr, ac), carries, accs)
            carries, accs = tile(2 * last, half, carries, accs, True)
        else:
            carries, accs = tile(last, kt, carries, accs, True)
        carries, accs = lax.fori_loop(1, last + 1, lambda i, st: tile(last - i, kt, st[0], st[1], False), (carries, accs))
        for p in range(pairs):
            o_ref[:, lanes[2 * p]] = jnp.where(halves[0], accs[2 * p], accs[2 * p + 1])
            tot_ref[:, lanes[2 * p]] = jnp.where(halves[0], carries[2 * p], carries[2 * p + 1])

        @pl.when((pl.program_id(0) == steps - 1) & (qb == nq - 1))
        def _():
            exchange.wait()

    return pl.pallas_call(
        body, name=name, grid=(steps, nq), in_specs=[qblk, full, full] + [hbm] * na,
        out_specs=[qblk, qblk] + [hbm] * na,
        out_shape=[jax.ShapeDtypeStruct((l, D_SB), F32), jax.ShapeDtypeStruct((l, D_SB), F32)]
        + [jax.ShapeDtypeStruct((N_DEV,) + s.shape, s.dtype) for s in shards],
        scratch_shapes=_DirectExchange.scratch(na),
    )(q, k, v, *shards)


def _sb_bwd(q, k, v, dycat, tot, blocks, name):
    l = q.shape[0]
    nq = l // CHUNK
    kt = _pick(l, SB_KEYS, CHUNK)
    sub = kt // CHUNK
    heads = 2 * SB_PAIRS
    steps = N_PAIRS // SB_PAIRS
    na = len(blocks)
    width_all = SB_PAIRS * PAIR
    qblk = pl.BlockSpec((CHUNK, width_all), lambda i, j: (j, i))
    doblk = pl.BlockSpec((CHUNK, width_all), lambda i, j: (j, D_SSD // width_all + i))
    full = pl.BlockSpec((l, width_all), lambda i, j: (0, i))
    resident = pl.BlockSpec((l, width_all), lambda i, j: (0, i), pipeline_mode=pl.Buffered(1))
    hbm = pl.BlockSpec(memory_space=pl.ANY)

    def body(q_ref, k_ref, v_ref, do_ref, tot_ref, *rest):
        dq_ref, dk_ref, dv_ref = rest[na:na + 3]
        exchange = _DirectExchange("scatter", rest[:na], rest[na + 3:2 * na + 3], *rest[2 * na + 3:])
        qb = pl.program_id(1)

        @pl.when((pl.program_id(0) == 0) & (qb == 0))
        def _():
            exchange.start()

        @pl.when(qb == 0)
        def _():
            dk_ref[...] = jnp.zeros_like(dk_ref)
            dv_ref[...] = jnp.zeros_like(dv_ref)

        r, c = _sb_masks()
        upto1 = (r <= c).astype(BF16)
        upto2 = jnp.concatenate([upto1, upto1], axis=0)
        halves = _half_masks()
        lanes = [slice((a // 2) * PAIR, (a // 2 + 1) * PAIR) for a in range(heads)]
        qm = [jnp.where(halves[a % 2], q_ref[:, lanes[a]], jnp.zeros((CHUNK, PAIR), q_ref.dtype)) for a in range(heads)]
        dom = [jnp.where(halves[a % 2], do_ref[:, lanes[a]], 0.0).astype(MXU_DTYPE) for a in range(heads)]
        total = [jnp.max(jnp.where(halves[a % 2], tot_ref[:, lanes[a]], -jnp.inf), axis=1, keepdims=True) for a in range(heads)]
        last = (qb * CHUNK) // kt

        def prefix(values, offset, sign, exact=True):
            parts = []
            for j in range(values.shape[1] // CHUNK):
                piece = values[:, j * CHUNK:(j + 1) * CHUNK]
                within = _split_dot(piece, upto2) if exact else _dot(piece.astype(BF16), upto1)
                parts.append(offset + within if sign > 0 else offset - within)
                step = jnp.sum(piece, axis=1, keepdims=True)
                offset = offset + step if sign > 0 else offset - step
            return jnp.concatenate(parts, axis=1), offset

        def tile(t, width, carry_p, carry_d, dq, masked):
            rows = pl.ds(pl.multiple_of(t * width, width), width)
            if masked:
                key_minus_query = (lax.broadcasted_iota(jnp.int32, (CHUNK, width), 1)
                                   - lax.broadcasted_iota(jnp.int32, (CHUNK, width), 0))
                keep = key_minus_query < qb * CHUNK - t * width

            def log_terms(a):
                lg = _dot_nt(qm[a], k_ref[rows, lanes[a]])
                sp = _softplus_logits(lg)
                return lg - sp, (jnp.where(keep, -sp, 0.0) if masked else -sp)

            def weights(a, ls, lr):
                remaining, p_next = prefix(lr, carry_p[a], -1)
                w = jnp.exp(ls + remaining)
                if masked:
                    w = jnp.where(keep, w, 0.0)
                return w, p_next

            def weight_grads(a, w):
                return _dot_nt(dom[a], v_ref[rows, lanes[a]]) * w

            def gradients(a, ls, w, da):
                d_incl, d_next = prefix(da, carry_d[a], 1, exact=False)
                sig = jnp.exp(ls)
                dl = da - sig * d_incl
                if masked:
                    dl = jnp.where(keep, dl, 0.0)
                dl_b = dl.astype(MXU_DTYPE)
                return (d_next, dq[a] + _dot(dl_b, k_ref[rows, lanes[a]]),
                        _dot_tn(w.astype(MXU_DTYPE), dom[a]), _dot_tn(dl_b, qm[a]))

            new_p, new_d, new_dq = [None] * heads, [None] * heads, [None] * heads
            dv_upd, dk_upd = [None] * heads, [None] * heads
            ls, lr = log_terms(0)
            w, new_p[0] = weights(0, ls, lr)
            for a in range(heads):
                if a + 1 < heads:
                    ls_next, lr_next = log_terms(a + 1)
                da = weight_grads(a, w)
                if a + 1 < heads:
                    w_next, new_p[a + 1] = weights(a + 1, ls_next, lr_next)
                new_d[a], new_dq[a], dv_upd[a], dk_upd[a] = gradients(a, ls, w, da)
                if a % 2 == 1:
                    dv_ref[rows, lanes[a]] += dv_upd[a - 1] + dv_upd[a]
                    dk_ref[rows, lanes[a]] += dk_upd[a - 1] + dk_upd[a]
                if a + 1 < heads:
                    ls, w = ls_next, w_next
            return tuple(new_p), tuple(new_d), tuple(new_dq)

        zeros = tuple(jnp.zeros((CHUNK, 1), F32) for _ in range(heads))
        dq0 = tuple(jnp.zeros((CHUNK, PAIR), F32) for _ in range(heads))
        state = lax.fori_loop(0, last, lambda t, st: tile(t, kt, st[0], st[1], st[2], False), (tuple(total), zeros, dq0))
        if sub % 2 == 0:
            half = kt // 2
            state = tile(2 * last, half, *state, True)
            state = lax.cond((qb * CHUNK) % kt >= half,
                             lambda cp, cd, dq: tile(2 * last + 1, half, cp, cd, dq, True), lambda cp, cd, dq: (cp, cd, dq), *state)
        else:
            state = tile(last, kt, *state, True)
        for p in range(SB_PAIRS):
            dq_ref[:, lanes[2 * p]] = jnp.where(halves[0], state[2][2 * p], state[2][2 * p + 1])

        @pl.when((pl.program_id(0) == steps - 1) & (qb == nq - 1))
        def _():
            exchange.wait()

    return pl.pallas_call(
        body, name=name, grid=(steps, nq),
        in_specs=[qblk, full, full, doblk, qblk] + [hbm] * na,
        out_specs=[qblk, resident, resident] + [hbm] * na,
        out_shape=[jax.ShapeDtypeStruct((l, D_SB), F32)] * 3 + [jax.ShapeDtypeStruct(b.shape, b.dtype) for b in blocks],
        scratch_shapes=_DirectExchange.scratch(na),
    )(q, k, v, dycat, tot, *blocks)


def _ada_bwd(c_all, dmod_shard, name):
    def body(c_ref, d_ref, o_ref):
        cv = c_ref[...]
        o_ref[...] = _dot_tn(cv * _sigmoid(cv), d_ref[...], HIGHEST)

    return pl.pallas_call(body, name=name, out_shape=jax.ShapeDtypeStruct((c_all.shape[1], dmod_shard.shape[1]), F32))(c_all, dmod_shard)


def _sum_small(parts, name):
    def body(p_ref, o_ref):
        acc = p_ref[0]
        for d in range(1, N_DEV):
            acc = acc + p_ref[d]
        o_ref[...] = acc

    return pl.pallas_call(body, name=name, out_shape=jax.ShapeDtypeStruct(parts.shape[1:], F32))(parts)


def _adamw(w, g, m, v, name):
    rows, cols = w.shape
    tr = _pick(rows, 256, 8)
    spec = pl.BlockSpec((tr, cols), lambda i: (i, 0))
    bc1 = 1.0 - ADAM_B1 ** ADAM_STEP
    bc2 = 1.0 - ADAM_B2 ** ADAM_STEP

    def body(w_ref, g_ref, m_ref, v_ref, d_ref, nm_ref, nv_ref):
        gv = g_ref[...]
        nm = ADAM_B1 * m_ref[...] + (1.0 - ADAM_B1) * gv
        nv = ADAM_B2 * v_ref[...] + (1.0 - ADAM_B2) * (gv * gv)
        nm_ref[...] = nm
        nv_ref[...] = nv
        d_ref[...] = -ADAM_LR * ((nm / bc1) / (jnp.sqrt(nv / bc2) + ADAM_EPS) + ADAM_WD * w_ref[...])

    return pl.pallas_call(body, name=name, grid=(rows // tr,), in_specs=[spec] * 4, out_specs=[spec] * 3,
                          out_shape=[jax.ShapeDtypeStruct((rows, cols), F32)] * 3)(w, g, m, v)


def _pad_w_in_t(w_in_t):
    lo = D_SSD + D_CONV
    return jnp.concatenate([w_in_t[D_SSD:lo + N_HEADS], jnp.zeros((DT_PAD - N_HEADS, w_in_t.shape[1]), w_in_t.dtype),
                            w_in_t[:D_SSD], w_in_t[lo + N_HEADS:]], axis=0)


def _local_step(x, target, mod, norm1_w, w_in_tp, conv_w, conv_b, dt_bias, a_log, d_skip, ssd_norm_w, q_norm_w, k_norm_w,
                norm2_w, later_shards):
    sh1, sc1, g1, sh2, sc2, g2 = [mod[:, i * D_MODEL:(i + 1) * D_MODEL] for i in range(N_MOD)]
    qw2, kw2 = jnp.tile(q_norm_w, (1, 2)), jnp.tile(k_norm_w, (1, 2))
    dskip_x = jnp.repeat(d_skip, HEAD_DIM, axis=1)
    dtb_x, al_x = jnp.repeat(dt_bias, HEAD_DIM, axis=1), jnp.repeat(a_log, HEAD_DIM, axis=1)
    head_ids = jnp.arange(HEAD_TILE, dtype=jnp.int32)[:, None]
    spread_x = (head_ids == jnp.arange(D_SSD, dtype=jnp.int32)[None, :] // HEAD_DIM).astype(BF16)

    h1 = _rms_mod(x, norm1_w, sc1, sh1, "rms_mod1")
    proj = _matmul_nt(h1, w_in_tp, "in_proj")

    xc = _conv_silu(proj, conv_w, conv_b, "conv_silu")
    ycore, states = _ssd_fwd(xc, proj, spread_x, dtb_x, al_x, "ssd_fwd")
    y_ssd = _ssd_gate_norm(ycore, xc, proj, dskip_x, ssd_norm_w, "ssd_gate_norm")

    qn, kn, vb = _qkv_prep(proj, qw2, kw2, "qkv_prep")
    o_sb, tot, *gathered = _sb_fwd(qn, kn, vb, later_shards, "sb_fwd")
    w_out, w_gate_t, w_up_t, w_down = [g.reshape(N_DEV * g.shape[1], D_MODEL) for g in gathered]
    ycat = jnp.concatenate([y_ssd, o_sb.astype(MXU_DTYPE)], axis=1)

    mix = _matmul(ycat, w_out, "out_proj")
    x1, h2 = _residual_rms_mod(x, mix, g1, norm2_w, sc2, sh2, "residual_rms_mod2")
    gate, up, act = _ffn_in(h2, w_gate_t, w_up_t, "ffn_in")
    ffn = _matmul(act, w_down, "ffn_down")
    dy, dffn, st_g2, loss_blk = _loss_head(x1, ffn, g2, target, "loss_head")

    g_down = _matmul_tn(act, dffn, "ffn_down_dw", MXU_DTYPE)
    dgate, dup = _ffn_down_bwd(dffn, w_down, gate, up, "ffn_down_dx")
    g_gate_t = _matmul_tn(dgate, h2, "ffn_gate_dw", MXU_DTYPE)
    g_up_t = _matmul_tn(dup, h2, "ffn_up_dw", MXU_DTYPE)
    dh2 = _matmul_sum2(dgate, w_gate_t, dup, w_up_t, "ffn_dh")
    dx1, dmix, st_n2 = _norm_bwd(x1, dh2, dy, norm2_w, sc2, "norm2_bwd", gated=(mix, g1))

    g_out = _matmul_tn(ycat, dmix, "out_proj_dw", MXU_DTYPE)
    dycat = _matmul_nt(dmix, w_out, "out_proj_dx")

    partials = [g.reshape(N_DEV, g.shape[0] // N_DEV, D_MODEL).astype(BF16) for g in (g_out, g_gate_t, g_up_t, g_down)]
    dqn, dkn, dv, *slots = _sb_bwd(qn, kn, vb, dycat, tot, partials, "sb_bwd")
    dq, dk, st_q, st_k = _qk_norm_bwd(proj, dqn, dkn, qw2, kw2, "qk_norm_bwd")

    dycore, dxs_skip, dz, st_gn = _ssd_gate_norm_bwd(dycat, ycore, xc, proj, dskip_x, ssd_norm_w, "ssd_gate_norm_bwd")
    dxc, ddt_tile, st_ssd, st_heads = _ssd_bwd(xc, proj, spread_x, dtb_x, al_x, dycore, states, "ssd_bwd")
    dxc = jnp.concatenate([dxc[:, :D_SSD] + dxs_skip, dxc[:, D_SSD:]], axis=1)
    dpre, st_conv = _conv_silu_bwd_pre(proj, dxc, conv_w, conv_b, "conv_silu_bwd")
    dxbc = _conv_bwd_input(dpre, conv_w, "conv_bwd_input")

    pieces = {"z": (dz, O_Z), "xbc": (dxbc, O_XBC), "dt": (ddt_tile, O_DT), "q": (dq, O_Q), "k": (dk, O_Q + D_SB), "v": (dv, O_Q + 2 * D_SB)}
    g_rows = {n: _matmul_tn(p, h1, "in_proj_dw_" + n, MXU_DTYPE) for n, (p, _) in pieces.items()}
    g_rows["dt"] = g_rows["dt"][:N_HEADS]
    g_in_t = jnp.concatenate(list(g_rows.values()), axis=0)
    g_in_blocks = g_in_t.reshape(N_DEV, g_in_t.shape[0] // N_DEV, D_MODEL).astype(BF16)
    dh1, g_in_slots = _matmul_pieces_beside_scatter([p for p, _ in pieces.values()], [o for _, o in pieces.values()], w_in_tp,
                                                    [g_in_blocks], "in_proj_dx")
    slots = [g_in_slots] + list(slots)
    grad_x, st_n1 = _norm_bwd(x, dh1, dx1, norm1_w, sc1, "norm1_bwd")

    pad = jnp.zeros((1, SM_SSD_NORM - SM_D_SKIP - N_HEADS), F32)
    small = jnp.concatenate(
        [st_n1[1:2], st_n1[0:1], st_n2[3:4], st_n2[1:2], st_n2[0:1], st_g2[0:1],
         st_n1[2:3], st_conv[4:5], st_conv[0:4].reshape(1, CONV_WIDTH * D_CONV),
         st_ssd[1:2, ::HEAD_DIM], st_ssd[0:1, ::HEAD_DIM], st_heads[2:3, ::HEAD_DIM], pad,
         st_gn[0:1], st_q[0:1, :HEAD_DIM] + st_q[0:1, HEAD_DIM:], st_k[0:1, :HEAD_DIM] + st_k[0:1, HEAD_DIM:], st_n2[2:3]], axis=1)
    return loss_blk, grad_x, slots, small


def kernel(x, c, w_ada, b_ada, norm1_w, w_in, conv_w, conv_b, dt_bias, a_log, d_skip, ssd_norm_w, q_norm_w, k_norm_w, w_out, norm2_w, w_gate, w_up, w_down, loss_target, m_w_ada, m_b_ada, m_norm1_w, m_w_in, m_conv_w, m_conv_b, m_dt_bias, m_a_log, m_d_skip, m_ssd_norm_w, m_q_norm_w, m_k_norm_w, m_w_out, m_norm2_w, m_w_gate, m_w_up, m_w_down, v_w_ada, v_b_ada, v_norm1_w, v_w_in, v_conv_w, v_conv_b, v_dt_bias, v_a_log, v_d_skip, v_ssd_norm_w, v_q_norm_w, v_k_norm_w, v_w_out, v_norm2_w, v_w_gate, v_w_up, v_w_down):
    me = 4 * lax.axis_index("x") + 2 * lax.axis_index("y") + lax.axis_index("c")
    conv_cols = D_CONV // N_DEV
    ada_cols = N_MOD * D_MODEL // N_DEV

    cond_conv = jnp.concatenate([c, conv_w[0].reshape(1, CONV_WIDTH * conv_cols)], axis=1)
    first, mods, w_in_t = _prologue(cond_conv, w_ada[0], w_in[0].T.astype(BF16), "prologue")
    w_in_tp = _pad_w_in_t(w_in_t.reshape(D_IN_PROJ, D_MODEL))
    later_shards = [s.astype(BF16) for s in (w_out[0], w_gate[0].T, w_up[0].T, w_down[0])]
    c_all = first[:, 0, :D_MODEL]
    conv_w_f = first[:, 0, D_MODEL:].reshape(N_DEV, CONV_WIDTH, conv_cols).transpose(1, 0, 2).reshape(CONV_WIDTH, D_CONV)
    mod = lax.dynamic_index_in_dim(mods, me, axis=1, keepdims=False)
    mod = mod.reshape(1, N_MOD * D_MODEL) + b_ada

    loss_blk, grad_x, slots, small = _local_step(
        x[0], loss_target[0], mod, norm1_w, w_in_tp, conv_w_f, conv_b, dt_bias, a_log, d_skip, ssd_norm_w, q_norm_w, k_norm_w,
        norm2_w, later_shards)
    loss = lax.psum(loss_blk[0, 0], ("x", "y", "c"))

    g_in_t, g_out, g_gate_t, g_up_t, g_down = [_sum_slots(s, "sum_grads_" + n) for s, n in zip(slots, ("in", "out", "gate", "up", "down"))]
    g_in, g_gate, g_up = g_in_t.T, g_gate_t.T, g_up_t.T

    parts = _all_gather_small(small, "gather_small")
    gsum = _sum_small(parts, "sum_small")
    dmod_shard = lax.dynamic_slice_in_dim(parts[:, 0, :N_MOD * D_MODEL], me * ada_cols, ada_cols, axis=1)
    g_ada = _ada_bwd(c_all, dmod_shard, "ada_bwd")
    g_conv_w = lax.dynamic_slice_in_dim(gsum[:, SM_CONV_W:SM_DT_BIAS].reshape(CONV_WIDTH, D_CONV), me * conv_cols, conv_cols, axis=1)

    def pack_small(b_ada_, norm1_, conv_b_, dt_bias_, a_log_, d_skip_, ssd_norm_, q_norm_, k_norm_, norm2_):
        return jnp.concatenate(
            [b_ada_, norm1_, conv_b_, jnp.zeros((1, CONV_WIDTH * D_CONV), F32), dt_bias_, a_log_, d_skip_,
             jnp.zeros((1, SM_SSD_NORM - SM_D_SKIP - N_HEADS), F32), ssd_norm_, q_norm_, k_norm_, norm2_], axis=1)

    def unpack_small(p):
        return {
            "b_ada": p[:, SM_B_ADA:SM_NORM1], "norm1_w": p[:, SM_NORM1:SM_CONV_B], "conv_b": p[:, SM_CONV_B:SM_CONV_W],
            "dt_bias": p[:, SM_DT_BIAS:SM_A_LOG], "a_log": p[:, SM_A_LOG:SM_D_SKIP], "d_skip": p[:, SM_D_SKIP:SM_D_SKIP + N_HEADS],
            "ssd_norm_w": p[:, SM_SSD_NORM:SM_Q_NORM], "q_norm_w": p[:, SM_Q_NORM:SM_K_NORM], "k_norm_w": p[:, SM_K_NORM:SM_NORM2],
            "norm2_w": p[:, SM_NORM2:SM_TOTAL]}

    w_small = pack_small(b_ada, norm1_w, conv_b, dt_bias, a_log, d_skip, ssd_norm_w, q_norm_w, k_norm_w, norm2_w)
    m_small = pack_small(m_b_ada, m_norm1_w, m_conv_b, m_dt_bias, m_a_log, m_d_skip, m_ssd_norm_w, m_q_norm_w, m_k_norm_w, m_norm2_w)
    v_small = pack_small(v_b_ada, v_norm1_w, v_conv_b, v_dt_bias, v_a_log, v_d_skip, v_ssd_norm_w, v_q_norm_w, v_k_norm_w, v_norm2_w)
    small_out = [unpack_small(t) for t in (gsum,) + tuple(_adamw(w_small, gsum, m_small, v_small, "adamw_small"))]

    sharded = {
        "w_ada": (w_ada[0], g_ada, m_w_ada[0], v_w_ada[0]),
        "w_in": (w_in[0], g_in, m_w_in[0], v_w_in[0]),
        "conv_w": (conv_w[0], g_conv_w, m_conv_w[0], v_conv_w[0]),
        "w_out": (w_out[0], g_out, m_w_out[0], v_w_out[0]),
        "w_gate": (w_gate[0], g_gate, m_w_gate[0], v_w_gate[0]),
        "w_up": (w_up[0], g_up, m_w_up[0], v_w_up[0]),
        "w_down": (w_down[0], g_down, m_w_down[0], v_w_down[0]),
    }
    sharded_out = {n: (t[1],) + tuple(_adamw(*t, "adamw_" + n)) for n, t in sharded.items()}

    names = ["w_ada", "b_ada", "norm1_w", "w_in", "conv_w", "conv_b", "dt_bias", "a_log", "d_skip", "ssd_norm_w", "q_norm_w",
             "k_norm_w", "w_out", "norm2_w", "w_gate", "w_up", "w_down"]
    outs = [loss, grad_x[None]]
    for kind in range(4):
        for n in names:
            outs.append(sharded_out[n][kind][None] if n in sharded_out else small_out[kind][n])
    return tuple(outs)
```
